```python
import math
import jax, jax.numpy as jnp
from jax import lax
import numpy as np

D_MODEL = 1024
BATCH = 8
SEQ = 8192
DEPTH = 1

PLE_DIM = 256
W_CONV = D_MODEL
CONV_GROUPS = 8
CONV_K = 31
HG_HEAD_DIM = 128
W_HGRN = D_MODEL
HG_HEADS = W_HGRN // HG_HEAD_DIM
W_MIX = W_CONV + W_HGRN
CHUNK = 64
EPS = 1e-6
W_IN_COLS = 3 * W_CONV + 4 * W_HGRN

kernel_name = "hymba_conformer_hgrn2_hybrid"


def rms_norm(x, g):
    xf = x.astype(jnp.float32)
    y = xf * lax.rsqrt(jnp.mean(xf * xf, axis=-1, keepdims=True) + EPS)
    return (y * g.astype(jnp.float32)).astype(x.dtype)


def group_layer_norm(x, g, b, n_groups):
    shp = x.shape
    xf = x.astype(jnp.float32).reshape(shp[:-1] + (n_groups, shp[-1] // n_groups))
    mu = jnp.mean(xf, axis=-1, keepdims=True)
    var = jnp.mean(jnp.square(xf - mu), axis=-1, keepdims=True)
    y = ((xf - mu) * lax.rsqrt(var + EPS)).reshape(shp)
    return (y * g.astype(jnp.float32) + b.astype(jnp.float32)).astype(x.dtype)


def conformer_conv_branch(z_val, z_glu, z_gate, conv_w, conv_b, cn_g, cn_b, w_pw2, b_pw2):
    v = z_val * jax.nn.sigmoid(z_glu)
    y = lax.conv_general_dilated(
        v, conv_w[:, None, :].astype(v.dtype),
        window_strides=(1,), padding=[(CONV_K - 1, 0)],
        dimension_numbers=("NWC", "WIO", "NWC"),
        feature_group_count=W_CONV) + conv_b.astype(v.dtype)
    y = jax.nn.silu(group_layer_norm(y, cn_g, cn_b, CONV_GROUPS))
    y = y @ w_pw2.astype(y.dtype) + b_pw2.astype(y.dtype)
    return y * jax.nn.silu(z_gate)


def _gla_chunk_step(S, inp):
    q, k, v, lf = inp
    b = jnp.cumsum(lf, axis=2)
    o_inter = jnp.einsum("bhck,bhkv->bhcv", q * jnp.exp(b), S)
    t_idx = jnp.arange(CHUNK)
    causal = (t_idx[:, None] >= t_idx[None, :])[None, None, :, :, None]
    diff = b[:, :, :, None, :] - b[:, :, None, :, :]
    decay = jnp.exp(jnp.where(causal, diff, -jnp.inf))
    A = jnp.einsum("bhtsk,bhsk->bhts", q[:, :, :, None, :] * decay, k)
    o_intra = jnp.einsum("bhts,bhsv->bhtv", A, v)
    b_last = b[:, :, -1, :]
    k_dec = k * jnp.exp(b_last[:, :, None, :] - b)
    S_new = jnp.exp(b_last)[..., None] * S + jnp.einsum("bhsk,bhsv->bhkv", k_dec, v)
    return S_new, o_inter + o_intra


def hgrn2_branch(zq, zf, zi, zg, lb, onorm_g):
    B, T, _ = zq.shape
    n_chunks = T // CHUNK
    lbf = lb.astype(jnp.float32)
    zf32 = zf.astype(jnp.float32)
    f = lbf + (1.0 - lbf) * jax.nn.sigmoid(zf32)
    log_f = jnp.log(f)
    k = (1.0 - lbf) * jax.nn.sigmoid(-zf32)
    q = jax.nn.silu(zq.astype(jnp.float32))
    v = zi.astype(jnp.float32)

    def to_chunks(t):
        return t.reshape(B, n_chunks, CHUNK, HG_HEADS, HG_HEAD_DIM).transpose(1, 0, 3, 2, 4)

    S0 = jnp.zeros((B, HG_HEADS, HG_HEAD_DIM, HG_HEAD_DIM), jnp.float32)
    _, o = lax.scan(_gla_chunk_step, S0,
                    (to_chunks(q), to_chunks(k), to_chunks(v), to_chunks(log_f)))
    o = o.transpose(1, 0, 3, 2, 4).reshape(B, T, HG_HEADS, HG_HEAD_DIM)
    o = rms_norm(o, onorm_g.reshape(HG_HEADS, HG_HEAD_DIM))
    o = o.reshape(B, T, W_HGRN).astype(zq.dtype)
    return o * jax.nn.silu(zg)


def _fwd_setup_inputs(seed: int = 0) -> dict:
    key = jax.random.key(seed)
    ks = jax.random.split(key, 20)
    f32 = jnp.float32
    nrm = lambda k, s, sc: (jax.random.normal(k, s, f32) * sc).astype(f32)
    return {
        "x": jax.random.normal(ks[0], (BATCH, SEQ, D_MODEL), f32),
        "p": jax.random.normal(ks[1], (DEPTH, BATCH, SEQ, PLE_DIM), f32),
        "ln_g": 1.0 + nrm(ks[2], (DEPTH, D_MODEL), 0.02),
        "w_in": nrm(ks[3], (DEPTH, D_MODEL, W_IN_COLS), D_MODEL ** -0.5),
        "conv_w": nrm(ks[4], (DEPTH, CONV_K, W_CONV), CONV_K ** -0.5),
        "conv_b": nrm(ks[5], (DEPTH, W_CONV), 0.01),
        "cnorm_g": 1.0 + nrm(ks[6], (DEPTH, W_CONV), 0.02),
        "cnorm_b": nrm(ks[7], (DEPTH, W_CONV), 0.01),
        "w_pw2": nrm(ks[8], (DEPTH, W_CONV, W_CONV), W_CONV ** -0.5),
        "b_pw2": nrm(ks[9], (DEPTH, W_CONV), 0.01),
        "lb_logits": nrm(ks[10], (DEPTH + 1, W_HGRN), 0.1),
        "onorm_g": 1.0 + nrm(ks[11], (DEPTH, W_HGRN), 0.02),
        "w_out": nrm(ks[12], (DEPTH, W_MIX, D_MODEL), W_MIX ** -0.5),
        "pe_norm_g": 1.0 + nrm(ks[13], (DEPTH, D_MODEL), 0.02),
        "w_pg": nrm(ks[14], (DEPTH, D_MODEL, D_MODEL), D_MODEL ** -0.5),
        "w_pp": nrm(ks[15], (DEPTH, PLE_DIM, D_MODEL), PLE_DIM ** -0.5),
        "final_g": 1.0 + nrm(ks[16], (D_MODEL,), 0.02),
    }


def _fwd_reference(x, p, ln_g, w_in, conv_w, conv_b, cnorm_g, cnorm_b, w_pw2, b_pw2,
              lb_logits, onorm_g, w_out, pe_norm_g, w_pg, w_pp, final_g):
    lbs = jnp.cumsum(jax.nn.softmax(lb_logits.astype(jnp.float32), axis=0), axis=0)
    h = x
    split_pts = [W_CONV, 2 * W_CONV, 3 * W_CONV,
                 3 * W_CONV + W_HGRN, 3 * W_CONV + 2 * W_HGRN, 3 * W_CONV + 3 * W_HGRN]
    for i in range(DEPTH):
        u = rms_norm(h, ln_g[i])
        z = u @ w_in[i].astype(u.dtype)
        c_val, c_glu, c_gate, hq, hf, hi, hg = jnp.split(z, split_pts, axis=-1)
        y_conv = conformer_conv_branch(c_val, c_glu, c_gate, conv_w[i], conv_b[i],
                                       cnorm_g[i], cnorm_b[i], w_pw2[i], b_pw2[i])
        y_hgrn = hgrn2_branch(hq, hf, hi, hg, lbs[i], onorm_g[i])
        y = jnp.concatenate([y_conv, y_hgrn.astype(y_conv.dtype)], axis=-1)
        h = h + y @ w_out[i].astype(y.dtype)
        pe = p[i] @ w_pp[i].astype(p.dtype)
        gate = jax.nn.sigmoid(rms_norm(h, pe_norm_g[i]) @ w_pg[i].astype(h.dtype))
        h = h + gate * pe.astype(h.dtype)
    return rms_norm(h, final_g)


import jax as _jax
import jax.numpy as _jnp

TWIN_FORMAT = 'train_step'
FWD_PARAMS = ['x', 'p', 'ln_g', 'w_in', 'conv_w', 'conv_b', 'cnorm_g', 'cnorm_b', 'w_pw2', 'b_pw2', 'lb_logits', 'onorm_g', 'w_out', 'pe_norm_g', 'w_pg', 'w_pp', 'final_g']
TWIN_WEIGHTS = ['ln_g', 'w_in', 'conv_w', 'conv_b', 'cnorm_g', 'cnorm_b', 'w_pw2', 'b_pw2', 'lb_logits', 'onorm_g', 'w_out', 'pe_norm_g', 'w_pg', 'w_pp', 'final_g']
TWIN_DIFF_INPUT = 'x'
TWIN_INPUTS = ['x', 'p', 'ln_g', 'w_in', 'conv_w', 'conv_b', 'cnorm_g', 'cnorm_b', 'w_pw2', 'b_pw2', 'lb_logits', 'onorm_g', 'w_out', 'pe_norm_g', 'w_pg', 'w_pp', 'final_g', 'loss_target', 'm_ln_g', 'm_w_in', 'm_conv_w', 'm_conv_b', 'm_cnorm_g', 'm_cnorm_b', 'm_w_pw2', 'm_b_pw2', 'm_lb_logits', 'm_onorm_g', 'm_w_out', 'm_pe_norm_g', 'm_w_pg', 'm_w_pp', 'm_final_g', 'v_ln_g', 'v_w_in', 'v_conv_w', 'v_conv_b', 'v_cnorm_g', 'v_cnorm_b', 'v_w_pw2', 'v_b_pw2', 'v_lb_logits', 'v_onorm_g', 'v_w_out', 'v_pe_norm_g', 'v_w_pg', 'v_w_pp', 'v_final_g']
TWIN_OUTPUTS = ['loss', 'grad_x', 'grad_ln_g', 'grad_w_in', 'grad_conv_w', 'grad_conv_b', 'grad_cnorm_g', 'grad_cnorm_b', 'grad_w_pw2', 'grad_b_pw2', 'grad_lb_logits', 'grad_onorm_g', 'grad_w_out', 'grad_pe_norm_g', 'grad_w_pg', 'grad_w_pp', 'grad_final_g', 'delta_ln_g', 'delta_w_in', 'delta_conv_w', 'delta_conv_b', 'delta_cnorm_g', 'delta_cnorm_b', 'delta_w_pw2', 'delta_b_pw2', 'delta_lb_logits', 'delta_onorm_g', 'delta_w_out', 'delta_pe_norm_g', 'delta_w_pg', 'delta_w_pp', 'delta_final_g', 'new_m_ln_g', 'new_m_w_in', 'new_m_conv_w', 'new_m_conv_b', 'new_m_cnorm_g', 'new_m_cnorm_b', 'new_m_w_pw2', 'new_m_b_pw2', 'new_m_lb_logits', 'new_m_onorm_g', 'new_m_w_out', 'new_m_pe_norm_g', 'new_m_w_pg', 'new_m_w_pp', 'new_m_final_g', 'new_v_ln_g', 'new_v_w_in', 'new_v_conv_w', 'new_v_conv_b', 'new_v_cnorm_g', 'new_v_cnorm_b', 'new_v_w_pw2', 'new_v_b_pw2', 'new_v_lb_logits', 'new_v_onorm_g', 'new_v_w_out', 'new_v_pe_norm_g', 'new_v_w_pg', 'new_v_w_pp', 'new_v_final_g']
TWIN_LEAF_KINDS = {'loss': 'loss', 'grad_x': 'grad_x', 'grad_ln_g': 'grad_w', 'grad_w_in': 'grad_w', 'grad_conv_w': 'grad_w', 'grad_conv_b': 'grad_w', 'grad_cnorm_g': 'grad_w', 'grad_cnorm_b': 'grad_w', 'grad_w_pw2': 'grad_w', 'grad_b_pw2': 'grad_w', 'grad_lb_logits': 'grad_w', 'grad_onorm_g': 'grad_w', 'grad_w_out': 'grad_w', 'grad_pe_norm_g': 'grad_w', 'grad_w_pg': 'grad_w', 'grad_w_pp': 'grad_w', 'grad_final_g': 'grad_w', 'delta_ln_g': 'delta_w', 'delta_w_in': 'delta_w', 'delta_conv_w': 'delta_w', 'delta_conv_b': 'delta_w', 'delta_cnorm_g': 'delta_w', 'delta_cnorm_b': 'delta_w', 'delta_w_pw2': 'delta_w', 'delta_b_pw2': 'delta_w', 'delta_lb_logits': 'delta_w', 'delta_onorm_g': 'delta_w', 'delta_w_out': 'delta_w', 'delta_pe_norm_g': 'delta_w', 'delta_w_pg': 'delta_w', 'delta_w_pp': 'delta_w', 'delta_final_g': 'delta_w', 'new_m_ln_g': 'new_m', 'new_m_w_in': 'new_m', 'new_m_conv_w': 'new_m', 'new_m_conv_b': 'new_m', 'new_m_cnorm_g': 'new_m', 'new_m_cnorm_b': 'new_m', 'new_m_w_pw2': 'new_m', 'new_m_b_pw2': 'new_m', 'new_m_lb_logits': 'new_m', 'new_m_onorm_g': 'new_m', 'new_m_w_out': 'new_m', 'new_m_pe_norm_g': 'new_m', 'new_m_w_pg': 'new_m', 'new_m_w_pp': 'new_m', 'new_m_final_g': 'new_m', 'new_v_ln_g': 'new_v', 'new_v_w_in': 'new_v', 'new_v_conv_w': 'new_v', 'new_v_conv_b': 'new_v', 'new_v_cnorm_g': 'new_v', 'new_v_cnorm_b': 'new_v', 'new_v_w_pw2': 'new_v', 'new_v_b_pw2': 'new_v', 'new_v_lb_logits': 'new_v', 'new_v_onorm_g': 'new_v', 'new_v_w_out': 'new_v', 'new_v_pe_norm_g': 'new_v', 'new_v_w_pg': 'new_v', 'new_v_w_pp': 'new_v', 'new_v_final_g': 'new_v'}


def _forward(args):
    return _fwd_reference(*[args[k] for k in FWD_PARAMS])


def _output_shape():
    out = _jax.eval_shape(lambda: _forward(_fwd_setup_inputs(0)))
    return out.shape, out.dtype

N_MICROBATCH = 1
ADAM_LR = 0.001
ADAM_B1 = 0.9
ADAM_B2 = 0.999
ADAM_EPS = 1e-08
ADAM_WD = 0.01
ADAM_STEP = 10
PER_EXAMPLE_BATCH_AXIS = {'x': 0, 'p': 1, 'loss_target': 0}
SHARED_INPUTS = []
_WEIGHT_DTYPES = {'ln_g': _jnp.float32, 'w_in': _jnp.float32, 'conv_w': _jnp.float32, 'conv_b': _jnp.float32, 'cnorm_g': _jnp.float32, 'cnorm_b': _jnp.float32, 'w_pw2': _jnp.float32, 'b_pw2': _jnp.float32, 'lb_logits': _jnp.float32, 'onorm_g': _jnp.float32, 'w_out': _jnp.float32, 'pe_norm_g': _jnp.float32, 'w_pg': _jnp.float32, 'w_pp': _jnp.float32, 'final_g': _jnp.float32}
MOMENT_SCALE = {'ln_g': 1.503625e-01, 'w_in': 5.623827e-02, 'conv_w': 5.381041e-02, 'conv_b': 1.149156e-01, 'cnorm_g': 6.875157e-02, 'cnorm_b': 5.423694e-02, 'w_pw2': 5.251118e-02, 'b_pw2': 8.965300e-02, 'lb_logits': 8.389353e-03, 'onorm_g': 9.507112e-02, 'w_out': 1.012777e-01, 'pe_norm_g': 4.388127e-02, 'w_pg': 4.289369e-02, 'w_pp': 1.099468e-01, 'final_g': 6.396483e+01}


def _to_microbatches(a, axis):
    t = _jnp.moveaxis(a, axis, 0)
    t = t.reshape((N_MICROBATCH, t.shape[0] // N_MICROBATCH) + t.shape[1:])
    return _jnp.moveaxis(t, 1, axis + 1)


def setup_inputs(seed: int = 0) -> dict:
    inp = _fwd_setup_inputs(seed)
    key = _jax.random.fold_in(_jax.random.key(seed), 7919)
    shape, _ = _output_shape()
    out = dict(inp)
    out["loss_target"] = _jax.random.normal(_jax.random.fold_in(key, 0), shape, _jnp.float32)
    for i, name in enumerate(TWIN_WEIGHTS):
        w = inp[name].astype(_jnp.float32)
        if MOMENT_SCALE is None:
            s = _jnp.sqrt(_jnp.mean(_jnp.square(w)) + 1e-30)
        else:
            s = MOMENT_SCALE[name]
        km, kv = _jax.random.split(_jax.random.fold_in(key, i + 1))
        out[name] = w
        out["m_" + name] = s * _jax.random.normal(km, w.shape, _jnp.float32)
        out["v_" + name] = (s * s) * _jax.random.uniform(kv, w.shape, _jnp.float32, 0.5, 1.5)
    if N_MICROBATCH > 1:
        for name, axis in PER_EXAMPLE_BATCH_AXIS.items():
            out[name] = _to_microbatches(out[name], axis)
    return {'x': out['x'], 'p': out['p'], 'ln_g': out['ln_g'], 'w_in': out['w_in'], 'conv_w': out['conv_w'], 'conv_b': out['conv_b'], 'cnorm_g': out['cnorm_g'], 'cnorm_b': out['cnorm_b'], 'w_pw2': out['w_pw2'], 'b_pw2': out['b_pw2'], 'lb_logits': out['lb_logits'], 'onorm_g': out['onorm_g'], 'w_out': out['w_out'], 'pe_norm_g': out['pe_norm_g'], 'w_pg': out['w_pg'], 'w_pp': out['w_pp'], 'final_g': out['final_g'], 'loss_target': out['loss_target'], 'm_ln_g': out['m_ln_g'], 'm_w_in': out['m_w_in'], 'm_conv_w': out['m_conv_w'], 'm_conv_b': out['m_conv_b'], 'm_cnorm_g': out['m_cnorm_g'], 'm_cnorm_b': out['m_cnorm_b'], 'm_w_pw2': out['m_w_pw2'], 'm_b_pw2': out['m_b_pw2'], 'm_lb_logits': out['m_lb_logits'], 'm_onorm_g': out['m_onorm_g'], 'm_w_out': out['m_w_out'], 'm_pe_norm_g': out['m_pe_norm_g'], 'm_w_pg': out['m_w_pg'], 'm_w_pp': out['m_w_pp'], 'm_final_g': out['m_final_g'], 'v_ln_g': out['v_ln_g'], 'v_w_in': out['v_w_in'], 'v_conv_w': out['v_conv_w'], 'v_conv_b': out['v_conv_b'], 'v_cnorm_g': out['v_cnorm_g'], 'v_cnorm_b': out['v_cnorm_b'], 'v_w_pw2': out['v_w_pw2'], 'v_b_pw2': out['v_b_pw2'], 'v_lb_logits': out['v_lb_logits'], 'v_onorm_g': out['v_onorm_g'], 'v_w_out': out['v_w_out'], 'v_pe_norm_g': out['v_pe_norm_g'], 'v_w_pg': out['v_w_pg'], 'v_w_pp': out['v_w_pp'], 'v_final_g': out['v_final_g']}


def _loss(weights, diff, rest, loss_target):
    with _jax.named_scope("forward"):
        args = {**rest, TWIN_DIFF_INPUT: diff, **{k: w.astype(_WEIGHT_DTYPES[k]) for k, w in weights.items()}}
        y = _forward(args)
    with _jax.named_scope("loss_head"):
        err = _jnp.square(y.astype(_jnp.float32) - loss_target)
        return 0.5 * _jnp.sum(_jnp.mean(err, axis=-1)) if err.ndim else 0.5 * err


def _adamw(w, g, m, v):
    m = ADAM_B1 * m + (1.0 - ADAM_B1) * g
    v = ADAM_B2 * v + (1.0 - ADAM_B2) * _jnp.square(g)
    m_hat = m / (1.0 - ADAM_B1 ** ADAM_STEP)
    v_hat = v / (1.0 - ADAM_B2 ** ADAM_STEP)
    delta = -ADAM_LR * (m_hat / (_jnp.sqrt(v_hat) + ADAM_EPS) + ADAM_WD * w)
    return delta, m, v


def reference(x, p, ln_g, w_in, conv_w, conv_b, cnorm_g, cnorm_b, w_pw2, b_pw2, lb_logits, onorm_g, w_out, pe_norm_g, w_pg, w_pp, final_g, loss_target, m_ln_g, m_w_in, m_conv_w, m_conv_b, m_cnorm_g, m_cnorm_b, m_w_pw2, m_b_pw2, m_lb_logits, m_onorm_g, m_w_out, m_pe_norm_g, m_w_pg, m_w_pp, m_final_g, v_ln_g, v_w_in, v_conv_w, v_conv_b, v_cnorm_g, v_cnorm_b, v_w_pw2, v_b_pw2, v_lb_logits, v_onorm_g, v_w_out, v_pe_norm_g, v_w_pg, v_w_pp, v_final_g):
    given = dict(x=x, p=p, ln_g=ln_g, w_in=w_in, conv_w=conv_w, conv_b=conv_b, cnorm_g=cnorm_g, cnorm_b=cnorm_b, w_pw2=w_pw2, b_pw2=b_pw2, lb_logits=lb_logits, onorm_g=onorm_g, w_out=w_out, pe_norm_g=pe_norm_g, w_pg=w_pg, w_pp=w_pp, final_g=final_g, loss_target=loss_target, m_ln_g=m_ln_g, m_w_in=m_w_in, m_conv_w=m_conv_w, m_conv_b=m_conv_b, m_cnorm_g=m_cnorm_g, m_cnorm_b=m_cnorm_b, m_w_pw2=m_w_pw2, m_b_pw2=m_b_pw2, m_lb_logits=m_lb_logits, m_onorm_g=m_onorm_g, m_w_out=m_w_out, m_pe_norm_g=m_pe_norm_g, m_w_pg=m_w_pg, m_w_pp=m_w_pp, m_final_g=m_final_g, v_ln_g=v_ln_g, v_w_in=v_w_in, v_conv_w=v_conv_w, v_conv_b=v_conv_b, v_cnorm_g=v_cnorm_g, v_cnorm_b=v_cnorm_b, v_w_pw2=v_w_pw2, v_b_pw2=v_b_pw2, v_lb_logits=v_lb_logits, v_onorm_g=v_onorm_g, v_w_out=v_w_out, v_pe_norm_g=v_pe_norm_g, v_w_pg=v_w_pg, v_w_pp=v_w_pp, v_final_g=v_final_g)
    weights = {n: given[n] for n in TWIN_WEIGHTS}
    shared = {n: given[n] for n in SHARED_INPUTS}
    per_example = {n: given[n] for n in ['x', 'p']}
    grad_fn = _jax.value_and_grad(_loss, argnums=(0, 1))

    def one_microbatch(ex, loss_target):
        ex = dict(ex)
        diff = ex.pop(TWIN_DIFF_INPUT)
        return grad_fn(weights, diff, {**shared, **ex}, loss_target)

    if N_MICROBATCH == 1:
        loss, (grad_w, grad_x) = one_microbatch(per_example, given["loss_target"])
    else:
        def body(carry, xs):
            loss_sum, grad_sum = carry
            l_k, (gw_k, gx_k) = one_microbatch(xs[0], xs[1])
            with _jax.named_scope("update"):
                return (loss_sum + l_k, _jax.tree.map(_jnp.add, grad_sum, gw_k)), gx_k

        init = (_jnp.zeros((), _jnp.float32), _jax.tree.map(_jnp.zeros_like, weights))
        (loss, grad_w), grad_x = _jax.lax.scan(body, init, (per_example, given["loss_target"]))
    with _jax.named_scope("update"):
        delta_w, new_m, new_v = {}, {}, {}
        for n in TWIN_WEIGHTS:
            delta_w[n], new_m[n], new_v[n] = _adamw(weights[n], grad_w[n], given["m_" + n], given["v_" + n])
    return (loss, grad_x, *[grad_w[n] for n in TWIN_WEIGHTS], *[delta_w[n] for n in TWIN_WEIGHTS],
            *[new_m[n] for n in TWIN_WEIGHTS], *[new_v[n] for n in TWIN_WEIGHTS])
```

```python
import functools

import jax
import jax.numpy as jnp
from jax import lax
from jax.experimental import pallas as pl
from jax.experimental.pallas import tpu as pltpu

F32 = jnp.float32
BF16 = jnp.bfloat16
MESH = pl.DeviceIdType.MESH

N_DEV = 8
D = 1024
N_COLS = 7 * D
COLS_PER_DEV = N_COLS // N_DEV
PLE = 256
HEAD = 128
N_HEADS = D // HEAD
CONV_K = 31
CONV_PAD = 32
CHUNK = 64
EPS = 1e-6
SUBLANES = 8

ADAM_LR = 0.001
ADAM_B1 = 0.9
ADAM_B2 = 0.999
ADAM_EPS = 1e-08
ADAM_WD = 0.01
ADAM_STEP = 10

MIB = 1024 * 1024
N_SMALL = 16
R_LN, R_CONVB, R_CNG, R_CNB, R_BPW2, R_LB0, R_LB1, R_ON, R_PEN, R_FIN, R_LOSS = range(11)


def _params(vmem_mib, **kw):
    return pltpu.CompilerParams(vmem_limit_bytes=vmem_mib * MIB, **kw)


def _dot(a, b):
    return jnp.dot(a.astype(BF16), b.astype(BF16), preferred_element_type=F32)


def _dot_nt(a, b):
    return lax.dot_general(a.astype(BF16), b.astype(BF16), (((1,), (1,)), ((), ())),
                           preferred_element_type=F32)


def _dot_tn(a, b):
    return lax.dot_general(a.astype(BF16), b.astype(BF16), (((0,), (0,)), ((), ())),
                           preferred_element_type=F32)


def _split(a):
    hi = a.astype(BF16)
    return hi, (a - hi.astype(F32)).astype(BF16)


def _dot_split(a, b, dims):
    ah, al = _split(a)
    bh, bl = _split(b)
    dg = lambda p, q: lax.dot_general(p, q, dims, preferred_element_type=F32)
    return dg(ah, bh) + (dg(ah, bl) + dg(al, bh))


def _sigmoid(x):
    return 1.0 / (1.0 + jnp.exp(-x))


def _rowsum8(a):
    r, c = a.shape
    return jnp.sum(a.reshape(r // SUBLANES, SUBLANES, c), axis=0)


def _tri_dot(tri, a):
    hi = a.astype(BF16)
    r1 = a - hi.astype(F32)
    mid = r1.astype(BF16)
    lo = (r1 - mid.astype(F32)).astype(BF16)
    return (jnp.dot(tri, hi, preferred_element_type=F32)
            + jnp.dot(tri, mid, preferred_element_type=F32)
            + jnp.dot(tri, lo, preferred_element_type=F32))


def _lower_bound(lbl):
    l0, l1 = lbl[0:1, :], lbl[1:2, :]
    m = jnp.maximum(l0, l1)
    e0, e1 = jnp.exp(l0 - m), jnp.exp(l1 - m)
    s = e0 + e1
    return e0 / s, e1 / s


ANY = pl.BlockSpec(memory_space=pl.ANY)


def _full(shape):
    return pl.BlockSpec(shape, lambda i: (0,) * len(shape))


def _peer(x, y, c, k):
    px = 1 - x if k & 4 else x
    py = 1 - y if k & 2 else y
    pc = 1 - c if k & 1 else c
    return (px, py, pc), 4 * px + 2 * py + pc


def _all_gather(shards):
    n = len(shards)

    def body(*refs):
        ins, outs = refs[:n], refs[n:2 * n]
        send_sems, recv_sems, local_sems = refs[2 * n:]
        x, y, c = lax.axis_index("x"), lax.axis_index("y"), lax.axis_index("c")
        me = 4 * x + 2 * y + c
        local = [pltpu.make_async_copy(ins[a], outs[a].at[me], local_sems.at[a]) for a in range(n)]
        for cp in local:
            cp.start()
        sends, recvs = [], []
        for a in range(n):
            for k in range(1, N_DEV):
                peer, peer_idx = _peer(x, y, c, k)
                sem = a * (N_DEV - 1) + k - 1
                sends.append(pltpu.make_async_remote_copy(
                    src_ref=ins[a], dst_ref=outs[a].at[me], send_sem=send_sems.at[sem],
                    recv_sem=recv_sems.at[sem], device_id=peer, device_id_type=MESH))
                recvs.append(pltpu.make_async_remote_copy(
                    src_ref=ins[a], dst_ref=outs[a].at[peer_idx], send_sem=send_sems.at[sem],
                    recv_sem=recv_sems.at[sem], device_id=peer, device_id_type=MESH))
        for cp in sends:
            cp.start()
        for cp in recvs:
            cp.wait_recv()
        for cp in sends:
            cp.wait_send()
        for cp in local:
            cp.wait()

    return pl.pallas_call(
        body, name="gather_weights",
        out_shape=[jax.ShapeDtypeStruct((N_DEV,) + s.shape, s.dtype) for s in shards],
        in_specs=[ANY] * n, out_specs=[ANY] * n,
        scratch_shapes=[pltpu.SemaphoreType.DMA((n * (N_DEV - 1),)),
                        pltpu.SemaphoreType.DMA((n * (N_DEV - 1),)),
                        pltpu.SemaphoreType.DMA((n,))],
    )(*shards)


def _exchange_grads(grads, small):
    n = len(grads)

    def body(*refs):
        ins, small_in = refs[:n], refs[n]
        outs, small_out = refs[n + 1:2 * n + 1], refs[2 * n + 1]
        send_sems, recv_sems, local_sems = refs[2 * n + 2:]
        x, y, c = lax.axis_index("x"), lax.axis_index("y"), lax.axis_index("c")
        me = 4 * x + 2 * y + c
        local = [pltpu.make_async_copy(ins[a].at[me], outs[a].at[me], local_sems.at[a])
                 for a in range(n)]
        local.append(pltpu.make_async_copy(small_in, small_out.at[me], local_sems.at[n]))
        for cp in local:
            cp.start()
        sends, recvs = [], []
        for a in range(n + 1):
            for k in range(1, N_DEV):
                peer, peer_idx = _peer(x, y, c, k)
                sem = a * (N_DEV - 1) + k - 1
                src = ins[a].at[peer_idx] if a < n else small_in
                out = outs[a] if a < n else small_out
                sends.append(pltpu.make_async_remote_copy(
                    src_ref=src, dst_ref=out.at[me], send_sem=send_sems.at[sem],
                    recv_sem=recv_sems.at[sem], device_id=peer, device_id_type=MESH))
                recvs.append(pltpu.make_async_remote_copy(
                    src_ref=src, dst_ref=out.at[peer_idx], send_sem=send_sems.at[sem],
                    recv_sem=recv_sems.at[sem], device_id=peer, device_id_type=MESH))
        for cp in sends:
            cp.start()
        for cp in recvs:
            cp.wait_recv()
        for cp in sends:
            cp.wait_send()
        for cp in local:
            cp.wait()

    n_sem = (n + 1) * (N_DEV - 1)
    return pl.pallas_call(
        body, name="exchange_grads",
        out_shape=[jax.ShapeDtypeStruct(g.shape, g.dtype) for g in grads]
        + [jax.ShapeDtypeStruct((N_DEV,) + small.shape, small.dtype)],
        in_specs=[ANY] * (n + 1), out_specs=[ANY] * (n + 1),
        scratch_shapes=[pltpu.SemaphoreType.DMA((n_sem,)), pltpu.SemaphoreType.DMA((n_sem,)),
                        pltpu.SemaphoreType.DMA((n + 1,))],
    )(*grads, small)


def _inproj_fwd(x, ln_g, w_in_all):
    t = x.shape[0]
    tt = min(256, t)

    def body(x_ref, g_ref, w_hbm, z_ref, u_ref, w_vmem):
        @pl.when(pl.program_id(0) == 0)
        def _():
            pltpu.sync_copy(w_hbm, w_vmem)

        xv = x_ref[...]
        rstd = lax.rsqrt(jnp.mean(xv * xv, axis=-1, keepdims=True) + EPS)
        ub = (xv * rstd * g_ref[...]).astype(BF16)
        u_ref[...] = ub
        for d in range(N_DEV):
            z_ref[:, COLS_PER_DEV * d:COLS_PER_DEV * (d + 1)] = jnp.dot(
                ub, w_vmem[d], preferred_element_type=F32)

    return pl.pallas_call(
        body, name="inproj_fwd", grid=(t // tt,),
        out_shape=[jax.ShapeDtypeStruct((t, N_COLS), F32), jax.ShapeDtypeStruct((t, D), BF16)],
        in_specs=[pl.BlockSpec((tt, D), lambda i: (i, 0)), _full((1, D)), ANY],
        out_specs=[pl.BlockSpec((tt, N_COLS), lambda i: (i, 0)),
                   pl.BlockSpec((tt, D), lambda i: (i, 0))],
        scratch_shapes=[pltpu.VMEM((N_DEV, D, COLS_PER_DEV), BF16)],
        compiler_params=_params(48, dimension_semantics=("arbitrary",)),
    )(x, ln_g, w_in_all)


def _shifted_copies(buf, shifted, rows):
    for b in range(1, SUBLANES):
        shifted[b, 0:rows, :] = buf[b:b + rows, :]


def _tap_ref(buf, shifted, offset):
    a, b = divmod(offset, SUBLANES)
    return (buf if b == 0 else shifted.at[b]), SUBLANES * a


def _group_norm_stats(blk):
    mu = jnp.mean(blk, axis=-1, keepdims=True)
    cen = blk - mu
    var = jnp.mean(cen * cen, axis=-1, keepdims=True)
    return cen * lax.rsqrt(var + EPS)


def _conv_fwd(z, conv_w_all, conv_b, cn_g, cn_b, w_pw2, b_pw2):
    t = z.shape[0]
    tt = min(256, t)
    rc = 32

    def body(val_ref, glu_ref, gate_ref, cw_ref, cb_ref, g_ref, b_ref, w_hbm, b2_ref,
             yc_ref, y2_ref, yo_ref, w_vmem, vbuf, vsh, y1buf):
        @pl.when(pl.program_id(0) == 0)
        def _():
            pltpu.sync_copy(w_hbm, w_vmem)
            vbuf[0:CONV_PAD, :] = jnp.zeros((CONV_PAD, D), F32)

        vbuf[CONV_PAD:CONV_PAD + tt, :] = val_ref[...] * _sigmoid(glu_ref[...])
        _shifted_copies(vbuf, vsh, tt + 24)

        def row_chunk(r, carry):
            r0 = pl.multiple_of(r * rc, rc)
            for g in range(N_HEADS):
                cs = slice(HEAD * g, HEAD * (g + 1))
                acc = jnp.zeros((rc, HEAD), F32)
                for k in range(CONV_K):
                    ref, off = _tap_ref(vbuf, vsh, k + 2)
                    acc = acc + cw_ref[g, k:k + 1, :] * ref[pl.ds(r0 + off, rc), cs]
                acc = acc + cb_ref[:, cs]
                yc_ref[pl.ds(r0, rc), cs] = acc
                n = _group_norm_stats(acc) * g_ref[:, cs] + b_ref[:, cs]
                y1buf[pl.ds(r0, rc), cs] = (n * _sigmoid(n)).astype(BF16)
            return carry

        lax.fori_loop(0, tt // rc, row_chunk, 0)
        vbuf[0:CONV_PAD, :] = vbuf[tt:tt + CONV_PAD, :]
        y2 = jnp.dot(y1buf[...], w_vmem[...], preferred_element_type=F32) + b2_ref[...]
        y2_ref[...] = y2
        gate = gate_ref[...]
        yo_ref[...] = (y2 * gate * _sigmoid(gate)).astype(BF16)

    col = lambda j: pl.BlockSpec((tt, D), lambda i: (i, j))
    row = pl.BlockSpec((tt, D), lambda i: (i, 0))
    return pl.pallas_call(
        body, name="conv_fwd", grid=(t // tt,),
        out_shape=[jax.ShapeDtypeStruct((t, D), F32), jax.ShapeDtypeStruct((t, D), F32),
                   jax.ShapeDtypeStruct((t, D), BF16)],
        in_specs=[col(0), col(1), col(2), _full((N_DEV, CONV_PAD, HEAD)), _full((1, D)),
                  _full((1, D)), _full((1, D)), ANY, _full((1, D))],
        out_specs=[row, row, row],
        scratch_shapes=[pltpu.VMEM((D, D), BF16), pltpu.VMEM((tt + CONV_PAD, D), F32),
                        pltpu.VMEM((SUBLANES, tt + CONV_PAD, D), F32), pltpu.VMEM((tt, D), BF16)],
        compiler_params=_params(48, dimension_semantics=("arbitrary",)),
    )(z, z, z, conv_w_all, conv_b, cn_g, cn_b, w_pw2, b_pw2)


def _chunk_quantities(zq, zf, lbh, tri):
    sig = _sigmoid(zf)
    sig_neg = _sigmoid(-zf)
    f = lbh + (1.0 - lbh) * sig
    k = (1.0 - lbh) * sig_neg
    q = zq * _sigmoid(zq)
    b = _tri_dot(tri, jnp.log(f))
    b_mid = b[CHUNK // 2 - 1:CHUNK // 2, :]
    b_last = b[CHUNK - 1:CHUNK, :]
    e_q = jnp.exp(b)
    e_qm = jnp.exp(b - b_mid)
    e_km = jnp.exp(b_mid - b)
    e_kd = jnp.exp(b_last - b)
    return q, k, f, sig, sig_neg, e_q, e_qm, e_km, e_kd, jnp.exp(b_last)


def _hgrn_fwd(z, lb_logits, onorm_g):
    t = z.shape[0]
    tt = min(256, t)
    nc = tt // CHUNK

    def body(q_ref, f_ref, i_ref, g_ref, lbl_ref, on_ref, o_ref, y_ref, s_ref, st):
        @pl.when(pl.program_id(0) == 0)
        def _():
            st[...] = jnp.zeros_like(st)

        lb, _ = _lower_bound(lbl_ref[...])
        rows = lax.broadcasted_iota(jnp.int32, (CHUNK, CHUNK), 0)
        cols = lax.broadcasted_iota(jnp.int32, (CHUNK, CHUNK), 1)
        causal = rows >= cols
        tri = causal.astype(BF16)

        def chunk(c, carry):
            r0 = pl.multiple_of(c * CHUNK, CHUNK)
            rs = pl.ds(r0, CHUNK)
            for h in range(N_HEADS):
                cs = slice(HEAD * h, HEAD * (h + 1))
                q, k, _, _, _, e_q, e_qm, e_km, e_kd, e_last = _chunk_quantities(
                    q_ref[rs, cs], f_ref[rs, cs], lb[:, cs], tri)
                v = i_ref[rs, cs]
                s_old = st[h]
                s_ref[c, h] = s_old
                a = jnp.where(causal, _dot_nt(q * e_qm, k * e_km), 0.0)
                o = _dot_nt(q * e_q, s_old) + _dot(a, v)
                st[h] = s_old * e_last + _dot_tn(v, k * e_kd)
                o_ref[rs, cs] = o
                n = o * lax.rsqrt(jnp.mean(o * o, axis=-1, keepdims=True) + EPS)
                zg = g_ref[rs, cs]
                y_ref[rs, cs] = (n * on_ref[:, cs] * zg * _sigmoid(zg)).astype(BF16)
            return carry

        lax.fori_loop(0, nc, chunk, 0)

    col = lambda j: pl.BlockSpec((tt, D), lambda i: (i, j))
    row = pl.BlockSpec((tt, D), lambda i: (i, 0))
    return pl.pallas_call(
        body, name="hgrn_fwd", grid=(t // tt,),
        out_shape=[jax.ShapeDtypeStruct((t, D), F32), jax.ShapeDtypeStruct((t, D), BF16),
                   jax.ShapeDtypeStruct((t // CHUNK, N_HEADS, HEAD, HEAD), F32)],
        in_specs=[col(3), col(4), col(5), col(6), _full((2, D)), _full((1, D))],
        out_specs=[row, row, pl.BlockSpec((nc, N_HEADS, HEAD, HEAD), lambda i: (i, 0, 0, 0))],
        scratch_shapes=[pltpu.VMEM((N_HEADS, HEAD, HEAD), F32)],
        compiler_params=_params(40, dimension_semantics=("arbitrary",)),
    )(z, z, z, z, lb_logits, onorm_g)


def _rms_bwd(dn, xhat, rstd):
    return rstd * (dn - xhat * jnp.mean(dn * xhat, axis=-1, keepdims=True))


def _tail(x, y_conv, y_hgrn, p, target, w_out, w_pg, w_pp_all, pe_g, fin_g):
    t = x.shape[0]
    tt = min(256, t)
    n_steps = t // tt

    def body(x_ref, yc_ref, yh_ref, p_ref, tg_ref, wo_hbm, wg_hbm, wp_hbm, pg_ref, fg_ref,
             dh1_ref, dyc_ref, dyh_ref, dwo_hbm, dwg_hbm, dwp_hbm, dpg_ref, dfg_ref, loss_ref,
             wo, wg, wp, dwo, dwg, dwp):
        i = pl.program_id(0)

        @pl.when(i == 0)
        def _():
            pltpu.sync_copy(wo_hbm, wo)
            pltpu.sync_copy(wg_hbm, wg)
            for d in range(N_DEV):
                pltpu.sync_copy(wp_hbm.at[d], wp.at[:, pl.ds(HEAD * d, HEAD)])
            dwo[...] = jnp.zeros_like(dwo)
            dwg[...] = jnp.zeros_like(dwg)
            dwp[...] = jnp.zeros_like(dwp)
            dpg_ref[...] = jnp.zeros_like(dpg_ref)
            dfg_ref[...] = jnp.zeros_like(dfg_ref)
            loss_ref[...] = jnp.zeros_like(loss_ref)

        ycv, yhv = yc_ref[...], yh_ref[...]
        h1 = (x_ref[...] + jnp.dot(ycv, wo[0:D, :], preferred_element_type=F32)
              + jnp.dot(yhv, wo[D:2 * D, :], preferred_element_type=F32))
        pb = p_ref[...].astype(BF16)
        pe = jnp.dot(pb, wp[...], preferred_element_type=F32)
        rstd1 = lax.rsqrt(jnp.mean(h1 * h1, axis=-1, keepdims=True) + EPS)
        n1 = h1 * rstd1
        rb = (n1 * pg_ref[...]).astype(BF16)
        gate = _sigmoid(jnp.dot(rb, wg[...], preferred_element_type=F32))
        h2 = h1 + gate * pe
        rstd2 = lax.rsqrt(jnp.mean(h2 * h2, axis=-1, keepdims=True) + EPS)
        n2 = h2 * rstd2
        err = n2 * fg_ref[...] - tg_ref[...]
        loss_ref[...] += _rowsum8(err * err)

        d_out = err * (1.0 / D)
        dfg_ref[...] += _rowsum8(d_out * n2)
        d_h2 = _rms_bwd(d_out * fg_ref[...], n2, rstd2)
        d_pe = (d_h2 * gate).astype(BF16)
        d_gpre = (d_h2 * pe * gate * (1.0 - gate)).astype(BF16)
        dwg[...] += _dot_tn(rb, d_gpre)
        dwp[...] += _dot_tn(pb, d_pe)
        dr = _dot_nt(d_gpre, wg[...])
        dpg_ref[...] += _rowsum8(dr * n1)
        d_h1 = d_h2 + _rms_bwd(dr * pg_ref[...], n1, rstd1)
        dh1_ref[...] = d_h1
        d_h1b = d_h1.astype(BF16)
        dwo[0:D, :] += _dot_tn(ycv, d_h1b)
        dwo[D:2 * D, :] += _dot_tn(yhv, d_h1b)
        dyc_ref[...] = _dot_nt(d_h1b, wo[0:D, :])
        dyh_ref[...] = _dot_nt(d_h1b, wo[D:2 * D, :])

        @pl.when(i == n_steps - 1)
        def _():
            pltpu.sync_copy(dwo, dwo_hbm)
            pltpu.sync_copy(dwg, dwg_hbm)
            for d in range(N_DEV):
                pltpu.sync_copy(dwp.at[:, pl.ds(HEAD * d, HEAD)], dwp_hbm.at[d])

    row = pl.BlockSpec((tt, D), lambda i: (i, 0))
    acc = _full((SUBLANES, D))
    return pl.pallas_call(
        body, name="tail_fwd_bwd", grid=(n_steps,),
        out_shape=[jax.ShapeDtypeStruct((t, D), F32)] * 3
        + [jax.ShapeDtypeStruct((2 * D, D), F32), jax.ShapeDtypeStruct((D, D), F32),
           jax.ShapeDtypeStruct((N_DEV, PLE, HEAD), F32)]
        + [jax.ShapeDtypeStruct((SUBLANES, D), F32)] * 3,
        in_specs=[row, row, row, pl.BlockSpec((tt, PLE), lambda i: (i, 0)), row,
                  ANY, ANY, ANY, _full((1, D)), _full((1, D))],
        out_specs=[row, row, row, ANY, ANY, ANY, acc, acc, acc],
        scratch_shapes=[pltpu.VMEM((2 * D, D), BF16), pltpu.VMEM((D, D), BF16),
                        pltpu.VMEM((PLE, D), BF16), pltpu.VMEM((2 * D, D), F32),
                        pltpu.VMEM((D, D), F32), pltpu.VMEM((PLE, D), F32)],
        compiler_params=_params(52, dimension_semantics=("arbitrary",)),
    )(x, y_conv, y_hgrn, p, target, w_out, w_pg, w_pp_all, pe_g, fin_g)


def _hgrn_bwd(dy, z, o_raw, states, lb_logits, onorm_g):
    t = z.shape[0]
    tt = min(256, t)
    nc = tt // CHUNK
    n_steps = t // tt

    def body(dy_ref, q_ref, f_ref, i_ref, g_ref, o_ref, s_ref, lbl_ref, on_ref,
             dz_ref, don_ref, dlb_ref, dst):
        @pl.when(pl.program_id(0) == 0)
        def _():
            dst[...] = jnp.zeros_like(dst)
            don_ref[...] = jnp.zeros_like(don_ref)
            dlb_ref[...] = jnp.zeros_like(dlb_ref)

        lb, _ = _lower_bound(lbl_ref[...])
        rows = lax.broadcasted_iota(jnp.int32, (CHUNK, CHUNK), 0)
        cols = lax.broadcasted_iota(jnp.int32, (CHUNK, CHUNK), 1)
        causal = rows >= cols
        tri = causal.astype(BF16)
        tri_rev = (rows <= cols).astype(BF16)
        is_last = lax.broadcasted_iota(jnp.int32, (CHUNK, HEAD), 0) == CHUNK - 1

        def chunk(cc, carry):
            c = nc - 1 - cc
            r0 = pl.multiple_of(c * CHUNK, CHUNK)
            rs = pl.ds(r0, CHUNK)
            for h in range(N_HEADS):
                cs = slice(HEAD * h, HEAD * (h + 1))
                zq, zf, zg = q_ref[rs, cs], f_ref[rs, cs], g_ref[rs, cs]
                lbh = lb[:, cs]
                q, k, f, sig, sig_neg, e_q, e_qm, e_km, e_kd, e_last = _chunk_quantities(
                    zq, zf, lbh, tri)
                v = i_ref[rs, cs]
                qt, qm, km, kd = q * e_q, q * e_qm, k * e_km, k * e_kd
                a = jnp.where(causal, _dot_nt(qm, km), 0.0)

                o = o_ref[rs, cs]
                rstd = lax.rsqrt(jnp.mean(o * o, axis=-1, keepdims=True) + EPS)
                n = o * rstd
                sg = _sigmoid(zg)
                dyv = dy_ref[rs, cs]
                on = on_ref[:, cs]
                d_zg = dyv * n * on * sg * (1.0 + zg * (1.0 - sg))
                d_on = dyv * zg * sg
                don_ref[:, cs] += _rowsum8(d_on * n)
                do = _rms_bwd(d_on * on, n, rstd)

                s_old = s_ref[c, h]
                ds_new = dst[h]
                da = jnp.where(causal, _dot_nt(do, v), 0.0)
                dv = _dot_tn(a, do) + _dot_nt(kd, ds_new)
                dkd = _dot(v, ds_new)
                dqm = _dot_split(da, km, (((1,), (0,)), ((), ())))
                dkm = _dot_split(da, qm, (((0,), (0,)), ((), ())))
                dq = _dot(do, s_old) * e_q + dqm * e_qm
                dk = dkm * e_km + dkd * e_kd
                dst[h] = ds_new * e_last + _dot_tn(do, qt)
                last = (jnp.sum(dkd * kd, axis=0, keepdims=True)
                        + e_last * jnp.sum(s_old * ds_new, axis=0, keepdims=True))
                db = q * dq - k * dk + jnp.where(is_last, last, 0.0)
                dlogf = _tri_dot(tri_rev, db)
                common = sig_neg * (dlogf / f - dk)
                dlb_ref[:, cs] += _rowsum8(common)
                dz_ref[rs, HEAD * h:HEAD * (h + 1)] = (
                    dq * _sigmoid(zq) * (1.0 + zq * (1.0 - _sigmoid(zq)))).astype(BF16)
                dz_ref[rs, D + HEAD * h:D + HEAD * (h + 1)] = (
                    (1.0 - lbh) * sig * common).astype(BF16)
                dz_ref[rs, 2 * D + HEAD * h:2 * D + HEAD * (h + 1)] = dv.astype(BF16)
                dz_ref[rs, 3 * D + HEAD * h:3 * D + HEAD * (h + 1)] = d_zg.astype(BF16)
            return carry

        lax.fori_loop(0, nc, chunk, 0)

    rev = lambda i: n_steps - 1 - i
    col = lambda j: pl.BlockSpec((tt, D), lambda i: (rev(i), j))
    row = pl.BlockSpec((tt, D), lambda i: (rev(i), 0))
    acc = _full((SUBLANES, D))
    return pl.pallas_call(
        body, name="hgrn_bwd", grid=(n_steps,),
        out_shape=[jax.ShapeDtypeStruct((t, 4 * D), BF16),
                   jax.ShapeDtypeStruct((SUBLANES, D), F32),
                   jax.ShapeDtypeStruct((SUBLANES, D), F32)],
        in_specs=[row, col(3), col(4), col(5), col(6), row,
                  pl.BlockSpec((nc, N_HEADS, HEAD, HEAD), lambda i: (rev(i), 0, 0, 0)),
                  _full((2, D)), _full((1, D))],
        out_specs=[pl.BlockSpec((tt, 4 * D), lambda i: (rev(i), 0)), acc, acc],
        scratch_shapes=[pltpu.VMEM((N_HEADS, HEAD, HEAD), F32)],
        compiler_params=_params(48, dimension_semantics=("arbitrary",)),
    )(dy, z, z, z, z, o_raw, states, lb_logits, onorm_g)


def _conv_bwd(dy, z, yc, y2, conv_w_all, cn_g, cn_b, w_pw2):
    t = z.shape[0]
    tt = min(256, t)
    rc = 32
    n_steps = t // tt

    def body(dy_ref, val_ref, glu_ref, gate_ref, yc_ref, y2_ref, cw_ref, g_ref, b_ref, w_hbm,
             dz_ref, dw_hbm, dcw_out, db2_ref, dg_ref, dbeta_ref, dcb_ref,
             w_vmem, dw, dbuf, dsh, y1buf, dnbuf, dcw_ref):
        i = pl.program_id(0)

        @pl.when(i == 0)
        def _():
            pltpu.sync_copy(w_hbm, w_vmem)
            dw[...] = jnp.zeros_like(dw)
            dbuf[tt:tt + CONV_PAD, :] = jnp.zeros((CONV_PAD, D), F32)
            dcw_ref[...] = jnp.zeros_like(dcw_ref)
            dcw_out[...] = jnp.zeros_like(dcw_out)
            db2_ref[...] = jnp.zeros_like(db2_ref)
            dg_ref[...] = jnp.zeros_like(dg_ref)
            dbeta_ref[...] = jnp.zeros_like(dbeta_ref)
            dcb_ref[...] = jnp.zeros_like(dcb_ref)

        gate = gate_ref[...]
        sg = _sigmoid(gate)
        dyv = dy_ref[...]
        dy2 = dyv * gate * sg
        dz_ref[:, 2 * D:3 * D] = (dyv * y2_ref[...] * sg * (1.0 + gate * (1.0 - sg))).astype(BF16)
        db2_ref[...] += _rowsum8(dy2)
        dy2b = dy2.astype(BF16)
        dnbuf[...] = _dot_nt(dy2b, w_vmem[...])

        def norm_chunk(r, carry):
            r0 = pl.multiple_of(r * rc, rc)
            rs = pl.ds(r0, rc)
            for g in range(N_HEADS):
                cs = slice(HEAD * g, HEAD * (g + 1))
                blk = yc_ref[rs, cs]
                mu = jnp.mean(blk, axis=-1, keepdims=True)
                cen = blk - mu
                rstd = lax.rsqrt(jnp.mean(cen * cen, axis=-1, keepdims=True) + EPS)
                xhat = cen * rstd
                n = xhat * g_ref[:, cs] + b_ref[:, cs]
                sn = _sigmoid(n)
                y1buf[rs, cs] = (n * sn).astype(BF16)
                dn = dnbuf[rs, cs] * sn * (1.0 + n * (1.0 - sn))
                dg_ref[:, cs] += _rowsum8(dn * xhat)
                dbeta_ref[:, cs] += _rowsum8(dn)
                dxh = dn * g_ref[:, cs]
                dyc = rstd * (dxh - jnp.mean(dxh, axis=-1, keepdims=True)
                              - xhat * jnp.mean(dxh * xhat, axis=-1, keepdims=True))
                dcb_ref[:, cs] += _rowsum8(dyc)
                dbuf[rs, cs] = dyc
            return carry

        lax.fori_loop(0, tt // rc, norm_chunk, 0)
        dw[...] += _dot_tn(y1buf[...], dy2b)
        _shifted_copies(dbuf, dsh, tt + 24)

        def conv_chunk(r, carry):
            r0 = pl.multiple_of(r * rc, rc)
            rs = pl.ds(r0, rc)
            for g in range(N_HEADS):
                cs = slice(HEAD * g, HEAD * (g + 1))
                sglu = _sigmoid(glu_ref[rs, cs])
                val = val_ref[rs, cs]
                v = val * sglu
                dv = jnp.zeros((rc, HEAD), F32)
                for k in range(CONV_K):
                    ref, off = _tap_ref(dbuf, dsh, CONV_K - 1 - k)
                    d_later = ref[pl.ds(r0 + off, rc), cs]
                    dv = dv + cw_ref[g, k:k + 1, :] * d_later
                    dcw_ref[g, k] += _rowsum8(v * d_later)
                dz_ref[rs, cs] = (dv * sglu).astype(BF16)
                dz_ref[rs, D + HEAD * g:D + HEAD * (g + 1)] = (
                    dv * val * sglu * (1.0 - sglu)).astype(BF16)
            return carry

        lax.fori_loop(0, tt // rc, conv_chunk, 0)
        dbuf[tt:tt + CONV_PAD, :] = dbuf[0:CONV_PAD, :]

        @pl.when(i == n_steps - 1)
        def _():
            pltpu.sync_copy(dw, dw_hbm)
            for g in range(N_HEADS):
                for k in range(CONV_K):
                    dcw_out[g, k:k + 1, :] = jnp.sum(dcw_ref[g, k], axis=0, keepdims=True)

    rev = lambda i: n_steps - 1 - i
    col = lambda j: pl.BlockSpec((tt, D), lambda i: (rev(i), j))
    row = pl.BlockSpec((tt, D), lambda i: (rev(i), 0))
    acc = _full((SUBLANES, D))
    return pl.pallas_call(
        body, name="conv_bwd", grid=(n_steps,),
        out_shape=[jax.ShapeDtypeStruct((t, 3 * D), BF16), jax.ShapeDtypeStruct((D, D), F32),
                   jax.ShapeDtypeStruct((N_DEV, CONV_PAD, HEAD), F32)]
        + [jax.ShapeDtypeStruct((SUBLANES, D), F32)] * 4,
        in_specs=[row, col(0), col(1), col(2), row, row, _full((N_DEV, CONV_PAD, HEAD)),
                  _full((1, D)), _full((1, D)), ANY],
        out_specs=[pl.BlockSpec((tt, 3 * D), lambda i: (rev(i), 0)), ANY,
                   _full((N_DEV, CONV_PAD, HEAD)), acc, acc, acc, acc],
        scratch_shapes=[pltpu.VMEM((D, D), BF16), pltpu.VMEM((D, D), F32),
                        pltpu.VMEM((tt + CONV_PAD, D), F32),
                        pltpu.VMEM((SUBLANES, tt + CONV_PAD, D), F32),
                        pltpu.VMEM((tt, D), BF16), pltpu.VMEM((tt, D), F32),
                        pltpu.VMEM((N_DEV, CONV_PAD, SUBLANES, HEAD), F32)],
        compiler_params=_params(52, dimension_semantics=("arbitrary",)),
    )(dy, z, z, z, yc, y2, conv_w_all, cn_g, cn_b, w_pw2)


def _inproj_bwd_dx(dz_c, dz_h, x, d_h1, ln_g, w_in_all):
    t = x.shape[0]
    tt = min(256, t)

    def body(dzc_ref, dzh_ref, x_ref, dh1_ref, g_ref, w_hbm, dx_ref, dg_ref, w_vmem, dzbuf):
        @pl.when(pl.program_id(0) == 0)
        def _():
            pltpu.sync_copy(w_hbm, w_vmem)
            dg_ref[...] = jnp.zeros_like(dg_ref)

        dzbuf[:, 0:3 * D] = dzc_ref[...]
        dzbuf[:, 3 * D:N_COLS] = dzh_ref[...]
        du = jnp.zeros((tt, D), F32)
        for d in range(N_DEV):
            du = du + lax.dot_general(
                dzbuf[:, COLS_PER_DEV * d:COLS_PER_DEV * (d + 1)], w_vmem[d],
                (((1,), (1,)), ((), ())), preferred_element_type=F32)
        xv = x_ref[...]
        rstd = lax.rsqrt(jnp.mean(xv * xv, axis=-1, keepdims=True) + EPS)
        xhat = xv * rstd
        dg_ref[...] += _rowsum8(du * xhat)
        dx_ref[...] = dh1_ref[...] + _rms_bwd(du * g_ref[...], xhat, rstd)

    row = pl.BlockSpec((tt, D), lambda i: (i, 0))
    return pl.pallas_call(
        body, name="inproj_bwd_dx", grid=(t // tt,),
        out_shape=[jax.ShapeDtypeStruct((t, D), F32), jax.ShapeDtypeStruct((SUBLANES, D), F32)],
        in_specs=[pl.BlockSpec((tt, 3 * D), lambda i: (i, 0)),
                  pl.BlockSpec((tt, 4 * D), lambda i: (i, 0)), row, row, _full((1, D)), ANY],
        out_specs=[row, _full((SUBLANES, D))],
        scratch_shapes=[pltpu.VMEM((N_DEV, D, COLS_PER_DEV), BF16), pltpu.VMEM((tt, N_COLS), BF16)],
        compiler_params=_params(48, dimension_semantics=("arbitrary",)),
    )(dz_c, dz_h, x, d_h1, ln_g, w_in_all)


def _inproj_bwd_dw(u, dz):
    t = u.shape[0]
    tt = min(512, t)

    def body(u_ref, dz_ref, dw_ref):
        @pl.when(pl.program_id(1) == 0)
        def _():
            dw_ref[...] = jnp.zeros_like(dw_ref)

        dw_ref[0] += lax.dot_general(u_ref[...], dz_ref[...], (((0,), (0,)), ((), ())),
                                     preferred_element_type=F32)

    return pl.pallas_call(
        body, name="inproj_bwd_dw", grid=(N_DEV, t // tt),
        out_shape=jax.ShapeDtypeStruct((N_DEV, D, COLS_PER_DEV), F32),
        in_specs=[pl.BlockSpec((tt, D), lambda j, i: (i, 0)),
                  pl.BlockSpec((tt, COLS_PER_DEV), lambda j, i: (i, j))],
        out_specs=pl.BlockSpec((1, D, COLS_PER_DEV), lambda j, i: (j, 0, 0)),
        compiler_params=_params(32, dimension_semantics=("arbitrary", "arbitrary")),
    )(u, dz)


def _adamw(w, g, m, v):
    m = ADAM_B1 * m + (1.0 - ADAM_B1) * g
    v = ADAM_B2 * v + (1.0 - ADAM_B2) * (g * g)
    m_hat = m / (1.0 - ADAM_B1 ** ADAM_STEP)
    v_hat = v / (1.0 - ADAM_B2 ** ADAM_STEP)
    delta = -ADAM_LR * (m_hat / (jnp.sqrt(v_hat) + ADAM_EPS) + ADAM_WD * w)
    return delta, m, v


def _pack_small(partials):
    rows = sorted(partials)

    def body(*refs):
        ins, out_ref = refs[:-1], refs[-1]
        out_ref[...] = jnp.zeros_like(out_ref)
        for j, row in enumerate(rows):
            out_ref[row:row + 1, :] = jnp.sum(ins[j][...], axis=0, keepdims=True)

    return pl.pallas_call(
        body, name="pack_small", out_shape=jax.ShapeDtypeStruct((N_SMALL, D), F32),
    )(*[partials[row] for row in rows])


def _sum_adam(name, recv, w, m, v, rows):
    r, c = w.shape

    def body(recv_ref, w_ref, m_ref, v_ref, g_ref, d_ref, mo_ref, vo_ref):
        g = recv_ref[0]
        for s in range(1, N_DEV):
            g = g + recv_ref[s]
        g_ref[...] = g
        d_ref[...], mo_ref[...], vo_ref[...] = _adamw(w_ref[...], g, m_ref[...], v_ref[...])

    blk = pl.BlockSpec((rows, c), lambda i: (i, 0))
    return pl.pallas_call(
        body, name=name, grid=(r // rows,),
        out_shape=[jax.ShapeDtypeStruct((r, c), F32)] * 4,
        in_specs=[pl.BlockSpec((N_DEV, rows, c), lambda i: (0, i, 0)), blk, blk, blk],
        out_specs=[blk] * 4,
        compiler_params=_params(40, dimension_semantics=("arbitrary",)),
    )(recv, w, m, v)


def _small_adam(gathered, lb_logits, w, m, v):
    def body(ga_ref, lbl_ref, w_ref, m_ref, v_ref, g_ref, d_ref, mo_ref, vo_ref, loss_ref):
        g = ga_ref[0]
        for s in range(1, N_DEV):
            g = g + ga_ref[s]
        s0, s1 = _lower_bound(lbl_ref[...])
        d_lb = g[R_LB0:R_LB0 + 1, :]
        rows = lax.broadcasted_iota(jnp.int32, (N_SMALL, D), 0)
        g = jnp.where(rows == R_LB0, d_lb * s0 * (1.0 - s0), g)
        g = jnp.where(rows == R_LB1, -d_lb * s0 * s1, g)
        g_ref[...] = g
        d_ref[...], mo_ref[...], vo_ref[...] = _adamw(w_ref[...], g, m_ref[...], v_ref[...])
        loss_ref[...] = (0.5 / D) * jnp.sum(g[R_LOSS:R_LOSS + 1, :], axis=-1, keepdims=True)

    return pl.pallas_call(
        body, name="small_adam",
        out_shape=[jax.ShapeDtypeStruct((N_SMALL, D), F32)] * 4 + [jax.ShapeDtypeStruct((1, 1), F32)],
    )(gathered, lb_logits, w, m, v)


def _pad_rows(a, rows):
    return jnp.pad(a, ((0, rows - a.shape[0]), (0, 0)))


def _pack_rows(rows):
    rows = [r.reshape(-1, D) for r in rows]
    packed = jnp.concatenate(rows, axis=0)
    return _pad_rows(packed, N_SMALL)


def kernel(x, p, ln_g, w_in, conv_w, conv_b, cnorm_g, cnorm_b, w_pw2, b_pw2, lb_logits, onorm_g, w_out, pe_norm_g, w_pg, w_pp, final_g, loss_target, m_ln_g, m_w_in, m_conv_w, m_conv_b, m_cnorm_g, m_cnorm_b, m_w_pw2, m_b_pw2, m_lb_logits, m_onorm_g, m_w_out, m_pe_norm_g, m_w_pg, m_w_pp, m_final_g, v_ln_g, v_w_in, v_conv_w, v_conv_b, v_cnorm_g, v_cnorm_b, v_w_pw2, v_b_pw2, v_lb_logits, v_onorm_g, v_w_out, v_pe_norm_g, v_w_pg, v_w_pp, v_final_g):
    t = x.shape[1]
    x2 = x.reshape(t, D)
    p2 = p.reshape(t, PLE)
    tg2 = loss_target.reshape(t, D)
    fin_g = final_g.reshape(1, D)

    w_in_all, conv_w_all, w_pw2_all, w_out_all, w_pg_all, w_pp_all = _all_gather([
        w_in[0].astype(BF16), _pad_rows(conv_w[0], CONV_PAD), w_pw2[0].astype(BF16),
        w_out[0].astype(BF16), w_pg[0].astype(BF16), w_pp[0].astype(BF16)])
    w_pw2_full = w_pw2_all.reshape(D, D)
    w_out_full = w_out_all.reshape(2 * D, D)
    w_pg_full = w_pg_all.reshape(D, D)

    z, u = _inproj_fwd(x2, ln_g, w_in_all)
    yc, y2, y_conv = _conv_fwd(z, conv_w_all, conv_b, cnorm_g, cnorm_b, w_pw2_full, b_pw2)
    o_raw, y_hgrn, states = _hgrn_fwd(z, lb_logits, onorm_g)

    (d_h1, dy_conv, dy_hgrn, d_w_out, d_w_pg, d_w_pp, d_pen_p, d_fin_p, loss_p) = _tail(
        x2, y_conv, y_hgrn, p2, tg2, w_out_full, w_pg_full, w_pp_all, pe_norm_g, fin_g)

    dz_h, d_on_p, d_lb_p = _hgrn_bwd(dy_hgrn, z, o_raw, states, lb_logits, onorm_g)
    dz_c, d_w_pw2, d_conv_w, d_b2_p, d_cng_p, d_cnb_p, d_cb_p = _conv_bwd(
        dy_conv, z, yc, y2, conv_w_all, cnorm_g, cnorm_b, w_pw2_full)
    grad_x, d_ln_p = _inproj_bwd_dx(dz_c, dz_h, x2, d_h1, ln_g, w_in_all)
    d_w_in = _inproj_bwd_dw(u, jnp.concatenate([dz_c, dz_h], axis=1))

    small = _pack_small({R_LN: d_ln_p, R_CONVB: d_cb_p, R_CNG: d_cng_p, R_CNB: d_cnb_p,
                         R_BPW2: d_b2_p, R_LB0: d_lb_p, R_ON: d_on_p, R_PEN: d_pen_p,
                         R_FIN: d_fin_p, R_LOSS: loss_p})
    r_w_in, r_conv_w, r_w_pw2, r_w_out, r_w_pg, r_w_pp, small_all = _exchange_grads(
        [d_w_in, d_conv_w, d_w_pw2.reshape(N_DEV, D // N_DEV, D),
         d_w_out.reshape(N_DEV, 2 * D // N_DEV, D), d_w_pg.reshape(N_DEV, D // N_DEV, D), d_w_pp],
        small)

    big = {}
    big["w_in"] = _sum_adam("adam_w_in", r_w_in, w_in[0], m_w_in[0], v_w_in[0], 128)
    cw = _sum_adam("adam_conv_w", r_conv_w, _pad_rows(conv_w[0], CONV_PAD),
                   _pad_rows(m_conv_w[0], CONV_PAD), _pad_rows(v_conv_w[0], CONV_PAD), CONV_PAD)
    big["conv_w"] = [a[:CONV_K] for a in cw]
    big["w_pw2"] = _sum_adam("adam_w_pw2", r_w_pw2, w_pw2[0], m_w_pw2[0], v_w_pw2[0], 128)
    big["w_out"] = _sum_adam("adam_w_out", r_w_out, w_out[0], m_w_out[0], v_w_out[0], 128)
    big["w_pg"] = _sum_adam("adam_w_pg", r_w_pg, w_pg[0], m_w_pg[0], v_w_pg[0], 128)
    big["w_pp"] = _sum_adam("adam_w_pp", r_w_pp, w_pp[0], m_w_pp[0], v_w_pp[0], PLE)

    small_w = [ln_g, conv_b, cnorm_g, cnorm_b, b_pw2, lb_logits, onorm_g, pe_norm_g, final_g]
    small_m = [m_ln_g, m_conv_b, m_cnorm_g, m_cnorm_b, m_b_pw2, m_lb_logits, m_onorm_g,
               m_pe_norm_g, m_final_g]
    small_v = [v_ln_g, v_conv_b, v_cnorm_g, v_cnorm_b, v_b_pw2, v_lb_logits, v_onorm_g,
               v_pe_norm_g, v_final_g]
    sg, sd, sm, sv, loss = _small_adam(small_all, lb_logits, _pack_rows(small_w),
                                       _pack_rows(small_m), _pack_rows(small_v))

    small_rows = {"ln_g": (R_LN, 1), "conv_b": (R_CONVB, 1), "cnorm_g": (R_CNG, 1),
                  "cnorm_b": (R_CNB, 1), "b_pw2": (R_BPW2, 1), "lb_logits": (R_LB0, 2),
                  "onorm_g": (R_ON, 1), "pe_norm_g": (R_PEN, 1), "final_g": (R_FIN, 1)}
    order = ["ln_g", "w_in", "conv_w", "conv_b", "cnorm_g", "cnorm_b", "w_pw2", "b_pw2",
             "lb_logits", "onorm_g", "w_out", "pe_norm_g", "w_pg", "w_pp", "final_g"]

    def leaf(kind, name):
        if name in big:
            return big[name][kind][None]
        r0, n = small_rows[name]
        a = (sg, sd, sm, sv)[kind][r0:r0 + n]
        return a.reshape(D) if name == "final_g" else a

    outs = [loss.reshape(()), grad_x.reshape(1, t, D)]
    for kind in range(4):
        outs += [leaf(kind, name) for name in order]
    return tuple(outs)
```

```python
import functools

import jax
import jax.numpy as jnp
from jax import lax
from jax.experimental import pallas as pl
from jax.experimental.pallas import tpu as pltpu

F32 = jnp.float32
BF16 = jnp.bfloat16
MESH = pl.DeviceIdType.MESH

N_DEV = 8
D = 1024
N_COLS = 7 * D
COLS_PER_DEV = N_COLS // N_DEV
PLE = 256
HEAD = 128
N_HEADS = D // HEAD
CONV_K = 31
CONV_PAD = 32
CHUNK = 64
EPS = 1e-6
SUBLANES = 8

ADAM_LR = 0.001
ADAM_B1 = 0.9
ADAM_B2 = 0.999
ADAM_EPS = 1e-08
ADAM_WD = 0.01
ADAM_STEP = 10

MIB = 1024 * 1024
N_SMALL = 16
R_LN, R_CONVB, R_CNG, R_CNB, R_BPW2, R_LB0, R_LB1, R_ON, R_PEN, R_FIN, R_LOSS = range(11)


def _params(vmem_mib, **kw):
    return pltpu.CompilerParams(vmem_limit_bytes=vmem_mib * MIB, **kw)


def _dot(a, b):
    return jnp.dot(a.astype(BF16), b.astype(BF16), preferred_element_type=F32)


def _dot_nt(a, b):
    return lax.dot_general(a.astype(BF16), b.astype(BF16), (((1,), (1,)), ((), ())),
                           preferred_element_type=F32)


def _dot_tn(a, b):
    return lax.dot_general(a.astype(BF16), b.astype(BF16), (((0,), (0,)), ((), ())),
                           preferred_element_type=F32)


def _split(a):
    hi = a.astype(BF16)
    return hi, (a - hi.astype(F32)).astype(BF16)


def _dot_split(a, b, dims):
    ah, al = _split(a)
    bh, bl = _split(b)
    dg = lambda p, q: lax.dot_general(p, q, dims, preferred_element_type=F32)
    return dg(ah, bh) + (dg(ah, bl) + dg(al, bh))


def _sigmoid(x):
    return 1.0 / (1.0 + jnp.exp(-x))


def _rowsum8(a):
    r, c = a.shape
    return jnp.sum(a.reshape(r // SUBLANES, SUBLANES, c), axis=0)


def _tri_dot(tri, a):
    hi = a.astype(BF16)
    r1 = a - hi.astype(F32)
    mid = r1.astype(BF16)
    lo = (r1 - mid.astype(F32)).astype(BF16)
    return (jnp.dot(tri, hi, preferred_element_type=F32)
            + jnp.dot(tri, mid, preferred_element_type=F32)
            + jnp.dot(tri, lo, preferred_element_type=F32))


def _lower_bound(lbl):
    l0, l1 = lbl[0:1, :], lbl[1:2, :]
    m = jnp.maximum(l0, l1)
    e0, e1 = jnp.exp(l0 - m), jnp.exp(l1 - m)
    s = e0 + e1
    return e0 / s, e1 / s


ANY = pl.BlockSpec(memory_space=pl.ANY)


def _full(shape):
    return pl.BlockSpec(shape, lambda i: (0,) * len(shape))


def _peer(x, y, c, k):
    px = 1 - x if k & 4 else x
    py = 1 - y if k & 2 else y
    pc = 1 - c if k & 1 else c
    return (px, py, pc), 4 * px + 2 * py + pc


class _Exchange:
    def __init__(self, srcs, outs, modes, send_sems, recv_sems, local_sems):
        x, y, c = lax.axis_index("x"), lax.axis_index("y"), lax.axis_index("c")
        me = 4 * x + 2 * y + c
        self.starts, self.send_waits, self.recv_waits = [], [], []

        def remote(a, k, src, peer, when):
            sem = a * N_DEV + k
            cp = pltpu.make_async_remote_copy(
                src_ref=src, dst_ref=outs[a].at[me], send_sem=send_sems.at[sem],
                recv_sem=recv_sems.at[sem], device_id=peer, device_id_type=MESH)
            self.starts.append((when, cp.start))
            self.send_waits.append((when, cp.wait_send))

        def arrival(a, k, when):
            sem = a * N_DEV + k
            _, sender = _peer(x, y, c, k)
            cp = pltpu.make_async_remote_copy(
                src_ref=outs[a].at[sender], dst_ref=outs[a].at[sender], send_sem=send_sems.at[sem],
                recv_sem=recv_sems.at[sem], device_id=(x, y, c), device_id_type=MESH)
            self.recv_waits.append((when, cp.wait_recv))

        def local(a, src, when):
            cp = pltpu.make_async_copy(src, outs[a].at[me], local_sems.at[a])
            self.starts.append((when, cp.start))
            self.send_waits.append((when, cp.wait))

        for a, (src, mode) in enumerate(zip(srcs, modes)):
            if mode in ("gather", "scatter"):
                local(a, src if mode == "gather" else src.at[me], None)
                for k in range(1, N_DEV):
                    peer, peer_idx = _peer(x, y, c, k)
                    remote(a, k, src if mode == "gather" else src.at[peer_idx], peer, None)
                    arrival(a, k, None)
                continue
            here, away = x == mode, x != mode
            for kk in range(4):
                py = 1 - y if kk & 2 else y
                pc = 1 - c if kk & 1 else c
                block = src.at[2 * py + pc]
                if kk == 0:
                    local(a, block, here)
                else:
                    remote(a, kk, block, (x, py, pc), here)
                remote(a, 4 + kk, block, (1 - x, py, pc), away)
            for k in range(1, N_DEV):
                arrival(a, k, here)

    @staticmethod
    def _run(actions):
        for when, fn in actions:
            if when is None:
                fn()
            else:
                pl.when(when)(fn)

    def start(self):
        self._run(self.starts)

    def wait(self):
        self._run(self.recv_waits)
        self._run(self.send_waits)


def _exchange_scratch(n):
    return [pltpu.SemaphoreType.DMA((n * N_DEV,)), pltpu.SemaphoreType.DMA((n * N_DEV,)),
            pltpu.SemaphoreType.DMA((n,))]


def _recv_shapes(srcs, modes):
    return [jax.ShapeDtypeStruct((N_DEV,) + (s.shape if m == "gather" else s.shape[1:]), s.dtype)
            for s, m in zip(srcs, modes)]


def _exchange_call(name, srcs, modes):
    n = len(srcs)

    def body(*refs):
        xch = _Exchange(refs[:n], refs[n:2 * n], modes, *refs[2 * n:])
        xch.start()
        xch.wait()

    return pl.pallas_call(
        body, name=name, out_shape=_recv_shapes(srcs, modes),
        in_specs=[ANY] * n, out_specs=[ANY] * n, scratch_shapes=_exchange_scratch(n),
    )(*srcs)


def _hosted(body, n_in, n_out, n_steps, modes):
    n = len(modes)

    def hosted(*refs):
        ins, srcs = refs[:n_in], refs[n_in:n_in + n]
        outs = refs[n_in + n:n_in + n + n_out]
        bufs = refs[n_in + n + n_out:n_in + 2 * n + n_out]
        scratch = refs[n_in + 2 * n + n_out:-3]
        xch = _Exchange(srcs, bufs, modes, *refs[-3:])
        pl.when(pl.program_id(0) == 0)(xch.start)
        body(*ins, *outs, *scratch)
        pl.when(pl.program_id(0) == n_steps - 1)(xch.wait)

    return hosted


def _inproj_fwd(x, ln_g, w_in_all, other_shards):
    t = x.shape[0]
    tt = min(256, t)
    modes = ["gather"] * len(other_shards)

    def body(x_ref, g_ref, w_hbm, z_ref, u_ref, w_vmem):
        @pl.when(pl.program_id(0) == 0)
        def _():
            pltpu.sync_copy(w_hbm, w_vmem)

        xv = x_ref[...]
        rstd = lax.rsqrt(jnp.mean(xv * xv, axis=-1, keepdims=True) + EPS)
        ub = (xv * rstd * g_ref[...]).astype(BF16)
        u_ref[...] = ub
        for d in range(N_DEV):
            z_ref[:, COLS_PER_DEV * d:COLS_PER_DEV * (d + 1)] = jnp.dot(
                ub, w_vmem[d], preferred_element_type=F32)

    n = len(modes)
    return pl.pallas_call(
        _hosted(body, 3, 2, t // tt, modes), name="inproj_fwd", grid=(t // tt,),
        out_shape=[jax.ShapeDtypeStruct((t, N_COLS), F32), jax.ShapeDtypeStruct((t, D), BF16)]
        + _recv_shapes(other_shards, modes),
        in_specs=[pl.BlockSpec((tt, D), lambda i: (i, 0)), _full((1, D)), ANY] + [ANY] * n,
        out_specs=[pl.BlockSpec((tt, N_COLS), lambda i: (i, 0)),
                   pl.BlockSpec((tt, D), lambda i: (i, 0))] + [ANY] * n,
        scratch_shapes=[pltpu.VMEM((N_DEV, D, COLS_PER_DEV), BF16)] + _exchange_scratch(n),
        compiler_params=_params(48, dimension_semantics=("arbitrary",)),
    )(x, ln_g, w_in_all, *other_shards)


def _shifted_copies(buf, shifted, rows):
    for b in range(1, SUBLANES):
        shifted[b, 0:rows, :] = buf[b:b + rows, :]


def _tap_ref(buf, shifted, offset):
    a, b = divmod(offset, SUBLANES)
    return (buf if b == 0 else shifted.at[b]), SUBLANES * a


def _group_norm_stats(blk):
    mu = jnp.mean(blk, axis=-1, keepdims=True)
    cen = blk - mu
    var = jnp.mean(cen * cen, axis=-1, keepdims=True)
    return cen * lax.rsqrt(var + EPS)


def _conv_fwd(z, conv_w_all, conv_b, cn_g, cn_b, w_pw2, b_pw2):
    t = z.shape[0]
    tt = min(256, t)
    rc = 32

    def body(val_ref, glu_ref, gate_ref, cw_ref, cb_ref, g_ref, b_ref, w_hbm, b2_ref,
             yc_ref, y2_ref, yo_ref, w_vmem, vbuf, vsh, y1buf):
        @pl.when(pl.program_id(0) == 0)
        def _():
            pltpu.sync_copy(w_hbm, w_vmem)
            vbuf[0:CONV_PAD, :] = jnp.zeros((CONV_PAD, D), F32)

        vbuf[CONV_PAD:CONV_PAD + tt, :] = val_ref[...] * _sigmoid(glu_ref[...])
        _shifted_copies(vbuf, vsh, tt + 24)

        def row_chunk(r, carry):
            r0 = pl.multiple_of(r * rc, rc)
            for g in range(N_HEADS):
                cs = slice(HEAD * g, HEAD * (g + 1))
                acc = jnp.zeros((rc, HEAD), F32)
                for k in range(CONV_K):
                    ref, off = _tap_ref(vbuf, vsh, k + 2)
                    acc = acc + cw_ref[g, k:k + 1, :] * ref[pl.ds(r0 + off, rc), cs]
                acc = acc + cb_ref[:, cs]
                yc_ref[pl.ds(r0, rc), cs] = acc
                n = _group_norm_stats(acc) * g_ref[:, cs] + b_ref[:, cs]
                y1buf[pl.ds(r0, rc), cs] = (n * _sigmoid(n)).astype(BF16)
            return carry

        lax.fori_loop(0, tt // rc, row_chunk, 0)
        vbuf[0:CONV_PAD, :] = vbuf[tt:tt + CONV_PAD, :]
        y2 = jnp.dot(y1buf[...], w_vmem[...], preferred_element_type=F32) + b2_ref[...]
        y2_ref[...] = y2
        gate = gate_ref[...]
        yo_ref[...] = (y2 * gate * _sigmoid(gate)).astype(BF16)

    col = lambda j: pl.BlockSpec((tt, D), lambda i: (i, j))
    row = pl.BlockSpec((tt, D), lambda i: (i, 0))
    return pl.pallas_call(
        body, name="conv_fwd", grid=(t // tt,),
        out_shape=[jax.ShapeDtypeStruct((t, D), F32), jax.ShapeDtypeStruct((t, D), F32),
                   jax.ShapeDtypeStruct((t, D), BF16)],
        in_specs=[col(0), col(1), col(2), _full((N_DEV, CONV_PAD, HEAD)), _full((1, D)),
                  _full((1, D)), _full((1, D)), ANY, _full((1, D))],
        out_specs=[row, row, row],
        scratch_shapes=[pltpu.VMEM((D, D), BF16), pltpu.VMEM((tt + CONV_PAD, D), F32),
                        pltpu.VMEM((SUBLANES, tt + CONV_PAD, D), F32), pltpu.VMEM((tt, D), BF16)],
        compiler_params=_params(48, dimension_semantics=("arbitrary",)),
    )(z, z, z, conv_w_all, conv_b, cn_g, cn_b, w_pw2, b_pw2)


def _chunk_quantities(zq, zf, lbh, tri):
    sig = _sigmoid(zf)
    sig_neg = _sigmoid(-zf)
    f = lbh + (1.0 - lbh) * sig
    k = (1.0 - lbh) * sig_neg
    q = zq * _sigmoid(zq)
    b = _tri_dot(tri, jnp.log(f))
    b_mid = b[CHUNK // 2 - 1:CHUNK // 2, :]
    b_last = b[CHUNK - 1:CHUNK, :]
    e_q = jnp.exp(b)
    e_qm = jnp.exp(b - b_mid)
    e_km = jnp.exp(b_mid - b)
    e_kd = jnp.exp(b_last - b)
    return q, k, f, sig, sig_neg, e_q, e_qm, e_km, e_kd, jnp.exp(b_last)


def _hgrn_fwd(z, lb_logits, onorm_g):
    t = z.shape[0]
    tt = min(256, t)
    nc = tt // CHUNK

    def body(q_ref, f_ref, i_ref, g_ref, lbl_ref, on_ref, o_ref, y_ref, s_ref, st):
        @pl.when(pl.program_id(0) == 0)
        def _():
            st[...] = jnp.zeros_like(st)

        lb, _ = _lower_bound(lbl_ref[...])
        rows = lax.broadcasted_iota(jnp.int32, (CHUNK, CHUNK), 0)
        cols = lax.broadcasted_iota(jnp.int32, (CHUNK, CHUNK), 1)
        causal = rows >= cols
        tri = causal.astype(BF16)

        def chunk(c, carry):
            r0 = pl.multiple_of(c * CHUNK, CHUNK)
            rs = pl.ds(r0, CHUNK)
            for h in range(N_HEADS):
                cs = slice(HEAD * h, HEAD * (h + 1))
                q, k, _, _, _, e_q, e_qm, e_km, e_kd, e_last = _chunk_quantities(
                    q_ref[rs, cs], f_ref[rs, cs], lb[:, cs], tri)
                v = i_ref[rs, cs]
                s_old = st[h]
                s_ref[c, h] = s_old
                a = jnp.where(causal, _dot_nt(q * e_qm, k * e_km), 0.0)
                o = _dot_nt(q * e_q, s_old) + _dot(a, v)
                st[h] = s_old * e_last + _dot_tn(v, k * e_kd)
                o_ref[rs, cs] = o
                n = o * lax.rsqrt(jnp.mean(o * o, axis=-1, keepdims=True) + EPS)
                zg = g_ref[rs, cs]
                y_ref[rs, cs] = (n * on_ref[:, cs] * zg * _sigmoid(zg)).astype(BF16)
            return carry

        lax.fori_loop(0, nc, chunk, 0)

    col = lambda j: pl.BlockSpec((tt, D), lambda i: (i, j))
    row = pl.BlockSpec((tt, D), lambda i: (i, 0))
    return pl.pallas_call(
        body, name="hgrn_fwd", grid=(t // tt,),
        out_shape=[jax.ShapeDtypeStruct((t, D), F32), jax.ShapeDtypeStruct((t, D), BF16),
                   jax.ShapeDtypeStruct((t // CHUNK, N_HEADS, HEAD, HEAD), F32)],
        in_specs=[col(3), col(4), col(5), col(6), _full((2, D)), _full((1, D))],
        out_specs=[row, row, pl.BlockSpec((nc, N_HEADS, HEAD, HEAD), lambda i: (i, 0, 0, 0))],
        scratch_shapes=[pltpu.VMEM((N_HEADS, HEAD, HEAD), F32)],
        compiler_params=_params(40, dimension_semantics=("arbitrary",)),
    )(z, z, z, z, lb_logits, onorm_g)


def _rms_bwd(dn, xhat, rstd):
    return rstd * (dn - xhat * jnp.mean(dn * xhat, axis=-1, keepdims=True))


def _tail(x, y_conv, y_hgrn, p, target, w_out, w_pg, w_pp_all, pe_g, fin_g):
    t = x.shape[0]
    tt = min(256, t)
    n_steps = t // tt

    def body(x_ref, yc_ref, yh_ref, p_ref, tg_ref, wo_hbm, wg_hbm, wp_hbm, pg_ref, fg_ref,
             dh1_ref, dyc_ref, dyh_ref, dwo_hbm, dwg_hbm, dwp_hbm, dpg_ref, dfg_ref, loss_ref,
             wo, wg, wp, dwo, dwg, dwp):
        i = pl.program_id(0)

        @pl.when(i == 0)
        def _():
            pltpu.sync_copy(wo_hbm, wo)
            pltpu.sync_copy(wg_hbm, wg)
            for d in range(N_DEV):
                pltpu.sync_copy(wp_hbm.at[d], wp.at[:, pl.ds(HEAD * d, HEAD)])
            dwo[...] = jnp.zeros_like(dwo)
            dwg[...] = jnp.zeros_like(dwg)
            dwp[...] = jnp.zeros_like(dwp)
            dpg_ref[...] = jnp.zeros_like(dpg_ref)
            dfg_ref[...] = jnp.zeros_like(dfg_ref)
            loss_ref[...] = jnp.zeros_like(loss_ref)

        ycv, yhv = yc_ref[...], yh_ref[...]
        h1 = (x_ref[...] + jnp.dot(ycv, wo[0:D, :], preferred_element_type=F32)
              + jnp.dot(yhv, wo[D:2 * D, :], preferred_element_type=F32))
        pb = p_ref[...].astype(BF16)
        pe = jnp.dot(pb, wp[...], preferred_element_type=F32)
        rstd1 = lax.rsqrt(jnp.mean(h1 * h1, axis=-1, keepdims=True) + EPS)
        n1 = h1 * rstd1
        rb = (n1 * pg_ref[...]).astype(BF16)
        gate = _sigmoid(jnp.dot(rb, wg[...], preferred_element_type=F32))
        h2 = h1 + gate * pe
        rstd2 = lax.rsqrt(jnp.mean(h2 * h2, axis=-1, keepdims=True) + EPS)
        n2 = h2 * rstd2
        err = n2 * fg_ref[...] - tg_ref[...]
        loss_ref[...] += _rowsum8(err * err)

        d_out = err * (1.0 / D)
        dfg_ref[...] += _rowsum8(d_out * n2)
        d_h2 = _rms_bwd(d_out * fg_ref[...], n2, rstd2)
        d_pe = (d_h2 * gate).astype(BF16)
        d_gpre = (d_h2 * pe * gate * (1.0 - gate)).astype(BF16)
        dwg[...] += _dot_tn(rb, d_gpre)
        dwp[...] += _dot_tn(pb, d_pe)
        dr = _dot_nt(d_gpre, wg[...])
        dpg_ref[...] += _rowsum8(dr * n1)
        d_h1 = d_h2 + _rms_bwd(dr * pg_ref[...], n1, rstd1)
        dh1_ref[...] = d_h1
        d_h1b = d_h1.astype(BF16)
        dwo[0:D, :] += _dot_tn(ycv, d_h1b)
        dwo[D:2 * D, :] += _dot_tn(yhv, d_h1b)
        dyc_ref[...] = _dot_nt(d_h1b, wo[0:D, :])
        dyh_ref[...] = _dot_nt(d_h1b, wo[D:2 * D, :])

        @pl.when(i == n_steps - 1)
        def _():
            pltpu.sync_copy(dwo, dwo_hbm)
            pltpu.sync_copy(dwg, dwg_hbm)
            for d in range(N_DEV):
                pltpu.sync_copy(dwp.at[:, pl.ds(HEAD * d, HEAD)], dwp_hbm.at[d])

    row = pl.BlockSpec((tt, D), lambda i: (i, 0))
    acc = _full((SUBLANES, D))
    return pl.pallas_call(
        body, name="tail_fwd_bwd", grid=(n_steps,),
        out_shape=[jax.ShapeDtypeStruct((t, D), F32)] * 3
        + [jax.ShapeDtypeStruct((2 * D, D), F32), jax.ShapeDtypeStruct((D, D), F32),
           jax.ShapeDtypeStruct((N_DEV, PLE, HEAD), F32)]
        + [jax.ShapeDtypeStruct((SUBLANES, D), F32)] * 3,
        in_specs=[row, row, row, pl.BlockSpec((tt, PLE), lambda i: (i, 0)), row,
                  ANY, ANY, ANY, _full((1, D)), _full((1, D))],
        out_specs=[row, row, row, ANY, ANY, ANY, acc, acc, acc],
        scratch_shapes=[pltpu.VMEM((2 * D, D), BF16), pltpu.VMEM((D, D), BF16),
                        pltpu.VMEM((PLE, D), BF16), pltpu.VMEM((2 * D, D), F32),
                        pltpu.VMEM((D, D), F32), pltpu.VMEM((PLE, D), F32)],
        compiler_params=_params(52, dimension_semantics=("arbitrary",)),
    )(x, y_conv, y_hgrn, p, target, w_out, w_pg, w_pp_all, pe_g, fin_g)


def _hgrn_bwd(dy, z, o_raw, states, lb_logits, onorm_g, grads):
    t = z.shape[0]
    tt = min(256, t)
    nc = tt // CHUNK
    n_steps = t // tt
    modes = ["scatter"] * len(grads)

    def body(dy_ref, q_ref, f_ref, i_ref, g_ref, o_ref, s_ref, lbl_ref, on_ref,
             dz_ref, don_ref, dlb_ref, dst):
        @pl.when(pl.program_id(0) == 0)
        def _():
            dst[...] = jnp.zeros_like(dst)
            don_ref[...] = jnp.zeros_like(don_ref)
            dlb_ref[...] = jnp.zeros_like(dlb_ref)

        lb, _ = _lower_bound(lbl_ref[...])
        rows = lax.broadcasted_iota(jnp.int32, (CHUNK, CHUNK), 0)
        cols = lax.broadcasted_iota(jnp.int32, (CHUNK, CHUNK), 1)
        causal = rows >= cols
        tri = causal.astype(BF16)
        tri_rev = (rows <= cols).astype(BF16)
        is_last = lax.broadcasted_iota(jnp.int32, (CHUNK, HEAD), 0) == CHUNK - 1

        def chunk(cc, carry):
            c = nc - 1 - cc
            r0 = pl.multiple_of(c * CHUNK, CHUNK)
            rs = pl.ds(r0, CHUNK)
            for h in range(N_HEADS):
                cs = slice(HEAD * h, HEAD * (h + 1))
                zq, zf, zg = q_ref[rs, cs], f_ref[rs, cs], g_ref[rs, cs]
                lbh = lb[:, cs]
                q, k, f, sig, sig_neg, e_q, e_qm, e_km, e_kd, e_last = _chunk_quantities(
                    zq, zf, lbh, tri)
                v = i_ref[rs, cs]
                qt, qm, km, kd = q * e_q, q * e_qm, k * e_km, k * e_kd
                a = jnp.where(causal, _dot_nt(qm, km), 0.0)

                o = o_ref[rs, cs]
                rstd = lax.rsqrt(jnp.mean(o * o, axis=-1, keepdims=True) + EPS)
                n = o * rstd
                sg = _sigmoid(zg)
                dyv = dy_ref[rs, cs]
                on = on_ref[:, cs]
                d_zg = dyv * n * on * sg * (1.0 + zg * (1.0 - sg))
                d_on = dyv * zg * sg
                don_ref[:, cs] += _rowsum8(d_on * n)
                do = _rms_bwd(d_on * on, n, rstd)

                s_old = s_ref[c, h]
                ds_new = dst[h]
                da = jnp.where(causal, _dot_nt(do, v), 0.0)
                dv = _dot_tn(a, do) + _dot_nt(kd, ds_new)
                dkd = _dot(v, ds_new)
                dqm = _dot_split(da, km, (((1,), (0,)), ((), ())))
                dkm = _dot_split(da, qm, (((0,), (0,)), ((), ())))
                dq = _dot(do, s_old) * e_q + dqm * e_qm
                dk = dkm * e_km + dkd * e_kd
                dst[h] = ds_new * e_last + _dot_tn(do, qt)
                last = (jnp.sum(dkd * kd, axis=0, keepdims=True)
                        + e_last * jnp.sum(s_old * ds_new, axis=0, keepdims=True))
                db = q * dq - k * dk + jnp.where(is_last, last, 0.0)
                dlogf = _tri_dot(tri_rev, db)
                common = sig_neg * (dlogf / f - dk)
                dlb_ref[:, cs] += _rowsum8(common)
                c0 = 3 * D + HEAD * h
                dz_ref[rs, c0:c0 + HEAD] = (
                    dq * _sigmoid(zq) * (1.0 + zq * (1.0 - _sigmoid(zq)))).astype(BF16)
                dz_ref[rs, D + c0:D + c0 + HEAD] = ((1.0 - lbh) * sig * common).astype(BF16)
                dz_ref[rs, 2 * D + c0:2 * D + c0 + HEAD] = dv.astype(BF16)
                dz_ref[rs, 3 * D + c0:3 * D + c0 + HEAD] = d_zg.astype(BF16)
            return carry

        lax.fori_loop(0, nc, chunk, 0)

    rev = lambda i: n_steps - 1 - i
    col = lambda j: pl.BlockSpec((tt, D), lambda i: (rev(i), j))
    row = pl.BlockSpec((tt, D), lambda i: (rev(i), 0))
    acc = _full((SUBLANES, D))
    n = len(modes)
    return pl.pallas_call(
        _hosted(body, 9, 3, n_steps, modes), name="hgrn_bwd", grid=(n_steps,),
        out_shape=[jax.ShapeDtypeStruct((t, N_COLS), BF16),
                   jax.ShapeDtypeStruct((SUBLANES, D), F32),
                   jax.ShapeDtypeStruct((SUBLANES, D), F32)] + _recv_shapes(grads, modes),
        in_specs=[row, col(3), col(4), col(5), col(6), row,
                  pl.BlockSpec((nc, N_HEADS, HEAD, HEAD), lambda i: (rev(i), 0, 0, 0)),
                  _full((2, D)), _full((1, D))] + [ANY] * n,
        out_specs=[pl.BlockSpec((tt, N_COLS), lambda i: (rev(i), 0)), acc, acc] + [ANY] * n,
        scratch_shapes=[pltpu.VMEM((N_HEADS, HEAD, HEAD), F32)] + _exchange_scratch(n),
        compiler_params=_params(48, dimension_semantics=("arbitrary",)),
    )(dy, z, z, z, z, o_raw, states, lb_logits, onorm_g, *grads)


def _conv_bwd(dy, z, yc, y2, conv_w_all, cn_g, cn_b, w_pw2, dz, grads, modes):
    t = z.shape[0]
    tt = min(256, t)
    rc = 32
    n_steps = t // tt

    def body(dy_ref, val_ref, glu_ref, gate_ref, yc_ref, y2_ref, cw_ref, g_ref, b_ref, w_hbm,
             dz_in, dz_ref, dw_hbm, dcw_out, db2_ref, dg_ref, dbeta_ref, dcb_ref,
             w_vmem, dw, dbuf, dsh, y1buf, dnbuf, dcw_ref):
        i = pl.program_id(0)

        @pl.when(i == 0)
        def _():
            pltpu.sync_copy(w_hbm, w_vmem)
            dw[...] = jnp.zeros_like(dw)
            dbuf[tt:tt + CONV_PAD, :] = jnp.zeros((CONV_PAD, D), F32)
            dcw_ref[...] = jnp.zeros_like(dcw_ref)
            dcw_out[...] = jnp.zeros_like(dcw_out)
            db2_ref[...] = jnp.zeros_like(db2_ref)
            dg_ref[...] = jnp.zeros_like(dg_ref)
            dbeta_ref[...] = jnp.zeros_like(dbeta_ref)
            dcb_ref[...] = jnp.zeros_like(dcb_ref)

        gate = gate_ref[...]
        sg = _sigmoid(gate)
        dyv = dy_ref[...]
        dy2 = dyv * gate * sg
        dz_ref[:, 2 * D:3 * D] = (dyv * y2_ref[...] * sg * (1.0 + gate * (1.0 - sg))).astype(BF16)
        db2_ref[...] += _rowsum8(dy2)
        dy2b = dy2.astype(BF16)
        dnbuf[...] = _dot_nt(dy2b, w_vmem[...])

        def norm_chunk(r, carry):
            r0 = pl.multiple_of(r * rc, rc)
            rs = pl.ds(r0, rc)
            for g in range(N_HEADS):
                cs = slice(HEAD * g, HEAD * (g + 1))
                blk = yc_ref[rs, cs]
                mu = jnp.mean(blk, axis=-1, keepdims=True)
                cen = blk - mu
                rstd = lax.rsqrt(jnp.mean(cen * cen, axis=-1, keepdims=True) + EPS)
                xhat = cen * rstd
                n = xhat * g_ref[:, cs] + b_ref[:, cs]
                sn = _sigmoid(n)
                y1buf[rs, cs] = (n * sn).astype(BF16)
                dn = dnbuf[rs, cs] * sn * (1.0 + n * (1.0 - sn))
                dg_ref[:, cs] += _rowsum8(dn * xhat)
                dbeta_ref[:, cs] += _rowsum8(dn)
                dxh = dn * g_ref[:, cs]
                dyc = rstd * (dxh - jnp.mean(dxh, axis=-1, keepdims=True)
                              - xhat * jnp.mean(dxh * xhat, axis=-1, keepdims=True))
                dcb_ref[:, cs] += _rowsum8(dyc)
                dbuf[rs, cs] = dyc
            return carry

        lax.fori_loop(0, tt // rc, norm_chunk, 0)
        dw[...] += _dot_tn(y1buf[...], dy2b)
        _shifted_copies(dbuf, dsh, tt + 24)

        def conv_chunk(r, carry):
            r0 = pl.multiple_of(r * rc, rc)
            rs = pl.ds(r0, rc)
            for g in range(N_HEADS):
                cs = slice(HEAD * g, HEAD * (g + 1))
                sglu = _sigmoid(glu_ref[rs, cs])
                val = val_ref[rs, cs]
                v = val * sglu
                dv = jnp.zeros((rc, HEAD), F32)
                for k in range(CONV_K):
                    ref, off = _tap_ref(dbuf, dsh, CONV_K - 1 - k)
                    d_later = ref[pl.ds(r0 + off, rc), cs]
                    dv = dv + cw_ref[g, k:k + 1, :] * d_later
                    dcw_ref[g, k] += _rowsum8(v * d_later)
                dz_ref[rs, cs] = (dv * sglu).astype(BF16)
                dz_ref[rs, D + HEAD * g:D + HEAD * (g + 1)] = (
                    dv * val * sglu * (1.0 - sglu)).astype(BF16)
            return carry

        lax.fori_loop(0, tt // rc, conv_chunk, 0)
        dbuf[tt:tt + CONV_PAD, :] = dbuf[0:CONV_PAD, :]

        @pl.when(i == n_steps - 1)
        def _():
            pltpu.sync_copy(dw, dw_hbm)
            for g in range(N_HEADS):
                for k in range(CONV_K):
                    dcw_out[g, k:k + 1, :] = jnp.sum(dcw_ref[g, k], axis=0, keepdims=True)

    rev = lambda i: n_steps - 1 - i
    col = lambda j: pl.BlockSpec((tt, D), lambda i: (rev(i), j))
    row = pl.BlockSpec((tt, D), lambda i: (rev(i), 0))
    acc = _full((SUBLANES, D))
    n = len(modes)
    return pl.pallas_call(
        _hosted(body, 11, 7, n_steps, modes), name="conv_bwd", grid=(n_steps,),
        out_shape=[jax.ShapeDtypeStruct((t, N_COLS), BF16), jax.ShapeDtypeStruct((D, D), F32),
                   jax.ShapeDtypeStruct((N_DEV, CONV_PAD, HEAD), F32)]
        + [jax.ShapeDtypeStruct((SUBLANES, D), F32)] * 4 + _recv_shapes(grads, modes),
        in_specs=[row, col(0), col(1), col(2), row, row, _full((N_DEV, CONV_PAD, HEAD)),
                  _full((1, D)), _full((1, D)), ANY, ANY] + [ANY] * n,
        out_specs=[pl.BlockSpec((tt, 3 * D), lambda i: (rev(i), 0)), ANY,
                   _full((N_DEV, CONV_PAD, HEAD)), acc, acc, acc, acc] + [ANY] * n,
        input_output_aliases={10: 0},
        scratch_shapes=[pltpu.VMEM((D, D), BF16), pltpu.VMEM((D, D), F32),
                        pltpu.VMEM((tt + CONV_PAD, D), F32),
                        pltpu.VMEM((SUBLANES, tt + CONV_PAD, D), F32),
                        pltpu.VMEM((tt, D), BF16), pltpu.VMEM((tt, D), F32),
                        pltpu.VMEM((N_DEV, CONV_PAD, SUBLANES, HEAD), F32)] + _exchange_scratch(n),
        compiler_params=_params(52, dimension_semantics=("arbitrary",)),
    )(dy, z, z, z, yc, y2, conv_w_all, cn_g, cn_b, w_pw2, dz, *grads)


def _inproj_bwd_dx(dz, x, d_h1, ln_g, w_in_all, grads, modes):
    t = x.shape[0]
    tt = min(256, t)

    def body(dz_ref, x_ref, dh1_ref, g_ref, w_hbm, dx_ref, dg_ref, w_vmem):
        @pl.when(pl.program_id(0) == 0)
        def _():
            pltpu.sync_copy(w_hbm, w_vmem)
            dg_ref[...] = jnp.zeros_like(dg_ref)

        du = jnp.zeros((tt, D), F32)
        for d in range(N_DEV):
            du = du + lax.dot_general(
                dz_ref[:, COLS_PER_DEV * d:COLS_PER_DEV * (d + 1)], w_vmem[d],
                (((1,), (1,)), ((), ())), preferred_element_type=F32)
        xv = x_ref[...]
        rstd = lax.rsqrt(jnp.mean(xv * xv, axis=-1, keepdims=True) + EPS)
        xhat = xv * rstd
        dg_ref[...] += _rowsum8(du * xhat)
        dx_ref[...] = dh1_ref[...] + _rms_bwd(du * g_ref[...], xhat, rstd)

    row = pl.BlockSpec((tt, D), lambda i: (i, 0))
    n = len(modes)
    return pl.pallas_call(
        _hosted(body, 5, 2, t // tt, modes), name="inproj_bwd_dx", grid=(t // tt,),
        out_shape=[jax.ShapeDtypeStruct((t, D), F32), jax.ShapeDtypeStruct((SUBLANES, D), F32)]
        + _recv_shapes(grads, modes),
        in_specs=[pl.BlockSpec((tt, N_COLS), lambda i: (i, 0)), row, row, _full((1, D)), ANY]
        + [ANY] * n,
        out_specs=[row, _full((SUBLANES, D))] + [ANY] * n,
        scratch_shapes=[pltpu.VMEM((N_DEV, D, COLS_PER_DEV), BF16)] + _exchange_scratch(n),
        compiler_params=_params(48, dimension_semantics=("arbitrary",)),
    )(dz, x, d_h1, ln_g, w_in_all, *grads)


def _inproj_bwd_dw(name, u, dz, first, count):
    t = u.shape[0]
    tt = min(512, t)

    def body(u_ref, dz_ref, dw_ref):
        @pl.when(pl.program_id(1) == 0)
        def _():
            dw_ref[...] = jnp.zeros_like(dw_ref)

        dw_ref[0] += lax.dot_general(u_ref[...], dz_ref[...], (((0,), (0,)), ((), ())),
                                     preferred_element_type=F32)

    return pl.pallas_call(
        body, name=name, grid=(count, t // tt),
        out_shape=jax.ShapeDtypeStruct((count, D, COLS_PER_DEV), F32),
        in_specs=[pl.BlockSpec((tt, D), lambda j, i: (i, 0)),
                  pl.BlockSpec((tt, COLS_PER_DEV), lambda j, i: (i, first + j))],
        out_specs=pl.BlockSpec((1, D, COLS_PER_DEV), lambda j, i: (j, 0, 0)),
        compiler_params=_params(32, dimension_semantics=("arbitrary", "arbitrary")),
    )(u, dz)


def _adamw(w, g, m, v):
    m = ADAM_B1 * m + (1.0 - ADAM_B1) * g
    v = ADAM_B2 * v + (1.0 - ADAM_B2) * (g * g)
    m_hat = m / (1.0 - ADAM_B1 ** ADAM_STEP)
    v_hat = v / (1.0 - ADAM_B2 ** ADAM_STEP)
    delta = -ADAM_LR * (m_hat / (jnp.sqrt(v_hat) + ADAM_EPS) + ADAM_WD * w)
    return delta, m, v


def _pack_small(partials):
    rows = sorted(partials)

    def body(*refs):
        ins, out_ref = refs[:-1], refs[-1]
        out_ref[...] = jnp.zeros_like(out_ref)
        for j, row in enumerate(rows):
            out_ref[row:row + 1, :] = jnp.sum(ins[j][...], axis=0, keepdims=True)

    return pl.pallas_call(
        body, name="pack_small", out_shape=jax.ShapeDtypeStruct((N_SMALL, D), F32),
    )(*[partials[row] for row in rows])


def _sum_adam(name, recvs, w, m, v, rows):
    r, c = w.shape
    n = len(recvs)

    def body(*refs):
        w_ref, m_ref, v_ref, g_ref, d_ref, mo_ref, vo_ref = refs[n:]

        def finish(recv_ref):
            g = recv_ref[0]
            for s in range(1, N_DEV):
                g = g + recv_ref[s]
            g_ref[...] = g
            d_ref[...], mo_ref[...], vo_ref[...] = _adamw(w_ref[...], g, m_ref[...], v_ref[...])

        if n == 1:
            finish(refs[0])
        else:
            for side in range(n):
                pl.when(lax.axis_index("x") == side)(functools.partial(finish, refs[side]))

    blk = pl.BlockSpec((rows, c), lambda i: (i, 0))
    return pl.pallas_call(
        body, name=name, grid=(r // rows,),
        out_shape=[jax.ShapeDtypeStruct((r, c), F32)] * 4,
        in_specs=[pl.BlockSpec((N_DEV, rows, c), lambda i: (0, i, 0))] * n + [blk, blk, blk],
        out_specs=[blk] * 4,
        compiler_params=_params(48, dimension_semantics=("arbitrary",)),
    )(*recvs, w, m, v)


def _small_adam(gathered, lb_logits, w, m, v):
    def body(ga_ref, lbl_ref, w_ref, m_ref, v_ref, g_ref, d_ref, mo_ref, vo_ref, loss_ref):
        g = ga_ref[0]
        for s in range(1, N_DEV):
            g = g + ga_ref[s]
        s0, s1 = _lower_bound(lbl_ref[...])
        d_lb = g[R_LB0:R_LB0 + 1, :]
        rows = lax.broadcasted_iota(jnp.int32, (N_SMALL, D), 0)
        g = jnp.where(rows == R_LB0, d_lb * s0 * (1.0 - s0), g)
        g = jnp.where(rows == R_LB1, -d_lb * s0 * s1, g)
        g_ref[...] = g
        d_ref[...], mo_ref[...], vo_ref[...] = _adamw(w_ref[...], g, m_ref[...], v_ref[...])
        loss_ref[...] = (0.5 / D) * jnp.sum(g[R_LOSS:R_LOSS + 1, :], axis=-1, keepdims=True)

    return pl.pallas_call(
        body, name="small_adam",
        out_shape=[jax.ShapeDtypeStruct((N_SMALL, D), F32)] * 4 + [jax.ShapeDtypeStruct((1, 1), F32)],
    )(gathered, lb_logits, w, m, v)


def _pad_rows(a, rows):
    return jnp.pad(a, ((0, rows - a.shape[0]), (0, 0)))


def _pack_rows(rows):
    rows = [r.reshape(-1, D) for r in rows]
    packed = jnp.concatenate(rows, axis=0)
    return _pad_rows(packed, N_SMALL)


def kernel(x, p, ln_g, w_in, conv_w, conv_b, cnorm_g, cnorm_b, w_pw2, b_pw2, lb_logits, onorm_g, w_out, pe_norm_g, w_pg, w_pp, final_g, loss_target, m_ln_g, m_w_in, m_conv_w, m_conv_b, m_cnorm_g, m_cnorm_b, m_w_pw2, m_b_pw2, m_lb_logits, m_onorm_g, m_w_out, m_pe_norm_g, m_w_pg, m_w_pp, m_final_g, v_ln_g, v_w_in, v_conv_w, v_conv_b, v_cnorm_g, v_cnorm_b, v_w_pw2, v_b_pw2, v_lb_logits, v_onorm_g, v_w_out, v_pe_norm_g, v_w_pg, v_w_pp, v_final_g):
    t = x.shape[1]
    x2 = x.reshape(t, D)
    p2 = p.reshape(t, PLE)
    tg2 = loss_target.reshape(t, D)
    fin_g = final_g.reshape(1, D)

    (w_in_all,) = _exchange_call("gather_w_in", [w_in[0].astype(BF16)], ["gather"])
    z, u, conv_w_all, w_pw2_all, w_out_all, w_pg_all, w_pp_all = _inproj_fwd(
        x2, ln_g, w_in_all,
        [_pad_rows(conv_w[0], CONV_PAD), w_pw2[0].astype(BF16), w_out[0].astype(BF16),
         w_pg[0].astype(BF16), w_pp[0].astype(BF16)])
    w_pw2_full = w_pw2_all.reshape(D, D)
    w_out_full = w_out_all.reshape(2 * D, D)
    w_pg_full = w_pg_all.reshape(D, D)

    yc, y2, y_conv = _conv_fwd(z, conv_w_all, conv_b, cnorm_g, cnorm_b, w_pw2_full, b_pw2)
    o_raw, y_hgrn, states = _hgrn_fwd(z, lb_logits, onorm_g)

    (d_h1, dy_conv, dy_hgrn, d_w_out, d_w_pg, d_w_pp, d_pen_p, d_fin_p, loss_p) = _tail(
        x2, y_conv, y_hgrn, p2, tg2, w_out_full, w_pg_full, w_pp_all, pe_norm_g, fin_g)

    dz, d_on_p, d_lb_p, r_w_out, r_w_pg, r_w_pp = _hgrn_bwd(
        dy_hgrn, z, o_raw, states, lb_logits, onorm_g,
        [d_w_out.reshape(N_DEV, 2 * D // N_DEV, D), d_w_pg.reshape(N_DEV, D // N_DEV, D), d_w_pp])
    d_w_in_hi = _inproj_bwd_dw("inproj_bwd_dw_hi", u, dz, N_DEV // 2, N_DEV // 2)
    dz, d_w_pw2, d_conv_w, d_b2_p, d_cng_p, d_cnb_p, d_cb_p, r_w_in_hi = _conv_bwd(
        dy_conv, z, yc, y2, conv_w_all, cnorm_g, cnorm_b, w_pw2_full, dz, [d_w_in_hi], [1])
    d_w_in_lo = _inproj_bwd_dw("inproj_bwd_dw_lo", u, dz, 0, N_DEV // 2)
    grad_x, d_ln_p, r_w_in_lo, r_w_pw2, r_conv_w = _inproj_bwd_dx(
        dz, x2, d_h1, ln_g, w_in_all,
        [d_w_in_lo, d_w_pw2.reshape(N_DEV, D // N_DEV, D), d_conv_w], [0, "scatter", "scatter"])

    small = _pack_small({R_LN: d_ln_p, R_CONVB: d_cb_p, R_CNG: d_cng_p, R_CNB: d_cnb_p,
                         R_BPW2: d_b2_p, R_LB0: d_lb_p, R_ON: d_on_p, R_PEN: d_pen_p,
                         R_FIN: d_fin_p, R_LOSS: loss_p})
    (small_all,) = _exchange_call("gather_small", [small], ["gather"])

    big = {}
    big["w_in"] = _sum_adam("adam_w_in", [r_w_in_lo, r_w_in_hi], w_in[0], m_w_in[0], v_w_in[0], 128)
    cw = _sum_adam("adam_conv_w", [r_conv_w], _pad_rows(conv_w[0], CONV_PAD),
                   _pad_rows(m_conv_w[0], CONV_PAD), _pad_rows(v_conv_w[0], CONV_PAD), CONV_PAD)
    big["conv_w"] = [a[:CONV_K] for a in cw]
    big["w_pw2"] = _sum_adam("adam_w_pw2", [r_w_pw2], w_pw2[0], m_w_pw2[0], v_w_pw2[0], 128)
    big["w_out"] = _sum_adam("adam_w_out", [r_w_out], w_out[0], m_w_out[0], v_w_out[0], 128)
    big["w_pg"] = _sum_adam("adam_w_pg", [r_w_pg], w_pg[0], m_w_pg[0], v_w_pg[0], 128)
    big["w_pp"] = _sum_adam("adam_w_pp", [r_w_pp], w_pp[0], m_w_pp[0], v_w_pp[0], PLE)

    small_w = [ln_g, conv_b, cnorm_g, cnorm_b, b_pw2, lb_logits, onorm_g, pe_norm_g, final_g]
    small_m = [m_ln_g, m_conv_b, m_cnorm_g, m_cnorm_b, m_b_pw2, m_lb_logits, m_onorm_g,
               m_pe_norm_g, m_final_g]
    small_v = [v_ln_g, v_conv_b, v_cnorm_g, v_cnorm_b, v_b_pw2, v_lb_logits, v_onorm_g,
               v_pe_norm_g, v_final_g]
    sg, sd, sm, sv, loss = _small_adam(small_all, lb_logits, _pack_rows(small_w),
                                       _pack_rows(small_m), _pack_rows(small_v))

    small_rows = {"ln_g": (R_LN, 1), "conv_b": (R_CONVB, 1), "cnorm_g": (R_CNG, 1),
                  "cnorm_b": (R_CNB, 1), "b_pw2": (R_BPW2, 1), "lb_logits": (R_LB0, 2),
                  "onorm_g": (R_ON, 1), "pe_norm_g": (R_PEN, 1), "final_g": (R_FIN, 1)}
    order = ["ln_g", "w_in", "conv_w", "conv_b", "cnorm_g", "cnorm_b", "w_pw2", "b_pw2",
             "lb_logits", "onorm_g", "w_out", "pe_norm_g", "w_pg", "w_pp", "final_g"]

    def leaf(kind, name):
        if name in big:
            return big[name][kind][None]
        r0, n = small_rows[name]
        a = (sg, sd, sm, sv)[kind][r0:r0 + n]
        return a.reshape(D) if name == "final_g" else a

    outs = [loss.reshape(()), grad_x.reshape(1, t, D)]
    for kind in range(4):
        outs += [leaf(kind, name) for name in order]
    return tuple(outs)
```

```python
import functools

import jax
import jax.numpy as jnp
from jax import lax
from jax.experimental import pallas as pl
from jax.experimental.pallas import tpu as pltpu

F32 = jnp.float32
BF16 = jnp.bfloat16
MESH = pl.DeviceIdType.MESH

N_DEV = 8
D = 1024
N_COLS = 7 * D
COLS_PER_DEV = N_COLS // N_DEV
PLE = 256
HEAD = 128
N_HEADS = D // HEAD
CONV_K = 31
CONV_PAD = 32
CHUNK = 64
EPS = 1e-6
SUBLANES = 8

ADAM_LR = 0.001
ADAM_B1 = 0.9
ADAM_B2 = 0.999
ADAM_EPS = 1e-08
ADAM_WD = 0.01
ADAM_STEP = 10

MIB = 1024 * 1024
N_SMALL = 16
R_LN, R_CONVB, R_CNG, R_CNB, R_BPW2, R_LB0, R_LB1, R_ON, R_PEN, R_FIN, R_LOSS = range(11)


def _params(vmem_mib, **kw):
    return pltpu.CompilerParams(vmem_limit_bytes=vmem_mib * MIB, **kw)


def _dot(a, b):
    return jnp.dot(a.astype(BF16), b.astype(BF16), preferred_element_type=F32)


def _dot_nt(a, b):
    return lax.dot_general(a.astype(BF16), b.astype(BF16), (((1,), (1,)), ((), ())),
                           preferred_element_type=F32)


def _dot_tn(a, b):
    return lax.dot_general(a.astype(BF16), b.astype(BF16), (((0,), (0,)), ((), ())),
                           preferred_element_type=F32)


def _split(a):
    hi = a.astype(BF16)
    return hi, (a - hi.astype(F32)).astype(BF16)


def _dot_split(a, b, dims):
    ah, al = _split(a)
    bh, bl = _split(b)
    dg = lambda p, q: lax.dot_general(p, q, dims, preferred_element_type=F32)
    return dg(ah, bh) + (dg(ah, bl) + dg(al, bh))


def _sigmoid(x):
    return 1.0 / (1.0 + jnp.exp(-x))


def _rowsum8(a):
    r, c = a.shape
    return jnp.sum(a.reshape(r // SUBLANES, SUBLANES, c), axis=0)


def _tri_dot(tri, a):
    hi = a.astype(BF16)
    r1 = a - hi.astype(F32)
    mid = r1.astype(BF16)
    lo = (r1 - mid.astype(F32)).astype(BF16)
    return (jnp.dot(tri, hi, preferred_element_type=F32)
            + jnp.dot(tri, mid, preferred_element_type=F32)
            + jnp.dot(tri, lo, preferred_element_type=F32))


def _lower_bound(lbl):
    l0, l1 = lbl[0:1, :], lbl[1:2, :]
    m = jnp.maximum(l0, l1)
    e0, e1 = jnp.exp(l0 - m), jnp.exp(l1 - m)
    s = e0 + e1
    return e0 / s, e1 / s


ANY = pl.BlockSpec(memory_space=pl.ANY)


def _full(shape):
    return pl.BlockSpec(shape, lambda i: (0,) * len(shape))


def _peer(x, y, c, k):
    px = 1 - x if k & 4 else x
    py = 1 - y if k & 2 else y
    pc = 1 - c if k & 1 else c
    return (px, py, pc), 4 * px + 2 * py + pc


class _Exchange:
    def __init__(self, srcs, outs, modes, send_sems, recv_sems, local_sems):
        x, y, c = lax.axis_index("x"), lax.axis_index("y"), lax.axis_index("c")
        me = 4 * x + 2 * y + c
        self.starts, self.send_waits, self.recv_waits = [], [], []

        def remote(a, k, src, slot, peer, when):
            sem = a * N_DEV + k
            cp = pltpu.make_async_remote_copy(
                src_ref=src, dst_ref=outs[a].at[slot], send_sem=send_sems.at[sem],
                recv_sem=recv_sems.at[sem], device_id=peer, device_id_type=MESH)
            self.starts.append((when, cp.start))
            self.send_waits.append((when, cp.wait_send))

        def arrival(a, k, slot, when):
            sem = a * N_DEV + k
            cp = pltpu.make_async_remote_copy(
                src_ref=outs[a].at[slot], dst_ref=outs[a].at[slot], send_sem=send_sems.at[sem],
                recv_sem=recv_sems.at[sem], device_id=(x, y, c), device_id_type=MESH)
            self.recv_waits.append((when, cp.wait_recv))

        def local(a, src, slot, when):
            cp = pltpu.make_async_copy(src, outs[a].at[slot], local_sems.at[a])
            self.starts.append((when, cp.start))
            self.send_waits.append((when, cp.wait))

        for a, (src, mode) in enumerate(zip(srcs, modes)):
            if mode in ("gather", "scatter"):
                local(a, src if mode == "gather" else src.at[me], me, None)
                for k in range(1, N_DEV):
                    peer, peer_idx = _peer(x, y, c, k)
                    remote(a, k, src if mode == "gather" else src.at[peer_idx], me, peer, None)
                    arrival(a, k, peer_idx, None)
                continue
            if isinstance(mode, tuple):
                here, away = x == mode[1], x != mode[1]
                chip = 2 * x + y
                local(a, src.at[y], chip, here)
                remote(a, 1, src.at[1 - y], chip, (x, 1 - y, c), here)
                remote(a, 2, src.at[y], chip, (1 - x, y, c), away)
                remote(a, 3, src.at[1 - y], chip, (1 - x, 1 - y, c), away)
                arrival(a, 1, 2 * x + 1 - y, here)
                arrival(a, 2, 2 * (1 - x) + y, here)
                arrival(a, 3, 2 * (1 - x) + 1 - y, here)
                continue
            here, away = x == mode, x != mode
            for kk in range(4):
                py = 1 - y if kk & 2 else y
                pc = 1 - c if kk & 1 else c
                block = src.at[2 * py + pc]
                if kk == 0:
                    local(a, block, me, here)
                else:
                    remote(a, kk, block, me, (x, py, pc), here)
                remote(a, 4 + kk, block, me, (1 - x, py, pc), away)
            for k in range(1, N_DEV):
                arrival(a, k, _peer(x, y, c, k)[1], here)

    @staticmethod
    def _run(actions):
        for when, fn in actions:
            if when is None:
                fn()
            else:
                pl.when(when)(fn)

    def start(self):
        self._run(self.starts)

    def wait(self):
        self._run(self.recv_waits)
        self._run(self.send_waits)


def _exchange_scratch(n):
    return [pltpu.SemaphoreType.DMA((n * N_DEV,)), pltpu.SemaphoreType.DMA((n * N_DEV,)),
            pltpu.SemaphoreType.DMA((n,))]


def _recv_shapes(srcs, modes):
    def shape(s, m):
        if m == "gather":
            return (N_DEV,) + s.shape
        return (N_DEV // 2 if isinstance(m, tuple) else N_DEV,) + s.shape[1:]

    return [jax.ShapeDtypeStruct(shape(s, m), s.dtype) for s, m in zip(srcs, modes)]


def _pair_reduce(name, blocks):
    shape = (2,) + blocks.shape[1:]

    def body(src, out_ref, stage, mine, send_sems, recv_sems, local_sems):
        x, y, c = lax.axis_index("x"), lax.axis_index("y"), lax.axis_index("c")
        sends, waits = [], []
        for py in range(2):
            sends.append(pltpu.make_async_remote_copy(
                src_ref=src.at[2 * py + 1 - c], dst_ref=stage.at[py], send_sem=send_sems.at[py],
                recv_sem=recv_sems.at[py], device_id=(x, y, 1 - c), device_id_type=MESH))
            waits.append(pltpu.make_async_copy(src.at[2 * py + c], mine.at[py], local_sems.at[py]))
        for cp in sends + waits:
            cp.start()
        for cp in waits:
            cp.wait()
        for cp in sends:
            cp.wait_recv()
        out_ref[...] = mine[...] + stage[...]
        for cp in sends:
            cp.wait_send()

    return pl.pallas_call(
        body, name=name, out_shape=jax.ShapeDtypeStruct(shape, F32), in_specs=[ANY],
        scratch_shapes=[pltpu.VMEM(shape, F32), pltpu.VMEM(shape, F32),
                        pltpu.SemaphoreType.DMA((2,)), pltpu.SemaphoreType.DMA((2,)),
                        pltpu.SemaphoreType.DMA((2,))],
        compiler_params=_params(40),
    )(blocks)


def _exchange_call(name, srcs, modes):
    n = len(srcs)

    def body(*refs):
        xch = _Exchange(refs[:n], refs[n:2 * n], modes, *refs[2 * n:])
        xch.start()
        xch.wait()

    return pl.pallas_call(
        body, name=name, out_shape=_recv_shapes(srcs, modes),
        in_specs=[ANY] * n, out_specs=[ANY] * n, scratch_shapes=_exchange_scratch(n),
    )(*srcs)


def _hosted(body, n_in, n_out, grid, modes):
    n = len(modes)

    def hosted(*refs):
        ins, srcs = refs[:n_in], refs[n_in:n_in + n]
        outs = refs[n_in + n:n_in + n + n_out]
        bufs = refs[n_in + n + n_out:n_in + 2 * n + n_out]
        scratch = refs[n_in + 2 * n + n_out:-3]
        xch = _Exchange(srcs, bufs, modes, *refs[-3:])
        first, last = True, True
        for axis, size in enumerate(grid):
            first = jnp.logical_and(first, pl.program_id(axis) == 0)
            last = jnp.logical_and(last, pl.program_id(axis) == size - 1)
        pl.when(first)(xch.start)
        body(*ins, *outs, *scratch)
        pl.when(last)(xch.wait)

    return hosted


def _inproj_fwd(x, ln_g, w_in_all, other_shards):
    t = x.shape[0]
    tt = min(256, t)
    modes = ["gather"] * len(other_shards)

    def body(x_ref, g_ref, w_hbm, z_ref, u_ref, w_vmem):
        @pl.when(pl.program_id(0) == 0)
        def _():
            pltpu.sync_copy(w_hbm, w_vmem)

        xv = x_ref[...]
        rstd = lax.rsqrt(jnp.mean(xv * xv, axis=-1, keepdims=True) + EPS)
        ub = (xv * rstd * g_ref[...]).astype(BF16)
        u_ref[...] = ub
        for d in range(N_DEV):
            z_ref[:, COLS_PER_DEV * d:COLS_PER_DEV * (d + 1)] = jnp.dot(
                ub, w_vmem[d], preferred_element_type=F32)

    n = len(modes)
    return pl.pallas_call(
        _hosted(body, 3, 2, (t // tt,), modes), name="inproj_fwd", grid=(t // tt,),
        out_shape=[jax.ShapeDtypeStruct((t, N_COLS), F32), jax.ShapeDtypeStruct((t, D), BF16)]
        + _recv_shapes(other_shards, modes),
        in_specs=[pl.BlockSpec((tt, D), lambda i: (i, 0)), _full((1, D)), ANY] + [ANY] * n,
        out_specs=[pl.BlockSpec((tt, N_COLS), lambda i: (i, 0)),
                   pl.BlockSpec((tt, D), lambda i: (i, 0))] + [ANY] * n,
        scratch_shapes=[pltpu.VMEM((N_DEV, D, COLS_PER_DEV), BF16)] + _exchange_scratch(n),
        compiler_params=_params(48, dimension_semantics=("arbitrary",)),
    )(x, ln_g, w_in_all, *other_shards)


def _shifted_copies(buf, shifted, rows):
    for b in range(1, SUBLANES):
        shifted[b, 0:rows, :] = buf[b:b + rows, :]


def _tap_ref(buf, shifted, offset):
    a, b = divmod(offset, SUBLANES)
    return (buf if b == 0 else shifted.at[b]), SUBLANES * a


def _group_norm_stats(blk):
    mu = jnp.mean(blk, axis=-1, keepdims=True)
    cen = blk - mu
    var = jnp.mean(cen * cen, axis=-1, keepdims=True)
    return cen * lax.rsqrt(var + EPS)


def _conv_fwd(z, conv_w_all, conv_b, cn_g, cn_b, w_pw2, b_pw2):
    t = z.shape[0]
    tt = min(256, t)
    rc = 32

    def body(val_ref, glu_ref, gate_ref, cw_ref, cb_ref, g_ref, b_ref, w_hbm, b2_ref,
             yc_ref, y2_ref, yo_ref, w_vmem, vbuf, vsh, y1buf):
        @pl.when(pl.program_id(0) == 0)
        def _():
            pltpu.sync_copy(w_hbm, w_vmem)
            vbuf[0:CONV_PAD, :] = jnp.zeros((CONV_PAD, D), F32)

        vbuf[CONV_PAD:CONV_PAD + tt, :] = val_ref[...] * _sigmoid(glu_ref[...])
        _shifted_copies(vbuf, vsh, tt + 24)

        def row_chunk(r, carry):
            r0 = pl.multiple_of(r * rc, rc)
            for g in range(N_HEADS):
                cs = slice(HEAD * g, HEAD * (g + 1))
                acc = jnp.zeros((rc, HEAD), F32)
                for k in range(CONV_K):
                    ref, off = _tap_ref(vbuf, vsh, k + 2)
                    acc = acc + cw_ref[g, k:k + 1, :] * ref[pl.ds(r0 + off, rc), cs]
                acc = acc + cb_ref[:, cs]
                yc_ref[pl.ds(r0, rc), cs] = acc
                n = _group_norm_stats(acc) * g_ref[:, cs] + b_ref[:, cs]
                y1buf[pl.ds(r0, rc), cs] = (n * _sigmoid(n)).astype(BF16)
            return carry

        lax.fori_loop(0, tt // rc, row_chunk, 0)
        vbuf[0:CONV_PAD, :] = vbuf[tt:tt + CONV_PAD, :]
        y2 = jnp.dot(y1buf[...], w_vmem[...], preferred_element_type=F32) + b2_ref[...]
        y2_ref[...] = y2
        gate = gate_ref[...]
        yo_ref[...] = (y2 * gate * _sigmoid(gate)).astype(BF16)

    col = lambda j: pl.BlockSpec((tt, D), lambda i: (i, j))
    row = pl.BlockSpec((tt, D), lambda i: (i, 0))
    return pl.pallas_call(
        body, name="conv_fwd", grid=(t // tt,),
        out_shape=[jax.ShapeDtypeStruct((t, D), F32), jax.ShapeDtypeStruct((t, D), F32),
                   jax.ShapeDtypeStruct((t, D), BF16)],
        in_specs=[col(0), col(1), col(2), _full((N_DEV, CONV_PAD, HEAD)), _full((1, D)),
                  _full((1, D)), _full((1, D)), ANY, _full((1, D))],
        out_specs=[row, row, row],
        scratch_shapes=[pltpu.VMEM((D, D), BF16), pltpu.VMEM((tt + CONV_PAD, D), F32),
                        pltpu.VMEM((SUBLANES, tt + CONV_PAD, D), F32), pltpu.VMEM((tt, D), BF16)],
        compiler_params=_params(48, dimension_semantics=("arbitrary",)),
    )(z, z, z, conv_w_all, conv_b, cn_g, cn_b, w_pw2, b_pw2)


def _chunk_quantities(zq, zf, lbh, tri):
    sig = _sigmoid(zf)
    sig_neg = _sigmoid(-zf)
    f = lbh + (1.0 - lbh) * sig
    k = (1.0 - lbh) * sig_neg
    q = zq * _sigmoid(zq)
    b = _tri_dot(tri, jnp.log(f))
    b_mid = b[CHUNK // 2 - 1:CHUNK // 2, :]
    b_last = b[CHUNK - 1:CHUNK, :]
    e_q = jnp.exp(b)
    e_qm = jnp.exp(b - b_mid)
    e_km = jnp.exp(b_mid - b)
    e_kd = jnp.exp(b_last - b)
    return q, k, f, sig, sig_neg, e_q, e_qm, e_km, e_kd, jnp.exp(b_last)


def _hgrn_fwd(z, lb_logits, onorm_g):
    t = z.shape[0]
    tt = min(256, t)
    nc = tt // CHUNK

    def body(q_ref, f_ref, i_ref, g_ref, lbl_ref, on_ref, o_ref, y_ref, s_ref, st):
        @pl.when(pl.program_id(0) == 0)
        def _():
            st[...] = jnp.zeros_like(st)

        lb, _ = _lower_bound(lbl_ref[...])
        rows = lax.broadcasted_iota(jnp.int32, (CHUNK, CHUNK), 0)
        cols = lax.broadcasted_iota(jnp.int32, (CHUNK, CHUNK), 1)
        causal = rows >= cols
        tri = causal.astype(BF16)

        def chunk(c, carry):
            r0 = pl.multiple_of(c * CHUNK, CHUNK)
            rs = pl.ds(r0, CHUNK)
            for h in range(N_HEADS):
                cs = slice(HEAD * h, HEAD * (h + 1))
                q, k, _, _, _, e_q, e_qm, e_km, e_kd, e_last = _chunk_quantities(
                    q_ref[rs, cs], f_ref[rs, cs], lb[:, cs], tri)
                v = i_ref[rs, cs]
                s_old = st[h]
                s_ref[c, h] = s_old
                a = jnp.where(causal, _dot_nt(q * e_qm, k * e_km), 0.0)
                o = _dot_nt(q * e_q, s_old) + _dot(a, v)
                st[h] = s_old * e_last + _dot_tn(v, k * e_kd)
                o_ref[rs, cs] = o
                n = o * lax.rsqrt(jnp.mean(o * o, axis=-1, keepdims=True) + EPS)
                zg = g_ref[rs, cs]
                y_ref[rs, cs] = (n * on_ref[:, cs] * zg * _sigmoid(zg)).astype(BF16)
            return carry

        lax.fori_loop(0, nc, chunk, 0)

    col = lambda j: pl.BlockSpec((tt, D), lambda i: (i, j))
    row = pl.BlockSpec((tt, D), lambda i: (i, 0))
    return pl.pallas_call(
        body, name="hgrn_fwd", grid=(t // tt,),
        out_shape=[jax.ShapeDtypeStruct((t, D), F32), jax.ShapeDtypeStruct((t, D), BF16),
                   jax.ShapeDtypeStruct((t // CHUNK, N_HEADS, HEAD, HEAD), F32)],
        in_specs=[col(3), col(4), col(5), col(6), _full((2, D)), _full((1, D))],
        out_specs=[row, row, pl.BlockSpec((nc, N_HEADS, HEAD, HEAD), lambda i: (i, 0, 0, 0))],
        scratch_shapes=[pltpu.VMEM((N_HEADS, HEAD, HEAD), F32)],
        compiler_params=_params(40, dimension_semantics=("arbitrary",)),
    )(z, z, z, z, lb_logits, onorm_g)


def _rms_bwd(dn, xhat, rstd):
    return rstd * (dn - xhat * jnp.mean(dn * xhat, axis=-1, keepdims=True))


def _tail(x, y_conv, y_hgrn, p, target, w_out, w_pg, w_pp_all, pe_g, fin_g):
    t = x.shape[0]
    tt = min(256, t)
    n_steps = t // tt

    def body(x_ref, yc_ref, yh_ref, p_ref, tg_ref, wo_hbm, wg_hbm, wp_hbm, pg_ref, fg_ref,
             dh1_ref, dyc_ref, dyh_ref, dwo_hbm, dwg_hbm, dwp_hbm, dpg_ref, dfg_ref, loss_ref,
             wo, wg, wp, dwo, dwg, dwp):
        i = pl.program_id(0)

        @pl.when(i == 0)
        def _():
            pltpu.sync_copy(wo_hbm, wo)
            pltpu.sync_copy(wg_hbm, wg)
            for d in range(N_DEV):
                pltpu.sync_copy(wp_hbm.at[d], wp.at[:, pl.ds(HEAD * d, HEAD)])
            dwo[...] = jnp.zeros_like(dwo)
            dwg[...] = jnp.zeros_like(dwg)
            dwp[...] = jnp.zeros_like(dwp)
            dpg_ref[...] = jnp.zeros_like(dpg_ref)
            dfg_ref[...] = jnp.zeros_like(dfg_ref)
            loss_ref[...] = jnp.zeros_like(loss_ref)

        ycv, yhv = yc_ref[...], yh_ref[...]
        h1 = (x_ref[...] + jnp.dot(ycv, wo[0:D, :], preferred_element_type=F32)
              + jnp.dot(yhv, wo[D:2 * D, :], preferred_element_type=F32))
        pb = p_ref[...].astype(BF16)
        pe = jnp.dot(pb, wp[...], preferred_element_type=F32)
        rstd1 = lax.rsqrt(jnp.mean(h1 * h1, axis=-1, keepdims=True) + EPS)
        n1 = h1 * rstd1
        rb = (n1 * pg_ref[...]).astype(BF16)
        gate = _sigmoid(jnp.dot(rb, wg[...], preferred_element_type=F32))
        h2 = h1 + gate * pe
        rstd2 = lax.rsqrt(jnp.mean(h2 * h2, axis=-1, keepdims=True) + EPS)
        n2 = h2 * rstd2
        err = n2 * fg_ref[...] - tg_ref[...]
        loss_ref[...] += _rowsum8(err * err)

        d_out = err * (1.0 / D)
        dfg_ref[...] += _rowsum8(d_out * n2)
        d_h2 = _rms_bwd(d_out * fg_ref[...], n2, rstd2)
        d_pe = (d_h2 * gate).astype(BF16)
        d_gpre = (d_h2 * pe * gate * (1.0 - gate)).astype(BF16)
        dwg[...] += _dot_tn(rb, d_gpre)
        dwp[...] += _dot_tn(pb, d_pe)
        dr = _dot_nt(d_gpre, wg[...])
        dpg_ref[...] += _rowsum8(dr * n1)
        d_h1 = d_h2 + _rms_bwd(dr * pg_ref[...], n1, rstd1)
        dh1_ref[...] = d_h1
        d_h1b = d_h1.astype(BF16)
        dwo[0:D, :] += _dot_tn(ycv, d_h1b)
        dwo[D:2 * D, :] += _dot_tn(yhv, d_h1b)
        dyc_ref[...] = _dot_nt(d_h1b, wo[0:D, :])
        dyh_ref[...] = _dot_nt(d_h1b, wo[D:2 * D, :])

        @pl.when(i == n_steps - 1)
        def _():
            pltpu.sync_copy(dwo, dwo_hbm)
            pltpu.sync_copy(dwg, dwg_hbm)
            for d in range(N_DEV):
                pltpu.sync_copy(dwp.at[:, pl.ds(HEAD * d, HEAD)], dwp_hbm.at[d])

    row = pl.BlockSpec((tt, D), lambda i: (i, 0))
    acc = _full((SUBLANES, D))
    return pl.pallas_call(
        body, name="tail_fwd_bwd", grid=(n_steps,),
        out_shape=[jax.ShapeDtypeStruct((t, D), F32)] * 3
        + [jax.ShapeDtypeStruct((2 * D, D), F32), jax.ShapeDtypeStruct((D, D), F32),
           jax.ShapeDtypeStruct((N_DEV, PLE, HEAD), F32)]
        + [jax.ShapeDtypeStruct((SUBLANES, D), F32)] * 3,
        in_specs=[row, row, row, pl.BlockSpec((tt, PLE), lambda i: (i, 0)), row,
                  ANY, ANY, ANY, _full((1, D)), _full((1, D))],
        out_specs=[row, row, row, ANY, ANY, ANY, acc, acc, acc],
        scratch_shapes=[pltpu.VMEM((2 * D, D), BF16), pltpu.VMEM((D, D), BF16),
                        pltpu.VMEM((PLE, D), BF16), pltpu.VMEM((2 * D, D), F32),
                        pltpu.VMEM((D, D), F32), pltpu.VMEM((PLE, D), F32)],
        compiler_params=_params(52, dimension_semantics=("arbitrary",)),
    )(x, y_conv, y_hgrn, p, target, w_out, w_pg, w_pp_all, pe_g, fin_g)


def _hgrn_bwd(dy, z, o_raw, states, lb_logits, onorm_g, grads):
    t = z.shape[0]
    tt = min(256, t)
    nc = tt // CHUNK
    n_steps = t // tt
    modes = ["scatter"] * len(grads)

    def body(dy_ref, q_ref, f_ref, i_ref, g_ref, o_ref, s_ref, lbl_ref, on_ref,
             dz_ref, don_ref, dlb_ref, dst):
        @pl.when(pl.program_id(0) == 0)
        def _():
            dst[...] = jnp.zeros_like(dst)
            don_ref[...] = jnp.zeros_like(don_ref)
            dlb_ref[...] = jnp.zeros_like(dlb_ref)

        lb, _ = _lower_bound(lbl_ref[...])
        rows = lax.broadcasted_iota(jnp.int32, (CHUNK, CHUNK), 0)
        cols = lax.broadcasted_iota(jnp.int32, (CHUNK, CHUNK), 1)
        causal = rows >= cols
        tri = causal.astype(BF16)
        tri_rev = (rows <= cols).astype(BF16)
        is_last = lax.broadcasted_iota(jnp.int32, (CHUNK, HEAD), 0) == CHUNK - 1

        def chunk(cc, carry):
            c = nc - 1 - cc
            r0 = pl.multiple_of(c * CHUNK, CHUNK)
            rs = pl.ds(r0, CHUNK)
            for h in range(N_HEADS):
                cs = slice(HEAD * h, HEAD * (h + 1))
                zq, zf, zg = q_ref[rs, cs], f_ref[rs, cs], g_ref[rs, cs]
                lbh = lb[:, cs]
                q, k, f, sig, sig_neg, e_q, e_qm, e_km, e_kd, e_last = _chunk_quantities(
                    zq, zf, lbh, tri)
                v = i_ref[rs, cs]
                qt, qm, km, kd = q * e_q, q * e_qm, k * e_km, k * e_kd
                a = jnp.where(causal, _dot_nt(qm, km), 0.0)

                o = o_ref[rs, cs]
                rstd = lax.rsqrt(jnp.mean(o * o, axis=-1, keepdims=True) + EPS)
                n = o * rstd
                sg = _sigmoid(zg)
                dyv = dy_ref[rs, cs]
                on = on_ref[:, cs]
                d_zg = dyv * n * on * sg * (1.0 + zg * (1.0 - sg))
                d_on = dyv * zg * sg
                don_ref[:, cs] += _rowsum8(d_on * n)
                do = _rms_bwd(d_on * on, n, rstd)

                s_old = s_ref[c, h]
                ds_new = dst[h]
                da = jnp.where(causal, _dot_nt(do, v), 0.0)
                dv = _dot_tn(a, do) + _dot_nt(kd, ds_new)
                dkd = _dot(v, ds_new)
                dqm = _dot_split(da, km, (((1,), (0,)), ((), ())))
                dkm = _dot_split(da, qm, (((0,), (0,)), ((), ())))
                dq = _dot(do, s_old) * e_q + dqm * e_qm
                dk = dkm * e_km + dkd * e_kd
                dst[h] = ds_new * e_last + _dot_tn(do, qt)
                last = (jnp.sum(dkd * kd, axis=0, keepdims=True)
                        + e_last * jnp.sum(s_old * ds_new, axis=0, keepdims=True))
                db = q * dq - k * dk + jnp.where(is_last, last, 0.0)
                dlogf = _tri_dot(tri_rev, db)
                common = sig_neg * (dlogf / f - dk)
                dlb_ref[:, cs] += _rowsum8(common)
                c0 = 3 * D + HEAD * h
                dz_ref[rs, c0:c0 + HEAD] = (
                    dq * _sigmoid(zq) * (1.0 + zq * (1.0 - _sigmoid(zq)))).astype(BF16)
                dz_ref[rs, D + c0:D + c0 + HEAD] = ((1.0 - lbh) * sig * common).astype(BF16)
                dz_ref[rs, 2 * D + c0:2 * D + c0 + HEAD] = dv.astype(BF16)
                dz_ref[rs, 3 * D + c0:3 * D + c0 + HEAD] = d_zg.astype(BF16)
            return carry

        lax.fori_loop(0, nc, chunk, 0)

    rev = lambda i: n_steps - 1 - i
    col = lambda j: pl.BlockSpec((tt, D), lambda i: (rev(i), j))
    row = pl.BlockSpec((tt, D), lambda i: (rev(i), 0))
    acc = _full((SUBLANES, D))
    n = len(modes)
    return pl.pallas_call(
        _hosted(body, 9, 3, (n_steps,), modes), name="hgrn_bwd", grid=(n_steps,),
        out_shape=[jax.ShapeDtypeStruct((t, N_COLS), BF16),
                   jax.ShapeDtypeStruct((SUBLANES, D), F32),
                   jax.ShapeDtypeStruct((SUBLANES, D), F32)] + _recv_shapes(grads, modes),
        in_specs=[row, col(3), col(4), col(5), col(6), row,
                  pl.BlockSpec((nc, N_HEADS, HEAD, HEAD), lambda i: (rev(i), 0, 0, 0)),
                  _full((2, D)), _full((1, D))] + [ANY] * n,
        out_specs=[pl.BlockSpec((tt, N_COLS), lambda i: (rev(i), 0)), acc, acc] + [ANY] * n,
        scratch_shapes=[pltpu.VMEM((N_HEADS, HEAD, HEAD), F32)] + _exchange_scratch(n),
        compiler_params=_params(48, dimension_semantics=("arbitrary",)),
    )(dy, z, z, z, z, o_raw, states, lb_logits, onorm_g, *grads)


def _conv_bwd(dy, z, yc, y2, conv_w_all, cn_g, cn_b, w_pw2, dz, grads, modes):
    t = z.shape[0]
    tt = min(256, t)
    rc = 32
    n_steps = t // tt

    def body(dy_ref, val_ref, glu_ref, gate_ref, yc_ref, y2_ref, cw_ref, g_ref, b_ref, w_hbm,
             dz_in, dz_ref, dw_hbm, dcw_out, db2_ref, dg_ref, dbeta_ref, dcb_ref,
             w_vmem, dw, dbuf, dsh, y1buf, dnbuf, dcw_ref):
        i = pl.program_id(0)

        @pl.when(i == 0)
        def _():
            pltpu.sync_copy(w_hbm, w_vmem)
            dw[...] = jnp.zeros_like(dw)
            dbuf[tt:tt + CONV_PAD, :] = jnp.zeros((CONV_PAD, D), F32)
            dcw_ref[...] = jnp.zeros_like(dcw_ref)
            dcw_out[...] = jnp.zeros_like(dcw_out)
            db2_ref[...] = jnp.zeros_like(db2_ref)
            dg_ref[...] = jnp.zeros_like(dg_ref)
            dbeta_ref[...] = jnp.zeros_like(dbeta_ref)
            dcb_ref[...] = jnp.zeros_like(dcb_ref)

        gate = gate_ref[...]
        sg = _sigmoid(gate)
        dyv = dy_ref[...]
        dy2 = dyv * gate * sg
        dz_ref[:, 2 * D:3 * D] = (dyv * y2_ref[...] * sg * (1.0 + gate * (1.0 - sg))).astype(BF16)
        db2_ref[...] += _rowsum8(dy2)
        dy2b = dy2.astype(BF16)
        dnbuf[...] = _dot_nt(dy2b, w_vmem[...])

        def norm_chunk(r, carry):
            r0 = pl.multiple_of(r * rc, rc)
            rs = pl.ds(r0, rc)
            for g in range(N_HEADS):
                cs = slice(HEAD * g, HEAD * (g + 1))
                blk = yc_ref[rs, cs]
                mu = jnp.mean(blk, axis=-1, keepdims=True)
                cen = blk - mu
                rstd = lax.rsqrt(jnp.mean(cen * cen, axis=-1, keepdims=True) + EPS)
                xhat = cen * rstd
                n = xhat * g_ref[:, cs] + b_ref[:, cs]
                sn = _sigmoid(n)
                y1buf[rs, cs] = (n * sn).astype(BF16)
                dn = dnbuf[rs, cs] * sn * (1.0 + n * (1.0 - sn))
                dg_ref[:, cs] += _rowsum8(dn * xhat)
                dbeta_ref[:, cs] += _rowsum8(dn)
                dxh = dn * g_ref[:, cs]
                dyc = rstd * (dxh - jnp.mean(dxh, axis=-1, keepdims=True)
                              - xhat * jnp.mean(dxh * xhat, axis=-1, keepdims=True))
                dcb_ref[:, cs] += _rowsum8(dyc)
                dbuf[rs, cs] = dyc
            return carry

        lax.fori_loop(0, tt // rc, norm_chunk, 0)
        dw[...] += _dot_tn(y1buf[...], dy2b)
        _shifted_copies(dbuf, dsh, tt + 24)

        def conv_chunk(r, carry):
            r0 = pl.multiple_of(r * rc, rc)
            rs = pl.ds(r0, rc)
            for g in range(N_HEADS):
                cs = slice(HEAD * g, HEAD * (g + 1))
                sglu = _sigmoid(glu_ref[rs, cs])
                val = val_ref[rs, cs]
                v = val * sglu
                dv = jnp.zeros((rc, HEAD), F32)
                for k in range(CONV_K):
                    ref, off = _tap_ref(dbuf, dsh, CONV_K - 1 - k)
                    d_later = ref[pl.ds(r0 + off, rc), cs]
                    dv = dv + cw_ref[g, k:k + 1, :] * d_later
                    dcw_ref[g, k] += _rowsum8(v * d_later)
                dz_ref[rs, cs] = (dv * sglu).astype(BF16)
                dz_ref[rs, D + HEAD * g:D + HEAD * (g + 1)] = (
                    dv * val * sglu * (1.0 - sglu)).astype(BF16)
            return carry

        lax.fori_loop(0, tt // rc, conv_chunk, 0)
        dbuf[tt:tt + CONV_PAD, :] = dbuf[0:CONV_PAD, :]

        @pl.when(i == n_steps - 1)
        def _():
            pltpu.sync_copy(dw, dw_hbm)
            for g in range(N_HEADS):
                for k in range(CONV_K):
                    dcw_out[g, k:k + 1, :] = jnp.sum(dcw_ref[g, k], axis=0, keepdims=True)

    rev = lambda i: n_steps - 1 - i
    col = lambda j: pl.BlockSpec((tt, D), lambda i: (rev(i), j))
    row = pl.BlockSpec((tt, D), lambda i: (rev(i), 0))
    acc = _full((SUBLANES, D))
    n = len(modes)
    return pl.pallas_call(
        _hosted(body, 11, 7, (n_steps,), modes), name="conv_bwd", grid=(n_steps,),
        out_shape=[jax.ShapeDtypeStruct((t, N_COLS), BF16), jax.ShapeDtypeStruct((D, D), F32),
                   jax.ShapeDtypeStruct((N_DEV, CONV_PAD, HEAD), F32)]
        + [jax.ShapeDtypeStruct((SUBLANES, D), F32)] * 4 + _recv_shapes(grads, modes),
        in_specs=[row, col(0), col(1), col(2), row, row, _full((N_DEV, CONV_PAD, HEAD)),
                  _full((1, D)), _full((1, D)), ANY, ANY] + [ANY] * n,
        out_specs=[pl.BlockSpec((tt, 3 * D), lambda i: (rev(i), 0)), ANY,
                   _full((N_DEV, CONV_PAD, HEAD)), acc, acc, acc, acc] + [ANY] * n,
        input_output_aliases={10: 0},
        scratch_shapes=[pltpu.VMEM((D, D), BF16), pltpu.VMEM((D, D), F32),
                        pltpu.VMEM((tt + CONV_PAD, D), F32),
                        pltpu.VMEM((SUBLANES, tt + CONV_PAD, D), F32),
                        pltpu.VMEM((tt, D), BF16), pltpu.VMEM((tt, D), F32),
                        pltpu.VMEM((N_DEV, CONV_PAD, SUBLANES, HEAD), F32)] + _exchange_scratch(n),
        compiler_params=_params(52, dimension_semantics=("arbitrary",)),
    )(dy, z, z, z, yc, y2, conv_w_all, cn_g, cn_b, w_pw2, dz, *grads)


def _inproj_bwd_dx(dz, x, d_h1, ln_g, w_in_all, grads, modes):
    t = x.shape[0]
    tt = min(256, t)

    def body(dz_ref, x_ref, dh1_ref, g_ref, w_hbm, dx_ref, dg_ref, w_vmem):
        @pl.when(pl.program_id(0) == 0)
        def _():
            pltpu.sync_copy(w_hbm, w_vmem)
            dg_ref[...] = jnp.zeros_like(dg_ref)

        du = jnp.zeros((tt, D), F32)
        for d in range(N_DEV):
            du = du + lax.dot_general(
                dz_ref[:, COLS_PER_DEV * d:COLS_PER_DEV * (d + 1)], w_vmem[d],
                (((1,), (1,)), ((), ())), preferred_element_type=F32)
        xv = x_ref[...]
        rstd = lax.rsqrt(jnp.mean(xv * xv, axis=-1, keepdims=True) + EPS)
        xhat = xv * rstd
        dg_ref[...] += _rowsum8(du * xhat)
        dx_ref[...] = dh1_ref[...] + _rms_bwd(du * g_ref[...], xhat, rstd)

    row = pl.BlockSpec((tt, D), lambda i: (i, 0))
    n = len(modes)
    return pl.pallas_call(
        _hosted(body, 5, 2, (t // tt,), modes), name="inproj_bwd_dx", grid=(t // tt,),
        out_shape=[jax.ShapeDtypeStruct((t, D), F32), jax.ShapeDtypeStruct((SUBLANES, D), F32)]
        + _recv_shapes(grads, modes),
        in_specs=[pl.BlockSpec((tt, N_COLS), lambda i: (i, 0)), row, row, _full((1, D)), ANY]
        + [ANY] * n,
        out_specs=[row, _full((SUBLANES, D))] + [ANY] * n,
        scratch_shapes=[pltpu.VMEM((N_DEV, D, COLS_PER_DEV), BF16)] + _exchange_scratch(n),
        compiler_params=_params(48, dimension_semantics=("arbitrary",)),
    )(dz, x, d_h1, ln_g, w_in_all, *grads)


def _inproj_bwd_dw(name, u, dz, first, count, grads=(), modes=()):
    t = u.shape[0]
    tt = min(512, t)
    grid = (count, t // tt)
    n = len(modes)

    def body(u_ref, dz_ref, dw_ref):
        @pl.when(pl.program_id(1) == 0)
        def _():
            dw_ref[...] = jnp.zeros_like(dw_ref)

        dw_ref[0] += lax.dot_general(u_ref[...], dz_ref[...], (((0,), (0,)), ((), ())),
                                     preferred_element_type=F32)

    return pl.pallas_call(
        _hosted(body, 2, 1, grid, modes) if n else body, name=name, grid=grid,
        out_shape=[jax.ShapeDtypeStruct((count, D, COLS_PER_DEV), F32)]
        + _recv_shapes(grads, modes),
        in_specs=[pl.BlockSpec((tt, D), lambda j, i: (i, 0)),
                  pl.BlockSpec((tt, COLS_PER_DEV), lambda j, i: (i, first + j))] + [ANY] * n,
        out_specs=[pl.BlockSpec((1, D, COLS_PER_DEV), lambda j, i: (j, 0, 0))] + [ANY] * n,
        scratch_shapes=_exchange_scratch(n) if n else [],
        compiler_params=_params(32, dimension_semantics=("arbitrary", "arbitrary")),
    )(u, dz, *grads)


def _adamw(w, g, m, v):
    m = ADAM_B1 * m + (1.0 - ADAM_B1) * g
    v = ADAM_B2 * v + (1.0 - ADAM_B2) * (g * g)
    m_hat = m / (1.0 - ADAM_B1 ** ADAM_STEP)
    v_hat = v / (1.0 - ADAM_B2 ** ADAM_STEP)
    delta = -ADAM_LR * (m_hat / (jnp.sqrt(v_hat) + ADAM_EPS) + ADAM_WD * w)
    return delta, m, v


def _pack_small(partials):
    rows = sorted(partials)

    def body(*refs):
        ins, out_ref = refs[:-1], refs[-1]
        out_ref[...] = jnp.zeros_like(out_ref)
        for j, row in enumerate(rows):
            out_ref[row:row + 1, :] = jnp.sum(ins[j][...], axis=0, keepdims=True)

    return pl.pallas_call(
        body, name="pack_small", out_shape=jax.ShapeDtypeStruct((N_SMALL, D), F32),
    )(*[partials[row] for row in rows])


def _sum_adam(name, recvs, w, m, v, rows):
    r, c = w.shape
    n = len(recvs)

    def body(*refs):
        w_ref, m_ref, v_ref, g_ref, d_ref, mo_ref, vo_ref = refs[n:]

        def finish(recv_ref):
            g = recv_ref[0]
            for s in range(1, recv_ref.shape[0]):
                g = g + recv_ref[s]
            g_ref[...] = g
            d_ref[...], mo_ref[...], vo_ref[...] = _adamw(w_ref[...], g, m_ref[...], v_ref[...])

        if n == 1:
            finish(refs[0])
        else:
            for side in range(n):
                pl.when(lax.axis_index("x") == side)(functools.partial(finish, refs[side]))

    blk = pl.BlockSpec((rows, c), lambda i: (i, 0))
    return pl.pallas_call(
        body, name=name, grid=(r // rows,),
        out_shape=[jax.ShapeDtypeStruct((r, c), F32)] * 4,
        in_specs=[pl.BlockSpec((rv.shape[0], rows, c), lambda i: (0, i, 0)) for rv in recvs]
        + [blk, blk, blk],
        out_specs=[blk] * 4,
        compiler_params=_params(48, dimension_semantics=("arbitrary",)),
    )(*recvs, w, m, v)


def _small_adam(gathered, lb_logits, w, m, v):
    def body(ga_ref, lbl_ref, w_ref, m_ref, v_ref, g_ref, d_ref, mo_ref, vo_ref, loss_ref):
        g = ga_ref[0]
        for s in range(1, N_DEV):
            g = g + ga_ref[s]
        s0, s1 = _lower_bound(lbl_ref[...])
        d_lb = g[R_LB0:R_LB0 + 1, :]
        rows = lax.broadcasted_iota(jnp.int32, (N_SMALL, D), 0)
        g = jnp.where(rows == R_LB0, d_lb * s0 * (1.0 - s0), g)
        g = jnp.where(rows == R_LB1, -d_lb * s0 * s1, g)
        g_ref[...] = g
        d_ref[...], mo_ref[...], vo_ref[...] = _adamw(w_ref[...], g, m_ref[...], v_ref[...])
        loss_ref[...] = (0.5 / D) * jnp.sum(g[R_LOSS:R_LOSS + 1, :], axis=-1, keepdims=True)

    return pl.pallas_call(
        body, name="small_adam",
        out_shape=[jax.ShapeDtypeStruct((N_SMALL, D), F32)] * 4 + [jax.ShapeDtypeStruct((1, 1), F32)],
    )(gathered, lb_logits, w, m, v)


def _pad_rows(a, rows):
    return jnp.pad(a, ((0, rows - a.shape[0]), (0, 0)))


def _pack_rows(rows):
    rows = [r.reshape(-1, D) for r in rows]
    packed = jnp.concatenate(rows, axis=0)
    return _pad_rows(packed, N_SMALL)


def kernel(x, p, ln_g, w_in, conv_w, conv_b, cnorm_g, cnorm_b, w_pw2, b_pw2, lb_logits, onorm_g, w_out, pe_norm_g, w_pg, w_pp, final_g, loss_target, m_ln_g, m_w_in, m_conv_w, m_conv_b, m_cnorm_g, m_cnorm_b, m_w_pw2, m_b_pw2, m_lb_logits, m_onorm_g, m_w_out, m_pe_norm_g, m_w_pg, m_w_pp, m_final_g, v_ln_g, v_w_in, v_conv_w, v_conv_b, v_cnorm_g, v_cnorm_b, v_w_pw2, v_b_pw2, v_lb_logits, v_onorm_g, v_w_out, v_pe_norm_g, v_w_pg, v_w_pp, v_final_g):
    t = x.shape[1]
    x2 = x.reshape(t, D)
    p2 = p.reshape(t, PLE)
    tg2 = loss_target.reshape(t, D)
    fin_g = final_g.reshape(1, D)

    (w_in_all,) = _exchange_call("gather_w_in", [w_in[0].astype(BF16)], ["gather"])
    z, u, conv_w_all, w_pw2_all, w_out_all, w_pg_all, w_pp_all = _inproj_fwd(
        x2, ln_g, w_in_all,
        [_pad_rows(conv_w[0], CONV_PAD), w_pw2[0].astype(BF16), w_out[0].astype(BF16),
         w_pg[0].astype(BF16), w_pp[0].astype(BF16)])
    w_pw2_full = w_pw2_all.reshape(D, D)
    w_out_full = w_out_all.reshape(2 * D, D)
    w_pg_full = w_pg_all.reshape(D, D)

    yc, y2, y_conv = _conv_fwd(z, conv_w_all, conv_b, cnorm_g, cnorm_b, w_pw2_full, b_pw2)
    o_raw, y_hgrn, states = _hgrn_fwd(z, lb_logits, onorm_g)

    (d_h1, dy_conv, dy_hgrn, d_w_out, d_w_pg, d_w_pp, d_pen_p, d_fin_p, loss_p) = _tail(
        x2, y_conv, y_hgrn, p2, tg2, w_out_full, w_pg_full, w_pp_all, pe_norm_g, fin_g)

    dz, d_on_p, d_lb_p, r_w_out, r_w_pg, r_w_pp = _hgrn_bwd(
        dy_hgrn, z, o_raw, states, lb_logits, onorm_g,
        [d_w_out.reshape(N_DEV, 2 * D // N_DEV, D), d_w_pg.reshape(N_DEV, D // N_DEV, D), d_w_pp])
    (d_w_in_hi,) = _inproj_bwd_dw("inproj_bwd_dw_hi", u, dz, N_DEV // 2, N_DEV // 2)
    dz, d_w_pw2, d_conv_w, d_b2_p, d_cng_p, d_cnb_p, d_cb_p, r_w_in_hi = _conv_bwd(
        dy_conv, z, yc, y2, conv_w_all, cnorm_g, cnorm_b, w_pw2_full, dz, [d_w_in_hi], [1])
    d_w_in_lo, r_w_pw2, r_conv_w = _inproj_bwd_dw(
        "inproj_bwd_dw_lo", u, dz, 0, N_DEV // 2,
        [d_w_pw2.reshape(N_DEV, D // N_DEV, D), d_conv_w], ["scatter", "scatter"])
    chip_lo = _pair_reduce("pair_reduce_lo", d_w_in_lo)
    grad_x, d_ln_p, r_w_in_lo = _inproj_bwd_dx(
        dz, x2, d_h1, ln_g, w_in_all, [chip_lo], [("chip", 0)])

    small = _pack_small({R_LN: d_ln_p, R_CONVB: d_cb_p, R_CNG: d_cng_p, R_CNB: d_cnb_p,
                         R_BPW2: d_b2_p, R_LB0: d_lb_p, R_ON: d_on_p, R_PEN: d_pen_p,
                         R_FIN: d_fin_p, R_LOSS: loss_p})
    (small_all,) = _exchange_call("gather_small", [small], ["gather"])

    big = {}
    big["w_in"] = _sum_adam("adam_w_in", [r_w_in_lo, r_w_in_hi], w_in[0], m_w_in[0], v_w_in[0], 128)
    cw = _sum_adam("adam_conv_w", [r_conv_w], _pad_rows(conv_w[0], CONV_PAD),
                   _pad_rows(m_conv_w[0], CONV_PAD), _pad_rows(v_conv_w[0], CONV_PAD), CONV_PAD)
    big["conv_w"] = [a[:CONV_K] for a in cw]
    big["w_pw2"] = _sum_adam("adam_w_pw2", [r_w_pw2], w_pw2[0], m_w_pw2[0], v_w_pw2[0], 128)
    big["w_out"] = _sum_adam("adam_w_out", [r_w_out], w_out[0], m_w_out[0], v_w_out[0], 128)
    big["w_pg"] = _sum_adam("adam_w_pg", [r_w_pg], w_pg[0], m_w_pg[0], v_w_pg[0], 128)
    big["w_pp"] = _sum_adam("adam_w_pp", [r_w_pp], w_pp[0], m_w_pp[0], v_w_pp[0], PLE)

    small_w = [ln_g, conv_b, cnorm_g, cnorm_b, b_pw2, lb_logits, onorm_g, pe_norm_g, final_g]
    small_m = [m_ln_g, m_conv_b, m_cnorm_g, m_cnorm_b, m_b_pw2, m_lb_logits, m_onorm_g,
               m_pe_norm_g, m_final_g]
    small_v = [v_ln_g, v_conv_b, v_cnorm_g, v_cnorm_b, v_b_pw2, v_lb_logits, v_onorm_g,
               v_pe_norm_g, v_final_g]
    sg, sd, sm, sv, loss = _small_adam(small_all, lb_logits, _pack_rows(small_w),
                                       _pack_rows(small_m), _pack_rows(small_v))

    small_rows = {"ln_g": (R_LN, 1), "conv_b": (R_CONVB, 1), "cnorm_g": (R_CNG, 1),
                  "cnorm_b": (R_CNB, 1), "b_pw2": (R_BPW2, 1), "lb_logits": (R_LB0, 2),
                  "onorm_g": (R_ON, 1), "pe_norm_g": (R_PEN, 1), "final_g": (R_FIN, 1)}
    order = ["ln_g", "w_in", "conv_w", "conv_b", "cnorm_g", "cnorm_b", "w_pw2", "b_pw2",
             "lb_logits", "onorm_g", "w_out", "pe_norm_g", "w_pg", "w_pp", "final_g"]

    def leaf(kind, name):
        if name in big:
            return big[name][kind][None]
        r0, n = small_rows[name]
        a = (sg, sd, sm, sv)[kind][r0:r0 + n]
        return a.reshape(D) if name == "final_g" else a

    outs = [loss.reshape(()), grad_x.reshape(1, t, D)]
    for kind in range(4):
        outs += [leaf(kind, name) for name in order]
    return tuple(outs)
```

```python
import functools

import jax
import jax.numpy as jnp
from jax import lax
from jax.experimental import pallas as pl
from jax.experimental.pallas import tpu as pltpu

F32 = jnp.float32
BF16 = jnp.bfloat16
MESH = pl.DeviceIdType.MESH

N_DEV = 8
D = 1024
N_COLS = 7 * D
COLS_PER_DEV = N_COLS // N_DEV
PLE = 256
HEAD = 128
N_HEADS = D // HEAD
CONV_K = 31
CONV_PAD = 32
CHUNK = 64
EPS = 1e-6
SUBLANES = 8

ADAM_LR = 0.001
ADAM_B1 = 0.9
ADAM_B2 = 0.999
ADAM_EPS = 1e-08
ADAM_WD = 0.01
ADAM_STEP = 10

MIB = 1024 * 1024
N_SMALL = 16
R_LN, R_CONVB, R_CNG, R_CNB, R_BPW2, R_LB0, R_LB1, R_ON, R_PEN, R_FIN, R_LOSS = range(11)


def _params(vmem_mib, **kw):
    return pltpu.CompilerParams(vmem_limit_bytes=vmem_mib * MIB, **kw)


def _dot(a, b):
    return jnp.dot(a.astype(BF16), b.astype(BF16), preferred_element_type=F32)


def _dot_nt(a, b):
    return lax.dot_general(a.astype(BF16), b.astype(BF16), (((1,), (1,)), ((), ())),
                           preferred_element_type=F32)


def _dot_tn(a, b):
    return lax.dot_general(a.astype(BF16), b.astype(BF16), (((0,), (0,)), ((), ())),
                           preferred_element_type=F32)


def _split(a):
    hi = a.astype(BF16)
    return hi, (a - hi.astype(F32)).astype(BF16)


def _dot_split(a, b, dims):
    ah, al = _split(a)
    bh, bl = _split(b)
    dg = lambda p, q: lax.dot_general(p, q, dims, preferred_element_type=F32)
    return dg(ah, bh) + (dg(ah, bl) + dg(al, bh))


def _sigmoid(x):
    return 1.0 / (1.0 + jnp.exp(-x))


def _rowsum8(a):
    r, c = a.shape
    return jnp.sum(a.reshape(r // SUBLANES, SUBLANES, c), axis=0)


def _tri_dot(tri, a):
    hi = a.astype(BF16)
    r1 = a - hi.astype(F32)
    mid = r1.astype(BF16)
    lo = (r1 - mid.astype(F32)).astype(BF16)
    return (jnp.dot(tri, hi, preferred_element_type=F32)
            + jnp.dot(tri, mid, preferred_element_type=F32)
            + jnp.dot(tri, lo, preferred_element_type=F32))


def _lower_bound(lbl):
    l0, l1 = lbl[0:1, :], lbl[1:2, :]
    m = jnp.maximum(l0, l1)
    e0, e1 = jnp.exp(l0 - m), jnp.exp(l1 - m)
    s = e0 + e1
    return e0 / s, e1 / s


ANY = pl.BlockSpec(memory_space=pl.ANY)


def _full(shape):
    return pl.BlockSpec(shape, lambda i: (0,) * len(shape))


def _peer(x, y, c, k):
    px = 1 - x if k & 4 else x
    py = 1 - y if k & 2 else y
    pc = 1 - c if k & 1 else c
    return (px, py, pc), 4 * px + 2 * py + pc


class _Exchange:
    def __init__(self, srcs, outs, modes, send_sems, recv_sems, local_sems):
        x, y, c = lax.axis_index("x"), lax.axis_index("y"), lax.axis_index("c")
        me = 4 * x + 2 * y + c
        self.starts, self.send_waits, self.recv_waits = [], [], []

        def remote(a, k, src, slot, peer, when):
            sem = a * N_DEV + k
            cp = pltpu.make_async_remote_copy(
                src_ref=src, dst_ref=outs[a].at[slot], send_sem=send_sems.at[sem],
                recv_sem=recv_sems.at[sem], device_id=peer, device_id_type=MESH)
            self.starts.append((when, cp.start))
            self.send_waits.append((when, cp.wait_send))

        def arrival(a, k, slot, when):
            sem = a * N_DEV + k
            cp = pltpu.make_async_remote_copy(
                src_ref=outs[a].at[slot], dst_ref=outs[a].at[slot], send_sem=send_sems.at[sem],
                recv_sem=recv_sems.at[sem], device_id=(x, y, c), device_id_type=MESH)
            self.recv_waits.append((when, cp.wait_recv))

        def local(a, src, slot, when):
            cp = pltpu.make_async_copy(src, outs[a].at[slot], local_sems.at[a])
            self.starts.append((when, cp.start))
            self.send_waits.append((when, cp.wait))

        for a, (src, mode) in enumerate(zip(srcs, modes)):
            if mode in ("gather", "scatter"):
                local(a, src if mode == "gather" else src.at[me], me, None)
                for k in range(1, N_DEV):
                    peer, peer_idx = _peer(x, y, c, k)
                    remote(a, k, src if mode == "gather" else src.at[peer_idx], me, peer, None)
                    arrival(a, k, peer_idx, None)
                continue
            if isinstance(mode, tuple):
                here, away = x == mode[1], x != mode[1]
                chip = 2 * x + y
                local(a, src.at[y], chip, here)
                remote(a, 1, src.at[1 - y], chip, (x, 1 - y, c), here)
                remote(a, 2, src.at[y], chip, (1 - x, y, c), away)
                remote(a, 3, src.at[1 - y], chip, (1 - x, 1 - y, c), away)
                arrival(a, 1, 2 * x + 1 - y, here)
                arrival(a, 2, 2 * (1 - x) + y, here)
                arrival(a, 3, 2 * (1 - x) + 1 - y, here)
                continue
            here, away = x == mode, x != mode
            for kk in range(4):
                py = 1 - y if kk & 2 else y
                pc = 1 - c if kk & 1 else c
                block = src.at[2 * py + pc]
                if kk == 0:
                    local(a, block, me, here)
                else:
                    remote(a, kk, block, me, (x, py, pc), here)
                remote(a, 4 + kk, block, me, (1 - x, py, pc), away)
            for k in range(1, N_DEV):
                arrival(a, k, _peer(x, y, c, k)[1], here)

    @staticmethod
    def _run(actions):
        for when, fn in actions:
            if when is None:
                fn()
            else:
                pl.when(when)(fn)

    def start(self):
        self._run(self.starts)

    def wait(self):
        self._run(self.recv_waits)
        self._run(self.send_waits)


def _exchange_scratch(n):
    return [pltpu.SemaphoreType.DMA((n * N_DEV,)), pltpu.SemaphoreType.DMA((n * N_DEV,)),
            pltpu.SemaphoreType.DMA((n,))]


def _recv_shapes(srcs, modes):
    def shape(s, m):
        if m == "gather":
            return (N_DEV,) + s.shape
        return (N_DEV // 2 if isinstance(m, tuple) else N_DEV,) + s.shape[1:]

    return [jax.ShapeDtypeStruct(shape(s, m), s.dtype) for s, m in zip(srcs, modes)]


def _pair_reduce(name, blocks):
    shape = (2,) + blocks.shape[1:]

    def body(src, out_ref, stage, mine, send_sems, recv_sems, local_sems):
        x, y, c = lax.axis_index("x"), lax.axis_index("y"), lax.axis_index("c")
        sends, waits = [], []
        for py in range(2):
            sends.append(pltpu.make_async_remote_copy(
                src_ref=src.at[2 * py + 1 - c], dst_ref=stage.at[py], send_sem=send_sems.at[py],
                recv_sem=recv_sems.at[py], device_id=(x, y, 1 - c), device_id_type=MESH))
            waits.append(pltpu.make_async_copy(src.at[2 * py + c], mine.at[py], local_sems.at[py]))
        for cp in sends + waits:
            cp.start()
        for cp in waits:
            cp.wait()
        for cp in sends:
            cp.wait_recv()
        out_ref[...] = mine[...] + stage[...]
        for cp in sends:
            cp.wait_send()

    return pl.pallas_call(
        body, name=name, out_shape=jax.ShapeDtypeStruct(shape, F32), in_specs=[ANY],
        scratch_shapes=[pltpu.VMEM(shape, F32), pltpu.VMEM(shape, F32),
                        pltpu.SemaphoreType.DMA((2,)), pltpu.SemaphoreType.DMA((2,)),
                        pltpu.SemaphoreType.DMA((2,))],
        compiler_params=_params(40),
    )(blocks)


def _exchange_call(name, srcs, modes):
    n = len(srcs)

    def body(*refs):
        xch = _Exchange(refs[:n], refs[n:2 * n], modes, *refs[2 * n:])
        xch.start()
        xch.wait()

    return pl.pallas_call(
        body, name=name, out_shape=_recv_shapes(srcs, modes),
        in_specs=[ANY] * n, out_specs=[ANY] * n, scratch_shapes=_exchange_scratch(n),
    )(*srcs)


def _hosted(body, n_in, n_out, grid, modes):
    n = len(modes)

    def hosted(*refs):
        ins, srcs = refs[:n_in], refs[n_in:n_in + n]
        outs = refs[n_in + n:n_in + n + n_out]
        bufs = refs[n_in + n + n_out:n_in + 2 * n + n_out]
        scratch = refs[n_in + 2 * n + n_out:-3]
        xch = _Exchange(srcs, bufs, modes, *refs[-3:])
        first, last = True, True
        for axis, size in enumerate(grid):
            first = jnp.logical_and(first, pl.program_id(axis) == 0)
            last = jnp.logical_and(last, pl.program_id(axis) == size - 1)
        pl.when(first)(xch.start)
        body(*ins, *outs, *scratch)
        pl.when(last)(xch.wait)

    return hosted


GATHER_ORDER = (0, 1, 2, 4, 3, 5, 6, 7)


def _inproj_fwd(x, ln_g, w_shard, other_shards):
    t = x.shape[0]
    tt = min(1024, t)
    n_t = t // tt
    modes = ["gather"] * len(other_shards)
    n = len(modes)
    me = 4 * lax.axis_index("x") + 2 * lax.axis_index("y") + lax.axis_index("c")
    order = jnp.bitwise_xor(me, jnp.array(GATHER_ORDER, jnp.int32)).astype(jnp.int32)

    def body(order_ref, x_ref, g_ref, shard_hbm, *refs):
        srcs = refs[:n]
        z_ref, u_ref, w_all = refs[n:n + 3]
        bufs = refs[n + 3:2 * n + 3]
        u_all, w_blk, w_send, w_recv, w_local = refs[2 * n + 3:2 * n + 8]
        j, i = pl.program_id(0), pl.program_id(1)
        x, y, c = lax.axis_index("x"), lax.axis_index("y"), lax.axis_index("c")
        mine = 4 * x + 2 * y + c
        others = _Exchange(srcs, bufs, modes, *refs[2 * n + 8:])

        def push(k):
            peer, _ = _peer(x, y, c, k)
            return pltpu.make_async_remote_copy(
                src_ref=shard_hbm, dst_ref=w_all.at[mine], send_sem=w_send.at[k],
                recv_sem=w_recv.at[k], device_id=peer, device_id_type=MESH)

        def landed(k):
            _, owner = _peer(x, y, c, k)
            return pltpu.make_async_remote_copy(
                src_ref=w_all.at[owner], dst_ref=w_all.at[owner], send_sem=w_send.at[k],
                recv_sem=w_recv.at[k], device_id=(x, y, c), device_id_type=MESH)

        keep = pltpu.make_async_copy(shard_hbm, w_all.at[mine], w_local.at[0])

        @pl.when(jnp.logical_and(j == 0, i == 0))
        def _():
            for k in GATHER_ORDER[1:]:
                push(k).start()
            keep.start()
            others.start()
            pltpu.sync_copy(shard_hbm, w_blk)

        for step, k in enumerate(GATHER_ORDER[1:], start=1):
            @pl.when(jnp.logical_and(j == step, i == 0))
            def _(k=k):
                landed(k).wait_recv()
                pltpu.sync_copy(w_all.at[_peer(x, y, c, k)[1]], w_blk)

        rows = pl.ds(pl.multiple_of(i * tt, tt), tt)

        @pl.when(j == 0)
        def _():
            xv = x_ref[...]
            rstd = lax.rsqrt(jnp.mean(xv * xv, axis=-1, keepdims=True) + EPS)
            ub = (xv * rstd * g_ref[...]).astype(BF16)
            u_ref[...] = ub
            u_all[rows, :] = ub

        z_ref[...] = jnp.dot(u_all[rows, :], w_blk[...], preferred_element_type=F32)

        @pl.when(jnp.logical_and(j == N_DEV - 1, i == n_t - 1))
        def _():
            for k in GATHER_ORDER[1:]:
                push(k).wait_send()
            keep.wait()
            others.wait()

    first_pass = lambda j, i, order_ref: (jnp.where(j == 0, i, n_t - 1), 0)
    grid_spec = pltpu.PrefetchScalarGridSpec(
        num_scalar_prefetch=1, grid=(N_DEV, n_t),
        in_specs=[pl.BlockSpec((tt, D), first_pass),
                  pl.BlockSpec((1, D), lambda j, i, order_ref: (0, 0)), ANY] + [ANY] * n,
        out_specs=[pl.BlockSpec((tt, COLS_PER_DEV), lambda j, i, order_ref: (i, order_ref[j])),
                   pl.BlockSpec((tt, D), first_pass), ANY] + [ANY] * n,
        scratch_shapes=[pltpu.VMEM((t, D), BF16), pltpu.VMEM((D, COLS_PER_DEV), BF16),
                        pltpu.SemaphoreType.DMA((N_DEV,)), pltpu.SemaphoreType.DMA((N_DEV,)),
                        pltpu.SemaphoreType.DMA((1,))] + _exchange_scratch(n))
    return pl.pallas_call(
        body, name="inproj_fwd", grid_spec=grid_spec,
        out_shape=[jax.ShapeDtypeStruct((t, N_COLS), F32), jax.ShapeDtypeStruct((t, D), BF16),
                   jax.ShapeDtypeStruct((N_DEV,) + w_shard.shape, BF16)]
        + _recv_shapes(other_shards, modes),
        compiler_params=_params(48, dimension_semantics=("arbitrary", "arbitrary")),
    )(order, x, ln_g, w_shard, *other_shards)


def _shifted_copies(buf, shifted, rows):
    for b in range(1, SUBLANES):
        shifted[b, 0:rows, :] = buf[b:b + rows, :]


def _tap_ref(buf, shifted, offset):
    a, b = divmod(offset, SUBLANES)
    return (buf if b == 0 else shifted.at[b]), SUBLANES * a


def _group_norm_stats(blk):
    mu = jnp.mean(blk, axis=-1, keepdims=True)
    cen = blk - mu
    var = jnp.mean(cen * cen, axis=-1, keepdims=True)
    return cen * lax.rsqrt(var + EPS)


def _conv_fwd(z, conv_w_all, conv_b, cn_g, cn_b, w_pw2, b_pw2):
    t = z.shape[0]
    tt = min(256, t)
    rc = 32

    def body(val_ref, glu_ref, gate_ref, cw_ref, cb_ref, g_ref, b_ref, w_hbm, b2_ref,
             yc_ref, y2_ref, yo_ref, w_vmem, vbuf, vsh, y1buf):
        @pl.when(pl.program_id(0) == 0)
        def _():
            pltpu.sync_copy(w_hbm, w_vmem)
            vbuf[0:CONV_PAD, :] = jnp.zeros((CONV_PAD, D), F32)

        vbuf[CONV_PAD:CONV_PAD + tt, :] = val_ref[...] * _sigmoid(glu_ref[...])
        _shifted_copies(vbuf, vsh, tt + 24)

        def row_chunk(r, carry):
            r0 = pl.multiple_of(r * rc, rc)
            for g in range(N_HEADS):
                cs = slice(HEAD * g, HEAD * (g + 1))
                acc = jnp.zeros((rc, HEAD), F32)
                for k in range(CONV_K):
                    ref, off = _tap_ref(vbuf, vsh, k + 2)
                    acc = acc + cw_ref[g, k:k + 1, :] * ref[pl.ds(r0 + off, rc), cs]
                acc = acc + cb_ref[:, cs]
                yc_ref[pl.ds(r0, rc), cs] = acc
                n = _group_norm_stats(acc) * g_ref[:, cs] + b_ref[:, cs]
                y1buf[pl.ds(r0, rc), cs] = (n * _sigmoid(n)).astype(BF16)
            return carry

        lax.fori_loop(0, tt // rc, row_chunk, 0)
        vbuf[0:CONV_PAD, :] = vbuf[tt:tt + CONV_PAD, :]
        y2 = jnp.dot(y1buf[...], w_vmem[...], preferred_element_type=F32) + b2_ref[...]
        y2_ref[...] = y2
        gate = gate_ref[...]
        yo_ref[...] = (y2 * gate * _sigmoid(gate)).astype(BF16)

    col = lambda j: pl.BlockSpec((tt, D), lambda i: (i, j))
    row = pl.BlockSpec((tt, D), lambda i: (i, 0))
    return pl.pallas_call(
        body, name="conv_fwd", grid=(t // tt,),
        out_shape=[jax.ShapeDtypeStruct((t, D), F32), jax.ShapeDtypeStruct((t, D), F32),
                   jax.ShapeDtypeStruct((t, D), BF16)],
        in_specs=[col(0), col(1), col(2), _full((N_DEV, CONV_PAD, HEAD)), _full((1, D)),
                  _full((1, D)), _full((1, D)), ANY, _full((1, D))],
        out_specs=[row, row, row],
        scratch_shapes=[pltpu.VMEM((D, D), BF16), pltpu.VMEM((tt + CONV_PAD, D), F32),
                        pltpu.VMEM((SUBLANES, tt + CONV_PAD, D), F32), pltpu.VMEM((tt, D), BF16)],
        compiler_params=_params(48, dimension_semantics=("arbitrary",)),
    )(z, z, z, conv_w_all, conv_b, cn_g, cn_b, w_pw2, b_pw2)


def _chunk_quantities(zq, zf, lbh, tri):
    sig = _sigmoid(zf)
    sig_neg = _sigmoid(-zf)
    f = lbh + (1.0 - lbh) * sig
    k = (1.0 - lbh) * sig_neg
    q = zq * _sigmoid(zq)
    b = _tri_dot(tri, jnp.log(f))
    b_mid = b[CHUNK // 2 - 1:CHUNK // 2, :]
    b_last = b[CHUNK - 1:CHUNK, :]
    e_q = jnp.exp(b)
    e_qm = jnp.exp(b - b_mid)
    e_km = jnp.exp(b_mid - b)
    e_kd = jnp.exp(b_last - b)
    return q, k, f, sig, sig_neg, e_q, e_qm, e_km, e_kd, jnp.exp(b_last)


def _hgrn_fwd(z, lb_logits, onorm_g, shards):
    t = z.shape[0]
    tt = min(256, t)
    nc = tt // CHUNK
    modes = ["gather"] * len(shards)
    n = len(modes)

    def body(q_ref, f_ref, i_ref, g_ref, lbl_ref, on_ref, o_ref, y_ref, s_ref, st):
        @pl.when(pl.program_id(0) == 0)
        def _():
            st[...] = jnp.zeros_like(st)

        lb, _ = _lower_bound(lbl_ref[...])
        rows = lax.broadcasted_iota(jnp.int32, (CHUNK, CHUNK), 0)
        cols = lax.broadcasted_iota(jnp.int32, (CHUNK, CHUNK), 1)
        causal = rows >= cols
        tri = causal.astype(BF16)

        def chunk(c, carry):
            r0 = pl.multiple_of(c * CHUNK, CHUNK)
            rs = pl.ds(r0, CHUNK)
            for h in range(N_HEADS):
                cs = slice(HEAD * h, HEAD * (h + 1))
                q, k, _, _, _, e_q, e_qm, e_km, e_kd, e_last = _chunk_quantities(
                    q_ref[rs, cs], f_ref[rs, cs], lb[:, cs], tri)
                v = i_ref[rs, cs]
                s_old = st[h]
                s_ref[c, h] = s_old
                a = jnp.where(causal, _dot_nt(q * e_qm, k * e_km), 0.0)
                o = _dot_nt(q * e_q, s_old) + _dot(a, v)
                st[h] = s_old * e_last + _dot_tn(v, k * e_kd)
                o_ref[rs, cs] = o
                n = o * lax.rsqrt(jnp.mean(o * o, axis=-1, keepdims=True) + EPS)
                zg = g_ref[rs, cs]
                y_ref[rs, cs] = (n * on_ref[:, cs] * zg * _sigmoid(zg)).astype(BF16)
            return carry

        lax.fori_loop(0, nc, chunk, 0)

    col = lambda j: pl.BlockSpec((tt, D), lambda i: (i, j))
    row = pl.BlockSpec((tt, D), lambda i: (i, 0))
    return pl.pallas_call(
        _hosted(body, 6, 3, (t // tt,), modes), name="hgrn_fwd", grid=(t // tt,),
        out_shape=[jax.ShapeDtypeStruct((t, D), F32), jax.ShapeDtypeStruct((t, D), BF16),
                   jax.ShapeDtypeStruct((t // CHUNK, N_HEADS, HEAD, HEAD), F32)]
        + _recv_shapes(shards, modes),
        in_specs=[col(3), col(4), col(5), col(6), _full((2, D)), _full((1, D))] + [ANY] * n,
        out_specs=[row, row, pl.BlockSpec((nc, N_HEADS, HEAD, HEAD), lambda i: (i, 0, 0, 0))]
        + [ANY] * n,
        scratch_shapes=[pltpu.VMEM((N_HEADS, HEAD, HEAD), F32)] + _exchange_scratch(n),
        compiler_params=_params(40, dimension_semantics=("arbitrary",)),
    )(z, z, z, z, lb_logits, onorm_g, *shards)


def _rms_bwd(dn, xhat, rstd):
    return rstd * (dn - xhat * jnp.mean(dn * xhat, axis=-1, keepdims=True))


def _tail(x, y_conv, y_hgrn, p, target, w_out, w_pg, w_pp_all, pe_g, fin_g):
    t = x.shape[0]
    tt = min(256, t)
    n_steps = t // tt

    def body(x_ref, yc_ref, yh_ref, p_ref, tg_ref, wo_hbm, wg_hbm, wp_hbm, pg_ref, fg_ref,
             dh1_ref, dyc_ref, dyh_ref, dwo_hbm, dwg_hbm, dwp_hbm, dpg_ref, dfg_ref, loss_ref,
             wo, wg, wp, dwo, dwg, dwp):
        i = pl.program_id(0)

        @pl.when(i == 0)
        def _():
            pltpu.sync_copy(wo_hbm, wo)
            pltpu.sync_copy(wg_hbm, wg)
            for d in range(N_DEV):
                pltpu.sync_copy(wp_hbm.at[d], wp.at[:, pl.ds(HEAD * d, HEAD)])
            dwo[...] = jnp.zeros_like(dwo)
            dwg[...] = jnp.zeros_like(dwg)
            dwp[...] = jnp.zeros_like(dwp)
            dpg_ref[...] = jnp.zeros_like(dpg_ref)
            dfg_ref[...] = jnp.zeros_like(dfg_ref)
            loss_ref[...] = jnp.zeros_like(loss_ref)

        ycv, yhv = yc_ref[...], yh_ref[...]
        h1 = (x_ref[...] + jnp.dot(ycv, wo[0:D, :], preferred_element_type=F32)
              + jnp.dot(yhv, wo[D:2 * D, :], preferred_element_type=F32))
        pb = p_ref[...].astype(BF16)
        pe = jnp.dot(pb, wp[...], preferred_element_type=F32)
        rstd1 = lax.rsqrt(jnp.mean(h1 * h1, axis=-1, keepdims=True) + EPS)
        n1 = h1 * rstd1
        rb = (n1 * pg_ref[...]).astype(BF16)
        gate = _sigmoid(jnp.dot(rb, wg[...], preferred_element_type=F32))
        h2 = h1 + gate * pe
        rstd2 = lax.rsqrt(jnp.mean(h2 * h2, axis=-1, keepdims=True) + EPS)
        n2 = h2 * rstd2
        err = n2 * fg_ref[...] - tg_ref[...]
        loss_ref[...] += _rowsum8(err * err)

        d_out = err * (1.0 / D)
        dfg_ref[...] += _rowsum8(d_out * n2)
        d_h2 = _rms_bwd(d_out * fg_ref[...], n2, rstd2)
        d_pe = (d_h2 * gate).astype(BF16)
        d_gpre = (d_h2 * pe * gate * (1.0 - gate)).astype(BF16)
        dwg[...] += _dot_tn(rb, d_gpre)
        dwp[...] += _dot_tn(pb, d_pe)
        dr = _dot_nt(d_gpre, wg[...])
        dpg_ref[...] += _rowsum8(dr * n1)
        d_h1 = d_h2 + _rms_bwd(dr * pg_ref[...], n1, rstd1)
        dh1_ref[...] = d_h1
        d_h1b = d_h1.astype(BF16)
        dwo[0:D, :] += _dot_tn(ycv, d_h1b)
        dwo[D:2 * D, :] += _dot_tn(yhv, d_h1b)
        dyc_ref[...] = _dot_nt(d_h1b, wo[0:D, :])
        dyh_ref[...] = _dot_nt(d_h1b, wo[D:2 * D, :])

        @pl.when(i == n_steps - 1)
        def _():
            pltpu.sync_copy(dwo, dwo_hbm)
            pltpu.sync_copy(dwg, dwg_hbm)
            for d in range(N_DEV):
                pltpu.sync_copy(dwp.at[:, pl.ds(HEAD * d, HEAD)], dwp_hbm.at[d])

    row = pl.BlockSpec((tt, D), lambda i: (i, 0))
    acc = _full((SUBLANES, D))
    return pl.pallas_call(
        body, name="tail_fwd_bwd", grid=(n_steps,),
        out_shape=[jax.ShapeDtypeStruct((t, D), F32)] * 3
        + [jax.ShapeDtypeStruct((2 * D, D), F32), jax.ShapeDtypeStruct((D, D), F32),
           jax.ShapeDtypeStruct((N_DEV, PLE, HEAD), F32)]
        + [jax.ShapeDtypeStruct((SUBLANES, D), F32)] * 3,
        in_specs=[row, row, row, pl.BlockSpec((tt, PLE), lambda i: (i, 0)), row,
                  ANY, ANY, ANY, _full((1, D)), _full((1, D))],
        out_specs=[row, row, row, ANY, ANY, ANY, acc, acc, acc],
        scratch_shapes=[pltpu.VMEM((2 * D, D), BF16), pltpu.VMEM((D, D), BF16),
                        pltpu.VMEM((PLE, D), BF16), pltpu.VMEM((2 * D, D), F32),
                        pltpu.VMEM((D, D), F32), pltpu.VMEM((PLE, D), F32)],
        compiler_params=_params(52, dimension_semantics=("arbitrary",)),
    )(x, y_conv, y_hgrn, p, target, w_out, w_pg, w_pp_all, pe_g, fin_g)


def _hgrn_bwd(dy, z, o_raw, states, lb_logits, onorm_g, grads):
    t = z.shape[0]
    tt = min(256, t)
    nc = tt // CHUNK
    n_steps = t // tt
    modes = ["scatter"] * len(grads)

    def body(dy_ref, q_ref, f_ref, i_ref, g_ref, o_ref, s_ref, lbl_ref, on_ref,
             dz_ref, don_ref, dlb_ref, dst):
        @pl.when(pl.program_id(0) == 0)
        def _():
            dst[...] = jnp.zeros_like(dst)
            don_ref[...] = jnp.zeros_like(don_ref)
            dlb_ref[...] = jnp.zeros_like(dlb_ref)

        lb, _ = _lower_bound(lbl_ref[...])
        rows = lax.broadcasted_iota(jnp.int32, (CHUNK, CHUNK), 0)
        cols = lax.broadcasted_iota(jnp.int32, (CHUNK, CHUNK), 1)
        causal = rows >= cols
        tri = causal.astype(BF16)
        tri_rev = (rows <= cols).astype(BF16)
        is_last = lax.broadcasted_iota(jnp.int32, (CHUNK, HEAD), 0) == CHUNK - 1

        def chunk(cc, carry):
            c = nc - 1 - cc
            r0 = pl.multiple_of(c * CHUNK, CHUNK)
            rs = pl.ds(r0, CHUNK)
            for h in range(N_HEADS):
                cs = slice(HEAD * h, HEAD * (h + 1))
                zq, zf, zg = q_ref[rs, cs], f_ref[rs, cs], g_ref[rs, cs]
                lbh = lb[:, cs]
                q, k, f, sig, sig_neg, e_q, e_qm, e_km, e_kd, e_last = _chunk_quantities(
                    zq, zf, lbh, tri)
                v = i_ref[rs, cs]
                qt, qm, km, kd = q * e_q, q * e_qm, k * e_km, k * e_kd
                a = jnp.where(causal, _dot_nt(qm, km), 0.0)

                o = o_ref[rs, cs]
                rstd = lax.rsqrt(jnp.mean(o * o, axis=-1, keepdims=True) + EPS)
                n = o * rstd
                sg = _sigmoid(zg)
                dyv = dy_ref[rs, cs]
                on = on_ref[:, cs]
                d_zg = dyv * n * on * sg * (1.0 + zg * (1.0 - sg))
                d_on = dyv * zg * sg
                don_ref[:, cs] += _rowsum8(d_on * n)
                do = _rms_bwd(d_on * on, n, rstd)

                s_old = s_ref[c, h]
                ds_new = dst[h]
                da = jnp.where(causal, _dot_nt(do, v), 0.0)
                dv = _dot_tn(a, do) + _dot_nt(kd, ds_new)
                dkd = _dot(v, ds_new)
                dqm = _dot_split(da, km, (((1,), (0,)), ((), ())))
                dkm = _dot_split(da, qm, (((0,), (0,)), ((), ())))
                dq = _dot(do, s_old) * e_q + dqm * e_qm
                dk = dkm * e_km + dkd * e_kd
                dst[h] = ds_new * e_last + _dot_tn(do, qt)
                last = (jnp.sum(dkd * kd, axis=0, keepdims=True)
                        + e_last * jnp.sum(s_old * ds_new, axis=0, keepdims=True))
                db = q * dq - k * dk + jnp.where(is_last, last, 0.0)
                dlogf = _tri_dot(tri_rev, db)
                common = sig_neg * (dlogf / f - dk)
                dlb_ref[:, cs] += _rowsum8(common)
                c0 = 3 * D + HEAD * h
                dz_ref[rs, c0:c0 + HEAD] = (
                    dq * _sigmoid(zq) * (1.0 + zq * (1.0 - _sigmoid(zq)))).astype(BF16)
                dz_ref[rs, D + c0:D + c0 + HEAD] = ((1.0 - lbh) * sig * common).astype(BF16)
                dz_ref[rs, 2 * D + c0:2 * D + c0 + HEAD] = dv.astype(BF16)
                dz_ref[rs, 3 * D + c0:3 * D + c0 + HEAD] = d_zg.astype(BF16)
            return carry

        lax.fori_loop(0, nc, chunk, 0)

    rev = lambda i: n_steps - 1 - i
    col = lambda j: pl.BlockSpec((tt, D), lambda i: (rev(i), j))
    row = pl.BlockSpec((tt, D), lambda i: (rev(i), 0))
    acc = _full((SUBLANES, D))
    n = len(modes)
    return pl.pallas_call(
        _hosted(body, 9, 3, (n_steps,), modes), name="hgrn_bwd", grid=(n_steps,),
        out_shape=[jax.ShapeDtypeStruct((t, N_COLS), BF16),
                   jax.ShapeDtypeStruct((SUBLANES, D), F32),
                   jax.ShapeDtypeStruct((SUBLANES, D), F32)] + _recv_shapes(grads, modes),
        in_specs=[row, col(3), col(4), col(5), col(6), row,
                  pl.BlockSpec((nc, N_HEADS, HEAD, HEAD), lambda i: (rev(i), 0, 0, 0)),
                  _full((2, D)), _full((1, D))] + [ANY] * n,
        out_specs=[pl.BlockSpec((tt, N_COLS), lambda i: (rev(i), 0)), acc, acc] + [ANY] * n,
        scratch_shapes=[pltpu.VMEM((N_HEADS, HEAD, HEAD), F32)] + _exchange_scratch(n),
        compiler_params=_params(48, dimension_semantics=("arbitrary",)),
    )(dy, z, z, z, z, o_raw, states, lb_logits, onorm_g, *grads)


def _conv_bwd(dy, z, yc, y2, conv_w_all, cn_g, cn_b, w_pw2, dz, grads, modes):
    t = z.shape[0]
    tt = min(256, t)
    rc = 32
    n_steps = t // tt

    def body(dy_ref, val_ref, glu_ref, gate_ref, yc_ref, y2_ref, cw_ref, g_ref, b_ref, w_hbm,
             dz_in, dz_ref, dw_hbm, dcw_out, db2_ref, dg_ref, dbeta_ref, dcb_ref,
             w_vmem, dw, dbuf, dsh, y1buf, dnbuf, dcw_ref):
        i = pl.program_id(0)

        @pl.when(i == 0)
        def _():
            pltpu.sync_copy(w_hbm, w_vmem)
            dw[...] = jnp.zeros_like(dw)
            dbuf[tt:tt + CONV_PAD, :] = jnp.zeros((CONV_PAD, D), F32)
            dcw_ref[...] = jnp.zeros_like(dcw_ref)
            dcw_out[...] = jnp.zeros_like(dcw_out)
            db2_ref[...] = jnp.zeros_like(db2_ref)
            dg_ref[...] = jnp.zeros_like(dg_ref)
            dbeta_ref[...] = jnp.zeros_like(dbeta_ref)
            dcb_ref[...] = jnp.zeros_like(dcb_ref)

        gate = gate_ref[...]
        sg = _sigmoid(gate)
        dyv = dy_ref[...]
        dy2 = dyv * gate * sg
        dz_ref[:, 2 * D:3 * D] = (dyv * y2_ref[...] * sg * (1.0 + gate * (1.0 - sg))).astype(BF16)
        db2_ref[...] += _rowsum8(dy2)
        dy2b = dy2.astype(BF16)
        dnbuf[...] = _dot_nt(dy2b, w_vmem[...])

        def norm_chunk(r, carry):
            r0 = pl.multiple_of(r * rc, rc)
            rs = pl.ds(r0, rc)
            for g in range(N_HEADS):
                cs = slice(HEAD * g, HEAD * (g + 1))
                blk = yc_ref[rs, cs]
                mu = jnp.mean(blk, axis=-1, keepdims=True)
                cen = blk - mu
                rstd = lax.rsqrt(jnp.mean(cen * cen, axis=-1, keepdims=True) + EPS)
                xhat = cen * rstd
                n = xhat * g_ref[:, cs] + b_ref[:, cs]
                sn = _sigmoid(n)
                y1buf[rs, cs] = (n * sn).astype(BF16)
                dn = dnbuf[rs, cs] * sn * (1.0 + n * (1.0 - sn))
                dg_ref[:, cs] += _rowsum8(dn * xhat)
                dbeta_ref[:, cs] += _rowsum8(dn)
                dxh = dn * g_ref[:, cs]
                dyc = rstd * (dxh - jnp.mean(dxh, axis=-1, keepdims=True)
                              - xhat * jnp.mean(dxh * xhat, axis=-1, keepdims=True))
                dcb_ref[:, cs] += _rowsum8(dyc)
                dbuf[rs, cs] = dyc
            return carry

        lax.fori_loop(0, tt // rc, norm_chunk, 0)
        dw[...] += _dot_tn(y1buf[...], dy2b)
        _shifted_copies(dbuf, dsh, tt + 24)

        def conv_chunk(r, carry):
            r0 = pl.multiple_of(r * rc, rc)
            rs = pl.ds(r0, rc)
            for g in range(N_HEADS):
                cs = slice(HEAD * g, HEAD * (g + 1))
                sglu = _sigmoid(glu_ref[rs, cs])
                val = val_ref[rs, cs]
                v = val * sglu
                dv = jnp.zeros((rc, HEAD), F32)
                for k in range(CONV_K):
                    ref, off = _tap_ref(dbuf, dsh, CONV_K - 1 - k)
                    d_later = ref[pl.ds(r0 + off, rc), cs]
                    dv = dv + cw_ref[g, k:k + 1, :] * d_later
                    dcw_ref[g, k] += _rowsum8(v * d_later)
                dz_ref[rs, cs] = (dv * sglu).astype(BF16)
                dz_ref[rs, D + HEAD * g:D + HEAD * (g + 1)] = (
                    dv * val * sglu * (1.0 - sglu)).astype(BF16)
            return carry

        lax.fori_loop(0, tt // rc, conv_chunk, 0)
        dbuf[tt:tt + CONV_PAD, :] = dbuf[0:CONV_PAD, :]

        @pl.when(i == n_steps - 1)
        def _():
            pltpu.sync_copy(dw, dw_hbm)
            for g in range(N_HEADS):
                for k in range(CONV_K):
                    dcw_out[g, k:k + 1, :] = jnp.sum(dcw_ref[g, k], axis=0, keepdims=True)

    rev = lambda i: n_steps - 1 - i
    col = lambda j: pl.BlockSpec((tt, D), lambda i: (rev(i), j))
    row = pl.BlockSpec((tt, D), lambda i: (rev(i), 0))
    acc = _full((SUBLANES, D))
    n = len(modes)
    return pl.pallas_call(
        _hosted(body, 11, 7, (n_steps,), modes), name="conv_bwd", grid=(n_steps,),
        out_shape=[jax.ShapeDtypeStruct((t, N_COLS), BF16), jax.ShapeDtypeStruct((D, D), F32),
                   jax.ShapeDtypeStruct((N_DEV, CONV_PAD, HEAD), F32)]
        + [jax.ShapeDtypeStruct((SUBLANES, D), F32)] * 4 + _recv_shapes(grads, modes),
        in_specs=[row, col(0), col(1), col(2), row, row, _full((N_DEV, CONV_PAD, HEAD)),
                  _full((1, D)), _full((1, D)), ANY, ANY] + [ANY] * n,
        out_specs=[pl.BlockSpec((tt, 3 * D), lambda i: (rev(i), 0)), ANY,
                   _full((N_DEV, CONV_PAD, HEAD)), acc, acc, acc, acc] + [ANY] * n,
        input_output_aliases={10: 0},
        scratch_shapes=[pltpu.VMEM((D, D), BF16), pltpu.VMEM((D, D), F32),
                        pltpu.VMEM((tt + CONV_PAD, D), F32),
                        pltpu.VMEM((SUBLANES, tt + CONV_PAD, D), F32),
                        pltpu.VMEM((tt, D), BF16), pltpu.VMEM((tt, D), F32),
                        pltpu.VMEM((N_DEV, CONV_PAD, SUBLANES, HEAD), F32)] + _exchange_scratch(n),
        compiler_params=_params(52, dimension_semantics=("arbitrary",)),
    )(dy, z, z, z, yc, y2, conv_w_all, cn_g, cn_b, w_pw2, dz, *grads)


def _inproj_bwd_dx(dz, x, d_h1, ln_g, w_in_all, grads, modes):
    t = x.shape[0]
    tt = min(256, t)

    def body(dz_ref, x_ref, dh1_ref, g_ref, w_hbm, dx_ref, dg_ref, w_vmem):
        @pl.when(pl.program_id(0) == 0)
        def _():
            pltpu.sync_copy(w_hbm, w_vmem)
            dg_ref[...] = jnp.zeros_like(dg_ref)

        du = jnp.zeros((tt, D), F32)
        for d in range(N_DEV):
            du = du + lax.dot_general(
                dz_ref[:, COLS_PER_DEV * d:COLS_PER_DEV * (d + 1)], w_vmem[d],
                (((1,), (1,)), ((), ())), preferred_element_type=F32)
        xv = x_ref[...]
        rstd = lax.rsqrt(jnp.mean(xv * xv, axis=-1, keepdims=True) + EPS)
        xhat = xv * rstd
        dg_ref[...] += _rowsum8(du * xhat)
        dx_ref[...] = dh1_ref[...] + _rms_bwd(du * g_ref[...], xhat, rstd)

    row = pl.BlockSpec((tt, D), lambda i: (i, 0))
    n = len(modes)
    return pl.pallas_call(
        _hosted(body, 5, 2, (t // tt,), modes), name="inproj_bwd_dx", grid=(t // tt,),
        out_shape=[jax.ShapeDtypeStruct((t, D), F32), jax.ShapeDtypeStruct((SUBLANES, D), F32)]
        + _recv_shapes(grads, modes),
        in_specs=[pl.BlockSpec((tt, N_COLS), lambda i: (i, 0)), row, row, _full((1, D)), ANY]
        + [ANY] * n,
        out_specs=[row, _full((SUBLANES, D))] + [ANY] * n,
        scratch_shapes=[pltpu.VMEM((N_DEV, D, COLS_PER_DEV), BF16)] + _exchange_scratch(n),
        compiler_params=_params(48, dimension_semantics=("arbitrary",)),
    )(dz, x, d_h1, ln_g, w_in_all, *grads)


def _inproj_bwd_dw(name, u, dz, first, count, grads=(), modes=()):
    t = u.shape[0]
    tt = min(512, t)
    grid = (count, t // tt)
    n = len(modes)

    def body(u_ref, dz_ref, dw_ref):
        @pl.when(pl.program_id(1) == 0)
        def _():
            dw_ref[...] = jnp.zeros_like(dw_ref)

        dw_ref[0] += lax.dot_general(u_ref[...], dz_ref[...], (((0,), (0,)), ((), ())),
                                     preferred_element_type=F32)

    return pl.pallas_call(
        _hosted(body, 2, 1, grid, modes) if n else body, name=name, grid=grid,
        out_shape=[jax.ShapeDtypeStruct((count, D, COLS_PER_DEV), F32)]
        + _recv_shapes(grads, modes),
        in_specs=[pl.BlockSpec((tt, D), lambda j, i: (i, 0)),
                  pl.BlockSpec((tt, COLS_PER_DEV), lambda j, i: (i, first + j))] + [ANY] * n,
        out_specs=[pl.BlockSpec((1, D, COLS_PER_DEV), lambda j, i: (j, 0, 0))] + [ANY] * n,
        scratch_shapes=_exchange_scratch(n) if n else [],
        compiler_params=_params(32, dimension_semantics=("arbitrary", "arbitrary")),
    )(u, dz, *grads)


def _adamw(w, g, m, v):
    m = ADAM_B1 * m + (1.0 - ADAM_B1) * g
    v = ADAM_B2 * v + (1.0 - ADAM_B2) * (g * g)
    m_hat = m / (1.0 - ADAM_B1 ** ADAM_STEP)
    v_hat = v / (1.0 - ADAM_B2 ** ADAM_STEP)
    delta = -ADAM_LR * (m_hat / (jnp.sqrt(v_hat) + ADAM_EPS) + ADAM_WD * w)
    return delta, m, v


def _pack_small(partials):
    rows = sorted(partials)

    def body(*refs):
        ins, out_ref = refs[:-1], refs[-1]
        out_ref[...] = jnp.zeros_like(out_ref)
        for j, row in enumerate(rows):
            out_ref[row:row + 1, :] = jnp.sum(ins[j][...], axis=0, keepdims=True)

    return pl.pallas_call(
        body, name="pack_small", out_shape=jax.ShapeDtypeStruct((N_SMALL, D), F32),
    )(*[partials[row] for row in rows])


def _sum_adam(name, recvs, w, m, v, rows):
    r, c = w.shape
    n = len(recvs)

    def body(*refs):
        w_ref, m_ref, v_ref, g_ref, d_ref, mo_ref, vo_ref = refs[n:]

        def finish(recv_ref):
            g = recv_ref[0]
            for s in range(1, recv_ref.shape[0]):
                g = g + recv_ref[s]
            g_ref[...] = g
            d_ref[...], mo_ref[...], vo_ref[...] = _adamw(w_ref[...], g, m_ref[...], v_ref[...])

        if n == 1:
            finish(refs[0])
        else:
            for side in range(n):
                pl.when(lax.axis_index("x") == side)(functools.partial(finish, refs[side]))

    blk = pl.BlockSpec((rows, c), lambda i: (i, 0))
    return pl.pallas_call(
        body, name=name, grid=(r // rows,),
        out_shape=[jax.ShapeDtypeStruct((r, c), F32)] * 4,
        in_specs=[pl.BlockSpec((rv.shape[0], rows, c), lambda i: (0, i, 0)) for rv in recvs]
        + [blk, blk, blk],
        out_specs=[blk] * 4,
        compiler_params=_params(48, dimension_semantics=("arbitrary",)),
    )(*recvs, w, m, v)


def _small_adam(gathered, lb_logits, w, m, v):
    def body(ga_ref, lbl_ref, w_ref, m_ref, v_ref, g_ref, d_ref, mo_ref, vo_ref, loss_ref):
        g = ga_ref[0]
        for s in range(1, N_DEV):
            g = g + ga_ref[s]
        s0, s1 = _lower_bound(lbl_ref[...])
        d_lb = g[R_LB0:R_LB0 + 1, :]
        rows = lax.broadcasted_iota(jnp.int32, (N_SMALL, D), 0)
        g = jnp.where(rows == R_LB0, d_lb * s0 * (1.0 - s0), g)
        g = jnp.where(rows == R_LB1, -d_lb * s0 * s1, g)
        g_ref[...] = g
        d_ref[...], mo_ref[...], vo_ref[...] = _adamw(w_ref[...], g, m_ref[...], v_ref[...])
        loss_ref[...] = (0.5 / D) * jnp.sum(g[R_LOSS:R_LOSS + 1, :], axis=-1, keepdims=True)

    return pl.pallas_call(
        body, name="small_adam",
        out_shape=[jax.ShapeDtypeStruct((N_SMALL, D), F32)] * 4 + [jax.ShapeDtypeStruct((1, 1), F32)],
    )(gathered, lb_logits, w, m, v)


def _pad_rows(a, rows):
    return jnp.pad(a, ((0, rows - a.shape[0]), (0, 0)))


def _pack_rows(rows):
    rows = [r.reshape(-1, D) for r in rows]
    packed = jnp.concatenate(rows, axis=0)
    return _pad_rows(packed, N_SMALL)


def kernel(x, p, ln_g, w_in, conv_w, conv_b, cnorm_g, cnorm_b, w_pw2, b_pw2, lb_logits, onorm_g, w_out, pe_norm_g, w_pg, w_pp, final_g, loss_target, m_ln_g, m_w_in, m_conv_w, m_conv_b, m_cnorm_g, m_cnorm_b, m_w_pw2, m_b_pw2, m_lb_logits, m_onorm_g, m_w_out, m_pe_norm_g, m_w_pg, m_w_pp, m_final_g, v_ln_g, v_w_in, v_conv_w, v_conv_b, v_cnorm_g, v_cnorm_b, v_w_pw2, v_b_pw2, v_lb_logits, v_onorm_g, v_w_out, v_pe_norm_g, v_w_pg, v_w_pp, v_final_g):
    t = x.shape[1]
    x2 = x.reshape(t, D)
    p2 = p.reshape(t, PLE)
    tg2 = loss_target.reshape(t, D)
    fin_g = final_g.reshape(1, D)

    z, u, w_in_all, conv_w_all, w_pw2_all = _inproj_fwd(
        x2, ln_g, w_in[0].astype(BF16), [_pad_rows(conv_w[0], CONV_PAD), w_pw2[0].astype(BF16)])
    o_raw, y_hgrn, states, w_out_all, w_pg_all, w_pp_all = _hgrn_fwd(
        z, lb_logits, onorm_g, [w_out[0].astype(BF16), w_pg[0].astype(BF16), w_pp[0].astype(BF16)])
    w_pw2_full = w_pw2_all.reshape(D, D)
    w_out_full = w_out_all.reshape(2 * D, D)
    w_pg_full = w_pg_all.reshape(D, D)
    yc, y2, y_conv = _conv_fwd(z, conv_w_all, conv_b, cnorm_g, cnorm_b, w_pw2_full, b_pw2)

    (d_h1, dy_conv, dy_hgrn, d_w_out, d_w_pg, d_w_pp, d_pen_p, d_fin_p, loss_p) = _tail(
        x2, y_conv, y_hgrn, p2, tg2, w_out_full, w_pg_full, w_pp_all, pe_norm_g, fin_g)

    dz, d_on_p, d_lb_p, r_w_out, r_w_pg, r_w_pp = _hgrn_bwd(
        dy_hgrn, z, o_raw, states, lb_logits, onorm_g,
        [d_w_out.reshape(N_DEV, 2 * D // N_DEV, D), d_w_pg.reshape(N_DEV, D // N_DEV, D), d_w_pp])
    (d_w_in_hi,) = _inproj_bwd_dw("inproj_bwd_dw_hi", u, dz, N_DEV // 2, N_DEV // 2)
    dz, d_w_pw2, d_conv_w, d_b2_p, d_cng_p, d_cnb_p, d_cb_p, r_w_in_hi = _conv_bwd(
        dy_conv, z, yc, y2, conv_w_all, cnorm_g, cnorm_b, w_pw2_full, dz, [d_w_in_hi], [1])
    d_w_in_lo, r_w_pw2, r_conv_w = _inproj_bwd_dw(
        "inproj_bwd_dw_lo", u, dz, 0, N_DEV // 2,
        [d_w_pw2.reshape(N_DEV, D // N_DEV, D), d_conv_w], ["scatter", "scatter"])
    chip_lo = _pair_reduce("pair_reduce_lo", d_w_in_lo)
    grad_x, d_ln_p, r_w_in_lo = _inproj_bwd_dx(
        dz, x2, d_h1, ln_g, w_in_all, [chip_lo], [("chip", 0)])

    small = _pack_small({R_LN: d_ln_p, R_CONVB: d_cb_p, R_CNG: d_cng_p, R_CNB: d_cnb_p,
                         R_BPW2: d_b2_p, R_LB0: d_lb_p, R_ON: d_on_p, R_PEN: d_pen_p,
                         R_FIN: d_fin_p, R_LOSS: loss_p})
    (small_all,) = _exchange_call("gather_small", [small], ["gather"])

    big = {}
    big["w_in"] = _sum_adam("adam_w_in", [r_w_in_lo, r_w_in_hi], w_in[0], m_w_in[0], v_w_in[0], 128)
    cw = _sum_adam("adam_conv_w", [r_conv_w], _pad_rows(conv_w[0], CONV_PAD),
                   _pad_rows(m_conv_w[0], CONV_PAD), _pad_rows(v_conv_w[0], CONV_PAD), CONV_PAD)
    big["conv_w"] = [a[:CONV_K] for a in cw]
    big["w_pw2"] = _sum_adam("adam_w_pw2", [r_w_pw2], w_pw2[0], m_w_pw2[0], v_w_pw2[0], 128)
    big["w_out"] = _sum_adam("adam_w_out", [r_w_out], w_out[0], m_w_out[0], v_w_out[0], 128)
    big["w_pg"] = _sum_adam("adam_w_pg", [r_w_pg], w_pg[0], m_w_pg[0], v_w_pg[0], 128)
    big["w_pp"] = _sum_adam("adam_w_pp", [r_w_pp], w_pp[0], m_w_pp[0], v_w_pp[0], PLE)

    small_w = [ln_g, conv_b, cnorm_g, cnorm_b, b_pw2, lb_logits, onorm_g, pe_norm_g, final_g]
    small_m = [m_ln_g, m_conv_b, m_cnorm_g, m_cnorm_b, m_b_pw2, m_lb_logits, m_onorm_g,
               m_pe_norm_g, m_final_g]
    small_v = [v_ln_g, v_conv_b, v_cnorm_g, v_cnorm_b, v_b_pw2, v_lb_logits, v_onorm_g,
               v_pe_norm_g, v_final_g]
    sg, sd, sm, sv, loss = _small_adam(small_all, lb_logits, _pack_rows(small_w),
                                       _pack_rows(small_m), _pack_rows(small_v))

    small_rows = {"ln_g": (R_LN, 1), "conv_b": (R_CONVB, 1), "cnorm_g": (R_CNG, 1),
                  "cnorm_b": (R_CNB, 1), "b_pw2": (R_BPW2, 1), "lb_logits": (R_LB0, 2),
                  "onorm_g": (R_ON, 1), "pe_norm_g": (R_PEN, 1), "final_g": (R_FIN, 1)}
    order = ["ln_g", "w_in", "conv_w", "conv_b", "cnorm_g", "cnorm_b", "w_pw2", "b_pw2",
             "lb_logits", "onorm_g", "w_out", "pe_norm_g", "w_pg", "w_pp", "final_g"]

    def leaf(kind, name):
        if name in big:
            return big[name][kind][None]
        r0, n = small_rows[name]
        a = (sg, sd, sm, sv)[kind][r0:r0 + n]
        return a.reshape(D) if name == "final_g" else a

    outs = [loss.reshape(()), grad_x.reshape(1, t, D)]
    for kind in range(4):
        outs += [leaf(kind, name) for name in order]
    return tuple(outs)
```

```python
import functools

import jax
import jax.numpy as jnp
from jax import lax
from jax.experimental import pallas as pl
from jax.experimental.pallas import tpu as pltpu

F32 = jnp.float32
BF16 = jnp.bfloat16
MESH = pl.DeviceIdType.MESH

N_DEV = 8
D = 1024
N_COLS = 7 * D
COLS_PER_DEV = N_COLS // N_DEV
PLE = 256
HEAD = 128
N_HEADS = D // HEAD
CONV_K = 31
CONV_PAD = 32
CHUNK = 64
EPS = 1e-6
SUBLANES = 8

ADAM_LR = 0.001
ADAM_B1 = 0.9
ADAM_B2 = 0.999
ADAM_EPS = 1e-08
ADAM_WD = 0.01
ADAM_STEP = 10

MIB = 1024 * 1024
N_SMALL = 16
R_LN, R_CONVB, R_CNG, R_CNB, R_BPW2, R_LB0, R_LB1, R_ON, R_PEN, R_FIN, R_LOSS = range(11)


def _params(vmem_mib, **kw):
    return pltpu.CompilerParams(vmem_limit_bytes=vmem_mib * MIB, **kw)


def _dot(a, b):
    return jnp.dot(a.astype(BF16), b.astype(BF16), preferred_element_type=F32)


def _dot_nt(a, b):
    return lax.dot_general(a.astype(BF16), b.astype(BF16), (((1,), (1,)), ((), ())),
                           preferred_element_type=F32)


def _dot_tn(a, b):
    return lax.dot_general(a.astype(BF16), b.astype(BF16), (((0,), (0,)), ((), ())),
                           preferred_element_type=F32)


def _split(a):
    hi = a.astype(BF16)
    return hi, (a - hi.astype(F32)).astype(BF16)


def _dot_split(a, b, dims):
    ah, al = _split(a)
    bh, bl = _split(b)
    dg = lambda p, q: lax.dot_general(p, q, dims, preferred_element_type=F32)
    return dg(ah, bh) + (dg(ah, bl) + dg(al, bh))


def _sigmoid(x):
    return 1.0 / (1.0 + jnp.exp(-x))


def _rowsum8(a):
    r, c = a.shape
    return jnp.sum(a.reshape(r // SUBLANES, SUBLANES, c), axis=0)


def _tri_dot(tri, a):
    hi = a.astype(BF16)
    r1 = a - hi.astype(F32)
    mid = r1.astype(BF16)
    lo = (r1 - mid.astype(F32)).astype(BF16)
    return (jnp.dot(tri, hi, preferred_element_type=F32)
            + jnp.dot(tri, mid, preferred_element_type=F32)
            + jnp.dot(tri, lo, preferred_element_type=F32))


def _lower_bound(lbl):
    l0, l1 = lbl[0:1, :], lbl[1:2, :]
    m = jnp.maximum(l0, l1)
    e0, e1 = jnp.exp(l0 - m), jnp.exp(l1 - m)
    s = e0 + e1
    return e0 / s, e1 / s


ANY = pl.BlockSpec(memory_space=pl.ANY)


def _full(shape):
    return pl.BlockSpec(shape, lambda i: (0,) * len(shape))


def _peer(x, y, c, k):
    px = 1 - x if k & 4 else x
    py = 1 - y if k & 2 else y
    pc = 1 - c if k & 1 else c
    return (px, py, pc), 4 * px + 2 * py + pc


class _Exchange:
    def __init__(self, srcs, outs, modes, send_sems, recv_sems, local_sems):
        x, y, c = lax.axis_index("x"), lax.axis_index("y"), lax.axis_index("c")
        me = 4 * x + 2 * y + c
        self.starts, self.send_waits, self.recv_waits = [], [], []

        def remote(a, k, src, slot, peer, when):
            sem = a * N_DEV + k
            cp = pltpu.make_async_remote_copy(
                src_ref=src, dst_ref=outs[a].at[slot], send_sem=send_sems.at[sem],
                recv_sem=recv_sems.at[sem], device_id=peer, device_id_type=MESH)
            self.starts.append((when, cp.start))
            self.send_waits.append((when, cp.wait_send))

        def arrival(a, k, slot, when):
            sem = a * N_DEV + k
            cp = pltpu.make_async_remote_copy(
                src_ref=outs[a].at[slot], dst_ref=outs[a].at[slot], send_sem=send_sems.at[sem],
                recv_sem=recv_sems.at[sem], device_id=(x, y, c), device_id_type=MESH)
            self.recv_waits.append((when, cp.wait_recv))

        def local(a, src, slot, when):
            cp = pltpu.make_async_copy(src, outs[a].at[slot], local_sems.at[a])
            self.starts.append((when, cp.start))
            self.send_waits.append((when, cp.wait))

        for a, (src, mode) in enumerate(zip(srcs, modes)):
            if mode in ("gather", "scatter"):
                local(a, src if mode == "gather" else src.at[me], me, None)
                for k in range(1, N_DEV):
                    peer, peer_idx = _peer(x, y, c, k)
                    remote(a, k, src if mode == "gather" else src.at[peer_idx], me, peer, None)
                    arrival(a, k, peer_idx, None)
                continue
            if isinstance(mode, tuple):
                here, away = x == mode[1], x != mode[1]
                chip = 2 * x + y
                local(a, src.at[y], chip, here)
                remote(a, 1, src.at[1 - y], chip, (x, 1 - y, c), here)
                remote(a, 2, src.at[y], chip, (1 - x, y, c), away)
                remote(a, 3, src.at[1 - y], chip, (1 - x, 1 - y, c), away)
                arrival(a, 1, 2 * x + 1 - y, here)
                arrival(a, 2, 2 * (1 - x) + y, here)
                arrival(a, 3, 2 * (1 - x) + 1 - y, here)
                continue
            here, away = x == mode, x != mode
            for kk in range(4):
                py = 1 - y if kk & 2 else y
                pc = 1 - c if kk & 1 else c
                block = src.at[2 * py + pc]
                if kk == 0:
                    local(a, block, me, here)
                else:
                    remote(a, kk, block, me, (x, py, pc), here)
                remote(a, 4 + kk, block, me, (1 - x, py, pc), away)
            for k in range(1, N_DEV):
                arrival(a, k, _peer(x, y, c, k)[1], here)

    @staticmethod
    def _run(actions):
        for when, fn in actions:
            if when is None:
                fn()
            else:
                pl.when(when)(fn)

    def start(self):
        self._run(self.starts)

    def wait(self):
        self._run(self.recv_waits)
        self._run(self.send_waits)


def _exchange_scratch(n):
    return [pltpu.SemaphoreType.DMA((n * N_DEV,)), pltpu.SemaphoreType.DMA((n * N_DEV,)),
            pltpu.SemaphoreType.DMA((n,))]


def _recv_shapes(srcs, modes):
    def shape(s, m):
        if m == "gather":
            return (N_DEV,) + s.shape
        return (N_DEV // 2 if isinstance(m, tuple) else N_DEV,) + s.shape[1:]

    return [jax.ShapeDtypeStruct(shape(s, m), s.dtype) for s, m in zip(srcs, modes)]


def _pair_reduce(name, blocks):
    shape = (2,) + blocks.shape[1:]

    def body(src, out_ref, stage, mine, send_sems, recv_sems, local_sems):
        x, y, c = lax.axis_index("x"), lax.axis_index("y"), lax.axis_index("c")
        sends, waits = [], []
        for py in range(2):
            sends.append(pltpu.make_async_remote_copy(
                src_ref=src.at[2 * py + 1 - c], dst_ref=stage.at[py], send_sem=send_sems.at[py],
                recv_sem=recv_sems.at[py], device_id=(x, y, 1 - c), device_id_type=MESH))
            waits.append(pltpu.make_async_copy(src.at[2 * py + c], mine.at[py], local_sems.at[py]))
        for cp in sends + waits:
            cp.start()
        for cp in waits:
            cp.wait()
        for cp in sends:
            cp.wait_recv()
        out_ref[...] = mine[...] + stage[...]
        for cp in sends:
            cp.wait_send()

    return pl.pallas_call(
        body, name=name, out_shape=jax.ShapeDtypeStruct(shape, F32), in_specs=[ANY],
        scratch_shapes=[pltpu.VMEM(shape, F32), pltpu.VMEM(shape, F32),
                        pltpu.SemaphoreType.DMA((2,)), pltpu.SemaphoreType.DMA((2,)),
                        pltpu.SemaphoreType.DMA((2,))],
        compiler_params=_params(40),
    )(blocks)


def _exchange_call(name, srcs, modes):
    n = len(srcs)

    def body(*refs):
        xch = _Exchange(refs[:n], refs[n:2 * n], modes, *refs[2 * n:])
        xch.start()
        xch.wait()

    return pl.pallas_call(
        body, name=name, out_shape=_recv_shapes(srcs, modes),
        in_specs=[ANY] * n, out_specs=[ANY] * n, scratch_shapes=_exchange_scratch(n),
    )(*srcs)


def _hosted(body, n_in, n_out, grid, modes):
    n = len(modes)

    def hosted(*refs):
        ins, srcs = refs[:n_in], refs[n_in:n_in + n]
        outs = refs[n_in + n:n_in + n + n_out]
        bufs = refs[n_in + n + n_out:n_in + 2 * n + n_out]
        scratch = refs[n_in + 2 * n + n_out:-3]
        xch = _Exchange(srcs, bufs, modes, *refs[-3:])
        first, last = True, True
        for axis, size in enumerate(grid):
            first = jnp.logical_and(first, pl.program_id(axis) == 0)
            last = jnp.logical_and(last, pl.program_id(axis) == size - 1)
        pl.when(first)(xch.start)
        body(*ins, *outs, *scratch)
        pl.when(last)(xch.wait)

    return hosted


GATHER_ORDER = (0, 1, 2, 4, 3, 5, 6, 7)
PUSHED = (1, 2, 4, 6)
FORWARDED = (2, 4, 6)


def _inproj_fwd(x, ln_g, w_shard, other_shards):
    t = x.shape[0]
    tt = min(1024, t)
    n_t = t // tt
    modes = ["gather"] * len(other_shards)
    n = len(modes)
    me = 4 * lax.axis_index("x") + 2 * lax.axis_index("y") + lax.axis_index("c")
    order = jnp.bitwise_xor(me, jnp.array(GATHER_ORDER, jnp.int32)).astype(jnp.int32)

    def body(order_ref, x_ref, g_ref, shard_hbm, *refs):
        srcs = refs[:n]
        z_ref, u_ref, w_all = refs[n:n + 3]
        bufs = refs[n + 3:2 * n + 3]
        u_all, w_blk, w_send, w_recv, w_local = refs[2 * n + 3:2 * n + 8]
        j, i = pl.program_id(0), pl.program_id(1)
        x, y, c = lax.axis_index("x"), lax.axis_index("y"), lax.axis_index("c")
        mine = 4 * x + 2 * y + c
        others = _Exchange(srcs, bufs, modes, *refs[2 * n + 8:])

        def push(k):
            peer, _ = _peer(x, y, c, k)
            return pltpu.make_async_remote_copy(
                src_ref=shard_hbm, dst_ref=w_all.at[mine], send_sem=w_send.at[k],
                recv_sem=w_recv.at[k], device_id=peer, device_id_type=MESH)

        def forward(k):
            _, owner = _peer(x, y, c, k)
            return pltpu.make_async_remote_copy(
                src_ref=w_all.at[owner], dst_ref=w_all.at[owner], send_sem=w_send.at[k + 1],
                recv_sem=w_recv.at[k + 1], device_id=(x, y, 1 - c), device_id_type=MESH)

        def landed(k):
            _, owner = _peer(x, y, c, k)
            return pltpu.make_async_remote_copy(
                src_ref=w_all.at[owner], dst_ref=w_all.at[owner], send_sem=w_send.at[k],
                recv_sem=w_recv.at[k], device_id=(x, y, c), device_id_type=MESH)

        keep = pltpu.make_async_copy(shard_hbm, w_all.at[mine], w_local.at[0])

        @pl.when(jnp.logical_and(j == 0, i == 0))
        def _():
            for k in PUSHED:
                push(k).start()
            keep.start()
            others.start()
            pltpu.sync_copy(shard_hbm, w_blk)

        for step, k in enumerate(GATHER_ORDER[1:], start=1):
            @pl.when(jnp.logical_and(j == step, i == 0))
            def _(k=k):
                landed(k).wait_recv()
                if k in FORWARDED:
                    forward(k).start()
                pltpu.sync_copy(w_all.at[_peer(x, y, c, k)[1]], w_blk)

        rows = pl.ds(pl.multiple_of(i * tt, tt), tt)

        @pl.when(j == 0)
        def _():
            xv = x_ref[...]
            rstd = lax.rsqrt(jnp.mean(xv * xv, axis=-1, keepdims=True) + EPS)
            ub = (xv * rstd * g_ref[...]).astype(BF16)
            u_ref[...] = ub
            u_all[rows, :] = ub

        z_ref[...] = jnp.dot(u_all[rows, :], w_blk[...], preferred_element_type=F32)

        @pl.when(jnp.logical_and(j == N_DEV - 1, i == n_t - 1))
        def _():
            for k in PUSHED:
                push(k).wait_send()
            for k in FORWARDED:
                forward(k).wait_send()
            keep.wait()
            others.wait()

    first_pass = lambda j, i, order_ref: (jnp.where(j == 0, i, n_t - 1), 0)
    grid_spec = pltpu.PrefetchScalarGridSpec(
        num_scalar_prefetch=1, grid=(N_DEV, n_t),
        in_specs=[pl.BlockSpec((tt, D), first_pass),
                  pl.BlockSpec((1, D), lambda j, i, order_ref: (0, 0)), ANY] + [ANY] * n,
        out_specs=[pl.BlockSpec((tt, COLS_PER_DEV), lambda j, i, order_ref: (i, order_ref[j])),
                   pl.BlockSpec((tt, D), first_pass), ANY] + [ANY] * n,
        scratch_shapes=[pltpu.VMEM((t, D), BF16), pltpu.VMEM((D, COLS_PER_DEV), BF16),
                        pltpu.SemaphoreType.DMA((N_DEV,)), pltpu.SemaphoreType.DMA((N_DEV,)),
                        pltpu.SemaphoreType.DMA((1,))] + _exchange_scratch(n))
    return pl.pallas_call(
        body, name="inproj_fwd", grid_spec=grid_spec,
        out_shape=[jax.ShapeDtypeStruct((t, N_COLS), F32), jax.ShapeDtypeStruct((t, D), BF16),
                   jax.ShapeDtypeStruct((N_DEV,) + w_shard.shape, BF16)]
        + _recv_shapes(other_shards, modes),
        compiler_params=_params(48, dimension_semantics=("arbitrary", "arbitrary")),
    )(order, x, ln_g, w_shard, *other_shards)


def _shifted_copies(buf, shifted, rows):
    for b in range(1, SUBLANES):
        shifted[b, 0:rows, :] = buf[b:b + rows, :]


def _tap_ref(buf, shifted, offset):
    a, b = divmod(offset, SUBLANES)
    return (buf if b == 0 else shifted.at[b]), SUBLANES * a


def _group_norm_stats(blk):
    mu = jnp.mean(blk, axis=-1, keepdims=True)
    cen = blk - mu
    var = jnp.mean(cen * cen, axis=-1, keepdims=True)
    return cen * lax.rsqrt(var + EPS)


def _conv_fwd(z, conv_w_all, conv_b, cn_g, cn_b, w_pw2, b_pw2):
    t = z.shape[0]
    tt = min(256, t)
    rc = 32

    def body(val_ref, glu_ref, gate_ref, cw_ref, cb_ref, g_ref, b_ref, w_hbm, b2_ref,
             yc_ref, y2_ref, yo_ref, w_vmem, vbuf, vsh, y1buf):
        @pl.when(pl.program_id(0) == 0)
        def _():
            pltpu.sync_copy(w_hbm, w_vmem)
            vbuf[0:CONV_PAD, :] = jnp.zeros((CONV_PAD, D), F32)

        vbuf[CONV_PAD:CONV_PAD + tt, :] = val_ref[...] * _sigmoid(glu_ref[...])
        _shifted_copies(vbuf, vsh, tt + 24)

        def row_chunk(r, carry):
            r0 = pl.multiple_of(r * rc, rc)
            for g in range(N_HEADS):
                cs = slice(HEAD * g, HEAD * (g + 1))
                acc = jnp.zeros((rc, HEAD), F32)
                for k in range(CONV_K):
                    ref, off = _tap_ref(vbuf, vsh, k + 2)
                    acc = acc + cw_ref[g, k:k + 1, :] * ref[pl.ds(r0 + off, rc), cs]
                acc = acc + cb_ref[:, cs]
                yc_ref[pl.ds(r0, rc), cs] = acc
                n = _group_norm_stats(acc) * g_ref[:, cs] + b_ref[:, cs]
                y1buf[pl.ds(r0, rc), cs] = (n * _sigmoid(n)).astype(BF16)
            return carry

        lax.fori_loop(0, tt // rc, row_chunk, 0)
        vbuf[0:CONV_PAD, :] = vbuf[tt:tt + CONV_PAD, :]
        y2 = jnp.dot(y1buf[...], w_vmem[...], preferred_element_type=F32) + b2_ref[...]
        y2_ref[...] = y2
        gate = gate_ref[...]
        yo_ref[...] = (y2 * gate * _sigmoid(gate)).astype(BF16)

    col = lambda j: pl.BlockSpec((tt, D), lambda i: (i, j))
    row = pl.BlockSpec((tt, D), lambda i: (i, 0))
    return pl.pallas_call(
        body, name="conv_fwd", grid=(t // tt,),
        out_shape=[jax.ShapeDtypeStruct((t, D), F32), jax.ShapeDtypeStruct((t, D), F32),
                   jax.ShapeDtypeStruct((t, D), BF16)],
        in_specs=[col(0), col(1), col(2), _full((N_DEV, CONV_PAD, HEAD)), _full((1, D)),
                  _full((1, D)), _full((1, D)), ANY, _full((1, D))],
        out_specs=[row, row, row],
        scratch_shapes=[pltpu.VMEM((D, D), BF16), pltpu.VMEM((tt + CONV_PAD, D), F32),
                        pltpu.VMEM((SUBLANES, tt + CONV_PAD, D), F32), pltpu.VMEM((tt, D), BF16)],
        compiler_params=_params(48, dimension_semantics=("arbitrary",)),
    )(z, z, z, conv_w_all, conv_b, cn_g, cn_b, w_pw2, b_pw2)


def _chunk_quantities(zq, zf, lbh, tri):
    sig = _sigmoid(zf)
    sig_neg = _sigmoid(-zf)
    f = lbh + (1.0 - lbh) * sig
    k = (1.0 - lbh) * sig_neg
    q = zq * _sigmoid(zq)
    b = _tri_dot(tri, jnp.log(f))
    b_mid = b[CHUNK // 2 - 1:CHUNK // 2, :]
    b_last = b[CHUNK - 1:CHUNK, :]
    e_q = jnp.exp(b)
    e_qm = jnp.exp(b - b_mid)
    e_km = jnp.exp(b_mid - b)
    e_kd = jnp.exp(b_last - b)
    return q, k, f, sig, sig_neg, e_q, e_qm, e_km, e_kd, jnp.exp(b_last)


def _hgrn_fwd(z, lb_logits, onorm_g, shards):
    t = z.shape[0]
    tt = min(256, t)
    nc = tt // CHUNK
    modes = ["gather"] * len(shards)
    n = len(modes)

    def body(q_ref, f_ref, i_ref, g_ref, lbl_ref, on_ref, o_ref, y_ref, s_ref, st):
        @pl.when(pl.program_id(0) == 0)
        def _():
            st[...] = jnp.zeros_like(st)

        lb, _ = _lower_bound(lbl_ref[...])
        rows = lax.broadcasted_iota(jnp.int32, (CHUNK, CHUNK), 0)
        cols = lax.broadcasted_iota(jnp.int32, (CHUNK, CHUNK), 1)
        causal = rows >= cols
        tri = causal.astype(BF16)

        def chunk(c, carry):
            r0 = pl.multiple_of(c * CHUNK, CHUNK)
            rs = pl.ds(r0, CHUNK)
            for h in range(N_HEADS):
                cs = slice(HEAD * h, HEAD * (h + 1))
                q, k, _, _, _, e_q, e_qm, e_km, e_kd, e_last = _chunk_quantities(
                    q_ref[rs, cs], f_ref[rs, cs], lb[:, cs], tri)
                v = i_ref[rs, cs]
                s_old = st[h]
                s_ref[c, h] = s_old
                a = jnp.where(causal, _dot_nt(q * e_qm, k * e_km), 0.0)
                o = _dot_nt(q * e_q, s_old) + _dot(a, v)
                st[h] = s_old * e_last + _dot_tn(v, k * e_kd)
                o_ref[rs, cs] = o
                n = o * lax.rsqrt(jnp.mean(o * o, axis=-1, keepdims=True) + EPS)
                zg = g_ref[rs, cs]
                y_ref[rs, cs] = (n * on_ref[:, cs] * zg * _sigmoid(zg)).astype(BF16)
            return carry

        lax.fori_loop(0, nc, chunk, 0)

    col = lambda j: pl.BlockSpec((tt, D), lambda i: (i, j))
    row = pl.BlockSpec((tt, D), lambda i: (i, 0))
    return pl.pallas_call(
        _hosted(body, 6, 3, (t // tt,), modes), name="hgrn_fwd", grid=(t // tt,),
        out_shape=[jax.ShapeDtypeStruct((t, D), F32), jax.ShapeDtypeStruct((t, D), BF16),
                   jax.ShapeDtypeStruct((t // CHUNK, N_HEADS, HEAD, HEAD), F32)]
        + _recv_shapes(shards, modes),
        in_specs=[col(3), col(4), col(5), col(6), _full((2, D)), _full((1, D))] + [ANY] * n,
        out_specs=[row, row, pl.BlockSpec((nc, N_HEADS, HEAD, HEAD), lambda i: (i, 0, 0, 0))]
        + [ANY] * n,
        scratch_shapes=[pltpu.VMEM((N_HEADS, HEAD, HEAD), F32)] + _exchange_scratch(n),
        compiler_params=_params(40, dimension_semantics=("arbitrary",)),
    )(z, z, z, z, lb_logits, onorm_g, *shards)


def _rms_bwd(dn, xhat, rstd):
    return rstd * (dn - xhat * jnp.mean(dn * xhat, axis=-1, keepdims=True))


def _tail(x, y_conv, y_hgrn, p, target, w_out, w_pg, w_pp_all, pe_g, fin_g):
    t = x.shape[0]
    tt = min(256, t)
    n_steps = t // tt

    def body(x_ref, yc_ref, yh_ref, p_ref, tg_ref, wo_hbm, wg_hbm, wp_hbm, pg_ref, fg_ref,
             dh1_ref, dyc_ref, dyh_ref, dwo_hbm, dwg_hbm, dwp_hbm, dpg_ref, dfg_ref, loss_ref,
             wo, wg, wp, dwo, dwg, dwp):
        i = pl.program_id(0)

        @pl.when(i == 0)
        def _():
            pltpu.sync_copy(wo_hbm, wo)
            pltpu.sync_copy(wg_hbm, wg)
            for d in range(N_DEV):
                pltpu.sync_copy(wp_hbm.at[d], wp.at[:, pl.ds(HEAD * d, HEAD)])
            dwo[...] = jnp.zeros_like(dwo)
            dwg[...] = jnp.zeros_like(dwg)
            dwp[...] = jnp.zeros_like(dwp)
            dpg_ref[...] = jnp.zeros_like(dpg_ref)
            dfg_ref[...] = jnp.zeros_like(dfg_ref)
            loss_ref[...] = jnp.zeros_like(loss_ref)

        ycv, yhv = yc_ref[...], yh_ref[...]
        h1 = (x_ref[...] + jnp.dot(ycv, wo[0:D, :], preferred_element_type=F32)
              + jnp.dot(yhv, wo[D:2 * D, :], preferred_element_type=F32))
        pb = p_ref[...].astype(BF16)
        pe = jnp.dot(pb, wp[...], preferred_element_type=F32)
        rstd1 = lax.rsqrt(jnp.mean(h1 * h1, axis=-1, keepdims=True) + EPS)
        n1 = h1 * rstd1
        rb = (n1 * pg_ref[...]).astype(BF16)
        gate = _sigmoid(jnp.dot(rb, wg[...], preferred_element_type=F32))
        h2 = h1 + gate * pe
        rstd2 = lax.rsqrt(jnp.mean(h2 * h2, axis=-1, keepdims=True) + EPS)
        n2 = h2 * rstd2
        err = n2 * fg_ref[...] - tg_ref[...]
        loss_ref[...] += _rowsum8(err * err)

        d_out = err * (1.0 / D)
        dfg_ref[...] += _rowsum8(d_out * n2)
        d_h2 = _rms_bwd(d_out * fg_ref[...], n2, rstd2)
        d_pe = (d_h2 * gate).astype(BF16)
        d_gpre = (d_h2 * pe * gate * (1.0 - gate)).astype(BF16)
        dwg[...] += _dot_tn(rb, d_gpre)
        dwp[...] += _dot_tn(pb, d_pe)
        dr = _dot_nt(d_gpre, wg[...])
        dpg_ref[...] += _rowsum8(dr * n1)
        d_h1 = d_h2 + _rms_bwd(dr * pg_ref[...], n1, rstd1)
        dh1_ref[...] = d_h1
        d_h1b = d_h1.astype(BF16)
        dwo[0:D, :] += _dot_tn(ycv, d_h1b)
        dwo[D:2 * D, :] += _dot_tn(yhv, d_h1b)
        dyc_ref[...] = _dot_nt(d_h1b, wo[0:D, :])
        dyh_ref[...] = _dot_nt(d_h1b, wo[D:2 * D, :])

        @pl.when(i == n_steps - 1)
        def _():
            pltpu.sync_copy(dwo, dwo_hbm)
            pltpu.sync_copy(dwg, dwg_hbm)
            for d in range(N_DEV):
                pltpu.sync_copy(dwp.at[:, pl.ds(HEAD * d, HEAD)], dwp_hbm.at[d])

    row = pl.BlockSpec((tt, D), lambda i: (i, 0))
    acc = _full((SUBLANES, D))
    return pl.pallas_call(
        body, name="tail_fwd_bwd", grid=(n_steps,),
        out_shape=[jax.ShapeDtypeStruct((t, D), F32)] * 3
        + [jax.ShapeDtypeStruct((2 * D, D), F32), jax.ShapeDtypeStruct((D, D), F32),
           jax.ShapeDtypeStruct((N_DEV, PLE, HEAD), F32)]
        + [jax.ShapeDtypeStruct((SUBLANES, D), F32)] * 3,
        in_specs=[row, row, row, pl.BlockSpec((tt, PLE), lambda i: (i, 0)), row,
                  ANY, ANY, ANY, _full((1, D)), _full((1, D))],
        out_specs=[row, row, row, ANY, ANY, ANY, acc, acc, acc],
        scratch_shapes=[pltpu.VMEM((2 * D, D), BF16), pltpu.VMEM((D, D), BF16),
                        pltpu.VMEM((PLE, D), BF16), pltpu.VMEM((2 * D, D), F32),
                        pltpu.VMEM((D, D), F32), pltpu.VMEM((PLE, D), F32)],
        compiler_params=_params(52, dimension_semantics=("arbitrary",)),
    )(x, y_conv, y_hgrn, p, target, w_out, w_pg, w_pp_all, pe_g, fin_g)


def _hgrn_bwd(dy, z, o_raw, states, lb_logits, onorm_g, grads):
    t = z.shape[0]
    tt = min(256, t)
    nc = tt // CHUNK
    n_steps = t // tt
    modes = ["scatter"] * len(grads)

    def body(dy_ref, q_ref, f_ref, i_ref, g_ref, o_ref, s_ref, lbl_ref, on_ref,
             dz_ref, don_ref, dlb_ref, dst):
        @pl.when(pl.program_id(0) == 0)
        def _():
            dst[...] = jnp.zeros_like(dst)
            don_ref[...] = jnp.zeros_like(don_ref)
            dlb_ref[...] = jnp.zeros_like(dlb_ref)

        lb, _ = _lower_bound(lbl_ref[...])
        rows = lax.broadcasted_iota(jnp.int32, (CHUNK, CHUNK), 0)
        cols = lax.broadcasted_iota(jnp.int32, (CHUNK, CHUNK), 1)
        causal = rows >= cols
        tri = causal.astype(BF16)
        tri_rev = (rows <= cols).astype(BF16)
        is_last = lax.broadcasted_iota(jnp.int32, (CHUNK, HEAD), 0) == CHUNK - 1

        def chunk(cc, carry):
            c = nc - 1 - cc
            r0 = pl.multiple_of(c * CHUNK, CHUNK)
            rs = pl.ds(r0, CHUNK)
            for h in range(N_HEADS):
                cs = slice(HEAD * h, HEAD * (h + 1))
                zq, zf, zg = q_ref[rs, cs], f_ref[rs, cs], g_ref[rs, cs]
                lbh = lb[:, cs]
                q, k, f, sig, sig_neg, e_q, e_qm, e_km, e_kd, e_last = _chunk_quantities(
                    zq, zf, lbh, tri)
                v = i_ref[rs, cs]
                qt, qm, km, kd = q * e_q, q * e_qm, k * e_km, k * e_kd
                a = jnp.where(causal, _dot_nt(qm, km), 0.0)

                o = o_ref[rs, cs]
                rstd = lax.rsqrt(jnp.mean(o * o, axis=-1, keepdims=True) + EPS)
                n = o * rstd
                sg = _sigmoid(zg)
                dyv = dy_ref[rs, cs]
                on = on_ref[:, cs]
                d_zg = dyv * n * on * sg * (1.0 + zg * (1.0 - sg))
                d_on = dyv * zg * sg
                don_ref[:, cs] += _rowsum8(d_on * n)
                do = _rms_bwd(d_on * on, n, rstd)

                s_old = s_ref[c, h]
                ds_new = dst[h]
                da = jnp.where(causal, _dot_nt(do, v), 0.0)
                dv = _dot_tn(a, do) + _dot_nt(kd, ds_new)
                dkd = _dot(v, ds_new)
                dqm = _dot_split(da, km, (((1,), (0,)), ((), ())))
                dkm = _dot_split(da, qm, (((0,), (0,)), ((), ())))
                dq = _dot(do, s_old) * e_q + dqm * e_qm
                dk = dkm * e_km + dkd * e_kd
                dst[h] = ds_new * e_last + _dot_tn(do, qt)
                last = (jnp.sum(dkd * kd, axis=0, keepdims=True)
                        + e_last * jnp.sum(s_old * ds_new, axis=0, keepdims=True))
                db = q * dq - k * dk + jnp.where(is_last, last, 0.0)
                dlogf = _tri_dot(tri_rev, db)
                common = sig_neg * (dlogf / f - dk)
                dlb_ref[:, cs] += _rowsum8(common)
                c0 = 3 * D + HEAD * h
                dz_ref[rs, c0:c0 + HEAD] = (
                    dq * _sigmoid(zq) * (1.0 + zq * (1.0 - _sigmoid(zq)))).astype(BF16)
                dz_ref[rs, D + c0:D + c0 + HEAD] = ((1.0 - lbh) * sig * common).astype(BF16)
                dz_ref[rs, 2 * D + c0:2 * D + c0 + HEAD] = dv.astype(BF16)
                dz_ref[rs, 3 * D + c0:3 * D + c0 + HEAD] = d_zg.astype(BF16)
            return carry

        lax.fori_loop(0, nc, chunk, 0)

    rev = lambda i: n_steps - 1 - i
    col = lambda j: pl.BlockSpec((tt, D), lambda i: (rev(i), j))
    row = pl.BlockSpec((tt, D), lambda i: (rev(i), 0))
    acc = _full((SUBLANES, D))
    n = len(modes)
    return pl.pallas_call(
        _hosted(body, 9, 3, (n_steps,), modes), name="hgrn_bwd", grid=(n_steps,),
        out_shape=[jax.ShapeDtypeStruct((t, N_COLS), BF16),
                   jax.ShapeDtypeStruct((SUBLANES, D), F32),
                   jax.ShapeDtypeStruct((SUBLANES, D), F32)] + _recv_shapes(grads, modes),
        in_specs=[row, col(3), col(4), col(5), col(6), row,
                  pl.BlockSpec((nc, N_HEADS, HEAD, HEAD), lambda i: (rev(i), 0, 0, 0)),
                  _full((2, D)), _full((1, D))] + [ANY] * n,
        out_specs=[pl.BlockSpec((tt, N_COLS), lambda i: (rev(i), 0)), acc, acc] + [ANY] * n,
        scratch_shapes=[pltpu.VMEM((N_HEADS, HEAD, HEAD), F32)] + _exchange_scratch(n),
        compiler_params=_params(48, dimension_semantics=("arbitrary",)),
    )(dy, z, z, z, z, o_raw, states, lb_logits, onorm_g, *grads)


def _conv_bwd(dy, z, yc, y2, conv_w_all, cn_g, cn_b, w_pw2, dz, grads, modes):
    t = z.shape[0]
    tt = min(256, t)
    rc = 32
    n_steps = t // tt

    def body(dy_ref, val_ref, glu_ref, gate_ref, yc_ref, y2_ref, cw_ref, g_ref, b_ref, w_hbm,
             dz_in, dz_ref, dw_hbm, dcw_out, db2_ref, dg_ref, dbeta_ref, dcb_ref,
             w_vmem, dw, dbuf, dsh, y1buf, dnbuf, dcw_ref):
        i = pl.program_id(0)

        @pl.when(i == 0)
        def _():
            pltpu.sync_copy(w_hbm, w_vmem)
            dw[...] = jnp.zeros_like(dw)
            dbuf[tt:tt + CONV_PAD, :] = jnp.zeros((CONV_PAD, D), F32)
            dcw_ref[...] = jnp.zeros_like(dcw_ref)
            dcw_out[...] = jnp.zeros_like(dcw_out)
            db2_ref[...] = jnp.zeros_like(db2_ref)
            dg_ref[...] = jnp.zeros_like(dg_ref)
            dbeta_ref[...] = jnp.zeros_like(dbeta_ref)
            dcb_ref[...] = jnp.zeros_like(dcb_ref)

        gate = gate_ref[...]
        sg = _sigmoid(gate)
        dyv = dy_ref[...]
        dy2 = dyv * gate * sg
        dz_ref[:, 2 * D:3 * D] = (dyv * y2_ref[...] * sg * (1.0 + gate * (1.0 - sg))).astype(BF16)
        db2_ref[...] += _rowsum8(dy2)
        dy2b = dy2.astype(BF16)
        dnbuf[...] = _dot_nt(dy2b, w_vmem[...])

        def norm_chunk(r, carry):
            r0 = pl.multiple_of(r * rc, rc)
            rs = pl.ds(r0, rc)
            for g in range(N_HEADS):
                cs = slice(HEAD * g, HEAD * (g + 1))
                blk = yc_ref[rs, cs]
                mu = jnp.mean(blk, axis=-1, keepdims=True)
                cen = blk - mu
                rstd = lax.rsqrt(jnp.mean(cen * cen, axis=-1, keepdims=True) + EPS)
                xhat = cen * rstd
                n = xhat * g_ref[:, cs] + b_ref[:, cs]
                sn = _sigmoid(n)
                y1buf[rs, cs] = (n * sn).astype(BF16)
                dn = dnbuf[rs, cs] * sn * (1.0 + n * (1.0 - sn))
                dg_ref[:, cs] += _rowsum8(dn * xhat)
                dbeta_ref[:, cs] += _rowsum8(dn)
                dxh = dn * g_ref[:, cs]
                dyc = rstd * (dxh - jnp.mean(dxh, axis=-1, keepdims=True)
                              - xhat * jnp.mean(dxh * xhat, axis=-1, keepdims=True))
                dcb_ref[:, cs] += _rowsum8(dyc)
                dbuf[rs, cs] = dyc
            return carry

        lax.fori_loop(0, tt // rc, norm_chunk, 0)
        dw[...] += _dot_tn(y1buf[...], dy2b)
        _shifted_copies(dbuf, dsh, tt + 24)

        def conv_chunk(r, carry):
            r0 = pl.multiple_of(r * rc, rc)
            rs = pl.ds(r0, rc)
            for g in range(N_HEADS):
                cs = slice(HEAD * g, HEAD * (g + 1))
                sglu = _sigmoid(glu_ref[rs, cs])
                val = val_ref[rs, cs]
                v = val * sglu
                dv = jnp.zeros((rc, HEAD), F32)
                for k in range(CONV_K):
                    ref, off = _tap_ref(dbuf, dsh, CONV_K - 1 - k)
                    d_later = ref[pl.ds(r0 + off, rc), cs]
                    dv = dv + cw_ref[g, k:k + 1, :] * d_later
                    dcw_ref[g, k] += _rowsum8(v * d_later)
                dz_ref[rs, cs] = (dv * sglu).astype(BF16)
                dz_ref[rs, D + HEAD * g:D + HEAD * (g + 1)] = (
                    dv * val * sglu * (1.0 - sglu)).astype(BF16)
            return carry

        lax.fori_loop(0, tt // rc, conv_chunk, 0)
        dbuf[tt:tt + CONV_PAD, :] = dbuf[0:CONV_PAD, :]

        @pl.when(i == n_steps - 1)
        def _():
            pltpu.sync_copy(dw, dw_hbm)
            for g in range(N_HEADS):
                for k in range(CONV_K):
                    dcw_out[g, k:k + 1, :] = jnp.sum(dcw_ref[g, k], axis=0, keepdims=True)

    rev = lambda i: n_steps - 1 - i
    col = lambda j: pl.BlockSpec((tt, D), lambda i: (rev(i), j))
    row = pl.BlockSpec((tt, D), lambda i: (rev(i), 0))
    acc = _full((SUBLANES, D))
    n = len(modes)
    return pl.pallas_call(
        _hosted(body, 11, 7, (n_steps,), modes), name="conv_bwd", grid=(n_steps,),
        out_shape=[jax.ShapeDtypeStruct((t, N_COLS), BF16), jax.ShapeDtypeStruct((D, D), F32),
                   jax.ShapeDtypeStruct((N_DEV, CONV_PAD, HEAD), F32)]
        + [jax.ShapeDtypeStruct((SUBLANES, D), F32)] * 4 + _recv_shapes(grads, modes),
        in_specs=[row, col(0), col(1), col(2), row, row, _full((N_DEV, CONV_PAD, HEAD)),
                  _full((1, D)), _full((1, D)), ANY, ANY] + [ANY] * n,
        out_specs=[pl.BlockSpec((tt, 3 * D), lambda i: (rev(i), 0)), ANY,
                   _full((N_DEV, CONV_PAD, HEAD)), acc, acc, acc, acc] + [ANY] * n,
        input_output_aliases={10: 0},
        scratch_shapes=[pltpu.VMEM((D, D), BF16), pltpu.VMEM((D, D), F32),
                        pltpu.VMEM((tt + CONV_PAD, D), F32),
                        pltpu.VMEM((SUBLANES, tt + CONV_PAD, D), F32),
                        pltpu.VMEM((tt, D), BF16), pltpu.VMEM((tt, D), F32),
                        pltpu.VMEM((N_DEV, CONV_PAD, SUBLANES, HEAD), F32)] + _exchange_scratch(n),
        compiler_params=_params(52, dimension_semantics=("arbitrary",)),
    )(dy, z, z, z, yc, y2, conv_w_all, cn_g, cn_b, w_pw2, dz, *grads)


def _inproj_bwd_dx(dz, x, d_h1, ln_g, w_in_all, grads, modes):
    t = x.shape[0]
    tt = min(256, t)

    def body(dz_ref, x_ref, dh1_ref, g_ref, w_hbm, dx_ref, dg_ref, w_vmem):
        @pl.when(pl.program_id(0) == 0)
        def _():
            pltpu.sync_copy(w_hbm, w_vmem)
            dg_ref[...] = jnp.zeros_like(dg_ref)

        du = jnp.zeros((tt, D), F32)
        for d in range(N_DEV):
            du = du + lax.dot_general(
                dz_ref[:, COLS_PER_DEV * d:COLS_PER_DEV * (d + 1)], w_vmem[d],
                (((1,), (1,)), ((), ())), preferred_element_type=F32)
        xv = x_ref[...]
        rstd = lax.rsqrt(jnp.mean(xv * xv, axis=-1, keepdims=True) + EPS)
        xhat = xv * rstd
        dg_ref[...] += _rowsum8(du * xhat)
        dx_ref[...] = dh1_ref[...] + _rms_bwd(du * g_ref[...], xhat, rstd)

    row = pl.BlockSpec((tt, D), lambda i: (i, 0))
    n = len(modes)
    return pl.pallas_call(
        _hosted(body, 5, 2, (t // tt,), modes), name="inproj_bwd_dx", grid=(t // tt,),
        out_shape=[jax.ShapeDtypeStruct((t, D), F32), jax.ShapeDtypeStruct((SUBLANES, D), F32)]
        + _recv_shapes(grads, modes),
        in_specs=[pl.BlockSpec((tt, N_COLS), lambda i: (i, 0)), row, row, _full((1, D)), ANY]
        + [ANY] * n,
        out_specs=[row, _full((SUBLANES, D))] + [ANY] * n,
        scratch_shapes=[pltpu.VMEM((N_DEV, D, COLS_PER_DEV), BF16)] + _exchange_scratch(n),
        compiler_params=_params(48, dimension_semantics=("arbitrary",)),
    )(dz, x, d_h1, ln_g, w_in_all, *grads)


def _inproj_bwd_dw(name, u, dz, first, count, grads=(), modes=()):
    t = u.shape[0]
    tt = min(512, t)
    grid = (count, t // tt)
    n = len(modes)

    def body(u_ref, dz_ref, dw_ref):
        @pl.when(pl.program_id(1) == 0)
        def _():
            dw_ref[...] = jnp.zeros_like(dw_ref)

        dw_ref[0] += lax.dot_general(u_ref[...], dz_ref[...], (((0,), (0,)), ((), ())),
                                     preferred_element_type=F32)

    return pl.pallas_call(
        _hosted(body, 2, 1, grid, modes) if n else body, name=name, grid=grid,
        out_shape=[jax.ShapeDtypeStruct((count, D, COLS_PER_DEV), F32)]
        + _recv_shapes(grads, modes),
        in_specs=[pl.BlockSpec((tt, D), lambda j, i: (i, 0)),
                  pl.BlockSpec((tt, COLS_PER_DEV), lambda j, i: (i, first + j))] + [ANY] * n,
        out_specs=[pl.BlockSpec((1, D, COLS_PER_DEV), lambda j, i: (j, 0, 0))] + [ANY] * n,
        scratch_shapes=_exchange_scratch(n) if n else [],
        compiler_params=_params(32, dimension_semantics=("arbitrary", "arbitrary")),
    )(u, dz, *grads)


def _adamw(w, g, m, v):
    m = ADAM_B1 * m + (1.0 - ADAM_B1) * g
    v = ADAM_B2 * v + (1.0 - ADAM_B2) * (g * g)
    m_hat = m / (1.0 - ADAM_B1 ** ADAM_STEP)
    v_hat = v / (1.0 - ADAM_B2 ** ADAM_STEP)
    delta = -ADAM_LR * (m_hat / (jnp.sqrt(v_hat) + ADAM_EPS) + ADAM_WD * w)
    return delta, m, v


def _pack_small(partials):
    rows = sorted(partials)

    def body(*refs):
        ins, out_ref = refs[:-1], refs[-1]
        out_ref[...] = jnp.zeros_like(out_ref)
        for j, row in enumerate(rows):
            out_ref[row:row + 1, :] = jnp.sum(ins[j][...], axis=0, keepdims=True)

    return pl.pallas_call(
        body, name="pack_small", out_shape=jax.ShapeDtypeStruct((N_SMALL, D), F32),
    )(*[partials[row] for row in rows])


def _sum_adam(name, recvs, w, m, v, rows):
    r, c = w.shape
    n = len(recvs)

    def body(*refs):
        w_ref, m_ref, v_ref, g_ref, d_ref, mo_ref, vo_ref = refs[n:]

        def finish(recv_ref):
            g = recv_ref[0]
            for s in range(1, recv_ref.shape[0]):
                g = g + recv_ref[s]
            g_ref[...] = g
            d_ref[...], mo_ref[...], vo_ref[...] = _adamw(w_ref[...], g, m_ref[...], v_ref[...])

        if n == 1:
            finish(refs[0])
        else:
            for side in range(n):
                pl.when(lax.axis_index("x") == side)(functools.partial(finish, refs[side]))

    blk = pl.BlockSpec((rows, c), lambda i: (i, 0))
    return pl.pallas_call(
        body, name=name, grid=(r // rows,),
        out_shape=[jax.ShapeDtypeStruct((r, c), F32)] * 4,
        in_specs=[pl.BlockSpec((rv.shape[0], rows, c), lambda i: (0, i, 0)) for rv in recvs]
        + [blk, blk, blk],
        out_specs=[blk] * 4,
        compiler_params=_params(48, dimension_semantics=("arbitrary",)),
    )(*recvs, w, m, v)


def _small_adam(gathered, lb_logits, w, m, v):
    def body(ga_ref, lbl_ref, w_ref, m_ref, v_ref, g_ref, d_ref, mo_ref, vo_ref, loss_ref):
        g = ga_ref[0]
        for s in range(1, N_DEV):
            g = g + ga_ref[s]
        s0, s1 = _lower_bound(lbl_ref[...])
        d_lb = g[R_LB0:R_LB0 + 1, :]
        rows = lax.broadcasted_iota(jnp.int32, (N_SMALL, D), 0)
        g = jnp.where(rows == R_LB0, d_lb * s0 * (1.0 - s0), g)
        g = jnp.where(rows == R_LB1, -d_lb * s0 * s1, g)
        g_ref[...] = g
        d_ref[...], mo_ref[...], vo_ref[...] = _adamw(w_ref[...], g, m_ref[...], v_ref[...])
        loss_ref[...] = (0.5 / D) * jnp.sum(g[R_LOSS:R_LOSS + 1, :], axis=-1, keepdims=True)

    return pl.pallas_call(
        body, name="small_adam",
        out_shape=[jax.ShapeDtypeStruct((N_SMALL, D), F32)] * 4 + [jax.ShapeDtypeStruct((1, 1), F32)],
    )(gathered, lb_logits, w, m, v)


def _pad_rows(a, rows):
    return jnp.pad(a, ((0, rows - a.shape[0]), (0, 0)))


def _pack_rows(rows):
    rows = [r.reshape(-1, D) for r in rows]
    packed = jnp.concatenate(rows, axis=0)
    return _pad_rows(packed, N_SMALL)


def kernel(x, p, ln_g, w_in, conv_w, conv_b, cnorm_g, cnorm_b, w_pw2, b_pw2, lb_logits, onorm_g, w_out, pe_norm_g, w_pg, w_pp, final_g, loss_target, m_ln_g, m_w_in, m_conv_w, m_conv_b, m_cnorm_g, m_cnorm_b, m_w_pw2, m_b_pw2, m_lb_logits, m_onorm_g, m_w_out, m_pe_norm_g, m_w_pg, m_w_pp, m_final_g, v_ln_g, v_w_in, v_conv_w, v_conv_b, v_cnorm_g, v_cnorm_b, v_w_pw2, v_b_pw2, v_lb_logits, v_onorm_g, v_w_out, v_pe_norm_g, v_w_pg, v_w_pp, v_final_g):
    t = x.shape[1]
    x2 = x.reshape(t, D)
    p2 = p.reshape(t, PLE)
    tg2 = loss_target.reshape(t, D)
    fin_g = final_g.reshape(1, D)

    z, u, w_in_all, conv_w_all, w_pw2_all = _inproj_fwd(
        x2, ln_g, w_in[0].astype(BF16), [_pad_rows(conv_w[0], CONV_PAD), w_pw2[0].astype(BF16)])
    o_raw, y_hgrn, states, w_out_all, w_pg_all, w_pp_all = _hgrn_fwd(
        z, lb_logits, onorm_g, [w_out[0].astype(BF16), w_pg[0].astype(BF16), w_pp[0].astype(BF16)])
    w_pw2_full = w_pw2_all.reshape(D, D)
    w_out_full = w_out_all.reshape(2 * D, D)
    w_pg_full = w_pg_all.reshape(D, D)
    yc, y2, y_conv = _conv_fwd(z, conv_w_all, conv_b, cnorm_g, cnorm_b, w_pw2_full, b_pw2)

    (d_h1, dy_conv, dy_hgrn, d_w_out, d_w_pg, d_w_pp, d_pen_p, d_fin_p, loss_p) = _tail(
        x2, y_conv, y_hgrn, p2, tg2, w_out_full, w_pg_full, w_pp_all, pe_norm_g, fin_g)

    dz, d_on_p, d_lb_p, r_w_out, r_w_pg, r_w_pp = _hgrn_bwd(
        dy_hgrn, z, o_raw, states, lb_logits, onorm_g,
        [d_w_out.reshape(N_DEV, 2 * D // N_DEV, D), d_w_pg.reshape(N_DEV, D // N_DEV, D), d_w_pp])
    (d_w_in_hi,) = _inproj_bwd_dw("inproj_bwd_dw_hi", u, dz, N_DEV // 2, N_DEV // 2)
    dz, d_w_pw2, d_conv_w, d_b2_p, d_cng_p, d_cnb_p, d_cb_p, r_w_in_hi = _conv_bwd(
        dy_conv, z, yc, y2, conv_w_all, cnorm_g, cnorm_b, w_pw2_full, dz, [d_w_in_hi], [1])
    d_w_in_lo, r_w_pw2, r_conv_w = _inproj_bwd_dw(
        "inproj_bwd_dw_lo", u, dz, 0, N_DEV // 2,
        [d_w_pw2.reshape(N_DEV, D // N_DEV, D), d_conv_w], ["scatter", "scatter"])
    chip_lo = _pair_reduce("pair_reduce_lo", d_w_in_lo)
    grad_x, d_ln_p, r_w_in_lo = _inproj_bwd_dx(
        dz, x2, d_h1, ln_g, w_in_all, [chip_lo], [("chip", 0)])

    small = _pack_small({R_LN: d_ln_p, R_CONVB: d_cb_p, R_CNG: d_cng_p, R_CNB: d_cnb_p,
                         R_BPW2: d_b2_p, R_LB0: d_lb_p, R_ON: d_on_p, R_PEN: d_pen_p,
                         R_FIN: d_fin_p, R_LOSS: loss_p})
    (small_all,) = _exchange_call("gather_small", [small], ["gather"])

    big = {}
    big["w_in"] = _sum_adam("adam_w_in", [r_w_in_lo, r_w_in_hi], w_in[0], m_w_in[0], v_w_in[0], 128)
    cw = _sum_adam("adam_conv_w", [r_conv_w], _pad_rows(conv_w[0], CONV_PAD),
                   _pad_rows(m_conv_w[0], CONV_PAD), _pad_rows(v_conv_w[0], CONV_PAD), CONV_PAD)
    big["conv_w"] = [a[:CONV_K] for a in cw]
    big["w_pw2"] = _sum_adam("adam_w_pw2", [r_w_pw2], w_pw2[0], m_w_pw2[0], v_w_pw2[0], 128)
    big["w_out"] = _sum_adam("adam_w_out", [r_w_out], w_out[0], m_w_out[0], v_w_out[0], 128)
    big["w_pg"] = _sum_adam("adam_w_pg", [r_w_pg], w_pg[0], m_w_pg[0], v_w_pg[0], 128)
    big["w_pp"] = _sum_adam("adam_w_pp", [r_w_pp], w_pp[0], m_w_pp[0], v_w_pp[0], PLE)

    small_w = [ln_g, conv_b, cnorm_g, cnorm_b, b_pw2, lb_logits, onorm_g, pe_norm_g, final_g]
    small_m = [m_ln_g, m_conv_b, m_cnorm_g, m_cnorm_b, m_b_pw2, m_lb_logits, m_onorm_g,
               m_pe_norm_g, m_final_g]
    small_v = [v_ln_g, v_conv_b, v_cnorm_g, v_cnorm_b, v_b_pw2, v_lb_logits, v_onorm_g,
               v_pe_norm_g, v_final_g]
    sg, sd, sm, sv, loss = _small_adam(small_all, lb_logits, _pack_rows(small_w),
                                       _pack_rows(small_m), _pack_rows(small_v))

    small_rows = {"ln_g": (R_LN, 1), "conv_b": (R_CONVB, 1), "cnorm_g": (R_CNG, 1),
                  "cnorm_b": (R_CNB, 1), "b_pw2": (R_BPW2, 1), "lb_logits": (R_LB0, 2),
                  "onorm_g": (R_ON, 1), "pe_norm_g": (R_PEN, 1), "final_g": (R_FIN, 1)}
    order = ["ln_g", "w_in", "conv_w", "conv_b", "cnorm_g", "cnorm_b", "w_pw2", "b_pw2",
             "lb_logits", "onorm_g", "w_out", "pe_norm_g", "w_pg", "w_pp", "final_g"]

    def leaf(kind, name):
        if name in big:
            return big[name][kind][None]
        r0, n = small_rows[name]
        a = (sg, sd, sm, sv)[kind][r0:r0 + n]
        return a.reshape(D) if name == "final_g" else a

    outs = [loss.reshape(()), grad_x.reshape(1, t, D)]
    for kind in range(4):
        outs += [leaf(kind, name) for name in order]
    return tuple(outs)
```

```python
import functools

import jax
import jax.numpy as jnp
from jax import lax
from jax.experimental import pallas as pl
from jax.experimental.pallas import tpu as pltpu

F32 = jnp.float32
BF16 = jnp.bfloat16
MESH = pl.DeviceIdType.MESH

N_DEV = 8
D = 1024
N_COLS = 7 * D
COLS_PER_DEV = N_COLS // N_DEV
PLE = 256
HEAD = 128
N_HEADS = D // HEAD
CONV_K = 31
CONV_PAD = 32
CHUNK = 64
EPS = 1e-6
SUBLANES = 8

ADAM_LR = 0.001
ADAM_B1 = 0.9
ADAM_B2 = 0.999
ADAM_EPS = 1e-08
ADAM_WD = 0.01
ADAM_STEP = 10

MIB = 1024 * 1024
N_SMALL = 16
R_LN, R_CONVB, R_CNG, R_CNB, R_BPW2, R_LB0, R_LB1, R_ON, R_PEN, R_FIN, R_LOSS = range(11)


def _params(vmem_mib, **kw):
    return pltpu.CompilerParams(vmem_limit_bytes=vmem_mib * MIB, **kw)


def _dot(a, b):
    return jnp.dot(a.astype(BF16), b.astype(BF16), preferred_element_type=F32)


def _dot_nt(a, b):
    return lax.dot_general(a.astype(BF16), b.astype(BF16), (((1,), (1,)), ((), ())),
                           preferred_element_type=F32)


def _dot_tn(a, b):
    return lax.dot_general(a.astype(BF16), b.astype(BF16), (((0,), (0,)), ((), ())),
                           preferred_element_type=F32)


def _split(a):
    hi = a.astype(BF16)
    return hi, (a - hi.astype(F32)).astype(BF16)


def _dot_split(a, b, dims):
    ah, al = _split(a)
    bh, bl = _split(b)
    dg = lambda p, q: lax.dot_general(p, q, dims, preferred_element_type=F32)
    return dg(ah, bh) + (dg(ah, bl) + dg(al, bh))


def _sigmoid(x):
    return 1.0 / (1.0 + jnp.exp(-x))


def _rowsum8(a):
    r, c = a.shape
    return jnp.sum(a.reshape(r // SUBLANES, SUBLANES, c), axis=0)


def _tri_dot(tri, a):
    hi = a.astype(BF16)
    r1 = a - hi.astype(F32)
    mid = r1.astype(BF16)
    lo = (r1 - mid.astype(F32)).astype(BF16)
    return (jnp.dot(tri, hi, preferred_element_type=F32)
            + jnp.dot(tri, mid, preferred_element_type=F32)
            + jnp.dot(tri, lo, preferred_element_type=F32))


def _lower_bound(lbl):
    l0, l1 = lbl[0:1, :], lbl[1:2, :]
    m = jnp.maximum(l0, l1)
    e0, e1 = jnp.exp(l0 - m), jnp.exp(l1 - m)
    s = e0 + e1
    return e0 / s, e1 / s


ANY = pl.BlockSpec(memory_space=pl.ANY)


def _full(shape):
    return pl.BlockSpec(shape, lambda i: (0,) * len(shape))


def _peer(x, y, c, k):
    px = 1 - x if k & 4 else x
    py = 1 - y if k & 2 else y
    pc = 1 - c if k & 1 else c
    return (px, py, pc), 4 * px + 2 * py + pc


class _Exchange:
    def __init__(self, srcs, outs, modes, send_sems, recv_sems, local_sems):
        x, y, c = lax.axis_index("x"), lax.axis_index("y"), lax.axis_index("c")
        me = 4 * x + 2 * y + c
        self.starts, self.send_waits, self.recv_waits = [], [], []

        def remote(a, k, src, slot, peer, when):
            sem = a * N_DEV + k
            cp = pltpu.make_async_remote_copy(
                src_ref=src, dst_ref=outs[a].at[slot], send_sem=send_sems.at[sem],
                recv_sem=recv_sems.at[sem], device_id=peer, device_id_type=MESH)
            self.starts.append((when, cp.start))
            self.send_waits.append((when, cp.wait_send))

        def arrival(a, k, slot, when):
            sem = a * N_DEV + k
            cp = pltpu.make_async_remote_copy(
                src_ref=outs[a].at[slot], dst_ref=outs[a].at[slot], send_sem=send_sems.at[sem],
                recv_sem=recv_sems.at[sem], device_id=(x, y, c), device_id_type=MESH)
            self.recv_waits.append((when, cp.wait_recv))

        def local(a, src, slot, when):
            cp = pltpu.make_async_copy(src, outs[a].at[slot], local_sems.at[a])
            self.starts.append((when, cp.start))
            self.send_waits.append((when, cp.wait))

        for a, (src, mode) in enumerate(zip(srcs, modes)):
            if mode in ("gather", "scatter"):
                local(a, src if mode == "gather" else src.at[me], me, None)
                for k in range(1, N_DEV):
                    peer, peer_idx = _peer(x, y, c, k)
                    remote(a, k, src if mode == "gather" else src.at[peer_idx], me, peer, None)
                    arrival(a, k, peer_idx, None)
                continue
            if isinstance(mode, tuple):
                here, away = x == mode[1], x != mode[1]
                chip = 2 * x + y
                local(a, src.at[y], chip, here)
                remote(a, 1, src.at[1 - y], chip, (x, 1 - y, c), here)
                remote(a, 2, src.at[y], chip, (1 - x, y, c), away)
                remote(a, 3, src.at[1 - y], chip, (1 - x, 1 - y, c), away)
                arrival(a, 1, 2 * x + 1 - y, here)
                arrival(a, 2, 2 * (1 - x) + y, here)
                arrival(a, 3, 2 * (1 - x) + 1 - y, here)
                continue
            here, away = x == mode, x != mode
            for kk in range(4):
                py = 1 - y if kk & 2 else y
                pc = 1 - c if kk & 1 else c
                block = src.at[2 * py + pc]
                if kk == 0:
                    local(a, block, me, here)
                else:
                    remote(a, kk, block, me, (x, py, pc), here)
                remote(a, 4 + kk, block, me, (1 - x, py, pc), away)
            for k in range(1, N_DEV):
                arrival(a, k, _peer(x, y, c, k)[1], here)

    @staticmethod
    def _run(actions):
        for when, fn in actions:
            if when is None:
                fn()
            else:
                pl.when(when)(fn)

    def start(self):
        self._run(self.starts)

    def wait(self):
        self._run(self.recv_waits)
        self._run(self.send_waits)


def _exchange_scratch(n):
    return [pltpu.SemaphoreType.DMA((n * N_DEV,)), pltpu.SemaphoreType.DMA((n * N_DEV,)),
            pltpu.SemaphoreType.DMA((n,))]


def _recv_shapes(srcs, modes):
    def shape(s, m):
        if m == "gather":
            return (N_DEV,) + s.shape
        return (N_DEV // 2 if isinstance(m, tuple) else N_DEV,) + s.shape[1:]

    return [jax.ShapeDtypeStruct(shape(s, m), s.dtype) for s, m in zip(srcs, modes)]


def _pair_reduce(name, blocks):
    shape = (2,) + blocks.shape[1:]

    def body(src, out_ref, stage, mine, send_sems, recv_sems, local_sems):
        x, y, c = lax.axis_index("x"), lax.axis_index("y"), lax.axis_index("c")
        sends, waits = [], []
        for py in range(2):
            sends.append(pltpu.make_async_remote_copy(
                src_ref=src.at[2 * py + 1 - c], dst_ref=stage.at[py], send_sem=send_sems.at[py],
                recv_sem=recv_sems.at[py], device_id=(x, y, 1 - c), device_id_type=MESH))
            waits.append(pltpu.make_async_copy(src.at[2 * py + c], mine.at[py], local_sems.at[py]))
        for cp in sends + waits:
            cp.start()
        for cp in waits:
            cp.wait()
        for cp in sends:
            cp.wait_recv()
        out_ref[...] = mine[...] + stage[...]
        for cp in sends:
            cp.wait_send()

    return pl.pallas_call(
        body, name=name, out_shape=jax.ShapeDtypeStruct(shape, F32), in_specs=[ANY],
        scratch_shapes=[pltpu.VMEM(shape, F32), pltpu.VMEM(shape, F32),
                        pltpu.SemaphoreType.DMA((2,)), pltpu.SemaphoreType.DMA((2,)),
                        pltpu.SemaphoreType.DMA((2,))],
        compiler_params=_params(40),
    )(blocks)


def _exchange_call(name, srcs, modes):
    n = len(srcs)

    def body(*refs):
        xch = _Exchange(refs[:n], refs[n:2 * n], modes, *refs[2 * n:])
        xch.start()
        xch.wait()

    return pl.pallas_call(
        body, name=name, out_shape=_recv_shapes(srcs, modes),
        in_specs=[ANY] * n, out_specs=[ANY] * n, scratch_shapes=_exchange_scratch(n),
    )(*srcs)


def _hosted(body, n_in, n_out, grid, modes):
    n = len(modes)

    def hosted(*refs):
        ins, srcs = refs[:n_in], refs[n_in:n_in + n]
        outs = refs[n_in + n:n_in + n + n_out]
        bufs = refs[n_in + n + n_out:n_in + 2 * n + n_out]
        scratch = refs[n_in + 2 * n + n_out:-3]
        xch = _Exchange(srcs, bufs, modes, *refs[-3:])
        first, last = True, True
        for axis, size in enumerate(grid):
            first = jnp.logical_and(first, pl.program_id(axis) == 0)
            last = jnp.logical_and(last, pl.program_id(axis) == size - 1)
        pl.when(first)(xch.start)
        body(*ins, *outs, *scratch)
        pl.when(last)(xch.wait)

    return hosted


N_CHIPS = N_DEV // 2
PAIR_COLS = 2 * COLS_PER_DEV
PUSHED = (1, 2, 4, 6)
FORWARDED = (2, 4, 6)


def _inproj_fwd(x, ln_g, w_shard, other_shards):
    t = x.shape[0]
    tt = min(512, t)
    n_t = t // tt
    modes = ["gather"] * len(other_shards)
    n = len(modes)
    chip = 2 * lax.axis_index("x") + lax.axis_index("y")
    order = jnp.bitwise_xor(chip, jnp.arange(N_CHIPS, dtype=jnp.int32)).astype(jnp.int32)

    def body(order_ref, x_ref, g_ref, shard_hbm, *refs):
        srcs = refs[:n]
        z_ref, u_ref, w_all = refs[n:n + 3]
        bufs = refs[n + 3:2 * n + 3]
        u_all, w_blk, w_send, w_recv, w_local = refs[2 * n + 3:2 * n + 8]
        p, i = pl.program_id(0), pl.program_id(1)
        x, y, c = lax.axis_index("x"), lax.axis_index("y"), lax.axis_index("c")
        mine = 4 * x + 2 * y + c
        others = _Exchange(srcs, bufs, modes, *refs[2 * n + 8:])

        def push(k):
            peer, _ = _peer(x, y, c, k)
            return pltpu.make_async_remote_copy(
                src_ref=shard_hbm, dst_ref=w_all.at[mine], send_sem=w_send.at[k],
                recv_sem=w_recv.at[k], device_id=peer, device_id_type=MESH)

        def forward(k):
            _, owner = _peer(x, y, c, k)
            return pltpu.make_async_remote_copy(
                src_ref=w_all.at[owner], dst_ref=w_all.at[owner], send_sem=w_send.at[k + 1],
                recv_sem=w_recv.at[k + 1], device_id=(x, y, 1 - c), device_id_type=MESH)

        def landed(k):
            _, owner = _peer(x, y, c, k)
            return pltpu.make_async_remote_copy(
                src_ref=w_all.at[owner], dst_ref=w_all.at[owner], send_sem=w_send.at[k],
                recv_sem=w_recv.at[k], device_id=(x, y, c), device_id_type=MESH)

        keep = pltpu.make_async_copy(shard_hbm, w_all.at[mine], w_local.at[0])

        def load_pair(step):
            same = shard_hbm if step == 0 else w_all.at[_peer(x, y, c, 2 * step)[1]]
            other = w_all.at[_peer(x, y, c, 2 * step + 1)[1]]
            for side in range(2):
                @pl.when(c == side)
                def _(side=side):
                    pltpu.sync_copy(same, w_blk.at[:, pl.ds(COLS_PER_DEV * side, COLS_PER_DEV)])
                    pltpu.sync_copy(
                        other, w_blk.at[:, pl.ds(COLS_PER_DEV * (1 - side), COLS_PER_DEV)])

        @pl.when(jnp.logical_and(p == 0, i == 0))
        def _():
            for k in PUSHED:
                push(k).start()
            keep.start()
            others.start()

        for step in range(N_CHIPS):
            @pl.when(jnp.logical_and(p == step, i == 0))
            def _(step=step):
                landed(2 * step + 1).wait_recv()
                load_pair(step)

        rows = pl.ds(pl.multiple_of(i * tt, tt), tt)

        @pl.when(p == 0)
        def _():
            xv = x_ref[...]
            rstd = lax.rsqrt(jnp.mean(xv * xv, axis=-1, keepdims=True) + EPS)
            ub = (xv * rstd * g_ref[...]).astype(BF16)
            u_ref[...] = ub
            u_all[rows, :] = ub

        z_ref[...] = jnp.dot(u_all[rows, :], w_blk[...], preferred_element_type=F32)

        for step in range(1, N_CHIPS):
            @pl.when(jnp.logical_and(p == step - 1, i == n_t - 1))
            def _(step=step):
                landed(2 * step).wait_recv()
                forward(2 * step).start()

        @pl.when(jnp.logical_and(p == N_CHIPS - 1, i == n_t - 1))
        def _():
            for k in PUSHED:
                push(k).wait_send()
            for k in FORWARDED:
                forward(k).wait_send()
            keep.wait()
            others.wait()

    first_pass = lambda p, i, order_ref: (jnp.where(p == 0, i, n_t - 1), 0)
    grid_spec = pltpu.PrefetchScalarGridSpec(
        num_scalar_prefetch=1, grid=(N_CHIPS, n_t),
        in_specs=[pl.BlockSpec((tt, D), first_pass),
                  pl.BlockSpec((1, D), lambda p, i, order_ref: (0, 0)), ANY] + [ANY] * n,
        out_specs=[pl.BlockSpec((tt, PAIR_COLS), lambda p, i, order_ref: (i, order_ref[p])),
                   pl.BlockSpec((tt, D), first_pass), ANY] + [ANY] * n,
        scratch_shapes=[pltpu.VMEM((t, D), BF16), pltpu.VMEM((D, PAIR_COLS), BF16),
                        pltpu.SemaphoreType.DMA((N_DEV,)), pltpu.SemaphoreType.DMA((N_DEV,)),
                        pltpu.SemaphoreType.DMA((1,))] + _exchange_scratch(n))
    return pl.pallas_call(
        body, name="inproj_fwd", grid_spec=grid_spec,
        out_shape=[jax.ShapeDtypeStruct((t, N_COLS), F32), jax.ShapeDtypeStruct((t, D), BF16),
                   jax.ShapeDtypeStruct((N_DEV,) + w_shard.shape, BF16)]
        + _recv_shapes(other_shards, modes),
        compiler_params=_params(48, dimension_semantics=("arbitrary", "arbitrary")),
    )(order, x, ln_g, w_shard, *other_shards)


def _shifted_copies(buf, shifted, rows):
    for b in range(1, SUBLANES):
        shifted[b, 0:rows, :] = buf[b:b + rows, :]


def _tap_ref(buf, shifted, offset):
    a, b = divmod(offset, SUBLANES)
    return (buf if b == 0 else shifted.at[b]), SUBLANES * a


def _group_norm_stats(blk):
    mu = jnp.mean(blk, axis=-1, keepdims=True)
    cen = blk - mu
    var = jnp.mean(cen * cen, axis=-1, keepdims=True)
    return cen * lax.rsqrt(var + EPS)


def _conv_fwd(z, conv_w_all, conv_b, cn_g, cn_b, w_pw2, b_pw2):
    t = z.shape[0]
    tt = min(256, t)
    rc = 32

    def body(val_ref, glu_ref, gate_ref, cw_ref, cb_ref, g_ref, b_ref, w_hbm, b2_ref,
             yc_ref, y2_ref, yo_ref, w_vmem, vbuf, vsh, y1buf):
        @pl.when(pl.program_id(0) == 0)
        def _():
            pltpu.sync_copy(w_hbm, w_vmem)
            vbuf[0:CONV_PAD, :] = jnp.zeros((CONV_PAD, D), F32)

        vbuf[CONV_PAD:CONV_PAD + tt, :] = val_ref[...] * _sigmoid(glu_ref[...])
        _shifted_copies(vbuf, vsh, tt + 24)

        def row_chunk(r, carry):
            r0 = pl.multiple_of(r * rc, rc)
            for g in range(N_HEADS):
                cs = slice(HEAD * g, HEAD * (g + 1))
                acc = jnp.zeros((rc, HEAD), F32)
                for k in range(CONV_K):
                    ref, off = _tap_ref(vbuf, vsh, k + 2)
                    acc = acc + cw_ref[g, k:k + 1, :] * ref[pl.ds(r0 + off, rc), cs]
                acc = acc + cb_ref[:, cs]
                yc_ref[pl.ds(r0, rc), cs] = acc
                n = _group_norm_stats(acc) * g_ref[:, cs] + b_ref[:, cs]
                y1buf[pl.ds(r0, rc), cs] = (n * _sigmoid(n)).astype(BF16)
            return carry

        lax.fori_loop(0, tt // rc, row_chunk, 0)
        vbuf[0:CONV_PAD, :] = vbuf[tt:tt + CONV_PAD, :]
        y2 = jnp.dot(y1buf[...], w_vmem[...], preferred_element_type=F32) + b2_ref[...]
        y2_ref[...] = y2
        gate = gate_ref[...]
        yo_ref[...] = (y2 * gate * _sigmoid(gate)).astype(BF16)

    col = lambda j: pl.BlockSpec((tt, D), lambda i: (i, j))
    row = pl.BlockSpec((tt, D), lambda i: (i, 0))
    return pl.pallas_call(
        body, name="conv_fwd", grid=(t // tt,),
        out_shape=[jax.ShapeDtypeStruct((t, D), F32), jax.ShapeDtypeStruct((t, D), F32),
                   jax.ShapeDtypeStruct((t, D), BF16)],
        in_specs=[col(0), col(1), col(2), _full((N_DEV, CONV_PAD, HEAD)), _full((1, D)),
                  _full((1, D)), _full((1, D)), ANY, _full((1, D))],
        out_specs=[row, row, row],
        scratch_shapes=[pltpu.VMEM((D, D), BF16), pltpu.VMEM((tt + CONV_PAD, D), F32),
                        pltpu.VMEM((SUBLANES, tt + CONV_PAD, D), F32), pltpu.VMEM((tt, D), BF16)],
        compiler_params=_params(48, dimension_semantics=("arbitrary",)),
    )(z, z, z, conv_w_all, conv_b, cn_g, cn_b, w_pw2, b_pw2)


def _chunk_quantities(zq, zf, lbh, tri):
    sig = _sigmoid(zf)
    sig_neg = _sigmoid(-zf)
    f = lbh + (1.0 - lbh) * sig
    k = (1.0 - lbh) * sig_neg
    q = zq * _sigmoid(zq)
    b = _tri_dot(tri, jnp.log(f))
    b_mid = b[CHUNK // 2 - 1:CHUNK // 2, :]
    b_last = b[CHUNK - 1:CHUNK, :]
    e_q = jnp.exp(b)
    e_qm = jnp.exp(b - b_mid)
    e_km = jnp.exp(b_mid - b)
    e_kd = jnp.exp(b_last - b)
    return q, k, f, sig, sig_neg, e_q, e_qm, e_km, e_kd, jnp.exp(b_last)


def _hgrn_fwd(z, lb_logits, onorm_g, shards):
    t = z.shape[0]
    tt = min(256, t)
    nc = tt // CHUNK
    modes = ["gather"] * len(shards)
    n = len(modes)

    def body(q_ref, f_ref, i_ref, g_ref, lbl_ref, on_ref, o_ref, y_ref, s_ref, st):
        @pl.when(pl.program_id(0) == 0)
        def _():
            st[...] = jnp.zeros_like(st)

        lb, _ = _lower_bound(lbl_ref[...])
        rows = lax.broadcasted_iota(jnp.int32, (CHUNK, CHUNK), 0)
        cols = lax.broadcasted_iota(jnp.int32, (CHUNK, CHUNK), 1)
        causal = rows >= cols
        tri = causal.astype(BF16)

        def chunk(c, carry):
            r0 = pl.multiple_of(c * CHUNK, CHUNK)
            rs = pl.ds(r0, CHUNK)
            for h in range(N_HEADS):
                cs = slice(HEAD * h, HEAD * (h + 1))
                q, k, _, _, _, e_q, e_qm, e_km, e_kd, e_last = _chunk_quantities(
                    q_ref[rs, cs], f_ref[rs, cs], lb[:, cs], tri)
                v = i_ref[rs, cs]
                s_old = st[h]
                s_ref[c, h] = s_old
                a = jnp.where(causal, _dot_nt(q * e_qm, k * e_km), 0.0)
                o = _dot_nt(q * e_q, s_old) + _dot(a, v)
                st[h] = s_old * e_last + _dot_tn(v, k * e_kd)
                o_ref[rs, cs] = o
                n = o * lax.rsqrt(jnp.mean(o * o, axis=-1, keepdims=True) + EPS)
                zg = g_ref[rs, cs]
                y_ref[rs, cs] = (n * on_ref[:, cs] * zg * _sigmoid(zg)).astype(BF16)
            return carry

        lax.fori_loop(0, nc, chunk, 0)

    col = lambda j: pl.BlockSpec((tt, D), lambda i: (i, j))
    row = pl.BlockSpec((tt, D), lambda i: (i, 0))
    return pl.pallas_call(
        _hosted(body, 6, 3, (t // tt,), modes), name="hgrn_fwd", grid=(t // tt,),
        out_shape=[jax.ShapeDtypeStruct((t, D), F32), jax.ShapeDtypeStruct((t, D), BF16),
                   jax.ShapeDtypeStruct((t // CHUNK, N_HEADS, HEAD, HEAD), F32)]
        + _recv_shapes(shards, modes),
        in_specs=[col(3), col(4), col(5), col(6), _full((2, D)), _full((1, D))] + [ANY] * n,
        out_specs=[row, row, pl.BlockSpec((nc, N_HEADS, HEAD, HEAD), lambda i: (i, 0, 0, 0))]
        + [ANY] * n,
        scratch_shapes=[pltpu.VMEM((N_HEADS, HEAD, HEAD), F32)] + _exchange_scratch(n),
        compiler_params=_params(40, dimension_semantics=("arbitrary",)),
    )(z, z, z, z, lb_logits, onorm_g, *shards)


def _rms_bwd(dn, xhat, rstd):
    return rstd * (dn - xhat * jnp.mean(dn * xhat, axis=-1, keepdims=True))


def _tail(x, y_conv, y_hgrn, p, target, w_out, w_pg, w_pp_all, pe_g, fin_g):
    t = x.shape[0]
    tt = min(256, t)
    n_steps = t // tt

    def body(x_ref, yc_ref, yh_ref, p_ref, tg_ref, wo_hbm, wg_hbm, wp_hbm, pg_ref, fg_ref,
             dh1_ref, dyc_ref, dyh_ref, dwo_hbm, dwg_hbm, dwp_hbm, dpg_ref, dfg_ref, loss_ref,
             wo, wg, wp, dwo, dwg, dwp):
        i = pl.program_id(0)

        @pl.when(i == 0)
        def _():
            pltpu.sync_copy(wo_hbm, wo)
            pltpu.sync_copy(wg_hbm, wg)
            for d in range(N_DEV):
                pltpu.sync_copy(wp_hbm.at[d], wp.at[:, pl.ds(HEAD * d, HEAD)])
            dwo[...] = jnp.zeros_like(dwo)
            dwg[...] = jnp.zeros_like(dwg)
            dwp[...] = jnp.zeros_like(dwp)
            dpg_ref[...] = jnp.zeros_like(dpg_ref)
            dfg_ref[...] = jnp.zeros_like(dfg_ref)
            loss_ref[...] = jnp.zeros_like(loss_ref)

        ycv, yhv = yc_ref[...], yh_ref[...]
        h1 = (x_ref[...] + jnp.dot(ycv, wo[0:D, :], preferred_element_type=F32)
              + jnp.dot(yhv, wo[D:2 * D, :], preferred_element_type=F32))
        pb = p_ref[...].astype(BF16)
        pe = jnp.dot(pb, wp[...], preferred_element_type=F32)
        rstd1 = lax.rsqrt(jnp.mean(h1 * h1, axis=-1, keepdims=True) + EPS)
        n1 = h1 * rstd1
        rb = (n1 * pg_ref[...]).astype(BF16)
        gate = _sigmoid(jnp.dot(rb, wg[...], preferred_element_type=F32))
        h2 = h1 + gate * pe
        rstd2 = lax.rsqrt(jnp.mean(h2 * h2, axis=-1, keepdims=True) + EPS)
        n2 = h2 * rstd2
        err = n2 * fg_ref[...] - tg_ref[...]
        loss_ref[...] += _rowsum8(err * err)

        d_out = err * (1.0 / D)
        dfg_ref[...] += _rowsum8(d_out * n2)
        d_h2 = _rms_bwd(d_out * fg_ref[...], n2, rstd2)
        d_pe = (d_h2 * gate).astype(BF16)
        d_gpre = (d_h2 * pe * gate * (1.0 - gate)).astype(BF16)
        dwg[...] += _dot_tn(rb, d_gpre)
        dwp[...] += _dot_tn(pb, d_pe)
        dr = _dot_nt(d_gpre, wg[...])
        dpg_ref[...] += _rowsum8(dr * n1)
        d_h1 = d_h2 + _rms_bwd(dr * pg_ref[...], n1, rstd1)
        dh1_ref[...] = d_h1
        d_h1b = d_h1.astype(BF16)
        dwo[0:D, :] += _dot_tn(ycv, d_h1b)
        dwo[D:2 * D, :] += _dot_tn(yhv, d_h1b)
        dyc_ref[...] = _dot_nt(d_h1b, wo[0:D, :])
        dyh_ref[...] = _dot_nt(d_h1b, wo[D:2 * D, :])

        @pl.when(i == n_steps - 1)
        def _():
            pltpu.sync_copy(dwo, dwo_hbm)
            pltpu.sync_copy(dwg, dwg_hbm)
            for d in range(N_DEV):
                pltpu.sync_copy(dwp.at[:, pl.ds(HEAD * d, HEAD)], dwp_hbm.at[d])

    row = pl.BlockSpec((tt, D), lambda i: (i, 0))
    acc = _full((SUBLANES, D))
    return pl.pallas_call(
        body, name="tail_fwd_bwd", grid=(n_steps,),
        out_shape=[jax.ShapeDtypeStruct((t, D), F32)] * 3
        + [jax.ShapeDtypeStruct((2 * D, D), F32), jax.ShapeDtypeStruct((D, D), F32),
           jax.ShapeDtypeStruct((N_DEV, PLE, HEAD), F32)]
        + [jax.ShapeDtypeStruct((SUBLANES, D), F32)] * 3,
        in_specs=[row, row, row, pl.BlockSpec((tt, PLE), lambda i: (i, 0)), row,
                  ANY, ANY, ANY, _full((1, D)), _full((1, D))],
        out_specs=[row, row, row, ANY, ANY, ANY, acc, acc, acc],
        scratch_shapes=[pltpu.VMEM((2 * D, D), BF16), pltpu.VMEM((D, D), BF16),
                        pltpu.VMEM((PLE, D), BF16), pltpu.VMEM((2 * D, D), F32),
                        pltpu.VMEM((D, D), F32), pltpu.VMEM((PLE, D), F32)],
        compiler_params=_params(52, dimension_semantics=("arbitrary",)),
    )(x, y_conv, y_hgrn, p, target, w_out, w_pg, w_pp_all, pe_g, fin_g)


def _hgrn_bwd(dy, z, o_raw, states, lb_logits, onorm_g, grads):
    t = z.shape[0]
    tt = min(256, t)
    nc = tt // CHUNK
    n_steps = t // tt
    modes = ["scatter"] * len(grads)

    def body(dy_ref, q_ref, f_ref, i_ref, g_ref, o_ref, s_ref, lbl_ref, on_ref,
             dz_ref, don_ref, dlb_ref, dst):
        @pl.when(pl.program_id(0) == 0)
        def _():
            dst[...] = jnp.zeros_like(dst)
            don_ref[...] = jnp.zeros_like(don_ref)
            dlb_ref[...] = jnp.zeros_like(dlb_ref)

        lb, _ = _lower_bound(lbl_ref[...])
        rows = lax.broadcasted_iota(jnp.int32, (CHUNK, CHUNK), 0)
        cols = lax.broadcasted_iota(jnp.int32, (CHUNK, CHUNK), 1)
        causal = rows >= cols
        tri = causal.astype(BF16)
        tri_rev = (rows <= cols).astype(BF16)
        is_last = lax.broadcasted_iota(jnp.int32, (CHUNK, HEAD), 0) == CHUNK - 1

        def chunk(cc, carry):
            c = nc - 1 - cc
            r0 = pl.multiple_of(c * CHUNK, CHUNK)
            rs = pl.ds(r0, CHUNK)
            for h in range(N_HEADS):
                cs = slice(HEAD * h, HEAD * (h + 1))
                zq, zf, zg = q_ref[rs, cs], f_ref[rs, cs], g_ref[rs, cs]
                lbh = lb[:, cs]
                q, k, f, sig, sig_neg, e_q, e_qm, e_km, e_kd, e_last = _chunk_quantities(
                    zq, zf, lbh, tri)
                v = i_ref[rs, cs]
                qt, qm, km, kd = q * e_q, q * e_qm, k * e_km, k * e_kd
                a = jnp.where(causal, _dot_nt(qm, km), 0.0)

                o = o_ref[rs, cs]
                rstd = lax.rsqrt(jnp.mean(o * o, axis=-1, keepdims=True) + EPS)
                n = o * rstd
                sg = _sigmoid(zg)
                dyv = dy_ref[rs, cs]
                on = on_ref[:, cs]
                d_zg = dyv * n * on * sg * (1.0 + zg * (1.0 - sg))
                d_on = dyv * zg * sg
                don_ref[:, cs] += _rowsum8(d_on * n)
                do = _rms_bwd(d_on * on, n, rstd)

                s_old = s_ref[c, h]
                ds_new = dst[h]
                da = jnp.where(causal, _dot_nt(do, v), 0.0)
                dv = _dot_tn(a, do) + _dot_nt(kd, ds_new)
                dkd = _dot(v, ds_new)
                dqm = _dot_split(da, km, (((1,), (0,)), ((), ())))
                dkm = _dot_split(da, qm, (((0,), (0,)), ((), ())))
                dq = _dot(do, s_old) * e_q + dqm * e_qm
                dk = dkm * e_km + dkd * e_kd
                dst[h] = ds_new * e_last + _dot_tn(do, qt)
                last = (jnp.sum(dkd * kd, axis=0, keepdims=True)
                        + e_last * jnp.sum(s_old * ds_new, axis=0, keepdims=True))
                db = q * dq - k * dk + jnp.where(is_last, last, 0.0)
                dlogf = _tri_dot(tri_rev, db)
                common = sig_neg * (dlogf / f - dk)
                dlb_ref[:, cs] += _rowsum8(common)
                c0 = 3 * D + HEAD * h
                dz_ref[rs, c0:c0 + HEAD] = (
                    dq * _sigmoid(zq) * (1.0 + zq * (1.0 - _sigmoid(zq)))).astype(BF16)
                dz_ref[rs, D + c0:D + c0 + HEAD] = ((1.0 - lbh) * sig * common).astype(BF16)
                dz_ref[rs, 2 * D + c0:2 * D + c0 + HEAD] = dv.astype(BF16)
                dz_ref[rs, 3 * D + c0:3 * D + c0 + HEAD] = d_zg.astype(BF16)
            return carry

        lax.fori_loop(0, nc, chunk, 0)

    rev = lambda i: n_steps - 1 - i
    col = lambda j: pl.BlockSpec((tt, D), lambda i: (rev(i), j))
    row = pl.BlockSpec((tt, D), lambda i: (rev(i), 0))
    acc = _full((SUBLANES, D))
    n = len(modes)
    return pl.pallas_call(
        _hosted(body, 9, 3, (n_steps,), modes), name="hgrn_bwd", grid=(n_steps,),
        out_shape=[jax.ShapeDtypeStruct((t, N_COLS), BF16),
                   jax.ShapeDtypeStruct((SUBLANES, D), F32),
                   jax.ShapeDtypeStruct((SUBLANES, D), F32)] + _recv_shapes(grads, modes),
        in_specs=[row, col(3), col(4), col(5), col(6), row,
                  pl.BlockSpec((nc, N_HEADS, HEAD, HEAD), lambda i: (rev(i), 0, 0, 0)),
                  _full((2, D)), _full((1, D))] + [ANY] * n,
        out_specs=[pl.BlockSpec((tt, N_COLS), lambda i: (rev(i), 0)), acc, acc] + [ANY] * n,
        scratch_shapes=[pltpu.VMEM((N_HEADS, HEAD, HEAD), F32)] + _exchange_scratch(n),
        compiler_params=_params(48, dimension_semantics=("arbitrary",)),
    )(dy, z, z, z, z, o_raw, states, lb_logits, onorm_g, *grads)


def _conv_bwd(dy, z, yc, y2, conv_w_all, cn_g, cn_b, w_pw2, dz, grads, modes):
    t = z.shape[0]
    tt = min(256, t)
    rc = 32
    n_steps = t // tt

    def body(dy_ref, val_ref, glu_ref, gate_ref, yc_ref, y2_ref, cw_ref, g_ref, b_ref, w_hbm,
             dz_in, dz_ref, dw_hbm, dcw_out, db2_ref, dg_ref, dbeta_ref, dcb_ref,
             w_vmem, dw, dbuf, dsh, y1buf, dnbuf, dcw_ref):
        i = pl.program_id(0)

        @pl.when(i == 0)
        def _():
            pltpu.sync_copy(w_hbm, w_vmem)
            dw[...] = jnp.zeros_like(dw)
            dbuf[tt:tt + CONV_PAD, :] = jnp.zeros((CONV_PAD, D), F32)
            dcw_ref[...] = jnp.zeros_like(dcw_ref)
            dcw_out[...] = jnp.zeros_like(dcw_out)
            db2_ref[...] = jnp.zeros_like(db2_ref)
            dg_ref[...] = jnp.zeros_like(dg_ref)
            dbeta_ref[...] = jnp.zeros_like(dbeta_ref)
            dcb_ref[...] = jnp.zeros_like(dcb_ref)

        gate = gate_ref[...]
        sg = _sigmoid(gate)
        dyv = dy_ref[...]
        dy2 = dyv * gate * sg
        dz_ref[:, 2 * D:3 * D] = (dyv * y2_ref[...] * sg * (1.0 + gate * (1.0 - sg))).astype(BF16)
        db2_ref[...] += _rowsum8(dy2)
        dy2b = dy2.astype(BF16)
        dnbuf[...] = _dot_nt(dy2b, w_vmem[...])

        def norm_chunk(r, carry):
            r0 = pl.multiple_of(r * rc, rc)
            rs = pl.ds(r0, rc)
            for g in range(N_HEADS):
                cs = slice(HEAD * g, HEAD * (g + 1))
                blk = yc_ref[rs, cs]
                mu = jnp.mean(blk, axis=-1, keepdims=True)
                cen = blk - mu
                rstd = lax.rsqrt(jnp.mean(cen * cen, axis=-1, keepdims=True) + EPS)
                xhat = cen * rstd
                n = xhat * g_ref[:, cs] + b_ref[:, cs]
                sn = _sigmoid(n)
                y1buf[rs, cs] = (n * sn).astype(BF16)
                dn = dnbuf[rs, cs] * sn * (1.0 + n * (1.0 - sn))
                dg_ref[:, cs] += _rowsum8(dn * xhat)
                dbeta_ref[:, cs] += _rowsum8(dn)
                dxh = dn * g_ref[:, cs]
                dyc = rstd * (dxh - jnp.mean(dxh, axis=-1, keepdims=True)
                              - xhat * jnp.mean(dxh * xhat, axis=-1, keepdims=True))
                dcb_ref[:, cs] += _rowsum8(dyc)
                dbuf[rs, cs] = dyc
            return carry

        lax.fori_loop(0, tt // rc, norm_chunk, 0)
        dw[...] += _dot_tn(y1buf[...], dy2b)
        _shifted_copies(dbuf, dsh, tt + 24)

        def conv_chunk(r, carry):
            r0 = pl.multiple_of(r * rc, rc)
            rs = pl.ds(r0, rc)
            for g in range(N_HEADS):
                cs = slice(HEAD * g, HEAD * (g + 1))
                sglu = _sigmoid(glu_ref[rs, cs])
                val = val_ref[rs, cs]
                v = val * sglu
                dv = jnp.zeros((rc, HEAD), F32)
                for k in range(CONV_K):
                    ref, off = _tap_ref(dbuf, dsh, CONV_K - 1 - k)
                    d_later = ref[pl.ds(r0 + off, rc), cs]
                    dv = dv + cw_ref[g, k:k + 1, :] * d_later
                    dcw_ref[g, k] += _rowsum8(v * d_later)
                dz_ref[rs, cs] = (dv * sglu).astype(BF16)
                dz_ref[rs, D + HEAD * g:D + HEAD * (g + 1)] = (
                    dv * val * sglu * (1.0 - sglu)).astype(BF16)
            return carry

        lax.fori_loop(0, tt // rc, conv_chunk, 0)
        dbuf[tt:tt + CONV_PAD, :] = dbuf[0:CONV_PAD, :]

        @pl.when(i == n_steps - 1)
        def _():
            pltpu.sync_copy(dw, dw_hbm)
            for g in range(N_HEADS):
                for k in range(CONV_K):
                    dcw_out[g, k:k + 1, :] = jnp.sum(dcw_ref[g, k], axis=0, keepdims=True)

    rev = lambda i: n_steps - 1 - i
    col = lambda j: pl.BlockSpec((tt, D), lambda i: (rev(i), j))
    row = pl.BlockSpec((tt, D), lambda i: (rev(i), 0))
    acc = _full((SUBLANES, D))
    n = len(modes)
    return pl.pallas_call(
        _hosted(body, 11, 7, (n_steps,), modes), name="conv_bwd", grid=(n_steps,),
        out_shape=[jax.ShapeDtypeStruct((t, N_COLS), BF16), jax.ShapeDtypeStruct((D, D), F32),
                   jax.ShapeDtypeStruct((N_DEV, CONV_PAD, HEAD), F32)]
        + [jax.ShapeDtypeStruct((SUBLANES, D), F32)] * 4 + _recv_shapes(grads, modes),
        in_specs=[row, col(0), col(1), col(2), row, row, _full((N_DEV, CONV_PAD, HEAD)),
                  _full((1, D)), _full((1, D)), ANY, ANY] + [ANY] * n,
        out_specs=[pl.BlockSpec((tt, 3 * D), lambda i: (rev(i), 0)), ANY,
                   _full((N_DEV, CONV_PAD, HEAD)), acc, acc, acc, acc] + [ANY] * n,
        input_output_aliases={10: 0},
        scratch_shapes=[pltpu.VMEM((D, D), BF16), pltpu.VMEM((D, D), F32),
                        pltpu.VMEM((tt + CONV_PAD, D), F32),
                        pltpu.VMEM((SUBLANES, tt + CONV_PAD, D), F32),
                        pltpu.VMEM((tt, D), BF16), pltpu.VMEM((tt, D), F32),
                        pltpu.VMEM((N_DEV, CONV_PAD, SUBLANES, HEAD), F32)] + _exchange_scratch(n),
        compiler_params=_params(52, dimension_semantics=("arbitrary",)),
    )(dy, z, z, z, yc, y2, conv_w_all, cn_g, cn_b, w_pw2, dz, *grads)


def _inproj_bwd_dx(dz, x, d_h1, ln_g, w_in_all, grads, modes):
    t = x.shape[0]
    tt = min(256, t)

    def body(dz_ref, x_ref, dh1_ref, g_ref, w_hbm, dx_ref, dg_ref, w_vmem):
        @pl.when(pl.program_id(0) == 0)
        def _():
            for d in range(N_DEV):
                pltpu.sync_copy(w_hbm.at[d], w_vmem.at[
                    d // 2, :, pl.ds(COLS_PER_DEV * (d % 2), COLS_PER_DEV)])
            dg_ref[...] = jnp.zeros_like(dg_ref)

        du = jnp.zeros((tt, D), F32)
        for q in range(N_CHIPS):
            du = du + lax.dot_general(
                dz_ref[:, PAIR_COLS * q:PAIR_COLS * (q + 1)], w_vmem[q],
                (((1,), (1,)), ((), ())), preferred_element_type=F32)
        xv = x_ref[...]
        rstd = lax.rsqrt(jnp.mean(xv * xv, axis=-1, keepdims=True) + EPS)
        xhat = xv * rstd
        dg_ref[...] += _rowsum8(du * xhat)
        dx_ref[...] = dh1_ref[...] + _rms_bwd(du * g_ref[...], xhat, rstd)

    row = pl.BlockSpec((tt, D), lambda i: (i, 0))
    n = len(modes)
    return pl.pallas_call(
        _hosted(body, 5, 2, (t // tt,), modes), name="inproj_bwd_dx", grid=(t // tt,),
        out_shape=[jax.ShapeDtypeStruct((t, D), F32), jax.ShapeDtypeStruct((SUBLANES, D), F32)]
        + _recv_shapes(grads, modes),
        in_specs=[pl.BlockSpec((tt, N_COLS), lambda i: (i, 0)), row, row, _full((1, D)), ANY]
        + [ANY] * n,
        out_specs=[row, _full((SUBLANES, D))] + [ANY] * n,
        scratch_shapes=[pltpu.VMEM((N_CHIPS, D, PAIR_COLS), BF16)] + _exchange_scratch(n),
        compiler_params=_params(48, dimension_semantics=("arbitrary",)),
    )(dz, x, d_h1, ln_g, w_in_all, *grads)


def _inproj_bwd_dw(name, u, dz, first, count, grads=(), modes=()):
    t = u.shape[0]
    tt = min(512, t)
    grid = (count // 2, t // tt)
    n = len(modes)

    def body(u_ref, dz_ref, dw_ref):
        @pl.when(pl.program_id(1) == 0)
        def _():
            dw_ref[...] = jnp.zeros_like(dw_ref)

        both = lax.dot_general(u_ref[...], dz_ref[...], (((0,), (0,)), ((), ())),
                               preferred_element_type=F32)
        dw_ref[0] += both[:, :COLS_PER_DEV]
        dw_ref[1] += both[:, COLS_PER_DEV:]

    return pl.pallas_call(
        _hosted(body, 2, 1, grid, modes) if n else body, name=name, grid=grid,
        out_shape=[jax.ShapeDtypeStruct((count, D, COLS_PER_DEV), F32)]
        + _recv_shapes(grads, modes),
        in_specs=[pl.BlockSpec((tt, D), lambda j, i: (i, 0)),
                  pl.BlockSpec((tt, PAIR_COLS), lambda j, i: (i, first // 2 + j))] + [ANY] * n,
        out_specs=[pl.BlockSpec((2, D, COLS_PER_DEV), lambda j, i: (j, 0, 0))] + [ANY] * n,
        scratch_shapes=_exchange_scratch(n) if n else [],
        compiler_params=_params(40, dimension_semantics=("arbitrary", "arbitrary")),
    )(u, dz, *grads)


def _adamw(w, g, m, v):
    m = ADAM_B1 * m + (1.0 - ADAM_B1) * g
    v = ADAM_B2 * v + (1.0 - ADAM_B2) * (g * g)
    m_hat = m / (1.0 - ADAM_B1 ** ADAM_STEP)
    v_hat = v / (1.0 - ADAM_B2 ** ADAM_STEP)
    delta = -ADAM_LR * (m_hat / (jnp.sqrt(v_hat) + ADAM_EPS) + ADAM_WD * w)
    return delta, m, v


def _pack_small(partials):
    rows = sorted(partials)

    def body(*refs):
        ins, out_ref = refs[:-1], refs[-1]
        out_ref[...] = jnp.zeros_like(out_ref)
        for j, row in enumerate(rows):
            out_ref[row:row + 1, :] = jnp.sum(ins[j][...], axis=0, keepdims=True)

    return pl.pallas_call(
        body, name="pack_small", out_shape=jax.ShapeDtypeStruct((N_SMALL, D), F32),
    )(*[partials[row] for row in rows])


def _sum_adam(name, recvs, w, m, v, rows):
    r, c = w.shape
    n = len(recvs)

    def body(*refs):
        w_ref, m_ref, v_ref, g_ref, d_ref, mo_ref, vo_ref = refs[n:]

        def finish(recv_ref):
            g = recv_ref[0]
            for s in range(1, recv_ref.shape[0]):
                g = g + recv_ref[s]
            g_ref[...] = g
            d_ref[...], mo_ref[...], vo_ref[...] = _adamw(w_ref[...], g, m_ref[...], v_ref[...])

        if n == 1:
            finish(refs[0])
        else:
            for side in range(n):
                pl.when(lax.axis_index("x") == side)(functools.partial(finish, refs[side]))

    blk = pl.BlockSpec((rows, c), lambda i: (i, 0))
    return pl.pallas_call(
        body, name=name, grid=(r // rows,),
        out_shape=[jax.ShapeDtypeStruct((r, c), F32)] * 4,
        in_specs=[pl.BlockSpec((rv.shape[0], rows, c), lambda i: (0, i, 0)) for rv in recvs]
        + [blk, blk, blk],
        out_specs=[blk] * 4,
        compiler_params=_params(48, dimension_semantics=("arbitrary",)),
    )(*recvs, w, m, v)


def _small_adam(gathered, lb_logits, w, m, v):
    def body(ga_ref, lbl_ref, w_ref, m_ref, v_ref, g_ref, d_ref, mo_ref, vo_ref, loss_ref):
        g = ga_ref[0]
        for s in range(1, N_DEV):
            g = g + ga_ref[s]
        s0, s1 = _lower_bound(lbl_ref[...])
        d_lb = g[R_LB0:R_LB0 + 1, :]
        rows = lax.broadcasted_iota(jnp.int32, (N_SMALL, D), 0)
        g = jnp.where(rows == R_LB0, d_lb * s0 * (1.0 - s0), g)
        g = jnp.where(rows == R_LB1, -d_lb * s0 * s1, g)
        g_ref[...] = g
        d_ref[...], mo_ref[...], vo_ref[...] = _adamw(w_ref[...], g, m_ref[...], v_ref[...])
        loss_ref[...] = (0.5 / D) * jnp.sum(g[R_LOSS:R_LOSS + 1, :], axis=-1, keepdims=True)

    return pl.pallas_call(
        body, name="small_adam",
        out_shape=[jax.ShapeDtypeStruct((N_SMALL, D), F32)] * 4 + [jax.ShapeDtypeStruct((1, 1), F32)],
    )(gathered, lb_logits, w, m, v)


def _pad_rows(a, rows):
    return jnp.pad(a, ((0, rows - a.shape[0]), (0, 0)))


def _pack_rows(rows):
    rows = [r.reshape(-1, D) for r in rows]
    packed = jnp.concatenate(rows, axis=0)
    return _pad_rows(packed, N_SMALL)


def kernel(x, p, ln_g, w_in, conv_w, conv_b, cnorm_g, cnorm_b, w_pw2, b_pw2, lb_logits, onorm_g, w_out, pe_norm_g, w_pg, w_pp, final_g, loss_target, m_ln_g, m_w_in, m_conv_w, m_conv_b, m_cnorm_g, m_cnorm_b, m_w_pw2, m_b_pw2, m_lb_logits, m_onorm_g, m_w_out, m_pe_norm_g, m_w_pg, m_w_pp, m_final_g, v_ln_g, v_w_in, v_conv_w, v_conv_b, v_cnorm_g, v_cnorm_b, v_w_pw2, v_b_pw2, v_lb_logits, v_onorm_g, v_w_out, v_pe_norm_g, v_w_pg, v_w_pp, v_final_g):
    t = x.shape[1]
    x2 = x.reshape(t, D)
    p2 = p.reshape(t, PLE)
    tg2 = loss_target.reshape(t, D)
    fin_g = final_g.reshape(1, D)

    z, u, w_in_all, conv_w_all, w_pw2_all = _inproj_fwd(
        x2, ln_g, w_in[0].astype(BF16), [_pad_rows(conv_w[0], CONV_PAD), w_pw2[0].astype(BF16)])
    o_raw, y_hgrn, states, w_out_all, w_pg_all, w_pp_all = _hgrn_fwd(
        z, lb_logits, onorm_g, [w_out[0].astype(BF16), w_pg[0].astype(BF16), w_pp[0].astype(BF16)])
    w_pw2_full = w_pw2_all.reshape(D, D)
    w_out_full = w_out_all.reshape(2 * D, D)
    w_pg_full = w_pg_all.reshape(D, D)
    yc, y2, y_conv = _conv_fwd(z, conv_w_all, conv_b, cnorm_g, cnorm_b, w_pw2_full, b_pw2)

    (d_h1, dy_conv, dy_hgrn, d_w_out, d_w_pg, d_w_pp, d_pen_p, d_fin_p, loss_p) = _tail(
        x2, y_conv, y_hgrn, p2, tg2, w_out_full, w_pg_full, w_pp_all, pe_norm_g, fin_g)

    dz, d_on_p, d_lb_p, r_w_out, r_w_pg, r_w_pp = _hgrn_bwd(
        dy_hgrn, z, o_raw, states, lb_logits, onorm_g,
        [d_w_out.reshape(N_DEV, 2 * D // N_DEV, D), d_w_pg.reshape(N_DEV, D // N_DEV, D), d_w_pp])
    (d_w_in_hi,) = _inproj_bwd_dw("inproj_bwd_dw_hi", u, dz, N_DEV // 2, N_DEV // 2)
    dz, d_w_pw2, d_conv_w, d_b2_p, d_cng_p, d_cnb_p, d_cb_p, r_w_in_hi = _conv_bwd(
        dy_conv, z, yc, y2, conv_w_all, cnorm_g, cnorm_b, w_pw2_full, dz, [d_w_in_hi], [1])
    d_w_in_lo, r_w_pw2, r_conv_w = _inproj_bwd_dw(
        "inproj_bwd_dw_lo", u, dz, 0, N_DEV // 2,
        [d_w_pw2.reshape(N_DEV, D // N_DEV, D), d_conv_w], ["scatter", "scatter"])
    chip_lo = _pair_reduce("pair_reduce_lo", d_w_in_lo)
    grad_x, d_ln_p, r_w_in_lo = _inproj_bwd_dx(
        dz, x2, d_h1, ln_g, w_in_all, [chip_lo], [("chip", 0)])

    small = _pack_small({R_LN: d_ln_p, R_CONVB: d_cb_p, R_CNG: d_cng_p, R_CNB: d_cnb_p,
                         R_BPW2: d_b2_p, R_LB0: d_lb_p, R_ON: d_on_p, R_PEN: d_pen_p,
                         R_FIN: d_fin_p, R_LOSS: loss_p})
    (small_all,) = _exchange_call("gather_small", [small], ["gather"])

    big = {}
    big["w_in"] = _sum_adam("adam_w_in", [r_w_in_lo, r_w_in_hi], w_in[0], m_w_in[0], v_w_in[0], 128)
    cw = _sum_adam("adam_conv_w", [r_conv_w], _pad_rows(conv_w[0], CONV_PAD),
                   _pad_rows(m_conv_w[0], CONV_PAD), _pad_rows(v_conv_w[0], CONV_PAD), CONV_PAD)
    big["conv_w"] = [a[:CONV_K] for a in cw]
    big["w_pw2"] = _sum_adam("adam_w_pw2", [r_w_pw2], w_pw2[0], m_w_pw2[0], v_w_pw2[0], 128)
    big["w_out"] = _sum_adam("adam_w_out", [r_w_out], w_out[0], m_w_out[0], v_w_out[0], 128)
    big["w_pg"] = _sum_adam("adam_w_pg", [r_w_pg], w_pg[0], m_w_pg[0], v_w_pg[0], 128)
    big["w_pp"] = _sum_adam("adam_w_pp", [r_w_pp], w_pp[0], m_w_pp[0], v_w_pp[0], PLE)

    small_w = [ln_g, conv_b, cnorm_g, cnorm_b, b_pw2, lb_logits, onorm_g, pe_norm_g, final_g]
    small_m = [m_ln_g, m_conv_b, m_cnorm_g, m_cnorm_b, m_b_pw2, m_lb_logits, m_onorm_g,
               m_pe_norm_g, m_final_g]
    small_v = [v_ln_g, v_conv_b, v_cnorm_g, v_cnorm_b, v_b_pw2, v_lb_logits, v_onorm_g,
               v_pe_norm_g, v_final_g]
    sg, sd, sm, sv, loss = _small_adam(small_all, lb_logits, _pack_rows(small_w),
                                       _pack_rows(small_m), _pack_rows(small_v))

    small_rows = {"ln_g": (R_LN, 1), "conv_b": (R_CONVB, 1), "cnorm_g": (R_CNG, 1),
                  "cnorm_b": (R_CNB, 1), "b_pw2": (R_BPW2, 1), "lb_logits": (R_LB0, 2),
                  "onorm_g": (R_ON, 1), "pe_norm_g": (R_PEN, 1), "final_g": (R_FIN, 1)}
    order = ["ln_g", "w_in", "conv_w", "conv_b", "cnorm_g", "cnorm_b", "w_pw2", "b_pw2",
             "lb_logits", "onorm_g", "w_out", "pe_norm_g", "w_pg", "w_pp", "final_g"]

    def leaf(kind, name):
        if name in big:
            return big[name][kind][None]
        r0, n = small_rows[name]
        a = (sg, sd, sm, sv)[kind][r0:r0 + n]
        return a.reshape(D) if name == "final_g" else a

    outs = [loss.reshape(()), grad_x.reshape(1, t, D)]
    for kind in range(4):
        outs += [leaf(kind, name) for name in order]
    return tuple(outs)
```

```python
import functools

import jax
import jax.numpy as jnp
from jax import lax
from jax.experimental import pallas as pl
from jax.experimental.pallas import tpu as pltpu

F32 = jnp.float32
BF16 = jnp.bfloat16
MESH = pl.DeviceIdType.MESH

N_DEV = 8
D = 1024
N_COLS = 7 * D
COLS_PER_DEV = N_COLS // N_DEV
PLE = 256
HEAD = 128
N_HEADS = D // HEAD
CONV_K = 31
CONV_PAD = 32
CHUNK = 64
EPS = 1e-6
SUBLANES = 8

ADAM_LR = 0.001
ADAM_B1 = 0.9
ADAM_B2 = 0.999
ADAM_EPS = 1e-08
ADAM_WD = 0.01
ADAM_STEP = 10

MIB = 1024 * 1024
N_SMALL = 16
R_LN, R_CONVB, R_CNG, R_CNB, R_BPW2, R_LB0, R_LB1, R_ON, R_PEN, R_FIN, R_LOSS = range(11)


def _params(vmem_mib, **kw):
    return pltpu.CompilerParams(vmem_limit_bytes=vmem_mib * MIB, **kw)


def _dot(a, b):
    return jnp.dot(a.astype(BF16), b.astype(BF16), preferred_element_type=F32)


def _dot_nt(a, b):
    return lax.dot_general(a.astype(BF16), b.astype(BF16), (((1,), (1,)), ((), ())),
                           preferred_element_type=F32)


def _dot_tn(a, b):
    return lax.dot_general(a.astype(BF16), b.astype(BF16), (((0,), (0,)), ((), ())),
                           preferred_element_type=F32)


def _split(a):
    hi = a.astype(BF16)
    return hi, (a - hi.astype(F32)).astype(BF16)


def _dot_split(a, b, dims):
    ah, al = _split(a)
    bh, bl = _split(b)
    dg = lambda p, q: lax.dot_general(p, q, dims, preferred_element_type=F32)
    return dg(ah, bh) + (dg(ah, bl) + dg(al, bh))


def _sigmoid(x):
    return 1.0 / (1.0 + jnp.exp(-x))


def _rowsum8(a):
    r, c = a.shape
    return jnp.sum(a.reshape(r // SUBLANES, SUBLANES, c), axis=0)


def _tri_dot(tri, a):
    hi = a.astype(BF16)
    r1 = a - hi.astype(F32)
    mid = r1.astype(BF16)
    lo = (r1 - mid.astype(F32)).astype(BF16)
    return (jnp.dot(tri, hi, preferred_element_type=F32)
            + jnp.dot(tri, mid, preferred_element_type=F32)
            + jnp.dot(tri, lo, preferred_element_type=F32))


def _lower_bound(lbl):
    l0, l1 = lbl[0:1, :], lbl[1:2, :]
    m = jnp.maximum(l0, l1)
    e0, e1 = jnp.exp(l0 - m), jnp.exp(l1 - m)
    s = e0 + e1
    return e0 / s, e1 / s


ANY = pl.BlockSpec(memory_space=pl.ANY)


def _full(shape):
    return pl.BlockSpec(shape, lambda i: (0,) * len(shape))


def _peer(x, y, c, k):
    px = 1 - x if k & 4 else x
    py = 1 - y if k & 2 else y
    pc = 1 - c if k & 1 else c
    return (px, py, pc), 4 * px + 2 * py + pc


class _Exchange:
    def __init__(self, srcs, outs, modes, send_sems, recv_sems, local_sems):
        x, y, c = lax.axis_index("x"), lax.axis_index("y"), lax.axis_index("c")
        me = 4 * x + 2 * y + c
        self.starts, self.send_waits, self.recv_waits = [], [], []

        def remote(a, k, src, slot, peer, when):
            sem = a * N_DEV + k
            cp = pltpu.make_async_remote_copy(
                src_ref=src, dst_ref=outs[a].at[slot], send_sem=send_sems.at[sem],
                recv_sem=recv_sems.at[sem], device_id=peer, device_id_type=MESH)
            self.starts.append((when, cp.start))
            self.send_waits.append((when, cp.wait_send))

        def arrival(a, k, slot, when):
            sem = a * N_DEV + k
            cp = pltpu.make_async_remote_copy(
                src_ref=outs[a].at[slot], dst_ref=outs[a].at[slot], send_sem=send_sems.at[sem],
                recv_sem=recv_sems.at[sem], device_id=(x, y, c), device_id_type=MESH)
            self.recv_waits.append((when, cp.wait_recv))

        def local(a, src, slot, when):
            cp = pltpu.make_async_copy(src, outs[a].at[slot], local_sems.at[a])
            self.starts.append((when, cp.start))
            self.send_waits.append((when, cp.wait))

        for a, (src, mode) in enumerate(zip(srcs, modes)):
            if mode in ("gather", "scatter"):
                local(a, src if mode == "gather" else src.at[me], me, None)
                for k in range(1, N_DEV):
                    peer, peer_idx = _peer(x, y, c, k)
                    remote(a, k, src if mode == "gather" else src.at[peer_idx], me, peer, None)
                    arrival(a, k, peer_idx, None)
                continue
            if isinstance(mode, tuple):
                here, away = x == mode[1], x != mode[1]
                chip = 2 * x + y
                local(a, src.at[y], chip, here)
                remote(a, 1, src.at[1 - y], chip, (x, 1 - y, c), here)
                remote(a, 2, src.at[y], chip, (1 - x, y, c), away)
                remote(a, 3, src.at[1 - y], chip, (1 - x, 1 - y, c), away)
                arrival(a, 1, 2 * x + 1 - y, here)
                arrival(a, 2, 2 * (1 - x) + y, here)
                arrival(a, 3, 2 * (1 - x) + 1 - y, here)
                continue
            here, away = x == mode, x != mode
            for kk in range(4):
                py = 1 - y if kk & 2 else y
                pc = 1 - c if kk & 1 else c
                block = src.at[2 * py + pc]
                if kk == 0:
                    local(a, block, me, here)
                else:
                    remote(a, kk, block, me, (x, py, pc), here)
                remote(a, 4 + kk, block, me, (1 - x, py, pc), away)
            for k in range(1, N_DEV):
                arrival(a, k, _peer(x, y, c, k)[1], here)

    @staticmethod
    def _run(actions):
        for when, fn in actions:
            if when is None:
                fn()
            else:
                pl.when(when)(fn)

    def start(self):
        self._run(self.starts)

    def wait(self):
        self._run(self.recv_waits)
        self._run(self.send_waits)


def _exchange_scratch(n):
    return [pltpu.SemaphoreType.DMA((n * N_DEV,)), pltpu.SemaphoreType.DMA((n * N_DEV,)),
            pltpu.SemaphoreType.DMA((n,))]


def _recv_shapes(srcs, modes):
    def shape(s, m):
        if m == "gather":
            return (N_DEV,) + s.shape
        return (N_DEV // 2 if isinstance(m, tuple) else N_DEV,) + s.shape[1:]

    return [jax.ShapeDtypeStruct(shape(s, m), s.dtype) for s, m in zip(srcs, modes)]


def _pair_reduce(name, blocks):
    shape = (2,) + blocks.shape[1:]

    def body(src, out_ref, stage, mine, send_sems, recv_sems, local_sems):
        x, y, c = lax.axis_index("x"), lax.axis_index("y"), lax.axis_index("c")
        sends, waits = [], []
        for py in range(2):
            sends.append(pltpu.make_async_remote_copy(
                src_ref=src.at[2 * py + 1 - c], dst_ref=stage.at[py], send_sem=send_sems.at[py],
                recv_sem=recv_sems.at[py], device_id=(x, y, 1 - c), device_id_type=MESH))
            waits.append(pltpu.make_async_copy(src.at[2 * py + c], mine.at[py], local_sems.at[py]))
        for cp in sends + waits:
            cp.start()
        for cp in waits:
            cp.wait()
        for cp in sends:
            cp.wait_recv()
        out_ref[...] = mine[...] + stage[...]
        for cp in sends:
            cp.wait_send()

    return pl.pallas_call(
        body, name=name, out_shape=jax.ShapeDtypeStruct(shape, F32), in_specs=[ANY],
        scratch_shapes=[pltpu.VMEM(shape, F32), pltpu.VMEM(shape, F32),
                        pltpu.SemaphoreType.DMA((2,)), pltpu.SemaphoreType.DMA((2,)),
                        pltpu.SemaphoreType.DMA((2,))],
        compiler_params=_params(40),
    )(blocks)


def _exchange_call(name, srcs, modes):
    n = len(srcs)

    def body(*refs):
        xch = _Exchange(refs[:n], refs[n:2 * n], modes, *refs[2 * n:])
        xch.start()
        xch.wait()

    return pl.pallas_call(
        body, name=name, out_shape=_recv_shapes(srcs, modes),
        in_specs=[ANY] * n, out_specs=[ANY] * n, scratch_shapes=_exchange_scratch(n),
    )(*srcs)


def _hosted(body, n_in, n_out, grid, modes):
    n = len(modes)

    def hosted(*refs):
        ins, srcs = refs[:n_in], refs[n_in:n_in + n]
        outs = refs[n_in + n:n_in + n + n_out]
        bufs = refs[n_in + n + n_out:n_in + 2 * n + n_out]
        scratch = refs[n_in + 2 * n + n_out:-3]
        xch = _Exchange(srcs, bufs, modes, *refs[-3:])
        first, last = True, True
        for axis, size in enumerate(grid):
            first = jnp.logical_and(first, pl.program_id(axis) == 0)
            last = jnp.logical_and(last, pl.program_id(axis) == size - 1)
        pl.when(first)(xch.start)
        body(*ins, *outs, *scratch)
        pl.when(last)(xch.wait)

    return hosted


N_CHIPS = N_DEV // 2
PAIR_COLS = 2 * COLS_PER_DEV
PUSHED = (1, 2, 4, 6)
FORWARDED = (2, 4, 6)


def _inproj_fwd(x, ln_g, w_shard, other_shards):
    t = x.shape[0]
    tt = min(512, t)
    n_t = t // tt
    modes = ["gather"] * len(other_shards)
    n = len(modes)
    chip = 2 * lax.axis_index("x") + lax.axis_index("y")
    order = jnp.bitwise_xor(chip, jnp.arange(N_CHIPS, dtype=jnp.int32)).astype(jnp.int32)

    def body(order_ref, x_ref, g_ref, shard_hbm, *refs):
        srcs = refs[:n]
        z_ref, u_ref, w_all = refs[n:n + 3]
        bufs = refs[n + 3:2 * n + 3]
        u_all, w_blk, w_send, w_recv, w_local = refs[2 * n + 3:2 * n + 8]
        p, i = pl.program_id(0), pl.program_id(1)
        x, y, c = lax.axis_index("x"), lax.axis_index("y"), lax.axis_index("c")
        mine = 4 * x + 2 * y + c
        others = _Exchange(srcs, bufs, modes, *refs[2 * n + 8:])

        def push(k):
            peer, _ = _peer(x, y, c, k)
            return pltpu.make_async_remote_copy(
                src_ref=shard_hbm, dst_ref=w_all.at[mine], send_sem=w_send.at[k],
                recv_sem=w_recv.at[k], device_id=peer, device_id_type=MESH)

        def forward(k):
            _, owner = _peer(x, y, c, k)
            return pltpu.make_async_remote_copy(
                src_ref=w_all.at[owner], dst_ref=w_all.at[owner], send_sem=w_send.at[k + 1],
                recv_sem=w_recv.at[k + 1], device_id=(x, y, 1 - c), device_id_type=MESH)

        def landed(k):
            _, owner = _peer(x, y, c, k)
            return pltpu.make_async_remote_copy(
                src_ref=w_all.at[owner], dst_ref=w_all.at[owner], send_sem=w_send.at[k],
                recv_sem=w_recv.at[k], device_id=(x, y, c), device_id_type=MESH)

        keep = pltpu.make_async_copy(shard_hbm, w_all.at[mine], w_local.at[0])

        def load_pair(step):
            same = shard_hbm if step == 0 else w_all.at[_peer(x, y, c, 2 * step)[1]]
            other = w_all.at[_peer(x, y, c, 2 * step + 1)[1]]
            for side in range(2):
                @pl.when(c == side)
                def _(side=side):
                    pltpu.sync_copy(same, w_blk.at[:, pl.ds(COLS_PER_DEV * side, COLS_PER_DEV)])
                    pltpu.sync_copy(
                        other, w_blk.at[:, pl.ds(COLS_PER_DEV * (1 - side), COLS_PER_DEV)])

        @pl.when(jnp.logical_and(p == 0, i == 0))
        def _():
            for k in PUSHED:
                push(k).start()
            keep.start()
            others.start()

        for step in range(N_CHIPS):
            @pl.when(jnp.logical_and(p == step, i == 0))
            def _(step=step):
                landed(2 * step + 1).wait_recv()
                load_pair(step)

        rows = pl.ds(pl.multiple_of(i * tt, tt), tt)

        @pl.when(p == 0)
        def _():
            xv = x_ref[...]
            rstd = lax.rsqrt(jnp.mean(xv * xv, axis=-1, keepdims=True) + EPS)
            ub = (xv * rstd * g_ref[...]).astype(BF16)
            u_ref[...] = ub
            u_all[rows, :] = ub

        z_ref[...] = jnp.dot(u_all[rows, :], w_blk[...], preferred_element_type=F32)

        for step in range(1, N_CHIPS):
            @pl.when(jnp.logical_and(p == step - 1, i == n_t - 1))
            def _(step=step):
                landed(2 * step).wait_recv()
                forward(2 * step).start()

        @pl.when(jnp.logical_and(p == N_CHIPS - 1, i == n_t - 1))
        def _():
            for k in PUSHED:
                push(k).wait_send()
            for k in FORWARDED:
                forward(k).wait_send()
            keep.wait()
            others.wait()

    first_pass = lambda p, i, order_ref: (jnp.where(p == 0, i, n_t - 1), 0)
    grid_spec = pltpu.PrefetchScalarGridSpec(
        num_scalar_prefetch=1, grid=(N_CHIPS, n_t),
        in_specs=[pl.BlockSpec((tt, D), first_pass),
                  pl.BlockSpec((1, D), lambda p, i, order_ref: (0, 0)), ANY] + [ANY] * n,
        out_specs=[pl.BlockSpec((tt, PAIR_COLS), lambda p, i, order_ref: (i, order_ref[p])),
                   pl.BlockSpec((tt, D), first_pass), ANY] + [ANY] * n,
        scratch_shapes=[pltpu.VMEM((t, D), BF16), pltpu.VMEM((D, PAIR_COLS), BF16),
                        pltpu.SemaphoreType.DMA((N_DEV,)), pltpu.SemaphoreType.DMA((N_DEV,)),
                        pltpu.SemaphoreType.DMA((1,))] + _exchange_scratch(n))
    return pl.pallas_call(
        body, name="inproj_fwd", grid_spec=grid_spec,
        out_shape=[jax.ShapeDtypeStruct((t, N_COLS), F32), jax.ShapeDtypeStruct((t, D), BF16),
                   jax.ShapeDtypeStruct((N_DEV,) + w_shard.shape, BF16)]
        + _recv_shapes(other_shards, modes),
        compiler_params=_params(48, dimension_semantics=("arbitrary", "arbitrary")),
    )(order, x, ln_g, w_shard, *other_shards)


def _shifted_copies(buf, shifted, rows):
    for b in range(1, SUBLANES):
        shifted[b, 0:rows, :] = buf[b:b + rows, :]


def _tap_ref(buf, shifted, offset):
    a, b = divmod(offset, SUBLANES)
    return (buf if b == 0 else shifted.at[b]), SUBLANES * a


def _group_norm_stats(blk):
    mu = jnp.mean(blk, axis=-1, keepdims=True)
    cen = blk - mu
    var = jnp.mean(cen * cen, axis=-1, keepdims=True)
    return cen * lax.rsqrt(var + EPS)


def _conv_fwd(z, conv_w_all, conv_b, cn_g, cn_b, w_pw2, b_pw2):
    t = z.shape[0]
    tt = min(256, t)
    rc = 32

    def body(val_ref, glu_ref, gate_ref, cw_ref, cb_ref, g_ref, b_ref, w_hbm, b2_ref,
             yc_ref, y2_ref, yo_ref, w_vmem, vbuf, vsh, y1buf):
        @pl.when(pl.program_id(0) == 0)
        def _():
            pltpu.sync_copy(w_hbm, w_vmem)
            vbuf[0:CONV_PAD, :] = jnp.zeros((CONV_PAD, D), F32)

        vbuf[CONV_PAD:CONV_PAD + tt, :] = val_ref[...] * _sigmoid(glu_ref[...])
        _shifted_copies(vbuf, vsh, tt + 24)

        def row_chunk(r, carry):
            r0 = pl.multiple_of(r * rc, rc)
            for g in range(N_HEADS):
                cs = slice(HEAD * g, HEAD * (g + 1))
                acc = jnp.zeros((rc, HEAD), F32)
                for k in range(CONV_K):
                    ref, off = _tap_ref(vbuf, vsh, k + 2)
                    acc = acc + cw_ref[g, k:k + 1, :] * ref[pl.ds(r0 + off, rc), cs]
                acc = acc + cb_ref[:, cs]
                yc_ref[pl.ds(r0, rc), cs] = acc
                n = _group_norm_stats(acc) * g_ref[:, cs] + b_ref[:, cs]
                y1buf[pl.ds(r0, rc), cs] = (n * _sigmoid(n)).astype(BF16)
            return carry

        lax.fori_loop(0, tt // rc, row_chunk, 0)
        vbuf[0:CONV_PAD, :] = vbuf[tt:tt + CONV_PAD, :]
        y2 = jnp.dot(y1buf[...], w_vmem[...], preferred_element_type=F32) + b2_ref[...]
        y2_ref[...] = y2
        gate = gate_ref[...]
        yo_ref[...] = (y2 * gate * _sigmoid(gate)).astype(BF16)

    col = lambda j: pl.BlockSpec((tt, D), lambda i: (i, j))
    row = pl.BlockSpec((tt, D), lambda i: (i, 0))
    return pl.pallas_call(
        body, name="conv_fwd", grid=(t // tt,),
        out_shape=[jax.ShapeDtypeStruct((t, D), F32), jax.ShapeDtypeStruct((t, D), F32),
                   jax.ShapeDtypeStruct((t, D), BF16)],
        in_specs=[col(0), col(1), col(2), _full((N_DEV, CONV_PAD, HEAD)), _full((1, D)),
                  _full((1, D)), _full((1, D)), ANY, _full((1, D))],
        out_specs=[row, row, row],
        scratch_shapes=[pltpu.VMEM((D, D), BF16), pltpu.VMEM((tt + CONV_PAD, D), F32),
                        pltpu.VMEM((SUBLANES, tt + CONV_PAD, D), F32), pltpu.VMEM((tt, D), BF16)],
        compiler_params=_params(48, dimension_semantics=("arbitrary",)),
    )(z, z, z, conv_w_all, conv_b, cn_g, cn_b, w_pw2, b_pw2)


HEAD_GROUP = 8
_HEAD_LANES = [slice(HEAD * j, HEAD * (j + 1)) for j in range(HEAD_GROUP)]


def _head_mean(a):
    return jnp.concatenate(
        [jnp.broadcast_to(jnp.mean(a[:, hs], axis=-1, keepdims=True), (a.shape[0], HEAD))
         for hs in _HEAD_LANES], axis=1)


def _chunk_quantities(zq, zf, lbh, tri):
    sig = _sigmoid(zf)
    sig_neg = _sigmoid(-zf)
    f = lbh + (1.0 - lbh) * sig
    k = (1.0 - lbh) * sig_neg
    q = zq * _sigmoid(zq)
    b = _tri_dot(tri, jnp.log(f))
    b_mid = b[CHUNK // 2 - 1:CHUNK // 2, :]
    b_last = b[CHUNK - 1:CHUNK, :]
    e_q = jnp.exp(b)
    e_qm = jnp.exp(b - b_mid)
    e_km = jnp.exp(b_mid - b)
    e_kd = jnp.exp(b_last - b)
    return q, k, f, sig, sig_neg, e_q, e_qm, e_km, e_kd, jnp.exp(b_last)


def _hgrn_fwd(z, lb_logits, onorm_g, shards):
    t = z.shape[0]
    tt = min(256, t)
    nc = tt // CHUNK
    modes = ["gather"] * len(shards)
    n = len(modes)

    def body(q_ref, f_ref, i_ref, g_ref, lbl_ref, on_ref, o_ref, y_ref, s_ref, st):
        @pl.when(pl.program_id(0) == 0)
        def _():
            st[...] = jnp.zeros_like(st)

        lb, _ = _lower_bound(lbl_ref[...])
        rows = lax.broadcasted_iota(jnp.int32, (CHUNK, CHUNK), 0)
        cols = lax.broadcasted_iota(jnp.int32, (CHUNK, CHUNK), 1)
        causal = rows >= cols
        tri = causal.astype(BF16)

        def chunk(c, carry):
            r0 = pl.multiple_of(c * CHUNK, CHUNK)
            rs = pl.ds(r0, CHUNK)
            for h0 in range(0, N_HEADS, HEAD_GROUP):
                cs = slice(HEAD * h0, HEAD * (h0 + HEAD_GROUP))
                q, k, _, _, _, e_q, e_qm, e_km, e_kd, e_last = _chunk_quantities(
                    q_ref[rs, cs], f_ref[rs, cs], lb[:, cs], tri)
                v = i_ref[rs, cs].astype(BF16)
                qm, km = (q * e_qm).astype(BF16), (k * e_km).astype(BF16)
                qt, kd = (q * e_q).astype(BF16), (k * e_kd).astype(BF16)
                outs = []
                for j, hs in enumerate(_HEAD_LANES):
                    s_old = st[h0 + j]
                    s_ref[c, h0 + j] = s_old
                    a = jnp.where(causal, _dot_nt(qm[:, hs], km[:, hs]), 0.0)
                    outs.append(_dot_nt(qt[:, hs], s_old) + _dot(a, v[:, hs]))
                    st[h0 + j] = s_old * e_last[:, hs] + _dot_tn(v[:, hs], kd[:, hs])
                o = jnp.concatenate(outs, axis=1)
                o_ref[rs, cs] = o
                n = o * lax.rsqrt(_head_mean(o * o) + EPS)
                zg = g_ref[rs, cs]
                y_ref[rs, cs] = (n * on_ref[:, cs] * zg * _sigmoid(zg)).astype(BF16)
            return carry

        lax.fori_loop(0, nc, chunk, 0, unroll=2)

    col = lambda j: pl.BlockSpec((tt, D), lambda i: (i, j))
    row = pl.BlockSpec((tt, D), lambda i: (i, 0))
    return pl.pallas_call(
        _hosted(body, 6, 3, (t // tt,), modes), name="hgrn_fwd", grid=(t // tt,),
        out_shape=[jax.ShapeDtypeStruct((t, D), F32), jax.ShapeDtypeStruct((t, D), BF16),
                   jax.ShapeDtypeStruct((t // CHUNK, N_HEADS, HEAD, HEAD), F32)]
        + _recv_shapes(shards, modes),
        in_specs=[col(3), col(4), col(5), col(6), _full((2, D)), _full((1, D))] + [ANY] * n,
        out_specs=[row, row, pl.BlockSpec((nc, N_HEADS, HEAD, HEAD), lambda i: (i, 0, 0, 0))]
        + [ANY] * n,
        scratch_shapes=[pltpu.VMEM((N_HEADS, HEAD, HEAD), F32)] + _exchange_scratch(n),
        compiler_params=_params(40, dimension_semantics=("arbitrary",)),
    )(z, z, z, z, lb_logits, onorm_g, *shards)


def _rms_bwd(dn, xhat, rstd):
    return rstd * (dn - xhat * jnp.mean(dn * xhat, axis=-1, keepdims=True))


def _tail(x, y_conv, y_hgrn, p, target, w_out, w_pg, w_pp_all, pe_g, fin_g):
    t = x.shape[0]
    tt = min(256, t)
    n_steps = t // tt

    def body(x_ref, yc_ref, yh_ref, p_ref, tg_ref, wo_hbm, wg_hbm, wp_hbm, pg_ref, fg_ref,
             dh1_ref, dyc_ref, dyh_ref, dwo_hbm, dwg_hbm, dwp_hbm, dpg_ref, dfg_ref, loss_ref,
             wo, wg, wp, dwo, dwg, dwp):
        i = pl.program_id(0)

        @pl.when(i == 0)
        def _():
            pltpu.sync_copy(wo_hbm, wo)
            pltpu.sync_copy(wg_hbm, wg)
            for d in range(N_DEV):
                pltpu.sync_copy(wp_hbm.at[d], wp.at[:, pl.ds(HEAD * d, HEAD)])
            dwo[...] = jnp.zeros_like(dwo)
            dwg[...] = jnp.zeros_like(dwg)
            dwp[...] = jnp.zeros_like(dwp)
            dpg_ref[...] = jnp.zeros_like(dpg_ref)
            dfg_ref[...] = jnp.zeros_like(dfg_ref)
            loss_ref[...] = jnp.zeros_like(loss_ref)

        ycv, yhv = yc_ref[...], yh_ref[...]
        h1 = (x_ref[...] + jnp.dot(ycv, wo[0:D, :], preferred_element_type=F32)
              + jnp.dot(yhv, wo[D:2 * D, :], preferred_element_type=F32))
        pb = p_ref[...].astype(BF16)
        pe = jnp.dot(pb, wp[...], preferred_element_type=F32)
        rstd1 = lax.rsqrt(jnp.mean(h1 * h1, axis=-1, keepdims=True) + EPS)
        n1 = h1 * rstd1
        rb = (n1 * pg_ref[...]).astype(BF16)
        gate = _sigmoid(jnp.dot(rb, wg[...], preferred_element_type=F32))
        h2 = h1 + gate * pe
        rstd2 = lax.rsqrt(jnp.mean(h2 * h2, axis=-1, keepdims=True) + EPS)
        n2 = h2 * rstd2
        err = n2 * fg_ref[...] - tg_ref[...]
        loss_ref[...] += _rowsum8(err * err)

        d_out = err * (1.0 / D)
        dfg_ref[...] += _rowsum8(d_out * n2)
        d_h2 = _rms_bwd(d_out * fg_ref[...], n2, rstd2)
        d_pe = (d_h2 * gate).astype(BF16)
        d_gpre = (d_h2 * pe * gate * (1.0 - gate)).astype(BF16)
        dwg[...] += _dot_tn(rb, d_gpre)
        dwp[...] += _dot_tn(pb, d_pe)
        dr = _dot_nt(d_gpre, wg[...])
        dpg_ref[...] += _rowsum8(dr * n1)
        d_h1 = d_h2 + _rms_bwd(dr * pg_ref[...], n1, rstd1)
        dh1_ref[...] = d_h1
        d_h1b = d_h1.astype(BF16)
        dwo[0:D, :] += _dot_tn(ycv, d_h1b)
        dwo[D:2 * D, :] += _dot_tn(yhv, d_h1b)
        dyc_ref[...] = _dot_nt(d_h1b, wo[0:D, :])
        dyh_ref[...] = _dot_nt(d_h1b, wo[D:2 * D, :])

        @pl.when(i == n_steps - 1)
        def _():
            pltpu.sync_copy(dwo, dwo_hbm)
            pltpu.sync_copy(dwg, dwg_hbm)
            for d in range(N_DEV):
                pltpu.sync_copy(dwp.at[:, pl.ds(HEAD * d, HEAD)], dwp_hbm.at[d])

    row = pl.BlockSpec((tt, D), lambda i: (i, 0))
    acc = _full((SUBLANES, D))
    return pl.pallas_call(
        body, name="tail_fwd_bwd", grid=(n_steps,),
        out_shape=[jax.ShapeDtypeStruct((t, D), F32)] * 3
        + [jax.ShapeDtypeStruct((2 * D, D), F32), jax.ShapeDtypeStruct((D, D), F32),
           jax.ShapeDtypeStruct((N_DEV, PLE, HEAD), F32)]
        + [jax.ShapeDtypeStruct((SUBLANES, D), F32)] * 3,
        in_specs=[row, row, row, pl.BlockSpec((tt, PLE), lambda i: (i, 0)), row,
                  ANY, ANY, ANY, _full((1, D)), _full((1, D))],
        out_specs=[row, row, row, ANY, ANY, ANY, acc, acc, acc],
        scratch_shapes=[pltpu.VMEM((2 * D, D), BF16), pltpu.VMEM((D, D), BF16),
                        pltpu.VMEM((PLE, D), BF16), pltpu.VMEM((2 * D, D), F32),
                        pltpu.VMEM((D, D), F32), pltpu.VMEM((PLE, D), F32)],
        compiler_params=_params(52, dimension_semantics=("arbitrary",)),
    )(x, y_conv, y_hgrn, p, target, w_out, w_pg, w_pp_all, pe_g, fin_g)


def _hgrn_bwd(dy, z, o_raw, states, lb_logits, onorm_g, grads):
    t = z.shape[0]
    tt = min(256, t)
    nc = tt // CHUNK
    n_steps = t // tt
    modes = ["scatter"] * len(grads)

    def body(dy_ref, q_ref, f_ref, i_ref, g_ref, o_ref, s_ref, lbl_ref, on_ref,
             dz_ref, don_ref, dlb_ref, dst):
        @pl.when(pl.program_id(0) == 0)
        def _():
            dst[...] = jnp.zeros_like(dst)
            don_ref[...] = jnp.zeros_like(don_ref)
            dlb_ref[...] = jnp.zeros_like(dlb_ref)

        lb, _ = _lower_bound(lbl_ref[...])
        rows = lax.broadcasted_iota(jnp.int32, (CHUNK, CHUNK), 0)
        cols = lax.broadcasted_iota(jnp.int32, (CHUNK, CHUNK), 1)
        causal = rows >= cols
        tri = causal.astype(BF16)
        tri_rev = (rows <= cols).astype(BF16)
        width = HEAD * HEAD_GROUP
        is_last = lax.broadcasted_iota(jnp.int32, (CHUNK, width), 0) == CHUNK - 1
        nn = (((1,), (0,)), ((), ()))
        tn = (((0,), (0,)), ((), ()))
        dg = functools.partial(lax.dot_general, preferred_element_type=F32)

        def chunk(cc, carry):
            c = nc - 1 - cc
            r0 = pl.multiple_of(c * CHUNK, CHUNK)
            rs = pl.ds(r0, CHUNK)
            for h0 in range(0, N_HEADS, HEAD_GROUP):
                cs = slice(HEAD * h0, HEAD * h0 + width)
                zq, zf, zg = q_ref[rs, cs], f_ref[rs, cs], g_ref[rs, cs]
                lbh = lb[:, cs]
                q, k, f, sig, sig_neg, e_q, e_qm, e_km, e_kd, e_last = _chunk_quantities(
                    zq, zf, lbh, tri)
                vb = i_ref[rs, cs].astype(BF16)
                qt, qm, km, kd = q * e_q, q * e_qm, k * e_km, k * e_kd
                qt_b, kd_b = qt.astype(BF16), kd.astype(BF16)
                qm_h, qm_l = _split(qm)
                km_h, km_l = _split(km)

                o = o_ref[rs, cs]
                rstd = lax.rsqrt(_head_mean(o * o) + EPS)
                n = o * rstd
                sg = _sigmoid(zg)
                dyv = dy_ref[rs, cs]
                on = on_ref[:, cs]
                d_zg = dyv * n * on * sg * (1.0 + zg * (1.0 - sg))
                d_on = dyv * zg * sg
                don_ref[:, cs] += _rowsum8(d_on * n)
                dn = d_on * on
                do_b = (rstd * (dn - n * _head_mean(dn * n))).astype(BF16)

                dv, dkd, dqt, dqm, dkm, s_dots = [], [], [], [], [], []
                for j, hs in enumerate(_HEAD_LANES):
                    s_old, ds_new = s_ref[c, h0 + j], dst[h0 + j]
                    ds_b = ds_new.astype(BF16)
                    a = jnp.where(causal, _dot_nt(qm_h[:, hs], km_h[:, hs]), 0.0)
                    da = jnp.where(causal, _dot_nt(do_b[:, hs], vb[:, hs]), 0.0)
                    dv.append(_dot_tn(a, do_b[:, hs]) + _dot_nt(kd_b[:, hs], ds_b))
                    dkd.append(_dot(vb[:, hs], ds_b))
                    dqt.append(_dot(do_b[:, hs], s_old))
                    da_h, da_l = _split(da)
                    dqm.append(dg(da_h, km_h[:, hs], nn)
                               + (dg(da_h, km_l[:, hs], nn) + dg(da_l, km_h[:, hs], nn)))
                    dkm.append(dg(da_h, qm_h[:, hs], tn)
                               + (dg(da_h, qm_l[:, hs], tn) + dg(da_l, qm_h[:, hs], tn)))
                    dst[h0 + j] = ds_new * e_last[:, hs] + _dot_tn(do_b[:, hs], qt_b[:, hs])
                    s_dots.append(jnp.sum(s_old * ds_new, axis=0, keepdims=True))
                dv, dkd, dqt, dqm, dkm, s_dots = [
                    jnp.concatenate(parts, axis=1) for parts in (dv, dkd, dqt, dqm, dkm, s_dots)]
                dq = dqt * e_q + dqm * e_qm
                dk = dkm * e_km + dkd * e_kd
                last = jnp.sum(dkd * kd, axis=0, keepdims=True) + e_last * s_dots
                db = q * dq - k * dk + jnp.where(is_last, last, 0.0)
                dlogf = _tri_dot(tri_rev, db)
                common = sig_neg * (dlogf / f - dk)
                dlb_ref[:, cs] += _rowsum8(common)
                c0 = 3 * D + HEAD * h0
                sq = _sigmoid(zq)
                dz_ref[rs, c0:c0 + width] = (dq * sq * (1.0 + zq * (1.0 - sq))).astype(BF16)
                dz_ref[rs, D + c0:D + c0 + width] = ((1.0 - lbh) * sig * common).astype(BF16)
                dz_ref[rs, 2 * D + c0:2 * D + c0 + width] = dv.astype(BF16)
                dz_ref[rs, 3 * D + c0:3 * D + c0 + width] = d_zg.astype(BF16)
            return carry

        lax.fori_loop(0, nc, chunk, 0, unroll=2)

    rev = lambda i: n_steps - 1 - i
    col = lambda j: pl.BlockSpec((tt, D), lambda i: (rev(i), j))
    row = pl.BlockSpec((tt, D), lambda i: (rev(i), 0))
    acc = _full((SUBLANES, D))
    n = len(modes)
    return pl.pallas_call(
        _hosted(body, 9, 3, (n_steps,), modes), name="hgrn_bwd", grid=(n_steps,),
        out_shape=[jax.ShapeDtypeStruct((t, N_COLS), BF16),
                   jax.ShapeDtypeStruct((SUBLANES, D), F32),
                   jax.ShapeDtypeStruct((SUBLANES, D), F32)] + _recv_shapes(grads, modes),
        in_specs=[row, col(3), col(4), col(5), col(6), row,
                  pl.BlockSpec((nc, N_HEADS, HEAD, HEAD), lambda i: (rev(i), 0, 0, 0)),
                  _full((2, D)), _full((1, D))] + [ANY] * n,
        out_specs=[pl.BlockSpec((tt, N_COLS), lambda i: (rev(i), 0)), acc, acc] + [ANY] * n,
        scratch_shapes=[pltpu.VMEM((N_HEADS, HEAD, HEAD), F32)] + _exchange_scratch(n),
        compiler_params=_params(48, dimension_semantics=("arbitrary",)),
    )(dy, z, z, z, z, o_raw, states, lb_logits, onorm_g, *grads)


def _conv_bwd(dy, z, yc, y2, conv_w_all, cn_g, cn_b, w_pw2, dz, grads, modes):
    t = z.shape[0]
    tt = min(256, t)
    rc = 32
    n_steps = t // tt

    def body(dy_ref, val_ref, glu_ref, gate_ref, yc_ref, y2_ref, cw_ref, g_ref, b_ref, w_hbm,
             dz_in, dz_ref, dw_hbm, dcw_out, db2_ref, dg_ref, dbeta_ref, dcb_ref,
             w_vmem, dw, dbuf, dsh, y1buf, dnbuf, dcw_ref):
        i = pl.program_id(0)

        @pl.when(i == 0)
        def _():
            pltpu.sync_copy(w_hbm, w_vmem)
            dw[...] = jnp.zeros_like(dw)
            dbuf[tt:tt + CONV_PAD, :] = jnp.zeros((CONV_PAD, D), F32)
            dcw_ref[...] = jnp.zeros_like(dcw_ref)
            dcw_out[...] = jnp.zeros_like(dcw_out)
            db2_ref[...] = jnp.zeros_like(db2_ref)
            dg_ref[...] = jnp.zeros_like(dg_ref)
            dbeta_ref[...] = jnp.zeros_like(dbeta_ref)
            dcb_ref[...] = jnp.zeros_like(dcb_ref)

        gate = gate_ref[...]
        sg = _sigmoid(gate)
        dyv = dy_ref[...]
        dy2 = dyv * gate * sg
        dz_ref[:, 2 * D:3 * D] = (dyv * y2_ref[...] * sg * (1.0 + gate * (1.0 - sg))).astype(BF16)
        db2_ref[...] += _rowsum8(dy2)
        dy2b = dy2.astype(BF16)
        dnbuf[...] = _dot_nt(dy2b, w_vmem[...])

        def norm_chunk(r, carry):
            r0 = pl.multiple_of(r * rc, rc)
            rs = pl.ds(r0, rc)
            for g in range(N_HEADS):
                cs = slice(HEAD * g, HEAD * (g + 1))
                blk = yc_ref[rs, cs]
                mu = jnp.mean(blk, axis=-1, keepdims=True)
                cen = blk - mu
                rstd = lax.rsqrt(jnp.mean(cen * cen, axis=-1, keepdims=True) + EPS)
                xhat = cen * rstd
                n = xhat * g_ref[:, cs] + b_ref[:, cs]
                sn = _sigmoid(n)
                y1buf[rs, cs] = (n * sn).astype(BF16)
                dn = dnbuf[rs, cs] * sn * (1.0 + n * (1.0 - sn))
                dg_ref[:, cs] += _rowsum8(dn * xhat)
                dbeta_ref[:, cs] += _rowsum8(dn)
                dxh = dn * g_ref[:, cs]
                dyc = rstd * (dxh - jnp.mean(dxh, axis=-1, keepdims=True)
                              - xhat * jnp.mean(dxh * xhat, axis=-1, keepdims=True))
                dcb_ref[:, cs] += _rowsum8(dyc)
                dbuf[rs, cs] = dyc
            return carry

        lax.fori_loop(0, tt // rc, norm_chunk, 0)
        dw[...] += _dot_tn(y1buf[...], dy2b)
        _shifted_copies(dbuf, dsh, tt + 24)

        def conv_chunk(r, carry):
            r0 = pl.multiple_of(r * rc, rc)
            rs = pl.ds(r0, rc)
            for g in range(N_HEADS):
                cs = slice(HEAD * g, HEAD * (g + 1))
                sglu = _sigmoid(glu_ref[rs, cs])
                val = val_ref[rs, cs]
                v = val * sglu
                dv = jnp.zeros((rc, HEAD), F32)
                for k in range(CONV_K):
                    ref, off = _tap_ref(dbuf, dsh, CONV_K - 1 - k)
                    d_later = ref[pl.ds(r0 + off, rc), cs]
                    dv = dv + cw_ref[g, k:k + 1, :] * d_later
                    dcw_ref[g, k] += _rowsum8(v * d_later)
                dz_ref[rs, cs] = (dv * sglu).astype(BF16)
                dz_ref[rs, D + HEAD * g:D + HEAD * (g + 1)] = (
                    dv * val * sglu * (1.0 - sglu)).astype(BF16)
            return carry

        lax.fori_loop(0, tt // rc, conv_chunk, 0)
        dbuf[tt:tt + CONV_PAD, :] = dbuf[0:CONV_PAD, :]

        @pl.when(i == n_steps - 1)
        def _():
            pltpu.sync_copy(dw, dw_hbm)
            for g in range(N_HEADS):
                for k in range(CONV_K):
                    dcw_out[g, k:k + 1, :] = jnp.sum(dcw_ref[g, k], axis=0, keepdims=True)

    rev = lambda i: n_steps - 1 - i
    col = lambda j: pl.BlockSpec((tt, D), lambda i: (rev(i), j))
    row = pl.BlockSpec((tt, D), lambda i: (rev(i), 0))
    acc = _full((SUBLANES, D))
    n = len(modes)
    return pl.pallas_call(
        _hosted(body, 11, 7, (n_steps,), modes), name="conv_bwd", grid=(n_steps,),
        out_shape=[jax.ShapeDtypeStruct((t, N_COLS), BF16), jax.ShapeDtypeStruct((D, D), F32),
                   jax.ShapeDtypeStruct((N_DEV, CONV_PAD, HEAD), F32)]
        + [jax.ShapeDtypeStruct((SUBLANES, D), F32)] * 4 + _recv_shapes(grads, modes),
        in_specs=[row, col(0), col(1), col(2), row, row, _full((N_DEV, CONV_PAD, HEAD)),
                  _full((1, D)), _full((1, D)), ANY, ANY] + [ANY] * n,
        out_specs=[pl.BlockSpec((tt, 3 * D), lambda i: (rev(i), 0)), ANY,
                   _full((N_DEV, CONV_PAD, HEAD)), acc, acc, acc, acc] + [ANY] * n,
        input_output_aliases={10: 0},
        scratch_shapes=[pltpu.VMEM((D, D), BF16), pltpu.VMEM((D, D), F32),
                        pltpu.VMEM((tt + CONV_PAD, D), F32),
                        pltpu.VMEM((SUBLANES, tt + CONV_PAD, D), F32),
                        pltpu.VMEM((tt, D), BF16), pltpu.VMEM((tt, D), F32),
                        pltpu.VMEM((N_DEV, CONV_PAD, SUBLANES, HEAD), F32)] + _exchange_scratch(n),
        compiler_params=_params(52, dimension_semantics=("arbitrary",)),
    )(dy, z, z, z, yc, y2, conv_w_all, cn_g, cn_b, w_pw2, dz, *grads)


def _inproj_bwd_dx(dz, x, d_h1, ln_g, w_in_all, grads, modes):
    t = x.shape[0]
    tt = min(256, t)

    def body(dz_ref, x_ref, dh1_ref, g_ref, w_hbm, dx_ref, dg_ref, w_vmem):
        @pl.when(pl.program_id(0) == 0)
        def _():
            for d in range(N_DEV):
                pltpu.sync_copy(w_hbm.at[d], w_vmem.at[
                    d // 2, :, pl.ds(COLS_PER_DEV * (d % 2), COLS_PER_DEV)])
            dg_ref[...] = jnp.zeros_like(dg_ref)

        du = jnp.zeros((tt, D), F32)
        for q in range(N_CHIPS):
            du = du + lax.dot_general(
                dz_ref[:, PAIR_COLS * q:PAIR_COLS * (q + 1)], w_vmem[q],
                (((1,), (1,)), ((), ())), preferred_element_type=F32)
        xv = x_ref[...]
        rstd = lax.rsqrt(jnp.mean(xv * xv, axis=-1, keepdims=True) + EPS)
        xhat = xv * rstd
        dg_ref[...] += _rowsum8(du * xhat)
        dx_ref[...] = dh1_ref[...] + _rms_bwd(du * g_ref[...], xhat, rstd)

    row = pl.BlockSpec((tt, D), lambda i: (i, 0))
    n = len(modes)
    return pl.pallas_call(
        _hosted(body, 5, 2, (t // tt,), modes), name="inproj_bwd_dx", grid=(t // tt,),
        out_shape=[jax.ShapeDtypeStruct((t, D), F32), jax.ShapeDtypeStruct((SUBLANES, D), F32)]
        + _recv_shapes(grads, modes),
        in_specs=[pl.BlockSpec((tt, N_COLS), lambda i: (i, 0)), row, row, _full((1, D)), ANY]
        + [ANY] * n,
        out_specs=[row, _full((SUBLANES, D))] + [ANY] * n,
        scratch_shapes=[pltpu.VMEM((N_CHIPS, D, PAIR_COLS), BF16)] + _exchange_scratch(n),
        compiler_params=_params(48, dimension_semantics=("arbitrary",)),
    )(dz, x, d_h1, ln_g, w_in_all, *grads)


def _inproj_bwd_dw(name, u, dz, first, count, grads=(), modes=()):
    t = u.shape[0]
    tt = min(512, t)
    grid = (count // 2, t // tt)
    n = len(modes)

    def body(u_ref, dz_ref, dw_ref):
        @pl.when(pl.program_id(1) == 0)
        def _():
            dw_ref[...] = jnp.zeros_like(dw_ref)

        both = lax.dot_general(u_ref[...], dz_ref[...], (((0,), (0,)), ((), ())),
                               preferred_element_type=F32)
        dw_ref[0] += both[:, :COLS_PER_DEV]
        dw_ref[1] += both[:, COLS_PER_DEV:]

    return pl.pallas_call(
        _hosted(body, 2, 1, grid, modes) if n else body, name=name, grid=grid,
        out_shape=[jax.ShapeDtypeStruct((count, D, COLS_PER_DEV), F32)]
        + _recv_shapes(grads, modes),
        in_specs=[pl.BlockSpec((tt, D), lambda j, i: (i, 0)),
                  pl.BlockSpec((tt, PAIR_COLS), lambda j, i: (i, first // 2 + j))] + [ANY] * n,
        out_specs=[pl.BlockSpec((2, D, COLS_PER_DEV), lambda j, i: (j, 0, 0))] + [ANY] * n,
        scratch_shapes=_exchange_scratch(n) if n else [],
        compiler_params=_params(40, dimension_semantics=("arbitrary", "arbitrary")),
    )(u, dz, *grads)


def _adamw(w, g, m, v):
    m = ADAM_B1 * m + (1.0 - ADAM_B1) * g
    v = ADAM_B2 * v + (1.0 - ADAM_B2) * (g * g)
    m_hat = m / (1.0 - ADAM_B1 ** ADAM_STEP)
    v_hat = v / (1.0 - ADAM_B2 ** ADAM_STEP)
    delta = -ADAM_LR * (m_hat / (jnp.sqrt(v_hat) + ADAM_EPS) + ADAM_WD * w)
    return delta, m, v


def _pack_small(partials):
    rows = sorted(partials)

    def body(*refs):
        ins, out_ref = refs[:-1], refs[-1]
        out_ref[...] = jnp.zeros_like(out_ref)
        for j, row in enumerate(rows):
            out_ref[row:row + 1, :] = jnp.sum(ins[j][...], axis=0, keepdims=True)

    return pl.pallas_call(
        body, name="pack_small", out_shape=jax.ShapeDtypeStruct((N_SMALL, D), F32),
    )(*[partials[row] for row in rows])


def _sum_adam(name, recvs, w, m, v, rows):
    r, c = w.shape
    n = len(recvs)

    def body(*refs):
        w_ref, m_ref, v_ref, g_ref, d_ref, mo_ref, vo_ref = refs[n:]

        def finish(recv_ref):
            g = recv_ref[0]
            for s in range(1, recv_ref.shape[0]):
                g = g + recv_ref[s]
            g_ref[...] = g
            d_ref[...], mo_ref[...], vo_ref[...] = _adamw(w_ref[...], g, m_ref[...], v_ref[...])

        if n == 1:
            finish(refs[0])
        else:
            for side in range(n):
                pl.when(lax.axis_index("x") == side)(functools.partial(finish, refs[side]))

    blk = pl.BlockSpec((rows, c), lambda i: (i, 0))
    return pl.pallas_call(
        body, name=name, grid=(r // rows,),
        out_shape=[jax.ShapeDtypeStruct((r, c), F32)] * 4,
        in_specs=[pl.BlockSpec((rv.shape[0], rows, c), lambda i: (0, i, 0)) for rv in recvs]
        + [blk, blk, blk],
        out_specs=[blk] * 4,
        compiler_params=_params(48, dimension_semantics=("arbitrary",)),
    )(*recvs, w, m, v)


def _small_adam(gathered, lb_logits, w, m, v):
    def body(ga_ref, lbl_ref, w_ref, m_ref, v_ref, g_ref, d_ref, mo_ref, vo_ref, loss_ref):
        g = ga_ref[0]
        for s in range(1, N_DEV):
            g = g + ga_ref[s]
        s0, s1 = _lower_bound(lbl_ref[...])
        d_lb = g[R_LB0:R_LB0 + 1, :]
        rows = lax.broadcasted_iota(jnp.int32, (N_SMALL, D), 0)
        g = jnp.where(rows == R_LB0, d_lb * s0 * (1.0 - s0), g)
        g = jnp.where(rows == R_LB1, -d_lb * s0 * s1, g)
        g_ref[...] = g
        d_ref[...], mo_ref[...], vo_ref[...] = _adamw(w_ref[...], g, m_ref[...], v_ref[...])
        loss_ref[...] = (0.5 / D) * jnp.sum(g[R_LOSS:R_LOSS + 1, :], axis=-1, keepdims=True)

    return pl.pallas_call(
        body, name="small_adam",
        out_shape=[jax.ShapeDtypeStruct((N_SMALL, D), F32)] * 4 + [jax.ShapeDtypeStruct((1, 1), F32)],
    )(gathered, lb_logits, w, m, v)


def _pad_rows(a, rows):
    return jnp.pad(a, ((0, rows - a.shape[0]), (0, 0)))


def _pack_rows(rows):
    rows = [r.reshape(-1, D) for r in rows]
    packed = jnp.concatenate(rows, axis=0)
    return _pad_rows(packed, N_SMALL)


def kernel(x, p, ln_g, w_in, conv_w, conv_b, cnorm_g, cnorm_b, w_pw2, b_pw2, lb_logits, onorm_g, w_out, pe_norm_g, w_pg, w_pp, final_g, loss_target, m_ln_g, m_w_in, m_conv_w, m_conv_b, m_cnorm_g, m_cnorm_b, m_w_pw2, m_b_pw2, m_lb_logits, m_onorm_g, m_w_out, m_pe_norm_g, m_w_pg, m_w_pp, m_final_g, v_ln_g, v_w_in, v_conv_w, v_conv_b, v_cnorm_g, v_cnorm_b, v_w_pw2, v_b_pw2, v_lb_logits, v_onorm_g, v_w_out, v_pe_norm_g, v_w_pg, v_w_pp, v_final_g):
    t = x.shape[1]
    x2 = x.reshape(t, D)
    p2 = p.reshape(t, PLE)
    tg2 = loss_target.reshape(t, D)
    fin_g = final_g.reshape(1, D)

    z, u, w_in_all, conv_w_all, w_pw2_all = _inproj_fwd(
        x2, ln_g, w_in[0].astype(BF16), [_pad_rows(conv_w[0], CONV_PAD), w_pw2[0].astype(BF16)])
    o_raw, y_hgrn, states, w_out_all, w_pg_all, w_pp_all = _hgrn_fwd(
        z, lb_logits, onorm_g, [w_out[0].astype(BF16), w_pg[0].astype(BF16), w_pp[0].astype(BF16)])
    w_pw2_full = w_pw2_all.reshape(D, D)
    w_out_full = w_out_all.reshape(2 * D, D)
    w_pg_full = w_pg_all.reshape(D, D)
    yc, y2, y_conv = _conv_fwd(z, conv_w_all, conv_b, cnorm_g, cnorm_b, w_pw2_full, b_pw2)

    (d_h1, dy_conv, dy_hgrn, d_w_out, d_w_pg, d_w_pp, d_pen_p, d_fin_p, loss_p) = _tail(
        x2, y_conv, y_hgrn, p2, tg2, w_out_full, w_pg_full, w_pp_all, pe_norm_g, fin_g)

    dz, d_on_p, d_lb_p, r_w_out, r_w_pg, r_w_pp = _hgrn_bwd(
        dy_hgrn, z, o_raw, states, lb_logits, onorm_g,
        [d_w_out.reshape(N_DEV, 2 * D // N_DEV, D), d_w_pg.reshape(N_DEV, D // N_DEV, D), d_w_pp])
    (d_w_in_hi,) = _inproj_bwd_dw("inproj_bwd_dw_hi", u, dz, N_DEV // 2, N_DEV // 2)
    dz, d_w_pw2, d_conv_w, d_b2_p, d_cng_p, d_cnb_p, d_cb_p, r_w_in_hi = _conv_bwd(
        dy_conv, z, yc, y2, conv_w_all, cnorm_g, cnorm_b, w_pw2_full, dz, [d_w_in_hi], [1])
    d_w_in_lo, r_w_pw2, r_conv_w = _inproj_bwd_dw(
        "inproj_bwd_dw_lo", u, dz, 0, N_DEV // 2,
        [d_w_pw2.reshape(N_DEV, D // N_DEV, D), d_conv_w], ["scatter", "scatter"])
    chip_lo = _pair_reduce("pair_reduce_lo", d_w_in_lo)
    grad_x, d_ln_p, r_w_in_lo = _inproj_bwd_dx(
        dz, x2, d_h1, ln_g, w_in_all, [chip_lo], [("chip", 0)])

    small = _pack_small({R_LN: d_ln_p, R_CONVB: d_cb_p, R_CNG: d_cng_p, R_CNB: d_cnb_p,
                         R_BPW2: d_b2_p, R_LB0: d_lb_p, R_ON: d_on_p, R_PEN: d_pen_p,
                         R_FIN: d_fin_p, R_LOSS: loss_p})
    (small_all,) = _exchange_call("gather_small", [small], ["gather"])

    big = {}
    big["w_in"] = _sum_adam("adam_w_in", [r_w_in_lo, r_w_in_hi], w_in[0], m_w_in[0], v_w_in[0], 128)
    cw = _sum_adam("adam_conv_w", [r_conv_w], _pad_rows(conv_w[0], CONV_PAD),
                   _pad_rows(m_conv_w[0], CONV_PAD), _pad_rows(v_conv_w[0], CONV_PAD), CONV_PAD)
    big["conv_w"] = [a[:CONV_K] for a in cw]
    big["w_pw2"] = _sum_adam("adam_w_pw2", [r_w_pw2], w_pw2[0], m_w_pw2[0], v_w_pw2[0], 128)
    big["w_out"] = _sum_adam("adam_w_out", [r_w_out], w_out[0], m_w_out[0], v_w_out[0], 128)
    big["w_pg"] = _sum_adam("adam_w_pg", [r_w_pg], w_pg[0], m_w_pg[0], v_w_pg[0], 128)
    big["w_pp"] = _sum_adam("adam_w_pp", [r_w_pp], w_pp[0], m_w_pp[0], v_w_pp[0], PLE)

    small_w = [ln_g, conv_b, cnorm_g, cnorm_b, b_pw2, lb_logits, onorm_g, pe_norm_g, final_g]
    small_m = [m_ln_g, m_conv_b, m_cnorm_g, m_cnorm_b, m_b_pw2, m_lb_logits, m_onorm_g,
               m_pe_norm_g, m_final_g]
    small_v = [v_ln_g, v_conv_b, v_cnorm_g, v_cnorm_b, v_b_pw2, v_lb_logits, v_onorm_g,
               v_pe_norm_g, v_final_g]
    sg, sd, sm, sv, loss = _small_adam(small_all, lb_logits, _pack_rows(small_w),
                                       _pack_rows(small_m), _pack_rows(small_v))

    small_rows = {"ln_g": (R_LN, 1), "conv_b": (R_CONVB, 1), "cnorm_g": (R_CNG, 1),
                  "cnorm_b": (R_CNB, 1), "b_pw2": (R_BPW2, 1), "lb_logits": (R_LB0, 2),
                  "onorm_g": (R_ON, 1), "pe_norm_g": (R_PEN, 1), "final_g": (R_FIN, 1)}
    order = ["ln_g", "w_in", "conv_w", "conv_b", "cnorm_g", "cnorm_b", "w_pw2", "b_pw2",
             "lb_logits", "onorm_g", "w_out", "pe_norm_g", "w_pg", "w_pp", "final_g"]

    def leaf(kind, name):
        if name in big:
            return big[name][kind][None]
        r0, n = small_rows[name]
        a = (sg, sd, sm, sv)[kind][r0:r0 + n]
        return a.reshape(D) if name == "final_g" else a

    outs = [loss.reshape(()), grad_x.reshape(1, t, D)]
    for kind in range(4):
        outs += [leaf(kind, name) for name in order]
    return tuple(outs)
```

```python
import functools

import jax
import jax.numpy as jnp
from jax import lax
from jax.experimental import pallas as pl
from jax.experimental.pallas import tpu as pltpu

F32 = jnp.float32
BF16 = jnp.bfloat16
MESH = pl.DeviceIdType.MESH

N_DEV = 8
D = 1024
N_COLS = 7 * D
COLS_PER_DEV = N_COLS // N_DEV
PLE = 256
HEAD = 128
N_HEADS = D // HEAD
CONV_K = 31
CONV_PAD = 32
CHUNK = 64
EPS = 1e-6
SUBLANES = 8

ADAM_LR = 0.001
ADAM_B1 = 0.9
ADAM_B2 = 0.999
ADAM_EPS = 1e-08
ADAM_WD = 0.01
ADAM_STEP = 10

MIB = 1024 * 1024
N_SMALL = 16
R_LN, R_CONVB, R_CNG, R_CNB, R_BPW2, R_LB0, R_LB1, R_ON, R_PEN, R_FIN, R_LOSS = range(11)


def _params(vmem_mib, **kw):
    return pltpu.CompilerParams(vmem_limit_bytes=vmem_mib * MIB, **kw)


def _dot(a, b):
    return jnp.dot(a.astype(BF16), b.astype(BF16), preferred_element_type=F32)


def _dot_nt(a, b):
    return lax.dot_general(a.astype(BF16), b.astype(BF16), (((1,), (1,)), ((), ())),
                           preferred_element_type=F32)


def _dot_tn(a, b):
    return lax.dot_general(a.astype(BF16), b.astype(BF16), (((0,), (0,)), ((), ())),
                           preferred_element_type=F32)


def _split(a):
    hi = a.astype(BF16)
    return hi, (a - hi.astype(F32)).astype(BF16)


def _dot_split(a, b, dims):
    ah, al = _split(a)
    bh, bl = _split(b)
    dg = lambda p, q: lax.dot_general(p, q, dims, preferred_element_type=F32)
    return dg(ah, bh) + (dg(ah, bl) + dg(al, bh))


def _sigmoid(x):
    return 1.0 / (1.0 + jnp.exp(-x))


def _rowsum8(a):
    r, c = a.shape
    return jnp.sum(a.reshape(r // SUBLANES, SUBLANES, c), axis=0)


def _tri_dot(tri, a):
    hi = a.astype(BF16)
    r1 = a - hi.astype(F32)
    mid = r1.astype(BF16)
    lo = (r1 - mid.astype(F32)).astype(BF16)
    return (jnp.dot(tri, hi, preferred_element_type=F32)
            + jnp.dot(tri, mid, preferred_element_type=F32)
            + jnp.dot(tri, lo, preferred_element_type=F32))


def _lower_bound(lbl):
    l0, l1 = lbl[0:1, :], lbl[1:2, :]
    m = jnp.maximum(l0, l1)
    e0, e1 = jnp.exp(l0 - m), jnp.exp(l1 - m)
    s = e0 + e1
    return e0 / s, e1 / s


ANY = pl.BlockSpec(memory_space=pl.ANY)


def _full(shape):
    return pl.BlockSpec(shape, lambda i: (0,) * len(shape))


def _peer(x, y, c, k):
    px = 1 - x if k & 4 else x
    py = 1 - y if k & 2 else y
    pc = 1 - c if k & 1 else c
    return (px, py, pc), 4 * px + 2 * py + pc


class _Exchange:
    def __init__(self, srcs, outs, modes, send_sems, recv_sems, local_sems):
        x, y, c = lax.axis_index("x"), lax.axis_index("y"), lax.axis_index("c")
        me = 4 * x + 2 * y + c
        self.starts, self.send_waits, self.recv_waits = [], [], []

        def remote(a, k, src, slot, peer, when):
            sem = a * N_DEV + k
            cp = pltpu.make_async_remote_copy(
                src_ref=src, dst_ref=outs[a].at[slot], send_sem=send_sems.at[sem],
                recv_sem=recv_sems.at[sem], device_id=peer, device_id_type=MESH)
            self.starts.append((when, cp.start))
            self.send_waits.append((when, cp.wait_send))

        def arrival(a, k, slot, when):
            sem = a * N_DEV + k
            cp = pltpu.make_async_remote_copy(
                src_ref=outs[a].at[slot], dst_ref=outs[a].at[slot], send_sem=send_sems.at[sem],
                recv_sem=recv_sems.at[sem], device_id=(x, y, c), device_id_type=MESH)
            self.recv_waits.append((when, cp.wait_recv))

        def local(a, src, slot, when):
            cp = pltpu.make_async_copy(src, outs[a].at[slot], local_sems.at[a])
            self.starts.append((when, cp.start))
            self.send_waits.append((when, cp.wait))

        for a, (src, mode) in enumerate(zip(srcs, modes)):
            if mode in ("gather", "scatter"):
                local(a, src if mode == "gather" else src.at[me], me, None)
                for k in range(1, N_DEV):
                    peer, peer_idx = _peer(x, y, c, k)
                    remote(a, k, src if mode == "gather" else src.at[peer_idx], me, peer, None)
                    arrival(a, k, peer_idx, None)
                continue
            if isinstance(mode, tuple):
                here, away = x == mode[1], x != mode[1]
                chip = 2 * x + y
                local(a, src.at[y], chip, here)
                remote(a, 1, src.at[1 - y], chip, (x, 1 - y, c), here)
                remote(a, 2, src.at[y], chip, (1 - x, y, c), away)
                remote(a, 3, src.at[1 - y], chip, (1 - x, 1 - y, c), away)
                arrival(a, 1, 2 * x + 1 - y, here)
                arrival(a, 2, 2 * (1 - x) + y, here)
                arrival(a, 3, 2 * (1 - x) + 1 - y, here)
                continue
            here, away = x == mode, x != mode
            for kk in range(4):
                py = 1 - y if kk & 2 else y
                pc = 1 - c if kk & 1 else c
                block = src.at[2 * py + pc]
                if kk == 0:
                    local(a, block, me, here)
                else:
                    remote(a, kk, block, me, (x, py, pc), here)
                remote(a, 4 + kk, block, me, (1 - x, py, pc), away)
            for k in range(1, N_DEV):
                arrival(a, k, _peer(x, y, c, k)[1], here)

    @staticmethod
    def _run(actions):
        for when, fn in actions:
            if when is None:
                fn()
            else:
                pl.when(when)(fn)

    def start(self):
        self._run(self.starts)

    def wait(self):
        self._run(self.recv_waits)
        self._run(self.send_waits)


def _exchange_scratch(n):
    return [pltpu.SemaphoreType.DMA((n * N_DEV,)), pltpu.SemaphoreType.DMA((n * N_DEV,)),
            pltpu.SemaphoreType.DMA((n,))]


def _recv_shapes(srcs, modes):
    def shape(s, m):
        if m == "gather":
            return (N_DEV,) + s.shape
        return (N_DEV // 2 if isinstance(m, tuple) else N_DEV,) + s.shape[1:]

    return [jax.ShapeDtypeStruct(shape(s, m), s.dtype) for s, m in zip(srcs, modes)]


def _pair_reduce(name, blocks):
    shape = (2,) + blocks.shape[1:]

    def body(src, out_ref, stage, mine, send_sems, recv_sems, local_sems):
        x, y, c = lax.axis_index("x"), lax.axis_index("y"), lax.axis_index("c")
        sends, waits = [], []
        for py in range(2):
            sends.append(pltpu.make_async_remote_copy(
                src_ref=src.at[2 * py + 1 - c], dst_ref=stage.at[py], send_sem=send_sems.at[py],
                recv_sem=recv_sems.at[py], device_id=(x, y, 1 - c), device_id_type=MESH))
            waits.append(pltpu.make_async_copy(src.at[2 * py + c], mine.at[py], local_sems.at[py]))
        for cp in sends + waits:
            cp.start()
        for cp in waits:
            cp.wait()
        for cp in sends:
            cp.wait_recv()
        out_ref[...] = mine[...] + stage[...]
        for cp in sends:
            cp.wait_send()

    return pl.pallas_call(
        body, name=name, out_shape=jax.ShapeDtypeStruct(shape, F32), in_specs=[ANY],
        scratch_shapes=[pltpu.VMEM(shape, F32), pltpu.VMEM(shape, F32),
                        pltpu.SemaphoreType.DMA((2,)), pltpu.SemaphoreType.DMA((2,)),
                        pltpu.SemaphoreType.DMA((2,))],
        compiler_params=_params(40),
    )(blocks)


def _exchange_call(name, srcs, modes):
    n = len(srcs)

    def body(*refs):
        xch = _Exchange(refs[:n], refs[n:2 * n], modes, *refs[2 * n:])
        xch.start()
        xch.wait()

    return pl.pallas_call(
        body, name=name, out_shape=_recv_shapes(srcs, modes),
        in_specs=[ANY] * n, out_specs=[ANY] * n, scratch_shapes=_exchange_scratch(n),
    )(*srcs)


def _hosted(body, n_in, n_out, grid, modes):
    n = len(modes)

    def hosted(*refs):
        ins, srcs = refs[:n_in], refs[n_in:n_in + n]
        outs = refs[n_in + n:n_in + n + n_out]
        bufs = refs[n_in + n + n_out:n_in + 2 * n + n_out]
        scratch = refs[n_in + 2 * n + n_out:-3]
        xch = _Exchange(srcs, bufs, modes, *refs[-3:])
        first, last = True, True
        for axis, size in enumerate(grid):
            first = jnp.logical_and(first, pl.program_id(axis) == 0)
            last = jnp.logical_and(last, pl.program_id(axis) == size - 1)
        pl.when(first)(xch.start)
        body(*ins, *outs, *scratch)
        pl.when(last)(xch.wait)

    return hosted


N_CHIPS = N_DEV // 2
PAIR_COLS = 2 * COLS_PER_DEV
PUSHED = (1, 2, 4, 6)
FORWARDED = (2, 4, 6)


def _inproj_fwd(x, ln_g, w_shard, other_shards):
    t = x.shape[0]
    tt = min(512, t)
    n_t = t // tt
    modes = ["gather"] * len(other_shards)
    n = len(modes)
    chip = 2 * lax.axis_index("x") + lax.axis_index("y")
    order = jnp.bitwise_xor(chip, jnp.arange(N_CHIPS, dtype=jnp.int32)).astype(jnp.int32)

    def body(order_ref, x_ref, g_ref, shard_hbm, *refs):
        srcs = refs[:n]
        z_ref, u_ref, w_all = refs[n:n + 3]
        bufs = refs[n + 3:2 * n + 3]
        u_all, w_blk, w_send, w_recv, w_local = refs[2 * n + 3:2 * n + 8]
        p, i = pl.program_id(0), pl.program_id(1)
        x, y, c = lax.axis_index("x"), lax.axis_index("y"), lax.axis_index("c")
        mine = 4 * x + 2 * y + c
        others = _Exchange(srcs, bufs, modes, *refs[2 * n + 8:])

        def push(k):
            peer, _ = _peer(x, y, c, k)
            return pltpu.make_async_remote_copy(
                src_ref=shard_hbm, dst_ref=w_all.at[mine], send_sem=w_send.at[k],
                recv_sem=w_recv.at[k], device_id=peer, device_id_type=MESH)

        def forward(k):
            _, owner = _peer(x, y, c, k)
            return pltpu.make_async_remote_copy(
                src_ref=w_all.at[owner], dst_ref=w_all.at[owner], send_sem=w_send.at[k + 1],
                recv_sem=w_recv.at[k + 1], device_id=(x, y, 1 - c), device_id_type=MESH)

        def landed(k):
            _, owner = _peer(x, y, c, k)
            return pltpu.make_async_remote_copy(
                src_ref=w_all.at[owner], dst_ref=w_all.at[owner], send_sem=w_send.at[k],
                recv_sem=w_recv.at[k], device_id=(x, y, c), device_id_type=MESH)

        keep = pltpu.make_async_copy(shard_hbm, w_all.at[mine], w_local.at[0])

        def load_pair(step):
            same = shard_hbm if step == 0 else w_all.at[_peer(x, y, c, 2 * step)[1]]
            other = w_all.at[_peer(x, y, c, 2 * step + 1)[1]]
            for side in range(2):
                @pl.when(c == side)
                def _(side=side):
                    pltpu.sync_copy(same, w_blk.at[:, pl.ds(COLS_PER_DEV * side, COLS_PER_DEV)])
                    pltpu.sync_copy(
                        other, w_blk.at[:, pl.ds(COLS_PER_DEV * (1 - side), COLS_PER_DEV)])

        @pl.when(jnp.logical_and(p == 0, i == 0))
        def _():
            for k in PUSHED:
                push(k).start()
            keep.start()
            others.start()

        for step in range(N_CHIPS):
            @pl.when(jnp.logical_and(p == step, i == 0))
            def _(step=step):
                landed(2 * step + 1).wait_recv()
                load_pair(step)

        rows = pl.ds(pl.multiple_of(i * tt, tt), tt)

        @pl.when(p == 0)
        def _():
            xv = x_ref[...]
            rstd = lax.rsqrt(jnp.mean(xv * xv, axis=-1, keepdims=True) + EPS)
            ub = (xv * rstd * g_ref[...]).astype(BF16)
            u_ref[...] = ub
            u_all[rows, :] = ub

        z_ref[...] = jnp.dot(u_all[rows, :], w_blk[...], preferred_element_type=F32)

        for step in range(1, N_CHIPS):
            @pl.when(jnp.logical_and(p == step - 1, i == n_t - 1))
            def _(step=step):
                landed(2 * step).wait_recv()
                forward(2 * step).start()

        @pl.when(jnp.logical_and(p == N_CHIPS - 1, i == n_t - 1))
        def _():
            for k in PUSHED:
                push(k).wait_send()
            for k in FORWARDED:
                forward(k).wait_send()
            keep.wait()
            others.wait()

    first_pass = lambda p, i, order_ref: (jnp.where(p == 0, i, n_t - 1), 0)
    grid_spec = pltpu.PrefetchScalarGridSpec(
        num_scalar_prefetch=1, grid=(N_CHIPS, n_t),
        in_specs=[pl.BlockSpec((tt, D), first_pass),
                  pl.BlockSpec((1, D), lambda p, i, order_ref: (0, 0)), ANY] + [ANY] * n,
        out_specs=[pl.BlockSpec((tt, PAIR_COLS), lambda p, i, order_ref: (i, order_ref[p])),
                   pl.BlockSpec((tt, D), first_pass), ANY] + [ANY] * n,
        scratch_shapes=[pltpu.VMEM((t, D), BF16), pltpu.VMEM((D, PAIR_COLS), BF16),
                        pltpu.SemaphoreType.DMA((N_DEV,)), pltpu.SemaphoreType.DMA((N_DEV,)),
                        pltpu.SemaphoreType.DMA((1,))] + _exchange_scratch(n))
    return pl.pallas_call(
        body, name="inproj_fwd", grid_spec=grid_spec,
        out_shape=[jax.ShapeDtypeStruct((t, N_COLS), F32), jax.ShapeDtypeStruct((t, D), BF16),
                   jax.ShapeDtypeStruct((N_DEV,) + w_shard.shape, BF16)]
        + _recv_shapes(other_shards, modes),
        compiler_params=_params(48, dimension_semantics=("arbitrary", "arbitrary")),
    )(order, x, ln_g, w_shard, *other_shards)


def _shifted_copies(buf, shifted, rows):
    for b in range(1, SUBLANES):
        shifted[b, 0:rows, :] = buf[b:b + rows, :]


def _tap_ref(buf, shifted, offset):
    a, b = divmod(offset, SUBLANES)
    return (buf if b == 0 else shifted.at[b]), SUBLANES * a


def _tap_slabs(buf, shifted, offset_of_tap):
    groups = {}
    for k in range(CONV_K):
        a, b = divmod(offset_of_tap(k), SUBLANES)
        groups.setdefault(b, []).append((SUBLANES * a, k))
    out = []
    for b, taps in sorted(groups.items()):
        taps.sort()
        lo = taps[0][0]
        out.append((buf if b == 0 else shifted.at[b], lo, [(k, off - lo) for off, k in taps]))
    return out


def _group_norm_stats(blk):
    mu = jnp.mean(blk, axis=-1, keepdims=True)
    cen = blk - mu
    var = jnp.mean(cen * cen, axis=-1, keepdims=True)
    return cen * lax.rsqrt(var + EPS)


def _conv_fwd(z, conv_w_all, conv_b, cn_g, cn_b, w_pw2, b_pw2):
    t = z.shape[0]
    tt = min(256, t)
    rc = 128

    def body(val_ref, glu_ref, gate_ref, cw_ref, cb_ref, g_ref, b_ref, w_hbm, b2_ref,
             yc_ref, y2_ref, yo_ref, w_vmem, vbuf, vsh, y1buf):
        @pl.when(pl.program_id(0) == 0)
        def _():
            pltpu.sync_copy(w_hbm, w_vmem)
            vbuf[0:CONV_PAD, :] = jnp.zeros((CONV_PAD, D), F32)

        vbuf[CONV_PAD:CONV_PAD + tt, :] = val_ref[...] * _sigmoid(glu_ref[...])
        _shifted_copies(vbuf, vsh, tt + 24)

        for g in range(N_HEADS):
            cs = slice(HEAD * g, HEAD * (g + 1))

            def row_chunk(r, carry, g=g, cs=cs):
                r0 = pl.multiple_of(r * rc, rc)
                acc = jnp.broadcast_to(cb_ref[:, cs], (rc, HEAD))
                for ref, lo, taps in _tap_slabs(vbuf, vsh, lambda k: k + 2):
                    slab = ref[pl.ds(r0 + lo, rc + taps[-1][1]), cs]
                    for k, off in taps:
                        acc = acc + cw_ref[g, k:k + 1, :] * slab[off:off + rc]
                yc_ref[pl.ds(r0, rc), cs] = acc
                n = _group_norm_stats(acc) * g_ref[:, cs] + b_ref[:, cs]
                y1buf[pl.ds(r0, rc), cs] = (n * _sigmoid(n)).astype(BF16)
                return carry

            lax.fori_loop(0, tt // rc, row_chunk, 0, unroll=True)
        vbuf[0:CONV_PAD, :] = vbuf[tt:tt + CONV_PAD, :]
        y2 = jnp.dot(y1buf[...], w_vmem[...], preferred_element_type=F32) + b2_ref[...]
        y2_ref[...] = y2
        gate = gate_ref[...]
        yo_ref[...] = (y2 * gate * _sigmoid(gate)).astype(BF16)

    col = lambda j: pl.BlockSpec((tt, D), lambda i: (i, j))
    row = pl.BlockSpec((tt, D), lambda i: (i, 0))
    return pl.pallas_call(
        body, name="conv_fwd", grid=(t // tt,),
        out_shape=[jax.ShapeDtypeStruct((t, D), F32), jax.ShapeDtypeStruct((t, D), F32),
                   jax.ShapeDtypeStruct((t, D), BF16)],
        in_specs=[col(0), col(1), col(2), _full((N_DEV, CONV_PAD, HEAD)), _full((1, D)),
                  _full((1, D)), _full((1, D)), ANY, _full((1, D))],
        out_specs=[row, row, row],
        scratch_shapes=[pltpu.VMEM((D, D), BF16), pltpu.VMEM((tt + CONV_PAD, D), F32),
                        pltpu.VMEM((SUBLANES, tt + CONV_PAD, D), F32), pltpu.VMEM((tt, D), BF16)],
        compiler_params=_params(48, dimension_semantics=("arbitrary",)),
    )(z, z, z, conv_w_all, conv_b, cn_g, cn_b, w_pw2, b_pw2)


HEAD_GROUP = 8
_HEAD_LANES = [slice(HEAD * j, HEAD * (j + 1)) for j in range(HEAD_GROUP)]


def _head_mean(a):
    return jnp.concatenate(
        [jnp.broadcast_to(jnp.mean(a[:, hs], axis=-1, keepdims=True), (a.shape[0], HEAD))
         for hs in _HEAD_LANES], axis=1)


def _chunk_quantities(zq, zf, lbh, tri):
    sig = _sigmoid(zf)
    sig_neg = _sigmoid(-zf)
    f = lbh + (1.0 - lbh) * sig
    k = (1.0 - lbh) * sig_neg
    q = zq * _sigmoid(zq)
    b = _tri_dot(tri, jnp.log(f))
    b_mid = b[CHUNK // 2 - 1:CHUNK // 2, :]
    b_last = b[CHUNK - 1:CHUNK, :]
    e_q = jnp.exp(b)
    e_qm = jnp.exp(b - b_mid)
    e_km = jnp.exp(b_mid - b)
    e_kd = jnp.exp(b_last - b)
    return q, k, f, sig, sig_neg, e_q, e_qm, e_km, e_kd, jnp.exp(b_last)


def _hgrn_fwd(z, lb_logits, onorm_g, shards):
    t = z.shape[0]
    tt = min(256, t)
    nc = tt // CHUNK
    modes = ["gather"] * len(shards)
    n = len(modes)

    def body(q_ref, f_ref, i_ref, g_ref, lbl_ref, on_ref, o_ref, y_ref, s_ref, st):
        @pl.when(pl.program_id(0) == 0)
        def _():
            st[...] = jnp.zeros_like(st)

        lb, _ = _lower_bound(lbl_ref[...])
        rows = lax.broadcasted_iota(jnp.int32, (CHUNK, CHUNK), 0)
        cols = lax.broadcasted_iota(jnp.int32, (CHUNK, CHUNK), 1)
        causal = rows >= cols
        tri = causal.astype(BF16)

        def chunk(c, carry):
            r0 = pl.multiple_of(c * CHUNK, CHUNK)
            rs = pl.ds(r0, CHUNK)
            for h0 in range(0, N_HEADS, HEAD_GROUP):
                cs = slice(HEAD * h0, HEAD * (h0 + HEAD_GROUP))
                q, k, _, _, _, e_q, e_qm, e_km, e_kd, e_last = _chunk_quantities(
                    q_ref[rs, cs], f_ref[rs, cs], lb[:, cs], tri)
                v = i_ref[rs, cs].astype(BF16)
                qm, km = (q * e_qm).astype(BF16), (k * e_km).astype(BF16)
                qt, kd = (q * e_q).astype(BF16), (k * e_kd).astype(BF16)
                outs = []
                for j, hs in enumerate(_HEAD_LANES):
                    s_old = st[h0 + j]
                    s_ref[c, h0 + j] = s_old
                    a = jnp.where(causal, _dot_nt(qm[:, hs], km[:, hs]), 0.0)
                    outs.append(_dot_nt(qt[:, hs], s_old) + _dot(a, v[:, hs]))
                    st[h0 + j] = s_old * e_last[:, hs] + _dot_tn(v[:, hs], kd[:, hs])
                o = jnp.concatenate(outs, axis=1)
                o_ref[rs, cs] = o
                n = o * lax.rsqrt(_head_mean(o * o) + EPS)
                zg = g_ref[rs, cs]
                y_ref[rs, cs] = (n * on_ref[:, cs] * zg * _sigmoid(zg)).astype(BF16)
            return carry

        lax.fori_loop(0, nc, chunk, 0, unroll=2)

    col = lambda j: pl.BlockSpec((tt, D), lambda i: (i, j))
    row = pl.BlockSpec((tt, D), lambda i: (i, 0))
    return pl.pallas_call(
        _hosted(body, 6, 3, (t // tt,), modes), name="hgrn_fwd", grid=(t // tt,),
        out_shape=[jax.ShapeDtypeStruct((t, D), F32), jax.ShapeDtypeStruct((t, D), BF16),
                   jax.ShapeDtypeStruct((t // CHUNK, N_HEADS, HEAD, HEAD), F32)]
        + _recv_shapes(shards, modes),
        in_specs=[col(3), col(4), col(5), col(6), _full((2, D)), _full((1, D))] + [ANY] * n,
        out_specs=[row, row, pl.BlockSpec((nc, N_HEADS, HEAD, HEAD), lambda i: (i, 0, 0, 0))]
        + [ANY] * n,
        scratch_shapes=[pltpu.VMEM((N_HEADS, HEAD, HEAD), F32)] + _exchange_scratch(n),
        compiler_params=_params(40, dimension_semantics=("arbitrary",)),
    )(z, z, z, z, lb_logits, onorm_g, *shards)


def _rms_bwd(dn, xhat, rstd):
    return rstd * (dn - xhat * jnp.mean(dn * xhat, axis=-1, keepdims=True))


def _tail(x, y_conv, y_hgrn, p, target, w_out, w_pg, w_pp_all, pe_g, fin_g):
    t = x.shape[0]
    tt = min(256, t)
    n_steps = t // tt

    def body(x_ref, yc_ref, yh_ref, p_ref, tg_ref, wo_hbm, wg_hbm, wp_hbm, pg_ref, fg_ref,
             dh1_ref, dyc_ref, dyh_ref, dwo_hbm, dwg_hbm, dwp_hbm, dpg_ref, dfg_ref, loss_ref,
             wo, wg, wp, dwo, dwg, dwp):
        i = pl.program_id(0)

        @pl.when(i == 0)
        def _():
            pltpu.sync_copy(wo_hbm, wo)
            pltpu.sync_copy(wg_hbm, wg)
            for d in range(N_DEV):
                pltpu.sync_copy(wp_hbm.at[d], wp.at[:, pl.ds(HEAD * d, HEAD)])
            dwo[...] = jnp.zeros_like(dwo)
            dwg[...] = jnp.zeros_like(dwg)
            dwp[...] = jnp.zeros_like(dwp)
            dpg_ref[...] = jnp.zeros_like(dpg_ref)
            dfg_ref[...] = jnp.zeros_like(dfg_ref)
            loss_ref[...] = jnp.zeros_like(loss_ref)

        ycv, yhv = yc_ref[...], yh_ref[...]
        h1 = (x_ref[...] + jnp.dot(ycv, wo[0:D, :], preferred_element_type=F32)
              + jnp.dot(yhv, wo[D:2 * D, :], preferred_element_type=F32))
        pb = p_ref[...].astype(BF16)
        pe = jnp.dot(pb, wp[...], preferred_element_type=F32)
        rstd1 = lax.rsqrt(jnp.mean(h1 * h1, axis=-1, keepdims=True) + EPS)
        n1 = h1 * rstd1
        rb = (n1 * pg_ref[...]).astype(BF16)
        gate = _sigmoid(jnp.dot(rb, wg[...], preferred_element_type=F32))
        h2 = h1 + gate * pe
        rstd2 = lax.rsqrt(jnp.mean(h2 * h2, axis=-1, keepdims=True) + EPS)
        n2 = h2 * rstd2
        err = n2 * fg_ref[...] - tg_ref[...]
        loss_ref[...] += _rowsum8(err * err)

        d_out = err * (1.0 / D)
        dfg_ref[...] += _rowsum8(d_out * n2)
        d_h2 = _rms_bwd(d_out * fg_ref[...], n2, rstd2)
        d_pe = (d_h2 * gate).astype(BF16)
        d_gpre = (d_h2 * pe * gate * (1.0 - gate)).astype(BF16)
        dwg[...] += _dot_tn(rb, d_gpre)
        dwp[...] += _dot_tn(pb, d_pe)
        dr = _dot_nt(d_gpre, wg[...])
        dpg_ref[...] += _rowsum8(dr * n1)
        d_h1 = d_h2 + _rms_bwd(dr * pg_ref[...], n1, rstd1)
        dh1_ref[...] = d_h1
        d_h1b = d_h1.astype(BF16)
        dwo[0:D, :] += _dot_tn(ycv, d_h1b)
        dwo[D:2 * D, :] += _dot_tn(yhv, d_h1b)
        dyc_ref[...] = _dot_nt(d_h1b, wo[0:D, :])
        dyh_ref[...] = _dot_nt(d_h1b, wo[D:2 * D, :])

        @pl.when(i == n_steps - 1)
        def _():
            pltpu.sync_copy(dwo, dwo_hbm)
            pltpu.sync_copy(dwg, dwg_hbm)
            for d in range(N_DEV):
                pltpu.sync_copy(dwp.at[:, pl.ds(HEAD * d, HEAD)], dwp_hbm.at[d])

    row = pl.BlockSpec((tt, D), lambda i: (i, 0))
    acc = _full((SUBLANES, D))
    return pl.pallas_call(
        body, name="tail_fwd_bwd", grid=(n_steps,),
        out_shape=[jax.ShapeDtypeStruct((t, D), F32)] * 3
        + [jax.ShapeDtypeStruct((2 * D, D), F32), jax.ShapeDtypeStruct((D, D), F32),
           jax.ShapeDtypeStruct((N_DEV, PLE, HEAD), F32)]
        + [jax.ShapeDtypeStruct((SUBLANES, D), F32)] * 3,
        in_specs=[row, row, row, pl.BlockSpec((tt, PLE), lambda i: (i, 0)), row,
                  ANY, ANY, ANY, _full((1, D)), _full((1, D))],
        out_specs=[row, row, row, ANY, ANY, ANY, acc, acc, acc],
        scratch_shapes=[pltpu.VMEM((2 * D, D), BF16), pltpu.VMEM((D, D), BF16),
                        pltpu.VMEM((PLE, D), BF16), pltpu.VMEM((2 * D, D), F32),
                        pltpu.VMEM((D, D), F32), pltpu.VMEM((PLE, D), F32)],
        compiler_params=_params(52, dimension_semantics=("arbitrary",)),
    )(x, y_conv, y_hgrn, p, target, w_out, w_pg, w_pp_all, pe_g, fin_g)


def _hgrn_bwd(dy, z, o_raw, states, lb_logits, onorm_g, grads):
    t = z.shape[0]
    tt = min(256, t)
    nc = tt // CHUNK
    n_steps = t // tt
    modes = ["scatter"] * len(grads)

    def body(dy_ref, q_ref, f_ref, i_ref, g_ref, o_ref, s_ref, lbl_ref, on_ref,
             dz_ref, don_ref, dlb_ref, dst):
        @pl.when(pl.program_id(0) == 0)
        def _():
            dst[...] = jnp.zeros_like(dst)
            don_ref[...] = jnp.zeros_like(don_ref)
            dlb_ref[...] = jnp.zeros_like(dlb_ref)

        lb, _ = _lower_bound(lbl_ref[...])
        rows = lax.broadcasted_iota(jnp.int32, (CHUNK, CHUNK), 0)
        cols = lax.broadcasted_iota(jnp.int32, (CHUNK, CHUNK), 1)
        causal = rows >= cols
        tri = causal.astype(BF16)
        tri_rev = (rows <= cols).astype(BF16)
        width = HEAD * HEAD_GROUP
        is_last = lax.broadcasted_iota(jnp.int32, (CHUNK, width), 0) == CHUNK - 1
        nn = (((1,), (0,)), ((), ()))
        tn = (((0,), (0,)), ((), ()))
        dg = functools.partial(lax.dot_general, preferred_element_type=F32)

        def chunk(cc, carry):
            c = nc - 1 - cc
            r0 = pl.multiple_of(c * CHUNK, CHUNK)
            rs = pl.ds(r0, CHUNK)
            for h0 in range(0, N_HEADS, HEAD_GROUP):
                cs = slice(HEAD * h0, HEAD * h0 + width)
                zq, zf, zg = q_ref[rs, cs], f_ref[rs, cs], g_ref[rs, cs]
                lbh = lb[:, cs]
                q, k, f, sig, sig_neg, e_q, e_qm, e_km, e_kd, e_last = _chunk_quantities(
                    zq, zf, lbh, tri)
                vb = i_ref[rs, cs].astype(BF16)
                qt, qm, km, kd = q * e_q, q * e_qm, k * e_km, k * e_kd
                qt_b, kd_b = qt.astype(BF16), kd.astype(BF16)
                qm_h, qm_l = _split(qm)
                km_h, km_l = _split(km)

                o = o_ref[rs, cs]
                rstd = lax.rsqrt(_head_mean(o * o) + EPS)
                n = o * rstd
                sg = _sigmoid(zg)
                dyv = dy_ref[rs, cs]
                on = on_ref[:, cs]
                d_zg = dyv * n * on * sg * (1.0 + zg * (1.0 - sg))
                d_on = dyv * zg * sg
                don_ref[:, cs] += _rowsum8(d_on * n)
                dn = d_on * on
                do_b = (rstd * (dn - n * _head_mean(dn * n))).astype(BF16)

                dv, dkd, dqt, dqm, dkm, s_dots = [], [], [], [], [], []
                for j, hs in enumerate(_HEAD_LANES):
                    s_old, ds_new = s_ref[c, h0 + j], dst[h0 + j]
                    ds_b = ds_new.astype(BF16)
                    a = jnp.where(causal, _dot_nt(qm_h[:, hs], km_h[:, hs]), 0.0)
                    da = jnp.where(causal, _dot_nt(do_b[:, hs], vb[:, hs]), 0.0)
                    dv.append(_dot_tn(a, do_b[:, hs]) + _dot_nt(kd_b[:, hs], ds_b))
                    dkd.append(_dot(vb[:, hs], ds_b))
                    dqt.append(_dot(do_b[:, hs], s_old))
                    da_h, da_l = _split(da)
                    dqm.append(dg(da_h, km_h[:, hs], nn)
                               + (dg(da_h, km_l[:, hs], nn) + dg(da_l, km_h[:, hs], nn)))
                    dkm.append(dg(da_h, qm_h[:, hs], tn)
                               + (dg(da_h, qm_l[:, hs], tn) + dg(da_l, qm_h[:, hs], tn)))
                    dst[h0 + j] = ds_new * e_last[:, hs] + _dot_tn(do_b[:, hs], qt_b[:, hs])
                    s_dots.append(jnp.sum(s_old * ds_new, axis=0, keepdims=True))
                dv, dkd, dqt, dqm, dkm, s_dots = [
                    jnp.concatenate(parts, axis=1) for parts in (dv, dkd, dqt, dqm, dkm, s_dots)]
                dq = dqt * e_q + dqm * e_qm
                dk = dkm * e_km + dkd * e_kd
                last = jnp.sum(dkd * kd, axis=0, keepdims=True) + e_last * s_dots
                db = q * dq - k * dk + jnp.where(is_last, last, 0.0)
                dlogf = _tri_dot(tri_rev, db)
                common = sig_neg * (dlogf / f - dk)
                dlb_ref[:, cs] += _rowsum8(common)
                c0 = 3 * D + HEAD * h0
                sq = _sigmoid(zq)
                dz_ref[rs, c0:c0 + width] = (dq * sq * (1.0 + zq * (1.0 - sq))).astype(BF16)
                dz_ref[rs, D + c0:D + c0 + width] = ((1.0 - lbh) * sig * common).astype(BF16)
                dz_ref[rs, 2 * D + c0:2 * D + c0 + width] = dv.astype(BF16)
                dz_ref[rs, 3 * D + c0:3 * D + c0 + width] = d_zg.astype(BF16)
            return carry

        lax.fori_loop(0, nc, chunk, 0, unroll=2)

    rev = lambda i: n_steps - 1 - i
    col = lambda j: pl.BlockSpec((tt, D), lambda i: (rev(i), j))
    row = pl.BlockSpec((tt, D), lambda i: (rev(i), 0))
    acc = _full((SUBLANES, D))
    n = len(modes)
    return pl.pallas_call(
        _hosted(body, 9, 3, (n_steps,), modes), name="hgrn_bwd", grid=(n_steps,),
        out_shape=[jax.ShapeDtypeStruct((t, N_COLS), BF16),
                   jax.ShapeDtypeStruct((SUBLANES, D), F32),
                   jax.ShapeDtypeStruct((SUBLANES, D), F32)] + _recv_shapes(grads, modes),
        in_specs=[row, col(3), col(4), col(5), col(6), row,
                  pl.BlockSpec((nc, N_HEADS, HEAD, HEAD), lambda i: (rev(i), 0, 0, 0)),
                  _full((2, D)), _full((1, D))] + [ANY] * n,
        out_specs=[pl.BlockSpec((tt, N_COLS), lambda i: (rev(i), 0)), acc, acc] + [ANY] * n,
        scratch_shapes=[pltpu.VMEM((N_HEADS, HEAD, HEAD), F32)] + _exchange_scratch(n),
        compiler_params=_params(48, dimension_semantics=("arbitrary",)),
    )(dy, z, z, z, z, o_raw, states, lb_logits, onorm_g, *grads)


def _conv_bwd(dy, z, yc, y2, conv_w_all, cn_g, cn_b, w_pw2, dz, grads, modes):
    t = z.shape[0]
    tt = min(256, t)
    rc = 32
    n_steps = t // tt

    def body(dy_ref, val_ref, glu_ref, gate_ref, yc_ref, y2_ref, cw_ref, g_ref, b_ref, w_hbm,
             dz_in, dz_ref, dw_hbm, dcw_out, db2_ref, dg_ref, dbeta_ref, dcb_ref,
             w_vmem, dw, dbuf, dsh, y1buf, dnbuf, dcw_ref):
        i = pl.program_id(0)

        @pl.when(i == 0)
        def _():
            pltpu.sync_copy(w_hbm, w_vmem)
            dw[...] = jnp.zeros_like(dw)
            dbuf[tt:tt + CONV_PAD, :] = jnp.zeros((CONV_PAD, D), F32)
            dcw_ref[...] = jnp.zeros_like(dcw_ref)
            dcw_out[...] = jnp.zeros_like(dcw_out)
            db2_ref[...] = jnp.zeros_like(db2_ref)
            dg_ref[...] = jnp.zeros_like(dg_ref)
            dbeta_ref[...] = jnp.zeros_like(dbeta_ref)
            dcb_ref[...] = jnp.zeros_like(dcb_ref)

        gate = gate_ref[...]
        sg = _sigmoid(gate)
        dyv = dy_ref[...]
        dy2 = dyv * gate * sg
        dz_ref[:, 2 * D:3 * D] = (dyv * y2_ref[...] * sg * (1.0 + gate * (1.0 - sg))).astype(BF16)
        db2_ref[...] += _rowsum8(dy2)
        dy2b = dy2.astype(BF16)
        dnbuf[...] = _dot_nt(dy2b, w_vmem[...])

        def norm_chunk(r, carry):
            r0 = pl.multiple_of(r * rc, rc)
            rs = pl.ds(r0, rc)
            for g in range(N_HEADS):
                cs = slice(HEAD * g, HEAD * (g + 1))
                blk = yc_ref[rs, cs]
                mu = jnp.mean(blk, axis=-1, keepdims=True)
                cen = blk - mu
                rstd = lax.rsqrt(jnp.mean(cen * cen, axis=-1, keepdims=True) + EPS)
                xhat = cen * rstd
                n = xhat * g_ref[:, cs] + b_ref[:, cs]
                sn = _sigmoid(n)
                y1buf[rs, cs] = (n * sn).astype(BF16)
                dn = dnbuf[rs, cs] * sn * (1.0 + n * (1.0 - sn))
                dg_ref[:, cs] += _rowsum8(dn * xhat)
                dbeta_ref[:, cs] += _rowsum8(dn)
                dxh = dn * g_ref[:, cs]
                dyc = rstd * (dxh - jnp.mean(dxh, axis=-1, keepdims=True)
                              - xhat * jnp.mean(dxh * xhat, axis=-1, keepdims=True))
                dcb_ref[:, cs] += _rowsum8(dyc)
                dbuf[rs, cs] = dyc
            return carry

        lax.fori_loop(0, tt // rc, norm_chunk, 0, unroll=2)
        dw[...] += _dot_tn(y1buf[...], dy2b)
        _shifted_copies(dbuf, dsh, tt + 24)

        def conv_chunk(r, carry):
            r0 = pl.multiple_of(r * rc, rc)
            rs = pl.ds(r0, rc)
            for g in range(N_HEADS):
                cs = slice(HEAD * g, HEAD * (g + 1))
                sglu = _sigmoid(glu_ref[rs, cs])
                val = val_ref[rs, cs]
                v = val * sglu
                dv = jnp.zeros((rc, HEAD), F32)
                for ref, lo, taps in _tap_slabs(dbuf, dsh, lambda k: CONV_K - 1 - k):
                    slab = ref[pl.ds(r0 + lo, rc + taps[-1][1]), cs]
                    for k, off in taps:
                        d_later = slab[off:off + rc]
                        dv = dv + cw_ref[g, k:k + 1, :] * d_later
                        dcw_ref[g, k] += _rowsum8(v * d_later)
                dz_ref[rs, cs] = (dv * sglu).astype(BF16)
                dz_ref[rs, D + HEAD * g:D + HEAD * (g + 1)] = (
                    dv * val * sglu * (1.0 - sglu)).astype(BF16)
            return carry

        lax.fori_loop(0, tt // rc, conv_chunk, 0, unroll=2)
        dbuf[tt:tt + CONV_PAD, :] = dbuf[0:CONV_PAD, :]

        @pl.when(i == n_steps - 1)
        def _():
            pltpu.sync_copy(dw, dw_hbm)
            for g in range(N_HEADS):
                for k in range(CONV_K):
                    dcw_out[g, k:k + 1, :] = jnp.sum(dcw_ref[g, k], axis=0, keepdims=True)

    rev = lambda i: n_steps - 1 - i
    col = lambda j: pl.BlockSpec((tt, D), lambda i: (rev(i), j))
    row = pl.BlockSpec((tt, D), lambda i: (rev(i), 0))
    acc = _full((SUBLANES, D))
    n = len(modes)
    return pl.pallas_call(
        _hosted(body, 11, 7, (n_steps,), modes), name="conv_bwd", grid=(n_steps,),
        out_shape=[jax.ShapeDtypeStruct((t, N_COLS), BF16), jax.ShapeDtypeStruct((D, D), F32),
                   jax.ShapeDtypeStruct((N_DEV, CONV_PAD, HEAD), F32)]
        + [jax.ShapeDtypeStruct((SUBLANES, D), F32)] * 4 + _recv_shapes(grads, modes),
        in_specs=[row, col(0), col(1), col(2), row, row, _full((N_DEV, CONV_PAD, HEAD)),
                  _full((1, D)), _full((1, D)), ANY, ANY] + [ANY] * n,
        out_specs=[pl.BlockSpec((tt, 3 * D), lambda i: (rev(i), 0)), ANY,
                   _full((N_DEV, CONV_PAD, HEAD)), acc, acc, acc, acc] + [ANY] * n,
        input_output_aliases={10: 0},
        scratch_shapes=[pltpu.VMEM((D, D), BF16), pltpu.VMEM((D, D), F32),
                        pltpu.VMEM((tt + CONV_PAD, D), F32),
                        pltpu.VMEM((SUBLANES, tt + CONV_PAD, D), F32),
                        pltpu.VMEM((tt, D), BF16), pltpu.VMEM((tt, D), F32),
                        pltpu.VMEM((N_DEV, CONV_PAD, SUBLANES, HEAD), F32)] + _exchange_scratch(n),
        compiler_params=_params(52, dimension_semantics=("arbitrary",)),
    )(dy, z, z, z, yc, y2, conv_w_all, cn_g, cn_b, w_pw2, dz, *grads)


def _inproj_bwd_dx(dz, x, d_h1, ln_g, w_in_all, grads, modes):
    t = x.shape[0]
    tt = min(256, t)

    def body(dz_ref, x_ref, dh1_ref, g_ref, w_hbm, dx_ref, dg_ref, w_vmem):
        @pl.when(pl.program_id(0) == 0)
        def _():
            for d in range(N_DEV):
                pltpu.sync_copy(w_hbm.at[d], w_vmem.at[
                    d // 2, :, pl.ds(COLS_PER_DEV * (d % 2), COLS_PER_DEV)])
            dg_ref[...] = jnp.zeros_like(dg_ref)

        du = jnp.zeros((tt, D), F32)
        for q in range(N_CHIPS):
            du = du + lax.dot_general(
                dz_ref[:, PAIR_COLS * q:PAIR_COLS * (q + 1)], w_vmem[q],
                (((1,), (1,)), ((), ())), preferred_element_type=F32)
        xv = x_ref[...]
        rstd = lax.rsqrt(jnp.mean(xv * xv, axis=-1, keepdims=True) + EPS)
        xhat = xv * rstd
        dg_ref[...] += _rowsum8(du * xhat)
        dx_ref[...] = dh1_ref[...] + _rms_bwd(du * g_ref[...], xhat, rstd)

    row = pl.BlockSpec((tt, D), lambda i: (i, 0))
    n = len(modes)
    return pl.pallas_call(
        _hosted(body, 5, 2, (t // tt,), modes), name="inproj_bwd_dx", grid=(t // tt,),
        out_shape=[jax.ShapeDtypeStruct((t, D), F32), jax.ShapeDtypeStruct((SUBLANES, D), F32)]
        + _recv_shapes(grads, modes),
        in_specs=[pl.BlockSpec((tt, N_COLS), lambda i: (i, 0)), row, row, _full((1, D)), ANY]
        + [ANY] * n,
        out_specs=[row, _full((SUBLANES, D))] + [ANY] * n,
        scratch_shapes=[pltpu.VMEM((N_CHIPS, D, PAIR_COLS), BF16)] + _exchange_scratch(n),
        compiler_params=_params(48, dimension_semantics=("arbitrary",)),
    )(dz, x, d_h1, ln_g, w_in_all, *grads)


def _inproj_bwd_dw(name, u, dz, first, count, grads=(), modes=()):
    t = u.shape[0]
    tt = min(512, t)
    grid = (count // 2, t // tt)
    n = len(modes)

    def body(u_ref, dz_ref, dw_ref):
        @pl.when(pl.program_id(1) == 0)
        def _():
            dw_ref[...] = jnp.zeros_like(dw_ref)

        both = lax.dot_general(u_ref[...], dz_ref[...], (((0,), (0,)), ((), ())),
                               preferred_element_type=F32)
        dw_ref[0] += both[:, :COLS_PER_DEV]
        dw_ref[1] += both[:, COLS_PER_DEV:]

    return pl.pallas_call(
        _hosted(body, 2, 1, grid, modes) if n else body, name=name, grid=grid,
        out_shape=[jax.ShapeDtypeStruct((count, D, COLS_PER_DEV), F32)]
        + _recv_shapes(grads, modes),
        in_specs=[pl.BlockSpec((tt, D), lambda j, i: (i, 0)),
                  pl.BlockSpec((tt, PAIR_COLS), lambda j, i: (i, first // 2 + j))] + [ANY] * n,
        out_specs=[pl.BlockSpec((2, D, COLS_PER_DEV), lambda j, i: (j, 0, 0))] + [ANY] * n,
        scratch_shapes=_exchange_scratch(n) if n else [],
        compiler_params=_params(40, dimension_semantics=("arbitrary", "arbitrary")),
    )(u, dz, *grads)


def _adamw(w, g, m, v):
    m = ADAM_B1 * m + (1.0 - ADAM_B1) * g
    v = ADAM_B2 * v + (1.0 - ADAM_B2) * (g * g)
    m_hat = m / (1.0 - ADAM_B1 ** ADAM_STEP)
    v_hat = v / (1.0 - ADAM_B2 ** ADAM_STEP)
    delta = -ADAM_LR * (m_hat / (jnp.sqrt(v_hat) + ADAM_EPS) + ADAM_WD * w)
    return delta, m, v


def _pack_small(partials):
    rows = sorted(partials)

    def body(*refs):
        ins, out_ref = refs[:-1], refs[-1]
        out_ref[...] = jnp.zeros_like(out_ref)
        for j, row in enumerate(rows):
            out_ref[row:row + 1, :] = jnp.sum(ins[j][...], axis=0, keepdims=True)

    return pl.pallas_call(
        body, name="pack_small", out_shape=jax.ShapeDtypeStruct((N_SMALL, D), F32),
    )(*[partials[row] for row in rows])


def _sum_adam(name, recvs, w, m, v, rows):
    r, c = w.shape
    n = len(recvs)

    def body(*refs):
        w_ref, m_ref, v_ref, g_ref, d_ref, mo_ref, vo_ref = refs[n:]

        def finish(recv_ref):
            g = recv_ref[0]
            for s in range(1, recv_ref.shape[0]):
                g = g + recv_ref[s]
            g_ref[...] = g
            d_ref[...], mo_ref[...], vo_ref[...] = _adamw(w_ref[...], g, m_ref[...], v_ref[...])

        if n == 1:
            finish(refs[0])
        else:
            for side in range(n):
                pl.when(lax.axis_index("x") == side)(functools.partial(finish, refs[side]))

    blk = pl.BlockSpec((rows, c), lambda i: (i, 0))
    return pl.pallas_call(
        body, name=name, grid=(r // rows,),
        out_shape=[jax.ShapeDtypeStruct((r, c), F32)] * 4,
        in_specs=[pl.BlockSpec((rv.shape[0], rows, c), lambda i: (0, i, 0)) for rv in recvs]
        + [blk, blk, blk],
        out_specs=[blk] * 4,
        compiler_params=_params(48, dimension_semantics=("arbitrary",)),
    )(*recvs, w, m, v)


def _small_adam(gathered, lb_logits, w, m, v):
    def body(ga_ref, lbl_ref, w_ref, m_ref, v_ref, g_ref, d_ref, mo_ref, vo_ref, loss_ref):
        g = ga_ref[0]
        for s in range(1, N_DEV):
            g = g + ga_ref[s]
        s0, s1 = _lower_bound(lbl_ref[...])
        d_lb = g[R_LB0:R_LB0 + 1, :]
        rows = lax.broadcasted_iota(jnp.int32, (N_SMALL, D), 0)
        g = jnp.where(rows == R_LB0, d_lb * s0 * (1.0 - s0), g)
        g = jnp.where(rows == R_LB1, -d_lb * s0 * s1, g)
        g_ref[...] = g
        d_ref[...], mo_ref[...], vo_ref[...] = _adamw(w_ref[...], g, m_ref[...], v_ref[...])
        loss_ref[...] = (0.5 / D) * jnp.sum(g[R_LOSS:R_LOSS + 1, :], axis=-1, keepdims=True)

    return pl.pallas_call(
        body, name="small_adam",
        out_shape=[jax.ShapeDtypeStruct((N_SMALL, D), F32)] * 4 + [jax.ShapeDtypeStruct((1, 1), F32)],
    )(gathered, lb_logits, w, m, v)


def _pad_rows(a, rows):
    return jnp.pad(a, ((0, rows - a.shape[0]), (0, 0)))


def _pack_rows(rows):
    rows = [r.reshape(-1, D) for r in rows]
    packed = jnp.concatenate(rows, axis=0)
    return _pad_rows(packed, N_SMALL)


def kernel(x, p, ln_g, w_in, conv_w, conv_b, cnorm_g, cnorm_b, w_pw2, b_pw2, lb_logits, onorm_g, w_out, pe_norm_g, w_pg, w_pp, final_g, loss_target, m_ln_g, m_w_in, m_conv_w, m_conv_b, m_cnorm_g, m_cnorm_b, m_w_pw2, m_b_pw2, m_lb_logits, m_onorm_g, m_w_out, m_pe_norm_g, m_w_pg, m_w_pp, m_final_g, v_ln_g, v_w_in, v_conv_w, v_conv_b, v_cnorm_g, v_cnorm_b, v_w_pw2, v_b_pw2, v_lb_logits, v_onorm_g, v_w_out, v_pe_norm_g, v_w_pg, v_w_pp, v_final_g):
    t = x.shape[1]
    x2 = x.reshape(t, D)
    p2 = p.reshape(t, PLE)
    tg2 = loss_target.reshape(t, D)
    fin_g = final_g.reshape(1, D)

    z, u, w_in_all, conv_w_all, w_pw2_all = _inproj_fwd(
        x2, ln_g, w_in[0].astype(BF16), [_pad_rows(conv_w[0], CONV_PAD), w_pw2[0].astype(BF16)])
    o_raw, y_hgrn, states, w_out_all, w_pg_all, w_pp_all = _hgrn_fwd(
        z, lb_logits, onorm_g, [w_out[0].astype(BF16), w_pg[0].astype(BF16), w_pp[0].astype(BF16)])
    w_pw2_full = w_pw2_all.reshape(D, D)
    w_out_full = w_out_all.reshape(2 * D, D)
    w_pg_full = w_pg_all.reshape(D, D)
    yc, y2, y_conv = _conv_fwd(z, conv_w_all, conv_b, cnorm_g, cnorm_b, w_pw2_full, b_pw2)

    (d_h1, dy_conv, dy_hgrn, d_w_out, d_w_pg, d_w_pp, d_pen_p, d_fin_p, loss_p) = _tail(
        x2, y_conv, y_hgrn, p2, tg2, w_out_full, w_pg_full, w_pp_all, pe_norm_g, fin_g)

    dz, d_on_p, d_lb_p, r_w_out, r_w_pg, r_w_pp = _hgrn_bwd(
        dy_hgrn, z, o_raw, states, lb_logits, onorm_g,
        [d_w_out.reshape(N_DEV, 2 * D // N_DEV, D), d_w_pg.reshape(N_DEV, D // N_DEV, D), d_w_pp])
    (d_w_in_hi,) = _inproj_bwd_dw("inproj_bwd_dw_hi", u, dz, N_DEV // 2, N_DEV // 2)
    dz, d_w_pw2, d_conv_w, d_b2_p, d_cng_p, d_cnb_p, d_cb_p, r_w_in_hi = _conv_bwd(
        dy_conv, z, yc, y2, conv_w_all, cnorm_g, cnorm_b, w_pw2_full, dz, [d_w_in_hi], [1])
    d_w_in_lo, r_w_pw2, r_conv_w = _inproj_bwd_dw(
        "inproj_bwd_dw_lo", u, dz, 0, N_DEV // 2,
        [d_w_pw2.reshape(N_DEV, D // N_DEV, D), d_conv_w], ["scatter", "scatter"])
    chip_lo = _pair_reduce("pair_reduce_lo", d_w_in_lo)
    grad_x, d_ln_p, r_w_in_lo = _inproj_bwd_dx(
        dz, x2, d_h1, ln_g, w_in_all, [chip_lo], [("chip", 0)])

    small = _pack_small({R_LN: d_ln_p, R_CONVB: d_cb_p, R_CNG: d_cng_p, R_CNB: d_cnb_p,
                         R_BPW2: d_b2_p, R_LB0: d_lb_p, R_ON: d_on_p, R_PEN: d_pen_p,
                         R_FIN: d_fin_p, R_LOSS: loss_p})
    (small_all,) = _exchange_call("gather_small", [small], ["gather"])

    big = {}
    big["w_in"] = _sum_adam("adam_w_in", [r_w_in_lo, r_w_in_hi], w_in[0], m_w_in[0], v_w_in[0], 128)
    cw = _sum_adam("adam_conv_w", [r_conv_w], _pad_rows(conv_w[0], CONV_PAD),
                   _pad_rows(m_conv_w[0], CONV_PAD), _pad_rows(v_conv_w[0], CONV_PAD), CONV_PAD)
    big["conv_w"] = [a[:CONV_K] for a in cw]
    big["w_pw2"] = _sum_adam("adam_w_pw2", [r_w_pw2], w_pw2[0], m_w_pw2[0], v_w_pw2[0], 128)
    big["w_out"] = _sum_adam("adam_w_out", [r_w_out], w_out[0], m_w_out[0], v_w_out[0], 128)
    big["w_pg"] = _sum_adam("adam_w_pg", [r_w_pg], w_pg[0], m_w_pg[0], v_w_pg[0], 128)
    big["w_pp"] = _sum_adam("adam_w_pp", [r_w_pp], w_pp[0], m_w_pp[0], v_w_pp[0], PLE)

    small_w = [ln_g, conv_b, cnorm_g, cnorm_b, b_pw2, lb_logits, onorm_g, pe_norm_g, final_g]
    small_m = [m_ln_g, m_conv_b, m_cnorm_g, m_cnorm_b, m_b_pw2, m_lb_logits, m_onorm_g,
               m_pe_norm_g, m_final_g]
    small_v = [v_ln_g, v_conv_b, v_cnorm_g, v_cnorm_b, v_b_pw2, v_lb_logits, v_onorm_g,
               v_pe_norm_g, v_final_g]
    sg, sd, sm, sv, loss = _small_adam(small_all, lb_logits, _pack_rows(small_w),
                                       _pack_rows(small_m), _pack_rows(small_v))

    small_rows = {"ln_g": (R_LN, 1), "conv_b": (R_CONVB, 1), "cnorm_g": (R_CNG, 1),
                  "cnorm_b": (R_CNB, 1), "b_pw2": (R_BPW2, 1), "lb_logits": (R_LB0, 2),
                  "onorm_g": (R_ON, 1), "pe_norm_g": (R_PEN, 1), "final_g": (R_FIN, 1)}
    order = ["ln_g", "w_in", "conv_w", "conv_b", "cnorm_g", "cnorm_b", "w_pw2", "b_pw2",
             "lb_logits", "onorm_g", "w_out", "pe_norm_g", "w_pg", "w_pp", "final_g"]

    def leaf(kind, name):
        if name in big:
            return big[name][kind][None]
        r0, n = small_rows[name]
        a = (sg, sd, sm, sv)[kind][r0:r0 + n]
        return a.reshape(D) if name == "final_g" else a

    outs = [loss.reshape(()), grad_x.reshape(1, t, D)]
    for kind in range(4):
        outs += [leaf(kind, name) for name in order]
    return tuple(outs)
```

```python
import functools

import jax
import jax.numpy as jnp
from jax import lax
from jax.experimental import pallas as pl
from jax.experimental.pallas import tpu as pltpu

F32 = jnp.float32
BF16 = jnp.bfloat16
MESH = pl.DeviceIdType.MESH

N_DEV = 8
D = 1024
N_COLS = 7 * D
COLS_PER_DEV = N_COLS // N_DEV
PLE = 256
HEAD = 128
N_HEADS = D // HEAD
CONV_K = 31
CONV_PAD = 32
CHUNK = 64
EPS = 1e-6
SUBLANES = 8

ADAM_LR = 0.001
ADAM_B1 = 0.9
ADAM_B2 = 0.999
ADAM_EPS = 1e-08
ADAM_WD = 0.01
ADAM_STEP = 10

MIB = 1024 * 1024
N_SMALL = 16
R_LN, R_CONVB, R_CNG, R_CNB, R_BPW2, R_LB0, R_LB1, R_ON, R_PEN, R_FIN, R_LOSS = range(11)


def _params(vmem_mib, **kw):
    return pltpu.CompilerParams(vmem_limit_bytes=vmem_mib * MIB, **kw)


def _dot(a, b):
    return jnp.dot(a.astype(BF16), b.astype(BF16), preferred_element_type=F32)


def _dot_nt(a, b):
    return lax.dot_general(a.astype(BF16), b.astype(BF16), (((1,), (1,)), ((), ())),
                           preferred_element_type=F32)


def _dot_tn(a, b):
    return lax.dot_general(a.astype(BF16), b.astype(BF16), (((0,), (0,)), ((), ())),
                           preferred_element_type=F32)


def _split(a):
    hi = a.astype(BF16)
    return hi, (a - hi.astype(F32)).astype(BF16)


def _dot_split(a, b, dims):
    ah, al = _split(a)
    bh, bl = _split(b)
    dg = lambda p, q: lax.dot_general(p, q, dims, preferred_element_type=F32)
    return dg(ah, bh) + (dg(ah, bl) + dg(al, bh))


def _sigmoid(x):
    return 1.0 / (1.0 + jnp.exp(-x))


def _rowsum8(a):
    r, c = a.shape
    return jnp.sum(a.reshape(r // SUBLANES, SUBLANES, c), axis=0)


def _tri_dot(tri, a):
    hi = a.astype(BF16)
    r1 = a - hi.astype(F32)
    mid = r1.astype(BF16)
    lo = (r1 - mid.astype(F32)).astype(BF16)
    return (jnp.dot(tri, hi, preferred_element_type=F32)
            + jnp.dot(tri, mid, preferred_element_type=F32)
            + jnp.dot(tri, lo, preferred_element_type=F32))


def _lower_bound(lbl):
    l0, l1 = lbl[0:1, :], lbl[1:2, :]
    m = jnp.maximum(l0, l1)
    e0, e1 = jnp.exp(l0 - m), jnp.exp(l1 - m)
    s = e0 + e1
    return e0 / s, e1 / s


ANY = pl.BlockSpec(memory_space=pl.ANY)


def _full(shape):
    return pl.BlockSpec(shape, lambda i: (0,) * len(shape))


def _peer(x, y, c, k):
    px = 1 - x if k & 4 else x
    py = 1 - y if k & 2 else y
    pc = 1 - c if k & 1 else c
    return (px, py, pc), 4 * px + 2 * py + pc


class _Exchange:
    def __init__(self, srcs, outs, modes, send_sems, recv_sems, local_sems):
        x, y, c = lax.axis_index("x"), lax.axis_index("y"), lax.axis_index("c")
        me = 4 * x + 2 * y + c
        self.starts, self.send_waits, self.recv_waits = [], [], []

        def remote(a, k, src, slot, peer, when):
            sem = a * N_DEV + k
            cp = pltpu.make_async_remote_copy(
                src_ref=src, dst_ref=outs[a].at[slot], send_sem=send_sems.at[sem],
                recv_sem=recv_sems.at[sem], device_id=peer, device_id_type=MESH)
            self.starts.append((when, cp.start))
            self.send_waits.append((when, cp.wait_send))

        def arrival(a, k, slot, when):
            sem = a * N_DEV + k
            cp = pltpu.make_async_remote_copy(
                src_ref=outs[a].at[slot], dst_ref=outs[a].at[slot], send_sem=send_sems.at[sem],
                recv_sem=recv_sems.at[sem], device_id=(x, y, c), device_id_type=MESH)
            self.recv_waits.append((when, cp.wait_recv))

        def local(a, src, slot, when):
            cp = pltpu.make_async_copy(src, outs[a].at[slot], local_sems.at[a])
            self.starts.append((when, cp.start))
            self.send_waits.append((when, cp.wait))

        for a, (src, mode) in enumerate(zip(srcs, modes)):
            if mode in ("gather", "scatter"):
                local(a, src if mode == "gather" else src.at[me], me, None)
                for k in range(1, N_DEV):
                    peer, peer_idx = _peer(x, y, c, k)
                    remote(a, k, src if mode == "gather" else src.at[peer_idx], me, peer, None)
                    arrival(a, k, peer_idx, None)
                continue
            if isinstance(mode, tuple):
                here, away = x == mode[1], x != mode[1]
                chip = 2 * x + y
                local(a, src.at[y], chip, here)
                remote(a, 1, src.at[1 - y], chip, (x, 1 - y, c), here)
                remote(a, 2, src.at[y], chip, (1 - x, y, c), away)
                remote(a, 3, src.at[1 - y], chip, (1 - x, 1 - y, c), away)
                arrival(a, 1, 2 * x + 1 - y, here)
                arrival(a, 2, 2 * (1 - x) + y, here)
                arrival(a, 3, 2 * (1 - x) + 1 - y, here)
                continue
            here, away = x == mode, x != mode
            for kk in range(4):
                py = 1 - y if kk & 2 else y
                pc = 1 - c if kk & 1 else c
                block = src.at[2 * py + pc]
                if kk == 0:
                    local(a, block, me, here)
                else:
                    remote(a, kk, block, me, (x, py, pc), here)
                remote(a, 4 + kk, block, me, (1 - x, py, pc), away)
            for k in range(1, N_DEV):
                arrival(a, k, _peer(x, y, c, k)[1], here)

    @staticmethod
    def _run(actions):
        for when, fn in actions:
            if when is None:
                fn()
            else:
                pl.when(when)(fn)

    def start(self):
        self._run(self.starts)

    def wait(self):
        self._run(self.recv_waits)
        self._run(self.send_waits)


def _exchange_scratch(n):
    return [pltpu.SemaphoreType.DMA((n * N_DEV,)), pltpu.SemaphoreType.DMA((n * N_DEV,)),
            pltpu.SemaphoreType.DMA((n,))]


def _recv_shapes(srcs, modes):
    def shape(s, m):
        if m == "gather":
            return (N_DEV,) + s.shape
        return (N_DEV // 2 if isinstance(m, tuple) else N_DEV,) + s.shape[1:]

    return [jax.ShapeDtypeStruct(shape(s, m), s.dtype) for s, m in zip(srcs, modes)]


def _pair_reduce(name, blocks):
    shape = (2,) + blocks.shape[1:]

    def body(src, out_ref, stage, mine, send_sems, recv_sems, local_sems):
        x, y, c = lax.axis_index("x"), lax.axis_index("y"), lax.axis_index("c")
        sends, waits = [], []
        for py in range(2):
            sends.append(pltpu.make_async_remote_copy(
                src_ref=src.at[2 * py + 1 - c], dst_ref=stage.at[py], send_sem=send_sems.at[py],
                recv_sem=recv_sems.at[py], device_id=(x, y, 1 - c), device_id_type=MESH))
            waits.append(pltpu.make_async_copy(src.at[2 * py + c], mine.at[py], local_sems.at[py]))
        for cp in sends + waits:
            cp.start()
        for cp in waits:
            cp.wait()
        for cp in sends:
            cp.wait_recv()
        out_ref[...] = mine[...] + stage[...]
        for cp in sends:
            cp.wait_send()

    return pl.pallas_call(
        body, name=name, out_shape=jax.ShapeDtypeStruct(shape, F32), in_specs=[ANY],
        scratch_shapes=[pltpu.VMEM(shape, F32), pltpu.VMEM(shape, F32),
                        pltpu.SemaphoreType.DMA((2,)), pltpu.SemaphoreType.DMA((2,)),
                        pltpu.SemaphoreType.DMA((2,))],
        compiler_params=_params(40),
    )(blocks)


def _exchange_call(name, srcs, modes):
    n = len(srcs)

    def body(*refs):
        xch = _Exchange(refs[:n], refs[n:2 * n], modes, *refs[2 * n:])
        xch.start()
        xch.wait()

    return pl.pallas_call(
        body, name=name, out_shape=_recv_shapes(srcs, modes),
        in_specs=[ANY] * n, out_specs=[ANY] * n, scratch_shapes=_exchange_scratch(n),
    )(*srcs)


def _hosted(body, n_in, n_out, grid, modes):
    n = len(modes)

    def hosted(*refs):
        ins, srcs = refs[:n_in], refs[n_in:n_in + n]
        outs = refs[n_in + n:n_in + n + n_out]
        bufs = refs[n_in + n + n_out:n_in + 2 * n + n_out]
        scratch = refs[n_in + 2 * n + n_out:-3]
        xch = _Exchange(srcs, bufs, modes, *refs[-3:])
        first, last = True, True
        for axis, size in enumerate(grid):
            first = jnp.logical_and(first, pl.program_id(axis) == 0)
            last = jnp.logical_and(last, pl.program_id(axis) == size - 1)
        pl.when(first)(xch.start)
        body(*ins, *outs, *scratch)
        pl.when(last)(xch.wait)

    return hosted


N_CHIPS = N_DEV // 2
PAIR_COLS = 2 * COLS_PER_DEV
PUSHED = (1, 2, 4, 6)
FORWARDED = (2, 4, 6)
NORM_ROWS = 32


def _inproj_fwd(x, ln_g, w_shard, other_shards):
    t = x.shape[0]
    tt = min(512, t)
    n_t = t // tt
    modes = ["gather"] * len(other_shards)
    n = len(modes)
    chip = 2 * lax.axis_index("x") + lax.axis_index("y")
    order = jnp.bitwise_xor(chip, jnp.arange(N_CHIPS, dtype=jnp.int32)).astype(jnp.int32)

    def body(order_ref, x_ref, g_ref, shard_hbm, *refs):
        srcs = refs[:n]
        z_ref, u_ref, w_all = refs[n:n + 3]
        bufs = refs[n + 3:2 * n + 3]
        u_all, w_blk, w_send, w_recv, w_local = refs[2 * n + 3:2 * n + 8]
        p, i = pl.program_id(0), pl.program_id(1)
        x, y, c = lax.axis_index("x"), lax.axis_index("y"), lax.axis_index("c")
        mine = 4 * x + 2 * y + c
        others = _Exchange(srcs, bufs, modes, *refs[2 * n + 8:])

        def push(k):
            peer, _ = _peer(x, y, c, k)
            return pltpu.make_async_remote_copy(
                src_ref=shard_hbm, dst_ref=w_all.at[mine], send_sem=w_send.at[k],
                recv_sem=w_recv.at[k], device_id=peer, device_id_type=MESH)

        def forward(k):
            _, owner = _peer(x, y, c, k)
            return pltpu.make_async_remote_copy(
                src_ref=w_all.at[owner], dst_ref=w_all.at[owner], send_sem=w_send.at[k + 1],
                recv_sem=w_recv.at[k + 1], device_id=(x, y, 1 - c), device_id_type=MESH)

        def landed(k):
            _, owner = _peer(x, y, c, k)
            return pltpu.make_async_remote_copy(
                src_ref=w_all.at[owner], dst_ref=w_all.at[owner], send_sem=w_send.at[k],
                recv_sem=w_recv.at[k], device_id=(x, y, c), device_id_type=MESH)

        keep = pltpu.make_async_copy(shard_hbm, w_all.at[mine], w_local.at[0])

        def load_pair(step):
            same = shard_hbm if step == 0 else w_all.at[_peer(x, y, c, 2 * step)[1]]
            other = w_all.at[_peer(x, y, c, 2 * step + 1)[1]]
            for side in range(2):
                @pl.when(c == side)
                def _(side=side):
                    pltpu.sync_copy(same, w_blk.at[:, pl.ds(COLS_PER_DEV * side, COLS_PER_DEV)])
                    pltpu.sync_copy(
                        other, w_blk.at[:, pl.ds(COLS_PER_DEV * (1 - side), COLS_PER_DEV)])

        @pl.when(jnp.logical_and(p == 0, i == 0))
        def _():
            for k in PUSHED:
                push(k).start()
            keep.start()
            others.start()

        for step in range(N_CHIPS):
            @pl.when(jnp.logical_and(p == step, i == 0))
            def _(step=step):
                landed(2 * step + 1).wait_recv()
                load_pair(step)

        rows = pl.ds(pl.multiple_of(i * tt, tt), tt)

        @pl.when(p == 0)
        def _():
            def norm_rows(r, carry):
                sub = pl.ds(pl.multiple_of(r * NORM_ROWS, NORM_ROWS), NORM_ROWS)
                xv = x_ref[sub, :]
                rstd = lax.rsqrt(jnp.mean(xv * xv, axis=-1, keepdims=True) + EPS)
                ub = (xv * rstd * g_ref[...]).astype(BF16)
                u_ref[sub, :] = ub
                u_all[pl.ds(pl.multiple_of(i * tt + r * NORM_ROWS, NORM_ROWS), NORM_ROWS), :] = ub
                return carry

            lax.fori_loop(0, tt // NORM_ROWS, norm_rows, 0, unroll=2)

        z_ref[...] = jnp.dot(u_all[rows, :], w_blk[...], preferred_element_type=F32)

        for step in range(1, N_CHIPS):
            @pl.when(jnp.logical_and(p == step - 1, i == n_t - 1))
            def _(step=step):
                landed(2 * step).wait_recv()
                forward(2 * step).start()

        @pl.when(jnp.logical_and(p == N_CHIPS - 1, i == n_t - 1))
        def _():
            for k in PUSHED:
                push(k).wait_send()
            for k in FORWARDED:
                forward(k).wait_send()
            keep.wait()
            others.wait()

    first_pass = lambda p, i, order_ref: (jnp.where(p == 0, i, n_t - 1), 0)
    grid_spec = pltpu.PrefetchScalarGridSpec(
        num_scalar_prefetch=1, grid=(N_CHIPS, n_t),
        in_specs=[pl.BlockSpec((tt, D), first_pass),
                  pl.BlockSpec((1, D), lambda p, i, order_ref: (0, 0)), ANY] + [ANY] * n,
        out_specs=[pl.BlockSpec((tt, PAIR_COLS), lambda p, i, order_ref: (i, order_ref[p])),
                   pl.BlockSpec((tt, D), first_pass), ANY] + [ANY] * n,
        scratch_shapes=[pltpu.VMEM((t, D), BF16), pltpu.VMEM((D, PAIR_COLS), BF16),
                        pltpu.SemaphoreType.DMA((N_DEV,)), pltpu.SemaphoreType.DMA((N_DEV,)),
                        pltpu.SemaphoreType.DMA((1,))] + _exchange_scratch(n))
    return pl.pallas_call(
        body, name="inproj_fwd", grid_spec=grid_spec,
        out_shape=[jax.ShapeDtypeStruct((t, N_COLS), F32), jax.ShapeDtypeStruct((t, D), BF16),
                   jax.ShapeDtypeStruct((N_DEV,) + w_shard.shape, BF16)]
        + _recv_shapes(other_shards, modes),
        compiler_params=_params(48, dimension_semantics=("arbitrary", "arbitrary")),
    )(order, x, ln_g, w_shard, *other_shards)


def _shifted_copies(buf, shifted, rows):
    for b in range(1, SUBLANES):
        shifted[b, 0:rows, :] = buf[b:b + rows, :]


def _tap_ref(buf, shifted, offset):
    a, b = divmod(offset, SUBLANES)
    return (buf if b == 0 else shifted.at[b]), SUBLANES * a


def _tap_slabs(buf, shifted, offset_of_tap):
    groups = {}
    for k in range(CONV_K):
        a, b = divmod(offset_of_tap(k), SUBLANES)
        groups.setdefault(b, []).append((SUBLANES * a, k))
    out = []
    for b, taps in sorted(groups.items()):
        taps.sort()
        lo = taps[0][0]
        out.append((buf if b == 0 else shifted.at[b], lo, [(k, off - lo) for off, k in taps]))
    return out


def _group_norm_stats(blk):
    mu = jnp.mean(blk, axis=-1, keepdims=True)
    cen = blk - mu
    var = jnp.mean(cen * cen, axis=-1, keepdims=True)
    return cen * lax.rsqrt(var + EPS)


def _conv_fwd(z, conv_w_all, conv_b, cn_g, cn_b, w_pw2, b_pw2):
    t = z.shape[0]
    tt = min(256, t)
    rc = 128

    def body(val_ref, glu_ref, gate_ref, cw_ref, cb_ref, g_ref, b_ref, w_hbm, b2_ref,
             yc_ref, y2_ref, yo_ref, w_vmem, vbuf, vsh, y1buf):
        @pl.when(pl.program_id(0) == 0)
        def _():
            pltpu.sync_copy(w_hbm, w_vmem)
            vbuf[0:CONV_PAD, :] = jnp.zeros((CONV_PAD, D), F32)

        vbuf[CONV_PAD:CONV_PAD + tt, :] = val_ref[...] * _sigmoid(glu_ref[...])
        _shifted_copies(vbuf, vsh, tt + 24)

        for g in range(N_HEADS):
            cs = slice(HEAD * g, HEAD * (g + 1))

            def row_chunk(r, carry, g=g, cs=cs):
                r0 = pl.multiple_of(r * rc, rc)
                acc = jnp.broadcast_to(cb_ref[:, cs], (rc, HEAD))
                for ref, lo, taps in _tap_slabs(vbuf, vsh, lambda k: k + 2):
                    slab = ref[pl.ds(r0 + lo, rc + taps[-1][1]), cs]
                    for k, off in taps:
                        acc = acc + cw_ref[g, k:k + 1, :] * slab[off:off + rc]
                yc_ref[pl.ds(r0, rc), cs] = acc
                n = _group_norm_stats(acc) * g_ref[:, cs] + b_ref[:, cs]
                y1buf[pl.ds(r0, rc), cs] = (n * _sigmoid(n)).astype(BF16)
                return carry

            lax.fori_loop(0, tt // rc, row_chunk, 0, unroll=True)
        vbuf[0:CONV_PAD, :] = vbuf[tt:tt + CONV_PAD, :]
        y2 = jnp.dot(y1buf[...], w_vmem[...], preferred_element_type=F32) + b2_ref[...]
        y2_ref[...] = y2
        gate = gate_ref[...]
        yo_ref[...] = (y2 * gate * _sigmoid(gate)).astype(BF16)

    col = lambda j: pl.BlockSpec((tt, D), lambda i: (i, j))
    row = pl.BlockSpec((tt, D), lambda i: (i, 0))
    return pl.pallas_call(
        body, name="conv_fwd", grid=(t // tt,),
        out_shape=[jax.ShapeDtypeStruct((t, D), F32), jax.ShapeDtypeStruct((t, D), F32),
                   jax.ShapeDtypeStruct((t, D), BF16)],
        in_specs=[col(0), col(1), col(2), _full((N_DEV, CONV_PAD, HEAD)), _full((1, D)),
                  _full((1, D)), _full((1, D)), ANY, _full((1, D))],
        out_specs=[row, row, row],
        scratch_shapes=[pltpu.VMEM((D, D), BF16), pltpu.VMEM((tt + CONV_PAD, D), F32),
                        pltpu.VMEM((SUBLANES, tt + CONV_PAD, D), F32), pltpu.VMEM((tt, D), BF16)],
        compiler_params=_params(48, dimension_semantics=("arbitrary",)),
    )(z, z, z, conv_w_all, conv_b, cn_g, cn_b, w_pw2, b_pw2)


HEAD_GROUP = 8
_HEAD_LANES = [slice(HEAD * j, HEAD * (j + 1)) for j in range(HEAD_GROUP)]


def _head_mean(a):
    return jnp.concatenate(
        [jnp.broadcast_to(jnp.mean(a[:, hs], axis=-1, keepdims=True), (a.shape[0], HEAD))
         for hs in _HEAD_LANES], axis=1)


def _chunk_quantities(zq, zf, lbh, tri):
    sig = _sigmoid(zf)
    sig_neg = _sigmoid(-zf)
    f = lbh + (1.0 - lbh) * sig
    k = (1.0 - lbh) * sig_neg
    q = zq * _sigmoid(zq)
    b = _tri_dot(tri, jnp.log(f))
    b_mid = b[CHUNK // 2 - 1:CHUNK // 2, :]
    b_last = b[CHUNK - 1:CHUNK, :]
    e_q = jnp.exp(b)
    e_qm = jnp.exp(b - b_mid)
    e_km = jnp.exp(b_mid - b)
    e_kd = jnp.exp(b_last - b)
    return q, k, f, sig, sig_neg, e_q, e_qm, e_km, e_kd, jnp.exp(b_last)


def _hgrn_fwd(z, lb_logits, onorm_g, shards):
    t = z.shape[0]
    tt = min(256, t)
    nc = tt // CHUNK
    modes = ["gather"] * len(shards)
    n = len(modes)

    def body(q_ref, f_ref, i_ref, g_ref, lbl_ref, on_ref, o_ref, y_ref, s_ref, st):
        @pl.when(pl.program_id(0) == 0)
        def _():
            st[...] = jnp.zeros_like(st)

        lb, _ = _lower_bound(lbl_ref[...])
        rows = lax.broadcasted_iota(jnp.int32, (CHUNK, CHUNK), 0)
        cols = lax.broadcasted_iota(jnp.int32, (CHUNK, CHUNK), 1)
        causal = rows >= cols
        tri = causal.astype(BF16)

        def chunk(c, carry):
            r0 = pl.multiple_of(c * CHUNK, CHUNK)
            rs = pl.ds(r0, CHUNK)
            for h0 in range(0, N_HEADS, HEAD_GROUP):
                cs = slice(HEAD * h0, HEAD * (h0 + HEAD_GROUP))
                q, k, _, _, _, e_q, e_qm, e_km, e_kd, e_last = _chunk_quantities(
                    q_ref[rs, cs], f_ref[rs, cs], lb[:, cs], tri)
                v = i_ref[rs, cs].astype(BF16)
                qm, km = (q * e_qm).astype(BF16), (k * e_km).astype(BF16)
                qt, kd = (q * e_q).astype(BF16), (k * e_kd).astype(BF16)
                outs = []
                for j, hs in enumerate(_HEAD_LANES):
                    s_old = st[h0 + j]
                    s_ref[c, h0 + j] = s_old
                    a = jnp.where(causal, _dot_nt(qm[:, hs], km[:, hs]), 0.0)
                    outs.append(_dot_nt(qt[:, hs], s_old) + _dot(a, v[:, hs]))
                    st[h0 + j] = s_old * e_last[:, hs] + _dot_tn(v[:, hs], kd[:, hs])
                o = jnp.concatenate(outs, axis=1)
                o_ref[rs, cs] = o
                n = o * lax.rsqrt(_head_mean(o * o) + EPS)
                zg = g_ref[rs, cs]
                y_ref[rs, cs] = (n * on_ref[:, cs] * zg * _sigmoid(zg)).astype(BF16)
            return carry

        lax.fori_loop(0, nc, chunk, 0, unroll=True)

    col = lambda j: pl.BlockSpec((tt, D), lambda i: (i, j))
    row = pl.BlockSpec((tt, D), lambda i: (i, 0))
    return pl.pallas_call(
        _hosted(body, 6, 3, (t // tt,), modes), name="hgrn_fwd", grid=(t // tt,),
        out_shape=[jax.ShapeDtypeStruct((t, D), F32), jax.ShapeDtypeStruct((t, D), BF16),
                   jax.ShapeDtypeStruct((t // CHUNK, N_HEADS, HEAD, HEAD), F32)]
        + _recv_shapes(shards, modes),
        in_specs=[col(3), col(4), col(5), col(6), _full((2, D)), _full((1, D))] + [ANY] * n,
        out_specs=[row, row, pl.BlockSpec((nc, N_HEADS, HEAD, HEAD), lambda i: (i, 0, 0, 0))]
        + [ANY] * n,
        scratch_shapes=[pltpu.VMEM((N_HEADS, HEAD, HEAD), F32)] + _exchange_scratch(n),
        compiler_params=_params(40, dimension_semantics=("arbitrary",)),
    )(z, z, z, z, lb_logits, onorm_g, *shards)


def _rms_bwd(dn, xhat, rstd):
    return rstd * (dn - xhat * jnp.mean(dn * xhat, axis=-1, keepdims=True))


def _tail(x, y_conv, y_hgrn, p, target, w_out, w_pg, w_pp_all, pe_g, fin_g):
    t = x.shape[0]
    tt = min(256, t)
    n_steps = t // tt

    def body(x_ref, yc_ref, yh_ref, p_ref, tg_ref, wo_hbm, wg_hbm, wp_hbm, pg_ref, fg_ref,
             dh1_ref, dyc_ref, dyh_ref, dwo_hbm, dwg_hbm, dwp_hbm, dpg_ref, dfg_ref, loss_ref,
             wo, wg, wp, dwo, dwg, dwp):
        i = pl.program_id(0)

        @pl.when(i == 0)
        def _():
            pltpu.sync_copy(wo_hbm, wo)
            pltpu.sync_copy(wg_hbm, wg)
            for d in range(N_DEV):
                pltpu.sync_copy(wp_hbm.at[d], wp.at[:, pl.ds(HEAD * d, HEAD)])
            dwo[...] = jnp.zeros_like(dwo)
            dwg[...] = jnp.zeros_like(dwg)
            dwp[...] = jnp.zeros_like(dwp)
            dpg_ref[...] = jnp.zeros_like(dpg_ref)
            dfg_ref[...] = jnp.zeros_like(dfg_ref)
            loss_ref[...] = jnp.zeros_like(loss_ref)

        ycv, yhv = yc_ref[...], yh_ref[...]
        h1 = (x_ref[...] + jnp.dot(ycv, wo[0:D, :], preferred_element_type=F32)
              + jnp.dot(yhv, wo[D:2 * D, :], preferred_element_type=F32))
        pb = p_ref[...].astype(BF16)
        pe = jnp.dot(pb, wp[...], preferred_element_type=F32)
        rstd1 = lax.rsqrt(jnp.mean(h1 * h1, axis=-1, keepdims=True) + EPS)
        n1 = h1 * rstd1
        rb = (n1 * pg_ref[...]).astype(BF16)
        gate = _sigmoid(jnp.dot(rb, wg[...], preferred_element_type=F32))
        h2 = h1 + gate * pe
        rstd2 = lax.rsqrt(jnp.mean(h2 * h2, axis=-1, keepdims=True) + EPS)
        n2 = h2 * rstd2
        err = n2 * fg_ref[...] - tg_ref[...]
        loss_ref[...] += _rowsum8(err * err)

        d_out = err * (1.0 / D)
        dfg_ref[...] += _rowsum8(d_out * n2)
        d_h2 = _rms_bwd(d_out * fg_ref[...], n2, rstd2)
        d_pe = (d_h2 * gate).astype(BF16)
        d_gpre = (d_h2 * pe * gate * (1.0 - gate)).astype(BF16)
        dwg[...] += _dot_tn(rb, d_gpre)
        dwp[...] += _dot_tn(pb, d_pe)
        dr = _dot_nt(d_gpre, wg[...])
        dpg_ref[...] += _rowsum8(dr * n1)
        d_h1 = d_h2 + _rms_bwd(dr * pg_ref[...], n1, rstd1)
        dh1_ref[...] = d_h1
        d_h1b = d_h1.astype(BF16)
        dwo[0:D, :] += _dot_tn(ycv, d_h1b)
        dwo[D:2 * D, :] += _dot_tn(yhv, d_h1b)
        dyc_ref[...] = _dot_nt(d_h1b, wo[0:D, :])
        dyh_ref[...] = _dot_nt(d_h1b, wo[D:2 * D, :])

        @pl.when(i == n_steps - 1)
        def _():
            pltpu.sync_copy(dwo, dwo_hbm)
            pltpu.sync_copy(dwg, dwg_hbm)
            for d in range(N_DEV):
                pltpu.sync_copy(dwp.at[:, pl.ds(HEAD * d, HEAD)], dwp_hbm.at[d])

    row = pl.BlockSpec((tt, D), lambda i: (i, 0))
    acc = _full((SUBLANES, D))
    return pl.pallas_call(
        body, name="tail_fwd_bwd", grid=(n_steps,),
        out_shape=[jax.ShapeDtypeStruct((t, D), F32)] * 3
        + [jax.ShapeDtypeStruct((2 * D, D), F32), jax.ShapeDtypeStruct((D, D), F32),
           jax.ShapeDtypeStruct((N_DEV, PLE, HEAD), F32)]
        + [jax.ShapeDtypeStruct((SUBLANES, D), F32)] * 3,
        in_specs=[row, row, row, pl.BlockSpec((tt, PLE), lambda i: (i, 0)), row,
                  ANY, ANY, ANY, _full((1, D)), _full((1, D))],
        out_specs=[row, row, row, ANY, ANY, ANY, acc, acc, acc],
        scratch_shapes=[pltpu.VMEM((2 * D, D), BF16), pltpu.VMEM((D, D), BF16),
                        pltpu.VMEM((PLE, D), BF16), pltpu.VMEM((2 * D, D), F32),
                        pltpu.VMEM((D, D), F32), pltpu.VMEM((PLE, D), F32)],
        compiler_params=_params(52, dimension_semantics=("arbitrary",)),
    )(x, y_conv, y_hgrn, p, target, w_out, w_pg, w_pp_all, pe_g, fin_g)


def _hgrn_bwd(dy, z, o_raw, states, lb_logits, onorm_g, grads):
    t = z.shape[0]
    tt = min(256, t)
    nc = tt // CHUNK
    n_steps = t // tt
    modes = ["scatter"] * len(grads)

    def body(dy_ref, q_ref, f_ref, i_ref, g_ref, o_ref, s_ref, lbl_ref, on_ref,
             dz_ref, don_ref, dlb_ref, dst):
        @pl.when(pl.program_id(0) == 0)
        def _():
            dst[...] = jnp.zeros_like(dst)
            don_ref[...] = jnp.zeros_like(don_ref)
            dlb_ref[...] = jnp.zeros_like(dlb_ref)

        lb, _ = _lower_bound(lbl_ref[...])
        rows = lax.broadcasted_iota(jnp.int32, (CHUNK, CHUNK), 0)
        cols = lax.broadcasted_iota(jnp.int32, (CHUNK, CHUNK), 1)
        causal = rows >= cols
        tri = causal.astype(BF16)
        tri_rev = (rows <= cols).astype(BF16)
        width = HEAD * HEAD_GROUP
        is_last = lax.broadcasted_iota(jnp.int32, (CHUNK, width), 0) == CHUNK - 1
        nn = (((1,), (0,)), ((), ()))
        tn = (((0,), (0,)), ((), ()))
        dg = functools.partial(lax.dot_general, preferred_element_type=F32)

        def chunk(cc, carry):
            c = nc - 1 - cc
            r0 = pl.multiple_of(c * CHUNK, CHUNK)
            rs = pl.ds(r0, CHUNK)
            for h0 in range(0, N_HEADS, HEAD_GROUP):
                cs = slice(HEAD * h0, HEAD * h0 + width)
                zq, zf, zg = q_ref[rs, cs], f_ref[rs, cs], g_ref[rs, cs]
                lbh = lb[:, cs]
                q, k, f, sig, sig_neg, e_q, e_qm, e_km, e_kd, e_last = _chunk_quantities(
                    zq, zf, lbh, tri)
                vb = i_ref[rs, cs].astype(BF16)
                qt, qm, km, kd = q * e_q, q * e_qm, k * e_km, k * e_kd
                qt_b, kd_b = qt.astype(BF16), kd.astype(BF16)
                qm_h, qm_l = _split(qm)
                km_h, km_l = _split(km)

                o = o_ref[rs, cs]
                rstd = lax.rsqrt(_head_mean(o * o) + EPS)
                n = o * rstd
                sg = _sigmoid(zg)
                dyv = dy_ref[rs, cs]
                on = on_ref[:, cs]
                d_zg = dyv * n * on * sg * (1.0 + zg * (1.0 - sg))
                d_on = dyv * zg * sg
                don_ref[:, cs] += _rowsum8(d_on * n)
                dn = d_on * on
                do_b = (rstd * (dn - n * _head_mean(dn * n))).astype(BF16)

                dv, dkd, dqt, dqm, dkm, s_dots = [], [], [], [], [], []
                for j, hs in enumerate(_HEAD_LANES):
                    s_old, ds_new = s_ref[c, h0 + j], dst[h0 + j]
                    ds_b = ds_new.astype(BF16)
                    a = jnp.where(causal, _dot_nt(qm_h[:, hs], km_h[:, hs]), 0.0)
                    da = jnp.where(causal, _dot_nt(do_b[:, hs], vb[:, hs]), 0.0)
                    dv.append(_dot_tn(a, do_b[:, hs]) + _dot_nt(kd_b[:, hs], ds_b))
                    dkd.append(_dot(vb[:, hs], ds_b))
                    dqt.append(_dot(do_b[:, hs], s_old))
                    da_h, da_l = _split(da)
                    dqm.append(dg(da_h, km_h[:, hs], nn)
                               + (dg(da_h, km_l[:, hs], nn) + dg(da_l, km_h[:, hs], nn)))
                    dkm.append(dg(da_h, qm_h[:, hs], tn)
                               + (dg(da_h, qm_l[:, hs], tn) + dg(da_l, qm_h[:, hs], tn)))
                    dst[h0 + j] = ds_new * e_last[:, hs] + _dot_tn(do_b[:, hs], qt_b[:, hs])
                    s_dots.append(jnp.sum(s_old * ds_new, axis=0, keepdims=True))
                dv, dkd, dqt, dqm, dkm, s_dots = [
                    jnp.concatenate(parts, axis=1) for parts in (dv, dkd, dqt, dqm, dkm, s_dots)]
                dq = dqt * e_q + dqm * e_qm
                dk = dkm * e_km + dkd * e_kd
                last = jnp.sum(dkd * kd, axis=0, keepdims=True) + e_last * s_dots
                db = q * dq - k * dk + jnp.where(is_last, last, 0.0)
                dlogf = _tri_dot(tri_rev, db)
                common = sig_neg * (dlogf / f - dk)
                dlb_ref[:, cs] += _rowsum8(common)
                c0 = 3 * D + HEAD * h0
                sq = _sigmoid(zq)
                dz_ref[rs, c0:c0 + width] = (dq * sq * (1.0 + zq * (1.0 - sq))).astype(BF16)
                dz_ref[rs, D + c0:D + c0 + width] = ((1.0 - lbh) * sig * common).astype(BF16)
                dz_ref[rs, 2 * D + c0:2 * D + c0 + width] = dv.astype(BF16)
                dz_ref[rs, 3 * D + c0:3 * D + c0 + width] = d_zg.astype(BF16)
            return carry

        lax.fori_loop(0, nc, chunk, 0, unroll=True)

    rev = lambda i: n_steps - 1 - i
    col = lambda j: pl.BlockSpec((tt, D), lambda i: (rev(i), j))
    row = pl.BlockSpec((tt, D), lambda i: (rev(i), 0))
    acc = _full((SUBLANES, D))
    n = len(modes)
    return pl.pallas_call(
        _hosted(body, 9, 3, (n_steps,), modes), name="hgrn_bwd", grid=(n_steps,),
        out_shape=[jax.ShapeDtypeStruct((t, N_COLS), BF16),
                   jax.ShapeDtypeStruct((SUBLANES, D), F32),
                   jax.ShapeDtypeStruct((SUBLANES, D), F32)] + _recv_shapes(grads, modes),
        in_specs=[row, col(3), col(4), col(5), col(6), row,
                  pl.BlockSpec((nc, N_HEADS, HEAD, HEAD), lambda i: (rev(i), 0, 0, 0)),
                  _full((2, D)), _full((1, D))] + [ANY] * n,
        out_specs=[pl.BlockSpec((tt, N_COLS), lambda i: (rev(i), 0)), acc, acc] + [ANY] * n,
        scratch_shapes=[pltpu.VMEM((N_HEADS, HEAD, HEAD), F32)] + _exchange_scratch(n),
        compiler_params=_params(48, dimension_semantics=("arbitrary",)),
    )(dy, z, z, z, z, o_raw, states, lb_logits, onorm_g, *grads)


def _conv_bwd(dy, z, yc, y2, conv_w_all, cn_g, cn_b, w_pw2, dz, grads, modes):
    t = z.shape[0]
    tt = min(256, t)
    rc = 32
    n_steps = t // tt

    def body(dy_ref, val_ref, glu_ref, gate_ref, yc_ref, y2_ref, cw_ref, g_ref, b_ref, w_hbm,
             dz_in, dz_ref, dw_hbm, dcw_out, db2_ref, dg_ref, dbeta_ref, dcb_ref,
             w_vmem, dw, dbuf, dsh, y1buf, dnbuf, dcw_ref):
        i = pl.program_id(0)

        @pl.when(i == 0)
        def _():
            pltpu.sync_copy(w_hbm, w_vmem)
            dw[...] = jnp.zeros_like(dw)
            dbuf[tt:tt + CONV_PAD, :] = jnp.zeros((CONV_PAD, D), F32)
            dcw_ref[...] = jnp.zeros_like(dcw_ref)
            dcw_out[...] = jnp.zeros_like(dcw_out)
            db2_ref[...] = jnp.zeros_like(db2_ref)
            dg_ref[...] = jnp.zeros_like(dg_ref)
            dbeta_ref[...] = jnp.zeros_like(dbeta_ref)
            dcb_ref[...] = jnp.zeros_like(dcb_ref)

        gate = gate_ref[...]
        sg = _sigmoid(gate)
        dyv = dy_ref[...]
        dy2 = dyv * gate * sg
        dz_ref[:, 2 * D:3 * D] = (dyv * y2_ref[...] * sg * (1.0 + gate * (1.0 - sg))).astype(BF16)
        db2_ref[...] += _rowsum8(dy2)
        dy2b = dy2.astype(BF16)
        dnbuf[...] = _dot_nt(dy2b, w_vmem[...])

        def norm_chunk(r, carry):
            r0 = pl.multiple_of(r * rc, rc)
            rs = pl.ds(r0, rc)
            for g in range(N_HEADS):
                cs = slice(HEAD * g, HEAD * (g + 1))
                blk = yc_ref[rs, cs]
                mu = jnp.mean(blk, axis=-1, keepdims=True)
                cen = blk - mu
                rstd = lax.rsqrt(jnp.mean(cen * cen, axis=-1, keepdims=True) + EPS)
                xhat = cen * rstd
                n = xhat * g_ref[:, cs] + b_ref[:, cs]
                sn = _sigmoid(n)
                y1buf[rs, cs] = (n * sn).astype(BF16)
                dn = dnbuf[rs, cs] * sn * (1.0 + n * (1.0 - sn))
                dg_ref[:, cs] += _rowsum8(dn * xhat)
                dbeta_ref[:, cs] += _rowsum8(dn)
                dxh = dn * g_ref[:, cs]
                dyc = rstd * (dxh - jnp.mean(dxh, axis=-1, keepdims=True)
                              - xhat * jnp.mean(dxh * xhat, axis=-1, keepdims=True))
                dcb_ref[:, cs] += _rowsum8(dyc)
                dbuf[rs, cs] = dyc
            return carry

        lax.fori_loop(0, tt // rc, norm_chunk, 0, unroll=2)
        dw[...] += _dot_tn(y1buf[...], dy2b)
        _shifted_copies(dbuf, dsh, tt + 24)

        def conv_chunk(r, carry):
            r0 = pl.multiple_of(r * rc, rc)
            rs = pl.ds(r0, rc)
            for g in range(N_HEADS):
                cs = slice(HEAD * g, HEAD * (g + 1))
                sglu = _sigmoid(glu_ref[rs, cs])
                val = val_ref[rs, cs]
                v = val * sglu
                dv = jnp.zeros((rc, HEAD), F32)
                for ref, lo, taps in _tap_slabs(dbuf, dsh, lambda k: CONV_K - 1 - k):
                    slab = ref[pl.ds(r0 + lo, rc + taps[-1][1]), cs]
                    for k, off in taps:
                        d_later = slab[off:off + rc]
                        dv = dv + cw_ref[g, k:k + 1, :] * d_later
                        dcw_ref[g, k] += _rowsum8(v * d_later)
                dz_ref[rs, cs] = (dv * sglu).astype(BF16)
                dz_ref[rs, D + HEAD * g:D + HEAD * (g + 1)] = (
                    dv * val * sglu * (1.0 - sglu)).astype(BF16)
            return carry

        lax.fori_loop(0, tt // rc, conv_chunk, 0, unroll=2)
        dbuf[tt:tt + CONV_PAD, :] = dbuf[0:CONV_PAD, :]

        @pl.when(i == n_steps - 1)
        def _():
            pltpu.sync_copy(dw, dw_hbm)
            for g in range(N_HEADS):
                for k in range(CONV_K):
                    dcw_out[g, k:k + 1, :] = jnp.sum(dcw_ref[g, k], axis=0, keepdims=True)

    rev = lambda i: n_steps - 1 - i
    col = lambda j: pl.BlockSpec((tt, D), lambda i: (rev(i), j))
    row = pl.BlockSpec((tt, D), lambda i: (rev(i), 0))
    acc = _full((SUBLANES, D))
    n = len(modes)
    return pl.pallas_call(
        _hosted(body, 11, 7, (n_steps,), modes), name="conv_bwd", grid=(n_steps,),
        out_shape=[jax.ShapeDtypeStruct((t, N_COLS), BF16), jax.ShapeDtypeStruct((D, D), F32),
                   jax.ShapeDtypeStruct((N_DEV, CONV_PAD, HEAD), F32)]
        + [jax.ShapeDtypeStruct((SUBLANES, D), F32)] * 4 + _recv_shapes(grads, modes),
        in_specs=[row, col(0), col(1), col(2), row, row, _full((N_DEV, CONV_PAD, HEAD)),
                  _full((1, D)), _full((1, D)), ANY, ANY] + [ANY] * n,
        out_specs=[pl.BlockSpec((tt, 3 * D), lambda i: (rev(i), 0)), ANY,
                   _full((N_DEV, CONV_PAD, HEAD)), acc, acc, acc, acc] + [ANY] * n,
        input_output_aliases={10: 0},
        scratch_shapes=[pltpu.VMEM((D, D), BF16), pltpu.VMEM((D, D), F32),
                        pltpu.VMEM((tt + CONV_PAD, D), F32),
                        pltpu.VMEM((SUBLANES, tt + CONV_PAD, D), F32),
                        pltpu.VMEM((tt, D), BF16), pltpu.VMEM((tt, D), F32),
                        pltpu.VMEM((N_DEV, CONV_PAD, SUBLANES, HEAD), F32)] + _exchange_scratch(n),
        compiler_params=_params(52, dimension_semantics=("arbitrary",)),
    )(dy, z, z, z, yc, y2, conv_w_all, cn_g, cn_b, w_pw2, dz, *grads)


def _inproj_bwd_dx(dz, x, d_h1, ln_g, w_in_all, grads, modes):
    t = x.shape[0]
    tt = min(256, t)

    def body(dz_ref, x_ref, dh1_ref, g_ref, w_hbm, dx_ref, dg_ref, w_vmem, w_sems):
        first = pl.program_id(0) == 0
        loads = [pltpu.make_async_copy(
            w_hbm.at[d], w_vmem.at[d // 2, :, pl.ds(COLS_PER_DEV * (d % 2), COLS_PER_DEV)],
            w_sems.at[d]) for d in range(N_DEV)]

        @pl.when(first)
        def _():
            for cp in loads:
                cp.start()
            dg_ref[...] = jnp.zeros_like(dg_ref)

        du = jnp.zeros((tt, D), F32)
        for q in range(N_CHIPS):
            @pl.when(first)
            def _(q=q):
                loads[2 * q].wait()
                loads[2 * q + 1].wait()

            du = du + lax.dot_general(
                dz_ref[:, PAIR_COLS * q:PAIR_COLS * (q + 1)], w_vmem[q],
                (((1,), (1,)), ((), ())), preferred_element_type=F32)
        xv = x_ref[...]
        rstd = lax.rsqrt(jnp.mean(xv * xv, axis=-1, keepdims=True) + EPS)
        xhat = xv * rstd
        dg_ref[...] += _rowsum8(du * xhat)
        dx_ref[...] = dh1_ref[...] + _rms_bwd(du * g_ref[...], xhat, rstd)

    row = pl.BlockSpec((tt, D), lambda i: (i, 0))
    n = len(modes)
    return pl.pallas_call(
        _hosted(body, 5, 2, (t // tt,), modes), name="inproj_bwd_dx", grid=(t // tt,),
        out_shape=[jax.ShapeDtypeStruct((t, D), F32), jax.ShapeDtypeStruct((SUBLANES, D), F32)]
        + _recv_shapes(grads, modes),
        in_specs=[pl.BlockSpec((tt, N_COLS), lambda i: (i, 0)), row, row, _full((1, D)), ANY]
        + [ANY] * n,
        out_specs=[row, _full((SUBLANES, D))] + [ANY] * n,
        scratch_shapes=[pltpu.VMEM((N_CHIPS, D, PAIR_COLS), BF16),
                        pltpu.SemaphoreType.DMA((N_DEV,))] + _exchange_scratch(n),
        compiler_params=_params(48, dimension_semantics=("arbitrary",)),
    )(dz, x, d_h1, ln_g, w_in_all, *grads)


def _inproj_bwd_dw(name, u, dz, first, count, grads=(), modes=()):
    t = u.shape[0]
    tt = min(512, t)
    grid = (count // 2, t // tt)
    n = len(modes)

    def body(u_ref, dz_ref, dw_ref):
        @pl.when(pl.program_id(1) == 0)
        def _():
            dw_ref[...] = jnp.zeros_like(dw_ref)

        both = lax.dot_general(u_ref[...], dz_ref[...], (((0,), (0,)), ((), ())),
                               preferred_element_type=F32)
        dw_ref[0] += both[:, :COLS_PER_DEV]
        dw_ref[1] += both[:, COLS_PER_DEV:]

    return pl.pallas_call(
        _hosted(body, 2, 1, grid, modes) if n else body, name=name, grid=grid,
        out_shape=[jax.ShapeDtypeStruct((count, D, COLS_PER_DEV), F32)]
        + _recv_shapes(grads, modes),
        in_specs=[pl.BlockSpec((tt, D), lambda j, i: (i, 0)),
                  pl.BlockSpec((tt, PAIR_COLS), lambda j, i: (i, first // 2 + j))] + [ANY] * n,
        out_specs=[pl.BlockSpec((2, D, COLS_PER_DEV), lambda j, i: (j, 0, 0))] + [ANY] * n,
        scratch_shapes=_exchange_scratch(n) if n else [],
        compiler_params=_params(40, dimension_semantics=("arbitrary", "arbitrary")),
    )(u, dz, *grads)


def _adamw(w, g, m, v):
    m = ADAM_B1 * m + (1.0 - ADAM_B1) * g
    v = ADAM_B2 * v + (1.0 - ADAM_B2) * (g * g)
    m_hat = m / (1.0 - ADAM_B1 ** ADAM_STEP)
    v_hat = v / (1.0 - ADAM_B2 ** ADAM_STEP)
    delta = -ADAM_LR * (m_hat / (jnp.sqrt(v_hat) + ADAM_EPS) + ADAM_WD * w)
    return delta, m, v


def _pack_small(partials):
    rows = sorted(partials)

    def body(*refs):
        ins, out_ref = refs[:-1], refs[-1]
        out_ref[...] = jnp.zeros_like(out_ref)
        for j, row in enumerate(rows):
            out_ref[row:row + 1, :] = jnp.sum(ins[j][...], axis=0, keepdims=True)

    return pl.pallas_call(
        body, name="pack_small", out_shape=jax.ShapeDtypeStruct((N_SMALL, D), F32),
    )(*[partials[row] for row in rows])


def _sum_adam(name, recvs, w, m, v, rows):
    r, c = w.shape
    n = len(recvs)

    def body(*refs):
        w_ref, m_ref, v_ref, g_ref, d_ref, mo_ref, vo_ref = refs[n:]

        def finish(recv_ref):
            g = recv_ref[0]
            for s in range(1, recv_ref.shape[0]):
                g = g + recv_ref[s]
            g_ref[...] = g
            d_ref[...], mo_ref[...], vo_ref[...] = _adamw(w_ref[...], g, m_ref[...], v_ref[...])

        if n == 1:
            finish(refs[0])
        else:
            for side in range(n):
                pl.when(lax.axis_index("x") == side)(functools.partial(finish, refs[side]))

    blk = pl.BlockSpec((rows, c), lambda i: (i, 0))
    return pl.pallas_call(
        body, name=name, grid=(r // rows,),
        out_shape=[jax.ShapeDtypeStruct((r, c), F32)] * 4,
        in_specs=[pl.BlockSpec((rv.shape[0], rows, c), lambda i: (0, i, 0)) for rv in recvs]
        + [blk, blk, blk],
        out_specs=[blk] * 4,
        compiler_params=_params(48, dimension_semantics=("arbitrary",)),
    )(*recvs, w, m, v)


def _small_adam(gathered, lb_logits, w, m, v):
    def body(ga_ref, lbl_ref, w_ref, m_ref, v_ref, g_ref, d_ref, mo_ref, vo_ref, loss_ref):
        g = ga_ref[0]
        for s in range(1, N_DEV):
            g = g + ga_ref[s]
        s0, s1 = _lower_bound(lbl_ref[...])
        d_lb = g[R_LB0:R_LB0 + 1, :]
        rows = lax.broadcasted_iota(jnp.int32, (N_SMALL, D), 0)
        g = jnp.where(rows == R_LB0, d_lb * s0 * (1.0 - s0), g)
        g = jnp.where(rows == R_LB1, -d_lb * s0 * s1, g)
        g_ref[...] = g
        d_ref[...], mo_ref[...], vo_ref[...] = _adamw(w_ref[...], g, m_ref[...], v_ref[...])
        loss_ref[...] = (0.5 / D) * jnp.sum(g[R_LOSS:R_LOSS + 1, :], axis=-1, keepdims=True)

    return pl.pallas_call(
        body, name="small_adam",
        out_shape=[jax.ShapeDtypeStruct((N_SMALL, D), F32)] * 4 + [jax.ShapeDtypeStruct((1, 1), F32)],
    )(gathered, lb_logits, w, m, v)


def _pad_rows(a, rows):
    return jnp.pad(a, ((0, rows - a.shape[0]), (0, 0)))


def _pack_rows(rows):
    rows = [r.reshape(-1, D) for r in rows]
    packed = jnp.concatenate(rows, axis=0)
    return _pad_rows(packed, N_SMALL)


def kernel(x, p, ln_g, w_in, conv_w, conv_b, cnorm_g, cnorm_b, w_pw2, b_pw2, lb_logits, onorm_g, w_out, pe_norm_g, w_pg, w_pp, final_g, loss_target, m_ln_g, m_w_in, m_conv_w, m_conv_b, m_cnorm_g, m_cnorm_b, m_w_pw2, m_b_pw2, m_lb_logits, m_onorm_g, m_w_out, m_pe_norm_g, m_w_pg, m_w_pp, m_final_g, v_ln_g, v_w_in, v_conv_w, v_conv_b, v_cnorm_g, v_cnorm_b, v_w_pw2, v_b_pw2, v_lb_logits, v_onorm_g, v_w_out, v_pe_norm_g, v_w_pg, v_w_pp, v_final_g):
    t = x.shape[1]
    x2 = x.reshape(t, D)
    p2 = p.reshape(t, PLE)
    tg2 = loss_target.reshape(t, D)
    fin_g = final_g.reshape(1, D)

    z, u, w_in_all, conv_w_all, w_pw2_all = _inproj_fwd(
        x2, ln_g, w_in[0].astype(BF16), [_pad_rows(conv_w[0], CONV_PAD), w_pw2[0].astype(BF16)])
    o_raw, y_hgrn, states, w_out_all, w_pg_all, w_pp_all = _hgrn_fwd(
        z, lb_logits, onorm_g, [w_out[0].astype(BF16), w_pg[0].astype(BF16), w_pp[0].astype(BF16)])
    w_pw2_full = w_pw2_all.reshape(D, D)
    w_out_full = w_out_all.reshape(2 * D, D)
    w_pg_full = w_pg_all.reshape(D, D)
    yc, y2, y_conv = _conv_fwd(z, conv_w_all, conv_b, cnorm_g, cnorm_b, w_pw2_full, b_pw2)

    (d_h1, dy_conv, dy_hgrn, d_w_out, d_w_pg, d_w_pp, d_pen_p, d_fin_p, loss_p) = _tail(
        x2, y_conv, y_hgrn, p2, tg2, w_out_full, w_pg_full, w_pp_all, pe_norm_g, fin_g)

    dz, d_on_p, d_lb_p, r_w_out, r_w_pg, r_w_pp = _hgrn_bwd(
        dy_hgrn, z, o_raw, states, lb_logits, onorm_g,
        [d_w_out.reshape(N_DEV, 2 * D // N_DEV, D), d_w_pg.reshape(N_DEV, D // N_DEV, D), d_w_pp])
    (d_w_in_hi,) = _inproj_bwd_dw("inproj_bwd_dw_hi", u, dz, N_DEV // 2, N_DEV // 2)
    dz, d_w_pw2, d_conv_w, d_b2_p, d_cng_p, d_cnb_p, d_cb_p, r_w_in_hi = _conv_bwd(
        dy_conv, z, yc, y2, conv_w_all, cnorm_g, cnorm_b, w_pw2_full, dz, [d_w_in_hi], [1])
    d_w_in_lo, r_w_pw2, r_conv_w = _inproj_bwd_dw(
        "inproj_bwd_dw_lo", u, dz, 0, N_DEV // 2,
        [d_w_pw2.reshape(N_DEV, D // N_DEV, D), d_conv_w], ["scatter", "scatter"])
    chip_lo = _pair_reduce("pair_reduce_lo", d_w_in_lo)
    grad_x, d_ln_p, r_w_in_lo = _inproj_bwd_dx(
        dz, x2, d_h1, ln_g, w_in_all, [chip_lo], [("chip", 0)])

    small = _pack_small({R_LN: d_ln_p, R_CONVB: d_cb_p, R_CNG: d_cng_p, R_CNB: d_cnb_p,
                         R_BPW2: d_b2_p, R_LB0: d_lb_p, R_ON: d_on_p, R_PEN: d_pen_p,
                         R_FIN: d_fin_p, R_LOSS: loss_p})
    (small_all,) = _exchange_call("gather_small", [small], ["gather"])

    big = {}
    big["w_in"] = _sum_adam("adam_w_in", [r_w_in_lo, r_w_in_hi], w_in[0], m_w_in[0], v_w_in[0], 128)
    cw = _sum_adam("adam_conv_w", [r_conv_w], _pad_rows(conv_w[0], CONV_PAD),
                   _pad_rows(m_conv_w[0], CONV_PAD), _pad_rows(v_conv_w[0], CONV_PAD), CONV_PAD)
    big["conv_w"] = [a[:CONV_K] for a in cw]
    big["w_pw2"] = _sum_adam("adam_w_pw2", [r_w_pw2], w_pw2[0], m_w_pw2[0], v_w_pw2[0], 128)
    big["w_out"] = _sum_adam("adam_w_out", [r_w_out], w_out[0], m_w_out[0], v_w_out[0], 128)
    big["w_pg"] = _sum_adam("adam_w_pg", [r_w_pg], w_pg[0], m_w_pg[0], v_w_pg[0], 128)
    big["w_pp"] = _sum_adam("adam_w_pp", [r_w_pp], w_pp[0], m_w_pp[0], v_w_pp[0], PLE)

    small_w = [ln_g, conv_b, cnorm_g, cnorm_b, b_pw2, lb_logits, onorm_g, pe_norm_g, final_g]
    small_m = [m_ln_g, m_conv_b, m_cnorm_g, m_cnorm_b, m_b_pw2, m_lb_logits, m_onorm_g,
               m_pe_norm_g, m_final_g]
    small_v = [v_ln_g, v_conv_b, v_cnorm_g, v_cnorm_b, v_b_pw2, v_lb_logits, v_onorm_g,
               v_pe_norm_g, v_final_g]
    sg, sd, sm, sv, loss = _small_adam(small_all, lb_logits, _pack_rows(small_w),
                                       _pack_rows(small_m), _pack_rows(small_v))

    small_rows = {"ln_g": (R_LN, 1), "conv_b": (R_CONVB, 1), "cnorm_g": (R_CNG, 1),
                  "cnorm_b": (R_CNB, 1), "b_pw2": (R_BPW2, 1), "lb_logits": (R_LB0, 2),
                  "onorm_g": (R_ON, 1), "pe_norm_g": (R_PEN, 1), "final_g": (R_FIN, 1)}
    order = ["ln_g", "w_in", "conv_w", "conv_b", "cnorm_g", "cnorm_b", "w_pw2", "b_pw2",
             "lb_logits", "onorm_g", "w_out", "pe_norm_g", "w_pg", "w_pp", "final_g"]

    def leaf(kind, name):
        if name in big:
            return big[name][kind][None]
        r0, n = small_rows[name]
        a = (sg, sd, sm, sv)[kind][r0:r0 + n]
        return a.reshape(D) if name == "final_g" else a

    outs = [loss.reshape(()), grad_x.reshape(1, t, D)]
    for kind in range(4):
        outs += [leaf(kind, name) for name in order]
    return tuple(outs)
```

```python
import functools

import jax
import jax.numpy as jnp
from jax import lax
from jax.experimental import pallas as pl
from jax.experimental.pallas import tpu as pltpu

F32 = jnp.float32
BF16 = jnp.bfloat16
MESH = pl.DeviceIdType.MESH

N_DEV = 8
D = 1024
N_COLS = 7 * D
COLS_PER_DEV = N_COLS // N_DEV
PLE = 256
HEAD = 128
N_HEADS = D // HEAD
CONV_K = 31
CONV_PAD = 32
CHUNK = 64
EPS = 1e-6
SUBLANES = 8

ADAM_LR = 0.001
ADAM_B1 = 0.9
ADAM_B2 = 0.999
ADAM_EPS = 1e-08
ADAM_WD = 0.01
ADAM_STEP = 10

MIB = 1024 * 1024
N_SMALL = 16
R_LN, R_CONVB, R_CNG, R_CNB, R_BPW2, R_LB0, R_LB1, R_ON, R_PEN, R_FIN, R_LOSS = range(11)


def _params(vmem_mib, **kw):
    return pltpu.CompilerParams(vmem_limit_bytes=vmem_mib * MIB, **kw)


def _dot(a, b):
    return jnp.dot(a.astype(BF16), b.astype(BF16), preferred_element_type=F32)


def _dot_nt(a, b):
    return lax.dot_general(a.astype(BF16), b.astype(BF16), (((1,), (1,)), ((), ())),
                           preferred_element_type=F32)


def _dot_tn(a, b):
    return lax.dot_general(a.astype(BF16), b.astype(BF16), (((0,), (0,)), ((), ())),
                           preferred_element_type=F32)


def _split(a):
    hi = a.astype(BF16)
    return hi, (a - hi.astype(F32)).astype(BF16)


def _dot_split(a, b, dims):
    ah, al = _split(a)
    bh, bl = _split(b)
    dg = lambda p, q: lax.dot_general(p, q, dims, preferred_element_type=F32)
    return dg(ah, bh) + (dg(ah, bl) + dg(al, bh))


def _sigmoid(x):
    return 1.0 / (1.0 + jnp.exp(-x))


def _rowsum8(a):
    r, c = a.shape
    return jnp.sum(a.reshape(r // SUBLANES, SUBLANES, c), axis=0)


def _tri_dot(tri, a):
    hi = a.astype(BF16)
    r1 = a - hi.astype(F32)
    mid = r1.astype(BF16)
    lo = (r1 - mid.astype(F32)).astype(BF16)
    return (jnp.dot(tri, hi, preferred_element_type=F32)
            + jnp.dot(tri, mid, preferred_element_type=F32)
            + jnp.dot(tri, lo, preferred_element_type=F32))


def _lower_bound(lbl):
    l0, l1 = lbl[0:1, :], lbl[1:2, :]
    m = jnp.maximum(l0, l1)
    e0, e1 = jnp.exp(l0 - m), jnp.exp(l1 - m)
    s = e0 + e1
    return e0 / s, e1 / s


ANY = pl.BlockSpec(memory_space=pl.ANY)


def _full(shape):
    return pl.BlockSpec(shape, lambda i: (0,) * len(shape))


def _peer(x, y, c, k):
    px = 1 - x if k & 4 else x
    py = 1 - y if k & 2 else y
    pc = 1 - c if k & 1 else c
    return (px, py, pc), 4 * px + 2 * py + pc


class _Exchange:
    def __init__(self, srcs, outs, modes, send_sems, recv_sems, local_sems):
        x, y, c = lax.axis_index("x"), lax.axis_index("y"), lax.axis_index("c")
        me = 4 * x + 2 * y + c
        self.starts, self.send_waits, self.recv_waits = [], [], []

        def remote(a, k, src, slot, peer, when):
            sem = a * N_DEV + k
            cp = pltpu.make_async_remote_copy(
                src_ref=src, dst_ref=outs[a].at[slot], send_sem=send_sems.at[sem],
                recv_sem=recv_sems.at[sem], device_id=peer, device_id_type=MESH)
            self.starts.append((when, cp.start))
            self.send_waits.append((when, cp.wait_send))

        def arrival(a, k, slot, when):
            sem = a * N_DEV + k
            cp = pltpu.make_async_remote_copy(
                src_ref=outs[a].at[slot], dst_ref=outs[a].at[slot], send_sem=send_sems.at[sem],
                recv_sem=recv_sems.at[sem], device_id=(x, y, c), device_id_type=MESH)
            self.recv_waits.append((when, cp.wait_recv))

        def local(a, src, slot, when):
            cp = pltpu.make_async_copy(src, outs[a].at[slot], local_sems.at[a])
            self.starts.append((when, cp.start))
            self.send_waits.append((when, cp.wait))

        for a, (src, mode) in enumerate(zip(srcs, modes)):
            if mode in ("gather", "scatter"):
                local(a, src if mode == "gather" else src.at[me], me, None)
                for k in range(1, N_DEV):
                    peer, peer_idx = _peer(x, y, c, k)
                    remote(a, k, src if mode == "gather" else src.at[peer_idx], me, peer, None)
                    arrival(a, k, peer_idx, None)
                continue
            if isinstance(mode, tuple):
                here, away = x == mode[1], x != mode[1]
                chip = 2 * x + y
                local(a, src.at[y], chip, here)
                remote(a, 1, src.at[1 - y], chip, (x, 1 - y, c), here)
                remote(a, 2, src.at[y], chip, (1 - x, y, c), away)
                remote(a, 3, src.at[1 - y], chip, (1 - x, 1 - y, c), away)
                arrival(a, 1, 2 * x + 1 - y, here)
                arrival(a, 2, 2 * (1 - x) + y, here)
                arrival(a, 3, 2 * (1 - x) + 1 - y, here)
                continue
            here, away = x == mode, x != mode
            for kk in range(4):
                py = 1 - y if kk & 2 else y
                pc = 1 - c if kk & 1 else c
                block = src.at[2 * py + pc]
                if kk == 0:
                    local(a, block, me, here)
                else:
                    remote(a, kk, block, me, (x, py, pc), here)
                remote(a, 4 + kk, block, me, (1 - x, py, pc), away)
            for k in range(1, N_DEV):
                arrival(a, k, _peer(x, y, c, k)[1], here)

    @staticmethod
    def _run(actions):
        for when, fn in actions:
            if when is None:
                fn()
            else:
                pl.when(when)(fn)

    def start(self):
        self._run(self.starts)

    def wait(self):
        self._run(self.recv_waits)
        self._run(self.send_waits)


def _exchange_scratch(n):
    return [pltpu.SemaphoreType.DMA((n * N_DEV,)), pltpu.SemaphoreType.DMA((n * N_DEV,)),
            pltpu.SemaphoreType.DMA((n,))]


def _recv_shapes(srcs, modes):
    def shape(s, m):
        if m == "gather":
            return (N_DEV,) + s.shape
        return (N_DEV // 2 if isinstance(m, tuple) else N_DEV,) + s.shape[1:]

    return [jax.ShapeDtypeStruct(shape(s, m), s.dtype) for s, m in zip(srcs, modes)]


def _pair_reduce(name, blocks):
    shape = (2,) + blocks.shape[1:]

    def body(src, out_ref, stage, mine, send_sems, recv_sems, local_sems):
        x, y, c = lax.axis_index("x"), lax.axis_index("y"), lax.axis_index("c")
        sends, waits = [], []
        for py in range(2):
            sends.append(pltpu.make_async_remote_copy(
                src_ref=src.at[2 * py + 1 - c], dst_ref=stage.at[py], send_sem=send_sems.at[py],
                recv_sem=recv_sems.at[py], device_id=(x, y, 1 - c), device_id_type=MESH))
            waits.append(pltpu.make_async_copy(src.at[2 * py + c], mine.at[py], local_sems.at[py]))
        for cp in sends + waits:
            cp.start()
        for cp in waits:
            cp.wait()
        for cp in sends:
            cp.wait_recv()
        out_ref[...] = mine[...] + stage[...]
        for cp in sends:
            cp.wait_send()

    return pl.pallas_call(
        body, name=name, out_shape=jax.ShapeDtypeStruct(shape, F32), in_specs=[ANY],
        scratch_shapes=[pltpu.VMEM(shape, F32), pltpu.VMEM(shape, F32),
                        pltpu.SemaphoreType.DMA((2,)), pltpu.SemaphoreType.DMA((2,)),
                        pltpu.SemaphoreType.DMA((2,))],
        compiler_params=_params(40),
    )(blocks)


def _exchange_call(name, srcs, modes):
    n = len(srcs)

    def body(*refs):
        xch = _Exchange(refs[:n], refs[n:2 * n], modes, *refs[2 * n:])
        xch.start()
        xch.wait()

    return pl.pallas_call(
        body, name=name, out_shape=_recv_shapes(srcs, modes),
        in_specs=[ANY] * n, out_specs=[ANY] * n, scratch_shapes=_exchange_scratch(n),
    )(*srcs)


def _hosted(body, n_in, n_out, grid, modes):
    n = len(modes)

    def hosted(*refs):
        ins, srcs = refs[:n_in], refs[n_in:n_in + n]
        outs = refs[n_in + n:n_in + n + n_out]
        bufs = refs[n_in + n + n_out:n_in + 2 * n + n_out]
        scratch = refs[n_in + 2 * n + n_out:-3]
        xch = _Exchange(srcs, bufs, modes, *refs[-3:])
        first, last = True, True
        for axis, size in enumerate(grid):
            first = jnp.logical_and(first, pl.program_id(axis) == 0)
            last = jnp.logical_and(last, pl.program_id(axis) == size - 1)
        pl.when(first)(xch.start)
        body(*ins, *outs, *scratch)
        pl.when(last)(xch.wait)

    return hosted


N_CHIPS = N_DEV // 2
PAIR_COLS = 2 * COLS_PER_DEV
PUSHED = (1, 2, 4, 6)
FORWARDED = (2, 4, 6)
NORM_ROWS = 32


def _inproj_fwd(x, ln_g, w_shard):
    t = x.shape[0]
    tt = min(512, t)
    n_t = t // tt
    chip = 2 * lax.axis_index("x") + lax.axis_index("y")
    order = jnp.bitwise_xor(chip, jnp.arange(N_CHIPS, dtype=jnp.int32)).astype(jnp.int32)

    def body(order_ref, x_ref, g_ref, shard_hbm, z_ref, u_ref, w_all,
             u_all, w_blk, w_send, w_recv, w_local):
        p, i = pl.program_id(0), pl.program_id(1)
        x, y, c = lax.axis_index("x"), lax.axis_index("y"), lax.axis_index("c")
        mine = 4 * x + 2 * y + c

        def push(k):
            peer, _ = _peer(x, y, c, k)
            return pltpu.make_async_remote_copy(
                src_ref=shard_hbm, dst_ref=w_all.at[mine], send_sem=w_send.at[k],
                recv_sem=w_recv.at[k], device_id=peer, device_id_type=MESH)

        def forward(k):
            _, owner = _peer(x, y, c, k)
            return pltpu.make_async_remote_copy(
                src_ref=w_all.at[owner], dst_ref=w_all.at[owner], send_sem=w_send.at[k + 1],
                recv_sem=w_recv.at[k + 1], device_id=(x, y, 1 - c), device_id_type=MESH)

        def landed(k):
            _, owner = _peer(x, y, c, k)
            return pltpu.make_async_remote_copy(
                src_ref=w_all.at[owner], dst_ref=w_all.at[owner], send_sem=w_send.at[k],
                recv_sem=w_recv.at[k], device_id=(x, y, c), device_id_type=MESH)

        keep = pltpu.make_async_copy(shard_hbm, w_all.at[mine], w_local.at[0])

        def load_pair(step):
            same = shard_hbm if step == 0 else w_all.at[_peer(x, y, c, 2 * step)[1]]
            other = w_all.at[_peer(x, y, c, 2 * step + 1)[1]]
            for side in range(2):
                @pl.when(c == side)
                def _(side=side):
                    pltpu.sync_copy(same, w_blk.at[:, pl.ds(COLS_PER_DEV * side, COLS_PER_DEV)])
                    pltpu.sync_copy(
                        other, w_blk.at[:, pl.ds(COLS_PER_DEV * (1 - side), COLS_PER_DEV)])

        @pl.when(jnp.logical_and(p == 0, i == 0))
        def _():
            for k in PUSHED[:-1]:
                push(k).start()
            keep.start()

        @pl.when(jnp.logical_and(p == 1, i == 0))
        def _():
            for k in PUSHED[1:-1]:
                push(k).wait_send()
            push(PUSHED[-1]).start()

        for step in range(N_CHIPS):
            @pl.when(jnp.logical_and(p == step, i == 0))
            def _(step=step):
                landed(2 * step + 1).wait_recv()
                load_pair(step)

        rows = pl.ds(pl.multiple_of(i * tt, tt), tt)

        @pl.when(p == 0)
        def _():
            def norm_rows(r, carry):
                sub = pl.ds(pl.multiple_of(r * NORM_ROWS, NORM_ROWS), NORM_ROWS)
                xv = x_ref[sub, :]
                rstd = lax.rsqrt(jnp.mean(xv * xv, axis=-1, keepdims=True) + EPS)
                ub = (xv * rstd * g_ref[...]).astype(BF16)
                u_ref[sub, :] = ub
                u_all[pl.ds(pl.multiple_of(i * tt + r * NORM_ROWS, NORM_ROWS), NORM_ROWS), :] = ub
                return carry

            lax.fori_loop(0, tt // NORM_ROWS, norm_rows, 0, unroll=2)

        z_ref[...] = jnp.dot(u_all[rows, :], w_blk[...], preferred_element_type=F32)

        for step in range(1, N_CHIPS):
            @pl.when(jnp.logical_and(p == step - 1, i == n_t - 1))
            def _(step=step):
                landed(2 * step).wait_recv()
                forward(2 * step).start()

        @pl.when(jnp.logical_and(p == N_CHIPS - 1, i == n_t - 1))
        def _():
            push(PUSHED[0]).wait_send()
            push(PUSHED[-1]).wait_send()
            for k in FORWARDED:
                forward(k).wait_send()
            keep.wait()

    first_pass = lambda p, i, order_ref: (jnp.where(p == 0, i, n_t - 1), 0)
    grid_spec = pltpu.PrefetchScalarGridSpec(
        num_scalar_prefetch=1, grid=(N_CHIPS, n_t),
        in_specs=[pl.BlockSpec((tt, D), first_pass),
                  pl.BlockSpec((1, D), lambda p, i, order_ref: (0, 0)), ANY],
        out_specs=[pl.BlockSpec((tt, PAIR_COLS), lambda p, i, order_ref: (i, order_ref[p])),
                   pl.BlockSpec((tt, D), first_pass), ANY],
        scratch_shapes=[pltpu.VMEM((t, D), BF16), pltpu.VMEM((D, PAIR_COLS), BF16),
                        pltpu.SemaphoreType.DMA((N_DEV,)), pltpu.SemaphoreType.DMA((N_DEV,)),
                        pltpu.SemaphoreType.DMA((1,))])
    return pl.pallas_call(
        body, name="inproj_fwd", grid_spec=grid_spec,
        out_shape=[jax.ShapeDtypeStruct((t, N_COLS), F32), jax.ShapeDtypeStruct((t, D), BF16),
                   jax.ShapeDtypeStruct((N_DEV,) + w_shard.shape, BF16)],
        compiler_params=_params(48, dimension_semantics=("arbitrary", "arbitrary")),
    )(order, x, ln_g, w_shard)


def _shifted_copies(buf, shifted, rows):
    for b in range(1, SUBLANES):
        shifted[b, 0:rows, :] = buf[b:b + rows, :]


def _tap_ref(buf, shifted, offset):
    a, b = divmod(offset, SUBLANES)
    return (buf if b == 0 else shifted.at[b]), SUBLANES * a


def _tap_slabs(buf, shifted, offset_of_tap):
    groups = {}
    for k in range(CONV_K):
        a, b = divmod(offset_of_tap(k), SUBLANES)
        groups.setdefault(b, []).append((SUBLANES * a, k))
    out = []
    for b, taps in sorted(groups.items()):
        taps.sort()
        lo = taps[0][0]
        out.append((buf if b == 0 else shifted.at[b], lo, [(k, off - lo) for off, k in taps]))
    return out


def _group_norm_stats(blk):
    mu = jnp.mean(blk, axis=-1, keepdims=True)
    cen = blk - mu
    var = jnp.mean(cen * cen, axis=-1, keepdims=True)
    return cen * lax.rsqrt(var + EPS)


def _conv_fwd(z, conv_w_all, conv_b, cn_g, cn_b, w_pw2, b_pw2):
    t = z.shape[0]
    tt = min(256, t)
    rc = 128

    def body(val_ref, glu_ref, gate_ref, cw_ref, cb_ref, g_ref, b_ref, w_hbm, b2_ref,
             yc_ref, y2_ref, yo_ref, w_vmem, vbuf, vsh, y1buf):
        @pl.when(pl.program_id(0) == 0)
        def _():
            pltpu.sync_copy(w_hbm, w_vmem)
            vbuf[0:CONV_PAD, :] = jnp.zeros((CONV_PAD, D), F32)

        vbuf[CONV_PAD:CONV_PAD + tt, :] = val_ref[...] * _sigmoid(glu_ref[...])
        _shifted_copies(vbuf, vsh, tt + 24)

        for g in range(N_HEADS):
            cs = slice(HEAD * g, HEAD * (g + 1))

            def row_chunk(r, carry, g=g, cs=cs):
                r0 = pl.multiple_of(r * rc, rc)
                acc = jnp.broadcast_to(cb_ref[:, cs], (rc, HEAD))
                for ref, lo, taps in _tap_slabs(vbuf, vsh, lambda k: k + 2):
                    slab = ref[pl.ds(r0 + lo, rc + taps[-1][1]), cs]
                    for k, off in taps:
                        acc = acc + cw_ref[g, k:k + 1, :] * slab[off:off + rc]
                yc_ref[pl.ds(r0, rc), cs] = acc
                n = _group_norm_stats(acc) * g_ref[:, cs] + b_ref[:, cs]
                y1buf[pl.ds(r0, rc), cs] = (n * _sigmoid(n)).astype(BF16)
                return carry

            lax.fori_loop(0, tt // rc, row_chunk, 0, unroll=True)
        vbuf[0:CONV_PAD, :] = vbuf[tt:tt + CONV_PAD, :]
        y2 = jnp.dot(y1buf[...], w_vmem[...], preferred_element_type=F32) + b2_ref[...]
        y2_ref[...] = y2
        gate = gate_ref[...]
        yo_ref[...] = (y2 * gate * _sigmoid(gate)).astype(BF16)

    col = lambda j: pl.BlockSpec((tt, D), lambda i: (i, j))
    row = pl.BlockSpec((tt, D), lambda i: (i, 0))
    return pl.pallas_call(
        body, name="conv_fwd", grid=(t // tt,),
        out_shape=[jax.ShapeDtypeStruct((t, D), F32), jax.ShapeDtypeStruct((t, D), F32),
                   jax.ShapeDtypeStruct((t, D), BF16)],
        in_specs=[col(0), col(1), col(2), _full((N_DEV, CONV_PAD, HEAD)), _full((1, D)),
                  _full((1, D)), _full((1, D)), ANY, _full((1, D))],
        out_specs=[row, row, row],
        scratch_shapes=[pltpu.VMEM((D, D), BF16), pltpu.VMEM((tt + CONV_PAD, D), F32),
                        pltpu.VMEM((SUBLANES, tt + CONV_PAD, D), F32), pltpu.VMEM((tt, D), BF16)],
        compiler_params=_params(48, dimension_semantics=("arbitrary",)),
    )(z, z, z, conv_w_all, conv_b, cn_g, cn_b, w_pw2, b_pw2)


HEAD_GROUP = 8
_HEAD_LANES = [slice(HEAD * j, HEAD * (j + 1)) for j in range(HEAD_GROUP)]


def _head_mean(a):
    return jnp.concatenate(
        [jnp.broadcast_to(jnp.mean(a[:, hs], axis=-1, keepdims=True), (a.shape[0], HEAD))
         for hs in _HEAD_LANES], axis=1)


def _chunk_quantities(zq, zf, lbh, tri):
    sig = _sigmoid(zf)
    sig_neg = _sigmoid(-zf)
    f = lbh + (1.0 - lbh) * sig
    k = (1.0 - lbh) * sig_neg
    q = zq * _sigmoid(zq)
    b = _tri_dot(tri, jnp.log(f))
    b_mid = b[CHUNK // 2 - 1:CHUNK // 2, :]
    b_last = b[CHUNK - 1:CHUNK, :]
    e_q = jnp.exp(b)
    e_qm = jnp.exp(b - b_mid)
    e_km = jnp.exp(b_mid - b)
    e_kd = jnp.exp(b_last - b)
    return q, k, f, sig, sig_neg, e_q, e_qm, e_km, e_kd, jnp.exp(b_last)


def _hgrn_fwd(z, lb_logits, onorm_g, shards):
    t = z.shape[0]
    tt = min(256, t)
    nc = tt // CHUNK
    modes = ["gather"] * len(shards)
    n = len(modes)

    def body(q_ref, f_ref, i_ref, g_ref, lbl_ref, on_ref, o_ref, y_ref, s_ref, st):
        @pl.when(pl.program_id(0) == 0)
        def _():
            st[...] = jnp.zeros_like(st)

        lb, _ = _lower_bound(lbl_ref[...])
        rows = lax.broadcasted_iota(jnp.int32, (CHUNK, CHUNK), 0)
        cols = lax.broadcasted_iota(jnp.int32, (CHUNK, CHUNK), 1)
        causal = rows >= cols
        tri = causal.astype(BF16)

        def chunk(c, carry):
            r0 = pl.multiple_of(c * CHUNK, CHUNK)
            rs = pl.ds(r0, CHUNK)
            for h0 in range(0, N_HEADS, HEAD_GROUP):
                cs = slice(HEAD * h0, HEAD * (h0 + HEAD_GROUP))
                q, k, _, _, _, e_q, e_qm, e_km, e_kd, e_last = _chunk_quantities(
                    q_ref[rs, cs], f_ref[rs, cs], lb[:, cs], tri)
                v = i_ref[rs, cs].astype(BF16)
                qm, km = (q * e_qm).astype(BF16), (k * e_km).astype(BF16)
                qt, kd = (q * e_q).astype(BF16), (k * e_kd).astype(BF16)
                outs = []
                for j, hs in enumerate(_HEAD_LANES):
                    s_old = st[h0 + j]
                    s_ref[c, h0 + j] = s_old
                    a = jnp.where(causal, _dot_nt(qm[:, hs], km[:, hs]), 0.0)
                    outs.append(_dot_nt(qt[:, hs], s_old) + _dot(a, v[:, hs]))
                    st[h0 + j] = s_old * e_last[:, hs] + _dot_tn(v[:, hs], kd[:, hs])
                o = jnp.concatenate(outs, axis=1)
                o_ref[rs, cs] = o
                n = o * lax.rsqrt(_head_mean(o * o) + EPS)
                zg = g_ref[rs, cs]
                y_ref[rs, cs] = (n * on_ref[:, cs] * zg * _sigmoid(zg)).astype(BF16)
            return carry

        lax.fori_loop(0, nc, chunk, 0, unroll=True)

    col = lambda j: pl.BlockSpec((tt, D), lambda i: (i, j))
    row = pl.BlockSpec((tt, D), lambda i: (i, 0))
    return pl.pallas_call(
        _hosted(body, 6, 3, (t // tt,), modes), name="hgrn_fwd", grid=(t // tt,),
        out_shape=[jax.ShapeDtypeStruct((t, D), F32), jax.ShapeDtypeStruct((t, D), BF16),
                   jax.ShapeDtypeStruct((t // CHUNK, N_HEADS, HEAD, HEAD), F32)]
        + _recv_shapes(shards, modes),
        in_specs=[col(3), col(4), col(5), col(6), _full((2, D)), _full((1, D))] + [ANY] * n,
        out_specs=[row, row, pl.BlockSpec((nc, N_HEADS, HEAD, HEAD), lambda i: (i, 0, 0, 0))]
        + [ANY] * n,
        scratch_shapes=[pltpu.VMEM((N_HEADS, HEAD, HEAD), F32)] + _exchange_scratch(n),
        compiler_params=_params(40, dimension_semantics=("arbitrary",)),
    )(z, z, z, z, lb_logits, onorm_g, *shards)


def _rms_bwd(dn, xhat, rstd):
    return rstd * (dn - xhat * jnp.mean(dn * xhat, axis=-1, keepdims=True))


def _tail(x, y_conv, y_hgrn, p, target, w_out, w_pg, w_pp_all, pe_g, fin_g):
    t = x.shape[0]
    tt = min(256, t)
    n_steps = t // tt

    def body(x_ref, yc_ref, yh_ref, p_ref, tg_ref, wo_hbm, wg_hbm, wp_hbm, pg_ref, fg_ref,
             dh1_ref, dyc_ref, dyh_ref, dwo_hbm, dwg_hbm, dwp_hbm, dpg_ref, dfg_ref, loss_ref,
             wo, wg, wp, dwo, dwg, dwp):
        i = pl.program_id(0)

        @pl.when(i == 0)
        def _():
            pltpu.sync_copy(wo_hbm, wo)
            pltpu.sync_copy(wg_hbm, wg)
            for d in range(N_DEV):
                pltpu.sync_copy(wp_hbm.at[d], wp.at[:, pl.ds(HEAD * d, HEAD)])
            dwo[...] = jnp.zeros_like(dwo)
            dwg[...] = jnp.zeros_like(dwg)
            dwp[...] = jnp.zeros_like(dwp)
            dpg_ref[...] = jnp.zeros_like(dpg_ref)
            dfg_ref[...] = jnp.zeros_like(dfg_ref)
            loss_ref[...] = jnp.zeros_like(loss_ref)

        ycv, yhv = yc_ref[...], yh_ref[...]
        h1 = (x_ref[...] + jnp.dot(ycv, wo[0:D, :], preferred_element_type=F32)
              + jnp.dot(yhv, wo[D:2 * D, :], preferred_element_type=F32))
        pb = p_ref[...].astype(BF16)
        pe = jnp.dot(pb, wp[...], preferred_element_type=F32)
        rstd1 = lax.rsqrt(jnp.mean(h1 * h1, axis=-1, keepdims=True) + EPS)
        n1 = h1 * rstd1
        rb = (n1 * pg_ref[...]).astype(BF16)
        gate = _sigmoid(jnp.dot(rb, wg[...], preferred_element_type=F32))
        h2 = h1 + gate * pe
        rstd2 = lax.rsqrt(jnp.mean(h2 * h2, axis=-1, keepdims=True) + EPS)
        n2 = h2 * rstd2
        err = n2 * fg_ref[...] - tg_ref[...]
        loss_ref[...] += _rowsum8(err * err)

        d_out = err * (1.0 / D)
        dfg_ref[...] += _rowsum8(d_out * n2)
        d_h2 = _rms_bwd(d_out * fg_ref[...], n2, rstd2)
        d_pe = (d_h2 * gate).astype(BF16)
        d_gpre = (d_h2 * pe * gate * (1.0 - gate)).astype(BF16)
        dwg[...] += _dot_tn(rb, d_gpre)
        dwp[...] += _dot_tn(pb, d_pe)
        dr = _dot_nt(d_gpre, wg[...])
        dpg_ref[...] += _rowsum8(dr * n1)
        d_h1 = d_h2 + _rms_bwd(dr * pg_ref[...], n1, rstd1)
        dh1_ref[...] = d_h1
        d_h1b = d_h1.astype(BF16)
        dwo[0:D, :] += _dot_tn(ycv, d_h1b)
        dwo[D:2 * D, :] += _dot_tn(yhv, d_h1b)
        dyc_ref[...] = _dot_nt(d_h1b, wo[0:D, :])
        dyh_ref[...] = _dot_nt(d_h1b, wo[D:2 * D, :])

        @pl.when(i == n_steps - 1)
        def _():
            pltpu.sync_copy(dwo, dwo_hbm)
            pltpu.sync_copy(dwg, dwg_hbm)
            for d in range(N_DEV):
                pltpu.sync_copy(dwp.at[:, pl.ds(HEAD * d, HEAD)], dwp_hbm.at[d])

    row = pl.BlockSpec((tt, D), lambda i: (i, 0))
    acc = _full((SUBLANES, D))
    return pl.pallas_call(
        body, name="tail_fwd_bwd", grid=(n_steps,),
        out_shape=[jax.ShapeDtypeStruct((t, D), F32)] * 3
        + [jax.ShapeDtypeStruct((2 * D, D), F32), jax.ShapeDtypeStruct((D, D), F32),
           jax.ShapeDtypeStruct((N_DEV, PLE, HEAD), F32)]
        + [jax.ShapeDtypeStruct((SUBLANES, D), F32)] * 3,
        in_specs=[row, row, row, pl.BlockSpec((tt, PLE), lambda i: (i, 0)), row,
                  ANY, ANY, ANY, _full((1, D)), _full((1, D))],
        out_specs=[row, row, row, ANY, ANY, ANY, acc, acc, acc],
        scratch_shapes=[pltpu.VMEM((2 * D, D), BF16), pltpu.VMEM((D, D), BF16),
                        pltpu.VMEM((PLE, D), BF16), pltpu.VMEM((2 * D, D), F32),
                        pltpu.VMEM((D, D), F32), pltpu.VMEM((PLE, D), F32)],
        compiler_params=_params(52, dimension_semantics=("arbitrary",)),
    )(x, y_conv, y_hgrn, p, target, w_out, w_pg, w_pp_all, pe_g, fin_g)


def _hgrn_bwd(dy, z, o_raw, states, lb_logits, onorm_g, grads):
    t = z.shape[0]
    tt = min(256, t)
    nc = tt // CHUNK
    n_steps = t // tt
    modes = ["scatter"] * len(grads)

    def body(dy_ref, q_ref, f_ref, i_ref, g_ref, o_ref, s_ref, lbl_ref, on_ref,
             dz_ref, don_ref, dlb_ref, dst):
        @pl.when(pl.program_id(0) == 0)
        def _():
            dst[...] = jnp.zeros_like(dst)
            don_ref[...] = jnp.zeros_like(don_ref)
            dlb_ref[...] = jnp.zeros_like(dlb_ref)

        lb, _ = _lower_bound(lbl_ref[...])
        rows = lax.broadcasted_iota(jnp.int32, (CHUNK, CHUNK), 0)
        cols = lax.broadcasted_iota(jnp.int32, (CHUNK, CHUNK), 1)
        causal = rows >= cols
        tri = causal.astype(BF16)
        tri_rev = (rows <= cols).astype(BF16)
        width = HEAD * HEAD_GROUP
        is_last = lax.broadcasted_iota(jnp.int32, (CHUNK, width), 0) == CHUNK - 1
        nn = (((1,), (0,)), ((), ()))
        tn = (((0,), (0,)), ((), ()))
        dg = functools.partial(lax.dot_general, preferred_element_type=F32)

        def chunk(cc, carry):
            c = nc - 1 - cc
            r0 = pl.multiple_of(c * CHUNK, CHUNK)
            rs = pl.ds(r0, CHUNK)
            for h0 in range(0, N_HEADS, HEAD_GROUP):
                cs = slice(HEAD * h0, HEAD * h0 + width)
                zq, zf, zg = q_ref[rs, cs], f_ref[rs, cs], g_ref[rs, cs]
                lbh = lb[:, cs]
                q, k, f, sig, sig_neg, e_q, e_qm, e_km, e_kd, e_last = _chunk_quantities(
                    zq, zf, lbh, tri)
                vb = i_ref[rs, cs].astype(BF16)
                qt, qm, km, kd = q * e_q, q * e_qm, k * e_km, k * e_kd
                qt_b, kd_b = qt.astype(BF16), kd.astype(BF16)
                qm_h, qm_l = _split(qm)
                km_h, km_l = _split(km)

                o = o_ref[rs, cs]
                rstd = lax.rsqrt(_head_mean(o * o) + EPS)
                n = o * rstd
                sg = _sigmoid(zg)
                dyv = dy_ref[rs, cs]
                on = on_ref[:, cs]
                d_zg = dyv * n * on * sg * (1.0 + zg * (1.0 - sg))
                d_on = dyv * zg * sg
                don_ref[:, cs] += _rowsum8(d_on * n)
                dn = d_on * on
                do_b = (rstd * (dn - n * _head_mean(dn * n))).astype(BF16)

                dv, dkd, dqt, dqm, dkm, s_dots = [], [], [], [], [], []
                for j, hs in enumerate(_HEAD_LANES):
                    s_old, ds_new = s_ref[c, h0 + j], dst[h0 + j]
                    ds_b = ds_new.astype(BF16)
                    a = jnp.where(causal, _dot_nt(qm_h[:, hs], km_h[:, hs]), 0.0)
                    da = jnp.where(causal, _dot_nt(do_b[:, hs], vb[:, hs]), 0.0)
                    dv.append(_dot_tn(a, do_b[:, hs]) + _dot_nt(kd_b[:, hs], ds_b))
                    dkd.append(_dot(vb[:, hs], ds_b))
                    dqt.append(_dot(do_b[:, hs], s_old))
                    da_h, da_l = _split(da)
                    dqm.append(dg(da_h, km_h[:, hs], nn)
                               + (dg(da_h, km_l[:, hs], nn) + dg(da_l, km_h[:, hs], nn)))
                    dkm.append(dg(da_h, qm_h[:, hs], tn)
                               + (dg(da_h, qm_l[:, hs], tn) + dg(da_l, qm_h[:, hs], tn)))
                    dst[h0 + j] = ds_new * e_last[:, hs] + _dot_tn(do_b[:, hs], qt_b[:, hs])
                    s_dots.append(jnp.sum(s_old * ds_new, axis=0, keepdims=True))
                dv, dkd, dqt, dqm, dkm, s_dots = [
                    jnp.concatenate(parts, axis=1) for parts in (dv, dkd, dqt, dqm, dkm, s_dots)]
                dq = dqt * e_q + dqm * e_qm
                dk = dkm * e_km + dkd * e_kd
                last = jnp.sum(dkd * kd, axis=0, keepdims=True) + e_last * s_dots
                db = q * dq - k * dk + jnp.where(is_last, last, 0.0)
                dlogf = _tri_dot(tri_rev, db)
                common = sig_neg * (dlogf / f - dk)
                dlb_ref[:, cs] += _rowsum8(common)
                c0 = 3 * D + HEAD * h0
                sq = _sigmoid(zq)
                dz_ref[rs, c0:c0 + width] = (dq * sq * (1.0 + zq * (1.0 - sq))).astype(BF16)
                dz_ref[rs, D + c0:D + c0 + width] = ((1.0 - lbh) * sig * common).astype(BF16)
                dz_ref[rs, 2 * D + c0:2 * D + c0 + width] = dv.astype(BF16)
                dz_ref[rs, 3 * D + c0:3 * D + c0 + width] = d_zg.astype(BF16)
            return carry

        lax.fori_loop(0, nc, chunk, 0, unroll=True)

    rev = lambda i: n_steps - 1 - i
    col = lambda j: pl.BlockSpec((tt, D), lambda i: (rev(i), j))
    row = pl.BlockSpec((tt, D), lambda i: (rev(i), 0))
    acc = _full((SUBLANES, D))
    n = len(modes)
    return pl.pallas_call(
        _hosted(body, 9, 3, (n_steps,), modes), name="hgrn_bwd", grid=(n_steps,),
        out_shape=[jax.ShapeDtypeStruct((t, N_COLS), BF16),
                   jax.ShapeDtypeStruct((SUBLANES, D), F32),
                   jax.ShapeDtypeStruct((SUBLANES, D), F32)] + _recv_shapes(grads, modes),
        in_specs=[row, col(3), col(4), col(5), col(6), row,
                  pl.BlockSpec((nc, N_HEADS, HEAD, HEAD), lambda i: (rev(i), 0, 0, 0)),
                  _full((2, D)), _full((1, D))] + [ANY] * n,
        out_specs=[pl.BlockSpec((tt, N_COLS), lambda i: (rev(i), 0)), acc, acc] + [ANY] * n,
        scratch_shapes=[pltpu.VMEM((N_HEADS, HEAD, HEAD), F32)] + _exchange_scratch(n),
        compiler_params=_params(48, dimension_semantics=("arbitrary",)),
    )(dy, z, z, z, z, o_raw, states, lb_logits, onorm_g, *grads)


def _conv_bwd(dy, z, yc, y2, conv_w_all, cn_g, cn_b, w_pw2, dz, grads, modes):
    t = z.shape[0]
    tt = min(256, t)
    rc = 32
    n_steps = t // tt

    def body(dy_ref, val_ref, glu_ref, gate_ref, yc_ref, y2_ref, cw_ref, g_ref, b_ref, w_hbm,
             dz_in, dz_ref, dw_hbm, dcw_out, db2_ref, dg_ref, dbeta_ref, dcb_ref,
             w_vmem, dw, dbuf, dsh, y1buf, dnbuf, dcw_ref):
        i = pl.program_id(0)

        @pl.when(i == 0)
        def _():
            pltpu.sync_copy(w_hbm, w_vmem)
            dw[...] = jnp.zeros_like(dw)
            dbuf[tt:tt + CONV_PAD, :] = jnp.zeros((CONV_PAD, D), F32)
            dcw_ref[...] = jnp.zeros_like(dcw_ref)
            dcw_out[...] = jnp.zeros_like(dcw_out)
            db2_ref[...] = jnp.zeros_like(db2_ref)
            dg_ref[...] = jnp.zeros_like(dg_ref)
            dbeta_ref[...] = jnp.zeros_like(dbeta_ref)
            dcb_ref[...] = jnp.zeros_like(dcb_ref)

        gate = gate_ref[...]
        sg = _sigmoid(gate)
        dyv = dy_ref[...]
        dy2 = dyv * gate * sg
        dz_ref[:, 2 * D:3 * D] = (dyv * y2_ref[...] * sg * (1.0 + gate * (1.0 - sg))).astype(BF16)
        db2_ref[...] += _rowsum8(dy2)
        dy2b = dy2.astype(BF16)
        dnbuf[...] = _dot_nt(dy2b, w_vmem[...])

        def norm_chunk(r, carry):
            r0 = pl.multiple_of(r * rc, rc)
            rs = pl.ds(r0, rc)
            for g in range(N_HEADS):
                cs = slice(HEAD * g, HEAD * (g + 1))
                blk = yc_ref[rs, cs]
                mu = jnp.mean(blk, axis=-1, keepdims=True)
                cen = blk - mu
                rstd = lax.rsqrt(jnp.mean(cen * cen, axis=-1, keepdims=True) + EPS)
                xhat = cen * rstd
                n = xhat * g_ref[:, cs] + b_ref[:, cs]
                sn = _sigmoid(n)
                y1buf[rs, cs] = (n * sn).astype(BF16)
                dn = dnbuf[rs, cs] * sn * (1.0 + n * (1.0 - sn))
                dg_ref[:, cs] += _rowsum8(dn * xhat)
                dbeta_ref[:, cs] += _rowsum8(dn)
                dxh = dn * g_ref[:, cs]
                dyc = rstd * (dxh - jnp.mean(dxh, axis=-1, keepdims=True)
                              - xhat * jnp.mean(dxh * xhat, axis=-1, keepdims=True))
                dcb_ref[:, cs] += _rowsum8(dyc)
                dbuf[rs, cs] = dyc
            return carry

        lax.fori_loop(0, tt // rc, norm_chunk, 0, unroll=2)
        dw[...] += _dot_tn(y1buf[...], dy2b)
        _shifted_copies(dbuf, dsh, tt + 24)

        def conv_chunk(r, carry):
            r0 = pl.multiple_of(r * rc, rc)
            rs = pl.ds(r0, rc)
            for g in range(N_HEADS):
                cs = slice(HEAD * g, HEAD * (g + 1))
                sglu = _sigmoid(glu_ref[rs, cs])
                val = val_ref[rs, cs]
                v = val * sglu
                dv = jnp.zeros((rc, HEAD), F32)
                for ref, lo, taps in _tap_slabs(dbuf, dsh, lambda k: CONV_K - 1 - k):
                    slab = ref[pl.ds(r0 + lo, rc + taps[-1][1]), cs]
                    for k, off in taps:
                        d_later = slab[off:off + rc]
                        dv = dv + cw_ref[g, k:k + 1, :] * d_later
                        dcw_ref[g, k] += _rowsum8(v * d_later)
                dz_ref[rs, cs] = (dv * sglu).astype(BF16)
                dz_ref[rs, D + HEAD * g:D + HEAD * (g + 1)] = (
                    dv * val * sglu * (1.0 - sglu)).astype(BF16)
            return carry

        lax.fori_loop(0, tt // rc, conv_chunk, 0, unroll=2)
        dbuf[tt:tt + CONV_PAD, :] = dbuf[0:CONV_PAD, :]

        @pl.when(i == n_steps - 1)
        def _():
            pltpu.sync_copy(dw, dw_hbm)
            for g in range(N_HEADS):
                for k in range(CONV_K):
                    dcw_out[g, k:k + 1, :] = jnp.sum(dcw_ref[g, k], axis=0, keepdims=True)

    rev = lambda i: n_steps - 1 - i
    col = lambda j: pl.BlockSpec((tt, D), lambda i: (rev(i), j))
    row = pl.BlockSpec((tt, D), lambda i: (rev(i), 0))
    acc = _full((SUBLANES, D))
    n = len(modes)
    return pl.pallas_call(
        _hosted(body, 11, 7, (n_steps,), modes), name="conv_bwd", grid=(n_steps,),
        out_shape=[jax.ShapeDtypeStruct((t, N_COLS), BF16), jax.ShapeDtypeStruct((D, D), F32),
                   jax.ShapeDtypeStruct((N_DEV, CONV_PAD, HEAD), F32)]
        + [jax.ShapeDtypeStruct((SUBLANES, D), F32)] * 4 + _recv_shapes(grads, modes),
        in_specs=[row, col(0), col(1), col(2), row, row, _full((N_DEV, CONV_PAD, HEAD)),
                  _full((1, D)), _full((1, D)), ANY, ANY] + [ANY] * n,
        out_specs=[pl.BlockSpec((tt, 3 * D), lambda i: (rev(i), 0)), ANY,
                   _full((N_DEV, CONV_PAD, HEAD)), acc, acc, acc, acc] + [ANY] * n,
        input_output_aliases={10: 0},
        scratch_shapes=[pltpu.VMEM((D, D), BF16), pltpu.VMEM((D, D), F32),
                        pltpu.VMEM((tt + CONV_PAD, D), F32),
                        pltpu.VMEM((SUBLANES, tt + CONV_PAD, D), F32),
                        pltpu.VMEM((tt, D), BF16), pltpu.VMEM((tt, D), F32),
                        pltpu.VMEM((N_DEV, CONV_PAD, SUBLANES, HEAD), F32)] + _exchange_scratch(n),
        compiler_params=_params(52, dimension_semantics=("arbitrary",)),
    )(dy, z, z, z, yc, y2, conv_w_all, cn_g, cn_b, w_pw2, dz, *grads)


def _inproj_bwd_dx(dz, x, d_h1, ln_g, w_in_all, grads, modes):
    t = x.shape[0]
    tt = min(256, t)

    def body(dz_ref, x_ref, dh1_ref, g_ref, w_hbm, dx_ref, dg_ref, w_vmem):
        @pl.when(pl.program_id(0) == 0)
        def _():
            for d in range(N_DEV):
                pltpu.sync_copy(w_hbm.at[d], w_vmem.at[
                    d // 2, :, pl.ds(COLS_PER_DEV * (d % 2), COLS_PER_DEV)])
            dg_ref[...] = jnp.zeros_like(dg_ref)

        du = jnp.zeros((tt, D), F32)
        for q in range(N_CHIPS):
            du = du + lax.dot_general(
                dz_ref[:, PAIR_COLS * q:PAIR_COLS * (q + 1)], w_vmem[q],
                (((1,), (1,)), ((), ())), preferred_element_type=F32)
        xv = x_ref[...]
        rstd = lax.rsqrt(jnp.mean(xv * xv, axis=-1, keepdims=True) + EPS)
        xhat = xv * rstd
        dg_ref[...] += _rowsum8(du * xhat)
        dx_ref[...] = dh1_ref[...] + _rms_bwd(du * g_ref[...], xhat, rstd)

    row = pl.BlockSpec((tt, D), lambda i: (i, 0))
    n = len(modes)
    return pl.pallas_call(
        _hosted(body, 5, 2, (t // tt,), modes), name="inproj_bwd_dx", grid=(t // tt,),
        out_shape=[jax.ShapeDtypeStruct((t, D), F32), jax.ShapeDtypeStruct((SUBLANES, D), F32)]
        + _recv_shapes(grads, modes),
        in_specs=[pl.BlockSpec((tt, N_COLS), lambda i: (i, 0)), row, row, _full((1, D)), ANY]
        + [ANY] * n,
        out_specs=[row, _full((SUBLANES, D))] + [ANY] * n,
        scratch_shapes=[pltpu.VMEM((N_CHIPS, D, PAIR_COLS), BF16)] + _exchange_scratch(n),
        compiler_params=_params(48, dimension_semantics=("arbitrary",)),
    )(dz, x, d_h1, ln_g, w_in_all, *grads)


def _inproj_bwd_dw(name, u, dz, first, count, grads=(), modes=()):
    t = u.shape[0]
    tt = min(512, t)
    grid = (count // 2, t // tt)
    n = len(modes)

    def body(u_ref, dz_ref, dw_ref):
        @pl.when(pl.program_id(1) == 0)
        def _():
            dw_ref[...] = jnp.zeros_like(dw_ref)

        both = lax.dot_general(u_ref[...], dz_ref[...], (((0,), (0,)), ((), ())),
                               preferred_element_type=F32)
        dw_ref[0] += both[:, :COLS_PER_DEV]
        dw_ref[1] += both[:, COLS_PER_DEV:]

    return pl.pallas_call(
        _hosted(body, 2, 1, grid, modes) if n else body, name=name, grid=grid,
        out_shape=[jax.ShapeDtypeStruct((count, D, COLS_PER_DEV), F32)]
        + _recv_shapes(grads, modes),
        in_specs=[pl.BlockSpec((tt, D), lambda j, i: (i, 0)),
                  pl.BlockSpec((tt, PAIR_COLS), lambda j, i: (i, first // 2 + j))] + [ANY] * n,
        out_specs=[pl.BlockSpec((2, D, COLS_PER_DEV), lambda j, i: (j, 0, 0))] + [ANY] * n,
        scratch_shapes=_exchange_scratch(n) if n else [],
        compiler_params=_params(40, dimension_semantics=("arbitrary", "arbitrary")),
    )(u, dz, *grads)


def _adamw(w, g, m, v):
    m = ADAM_B1 * m + (1.0 - ADAM_B1) * g
    v = ADAM_B2 * v + (1.0 - ADAM_B2) * (g * g)
    m_hat = m / (1.0 - ADAM_B1 ** ADAM_STEP)
    v_hat = v / (1.0 - ADAM_B2 ** ADAM_STEP)
    delta = -ADAM_LR * (m_hat / (jnp.sqrt(v_hat) + ADAM_EPS) + ADAM_WD * w)
    return delta, m, v


def _pack_small(partials):
    rows = sorted(partials)

    def body(*refs):
        ins, out_ref = refs[:-1], refs[-1]
        out_ref[...] = jnp.zeros_like(out_ref)
        for j, row in enumerate(rows):
            out_ref[row:row + 1, :] = jnp.sum(ins[j][...], axis=0, keepdims=True)

    return pl.pallas_call(
        body, name="pack_small", out_shape=jax.ShapeDtypeStruct((N_SMALL, D), F32),
    )(*[partials[row] for row in rows])


def _sum_adam(name, recvs, w, m, v, rows):
    r, c = w.shape
    n = len(recvs)

    def body(*refs):
        w_ref, m_ref, v_ref, g_ref, d_ref, mo_ref, vo_ref = refs[n:]

        def finish(recv_ref):
            g = recv_ref[0]
            for s in range(1, recv_ref.shape[0]):
                g = g + recv_ref[s]
            g_ref[...] = g
            d_ref[...], mo_ref[...], vo_ref[...] = _adamw(w_ref[...], g, m_ref[...], v_ref[...])

        if n == 1:
            finish(refs[0])
        else:
            for side in range(n):
                pl.when(lax.axis_index("x") == side)(functools.partial(finish, refs[side]))

    blk = pl.BlockSpec((rows, c), lambda i: (i, 0))
    return pl.pallas_call(
        body, name=name, grid=(r // rows,),
        out_shape=[jax.ShapeDtypeStruct((r, c), F32)] * 4,
        in_specs=[pl.BlockSpec((rv.shape[0], rows, c), lambda i: (0, i, 0)) for rv in recvs]
        + [blk, blk, blk],
        out_specs=[blk] * 4,
        compiler_params=_params(48, dimension_semantics=("arbitrary",)),
    )(*recvs, w, m, v)


def _small_adam(gathered, lb_logits, w, m, v):
    def body(ga_ref, lbl_ref, w_ref, m_ref, v_ref, g_ref, d_ref, mo_ref, vo_ref, loss_ref):
        g = ga_ref[0]
        for s in range(1, N_DEV):
            g = g + ga_ref[s]
        s0, s1 = _lower_bound(lbl_ref[...])
        d_lb = g[R_LB0:R_LB0 + 1, :]
        rows = lax.broadcasted_iota(jnp.int32, (N_SMALL, D), 0)
        g = jnp.where(rows == R_LB0, d_lb * s0 * (1.0 - s0), g)
        g = jnp.where(rows == R_LB1, -d_lb * s0 * s1, g)
        g_ref[...] = g
        d_ref[...], mo_ref[...], vo_ref[...] = _adamw(w_ref[...], g, m_ref[...], v_ref[...])
        loss_ref[...] = (0.5 / D) * jnp.sum(g[R_LOSS:R_LOSS + 1, :], axis=-1, keepdims=True)

    return pl.pallas_call(
        body, name="small_adam",
        out_shape=[jax.ShapeDtypeStruct((N_SMALL, D), F32)] * 4 + [jax.ShapeDtypeStruct((1, 1), F32)],
    )(gathered, lb_logits, w, m, v)


def _pad_rows(a, rows):
    return jnp.pad(a, ((0, rows - a.shape[0]), (0, 0)))


def _pack_rows(rows):
    rows = [r.reshape(-1, D) for r in rows]
    packed = jnp.concatenate(rows, axis=0)
    return _pad_rows(packed, N_SMALL)


def kernel(x, p, ln_g, w_in, conv_w, conv_b, cnorm_g, cnorm_b, w_pw2, b_pw2, lb_logits, onorm_g, w_out, pe_norm_g, w_pg, w_pp, final_g, loss_target, m_ln_g, m_w_in, m_conv_w, m_conv_b, m_cnorm_g, m_cnorm_b, m_w_pw2, m_b_pw2, m_lb_logits, m_onorm_g, m_w_out, m_pe_norm_g, m_w_pg, m_w_pp, m_final_g, v_ln_g, v_w_in, v_conv_w, v_conv_b, v_cnorm_g, v_cnorm_b, v_w_pw2, v_b_pw2, v_lb_logits, v_onorm_g, v_w_out, v_pe_norm_g, v_w_pg, v_w_pp, v_final_g):
    t = x.shape[1]
    x2 = x.reshape(t, D)
    p2 = p.reshape(t, PLE)
    tg2 = loss_target.reshape(t, D)
    fin_g = final_g.reshape(1, D)

    z, u, w_in_all = _inproj_fwd(x2, ln_g, w_in[0].astype(BF16))
    (o_raw, y_hgrn, states, conv_w_all, w_pw2_all, w_out_all, w_pg_all, w_pp_all) = _hgrn_fwd(
        z, lb_logits, onorm_g,
        [_pad_rows(conv_w[0], CONV_PAD), w_pw2[0].astype(BF16), w_out[0].astype(BF16),
         w_pg[0].astype(BF16), w_pp[0].astype(BF16)])
    w_pw2_full = w_pw2_all.reshape(D, D)
    w_out_full = w_out_all.reshape(2 * D, D)
    w_pg_full = w_pg_all.reshape(D, D)
    yc, y2, y_conv = _conv_fwd(z, conv_w_all, conv_b, cnorm_g, cnorm_b, w_pw2_full, b_pw2)

    (d_h1, dy_conv, dy_hgrn, d_w_out, d_w_pg, d_w_pp, d_pen_p, d_fin_p, loss_p) = _tail(
        x2, y_conv, y_hgrn, p2, tg2, w_out_full, w_pg_full, w_pp_all, pe_norm_g, fin_g)

    dz, d_on_p, d_lb_p, r_w_out, r_w_pg, r_w_pp = _hgrn_bwd(
        dy_hgrn, z, o_raw, states, lb_logits, onorm_g,
        [d_w_out.reshape(N_DEV, 2 * D // N_DEV, D), d_w_pg.reshape(N_DEV, D // N_DEV, D), d_w_pp])
    (d_w_in_hi,) = _inproj_bwd_dw("inproj_bwd_dw_hi", u, dz, N_DEV // 2, N_DEV // 2)
    dz, d_w_pw2, d_conv_w, d_b2_p, d_cng_p, d_cnb_p, d_cb_p, r_w_in_hi = _conv_bwd(
        dy_conv, z, yc, y2, conv_w_all, cnorm_g, cnorm_b, w_pw2_full, dz, [d_w_in_hi], [1])
    d_w_in_lo, r_w_pw2, r_conv_w = _inproj_bwd_dw(
        "inproj_bwd_dw_lo", u, dz, 0, N_DEV // 2,
        [d_w_pw2.reshape(N_DEV, D // N_DEV, D), d_conv_w], ["scatter", "scatter"])
    chip_lo = _pair_reduce("pair_reduce_lo", d_w_in_lo)
    grad_x, d_ln_p, r_w_in_lo = _inproj_bwd_dx(
        dz, x2, d_h1, ln_g, w_in_all, [chip_lo], [("chip", 0)])

    small = _pack_small({R_LN: d_ln_p, R_CONVB: d_cb_p, R_CNG: d_cng_p, R_CNB: d_cnb_p,
                         R_BPW2: d_b2_p, R_LB0: d_lb_p, R_ON: d_on_p, R_PEN: d_pen_p,
                         R_FIN: d_fin_p, R_LOSS: loss_p})
    (small_all,) = _exchange_call("gather_small", [small], ["gather"])

    big = {}
    big["w_in"] = _sum_adam("adam_w_in", [r_w_in_lo, r_w_in_hi], w_in[0], m_w_in[0], v_w_in[0], 128)
    cw = _sum_adam("adam_conv_w", [r_conv_w], _pad_rows(conv_w[0], CONV_PAD),
                   _pad_rows(m_conv_w[0], CONV_PAD), _pad_rows(v_conv_w[0], CONV_PAD), CONV_PAD)
    big["conv_w"] = [a[:CONV_K] for a in cw]
    big["w_pw2"] = _sum_adam("adam_w_pw2", [r_w_pw2], w_pw2[0], m_w_pw2[0], v_w_pw2[0], 128)
    big["w_out"] = _sum_adam("adam_w_out", [r_w_out], w_out[0], m_w_out[0], v_w_out[0], 128)
    big["w_pg"] = _sum_adam("adam_w_pg", [r_w_pg], w_pg[0], m_w_pg[0], v_w_pg[0], 128)
    big["w_pp"] = _sum_adam("adam_w_pp", [r_w_pp], w_pp[0], m_w_pp[0], v_w_pp[0], PLE)

    small_w = [ln_g, conv_b, cnorm_g, cnorm_b, b_pw2, lb_logits, onorm_g, pe_norm_g, final_g]
    small_m = [m_ln_g, m_conv_b, m_cnorm_g, m_cnorm_b, m_b_pw2, m_lb_logits, m_onorm_g,
               m_pe_norm_g, m_final_g]
    small_v = [v_ln_g, v_conv_b, v_cnorm_g, v_cnorm_b, v_b_pw2, v_lb_logits, v_onorm_g,
               v_pe_norm_g, v_final_g]
    sg, sd, sm, sv, loss = _small_adam(small_all, lb_logits, _pack_rows(small_w),
                                       _pack_rows(small_m), _pack_rows(small_v))

    small_rows = {"ln_g": (R_LN, 1), "conv_b": (R_CONVB, 1), "cnorm_g": (R_CNG, 1),
                  "cnorm_b": (R_CNB, 1), "b_pw2": (R_BPW2, 1), "lb_logits": (R_LB0, 2),
                  "onorm_g": (R_ON, 1), "pe_norm_g": (R_PEN, 1), "final_g": (R_FIN, 1)}
    order = ["ln_g", "w_in", "conv_w", "conv_b", "cnorm_g", "cnorm_b", "w_pw2", "b_pw2",
             "lb_logits", "onorm_g", "w_out", "pe_norm_g", "w_pg", "w_pp", "final_g"]

    def leaf(kind, name):
        if name in big:
            return big[name][kind][None]
        r0, n = small_rows[name]
        a = (sg, sd, sm, sv)[kind][r0:r0 + n]
        return a.reshape(D) if name == "final_g" else a

    outs = [loss.reshape(()), grad_x.reshape(1, t, D)]
    for kind in range(4):
        outs += [leaf(kind, name) for name in order]
    return tuple(outs)
```

```python
import functools

import jax
import jax.numpy as jnp
from jax import lax
from jax.experimental import pallas as pl
from jax.experimental.pallas import tpu as pltpu

F32 = jnp.float32
BF16 = jnp.bfloat16
MESH = pl.DeviceIdType.MESH

N_DEV = 8
D = 1024
N_COLS = 7 * D
COLS_PER_DEV = N_COLS // N_DEV
PLE = 256
HEAD = 128
N_HEADS = D // HEAD
CONV_K = 31
CONV_PAD = 32
CHUNK = 64
EPS = 1e-6
SUBLANES = 8

ADAM_LR = 0.001
ADAM_B1 = 0.9
ADAM_B2 = 0.999
ADAM_EPS = 1e-08
ADAM_WD = 0.01
ADAM_STEP = 10

MIB = 1024 * 1024
N_SMALL = 16
R_LN, R_CONVB, R_CNG, R_CNB, R_BPW2, R_LB0, R_LB1, R_ON, R_PEN, R_FIN, R_LOSS = range(11)


def _params(vmem_mib, **kw):
    return pltpu.CompilerParams(vmem_limit_bytes=vmem_mib * MIB, **kw)


def _dot(a, b):
    return jnp.dot(a.astype(BF16), b.astype(BF16), preferred_element_type=F32)


def _dot_nt(a, b):
    return lax.dot_general(a.astype(BF16), b.astype(BF16), (((1,), (1,)), ((), ())),
                           preferred_element_type=F32)


def _dot_tn(a, b):
    return lax.dot_general(a.astype(BF16), b.astype(BF16), (((0,), (0,)), ((), ())),
                           preferred_element_type=F32)


def _split(a):
    hi = a.astype(BF16)
    return hi, (a - hi.astype(F32)).astype(BF16)


def _dot_split(a, b, dims):
    ah, al = _split(a)
    bh, bl = _split(b)
    dg = lambda p, q: lax.dot_general(p, q, dims, preferred_element_type=F32)
    return dg(ah, bh) + (dg(ah, bl) + dg(al, bh))


def _sigmoid(x):
    return 1.0 / (1.0 + jnp.exp(-x))


def _rowsum8(a):
    r, c = a.shape
    return jnp.sum(a.reshape(r // SUBLANES, SUBLANES, c), axis=0)


def _tri_dot(tri, a):
    hi = a.astype(BF16)
    r1 = a - hi.astype(F32)
    mid = r1.astype(BF16)
    lo = (r1 - mid.astype(F32)).astype(BF16)
    return (jnp.dot(tri, hi, preferred_element_type=F32)
            + jnp.dot(tri, mid, preferred_element_type=F32)
            + jnp.dot(tri, lo, preferred_element_type=F32))


def _lower_bound(lbl):
    l0, l1 = lbl[0:1, :], lbl[1:2, :]
    m = jnp.maximum(l0, l1)
    e0, e1 = jnp.exp(l0 - m), jnp.exp(l1 - m)
    s = e0 + e1
    return e0 / s, e1 / s


ANY = pl.BlockSpec(memory_space=pl.ANY)


def _full(shape):
    return pl.BlockSpec(shape, lambda i: (0,) * len(shape))


def _peer(x, y, c, k):
    px = 1 - x if k & 4 else x
    py = 1 - y if k & 2 else y
    pc = 1 - c if k & 1 else c
    return (px, py, pc), 4 * px + 2 * py + pc


class _Exchange:
    def __init__(self, srcs, outs, modes, send_sems, recv_sems, local_sems):
        x, y, c = lax.axis_index("x"), lax.axis_index("y"), lax.axis_index("c")
        me = 4 * x + 2 * y + c
        self.starts, self.send_waits, self.recv_waits = [], [], []

        def remote(a, k, src, slot, peer, when):
            sem = a * N_DEV + k
            cp = pltpu.make_async_remote_copy(
                src_ref=src, dst_ref=outs[a].at[slot], send_sem=send_sems.at[sem],
                recv_sem=recv_sems.at[sem], device_id=peer, device_id_type=MESH)
            self.starts.append((when, cp.start))
            self.send_waits.append((when, cp.wait_send))

        def arrival(a, k, slot, when):
            sem = a * N_DEV + k
            cp = pltpu.make_async_remote_copy(
                src_ref=outs[a].at[slot], dst_ref=outs[a].at[slot], send_sem=send_sems.at[sem],
                recv_sem=recv_sems.at[sem], device_id=(x, y, c), device_id_type=MESH)
            self.recv_waits.append((when, cp.wait_recv))

        def local(a, src, slot, when):
            cp = pltpu.make_async_copy(src, outs[a].at[slot], local_sems.at[a])
            self.starts.append((when, cp.start))
            self.send_waits.append((when, cp.wait))

        for a, (src, mode) in enumerate(zip(srcs, modes)):
            if mode in ("gather", "scatter"):
                local(a, src if mode == "gather" else src.at[me], me, None)
                for k in range(1, N_DEV):
                    peer, peer_idx = _peer(x, y, c, k)
                    remote(a, k, src if mode == "gather" else src.at[peer_idx], me, peer, None)
                    arrival(a, k, peer_idx, None)
                continue
            if isinstance(mode, tuple):
                here, away = x == mode[1], x != mode[1]
                chip = 2 * x + y
                local(a, src.at[y], chip, here)
                remote(a, 1, src.at[1 - y], chip, (x, 1 - y, c), here)
                remote(a, 2, src.at[y], chip, (1 - x, y, c), away)
                remote(a, 3, src.at[1 - y], chip, (1 - x, 1 - y, c), away)
                arrival(a, 1, 2 * x + 1 - y, here)
                arrival(a, 2, 2 * (1 - x) + y, here)
                arrival(a, 3, 2 * (1 - x) + 1 - y, here)
                continue
            here, away = x == mode, x != mode
            for kk in range(4):
                py = 1 - y if kk & 2 else y
                pc = 1 - c if kk & 1 else c
                block = src.at[2 * py + pc]
                if kk == 0:
                    local(a, block, me, here)
                else:
                    remote(a, kk, block, me, (x, py, pc), here)
                remote(a, 4 + kk, block, me, (1 - x, py, pc), away)
            for k in range(1, N_DEV):
                arrival(a, k, _peer(x, y, c, k)[1], here)

    @staticmethod
    def _run(actions):
        for when, fn in actions:
            if when is None:
                fn()
            else:
                pl.when(when)(fn)

    def start(self):
        self._run(self.starts)

    def wait(self):
        self._run(self.recv_waits)
        self._run(self.send_waits)


def _exchange_scratch(n):
    return [pltpu.SemaphoreType.DMA((n * N_DEV,)), pltpu.SemaphoreType.DMA((n * N_DEV,)),
            pltpu.SemaphoreType.DMA((n,))]


def _recv_shapes(srcs, modes):
    def shape(s, m):
        if m == "gather":
            return (N_DEV,) + s.shape
        return (N_DEV // 2 if isinstance(m, tuple) else N_DEV,) + s.shape[1:]

    return [jax.ShapeDtypeStruct(shape(s, m), s.dtype) for s, m in zip(srcs, modes)]


def _pair_reduce(name, blocks):
    shape = (2,) + blocks.shape[1:]

    def body(src, out_ref, stage, mine, send_sems, recv_sems, local_sems):
        x, y, c = lax.axis_index("x"), lax.axis_index("y"), lax.axis_index("c")
        sends, waits = [], []
        for py in range(2):
            sends.append(pltpu.make_async_remote_copy(
                src_ref=src.at[2 * py + 1 - c], dst_ref=stage.at[py], send_sem=send_sems.at[py],
                recv_sem=recv_sems.at[py], device_id=(x, y, 1 - c), device_id_type=MESH))
            waits.append(pltpu.make_async_copy(src.at[2 * py + c], mine.at[py], local_sems.at[py]))
        for cp in sends + waits:
            cp.start()
        for cp in waits:
            cp.wait()
        for cp in sends:
            cp.wait_recv()
        out_ref[...] = mine[...] + stage[...]
        for cp in sends:
            cp.wait_send()

    return pl.pallas_call(
        body, name=name, out_shape=jax.ShapeDtypeStruct(shape, F32), in_specs=[ANY],
        scratch_shapes=[pltpu.VMEM(shape, F32), pltpu.VMEM(shape, F32),
                        pltpu.SemaphoreType.DMA((2,)), pltpu.SemaphoreType.DMA((2,)),
                        pltpu.SemaphoreType.DMA((2,))],
        compiler_params=_params(40),
    )(blocks)


def _exchange_call(name, srcs, modes):
    n = len(srcs)

    def body(*refs):
        xch = _Exchange(refs[:n], refs[n:2 * n], modes, *refs[2 * n:])
        xch.start()
        xch.wait()

    return pl.pallas_call(
        body, name=name, out_shape=_recv_shapes(srcs, modes),
        in_specs=[ANY] * n, out_specs=[ANY] * n, scratch_shapes=_exchange_scratch(n),
    )(*srcs)


def _hosted(body, n_in, n_out, grid, modes):
    n = len(modes)

    def hosted(*refs):
        ins, srcs = refs[:n_in], refs[n_in:n_in + n]
        outs = refs[n_in + n:n_in + n + n_out]
        bufs = refs[n_in + n + n_out:n_in + 2 * n + n_out]
        scratch = refs[n_in + 2 * n + n_out:-3]
        xch = _Exchange(srcs, bufs, modes, *refs[-3:])
        first, last = True, True
        for axis, size in enumerate(grid):
            first = jnp.logical_and(first, pl.program_id(axis) == 0)
            last = jnp.logical_and(last, pl.program_id(axis) == size - 1)
        pl.when(first)(xch.start)
        body(*ins, *outs, *scratch)
        pl.when(last)(xch.wait)

    return hosted


N_CHIPS = N_DEV // 2
PAIR_COLS = 2 * COLS_PER_DEV
PUSHED = (1, 2, 4, 6)
FORWARDED = (2, 4, 6)
NORM_ROWS = 32


def _inproj_fwd(x, ln_g, w_shard):
    t = x.shape[0]
    tt = min(512, t)
    n_t = t // tt
    chip = 2 * lax.axis_index("x") + lax.axis_index("y")
    order = jnp.bitwise_xor(chip, jnp.arange(N_CHIPS, dtype=jnp.int32)).astype(jnp.int32)

    def body(order_ref, x_ref, g_ref, shard_hbm, z_ref, u_ref, w_all,
             u_all, w_blk, w_send, w_recv, w_local):
        p, i = pl.program_id(0), pl.program_id(1)
        x, y, c = lax.axis_index("x"), lax.axis_index("y"), lax.axis_index("c")
        mine = 4 * x + 2 * y + c

        def push(k):
            peer, _ = _peer(x, y, c, k)
            return pltpu.make_async_remote_copy(
                src_ref=shard_hbm, dst_ref=w_all.at[mine], send_sem=w_send.at[k],
                recv_sem=w_recv.at[k], device_id=peer, device_id_type=MESH)

        def forward(k):
            _, owner = _peer(x, y, c, k)
            return pltpu.make_async_remote_copy(
                src_ref=w_all.at[owner], dst_ref=w_all.at[owner], send_sem=w_send.at[k + 1],
                recv_sem=w_recv.at[k + 1], device_id=(x, y, 1 - c), device_id_type=MESH)

        def landed(k):
            _, owner = _peer(x, y, c, k)
            return pltpu.make_async_remote_copy(
                src_ref=w_all.at[owner], dst_ref=w_all.at[owner], send_sem=w_send.at[k],
                recv_sem=w_recv.at[k], device_id=(x, y, c), device_id_type=MESH)

        keep = pltpu.make_async_copy(shard_hbm, w_all.at[mine], w_local.at[0])

        def load_pair(step):
            same = shard_hbm if step == 0 else w_all.at[_peer(x, y, c, 2 * step)[1]]
            other = w_all.at[_peer(x, y, c, 2 * step + 1)[1]]
            for side in range(2):
                @pl.when(c == side)
                def _(side=side):
                    pltpu.sync_copy(same, w_blk.at[:, pl.ds(COLS_PER_DEV * side, COLS_PER_DEV)])
                    pltpu.sync_copy(
                        other, w_blk.at[:, pl.ds(COLS_PER_DEV * (1 - side), COLS_PER_DEV)])

        @pl.when(jnp.logical_and(p == 0, i == 0))
        def _():
            for k in PUSHED[:-1]:
                push(k).start()
            keep.start()

        @pl.when(jnp.logical_and(p == 1, i == 0))
        def _():
            for k in PUSHED[1:-1]:
                push(k).wait_send()
            push(PUSHED[-1]).start()

        for step in range(N_CHIPS):
            @pl.when(jnp.logical_and(p == step, i == 0))
            def _(step=step):
                landed(2 * step + 1).wait_recv()
                load_pair(step)

        rows = pl.ds(pl.multiple_of(i * tt, tt), tt)

        @pl.when(p == 0)
        def _():
            def norm_rows(r, carry):
                sub = pl.ds(pl.multiple_of(r * NORM_ROWS, NORM_ROWS), NORM_ROWS)
                xv = x_ref[sub, :]
                rstd = lax.rsqrt(jnp.mean(xv * xv, axis=-1, keepdims=True) + EPS)
                ub = (xv * rstd * g_ref[...]).astype(BF16)
                u_ref[sub, :] = ub
                u_all[pl.ds(pl.multiple_of(i * tt + r * NORM_ROWS, NORM_ROWS), NORM_ROWS), :] = ub
                return carry

            lax.fori_loop(0, tt // NORM_ROWS, norm_rows, 0, unroll=2)

        z_ref[...] = jnp.dot(u_all[rows, :], w_blk[...], preferred_element_type=F32)

        for step in range(1, N_CHIPS):
            @pl.when(jnp.logical_and(p == step - 1, i == n_t - 1))
            def _(step=step):
                landed(2 * step).wait_recv()
                forward(2 * step).start()

        @pl.when(jnp.logical_and(p == N_CHIPS - 1, i == n_t - 1))
        def _():
            push(PUSHED[0]).wait_send()
            push(PUSHED[-1]).wait_send()
            for k in FORWARDED:
                forward(k).wait_send()
            keep.wait()

    first_pass = lambda p, i, order_ref: (jnp.where(p == 0, i, n_t - 1), 0)
    grid_spec = pltpu.PrefetchScalarGridSpec(
        num_scalar_prefetch=1, grid=(N_CHIPS, n_t),
        in_specs=[pl.BlockSpec((tt, D), first_pass),
                  pl.BlockSpec((1, D), lambda p, i, order_ref: (0, 0)), ANY],
        out_specs=[pl.BlockSpec((tt, PAIR_COLS), lambda p, i, order_ref: (i, order_ref[p])),
                   pl.BlockSpec((tt, D), first_pass), ANY],
        scratch_shapes=[pltpu.VMEM((t, D), BF16), pltpu.VMEM((D, PAIR_COLS), BF16),
                        pltpu.SemaphoreType.DMA((N_DEV,)), pltpu.SemaphoreType.DMA((N_DEV,)),
                        pltpu.SemaphoreType.DMA((1,))])
    return pl.pallas_call(
        body, name="inproj_fwd", grid_spec=grid_spec,
        out_shape=[jax.ShapeDtypeStruct((t, N_COLS), F32), jax.ShapeDtypeStruct((t, D), BF16),
                   jax.ShapeDtypeStruct((N_DEV,) + w_shard.shape, BF16)],
        compiler_params=_params(48, dimension_semantics=("arbitrary", "arbitrary")),
    )(order, x, ln_g, w_shard)


def _shifted_copies(buf, shifted, rows):
    for b in range(1, SUBLANES):
        shifted[b, 0:rows, :] = buf[b:b + rows, :]


def _tap_ref(buf, shifted, offset):
    a, b = divmod(offset, SUBLANES)
    return (buf if b == 0 else shifted.at[b]), SUBLANES * a


def _tap_slabs(buf, shifted, offset_of_tap):
    groups = {}
    for k in range(CONV_K):
        a, b = divmod(offset_of_tap(k), SUBLANES)
        groups.setdefault(b, []).append((SUBLANES * a, k))
    out = []
    for b, taps in sorted(groups.items()):
        taps.sort()
        lo = taps[0][0]
        out.append((buf if b == 0 else shifted.at[b], lo, [(k, off - lo) for off, k in taps]))
    return out


def _group_norm_stats(blk):
    mu = jnp.mean(blk, axis=-1, keepdims=True)
    cen = blk - mu
    var = jnp.mean(cen * cen, axis=-1, keepdims=True)
    return cen * lax.rsqrt(var + EPS)


def _conv_fwd(z, conv_w_all, conv_b, cn_g, cn_b, w_pw2, b_pw2):
    t = z.shape[0]
    tt = min(256, t)
    rc = 128

    def body(val_ref, glu_ref, gate_ref, cw_ref, cb_ref, g_ref, b_ref, w_hbm, b2_ref,
             yc_ref, y2_ref, yo_ref, w_vmem, vbuf, vsh, y1buf):
        @pl.when(pl.program_id(0) == 0)
        def _():
            pltpu.sync_copy(w_hbm, w_vmem)
            vbuf[0:CONV_PAD, :] = jnp.zeros((CONV_PAD, D), F32)

        vbuf[CONV_PAD:CONV_PAD + tt, :] = val_ref[...] * _sigmoid(glu_ref[...])
        _shifted_copies(vbuf, vsh, tt + 24)

        for g in range(N_HEADS):
            cs = slice(HEAD * g, HEAD * (g + 1))

            def row_chunk(r, carry, g=g, cs=cs):
                r0 = pl.multiple_of(r * rc, rc)
                acc = jnp.broadcast_to(cb_ref[:, cs], (rc, HEAD))
                for ref, lo, taps in _tap_slabs(vbuf, vsh, lambda k: k + 2):
                    slab = ref[pl.ds(r0 + lo, rc + taps[-1][1]), cs]
                    for k, off in taps:
                        acc = acc + cw_ref[g, k:k + 1, :] * slab[off:off + rc]
                yc_ref[pl.ds(r0, rc), cs] = acc
                n = _group_norm_stats(acc) * g_ref[:, cs] + b_ref[:, cs]
                y1buf[pl.ds(r0, rc), cs] = (n * _sigmoid(n)).astype(BF16)
                return carry

            lax.fori_loop(0, tt // rc, row_chunk, 0, unroll=True)
        vbuf[0:CONV_PAD, :] = vbuf[tt:tt + CONV_PAD, :]
        y2 = jnp.dot(y1buf[...], w_vmem[...], preferred_element_type=F32) + b2_ref[...]
        y2_ref[...] = y2
        gate = gate_ref[...]
        yo_ref[...] = (y2 * gate * _sigmoid(gate)).astype(BF16)

    col = lambda j: pl.BlockSpec((tt, D), lambda i: (i, j))
    row = pl.BlockSpec((tt, D), lambda i: (i, 0))
    return pl.pallas_call(
        body, name="conv_fwd", grid=(t // tt,),
        out_shape=[jax.ShapeDtypeStruct((t, D), F32), jax.ShapeDtypeStruct((t, D), F32),
                   jax.ShapeDtypeStruct((t, D), BF16)],
        in_specs=[col(0), col(1), col(2), _full((N_DEV, CONV_PAD, HEAD)), _full((1, D)),
                  _full((1, D)), _full((1, D)), ANY, _full((1, D))],
        out_specs=[row, row, row],
        scratch_shapes=[pltpu.VMEM((D, D), BF16), pltpu.VMEM((tt + CONV_PAD, D), F32),
                        pltpu.VMEM((SUBLANES, tt + CONV_PAD, D), F32), pltpu.VMEM((tt, D), BF16)],
        compiler_params=_params(48, dimension_semantics=("arbitrary",)),
    )(z, z, z, conv_w_all, conv_b, cn_g, cn_b, w_pw2, b_pw2)


HEAD_GROUP = 8
_HEAD_LANES = [slice(HEAD * j, HEAD * (j + 1)) for j in range(HEAD_GROUP)]


def _head_mean(a):
    return jnp.concatenate(
        [jnp.broadcast_to(jnp.mean(a[:, hs], axis=-1, keepdims=True), (a.shape[0], HEAD))
         for hs in _HEAD_LANES], axis=1)


def _chunk_quantities(zq, zf, lbh, tri):
    sig = _sigmoid(zf)
    sig_neg = _sigmoid(-zf)
    f = lbh + (1.0 - lbh) * sig
    k = (1.0 - lbh) * sig_neg
    q = zq * _sigmoid(zq)
    b = _tri_dot(tri, jnp.log(f))
    b_mid = b[CHUNK // 2 - 1:CHUNK // 2, :]
    b_last = b[CHUNK - 1:CHUNK, :]
    e_q = jnp.exp(b)
    e_qm = jnp.exp(b - b_mid)
    e_km = jnp.exp(b_mid - b)
    e_kd = jnp.exp(b_last - b)
    return q, k, f, sig, sig_neg, e_q, e_qm, e_km, e_kd, jnp.exp(b_last)


def _hgrn_fwd(z, lb_logits, onorm_g, shards):
    t = z.shape[0]
    tt = min(256, t)
    nc = tt // CHUNK
    modes = ["gather"] * len(shards)
    n = len(modes)

    def body(q_ref, f_ref, i_ref, g_ref, lbl_ref, on_ref, o_ref, y_ref, s_ref, st):
        @pl.when(pl.program_id(0) == 0)
        def _():
            st[...] = jnp.zeros_like(st)

        lb, _ = _lower_bound(lbl_ref[...])
        rows = lax.broadcasted_iota(jnp.int32, (CHUNK, CHUNK), 0)
        cols = lax.broadcasted_iota(jnp.int32, (CHUNK, CHUNK), 1)
        causal = rows >= cols
        tri = causal.astype(BF16)

        def chunk(c, carry):
            r0 = pl.multiple_of(c * CHUNK, CHUNK)
            rs = pl.ds(r0, CHUNK)
            for h0 in range(0, N_HEADS, HEAD_GROUP):
                cs = slice(HEAD * h0, HEAD * (h0 + HEAD_GROUP))
                q, k, _, _, _, e_q, e_qm, e_km, e_kd, e_last = _chunk_quantities(
                    q_ref[rs, cs], f_ref[rs, cs], lb[:, cs], tri)
                v = i_ref[rs, cs].astype(BF16)
                qm, km = (q * e_qm).astype(BF16), (k * e_km).astype(BF16)
                qt, kd = (q * e_q).astype(BF16), (k * e_kd).astype(BF16)
                outs = []
                for j, hs in enumerate(_HEAD_LANES):
                    s_old = st[h0 + j]
                    s_ref[c, h0 + j] = s_old.astype(BF16)
                    a = jnp.where(causal, _dot_nt(qm[:, hs], km[:, hs]), 0.0)
                    outs.append(_dot_nt(qt[:, hs], s_old) + _dot(a, v[:, hs]))
                    st[h0 + j] = s_old * e_last[:, hs] + _dot_tn(v[:, hs], kd[:, hs])
                o = jnp.concatenate(outs, axis=1)
                o_ref[rs, cs] = o
                n = o * lax.rsqrt(_head_mean(o * o) + EPS)
                zg = g_ref[rs, cs]
                y_ref[rs, cs] = (n * on_ref[:, cs] * zg * _sigmoid(zg)).astype(BF16)
            return carry

        lax.fori_loop(0, nc, chunk, 0, unroll=True)

    col = lambda j: pl.BlockSpec((tt, D), lambda i: (i, j))
    row = pl.BlockSpec((tt, D), lambda i: (i, 0))
    return pl.pallas_call(
        _hosted(body, 6, 3, (t // tt,), modes), name="hgrn_fwd", grid=(t // tt,),
        out_shape=[jax.ShapeDtypeStruct((t, D), F32), jax.ShapeDtypeStruct((t, D), BF16),
                   jax.ShapeDtypeStruct((t // CHUNK, N_HEADS, HEAD, HEAD), BF16)]
        + _recv_shapes(shards, modes),
        in_specs=[col(3), col(4), col(5), col(6), _full((2, D)), _full((1, D))] + [ANY] * n,
        out_specs=[row, row, pl.BlockSpec((nc, N_HEADS, HEAD, HEAD), lambda i: (i, 0, 0, 0))]
        + [ANY] * n,
        scratch_shapes=[pltpu.VMEM((N_HEADS, HEAD, HEAD), F32)] + _exchange_scratch(n),
        compiler_params=_params(40, dimension_semantics=("arbitrary",)),
    )(z, z, z, z, lb_logits, onorm_g, *shards)


def _rms_bwd(dn, xhat, rstd):
    return rstd * (dn - xhat * jnp.mean(dn * xhat, axis=-1, keepdims=True))


def _tail(x, y_conv, y_hgrn, p, target, w_out, w_pg, w_pp_all, pe_g, fin_g):
    t = x.shape[0]
    tt = min(256, t)
    n_steps = t // tt

    def body(x_ref, yc_ref, yh_ref, p_ref, tg_ref, wo_hbm, wg_hbm, wp_hbm, pg_ref, fg_ref,
             dh1_ref, dyc_ref, dyh_ref, dwo_hbm, dwg_hbm, dwp_hbm, dpg_ref, dfg_ref, loss_ref,
             wo, wg, wp, dwo, dwg, dwp):
        i = pl.program_id(0)

        @pl.when(i == 0)
        def _():
            pltpu.sync_copy(wo_hbm, wo)
            pltpu.sync_copy(wg_hbm, wg)
            for d in range(N_DEV):
                pltpu.sync_copy(wp_hbm.at[d], wp.at[:, pl.ds(HEAD * d, HEAD)])
            dwo[...] = jnp.zeros_like(dwo)
            dwg[...] = jnp.zeros_like(dwg)
            dwp[...] = jnp.zeros_like(dwp)
            dpg_ref[...] = jnp.zeros_like(dpg_ref)
            dfg_ref[...] = jnp.zeros_like(dfg_ref)
            loss_ref[...] = jnp.zeros_like(loss_ref)

        ycv, yhv = yc_ref[...], yh_ref[...]
        h1 = (x_ref[...] + jnp.dot(ycv, wo[0:D, :], preferred_element_type=F32)
              + jnp.dot(yhv, wo[D:2 * D, :], preferred_element_type=F32))
        pb = p_ref[...].astype(BF16)
        pe = jnp.dot(pb, wp[...], preferred_element_type=F32)
        rstd1 = lax.rsqrt(jnp.mean(h1 * h1, axis=-1, keepdims=True) + EPS)
        n1 = h1 * rstd1
        rb = (n1 * pg_ref[...]).astype(BF16)
        gate = _sigmoid(jnp.dot(rb, wg[...], preferred_element_type=F32))
        h2 = h1 + gate * pe
        rstd2 = lax.rsqrt(jnp.mean(h2 * h2, axis=-1, keepdims=True) + EPS)
        n2 = h2 * rstd2
        err = n2 * fg_ref[...] - tg_ref[...]
        loss_ref[...] += _rowsum8(err * err)

        d_out = err * (1.0 / D)
        dfg_ref[...] += _rowsum8(d_out * n2)
        d_h2 = _rms_bwd(d_out * fg_ref[...], n2, rstd2)
        d_pe = (d_h2 * gate).astype(BF16)
        d_gpre = (d_h2 * pe * gate * (1.0 - gate)).astype(BF16)
        dwg[...] += _dot_tn(rb, d_gpre)
        dwp[...] += _dot_tn(pb, d_pe)
        dr = _dot_nt(d_gpre, wg[...])
        dpg_ref[...] += _rowsum8(dr * n1)
        d_h1 = d_h2 + _rms_bwd(dr * pg_ref[...], n1, rstd1)
        dh1_ref[...] = d_h1
        d_h1b = d_h1.astype(BF16)
        dwo[0:D, :] += _dot_tn(ycv, d_h1b)
        dwo[D:2 * D, :] += _dot_tn(yhv, d_h1b)
        dyc_ref[...] = _dot_nt(d_h1b, wo[0:D, :])
        dyh_ref[...] = _dot_nt(d_h1b, wo[D:2 * D, :])

        @pl.when(i == n_steps - 1)
        def _():
            pltpu.sync_copy(dwo, dwo_hbm)
            pltpu.sync_copy(dwg, dwg_hbm)
            for d in range(N_DEV):
                pltpu.sync_copy(dwp.at[:, pl.ds(HEAD * d, HEAD)], dwp_hbm.at[d])

    row = pl.BlockSpec((tt, D), lambda i: (i, 0))
    acc = _full((SUBLANES, D))
    return pl.pallas_call(
        body, name="tail_fwd_bwd", grid=(n_steps,),
        out_shape=[jax.ShapeDtypeStruct((t, D), F32)] * 3
        + [jax.ShapeDtypeStruct((2 * D, D), F32), jax.ShapeDtypeStruct((D, D), F32),
           jax.ShapeDtypeStruct((N_DEV, PLE, HEAD), F32)]
        + [jax.ShapeDtypeStruct((SUBLANES, D), F32)] * 3,
        in_specs=[row, row, row, pl.BlockSpec((tt, PLE), lambda i: (i, 0)), row,
                  ANY, ANY, ANY, _full((1, D)), _full((1, D))],
        out_specs=[row, row, row, ANY, ANY, ANY, acc, acc, acc],
        scratch_shapes=[pltpu.VMEM((2 * D, D), BF16), pltpu.VMEM((D, D), BF16),
                        pltpu.VMEM((PLE, D), BF16), pltpu.VMEM((2 * D, D), F32),
                        pltpu.VMEM((D, D), F32), pltpu.VMEM((PLE, D), F32)],
        compiler_params=_params(52, dimension_semantics=("arbitrary",)),
    )(x, y_conv, y_hgrn, p, target, w_out, w_pg, w_pp_all, pe_g, fin_g)


def _hgrn_bwd(dy, z, o_raw, states, lb_logits, onorm_g, grads):
    t = z.shape[0]
    tt = min(256, t)
    nc = tt // CHUNK
    n_steps = t // tt
    modes = ["scatter"] * len(grads)

    def body(dy_ref, q_ref, f_ref, i_ref, g_ref, o_ref, s_ref, lbl_ref, on_ref,
             dz_ref, don_ref, dlb_ref, dst):
        @pl.when(pl.program_id(0) == 0)
        def _():
            dst[...] = jnp.zeros_like(dst)
            don_ref[...] = jnp.zeros_like(don_ref)
            dlb_ref[...] = jnp.zeros_like(dlb_ref)

        lb, _ = _lower_bound(lbl_ref[...])
        rows = lax.broadcasted_iota(jnp.int32, (CHUNK, CHUNK), 0)
        cols = lax.broadcasted_iota(jnp.int32, (CHUNK, CHUNK), 1)
        causal = rows >= cols
        tri = causal.astype(BF16)
        tri_rev = (rows <= cols).astype(BF16)
        width = HEAD * HEAD_GROUP
        is_last = lax.broadcasted_iota(jnp.int32, (CHUNK, width), 0) == CHUNK - 1
        nn = (((1,), (0,)), ((), ()))
        tn = (((0,), (0,)), ((), ()))
        dg = functools.partial(lax.dot_general, preferred_element_type=F32)

        def chunk(cc, carry):
            c = nc - 1 - cc
            r0 = pl.multiple_of(c * CHUNK, CHUNK)
            rs = pl.ds(r0, CHUNK)
            for h0 in range(0, N_HEADS, HEAD_GROUP):
                cs = slice(HEAD * h0, HEAD * h0 + width)
                zq, zf, zg = q_ref[rs, cs], f_ref[rs, cs], g_ref[rs, cs]
                lbh = lb[:, cs]
                q, k, f, sig, sig_neg, e_q, e_qm, e_km, e_kd, e_last = _chunk_quantities(
                    zq, zf, lbh, tri)
                vb = i_ref[rs, cs].astype(BF16)
                qt, qm, km, kd = q * e_q, q * e_qm, k * e_km, k * e_kd
                qt_b, kd_b = qt.astype(BF16), kd.astype(BF16)
                qm_h, qm_l = _split(qm)
                km_h, km_l = _split(km)

                o = o_ref[rs, cs]
                rstd = lax.rsqrt(_head_mean(o * o) + EPS)
                n = o * rstd
                sg = _sigmoid(zg)
                dyv = dy_ref[rs, cs]
                on = on_ref[:, cs]
                d_zg = dyv * n * on * sg * (1.0 + zg * (1.0 - sg))
                d_on = dyv * zg * sg
                don_ref[:, cs] += _rowsum8(d_on * n)
                dn = d_on * on
                do_b = (rstd * (dn - n * _head_mean(dn * n))).astype(BF16)

                dv, dkd, dqt, dqm, dkm, s_dots = [], [], [], [], [], []
                for j, hs in enumerate(_HEAD_LANES):
                    s_old, ds_new = s_ref[c, h0 + j], dst[h0 + j]
                    ds_b = ds_new.astype(BF16)
                    a = jnp.where(causal, _dot_nt(qm_h[:, hs], km_h[:, hs]), 0.0)
                    da = jnp.where(causal, _dot_nt(do_b[:, hs], vb[:, hs]), 0.0)
                    dv.append(_dot_tn(a, do_b[:, hs]) + _dot_nt(kd_b[:, hs], ds_b))
                    dkd.append(_dot(vb[:, hs], ds_b))
                    dqt.append(_dot(do_b[:, hs], s_old))
                    da_h, da_l = _split(da)
                    dqm.append(dg(da_h, km_h[:, hs], nn)
                               + (dg(da_h, km_l[:, hs], nn) + dg(da_l, km_h[:, hs], nn)))
                    dkm.append(dg(da_h, qm_h[:, hs], tn)
                               + (dg(da_h, qm_l[:, hs], tn) + dg(da_l, qm_h[:, hs], tn)))
                    dst[h0 + j] = ds_new * e_last[:, hs] + _dot_tn(do_b[:, hs], qt_b[:, hs])
                    s_dots.append(jnp.sum(s_old.astype(F32) * ds_new, axis=0, keepdims=True))
                dv, dkd, dqt, dqm, dkm, s_dots = [
                    jnp.concatenate(parts, axis=1) for parts in (dv, dkd, dqt, dqm, dkm, s_dots)]
                dq = dqt * e_q + dqm * e_qm
                dk = dkm * e_km + dkd * e_kd
                last = jnp.sum(dkd * kd, axis=0, keepdims=True) + e_last * s_dots
                db = q * dq - k * dk + jnp.where(is_last, last, 0.0)
                dlogf = _tri_dot(tri_rev, db)
                common = sig_neg * (dlogf / f - dk)
                dlb_ref[:, cs] += _rowsum8(common)
                c0 = 3 * D + HEAD * h0
                sq = _sigmoid(zq)
                dz_ref[rs, c0:c0 + width] = (dq * sq * (1.0 + zq * (1.0 - sq))).astype(BF16)
                dz_ref[rs, D + c0:D + c0 + width] = ((1.0 - lbh) * sig * common).astype(BF16)
                dz_ref[rs, 2 * D + c0:2 * D + c0 + width] = dv.astype(BF16)
                dz_ref[rs, 3 * D + c0:3 * D + c0 + width] = d_zg.astype(BF16)
            return carry

        lax.fori_loop(0, nc, chunk, 0, unroll=True)

    rev = lambda i: n_steps - 1 - i
    col = lambda j: pl.BlockSpec((tt, D), lambda i: (rev(i), j))
    row = pl.BlockSpec((tt, D), lambda i: (rev(i), 0))
    acc = _full((SUBLANES, D))
    n = len(modes)
    return pl.pallas_call(
        _hosted(body, 9, 3, (n_steps,), modes), name="hgrn_bwd", grid=(n_steps,),
        out_shape=[jax.ShapeDtypeStruct((t, N_COLS), BF16),
                   jax.ShapeDtypeStruct((SUBLANES, D), F32),
                   jax.ShapeDtypeStruct((SUBLANES, D), F32)] + _recv_shapes(grads, modes),
        in_specs=[row, col(3), col(4), col(5), col(6), row,
                  pl.BlockSpec((nc, N_HEADS, HEAD, HEAD), lambda i: (rev(i), 0, 0, 0)),
                  _full((2, D)), _full((1, D))] + [ANY] * n,
        out_specs=[pl.BlockSpec((tt, N_COLS), lambda i: (rev(i), 0)), acc, acc] + [ANY] * n,
        scratch_shapes=[pltpu.VMEM((N_HEADS, HEAD, HEAD), F32)] + _exchange_scratch(n),
        compiler_params=_params(48, dimension_semantics=("arbitrary",)),
    )(dy, z, z, z, z, o_raw, states, lb_logits, onorm_g, *grads)


def _conv_bwd(dy, z, yc, y2, conv_w_all, cn_g, cn_b, w_pw2, dz, grads, modes):
    t = z.shape[0]
    tt = min(256, t)
    rc = 32
    n_steps = t // tt

    def body(dy_ref, val_ref, glu_ref, gate_ref, yc_ref, y2_ref, cw_ref, g_ref, b_ref, w_hbm,
             dz_in, dz_ref, dw_hbm, dcw_out, db2_ref, dg_ref, dbeta_ref, dcb_ref,
             w_vmem, dw, dbuf, dsh, y1buf, dnbuf, dcw_ref):
        i = pl.program_id(0)

        @pl.when(i == 0)
        def _():
            pltpu.sync_copy(w_hbm, w_vmem)
            dw[...] = jnp.zeros_like(dw)
            dbuf[tt:tt + CONV_PAD, :] = jnp.zeros((CONV_PAD, D), F32)
            dcw_ref[...] = jnp.zeros_like(dcw_ref)
            dcw_out[...] = jnp.zeros_like(dcw_out)
            db2_ref[...] = jnp.zeros_like(db2_ref)
            dg_ref[...] = jnp.zeros_like(dg_ref)
            dbeta_ref[...] = jnp.zeros_like(dbeta_ref)
            dcb_ref[...] = jnp.zeros_like(dcb_ref)

        gate = gate_ref[...]
        sg = _sigmoid(gate)
        dyv = dy_ref[...]
        dy2 = dyv * gate * sg
        dz_ref[:, 2 * D:3 * D] = (dyv * y2_ref[...] * sg * (1.0 + gate * (1.0 - sg))).astype(BF16)
        db2_ref[...] += _rowsum8(dy2)
        dy2b = dy2.astype(BF16)
        dnbuf[...] = _dot_nt(dy2b, w_vmem[...])

        def norm_chunk(r, carry):
            r0 = pl.multiple_of(r * rc, rc)
            rs = pl.ds(r0, rc)
            for g in range(N_HEADS):
                cs = slice(HEAD * g, HEAD * (g + 1))
                blk = yc_ref[rs, cs]
                mu = jnp.mean(blk, axis=-1, keepdims=True)
                cen = blk - mu
                rstd = lax.rsqrt(jnp.mean(cen * cen, axis=-1, keepdims=True) + EPS)
                xhat = cen * rstd
                n = xhat * g_ref[:, cs] + b_ref[:, cs]
                sn = _sigmoid(n)
                y1buf[rs, cs] = (n * sn).astype(BF16)
                dn = dnbuf[rs, cs] * sn * (1.0 + n * (1.0 - sn))
                dg_ref[:, cs] += _rowsum8(dn * xhat)
                dbeta_ref[:, cs] += _rowsum8(dn)
                dxh = dn * g_ref[:, cs]
                dyc = rstd * (dxh - jnp.mean(dxh, axis=-1, keepdims=True)
                              - xhat * jnp.mean(dxh * xhat, axis=-1, keepdims=True))
                dcb_ref[:, cs] += _rowsum8(dyc)
                dbuf[rs, cs] = dyc
            return carry

        lax.fori_loop(0, tt // rc, norm_chunk, 0, unroll=2)
        dw[...] += _dot_tn(y1buf[...], dy2b)
        _shifted_copies(dbuf, dsh, tt + 24)

        def conv_chunk(r, carry):
            r0 = pl.multiple_of(r * rc, rc)
            rs = pl.ds(r0, rc)
            for g in range(N_HEADS):
                cs = slice(HEAD * g, HEAD * (g + 1))
                sglu = _sigmoid(glu_ref[rs, cs])
                val = val_ref[rs, cs]
                v = val * sglu
                dv = jnp.zeros((rc, HEAD), F32)
                for ref, lo, taps in _tap_slabs(dbuf, dsh, lambda k: CONV_K - 1 - k):
                    slab = ref[pl.ds(r0 + lo, rc + taps[-1][1]), cs]
                    for k, off in taps:
                        d_later = slab[off:off + rc]
                        dv = dv + cw_ref[g, k:k + 1, :] * d_later
                        dcw_ref[g, k] += _rowsum8(v * d_later)
                dz_ref[rs, cs] = (dv * sglu).astype(BF16)
                dz_ref[rs, D + HEAD * g:D + HEAD * (g + 1)] = (
                    dv * val * sglu * (1.0 - sglu)).astype(BF16)
            return carry

        lax.fori_loop(0, tt // rc, conv_chunk, 0, unroll=2)
        dbuf[tt:tt + CONV_PAD, :] = dbuf[0:CONV_PAD, :]

        @pl.when(i == n_steps - 1)
        def _():
            pltpu.sync_copy(dw, dw_hbm)
            for g in range(N_HEADS):
                for k in range(CONV_K):
                    dcw_out[g, k:k + 1, :] = jnp.sum(dcw_ref[g, k], axis=0, keepdims=True)

    rev = lambda i: n_steps - 1 - i
    col = lambda j: pl.BlockSpec((tt, D), lambda i: (rev(i), j))
    row = pl.BlockSpec((tt, D), lambda i: (rev(i), 0))
    acc = _full((SUBLANES, D))
    n = len(modes)
    return pl.pallas_call(
        _hosted(body, 11, 7, (n_steps,), modes), name="conv_bwd", grid=(n_steps,),
        out_shape=[jax.ShapeDtypeStruct((t, N_COLS), BF16), jax.ShapeDtypeStruct((D, D), F32),
                   jax.ShapeDtypeStruct((N_DEV, CONV_PAD, HEAD), F32)]
        + [jax.ShapeDtypeStruct((SUBLANES, D), F32)] * 4 + _recv_shapes(grads, modes),
        in_specs=[row, col(0), col(1), col(2), row, row, _full((N_DEV, CONV_PAD, HEAD)),
                  _full((1, D)), _full((1, D)), ANY, ANY] + [ANY] * n,
        out_specs=[pl.BlockSpec((tt, 3 * D), lambda i: (rev(i), 0)), ANY,
                   _full((N_DEV, CONV_PAD, HEAD)), acc, acc, acc, acc] + [ANY] * n,
        input_output_aliases={10: 0},
        scratch_shapes=[pltpu.VMEM((D, D), BF16), pltpu.VMEM((D, D), F32),
                        pltpu.VMEM((tt + CONV_PAD, D), F32),
                        pltpu.VMEM((SUBLANES, tt + CONV_PAD, D), F32),
                        pltpu.VMEM((tt, D), BF16), pltpu.VMEM((tt, D), F32),
                        pltpu.VMEM((N_DEV, CONV_PAD, SUBLANES, HEAD), F32)] + _exchange_scratch(n),
        compiler_params=_params(52, dimension_semantics=("arbitrary",)),
    )(dy, z, z, z, yc, y2, conv_w_all, cn_g, cn_b, w_pw2, dz, *grads)


def _inproj_bwd_dx(dz, x, d_h1, ln_g, w_in_all, grads, modes):
    t = x.shape[0]
    tt = min(256, t)

    def body(dz_ref, x_ref, dh1_ref, g_ref, w_hbm, dx_ref, dg_ref, w_vmem):
        @pl.when(pl.program_id(0) == 0)
        def _():
            for d in range(N_DEV):
                pltpu.sync_copy(w_hbm.at[d], w_vmem.at[
                    d // 2, :, pl.ds(COLS_PER_DEV * (d % 2), COLS_PER_DEV)])
            dg_ref[...] = jnp.zeros_like(dg_ref)

        du = jnp.zeros((tt, D), F32)
        for q in range(N_CHIPS):
            du = du + lax.dot_general(
                dz_ref[:, PAIR_COLS * q:PAIR_COLS * (q + 1)], w_vmem[q],
                (((1,), (1,)), ((), ())), preferred_element_type=F32)
        xv = x_ref[...]
        rstd = lax.rsqrt(jnp.mean(xv * xv, axis=-1, keepdims=True) + EPS)
        xhat = xv * rstd
        dg_ref[...] += _rowsum8(du * xhat)
        dx_ref[...] = dh1_ref[...] + _rms_bwd(du * g_ref[...], xhat, rstd)

    row = pl.BlockSpec((tt, D), lambda i: (i, 0))
    n = len(modes)
    return pl.pallas_call(
        _hosted(body, 5, 2, (t // tt,), modes), name="inproj_bwd_dx", grid=(t // tt,),
        out_shape=[jax.ShapeDtypeStruct((t, D), F32), jax.ShapeDtypeStruct((SUBLANES, D), F32)]
        + _recv_shapes(grads, modes),
        in_specs=[pl.BlockSpec((tt, N_COLS), lambda i: (i, 0)), row, row, _full((1, D)), ANY]
        + [ANY] * n,
        out_specs=[row, _full((SUBLANES, D))] + [ANY] * n,
        scratch_shapes=[pltpu.VMEM((N_CHIPS, D, PAIR_COLS), BF16)] + _exchange_scratch(n),
        compiler_params=_params(48, dimension_semantics=("arbitrary",)),
    )(dz, x, d_h1, ln_g, w_in_all, *grads)


def _inproj_bwd_dw(name, u, dz, first, count, grads=(), modes=()):
    t = u.shape[0]
    tt = min(512, t)
    grid = (count // 2, t // tt)
    n = len(modes)

    def body(u_ref, dz_ref, dw_ref):
        @pl.when(pl.program_id(1) == 0)
        def _():
            dw_ref[...] = jnp.zeros_like(dw_ref)

        both = lax.dot_general(u_ref[...], dz_ref[...], (((0,), (0,)), ((), ())),
                               preferred_element_type=F32)
        dw_ref[0] += both[:, :COLS_PER_DEV]
        dw_ref[1] += both[:, COLS_PER_DEV:]

    return pl.pallas_call(
        _hosted(body, 2, 1, grid, modes) if n else body, name=name, grid=grid,
        out_shape=[jax.ShapeDtypeStruct((count, D, COLS_PER_DEV), F32)]
        + _recv_shapes(grads, modes),
        in_specs=[pl.BlockSpec((tt, D), lambda j, i: (i, 0)),
                  pl.BlockSpec((tt, PAIR_COLS), lambda j, i: (i, first // 2 + j))] + [ANY] * n,
        out_specs=[pl.BlockSpec((2, D, COLS_PER_DEV), lambda j, i: (j, 0, 0))] + [ANY] * n,
        scratch_shapes=_exchange_scratch(n) if n else [],
        compiler_params=_params(40, dimension_semantics=("arbitrary", "arbitrary")),
    )(u, dz, *grads)


def _adamw(w, g, m, v):
    m = ADAM_B1 * m + (1.0 - ADAM_B1) * g
    v = ADAM_B2 * v + (1.0 - ADAM_B2) * (g * g)
    m_hat = m / (1.0 - ADAM_B1 ** ADAM_STEP)
    v_hat = v / (1.0 - ADAM_B2 ** ADAM_STEP)
    delta = -ADAM_LR * (m_hat / (jnp.sqrt(v_hat) + ADAM_EPS) + ADAM_WD * w)
    return delta, m, v


def _pack_small(partials):
    rows = sorted(partials)

    def body(*refs):
        ins, out_ref = refs[:-1], refs[-1]
        out_ref[...] = jnp.zeros_like(out_ref)
        for j, row in enumerate(rows):
            out_ref[row:row + 1, :] = jnp.sum(ins[j][...], axis=0, keepdims=True)

    return pl.pallas_call(
        body, name="pack_small", out_shape=jax.ShapeDtypeStruct((N_SMALL, D), F32),
    )(*[partials[row] for row in rows])


def _sum_adam(name, recvs, w, m, v, rows):
    r, c = w.shape
    n = len(recvs)

    def body(*refs):
        w_ref, m_ref, v_ref, g_ref, d_ref, mo_ref, vo_ref = refs[n:]

        def finish(recv_ref):
            g = recv_ref[0]
            for s in range(1, recv_ref.shape[0]):
                g = g + recv_ref[s]
            g_ref[...] = g
            d_ref[...], mo_ref[...], vo_ref[...] = _adamw(w_ref[...], g, m_ref[...], v_ref[...])

        if n == 1:
            finish(refs[0])
        else:
            for side in range(n):
                pl.when(lax.axis_index("x") == side)(functools.partial(finish, refs[side]))

    blk = pl.BlockSpec((rows, c), lambda i: (i, 0))
    return pl.pallas_call(
        body, name=name, grid=(r // rows,),
        out_shape=[jax.ShapeDtypeStruct((r, c), F32)] * 4,
        in_specs=[pl.BlockSpec((rv.shape[0], rows, c), lambda i: (0, i, 0)) for rv in recvs]
        + [blk, blk, blk],
        out_specs=[blk] * 4,
        compiler_params=_params(48, dimension_semantics=("arbitrary",)),
    )(*recvs, w, m, v)


def _small_adam(gathered, lb_logits, w, m, v):
    def body(ga_ref, lbl_ref, w_ref, m_ref, v_ref, g_ref, d_ref, mo_ref, vo_ref, loss_ref):
        g = ga_ref[0]
        for s in range(1, N_DEV):
            g = g + ga_ref[s]
        s0, s1 = _lower_bound(lbl_ref[...])
        d_lb = g[R_LB0:R_LB0 + 1, :]
        rows = lax.broadcasted_iota(jnp.int32, (N_SMALL, D), 0)
        g = jnp.where(rows == R_LB0, d_lb * s0 * (1.0 - s0), g)
        g = jnp.where(rows == R_LB1, -d_lb * s0 * s1, g)
        g_ref[...] = g
        d_ref[...], mo_ref[...], vo_ref[...] = _adamw(w_ref[...], g, m_ref[...], v_ref[...])
        loss_ref[...] = (0.5 / D) * jnp.sum(g[R_LOSS:R_LOSS + 1, :], axis=-1, keepdims=True)

    return pl.pallas_call(
        body, name="small_adam",
        out_shape=[jax.ShapeDtypeStruct((N_SMALL, D), F32)] * 4 + [jax.ShapeDtypeStruct((1, 1), F32)],
    )(gathered, lb_logits, w, m, v)


def _pad_rows(a, rows):
    return jnp.pad(a, ((0, rows - a.shape[0]), (0, 0)))


def _pack_rows(rows):
    rows = [r.reshape(-1, D) for r in rows]
    packed = jnp.concatenate(rows, axis=0)
    return _pad_rows(packed, N_SMALL)


def kernel(x, p, ln_g, w_in, conv_w, conv_b, cnorm_g, cnorm_b, w_pw2, b_pw2, lb_logits, onorm_g, w_out, pe_norm_g, w_pg, w_pp, final_g, loss_target, m_ln_g, m_w_in, m_conv_w, m_conv_b, m_cnorm_g, m_cnorm_b, m_w_pw2, m_b_pw2, m_lb_logits, m_onorm_g, m_w_out, m_pe_norm_g, m_w_pg, m_w_pp, m_final_g, v_ln_g, v_w_in, v_conv_w, v_conv_b, v_cnorm_g, v_cnorm_b, v_w_pw2, v_b_pw2, v_lb_logits, v_onorm_g, v_w_out, v_pe_norm_g, v_w_pg, v_w_pp, v_final_g):
    t = x.shape[1]
    x2 = x.reshape(t, D)
    p2 = p.reshape(t, PLE)
    tg2 = loss_target.reshape(t, D)
    fin_g = final_g.reshape(1, D)

    z, u, w_in_all = _inproj_fwd(x2, ln_g, w_in[0].astype(BF16))
    (o_raw, y_hgrn, states, conv_w_all, w_pw2_all, w_out_all, w_pg_all, w_pp_all) = _hgrn_fwd(
        z, lb_logits, onorm_g,
        [_pad_rows(conv_w[0], CONV_PAD), w_pw2[0].astype(BF16), w_out[0].astype(BF16),
         w_pg[0].astype(BF16), w_pp[0].astype(BF16)])
    w_pw2_full = w_pw2_all.reshape(D, D)
    w_out_full = w_out_all.reshape(2 * D, D)
    w_pg_full = w_pg_all.reshape(D, D)
    yc, y2, y_conv = _conv_fwd(z, conv_w_all, conv_b, cnorm_g, cnorm_b, w_pw2_full, b_pw2)

    (d_h1, dy_conv, dy_hgrn, d_w_out, d_w_pg, d_w_pp, d_pen_p, d_fin_p, loss_p) = _tail(
        x2, y_conv, y_hgrn, p2, tg2, w_out_full, w_pg_full, w_pp_all, pe_norm_g, fin_g)

    dz, d_on_p, d_lb_p, r_w_out, r_w_pg, r_w_pp = _hgrn_bwd(
        dy_hgrn, z, o_raw, states, lb_logits, onorm_g,
        [d_w_out.reshape(N_DEV, 2 * D // N_DEV, D), d_w_pg.reshape(N_DEV, D // N_DEV, D), d_w_pp])
    (d_w_in_hi,) = _inproj_bwd_dw("inproj_bwd_dw_hi", u, dz, N_DEV // 2, N_DEV // 2)
    dz, d_w_pw2, d_conv_w, d_b2_p, d_cng_p, d_cnb_p, d_cb_p, r_w_in_hi = _conv_bwd(
        dy_conv, z, yc, y2, conv_w_all, cnorm_g, cnorm_b, w_pw2_full, dz, [d_w_in_hi], [1])
    d_w_in_lo, r_w_pw2, r_conv_w = _inproj_bwd_dw(
        "inproj_bwd_dw_lo", u, dz, 0, N_DEV // 2,
        [d_w_pw2.reshape(N_DEV, D // N_DEV, D), d_conv_w], ["scatter", "scatter"])
    chip_lo = _pair_reduce("pair_reduce_lo", d_w_in_lo)
    grad_x, d_ln_p, r_w_in_lo = _inproj_bwd_dx(
        dz, x2, d_h1, ln_g, w_in_all, [chip_lo], [("chip", 0)])

    small = _pack_small({R_LN: d_ln_p, R_CONVB: d_cb_p, R_CNG: d_cng_p, R_CNB: d_cnb_p,
                         R_BPW2: d_b2_p, R_LB0: d_lb_p, R_ON: d_on_p, R_PEN: d_pen_p,
                         R_FIN: d_fin_p, R_LOSS: loss_p})
    (small_all,) = _exchange_call("gather_small", [small], ["gather"])

    big = {}
    big["w_in"] = _sum_adam("adam_w_in", [r_w_in_lo, r_w_in_hi], w_in[0], m_w_in[0], v_w_in[0], 128)
    cw = _sum_adam("adam_conv_w", [r_conv_w], _pad_rows(conv_w[0], CONV_PAD),
                   _pad_rows(m_conv_w[0], CONV_PAD), _pad_rows(v_conv_w[0], CONV_PAD), CONV_PAD)
    big["conv_w"] = [a[:CONV_K] for a in cw]
    big["w_pw2"] = _sum_adam("adam_w_pw2", [r_w_pw2], w_pw2[0], m_w_pw2[0], v_w_pw2[0], 128)
    big["w_out"] = _sum_adam("adam_w_out", [r_w_out], w_out[0], m_w_out[0], v_w_out[0], 128)
    big["w_pg"] = _sum_adam("adam_w_pg", [r_w_pg], w_pg[0], m_w_pg[0], v_w_pg[0], 128)
    big["w_pp"] = _sum_adam("adam_w_pp", [r_w_pp], w_pp[0], m_w_pp[0], v_w_pp[0], PLE)

    small_w = [ln_g, conv_b, cnorm_g, cnorm_b, b_pw2, lb_logits, onorm_g, pe_norm_g, final_g]
    small_m = [m_ln_g, m_conv_b, m_cnorm_g, m_cnorm_b, m_b_pw2, m_lb_logits, m_onorm_g,
               m_pe_norm_g, m_final_g]
    small_v = [v_ln_g, v_conv_b, v_cnorm_g, v_cnorm_b, v_b_pw2, v_lb_logits, v_onorm_g,
               v_pe_norm_g, v_final_g]
    sg, sd, sm, sv, loss = _small_adam(small_all, lb_logits, _pack_rows(small_w),
                                       _pack_rows(small_m), _pack_rows(small_v))

    small_rows = {"ln_g": (R_LN, 1), "conv_b": (R_CONVB, 1), "cnorm_g": (R_CNG, 1),
                  "cnorm_b": (R_CNB, 1), "b_pw2": (R_BPW2, 1), "lb_logits": (R_LB0, 2),
                  "onorm_g": (R_ON, 1), "pe_norm_g": (R_PEN, 1), "final_g": (R_FIN, 1)}
    order = ["ln_g", "w_in", "conv_w", "conv_b", "cnorm_g", "cnorm_b", "w_pw2", "b_pw2",
             "lb_logits", "onorm_g", "w_out", "pe_norm_g", "w_pg", "w_pp", "final_g"]

    def leaf(kind, name):
        if name in big:
            return big[name][kind][None]
        r0, n = small_rows[name]
        a = (sg, sd, sm, sv)[kind][r0:r0 + n]
        return a.reshape(D) if name == "final_g" else a

    outs = [loss.reshape(()), grad_x.reshape(1, t, D)]
    for kind in range(4):
        outs += [leaf(kind, name) for name in order]
    return tuple(outs)
```

```python
import functools

import jax
import jax.numpy as jnp
from jax import lax
from jax.experimental import pallas as pl
from jax.experimental.pallas import tpu as pltpu

F32 = jnp.float32
BF16 = jnp.bfloat16
MESH = pl.DeviceIdType.MESH

N_DEV = 8
D = 1024
N_COLS = 7 * D
COLS_PER_DEV = N_COLS // N_DEV
PLE = 256
HEAD = 128
N_HEADS = D // HEAD
CONV_K = 31
CONV_PAD = 32
CHUNK = 64
EPS = 1e-6
SUBLANES = 8

ADAM_LR = 0.001
ADAM_B1 = 0.9
ADAM_B2 = 0.999
ADAM_EPS = 1e-08
ADAM_WD = 0.01
ADAM_STEP = 10

MIB = 1024 * 1024
N_SMALL = 16
R_LN, R_CONVB, R_CNG, R_CNB, R_BPW2, R_LB0, R_LB1, R_ON, R_PEN, R_FIN, R_LOSS = range(11)


def _params(vmem_mib, **kw):
    return pltpu.CompilerParams(vmem_limit_bytes=vmem_mib * MIB, **kw)


def _dot(a, b):
    return jnp.dot(a.astype(BF16), b.astype(BF16), preferred_element_type=F32)


def _dot_nt(a, b):
    return lax.dot_general(a.astype(BF16), b.astype(BF16), (((1,), (1,)), ((), ())),
                           preferred_element_type=F32)


def _dot_tn(a, b):
    return lax.dot_general(a.astype(BF16), b.astype(BF16), (((0,), (0,)), ((), ())),
                           preferred_element_type=F32)


def _split(a):
    hi = a.astype(BF16)
    return hi, (a - hi.astype(F32)).astype(BF16)


def _sigmoid(x):
    return 1.0 / (1.0 + jnp.exp(-x))


def _rowsum8(a):
    r, c = a.shape
    return jnp.sum(a.reshape(r // SUBLANES, SUBLANES, c), axis=0)


def _tri_dot(tri, a):
    hi = a.astype(BF16)
    r1 = a - hi.astype(F32)
    mid = r1.astype(BF16)
    lo = (r1 - mid.astype(F32)).astype(BF16)
    return (jnp.dot(tri, hi, preferred_element_type=F32)
            + jnp.dot(tri, mid, preferred_element_type=F32)
            + jnp.dot(tri, lo, preferred_element_type=F32))


def _lower_bound(lbl):
    l0, l1 = lbl[0:1, :], lbl[1:2, :]
    m = jnp.maximum(l0, l1)
    e0, e1 = jnp.exp(l0 - m), jnp.exp(l1 - m)
    s = e0 + e1
    return e0 / s, e1 / s


ANY = pl.BlockSpec(memory_space=pl.ANY)


def _full(shape):
    return pl.BlockSpec(shape, lambda i: (0,) * len(shape))


def _peer(x, y, c, k):
    px = 1 - x if k & 4 else x
    py = 1 - y if k & 2 else y
    pc = 1 - c if k & 1 else c
    return (px, py, pc), 4 * px + 2 * py + pc


class _Exchange:
    def __init__(self, srcs, outs, modes, send_sems, recv_sems, local_sems):
        x, y, c = lax.axis_index("x"), lax.axis_index("y"), lax.axis_index("c")
        me = 4 * x + 2 * y + c
        self.starts, self.send_waits, self.recv_waits = [], [], []

        def remote(a, k, src, slot, peer, when):
            sem = a * N_DEV + k
            cp = pltpu.make_async_remote_copy(
                src_ref=src, dst_ref=outs[a].at[slot], send_sem=send_sems.at[sem],
                recv_sem=recv_sems.at[sem], device_id=peer, device_id_type=MESH)
            self.starts.append((when, cp.start))
            self.send_waits.append((when, cp.wait_send))

        def arrival(a, k, slot, when):
            sem = a * N_DEV + k
            cp = pltpu.make_async_remote_copy(
                src_ref=outs[a].at[slot], dst_ref=outs[a].at[slot], send_sem=send_sems.at[sem],
                recv_sem=recv_sems.at[sem], device_id=(x, y, c), device_id_type=MESH)
            self.recv_waits.append((when, cp.wait_recv))

        def local(a, src, slot, when):
            cp = pltpu.make_async_copy(src, outs[a].at[slot], local_sems.at[a])
            self.starts.append((when, cp.start))
            self.send_waits.append((when, cp.wait))

        for a, (src, mode) in enumerate(zip(srcs, modes)):
            if mode in ("gather", "scatter"):
                local(a, src if mode == "gather" else src.at[me], me, None)
                for k in range(1, N_DEV):
                    peer, peer_idx = _peer(x, y, c, k)
                    remote(a, k, src if mode == "gather" else src.at[peer_idx], me, peer, None)
                    arrival(a, k, peer_idx, None)
                continue
            if isinstance(mode, tuple):
                here, away = x == mode[1], x != mode[1]
                chip = 2 * x + y
                local(a, src.at[y], chip, here)
                remote(a, 1, src.at[1 - y], chip, (x, 1 - y, c), here)
                remote(a, 2, src.at[y], chip, (1 - x, y, c), away)
                remote(a, 3, src.at[1 - y], chip, (1 - x, 1 - y, c), away)
                arrival(a, 1, 2 * x + 1 - y, here)
                arrival(a, 2, 2 * (1 - x) + y, here)
                arrival(a, 3, 2 * (1 - x) + 1 - y, here)
                continue
            here, away = x == mode, x != mode
            for kk in range(4):
                py = 1 - y if kk & 2 else y
                pc = 1 - c if kk & 1 else c
                block = src.at[2 * py + pc]
                if kk == 0:
                    local(a, block, me, here)
                else:
                    remote(a, kk, block, me, (x, py, pc), here)
                remote(a, 4 + kk, block, me, (1 - x, py, pc), away)
            for k in range(1, N_DEV):
                arrival(a, k, _peer(x, y, c, k)[1], here)

    @staticmethod
    def _run(actions):
        for when, fn in actions:
            if when is None:
                fn()
            else:
                pl.when(when)(fn)

    def start(self):
        self._run(self.starts)

    def wait(self):
        self._run(self.recv_waits)
        self._run(self.send_waits)


def _exchange_scratch(n):
    return [pltpu.SemaphoreType.DMA((n * N_DEV,)), pltpu.SemaphoreType.DMA((n * N_DEV,)),
            pltpu.SemaphoreType.DMA((n,))]


def _recv_shapes(srcs, modes):
    def shape(s, m):
        if m == "gather":
            return (N_DEV,) + s.shape
        return (N_DEV // 2 if isinstance(m, tuple) else N_DEV,) + s.shape[1:]

    return [jax.ShapeDtypeStruct(shape(s, m), s.dtype) for s, m in zip(srcs, modes)]


def _pair_reduce(name, blocks):
    shape = (2,) + blocks.shape[1:]

    def body(src, out_ref, stage, mine, send_sems, recv_sems, local_sems):
        x, y, c = lax.axis_index("x"), lax.axis_index("y"), lax.axis_index("c")
        sends, waits = [], []
        for py in range(2):
            sends.append(pltpu.make_async_remote_copy(
                src_ref=src.at[2 * py + 1 - c], dst_ref=stage.at[py], send_sem=send_sems.at[py],
                recv_sem=recv_sems.at[py], device_id=(x, y, 1 - c), device_id_type=MESH))
            waits.append(pltpu.make_async_copy(src.at[2 * py + c], mine.at[py], local_sems.at[py]))
        for cp in sends + waits:
            cp.start()
        for cp in waits:
            cp.wait()
        for cp in sends:
            cp.wait_recv()
        out_ref[...] = mine[...] + stage[...]
        for cp in sends:
            cp.wait_send()

    return pl.pallas_call(
        body, name=name, out_shape=jax.ShapeDtypeStruct(shape, F32), in_specs=[ANY],
        scratch_shapes=[pltpu.VMEM(shape, F32), pltpu.VMEM(shape, F32),
                        pltpu.SemaphoreType.DMA((2,)), pltpu.SemaphoreType.DMA((2,)),
                        pltpu.SemaphoreType.DMA((2,))],
        compiler_params=_params(40),
    )(blocks)


def _exchange_call(name, srcs, modes):
    n = len(srcs)

    def body(*refs):
        xch = _Exchange(refs[:n], refs[n:2 * n], modes, *refs[2 * n:])
        xch.start()
        xch.wait()

    return pl.pallas_call(
        body, name=name, out_shape=_recv_shapes(srcs, modes),
        in_specs=[ANY] * n, out_specs=[ANY] * n, scratch_shapes=_exchange_scratch(n),
    )(*srcs)


def _hosted(body, n_in, n_out, grid, modes):
    n = len(modes)

    def hosted(*refs):
        ins, srcs = refs[:n_in], refs[n_in:n_in + n]
        outs = refs[n_in + n:n_in + n + n_out]
        bufs = refs[n_in + n + n_out:n_in + 2 * n + n_out]
        scratch = refs[n_in + 2 * n + n_out:-3]
        xch = _Exchange(srcs, bufs, modes, *refs[-3:])
        first, last = True, True
        for axis, size in enumerate(grid):
            first = jnp.logical_and(first, pl.program_id(axis) == 0)
            last = jnp.logical_and(last, pl.program_id(axis) == size - 1)
        pl.when(first)(xch.start)
        body(*ins, *outs, *scratch)
        pl.when(last)(xch.wait)

    return hosted


N_CHIPS = N_DEV // 2
PAIR_COLS = 2 * COLS_PER_DEV
PUSHED = (1, 2, 4, 6)
FORWARDED = (2, 4, 6)
NORM_ROWS = 32


def _inproj_fwd(x, ln_g, w_shard):
    t = x.shape[0]
    tt = min(512, t)
    n_t = t // tt
    chip = 2 * lax.axis_index("x") + lax.axis_index("y")
    order = jnp.bitwise_xor(chip, jnp.arange(N_CHIPS, dtype=jnp.int32)).astype(jnp.int32)

    def body(order_ref, x_ref, g_ref, shard_hbm, z_ref, u_ref, w_all,
             u_all, w_blk, w_send, w_recv, w_local):
        p, i = pl.program_id(0), pl.program_id(1)
        x, y, c = lax.axis_index("x"), lax.axis_index("y"), lax.axis_index("c")
        mine = 4 * x + 2 * y + c

        def push(k):
            peer, _ = _peer(x, y, c, k)
            return pltpu.make_async_remote_copy(
                src_ref=shard_hbm, dst_ref=w_all.at[mine], send_sem=w_send.at[k],
                recv_sem=w_recv.at[k], device_id=peer, device_id_type=MESH)

        def forward(k):
            _, owner = _peer(x, y, c, k)
            return pltpu.make_async_remote_copy(
                src_ref=w_all.at[owner], dst_ref=w_all.at[owner], send_sem=w_send.at[k + 1],
                recv_sem=w_recv.at[k + 1], device_id=(x, y, 1 - c), device_id_type=MESH)

        def landed(k):
            _, owner = _peer(x, y, c, k)
            return pltpu.make_async_remote_copy(
                src_ref=w_all.at[owner], dst_ref=w_all.at[owner], send_sem=w_send.at[k],
                recv_sem=w_recv.at[k], device_id=(x, y, c), device_id_type=MESH)

        keep = pltpu.make_async_copy(shard_hbm, w_all.at[mine], w_local.at[0])

        def load_pair(step):
            same = shard_hbm if step == 0 else w_all.at[_peer(x, y, c, 2 * step)[1]]
            other = w_all.at[_peer(x, y, c, 2 * step + 1)[1]]
            for side in range(2):
                @pl.when(c == side)
                def _(side=side):
                    pltpu.sync_copy(same, w_blk.at[:, pl.ds(COLS_PER_DEV * side, COLS_PER_DEV)])
                    pltpu.sync_copy(
                        other, w_blk.at[:, pl.ds(COLS_PER_DEV * (1 - side), COLS_PER_DEV)])

        @pl.when(jnp.logical_and(p == 0, i == 0))
        def _():
            for k in PUSHED[:-1]:
                push(k).start()
            keep.start()

        @pl.when(jnp.logical_and(p == 1, i == 0))
        def _():
            for k in PUSHED[1:-1]:
                push(k).wait_send()
            push(PUSHED[-1]).start()

        for step in range(N_CHIPS):
            @pl.when(jnp.logical_and(p == step, i == 0))
            def _(step=step):
                landed(2 * step + 1).wait_recv()
                load_pair(step)

        rows = pl.ds(pl.multiple_of(i * tt, tt), tt)

        @pl.when(p == 0)
        def _():
            def norm_rows(r, carry):
                sub = pl.ds(pl.multiple_of(r * NORM_ROWS, NORM_ROWS), NORM_ROWS)
                xv = x_ref[sub, :]
                rstd = lax.rsqrt(jnp.mean(xv * xv, axis=-1, keepdims=True) + EPS)
                ub = (xv * rstd * g_ref[...]).astype(BF16)
                u_ref[sub, :] = ub
                u_all[pl.ds(pl.multiple_of(i * tt + r * NORM_ROWS, NORM_ROWS), NORM_ROWS), :] = ub
                return carry

            lax.fori_loop(0, tt // NORM_ROWS, norm_rows, 0, unroll=2)

        z_ref[...] = jnp.dot(u_all[rows, :], w_blk[...], preferred_element_type=F32)

        for step in range(1, N_CHIPS):
            @pl.when(jnp.logical_and(p == step - 1, i == n_t - 1))
            def _(step=step):
                landed(2 * step).wait_recv()
                forward(2 * step).start()

        @pl.when(jnp.logical_and(p == N_CHIPS - 1, i == n_t - 1))
        def _():
            push(PUSHED[0]).wait_send()
            push(PUSHED[-1]).wait_send()
            for k in FORWARDED:
                forward(k).wait_send()
            keep.wait()

    first_pass = lambda p, i, order_ref: (jnp.where(p == 0, i, n_t - 1), 0)
    grid_spec = pltpu.PrefetchScalarGridSpec(
        num_scalar_prefetch=1, grid=(N_CHIPS, n_t),
        in_specs=[pl.BlockSpec((tt, D), first_pass),
                  pl.BlockSpec((1, D), lambda p, i, order_ref: (0, 0)), ANY],
        out_specs=[pl.BlockSpec((tt, PAIR_COLS), lambda p, i, order_ref: (i, order_ref[p])),
                   pl.BlockSpec((tt, D), first_pass), ANY],
        scratch_shapes=[pltpu.VMEM((t, D), BF16), pltpu.VMEM((D, PAIR_COLS), BF16),
                        pltpu.SemaphoreType.DMA((N_DEV,)), pltpu.SemaphoreType.DMA((N_DEV,)),
                        pltpu.SemaphoreType.DMA((1,))])
    return pl.pallas_call(
        body, name="inproj_fwd", grid_spec=grid_spec,
        out_shape=[jax.ShapeDtypeStruct((t, N_COLS), F32), jax.ShapeDtypeStruct((t, D), BF16),
                   jax.ShapeDtypeStruct((N_DEV,) + w_shard.shape, BF16)],
        compiler_params=_params(48, dimension_semantics=("arbitrary", "arbitrary")),
    )(order, x, ln_g, w_shard)


def _shifted_copies(buf, shifted, rows):
    for b in range(1, SUBLANES):
        shifted[b, 0:rows, :] = buf[b:b + rows, :]


def _tap_slabs(buf, shifted, offset_of_tap):
    groups = {}
    for k in range(CONV_K):
        a, b = divmod(offset_of_tap(k), SUBLANES)
        groups.setdefault(b, []).append((SUBLANES * a, k))
    out = []
    for b, taps in sorted(groups.items()):
        taps.sort()
        lo = taps[0][0]
        out.append((buf if b == 0 else shifted.at[b], lo, [(k, off - lo) for off, k in taps]))
    return out


def _group_norm_stats(blk):
    mu = jnp.mean(blk, axis=-1, keepdims=True)
    cen = blk - mu
    var = jnp.mean(cen * cen, axis=-1, keepdims=True)
    return cen * lax.rsqrt(var + EPS)


def _conv_fwd(z, conv_w_all, conv_b, cn_g, cn_b, w_pw2, b_pw2):
    t = z.shape[0]
    tt = min(256, t)
    rc = 128

    def body(val_ref, glu_ref, gate_ref, cw_ref, cb_ref, g_ref, b_ref, w_hbm, b2_ref,
             yc_ref, y2_ref, yo_ref, w_vmem, vbuf, vsh, y1buf):
        @pl.when(pl.program_id(0) == 0)
        def _():
            pltpu.sync_copy(w_hbm, w_vmem)
            vbuf[0:CONV_PAD, :] = jnp.zeros((CONV_PAD, D), F32)

        vbuf[CONV_PAD:CONV_PAD + tt, :] = val_ref[...] * _sigmoid(glu_ref[...])
        _shifted_copies(vbuf, vsh, tt + 24)

        for g in range(N_HEADS):
            cs = slice(HEAD * g, HEAD * (g + 1))

            def row_chunk(r, carry, g=g, cs=cs):
                r0 = pl.multiple_of(r * rc, rc)
                acc = jnp.broadcast_to(cb_ref[:, cs], (rc, HEAD))
                for ref, lo, taps in _tap_slabs(vbuf, vsh, lambda k: k + 2):
                    slab = ref[pl.ds(r0 + lo, rc + taps[-1][1]), cs]
                    for k, off in taps:
                        acc = acc + cw_ref[g, k:k + 1, :] * slab[off:off + rc]
                yc_ref[pl.ds(r0, rc), cs] = acc
                n = _group_norm_stats(acc) * g_ref[:, cs] + b_ref[:, cs]
                y1buf[pl.ds(r0, rc), cs] = (n * _sigmoid(n)).astype(BF16)
                return carry

            lax.fori_loop(0, tt // rc, row_chunk, 0, unroll=True)
        vbuf[0:CONV_PAD, :] = vbuf[tt:tt + CONV_PAD, :]
        y2 = jnp.dot(y1buf[...], w_vmem[...], preferred_element_type=F32) + b2_ref[...]
        y2_ref[...] = y2
        gate = gate_ref[...]
        yo_ref[...] = (y2 * gate * _sigmoid(gate)).astype(BF16)

    col = lambda j: pl.BlockSpec((tt, D), lambda i: (i, j))
    row = pl.BlockSpec((tt, D), lambda i: (i, 0))
    return pl.pallas_call(
        body, name="conv_fwd", grid=(t // tt,),
        out_shape=[jax.ShapeDtypeStruct((t, D), F32), jax.ShapeDtypeStruct((t, D), F32),
                   jax.ShapeDtypeStruct((t, D), BF16)],
        in_specs=[col(0), col(1), col(2), _full((N_DEV, CONV_PAD, HEAD)), _full((1, D)),
                  _full((1, D)), _full((1, D)), ANY, _full((1, D))],
        out_specs=[row, row, row],
        scratch_shapes=[pltpu.VMEM((D, D), BF16), pltpu.VMEM((tt + CONV_PAD, D), F32),
                        pltpu.VMEM((SUBLANES, tt + CONV_PAD, D), F32), pltpu.VMEM((tt, D), BF16)],
        compiler_params=_params(48, dimension_semantics=("arbitrary",)),
    )(z, z, z, conv_w_all, conv_b, cn_g, cn_b, w_pw2, b_pw2)


HEAD_GROUP = 8
_HEAD_LANES = [slice(HEAD * j, HEAD * (j + 1)) for j in range(HEAD_GROUP)]


def _head_mean(a):
    return jnp.concatenate(
        [jnp.broadcast_to(jnp.mean(a[:, hs], axis=-1, keepdims=True), (a.shape[0], HEAD))
         for hs in _HEAD_LANES], axis=1)


def _chunk_quantities(zq, zf, lbh, tri):
    sig = _sigmoid(zf)
    sig_neg = _sigmoid(-zf)
    f = lbh + (1.0 - lbh) * sig
    k = (1.0 - lbh) * sig_neg
    q = zq * _sigmoid(zq)
    b = _tri_dot(tri, jnp.log(f))
    b_mid = b[CHUNK // 2 - 1:CHUNK // 2, :]
    b_last = b[CHUNK - 1:CHUNK, :]
    e_q = jnp.exp(b)
    e_qm = jnp.exp(b - b_mid)
    e_km = jnp.exp(b_mid - b)
    e_kd = jnp.exp(b_last - b)
    return q, k, f, sig, sig_neg, e_q, e_qm, e_km, e_kd, jnp.exp(b_last)


def _hgrn_fwd(z, lb_logits, onorm_g, shards):
    t = z.shape[0]
    tt = min(256, t)
    nc = tt // CHUNK
    modes = ["gather"] * len(shards)
    n = len(modes)

    def body(q_ref, f_ref, i_ref, g_ref, lbl_ref, on_ref, o_ref, y_ref, s_ref, st):
        @pl.when(pl.program_id(0) == 0)
        def _():
            st[...] = jnp.zeros_like(st)

        lb, _ = _lower_bound(lbl_ref[...])
        rows = lax.broadcasted_iota(jnp.int32, (CHUNK, CHUNK), 0)
        cols = lax.broadcasted_iota(jnp.int32, (CHUNK, CHUNK), 1)
        causal = rows >= cols
        tri = causal.astype(BF16)

        def chunk(c, carry):
            r0 = pl.multiple_of(c * CHUNK, CHUNK)
            rs = pl.ds(r0, CHUNK)
            for h0 in range(0, N_HEADS, HEAD_GROUP):
                cs = slice(HEAD * h0, HEAD * (h0 + HEAD_GROUP))
                q, k, _, _, _, e_q, e_qm, e_km, e_kd, e_last = _chunk_quantities(
                    q_ref[rs, cs], f_ref[rs, cs], lb[:, cs], tri)
                v = i_ref[rs, cs].astype(BF16)
                qm, km = (q * e_qm).astype(BF16), (k * e_km).astype(BF16)
                qt, kd = (q * e_q).astype(BF16), (k * e_kd).astype(BF16)
                outs = []
                for j, hs in enumerate(_HEAD_LANES):
                    s_old = st[h0 + j]
                    s_ref[c, h0 + j] = s_old.astype(BF16)
                    a = jnp.where(causal, _dot_nt(qm[:, hs], km[:, hs]), 0.0)
                    outs.append(_dot_nt(qt[:, hs], s_old) + _dot(a, v[:, hs]))
                    st[h0 + j] = s_old * e_last[:, hs] + _dot_tn(v[:, hs], kd[:, hs])
                o = jnp.concatenate(outs, axis=1)
                o_ref[rs, cs] = o
                n = o * lax.rsqrt(_head_mean(o * o) + EPS)
                zg = g_ref[rs, cs]
                y_ref[rs, cs] = (n * on_ref[:, cs] * zg * _sigmoid(zg)).astype(BF16)
            return carry

        lax.fori_loop(0, nc, chunk, 0, unroll=True)

    col = lambda j: pl.BlockSpec((tt, D), lambda i: (i, j))
    row = pl.BlockSpec((tt, D), lambda i: (i, 0))
    return pl.pallas_call(
        _hosted(body, 6, 3, (t // tt,), modes), name="hgrn_fwd", grid=(t // tt,),
        out_shape=[jax.ShapeDtypeStruct((t, D), F32), jax.ShapeDtypeStruct((t, D), BF16),
                   jax.ShapeDtypeStruct((t // CHUNK, N_HEADS, HEAD, HEAD), BF16)]
        + _recv_shapes(shards, modes),
        in_specs=[col(3), col(4), col(5), col(6), _full((2, D)), _full((1, D))] + [ANY] * n,
        out_specs=[row, row, pl.BlockSpec((nc, N_HEADS, HEAD, HEAD), lambda i: (i, 0, 0, 0))]
        + [ANY] * n,
        scratch_shapes=[pltpu.VMEM((N_HEADS, HEAD, HEAD), F32)] + _exchange_scratch(n),
        compiler_params=_params(40, dimension_semantics=("arbitrary",)),
    )(z, z, z, z, lb_logits, onorm_g, *shards)


def _rms_bwd(dn, xhat, rstd):
    return rstd * (dn - xhat * jnp.mean(dn * xhat, axis=-1, keepdims=True))


def _tail(x, y_conv, y_hgrn, p, target, w_out, w_pg, w_pp_all, pe_g, fin_g):
    t = x.shape[0]
    tt = min(256, t)
    n_steps = t // tt

    def body(x_ref, yc_ref, yh_ref, p_ref, tg_ref, wo_hbm, wg_hbm, wp_hbm, pg_ref, fg_ref,
             dh1_ref, dyc_ref, dyh_ref, dwo_hbm, dwg_hbm, dwp_hbm, dpg_ref, dfg_ref, loss_ref,
             wo, wg, wp, dwo, dwg, dwp):
        i = pl.program_id(0)

        @pl.when(i == 0)
        def _():
            pltpu.sync_copy(wo_hbm, wo)
            pltpu.sync_copy(wg_hbm, wg)
            for d in range(N_DEV):
                pltpu.sync_copy(wp_hbm.at[d], wp.at[:, pl.ds(HEAD * d, HEAD)])
            dwo[...] = jnp.zeros_like(dwo)
            dwg[...] = jnp.zeros_like(dwg)
            dwp[...] = jnp.zeros_like(dwp)
            dpg_ref[...] = jnp.zeros_like(dpg_ref)
            dfg_ref[...] = jnp.zeros_like(dfg_ref)
            loss_ref[...] = jnp.zeros_like(loss_ref)

        ycv, yhv = yc_ref[...], yh_ref[...]
        h1 = (x_ref[...] + jnp.dot(ycv, wo[0:D, :], preferred_element_type=F32)
              + jnp.dot(yhv, wo[D:2 * D, :], preferred_element_type=F32))
        pb = p_ref[...].astype(BF16)
        pe = jnp.dot(pb, wp[...], preferred_element_type=F32)
        rstd1 = lax.rsqrt(jnp.mean(h1 * h1, axis=-1, keepdims=True) + EPS)
        n1 = h1 * rstd1
        rb = (n1 * pg_ref[...]).astype(BF16)
        gate = _sigmoid(jnp.dot(rb, wg[...], preferred_element_type=F32))
        h2 = h1 + gate * pe
        rstd2 = lax.rsqrt(jnp.mean(h2 * h2, axis=-1, keepdims=True) + EPS)
        n2 = h2 * rstd2
        err = n2 * fg_ref[...] - tg_ref[...]
        loss_ref[...] += _rowsum8(err * err)

        d_out = err * (1.0 / D)
        dfg_ref[...] += _rowsum8(d_out * n2)
        d_h2 = _rms_bwd(d_out * fg_ref[...], n2, rstd2)
        d_pe = (d_h2 * gate).astype(BF16)
        d_gpre = (d_h2 * pe * gate * (1.0 - gate)).astype(BF16)
        dwg[...] += _dot_tn(rb, d_gpre)
        dwp[...] += _dot_tn(pb, d_pe)
        dr = _dot_nt(d_gpre, wg[...])
        dpg_ref[...] += _rowsum8(dr * n1)
        d_h1 = d_h2 + _rms_bwd(dr * pg_ref[...], n1, rstd1)
        dh1_ref[...] = d_h1
        d_h1b = d_h1.astype(BF16)
        dwo[0:D, :] += _dot_tn(ycv, d_h1b)
        dwo[D:2 * D, :] += _dot_tn(yhv, d_h1b)
        dyc_ref[...] = _dot_nt(d_h1b, wo[0:D, :])
        dyh_ref[...] = _dot_nt(d_h1b, wo[D:2 * D, :])

        @pl.when(i == n_steps - 1)
        def _():
            pltpu.sync_copy(dwo, dwo_hbm)
            pltpu.sync_copy(dwg, dwg_hbm)
            for d in range(N_DEV):
                pltpu.sync_copy(dwp.at[:, pl.ds(HEAD * d, HEAD)], dwp_hbm.at[d])

    row = pl.BlockSpec((tt, D), lambda i: (i, 0))
    acc = _full((SUBLANES, D))
    return pl.pallas_call(
        body, name="tail_fwd_bwd", grid=(n_steps,),
        out_shape=[jax.ShapeDtypeStruct((t, D), F32)] * 3
        + [jax.ShapeDtypeStruct((2 * D, D), F32), jax.ShapeDtypeStruct((D, D), F32),
           jax.ShapeDtypeStruct((N_DEV, PLE, HEAD), F32)]
        + [jax.ShapeDtypeStruct((SUBLANES, D), F32)] * 3,
        in_specs=[row, row, row, pl.BlockSpec((tt, PLE), lambda i: (i, 0)), row,
                  ANY, ANY, ANY, _full((1, D)), _full((1, D))],
        out_specs=[row, row, row, ANY, ANY, ANY, acc, acc, acc],
        scratch_shapes=[pltpu.VMEM((2 * D, D), BF16), pltpu.VMEM((D, D), BF16),
                        pltpu.VMEM((PLE, D), BF16), pltpu.VMEM((2 * D, D), F32),
                        pltpu.VMEM((D, D), F32), pltpu.VMEM((PLE, D), F32)],
        compiler_params=_params(52, dimension_semantics=("arbitrary",)),
    )(x, y_conv, y_hgrn, p, target, w_out, w_pg, w_pp_all, pe_g, fin_g)


def _hgrn_bwd(dy, z, o_raw, states, lb_logits, onorm_g, grads):
    t = z.shape[0]
    tt = min(256, t)
    nc = tt // CHUNK
    n_steps = t // tt
    modes = ["scatter"] * len(grads)

    def body(dy_ref, q_ref, f_ref, i_ref, g_ref, o_ref, s_ref, lbl_ref, on_ref,
             dz_ref, don_ref, dlb_ref, dst):
        @pl.when(pl.program_id(0) == 0)
        def _():
            dst[...] = jnp.zeros_like(dst)
            don_ref[...] = jnp.zeros_like(don_ref)
            dlb_ref[...] = jnp.zeros_like(dlb_ref)

        lb, _ = _lower_bound(lbl_ref[...])
        rows = lax.broadcasted_iota(jnp.int32, (CHUNK, CHUNK), 0)
        cols = lax.broadcasted_iota(jnp.int32, (CHUNK, CHUNK), 1)
        causal = rows >= cols
        tri = causal.astype(BF16)
        tri_rev = (rows <= cols).astype(BF16)
        width = HEAD * HEAD_GROUP
        is_last = lax.broadcasted_iota(jnp.int32, (CHUNK, width), 0) == CHUNK - 1
        nn = (((1,), (0,)), ((), ()))
        tn = (((0,), (0,)), ((), ()))
        dg = functools.partial(lax.dot_general, preferred_element_type=F32)

        def chunk(cc, carry):
            c = nc - 1 - cc
            r0 = pl.multiple_of(c * CHUNK, CHUNK)
            rs = pl.ds(r0, CHUNK)
            for h0 in range(0, N_HEADS, HEAD_GROUP):
                cs = slice(HEAD * h0, HEAD * h0 + width)
                zq, zf, zg = q_ref[rs, cs], f_ref[rs, cs], g_ref[rs, cs]
                lbh = lb[:, cs]
                q, k, f, sig, sig_neg, e_q, e_qm, e_km, e_kd, e_last = _chunk_quantities(
                    zq, zf, lbh, tri)
                vb = i_ref[rs, cs].astype(BF16)
                qt, qm, km, kd = q * e_q, q * e_qm, k * e_km, k * e_kd
                qt_b, kd_b = qt.astype(BF16), kd.astype(BF16)
                qm_h, qm_l = _split(qm)
                km_h, km_l = _split(km)

                o = o_ref[rs, cs]
                rstd = lax.rsqrt(_head_mean(o * o) + EPS)
                n = o * rstd
                sg = _sigmoid(zg)
                dyv = dy_ref[rs, cs]
                on = on_ref[:, cs]
                d_zg = dyv * n * on * sg * (1.0 + zg * (1.0 - sg))
                d_on = dyv * zg * sg
                don_ref[:, cs] += _rowsum8(d_on * n)
                dn = d_on * on
                do_b = (rstd * (dn - n * _head_mean(dn * n))).astype(BF16)

                dv, dkd, dqt, dqm, dkm, s_dots = [], [], [], [], [], []
                for j, hs in enumerate(_HEAD_LANES):
                    s_old, ds_new = s_ref[c, h0 + j], dst[h0 + j]
                    ds_b = ds_new.astype(BF16)
                    a = jnp.where(causal, _dot_nt(qm_h[:, hs], km_h[:, hs]), 0.0)
                    da = jnp.where(causal, _dot_nt(do_b[:, hs], vb[:, hs]), 0.0)
                    dv.append(_dot_tn(a, do_b[:, hs]) + _dot_nt(kd_b[:, hs], ds_b))
                    dkd.append(_dot(vb[:, hs], ds_b))
                    dqt.append(_dot(do_b[:, hs], s_old))
                    da_h, da_l = _split(da)
                    dqm.append(dg(da_h, km_h[:, hs], nn)
                               + (dg(da_h, km_l[:, hs], nn) + dg(da_l, km_h[:, hs], nn)))
                    dkm.append(dg(da_h, qm_h[:, hs], tn)
                               + (dg(da_h, qm_l[:, hs], tn) + dg(da_l, qm_h[:, hs], tn)))
                    dst[h0 + j] = ds_new * e_last[:, hs] + _dot_tn(do_b[:, hs], qt_b[:, hs])
                    s_dots.append(jnp.sum(s_old.astype(F32) * ds_new, axis=0, keepdims=True))
                dv, dkd, dqt, dqm, dkm, s_dots = [
                    jnp.concatenate(parts, axis=1) for parts in (dv, dkd, dqt, dqm, dkm, s_dots)]
                dq = dqt * e_q + dqm * e_qm
                dk = dkm * e_km + dkd * e_kd
                last = jnp.sum(dkd * kd, axis=0, keepdims=True) + e_last * s_dots
                db = q * dq - k * dk + jnp.where(is_last, last, 0.0)
                dlogf = _tri_dot(tri_rev, db)
                common = sig_neg * (dlogf / f - dk)
                dlb_ref[:, cs] += _rowsum8(common)
                c0 = 3 * D + HEAD * h0
                sq = _sigmoid(zq)
                dz_ref[rs, c0:c0 + width] = (dq * sq * (1.0 + zq * (1.0 - sq))).astype(BF16)
                dz_ref[rs, D + c0:D + c0 + width] = ((1.0 - lbh) * sig * common).astype(BF16)
                dz_ref[rs, 2 * D + c0:2 * D + c0 + width] = dv.astype(BF16)
                dz_ref[rs, 3 * D + c0:3 * D + c0 + width] = d_zg.astype(BF16)
            return carry

        lax.fori_loop(0, nc, chunk, 0, unroll=True)

    rev = lambda i: n_steps - 1 - i
    col = lambda j: pl.BlockSpec((tt, D), lambda i: (rev(i), j))
    row = pl.BlockSpec((tt, D), lambda i: (rev(i), 0))
    acc = _full((SUBLANES, D))
    n = len(modes)
    return pl.pallas_call(
        _hosted(body, 9, 3, (n_steps,), modes), name="hgrn_bwd", grid=(n_steps,),
        out_shape=[jax.ShapeDtypeStruct((t, N_COLS), BF16),
                   jax.ShapeDtypeStruct((SUBLANES, D), F32),
                   jax.ShapeDtypeStruct((SUBLANES, D), F32)] + _recv_shapes(grads, modes),
        in_specs=[row, col(3), col(4), col(5), col(6), row,
                  pl.BlockSpec((nc, N_HEADS, HEAD, HEAD), lambda i: (rev(i), 0, 0, 0)),
                  _full((2, D)), _full((1, D))] + [ANY] * n,
        out_specs=[pl.BlockSpec((tt, N_COLS), lambda i: (rev(i), 0)), acc, acc] + [ANY] * n,
        scratch_shapes=[pltpu.VMEM((N_HEADS, HEAD, HEAD), F32)] + _exchange_scratch(n),
        compiler_params=_params(48, dimension_semantics=("arbitrary",)),
    )(dy, z, z, z, z, o_raw, states, lb_logits, onorm_g, *grads)


def _conv_bwd(dy, z, yc, y2, conv_w_all, cn_g, cn_b, w_pw2, dz, grads, modes):
    t = z.shape[0]
    tt = min(256, t)
    rc = 32
    n_steps = t // tt

    def body(dy_ref, val_ref, glu_ref, gate_ref, yc_ref, y2_ref, cw_ref, g_ref, b_ref, w_hbm,
             dz_in, dz_ref, dw_hbm, dcw_out, db2_ref, dg_ref, dbeta_ref, dcb_ref,
             w_vmem, dw, dbuf, dsh, y1buf, dnbuf, dcw_ref):
        i = pl.program_id(0)

        @pl.when(i == 0)
        def _():
            pltpu.sync_copy(w_hbm, w_vmem)
            dw[...] = jnp.zeros_like(dw)
            dbuf[tt:tt + CONV_PAD, :] = jnp.zeros((CONV_PAD, D), F32)
            dcw_ref[...] = jnp.zeros_like(dcw_ref)
            dcw_out[...] = jnp.zeros_like(dcw_out)
            db2_ref[...] = jnp.zeros_like(db2_ref)
            dg_ref[...] = jnp.zeros_like(dg_ref)
            dbeta_ref[...] = jnp.zeros_like(dbeta_ref)
            dcb_ref[...] = jnp.zeros_like(dcb_ref)

        gate = gate_ref[...]
        sg = _sigmoid(gate)
        dyv = dy_ref[...]
        dy2 = dyv * gate * sg
        dz_ref[:, 2 * D:3 * D] = (dyv * y2_ref[...] * sg * (1.0 + gate * (1.0 - sg))).astype(BF16)
        db2_ref[...] += _rowsum8(dy2)
        dy2b = dy2.astype(BF16)
        dnbuf[...] = _dot_nt(dy2b, w_vmem[...])

        def norm_chunk(r, carry):
            r0 = pl.multiple_of(r * rc, rc)
            rs = pl.ds(r0, rc)
            for g in range(N_HEADS):
                cs = slice(HEAD * g, HEAD * (g + 1))
                blk = yc_ref[rs, cs]
                mu = jnp.mean(blk, axis=-1, keepdims=True)
                cen = blk - mu
                rstd = lax.rsqrt(jnp.mean(cen * cen, axis=-1, keepdims=True) + EPS)
                xhat = cen * rstd
                n = xhat * g_ref[:, cs] + b_ref[:, cs]
                sn = _sigmoid(n)
                y1buf[rs, cs] = (n * sn).astype(BF16)
                dn = dnbuf[rs, cs] * sn * (1.0 + n * (1.0 - sn))
                dg_ref[:, cs] += _rowsum8(dn * xhat)
                dbeta_ref[:, cs] += _rowsum8(dn)
                dxh = dn * g_ref[:, cs]
                dyc = rstd * (dxh - jnp.mean(dxh, axis=-1, keepdims=True)
                              - xhat * jnp.mean(dxh * xhat, axis=-1, keepdims=True))
                dcb_ref[:, cs] += _rowsum8(dyc)
                dbuf[rs, cs] = dyc
            return carry

        lax.fori_loop(0, tt // rc, norm_chunk, 0, unroll=True)
        dw[...] += _dot_tn(y1buf[...], dy2b)
        _shifted_copies(dbuf, dsh, tt + 24)

        def conv_chunk(r, carry):
            r0 = pl.multiple_of(r * rc, rc)
            rs = pl.ds(r0, rc)
            for g in range(N_HEADS):
                cs = slice(HEAD * g, HEAD * (g + 1))
                sglu = _sigmoid(glu_ref[rs, cs])
                val = val_ref[rs, cs]
                v = val * sglu
                dv = jnp.zeros((rc, HEAD), F32)
                for ref, lo, taps in _tap_slabs(dbuf, dsh, lambda k: CONV_K - 1 - k):
                    slab = ref[pl.ds(r0 + lo, rc + taps[-1][1]), cs]
                    for k, off in taps:
                        d_later = slab[off:off + rc]
                        dv = dv + cw_ref[g, k:k + 1, :] * d_later
                        dcw_ref[g, k] += _rowsum8(v * d_later)
                dz_ref[rs, cs] = (dv * sglu).astype(BF16)
                dz_ref[rs, D + HEAD * g:D + HEAD * (g + 1)] = (
                    dv * val * sglu * (1.0 - sglu)).astype(BF16)
            return carry

        lax.fori_loop(0, tt // rc, conv_chunk, 0, unroll=2)
        dbuf[tt:tt + CONV_PAD, :] = dbuf[0:CONV_PAD, :]

        @pl.when(i == n_steps - 1)
        def _():
            pltpu.sync_copy(dw, dw_hbm)
            for g in range(N_HEADS):
                for k in range(CONV_K):
                    dcw_out[g, k:k + 1, :] = jnp.sum(dcw_ref[g, k], axis=0, keepdims=True)

    rev = lambda i: n_steps - 1 - i
    col = lambda j: pl.BlockSpec((tt, D), lambda i: (rev(i), j))
    row = pl.BlockSpec((tt, D), lambda i: (rev(i), 0))
    acc = _full((SUBLANES, D))
    n = len(modes)
    return pl.pallas_call(
        _hosted(body, 11, 7, (n_steps,), modes), name="conv_bwd", grid=(n_steps,),
        out_shape=[jax.ShapeDtypeStruct((t, N_COLS), BF16), jax.ShapeDtypeStruct((D, D), F32),
                   jax.ShapeDtypeStruct((N_DEV, CONV_PAD, HEAD), F32)]
        + [jax.ShapeDtypeStruct((SUBLANES, D), F32)] * 4 + _recv_shapes(grads, modes),
        in_specs=[row, col(0), col(1), col(2), row, row, _full((N_DEV, CONV_PAD, HEAD)),
                  _full((1, D)), _full((1, D)), ANY, ANY] + [ANY] * n,
        out_specs=[pl.BlockSpec((tt, 3 * D), lambda i: (rev(i), 0)), ANY,
                   _full((N_DEV, CONV_PAD, HEAD)), acc, acc, acc, acc] + [ANY] * n,
        input_output_aliases={10: 0},
        scratch_shapes=[pltpu.VMEM((D, D), BF16), pltpu.VMEM((D, D), F32),
                        pltpu.VMEM((tt + CONV_PAD, D), F32),
                        pltpu.VMEM((SUBLANES, tt + CONV_PAD, D), F32),
                        pltpu.VMEM((tt, D), BF16), pltpu.VMEM((tt, D), F32),
                        pltpu.VMEM((N_DEV, CONV_PAD, SUBLANES, HEAD), F32)] + _exchange_scratch(n),
        compiler_params=_params(52, dimension_semantics=("arbitrary",)),
    )(dy, z, z, z, yc, y2, conv_w_all, cn_g, cn_b, w_pw2, dz, *grads)


def _inproj_bwd_dx(dz, x, d_h1, ln_g, w_in_all, grads, modes):
    t = x.shape[0]
    tt = min(256, t)

    def body(dz_ref, x_ref, dh1_ref, g_ref, w_hbm, dx_ref, dg_ref, w_vmem):
        @pl.when(pl.program_id(0) == 0)
        def _():
            for d in range(N_DEV):
                pltpu.sync_copy(w_hbm.at[d], w_vmem.at[
                    d // 2, :, pl.ds(COLS_PER_DEV * (d % 2), COLS_PER_DEV)])
            dg_ref[...] = jnp.zeros_like(dg_ref)

        du = jnp.zeros((tt, D), F32)
        for q in range(N_CHIPS):
            du = du + lax.dot_general(
                dz_ref[:, PAIR_COLS * q:PAIR_COLS * (q + 1)], w_vmem[q],
                (((1,), (1,)), ((), ())), preferred_element_type=F32)
        xv = x_ref[...]
        rstd = lax.rsqrt(jnp.mean(xv * xv, axis=-1, keepdims=True) + EPS)
        xhat = xv * rstd
        dg_ref[...] += _rowsum8(du * xhat)
        dx_ref[...] = dh1_ref[...] + _rms_bwd(du * g_ref[...], xhat, rstd)

    row = pl.BlockSpec((tt, D), lambda i: (i, 0))
    n = len(modes)
    return pl.pallas_call(
        _hosted(body, 5, 2, (t // tt,), modes), name="inproj_bwd_dx", grid=(t // tt,),
        out_shape=[jax.ShapeDtypeStruct((t, D), F32), jax.ShapeDtypeStruct((SUBLANES, D), F32)]
        + _recv_shapes(grads, modes),
        in_specs=[pl.BlockSpec((tt, N_COLS), lambda i: (i, 0)), row, row, _full((1, D)), ANY]
        + [ANY] * n,
        out_specs=[row, _full((SUBLANES, D))] + [ANY] * n,
        scratch_shapes=[pltpu.VMEM((N_CHIPS, D, PAIR_COLS), BF16)] + _exchange_scratch(n),
        compiler_params=_params(48, dimension_semantics=("arbitrary",)),
    )(dz, x, d_h1, ln_g, w_in_all, *grads)


def _inproj_bwd_dw(name, u, dz, first, count, grads=(), modes=()):
    t = u.shape[0]
    tt = min(512, t)
    grid = (count // 2, t // tt)
    n = len(modes)

    def body(u_ref, dz_ref, dw_ref):
        @pl.when(pl.program_id(1) == 0)
        def _():
            dw_ref[...] = jnp.zeros_like(dw_ref)

        both = lax.dot_general(u_ref[...], dz_ref[...], (((0,), (0,)), ((), ())),
                               preferred_element_type=F32)
        dw_ref[0] += both[:, :COLS_PER_DEV]
        dw_ref[1] += both[:, COLS_PER_DEV:]

    return pl.pallas_call(
        _hosted(body, 2, 1, grid, modes) if n else body, name=name, grid=grid,
        out_shape=[jax.ShapeDtypeStruct((count, D, COLS_PER_DEV), F32)]
        + _recv_shapes(grads, modes),
        in_specs=[pl.BlockSpec((tt, D), lambda j, i: (i, 0)),
                  pl.BlockSpec((tt, PAIR_COLS), lambda j, i: (i, first // 2 + j))] + [ANY] * n,
        out_specs=[pl.BlockSpec((2, D, COLS_PER_DEV), lambda j, i: (j, 0, 0))] + [ANY] * n,
        scratch_shapes=_exchange_scratch(n) if n else [],
        compiler_params=_params(40, dimension_semantics=("arbitrary", "arbitrary")),
    )(u, dz, *grads)


def _adamw(w, g, m, v):
    m = ADAM_B1 * m + (1.0 - ADAM_B1) * g
    v = ADAM_B2 * v + (1.0 - ADAM_B2) * (g * g)
    m_hat = m / (1.0 - ADAM_B1 ** ADAM_STEP)
    v_hat = v / (1.0 - ADAM_B2 ** ADAM_STEP)
    delta = -ADAM_LR * (m_hat / (jnp.sqrt(v_hat) + ADAM_EPS) + ADAM_WD * w)
    return delta, m, v


def _pack_small(partials):
    rows = sorted(partials)

    def body(*refs):
        ins, out_ref = refs[:-1], refs[-1]
        out_ref[...] = jnp.zeros_like(out_ref)
        for j, row in enumerate(rows):
            out_ref[row:row + 1, :] = jnp.sum(ins[j][...], axis=0, keepdims=True)

    return pl.pallas_call(
        body, name="pack_small", out_shape=jax.ShapeDtypeStruct((N_SMALL, D), F32),
    )(*[partials[row] for row in rows])


def _sum_adam(name, recvs, w, m, v, rows):
    r, c = w.shape
    n = len(recvs)

    def body(*refs):
        w_ref, m_ref, v_ref, g_ref, d_ref, mo_ref, vo_ref = refs[n:]

        def finish(recv_ref):
            g = recv_ref[0]
            for s in range(1, recv_ref.shape[0]):
                g = g + recv_ref[s]
            g_ref[...] = g
            d_ref[...], mo_ref[...], vo_ref[...] = _adamw(w_ref[...], g, m_ref[...], v_ref[...])

        if n == 1:
            finish(refs[0])
        else:
            for side in range(n):
                pl.when(lax.axis_index("x") == side)(functools.partial(finish, refs[side]))

    blk = pl.BlockSpec((rows, c), lambda i: (i, 0))
    return pl.pallas_call(
        body, name=name, grid=(r // rows,),
        out_shape=[jax.ShapeDtypeStruct((r, c), F32)] * 4,
        in_specs=[pl.BlockSpec((rv.shape[0], rows, c), lambda i: (0, i, 0)) for rv in recvs]
        + [blk, blk, blk],
        out_specs=[blk] * 4,
        compiler_params=_params(48, dimension_semantics=("arbitrary",)),
    )(*recvs, w, m, v)


def _small_adam(gathered, lb_logits, w, m, v):
    def body(ga_ref, lbl_ref, w_ref, m_ref, v_ref, g_ref, d_ref, mo_ref, vo_ref, loss_ref):
        g = ga_ref[0]
        for s in range(1, N_DEV):
            g = g + ga_ref[s]
        s0, s1 = _lower_bound(lbl_ref[...])
        d_lb = g[R_LB0:R_LB0 + 1, :]
        rows = lax.broadcasted_iota(jnp.int32, (N_SMALL, D), 0)
        g = jnp.where(rows == R_LB0, d_lb * s0 * (1.0 - s0), g)
        g = jnp.where(rows == R_LB1, -d_lb * s0 * s1, g)
        g_ref[...] = g
        d_ref[...], mo_ref[...], vo_ref[...] = _adamw(w_ref[...], g, m_ref[...], v_ref[...])
        loss_ref[...] = (0.5 / D) * jnp.sum(g[R_LOSS:R_LOSS + 1, :], axis=-1, keepdims=True)

    return pl.pallas_call(
        body, name="small_adam",
        out_shape=[jax.ShapeDtypeStruct((N_SMALL, D), F32)] * 4 + [jax.ShapeDtypeStruct((1, 1), F32)],
    )(gathered, lb_logits, w, m, v)


def _pad_rows(a, rows):
    return jnp.pad(a, ((0, rows - a.shape[0]), (0, 0)))


def _pack_rows(rows):
    rows = [r.reshape(-1, D) for r in rows]
    packed = jnp.concatenate(rows, axis=0)
    return _pad_rows(packed, N_SMALL)


def kernel(x, p, ln_g, w_in, conv_w, conv_b, cnorm_g, cnorm_b, w_pw2, b_pw2, lb_logits, onorm_g, w_out, pe_norm_g, w_pg, w_pp, final_g, loss_target, m_ln_g, m_w_in, m_conv_w, m_conv_b, m_cnorm_g, m_cnorm_b, m_w_pw2, m_b_pw2, m_lb_logits, m_onorm_g, m_w_out, m_pe_norm_g, m_w_pg, m_w_pp, m_final_g, v_ln_g, v_w_in, v_conv_w, v_conv_b, v_cnorm_g, v_cnorm_b, v_w_pw2, v_b_pw2, v_lb_logits, v_onorm_g, v_w_out, v_pe_norm_g, v_w_pg, v_w_pp, v_final_g):
    t = x.shape[1]
    x2 = x.reshape(t, D)
    p2 = p.reshape(t, PLE)
    tg2 = loss_target.reshape(t, D)
    fin_g = final_g.reshape(1, D)

    z, u, w_in_all = _inproj_fwd(x2, ln_g, w_in[0].astype(BF16))
    (o_raw, y_hgrn, states, conv_w_all, w_pw2_all, w_out_all, w_pg_all, w_pp_all) = _hgrn_fwd(
        z, lb_logits, onorm_g,
        [_pad_rows(conv_w[0], CONV_PAD), w_pw2[0].astype(BF16), w_out[0].astype(BF16),
         w_pg[0].astype(BF16), w_pp[0].astype(BF16)])
    w_pw2_full = w_pw2_all.reshape(D, D)
    w_out_full = w_out_all.reshape(2 * D, D)
    w_pg_full = w_pg_all.reshape(D, D)
    yc, y2, y_conv = _conv_fwd(z, conv_w_all, conv_b, cnorm_g, cnorm_b, w_pw2_full, b_pw2)

    (d_h1, dy_conv, dy_hgrn, d_w_out, d_w_pg, d_w_pp, d_pen_p, d_fin_p, loss_p) = _tail(
        x2, y_conv, y_hgrn, p2, tg2, w_out_full, w_pg_full, w_pp_all, pe_norm_g, fin_g)

    dz, d_on_p, d_lb_p, r_w_out, r_w_pg, r_w_pp = _hgrn_bwd(
        dy_hgrn, z, o_raw, states, lb_logits, onorm_g,
        [d_w_out.reshape(N_DEV, 2 * D // N_DEV, D), d_w_pg.reshape(N_DEV, D // N_DEV, D), d_w_pp])
    (d_w_in_hi,) = _inproj_bwd_dw("inproj_bwd_dw_hi", u, dz, N_DEV // 2, N_DEV // 2)
    dz, d_w_pw2, d_conv_w, d_b2_p, d_cng_p, d_cnb_p, d_cb_p, r_w_in_hi = _conv_bwd(
        dy_conv, z, yc, y2, conv_w_all, cnorm_g, cnorm_b, w_pw2_full, dz, [d_w_in_hi], [1])
    d_w_in_lo, r_w_pw2, r_conv_w = _inproj_bwd_dw(
        "inproj_bwd_dw_lo", u, dz, 0, N_DEV // 2,
        [d_w_pw2.reshape(N_DEV, D // N_DEV, D), d_conv_w], ["scatter", "scatter"])
    chip_lo = _pair_reduce("pair_reduce_lo", d_w_in_lo)
    grad_x, d_ln_p, r_w_in_lo = _inproj_bwd_dx(
        dz, x2, d_h1, ln_g, w_in_all, [chip_lo], [("chip", 0)])

    small = _pack_small({R_LN: d_ln_p, R_CONVB: d_cb_p, R_CNG: d_cng_p, R_CNB: d_cnb_p,
                         R_BPW2: d_b2_p, R_LB0: d_lb_p, R_ON: d_on_p, R_PEN: d_pen_p,
                         R_FIN: d_fin_p, R_LOSS: loss_p})
    (small_all,) = _exchange_call("gather_small", [small], ["gather"])

    big = {}
    big["w_in"] = _sum_adam("adam_w_in", [r_w_in_lo, r_w_in_hi], w_in[0], m_w_in[0], v_w_in[0], 128)
    cw = _sum_adam("adam_conv_w", [r_conv_w], _pad_rows(conv_w[0], CONV_PAD),
                   _pad_rows(m_conv_w[0], CONV_PAD), _pad_rows(v_conv_w[0], CONV_PAD), CONV_PAD)
    big["conv_w"] = [a[:CONV_K] for a in cw]
    big["w_pw2"] = _sum_adam("adam_w_pw2", [r_w_pw2], w_pw2[0], m_w_pw2[0], v_w_pw2[0], 128)
    big["w_out"] = _sum_adam("adam_w_out", [r_w_out], w_out[0], m_w_out[0], v_w_out[0], 128)
    big["w_pg"] = _sum_adam("adam_w_pg", [r_w_pg], w_pg[0], m_w_pg[0], v_w_pg[0], 128)
    big["w_pp"] = _sum_adam("adam_w_pp", [r_w_pp], w_pp[0], m_w_pp[0], v_w_pp[0], PLE)

    small_w = [ln_g, conv_b, cnorm_g, cnorm_b, b_pw2, lb_logits, onorm_g, pe_norm_g, final_g]
    small_m = [m_ln_g, m_conv_b, m_cnorm_g, m_cnorm_b, m_b_pw2, m_lb_logits, m_onorm_g,
               m_pe_norm_g, m_final_g]
    small_v = [v_ln_g, v_conv_b, v_cnorm_g, v_cnorm_b, v_b_pw2, v_lb_logits, v_onorm_g,
               v_pe_norm_g, v_final_g]
    sg, sd, sm, sv, loss = _small_adam(small_all, lb_logits, _pack_rows(small_w),
                                       _pack_rows(small_m), _pack_rows(small_v))

    small_rows = {"ln_g": (R_LN, 1), "conv_b": (R_CONVB, 1), "cnorm_g": (R_CNG, 1),
                  "cnorm_b": (R_CNB, 1), "b_pw2": (R_BPW2, 1), "lb_logits": (R_LB0, 2),
                  "onorm_g": (R_ON, 1), "pe_norm_g": (R_PEN, 1), "final_g": (R_FIN, 1)}
    order = ["ln_g", "w_in", "conv_w", "conv_b", "cnorm_g", "cnorm_b", "w_pw2", "b_pw2",
             "lb_logits", "onorm_g", "w_out", "pe_norm_g", "w_pg", "w_pp", "final_g"]

    def leaf(kind, name):
        if name in big:
            return big[name][kind][None]
        r0, n = small_rows[name]
        a = (sg, sd, sm, sv)[kind][r0:r0 + n]
        return a.reshape(D) if name == "final_g" else a

    outs = [loss.reshape(()), grad_x.reshape(1, t, D)]
    for kind in range(4):
        outs += [leaf(kind, name) for name in order]
    return tuple(outs)
```

```python
import functools

import jax
import jax.numpy as jnp
from jax import lax
from jax.experimental import pallas as pl
from jax.experimental.pallas import tpu as pltpu

F32 = jnp.float32
BF16 = jnp.bfloat16
MESH = pl.DeviceIdType.MESH

N_DEV = 8
D = 1024
N_COLS = 7 * D
COLS_PER_DEV = N_COLS // N_DEV
PLE = 256
HEAD = 128
N_HEADS = D // HEAD
CONV_K = 31
CONV_PAD = 32
CHUNK = 64
EPS = 1e-6
SUBLANES = 8

ADAM_LR = 0.001
ADAM_B1 = 0.9
ADAM_B2 = 0.999
ADAM_EPS = 1e-08
ADAM_WD = 0.01
ADAM_STEP = 10

MIB = 1024 * 1024
N_SMALL = 16
R_LN, R_CONVB, R_CNG, R_CNB, R_BPW2, R_LB0, R_LB1, R_ON, R_PEN, R_FIN, R_LOSS = range(11)


def _params(vmem_mib, **kw):
    return pltpu.CompilerParams(vmem_limit_bytes=vmem_mib * MIB, **kw)


def _dot(a, b):
    return jnp.dot(a.astype(BF16), b.astype(BF16), preferred_element_type=F32)


def _dot_nt(a, b):
    return lax.dot_general(a.astype(BF16), b.astype(BF16), (((1,), (1,)), ((), ())),
                           preferred_element_type=F32)


def _dot_tn(a, b):
    return lax.dot_general(a.astype(BF16), b.astype(BF16), (((0,), (0,)), ((), ())),
                           preferred_element_type=F32)


def _split(a):
    hi = a.astype(BF16)
    return hi, (a - hi.astype(F32)).astype(BF16)


def _sigmoid(x):
    return 1.0 / (1.0 + jnp.exp(-x))


def _rowsum8(a):
    r, c = a.shape
    return jnp.sum(a.reshape(r // SUBLANES, SUBLANES, c), axis=0)


def _tri_dot(tri, a):
    hi = a.astype(BF16)
    r1 = a - hi.astype(F32)
    mid = r1.astype(BF16)
    lo = (r1 - mid.astype(F32)).astype(BF16)
    return (jnp.dot(tri, hi, preferred_element_type=F32)
            + jnp.dot(tri, mid, preferred_element_type=F32)
            + jnp.dot(tri, lo, preferred_element_type=F32))


def _lower_bound(lbl):
    l0, l1 = lbl[0:1, :], lbl[1:2, :]
    m = jnp.maximum(l0, l1)
    e0, e1 = jnp.exp(l0 - m), jnp.exp(l1 - m)
    s = e0 + e1
    return e0 / s, e1 / s


ANY = pl.BlockSpec(memory_space=pl.ANY)


def _full(shape):
    return pl.BlockSpec(shape, lambda i: (0,) * len(shape))


def _peer(x, y, c, k):
    px = 1 - x if k & 4 else x
    py = 1 - y if k & 2 else y
    pc = 1 - c if k & 1 else c
    return (px, py, pc), 4 * px + 2 * py + pc


class _Exchange:
    def __init__(self, srcs, outs, modes, send_sems, recv_sems, local_sems):
        x, y, c = lax.axis_index("x"), lax.axis_index("y"), lax.axis_index("c")
        me = 4 * x + 2 * y + c
        self.starts, self.send_waits, self.recv_waits = [], [], []

        def remote(a, k, src, slot, peer, when):
            sem = a * N_DEV + k
            cp = pltpu.make_async_remote_copy(
                src_ref=src, dst_ref=outs[a].at[slot], send_sem=send_sems.at[sem],
                recv_sem=recv_sems.at[sem], device_id=peer, device_id_type=MESH)
            self.starts.append((when, cp.start))
            self.send_waits.append((when, cp.wait_send))

        def arrival(a, k, slot, when):
            sem = a * N_DEV + k
            cp = pltpu.make_async_remote_copy(
                src_ref=outs[a].at[slot], dst_ref=outs[a].at[slot], send_sem=send_sems.at[sem],
                recv_sem=recv_sems.at[sem], device_id=(x, y, c), device_id_type=MESH)
            self.recv_waits.append((when, cp.wait_recv))

        def local(a, src, slot, when):
            cp = pltpu.make_async_copy(src, outs[a].at[slot], local_sems.at[a])
            self.starts.append((when, cp.start))
            self.send_waits.append((when, cp.wait))

        for a, (src, mode) in enumerate(zip(srcs, modes)):
            if mode in ("gather", "scatter"):
                local(a, src if mode == "gather" else src.at[me], me, None)
                for k in range(1, N_DEV):
                    peer, peer_idx = _peer(x, y, c, k)
                    remote(a, k, src if mode == "gather" else src.at[peer_idx], me, peer, None)
                    arrival(a, k, peer_idx, None)
                continue
            if isinstance(mode, tuple):
                here, away = x == mode[1], x != mode[1]
                chip = 2 * x + y
                local(a, src.at[y], chip, here)
                remote(a, 1, src.at[1 - y], chip, (x, 1 - y, c), here)
                remote(a, 2, src.at[y], chip, (1 - x, y, c), away)
                remote(a, 3, src.at[1 - y], chip, (1 - x, 1 - y, c), away)
                arrival(a, 1, 2 * x + 1 - y, here)
                arrival(a, 2, 2 * (1 - x) + y, here)
                arrival(a, 3, 2 * (1 - x) + 1 - y, here)
                continue
            here, away = x == mode, x != mode
            for kk in range(4):
                py = 1 - y if kk & 2 else y
                pc = 1 - c if kk & 1 else c
                block = src.at[2 * py + pc]
                if kk == 0:
                    local(a, block, me, here)
                else:
                    remote(a, kk, block, me, (x, py, pc), here)
                remote(a, 4 + kk, block, me, (1 - x, py, pc), away)
            for k in range(1, N_DEV):
                arrival(a, k, _peer(x, y, c, k)[1], here)

    @staticmethod
    def _run(actions):
        for when, fn in actions:
            if when is None:
                fn()
            else:
                pl.when(when)(fn)

    def start(self):
        self._run(self.starts)

    def wait(self):
        self._run(self.recv_waits)
        self._run(self.send_waits)


def _exchange_scratch(n):
    return [pltpu.SemaphoreType.DMA((n * N_DEV,)), pltpu.SemaphoreType.DMA((n * N_DEV,)),
            pltpu.SemaphoreType.DMA((n,))]


def _recv_shapes(srcs, modes):
    def shape(s, m):
        if m == "gather":
            return (N_DEV,) + s.shape
        return (N_DEV // 2 if isinstance(m, tuple) else N_DEV,) + s.shape[1:]

    return [jax.ShapeDtypeStruct(shape(s, m), s.dtype) for s, m in zip(srcs, modes)]


def _pair_reduce(name, blocks):
    shape = (2,) + blocks.shape[1:]

    def body(src, out_ref, stage, mine, send_sems, recv_sems, local_sems):
        x, y, c = lax.axis_index("x"), lax.axis_index("y"), lax.axis_index("c")
        sends, waits = [], []
        for py in range(2):
            sends.append(pltpu.make_async_remote_copy(
                src_ref=src.at[2 * py + 1 - c], dst_ref=stage.at[py], send_sem=send_sems.at[py],
                recv_sem=recv_sems.at[py], device_id=(x, y, 1 - c), device_id_type=MESH))
            waits.append(pltpu.make_async_copy(src.at[2 * py + c], mine.at[py], local_sems.at[py]))
        for cp in sends + waits:
            cp.start()
        for cp in waits:
            cp.wait()
        for cp in sends:
            cp.wait_recv()
        out_ref[...] = mine[...] + stage[...]
        for cp in sends:
            cp.wait_send()

    return pl.pallas_call(
        body, name=name, out_shape=jax.ShapeDtypeStruct(shape, F32), in_specs=[ANY],
        scratch_shapes=[pltpu.VMEM(shape, F32), pltpu.VMEM(shape, F32),
                        pltpu.SemaphoreType.DMA((2,)), pltpu.SemaphoreType.DMA((2,)),
                        pltpu.SemaphoreType.DMA((2,))],
        compiler_params=_params(40),
    )(blocks)


def _exchange_call(name, srcs, modes):
    n = len(srcs)

    def body(*refs):
        xch = _Exchange(refs[:n], refs[n:2 * n], modes, *refs[2 * n:])
        xch.start()
        xch.wait()

    return pl.pallas_call(
        body, name=name, out_shape=_recv_shapes(srcs, modes),
        in_specs=[ANY] * n, out_specs=[ANY] * n, scratch_shapes=_exchange_scratch(n),
    )(*srcs)


def _hosted(body, n_in, n_out, grid, modes):
    n = len(modes)

    def hosted(*refs):
        ins, srcs = refs[:n_in], refs[n_in:n_in + n]
        outs = refs[n_in + n:n_in + n + n_out]
        bufs = refs[n_in + n + n_out:n_in + 2 * n + n_out]
        scratch = refs[n_in + 2 * n + n_out:-3]
        xch = _Exchange(srcs, bufs, modes, *refs[-3:])
        first, last = True, True
        for axis, size in enumerate(grid):
            first = jnp.logical_and(first, pl.program_id(axis) == 0)
            last = jnp.logical_and(last, pl.program_id(axis) == size - 1)
        pl.when(first)(xch.start)
        body(*ins, *outs, *scratch)
        pl.when(last)(xch.wait)

    return hosted


N_CHIPS = N_DEV // 2
PAIR_COLS = 2 * COLS_PER_DEV
PUSHED = (1, 2, 4, 6)
FORWARDED = (2, 4, 6)
NORM_ROWS = 32


def _inproj_fwd(x, ln_g, w_shard):
    t = x.shape[0]
    tt = min(512, t)
    n_t = t // tt
    chip = 2 * lax.axis_index("x") + lax.axis_index("y")
    order = jnp.bitwise_xor(chip, jnp.arange(N_CHIPS, dtype=jnp.int32)).astype(jnp.int32)

    def body(order_ref, x_ref, g_ref, shard_hbm, z_ref, u_ref, w_all,
             u_all, w_blk, w_send, w_recv, w_local):
        p, i = pl.program_id(0), pl.program_id(1)
        x, y, c = lax.axis_index("x"), lax.axis_index("y"), lax.axis_index("c")
        mine = 4 * x + 2 * y + c

        def push(k):
            peer, _ = _peer(x, y, c, k)
            return pltpu.make_async_remote_copy(
                src_ref=shard_hbm, dst_ref=w_all.at[mine], send_sem=w_send.at[k],
                recv_sem=w_recv.at[k], device_id=peer, device_id_type=MESH)

        def forward(k):
            _, owner = _peer(x, y, c, k)
            return pltpu.make_async_remote_copy(
                src_ref=w_all.at[owner], dst_ref=w_all.at[owner], send_sem=w_send.at[k + 1],
                recv_sem=w_recv.at[k + 1], device_id=(x, y, 1 - c), device_id_type=MESH)

        def landed(k):
            _, owner = _peer(x, y, c, k)
            return pltpu.make_async_remote_copy(
                src_ref=w_all.at[owner], dst_ref=w_all.at[owner], send_sem=w_send.at[k],
                recv_sem=w_recv.at[k], device_id=(x, y, c), device_id_type=MESH)

        keep = pltpu.make_async_copy(shard_hbm, w_all.at[mine], w_local.at[0])

        def load_pair(step):
            same = shard_hbm if step == 0 else w_all.at[_peer(x, y, c, 2 * step)[1]]
            other = w_all.at[_peer(x, y, c, 2 * step + 1)[1]]
            for side in range(2):
                @pl.when(c == side)
                def _(side=side):
                    pltpu.sync_copy(same, w_blk.at[:, pl.ds(COLS_PER_DEV * side, COLS_PER_DEV)])
                    pltpu.sync_copy(
                        other, w_blk.at[:, pl.ds(COLS_PER_DEV * (1 - side), COLS_PER_DEV)])

        @pl.when(jnp.logical_and(p == 0, i == 0))
        def _():
            for k in PUSHED[:-1]:
                push(k).start()
            keep.start()

        @pl.when(jnp.logical_and(p == 1, i == 0))
        def _():
            for k in PUSHED[1:-1]:
                push(k).wait_send()
            push(PUSHED[-1]).start()

        for step in range(N_CHIPS):
            @pl.when(jnp.logical_and(p == step, i == 0))
            def _(step=step):
                landed(2 * step + 1).wait_recv()
                load_pair(step)

        rows = pl.ds(pl.multiple_of(i * tt, tt), tt)

        @pl.when(p == 0)
        def _():
            def norm_rows(r, carry):
                sub = pl.ds(pl.multiple_of(r * NORM_ROWS, NORM_ROWS), NORM_ROWS)
                xv = x_ref[sub, :]
                rstd = lax.rsqrt(jnp.mean(xv * xv, axis=-1, keepdims=True) + EPS)
                ub = (xv * rstd * g_ref[...]).astype(BF16)
                u_ref[sub, :] = ub
                u_all[pl.ds(pl.multiple_of(i * tt + r * NORM_ROWS, NORM_ROWS), NORM_ROWS), :] = ub
                return carry

            lax.fori_loop(0, tt // NORM_ROWS, norm_rows, 0, unroll=2)

        z_ref[...] = jnp.dot(u_all[rows, :], w_blk[...], preferred_element_type=F32)

        for step in range(1, N_CHIPS):
            @pl.when(jnp.logical_and(p == step - 1, i == n_t - 1))
            def _(step=step):
                landed(2 * step).wait_recv()
                forward(2 * step).start()

        @pl.when(jnp.logical_and(p == N_CHIPS - 1, i == n_t - 1))
        def _():
            push(PUSHED[0]).wait_send()
            push(PUSHED[-1]).wait_send()
            for k in FORWARDED:
                forward(k).wait_send()
            keep.wait()

    first_pass = lambda p, i, order_ref: (jnp.where(p == 0, i, n_t - 1), 0)
    grid_spec = pltpu.PrefetchScalarGridSpec(
        num_scalar_prefetch=1, grid=(N_CHIPS, n_t),
        in_specs=[pl.BlockSpec((tt, D), first_pass),
                  pl.BlockSpec((1, D), lambda p, i, order_ref: (0, 0)), ANY],
        out_specs=[pl.BlockSpec((tt, PAIR_COLS), lambda p, i, order_ref: (i, order_ref[p])),
                   pl.BlockSpec((tt, D), first_pass), ANY],
        scratch_shapes=[pltpu.VMEM((t, D), BF16), pltpu.VMEM((D, PAIR_COLS), BF16),
                        pltpu.SemaphoreType.DMA((N_DEV,)), pltpu.SemaphoreType.DMA((N_DEV,)),
                        pltpu.SemaphoreType.DMA((1,))])
    return pl.pallas_call(
        body, name="inproj_fwd", grid_spec=grid_spec,
        out_shape=[jax.ShapeDtypeStruct((t, N_COLS), F32), jax.ShapeDtypeStruct((t, D), BF16),
                   jax.ShapeDtypeStruct((N_DEV,) + w_shard.shape, BF16)],
        compiler_params=_params(48, dimension_semantics=("arbitrary", "arbitrary")),
    )(order, x, ln_g, w_shard)


def _shifted_copies(buf, shifted, rows):
    for b in range(1, SUBLANES):
        shifted[b, 0:rows, :] = buf[b:b + rows, :]


def _tap_slabs(buf, shifted, offset_of_tap):
    groups = {}
    for k in range(CONV_K):
        a, b = divmod(offset_of_tap(k), SUBLANES)
        groups.setdefault(b, []).append((SUBLANES * a, k))
    out = []
    for b, taps in sorted(groups.items()):
        taps.sort()
        lo = taps[0][0]
        out.append((buf if b == 0 else shifted.at[b], lo, [(k, off - lo) for off, k in taps]))
    return out


def _group_norm_stats(blk):
    mu = jnp.mean(blk, axis=-1, keepdims=True)
    cen = blk - mu
    var = jnp.mean(cen * cen, axis=-1, keepdims=True)
    return cen * lax.rsqrt(var + EPS)


def _conv_fwd(z, conv_w_all, conv_b, cn_g, cn_b, w_pw2, b_pw2):
    t = z.shape[0]
    tt = min(256, t)
    rc = 128

    def body(val_ref, glu_ref, gate_ref, cw_ref, cb_ref, g_ref, b_ref, w_hbm, b2_ref,
             yc_ref, y2_ref, yo_ref, w_vmem, vbuf, vsh, y1buf):
        @pl.when(pl.program_id(0) == 0)
        def _():
            pltpu.sync_copy(w_hbm, w_vmem)
            vbuf[0:CONV_PAD, :] = jnp.zeros((CONV_PAD, D), F32)

        vbuf[CONV_PAD:CONV_PAD + tt, :] = val_ref[...] * _sigmoid(glu_ref[...])
        _shifted_copies(vbuf, vsh, tt + 24)

        for g in range(N_HEADS):
            cs = slice(HEAD * g, HEAD * (g + 1))

            def row_chunk(r, carry, g=g, cs=cs):
                r0 = pl.multiple_of(r * rc, rc)
                acc = jnp.broadcast_to(cb_ref[:, cs], (rc, HEAD))
                for ref, lo, taps in _tap_slabs(vbuf, vsh, lambda k: k + 2):
                    slab = ref[pl.ds(r0 + lo, rc + taps[-1][1]), cs]
                    for k, off in taps:
                        acc = acc + cw_ref[g, k:k + 1, :] * slab[off:off + rc]
                yc_ref[pl.ds(r0, rc), cs] = acc
                n = _group_norm_stats(acc) * g_ref[:, cs] + b_ref[:, cs]
                y1buf[pl.ds(r0, rc), cs] = (n * _sigmoid(n)).astype(BF16)
                return carry

            lax.fori_loop(0, tt // rc, row_chunk, 0, unroll=True)
        vbuf[0:CONV_PAD, :] = vbuf[tt:tt + CONV_PAD, :]
        y2 = jnp.dot(y1buf[...], w_vmem[...], preferred_element_type=F32) + b2_ref[...]
        y2_ref[...] = y2
        gate = gate_ref[...]
        yo_ref[...] = (y2 * gate * _sigmoid(gate)).astype(BF16)

    col = lambda j: pl.BlockSpec((tt, D), lambda i: (i, j))
    row = pl.BlockSpec((tt, D), lambda i: (i, 0))
    return pl.pallas_call(
        body, name="conv_fwd", grid=(t // tt,),
        out_shape=[jax.ShapeDtypeStruct((t, D), F32), jax.ShapeDtypeStruct((t, D), F32),
                   jax.ShapeDtypeStruct((t, D), BF16)],
        in_specs=[col(0), col(1), col(2), _full((N_DEV, CONV_PAD, HEAD)), _full((1, D)),
                  _full((1, D)), _full((1, D)), ANY, _full((1, D))],
        out_specs=[row, row, row],
        scratch_shapes=[pltpu.VMEM((D, D), BF16), pltpu.VMEM((tt + CONV_PAD, D), F32),
                        pltpu.VMEM((SUBLANES, tt + CONV_PAD, D), F32), pltpu.VMEM((tt, D), BF16)],
        compiler_params=_params(48, dimension_semantics=("arbitrary",)),
    )(z, z, z, conv_w_all, conv_b, cn_g, cn_b, w_pw2, b_pw2)


HEAD_GROUP = 8
_HEAD_LANES = [slice(HEAD * j, HEAD * (j + 1)) for j in range(HEAD_GROUP)]


def _head_mean(a):
    return jnp.concatenate(
        [jnp.broadcast_to(jnp.mean(a[:, hs], axis=-1, keepdims=True), (a.shape[0], HEAD))
         for hs in _HEAD_LANES], axis=1)


def _chunk_quantities(zq, zf, lbh, tri):
    sig = _sigmoid(zf)
    sig_neg = _sigmoid(-zf)
    f = lbh + (1.0 - lbh) * sig
    k = (1.0 - lbh) * sig_neg
    q = zq * _sigmoid(zq)
    b = _tri_dot(tri, jnp.log(f))
    b_mid = b[CHUNK // 2 - 1:CHUNK // 2, :]
    b_last = b[CHUNK - 1:CHUNK, :]
    e_q = jnp.exp(b)
    e_qm = jnp.exp(b - b_mid)
    e_km = jnp.exp(b_mid - b)
    e_kd = jnp.exp(b_last - b)
    return q, k, f, sig, sig_neg, e_q, e_qm, e_km, e_kd, jnp.exp(b_last)


def _hgrn_fwd(z, lb_logits, onorm_g, shards):
    t = z.shape[0]
    tt = min(256, t)
    nc = tt // CHUNK
    modes = ["gather"] * len(shards)
    n = len(modes)

    def body(q_ref, f_ref, i_ref, g_ref, lbl_ref, on_ref, o_ref, y_ref, s_ref, st):
        @pl.when(pl.program_id(0) == 0)
        def _():
            st[...] = jnp.zeros_like(st)

        lb, _ = _lower_bound(lbl_ref[...])
        rows = lax.broadcasted_iota(jnp.int32, (CHUNK, CHUNK), 0)
        cols = lax.broadcasted_iota(jnp.int32, (CHUNK, CHUNK), 1)
        causal = rows >= cols
        tri = causal.astype(BF16)

        def chunk(c, carry):
            r0 = pl.multiple_of(c * CHUNK, CHUNK)
            rs = pl.ds(r0, CHUNK)
            for h0 in range(0, N_HEADS, HEAD_GROUP):
                cs = slice(HEAD * h0, HEAD * (h0 + HEAD_GROUP))
                q, k, _, _, _, e_q, e_qm, e_km, e_kd, e_last = _chunk_quantities(
                    q_ref[rs, cs], f_ref[rs, cs], lb[:, cs], tri)
                v = i_ref[rs, cs].astype(BF16)
                qm, km = (q * e_qm).astype(BF16), (k * e_km).astype(BF16)
                qt, kd = (q * e_q).astype(BF16), (k * e_kd).astype(BF16)
                outs = []
                for j, hs in enumerate(_HEAD_LANES):
                    s_old = st[h0 + j]
                    s_ref[c, h0 + j] = s_old.astype(BF16)
                    a = jnp.where(causal, _dot_nt(qm[:, hs], km[:, hs]), 0.0)
                    outs.append(_dot_nt(qt[:, hs], s_old) + _dot(a, v[:, hs]))
                    st[h0 + j] = s_old * e_last[:, hs] + _dot_tn(v[:, hs], kd[:, hs])
                o = jnp.concatenate(outs, axis=1)
                o_ref[rs, cs] = o
                n = o * lax.rsqrt(_head_mean(o * o) + EPS)
                zg = g_ref[rs, cs]
                y_ref[rs, cs] = (n * on_ref[:, cs] * zg * _sigmoid(zg)).astype(BF16)
            return carry

        lax.fori_loop(0, nc, chunk, 0, unroll=True)

    col = lambda j: pl.BlockSpec((tt, D), lambda i: (i, j))
    row = pl.BlockSpec((tt, D), lambda i: (i, 0))
    return pl.pallas_call(
        _hosted(body, 6, 3, (t // tt,), modes), name="hgrn_fwd", grid=(t // tt,),
        out_shape=[jax.ShapeDtypeStruct((t, D), F32), jax.ShapeDtypeStruct((t, D), BF16),
                   jax.ShapeDtypeStruct((t // CHUNK, N_HEADS, HEAD, HEAD), BF16)]
        + _recv_shapes(shards, modes),
        in_specs=[col(3), col(4), col(5), col(6), _full((2, D)), _full((1, D))] + [ANY] * n,
        out_specs=[row, row, pl.BlockSpec((nc, N_HEADS, HEAD, HEAD), lambda i: (i, 0, 0, 0))]
        + [ANY] * n,
        scratch_shapes=[pltpu.VMEM((N_HEADS, HEAD, HEAD), F32)] + _exchange_scratch(n),
        compiler_params=_params(40, dimension_semantics=("arbitrary",)),
    )(z, z, z, z, lb_logits, onorm_g, *shards)


def _rms_bwd(dn, xhat, rstd):
    return rstd * (dn - xhat * jnp.mean(dn * xhat, axis=-1, keepdims=True))


def _tail(x, y_conv, y_hgrn, p, target, w_out, w_pg, w_pp_all, pe_g, fin_g):
    t = x.shape[0]
    tt = min(256, t)
    n_steps = t // tt

    def body(x_ref, yc_ref, yh_ref, p_ref, tg_ref, wo_hbm, wg_hbm, wp_hbm, pg_ref, fg_ref,
             dh1_ref, dyc_ref, dyh_ref, dwo_hbm, dwg_hbm, dwp_hbm, dpg_ref, dfg_ref, loss_ref,
             wo, wg, wp, dwo, dwg, dwp):
        i = pl.program_id(0)

        @pl.when(i == 0)
        def _():
            pltpu.sync_copy(wo_hbm, wo)
            pltpu.sync_copy(wg_hbm, wg)
            for d in range(N_DEV):
                pltpu.sync_copy(wp_hbm.at[d], wp.at[:, pl.ds(HEAD * d, HEAD)])
            dwo[...] = jnp.zeros_like(dwo)
            dwg[...] = jnp.zeros_like(dwg)
            dwp[...] = jnp.zeros_like(dwp)
            dpg_ref[...] = jnp.zeros_like(dpg_ref)
            dfg_ref[...] = jnp.zeros_like(dfg_ref)
            loss_ref[...] = jnp.zeros_like(loss_ref)

        ycv, yhv = yc_ref[...], yh_ref[...]
        h1 = (x_ref[...] + jnp.dot(ycv, wo[0:D, :], preferred_element_type=F32)
              + jnp.dot(yhv, wo[D:2 * D, :], preferred_element_type=F32))
        pb = p_ref[...].astype(BF16)
        pe = jnp.dot(pb, wp[...], preferred_element_type=F32)
        rstd1 = lax.rsqrt(jnp.mean(h1 * h1, axis=-1, keepdims=True) + EPS)
        n1 = h1 * rstd1
        rb = (n1 * pg_ref[...]).astype(BF16)
        gate = _sigmoid(jnp.dot(rb, wg[...], preferred_element_type=F32))
        h2 = h1 + gate * pe
        rstd2 = lax.rsqrt(jnp.mean(h2 * h2, axis=-1, keepdims=True) + EPS)
        n2 = h2 * rstd2
        err = n2 * fg_ref[...] - tg_ref[...]
        loss_ref[...] += _rowsum8(err * err)

        d_out = err * (1.0 / D)
        dfg_ref[...] += _rowsum8(d_out * n2)
        d_h2 = _rms_bwd(d_out * fg_ref[...], n2, rstd2)
        d_pe = (d_h2 * gate).astype(BF16)
        d_gpre = (d_h2 * pe * gate * (1.0 - gate)).astype(BF16)
        dwg[...] += _dot_tn(rb, d_gpre)
        dwp[...] += _dot_tn(pb, d_pe)
        dr = _dot_nt(d_gpre, wg[...])
        dpg_ref[...] += _rowsum8(dr * n1)
        d_h1 = d_h2 + _rms_bwd(dr * pg_ref[...], n1, rstd1)
        dh1_ref[...] = d_h1
        d_h1b = d_h1.astype(BF16)
        dwo[0:D, :] += _dot_tn(ycv, d_h1b)
        dwo[D:2 * D, :] += _dot_tn(yhv, d_h1b)
        dyc_ref[...] = _dot_nt(d_h1b, wo[0:D, :])
        dyh_ref[...] = _dot_nt(d_h1b, wo[D:2 * D, :])

        @pl.when(i == n_steps - 1)
        def _():
            pltpu.sync_copy(dwo, dwo_hbm)
            pltpu.sync_copy(dwg, dwg_hbm)
            for d in range(N_DEV):
                pltpu.sync_copy(dwp.at[:, pl.ds(HEAD * d, HEAD)], dwp_hbm.at[d])

    row = pl.BlockSpec((tt, D), lambda i: (i, 0))
    acc = _full((SUBLANES, D))
    return pl.pallas_call(
        body, name="tail_fwd_bwd", grid=(n_steps,),
        out_shape=[jax.ShapeDtypeStruct((t, D), F32)] * 3
        + [jax.ShapeDtypeStruct((2 * D, D), F32), jax.ShapeDtypeStruct((D, D), F32),
           jax.ShapeDtypeStruct((N_DEV, PLE, HEAD), F32)]
        + [jax.ShapeDtypeStruct((SUBLANES, D), F32)] * 3,
        in_specs=[row, row, row, pl.BlockSpec((tt, PLE), lambda i: (i, 0)), row,
                  ANY, ANY, ANY, _full((1, D)), _full((1, D))],
        out_specs=[row, row, row, ANY, ANY, ANY, acc, acc, acc],
        scratch_shapes=[pltpu.VMEM((2 * D, D), BF16), pltpu.VMEM((D, D), BF16),
                        pltpu.VMEM((PLE, D), BF16), pltpu.VMEM((2 * D, D), F32),
                        pltpu.VMEM((D, D), F32), pltpu.VMEM((PLE, D), F32)],
        compiler_params=_params(52, dimension_semantics=("arbitrary",)),
    )(x, y_conv, y_hgrn, p, target, w_out, w_pg, w_pp_all, pe_g, fin_g)


def _hgrn_bwd(dy, z, o_raw, states, lb_logits, onorm_g, grads):
    t = z.shape[0]
    tt = min(256, t)
    nc = tt // CHUNK
    n_steps = t // tt
    modes = ["scatter"] * len(grads)

    def body(dy_ref, q_ref, f_ref, i_ref, g_ref, o_ref, s_ref, lbl_ref, on_ref,
             dz_ref, don_ref, dlb_ref, dst):
        @pl.when(pl.program_id(0) == 0)
        def _():
            dst[...] = jnp.zeros_like(dst)
            don_ref[...] = jnp.zeros_like(don_ref)
            dlb_ref[...] = jnp.zeros_like(dlb_ref)

        lb, _ = _lower_bound(lbl_ref[...])
        rows = lax.broadcasted_iota(jnp.int32, (CHUNK, CHUNK), 0)
        cols = lax.broadcasted_iota(jnp.int32, (CHUNK, CHUNK), 1)
        causal = rows >= cols
        tri = causal.astype(BF16)
        tri_rev = (rows <= cols).astype(BF16)
        width = HEAD * HEAD_GROUP
        is_last = lax.broadcasted_iota(jnp.int32, (CHUNK, width), 0) == CHUNK - 1
        nn = (((1,), (0,)), ((), ()))
        tn = (((0,), (0,)), ((), ()))
        dg = functools.partial(lax.dot_general, preferred_element_type=F32)

        def chunk(cc, carry):
            c = nc - 1 - cc
            r0 = pl.multiple_of(c * CHUNK, CHUNK)
            rs = pl.ds(r0, CHUNK)
            for h0 in range(0, N_HEADS, HEAD_GROUP):
                cs = slice(HEAD * h0, HEAD * h0 + width)
                zq, zf, zg = q_ref[rs, cs], f_ref[rs, cs], g_ref[rs, cs]
                lbh = lb[:, cs]
                q, k, f, sig, sig_neg, e_q, e_qm, e_km, e_kd, e_last = _chunk_quantities(
                    zq, zf, lbh, tri)
                vb = i_ref[rs, cs].astype(BF16)
                qt, qm, km, kd = q * e_q, q * e_qm, k * e_km, k * e_kd
                qt_b, kd_b = qt.astype(BF16), kd.astype(BF16)
                qm_h, qm_l = _split(qm)
                km_h, km_l = _split(km)

                o = o_ref[rs, cs]
                rstd = lax.rsqrt(_head_mean(o * o) + EPS)
                n = o * rstd
                sg = _sigmoid(zg)
                dyv = dy_ref[rs, cs]
                on = on_ref[:, cs]
                d_zg = dyv * n * on * sg * (1.0 + zg * (1.0 - sg))
                d_on = dyv * zg * sg
                don_ref[:, cs] += _rowsum8(d_on * n)
                dn = d_on * on
                do_b = (rstd * (dn - n * _head_mean(dn * n))).astype(BF16)

                dv, dkd, dqt, dqm, dkm, s_dots = [], [], [], [], [], []
                for j, hs in enumerate(_HEAD_LANES):
                    s_old, ds_new = s_ref[c, h0 + j], dst[h0 + j]
                    ds_b = ds_new.astype(BF16)
                    a = jnp.where(causal, _dot_nt(qm_h[:, hs], km_h[:, hs]), 0.0)
                    da = jnp.where(causal, _dot_nt(do_b[:, hs], vb[:, hs]), 0.0)
                    dv.append(_dot_tn(a, do_b[:, hs]) + _dot_nt(kd_b[:, hs], ds_b))
                    dkd.append(_dot(vb[:, hs], ds_b))
                    dqt.append(_dot(do_b[:, hs], s_old))
                    da_h, da_l = _split(da)
                    dqm.append(dg(da_h, km_h[:, hs], nn)
                               + (dg(da_h, km_l[:, hs], nn) + dg(da_l, km_h[:, hs], nn)))
                    dkm.append(dg(da_h, qm_h[:, hs], tn)
                               + (dg(da_h, qm_l[:, hs], tn) + dg(da_l, qm_h[:, hs], tn)))
                    dst[h0 + j] = ds_new * e_last[:, hs] + _dot_tn(do_b[:, hs], qt_b[:, hs])
                    s_dots.append(jnp.sum(s_old.astype(F32) * ds_new, axis=0, keepdims=True))
                dv, dkd, dqt, dqm, dkm, s_dots = [
                    jnp.concatenate(parts, axis=1) for parts in (dv, dkd, dqt, dqm, dkm, s_dots)]
                dq = dqt * e_q + dqm * e_qm
                dk = dkm * e_km + dkd * e_kd
                last = jnp.sum(dkd * kd, axis=0, keepdims=True) + e_last * s_dots
                db = q * dq - k * dk + jnp.where(is_last, last, 0.0)
                dlogf = _tri_dot(tri_rev, db)
                common = sig_neg * (dlogf / f - dk)
                dlb_ref[:, cs] += _rowsum8(common)
                c0 = 3 * D + HEAD * h0
                sq = _sigmoid(zq)
                dz_ref[rs, c0:c0 + width] = (dq * sq * (1.0 + zq * (1.0 - sq))).astype(BF16)
                dz_ref[rs, D + c0:D + c0 + width] = ((1.0 - lbh) * sig * common).astype(BF16)
                dz_ref[rs, 2 * D + c0:2 * D + c0 + width] = dv.astype(BF16)
                dz_ref[rs, 3 * D + c0:3 * D + c0 + width] = d_zg.astype(BF16)
            return carry

        lax.fori_loop(0, nc, chunk, 0, unroll=True)

    rev = lambda i: n_steps - 1 - i
    col = lambda j: pl.BlockSpec((tt, D), lambda i: (rev(i), j))
    row = pl.BlockSpec((tt, D), lambda i: (rev(i), 0))
    acc = _full((SUBLANES, D))
    n = len(modes)
    return pl.pallas_call(
        _hosted(body, 9, 3, (n_steps,), modes), name="hgrn_bwd", grid=(n_steps,),
        out_shape=[jax.ShapeDtypeStruct((t, N_COLS), BF16),
                   jax.ShapeDtypeStruct((SUBLANES, D), F32),
                   jax.ShapeDtypeStruct((SUBLANES, D), F32)] + _recv_shapes(grads, modes),
        in_specs=[row, col(3), col(4), col(5), col(6), row,
                  pl.BlockSpec((nc, N_HEADS, HEAD, HEAD), lambda i: (rev(i), 0, 0, 0)),
                  _full((2, D)), _full((1, D))] + [ANY] * n,
        out_specs=[pl.BlockSpec((tt, N_COLS), lambda i: (rev(i), 0)), acc, acc] + [ANY] * n,
        scratch_shapes=[pltpu.VMEM((N_HEADS, HEAD, HEAD), F32)] + _exchange_scratch(n),
        compiler_params=_params(48, dimension_semantics=("arbitrary",)),
    )(dy, z, z, z, z, o_raw, states, lb_logits, onorm_g, *grads)


def _conv_bwd(dy, z, yc, y2, conv_w_all, cn_g, cn_b, w_pw2, dz, grads, modes):
    t = z.shape[0]
    tt = min(256, t)
    rc = 32
    n_steps = t // tt

    def body(dy_ref, val_ref, glu_ref, gate_ref, yc_ref, y2_ref, cw_ref, g_ref, b_ref, w_hbm,
             dz_in, dz_ref, dw_hbm, dcw_out, db2_ref, dg_ref, dbeta_ref, dcb_ref,
             w_vmem, dw, dbuf, dsh, y1buf, dnbuf, dcw_ref):
        i = pl.program_id(0)

        @pl.when(i == 0)
        def _():
            pltpu.sync_copy(w_hbm, w_vmem)
            dw[...] = jnp.zeros_like(dw)
            dbuf[tt:tt + CONV_PAD, :] = jnp.zeros((CONV_PAD, D), F32)
            dcw_ref[...] = jnp.zeros_like(dcw_ref)
            dcw_out[...] = jnp.zeros_like(dcw_out)
            db2_ref[...] = jnp.zeros_like(db2_ref)
            dg_ref[...] = jnp.zeros_like(dg_ref)
            dbeta_ref[...] = jnp.zeros_like(dbeta_ref)
            dcb_ref[...] = jnp.zeros_like(dcb_ref)

        gate = gate_ref[...]
        sg = _sigmoid(gate)
        dyv = dy_ref[...]
        dy2 = dyv * gate * sg
        dz_ref[:, 2 * D:3 * D] = (dyv * y2_ref[...] * sg * (1.0 + gate * (1.0 - sg))).astype(BF16)
        db2_ref[...] += _rowsum8(dy2)
        dy2b = dy2.astype(BF16)
        dnbuf[...] = _dot_nt(dy2b, w_vmem[...])

        def norm_chunk(r, carry):
            r0 = pl.multiple_of(r * rc, rc)
            rs = pl.ds(r0, rc)
            for g in range(N_HEADS):
                cs = slice(HEAD * g, HEAD * (g + 1))
                blk = yc_ref[rs, cs]
                mu = jnp.mean(blk, axis=-1, keepdims=True)
                cen = blk - mu
                rstd = lax.rsqrt(jnp.mean(cen * cen, axis=-1, keepdims=True) + EPS)
                xhat = cen * rstd
                n = xhat * g_ref[:, cs] + b_ref[:, cs]
                sn = _sigmoid(n)
                y1buf[rs, cs] = (n * sn).astype(BF16)
                dn = dnbuf[rs, cs] * sn * (1.0 + n * (1.0 - sn))
                dg_ref[:, cs] += _rowsum8(dn * xhat)
                dbeta_ref[:, cs] += _rowsum8(dn)
                dxh = dn * g_ref[:, cs]
                dyc = rstd * (dxh - jnp.mean(dxh, axis=-1, keepdims=True)
                              - xhat * jnp.mean(dxh * xhat, axis=-1, keepdims=True))
                dcb_ref[:, cs] += _rowsum8(dyc)
                dbuf[rs, cs] = dyc
            return carry

        lax.fori_loop(0, tt // rc, norm_chunk, 0, unroll=True)
        dw[...] += _dot_tn(y1buf[...], dy2b)
        _shifted_copies(dbuf, dsh, tt + 24)

        def conv_chunk(r, carry):
            r0 = pl.multiple_of(r * rc, rc)
            rs = pl.ds(r0, rc)
            for g in range(N_HEADS):
                cs = slice(HEAD * g, HEAD * (g + 1))
                sglu = _sigmoid(glu_ref[rs, cs])
                val = val_ref[rs, cs]
                v = val * sglu
                dv = jnp.zeros((rc, HEAD), F32)
                for ref, lo, taps in _tap_slabs(dbuf, dsh, lambda k: CONV_K - 1 - k):
                    slab = ref[pl.ds(r0 + lo, rc + taps[-1][1]), cs]
                    for k, off in taps:
                        d_later = slab[off:off + rc]
                        dv = dv + cw_ref[g, k:k + 1, :] * d_later
                        dcw_ref[g, k] += _rowsum8(v * d_later)
                dz_ref[rs, cs] = (dv * sglu).astype(BF16)
                dz_ref[rs, D + HEAD * g:D + HEAD * (g + 1)] = (
                    dv * val * sglu * (1.0 - sglu)).astype(BF16)
            return carry

        lax.fori_loop(0, tt // rc, conv_chunk, 0, unroll=2)
        dbuf[tt:tt + CONV_PAD, :] = dbuf[0:CONV_PAD, :]

        @pl.when(i == n_steps - 1)
        def _():
            pltpu.sync_copy(dw, dw_hbm)
            for g in range(N_HEADS):
                for k in range(CONV_K):
                    dcw_out[g, k:k + 1, :] = jnp.sum(dcw_ref[g, k], axis=0, keepdims=True)

    rev = lambda i: n_steps - 1 - i
    col = lambda j: pl.BlockSpec((tt, D), lambda i: (rev(i), j))
    row = pl.BlockSpec((tt, D), lambda i: (rev(i), 0))
    acc = _full((SUBLANES, D))
    n = len(modes)
    return pl.pallas_call(
        _hosted(body, 11, 7, (n_steps,), modes), name="conv_bwd", grid=(n_steps,),
        out_shape=[jax.ShapeDtypeStruct((t, N_COLS), BF16), jax.ShapeDtypeStruct((D, D), F32),
                   jax.ShapeDtypeStruct((N_DEV, CONV_PAD, HEAD), F32)]
        + [jax.ShapeDtypeStruct((SUBLANES, D), F32)] * 4 + _recv_shapes(grads, modes),
        in_specs=[row, col(0), col(1), col(2), row, row, _full((N_DEV, CONV_PAD, HEAD)),
                  _full((1, D)), _full((1, D)), ANY, ANY] + [ANY] * n,
        out_specs=[pl.BlockSpec((tt, 3 * D), lambda i: (rev(i), 0)), ANY,
                   _full((N_DEV, CONV_PAD, HEAD)), acc, acc, acc, acc] + [ANY] * n,
        input_output_aliases={10: 0},
        scratch_shapes=[pltpu.VMEM((D, D), BF16), pltpu.VMEM((D, D), F32),
                        pltpu.VMEM((tt + CONV_PAD, D), F32),
                        pltpu.VMEM((SUBLANES, tt + CONV_PAD, D), F32),
                        pltpu.VMEM((tt, D), BF16), pltpu.VMEM((tt, D), F32),
                        pltpu.VMEM((N_DEV, CONV_PAD, SUBLANES, HEAD), F32)] + _exchange_scratch(n),
        compiler_params=_params(52, dimension_semantics=("arbitrary",)),
    )(dy, z, z, z, yc, y2, conv_w_all, cn_g, cn_b, w_pw2, dz, *grads)


def _inproj_bwd_dx(dz, x, d_h1, ln_g, w_in_all, grads, modes):
    t = x.shape[0]
    tt = min(256, t)

    def body(dz_ref, x_ref, dh1_ref, g_ref, w_hbm, dx_ref, dg_ref, w_vmem):
        @pl.when(pl.program_id(0) == 0)
        def _():
            for d in range(N_DEV):
                pltpu.sync_copy(w_hbm.at[d], w_vmem.at[
                    d // 2, :, pl.ds(COLS_PER_DEV * (d % 2), COLS_PER_DEV)])
            dg_ref[...] = jnp.zeros_like(dg_ref)

        du = jnp.zeros((tt, D), F32)
        for q in range(N_CHIPS):
            du = du + lax.dot_general(
                dz_ref[:, PAIR_COLS * q:PAIR_COLS * (q + 1)], w_vmem[q],
                (((1,), (1,)), ((), ())), preferred_element_type=F32)
        xv = x_ref[...]
        rstd = lax.rsqrt(jnp.mean(xv * xv, axis=-1, keepdims=True) + EPS)
        xhat = xv * rstd
        dg_ref[...] += _rowsum8(du * xhat)
        dx_ref[...] = dh1_ref[...] + _rms_bwd(du * g_ref[...], xhat, rstd)

    row = pl.BlockSpec((tt, D), lambda i: (i, 0))
    n = len(modes)
    return pl.pallas_call(
        _hosted(body, 5, 2, (t // tt,), modes), name="inproj_bwd_dx", grid=(t // tt,),
        out_shape=[jax.ShapeDtypeStruct((t, D), F32), jax.ShapeDtypeStruct((SUBLANES, D), F32)]
        + _recv_shapes(grads, modes),
        in_specs=[pl.BlockSpec((tt, N_COLS), lambda i: (i, 0)), row, row, _full((1, D)), ANY]
        + [ANY] * n,
        out_specs=[row, _full((SUBLANES, D))] + [ANY] * n,
        scratch_shapes=[pltpu.VMEM((N_CHIPS, D, PAIR_COLS), BF16)] + _exchange_scratch(n),
        compiler_params=_params(48, dimension_semantics=("arbitrary",)),
    )(dz, x, d_h1, ln_g, w_in_all, *grads)


def _inproj_bwd_dw(name, u, dz, first, count, grads=(), modes=()):
    t = u.shape[0]
    tt = min(512, t)
    grid = (count // 2, t // tt)
    n = len(modes)

    def body(u_ref, dz_ref, dw_ref):
        @pl.when(pl.program_id(1) == 0)
        def _():
            dw_ref[...] = jnp.zeros_like(dw_ref)

        both = lax.dot_general(u_ref[...], dz_ref[...], (((0,), (0,)), ((), ())),
                               preferred_element_type=F32)
        dw_ref[0] += both[:, :COLS_PER_DEV]
        dw_ref[1] += both[:, COLS_PER_DEV:]

    return pl.pallas_call(
        _hosted(body, 2, 1, grid, modes) if n else body, name=name, grid=grid,
        out_shape=[jax.ShapeDtypeStruct((count, D, COLS_PER_DEV), F32)]
        + _recv_shapes(grads, modes),
        in_specs=[pl.BlockSpec((tt, D), lambda j, i: (i, 0)),
                  pl.BlockSpec((tt, PAIR_COLS), lambda j, i: (i, first // 2 + j))] + [ANY] * n,
        out_specs=[pl.BlockSpec((2, D, COLS_PER_DEV), lambda j, i: (j, 0, 0))] + [ANY] * n,
        scratch_shapes=_exchange_scratch(n) if n else [],
        compiler_params=_params(40, dimension_semantics=("arbitrary", "arbitrary")),
    )(u, dz, *grads)


def _adamw(w, g, m, v):
    m = ADAM_B1 * m + (1.0 - ADAM_B1) * g
    v = ADAM_B2 * v + (1.0 - ADAM_B2) * (g * g)
    m_hat = m / (1.0 - ADAM_B1 ** ADAM_STEP)
    v_hat = v / (1.0 - ADAM_B2 ** ADAM_STEP)
    delta = -ADAM_LR * (m_hat / (jnp.sqrt(v_hat) + ADAM_EPS) + ADAM_WD * w)
    return delta, m, v


def _pack_small(partials):
    rows = sorted(partials)

    def body(*refs):
        ins, out_ref = refs[:-1], refs[-1]
        out_ref[...] = jnp.zeros_like(out_ref)
        for j, row in enumerate(rows):
            out_ref[row:row + 1, :] = jnp.sum(ins[j][...], axis=0, keepdims=True)

    return pl.pallas_call(
        body, name="pack_small", out_shape=jax.ShapeDtypeStruct((N_SMALL, D), F32),
    )(*[partials[row] for row in rows])


def _sum_adam(name, recvs, w, m, v, rows):
    r, c = w.shape
    n = len(recvs)
    side = lax.axis_index("x").astype(jnp.int32).reshape(1)

    def body(side_ref, *refs):
        w_ref, m_ref, v_ref, g_ref, d_ref, mo_ref, vo_ref = refs[n:]

        def finish(recv_ref):
            g = recv_ref[0]
            for s in range(1, recv_ref.shape[0]):
                g = g + recv_ref[s]
            g_ref[...] = g
            d_ref[...], mo_ref[...], vo_ref[...] = _adamw(w_ref[...], g, m_ref[...], v_ref[...])

        if n == 1:
            finish(refs[0])
        else:
            for s in range(n):
                pl.when(side_ref[0] == s)(functools.partial(finish, refs[s]))

    def recv_spec(s, rv):
        if n == 1:
            return pl.BlockSpec((rv.shape[0], rows, c), lambda i, side_ref: (0, i, 0))
        return pl.BlockSpec((rv.shape[0], rows, c),
                            lambda i, side_ref: (0, jnp.where(side_ref[0] == s, i, 0), 0))

    blk = pl.BlockSpec((rows, c), lambda i, side_ref: (i, 0))
    grid_spec = pltpu.PrefetchScalarGridSpec(
        num_scalar_prefetch=1, grid=(r // rows,),
        in_specs=[recv_spec(s, rv) for s, rv in enumerate(recvs)] + [blk, blk, blk],
        out_specs=[blk] * 4)
    return pl.pallas_call(
        body, name=name, grid_spec=grid_spec,
        out_shape=[jax.ShapeDtypeStruct((r, c), F32)] * 4,
        compiler_params=_params(48, dimension_semantics=("arbitrary",)),
    )(side, *recvs, w, m, v)


def _small_adam(gathered, lb_logits, w, m, v):
    def body(ga_ref, lbl_ref, w_ref, m_ref, v_ref, g_ref, d_ref, mo_ref, vo_ref, loss_ref):
        g = ga_ref[0]
        for s in range(1, N_DEV):
            g = g + ga_ref[s]
        s0, s1 = _lower_bound(lbl_ref[...])
        d_lb = g[R_LB0:R_LB0 + 1, :]
        rows = lax.broadcasted_iota(jnp.int32, (N_SMALL, D), 0)
        g = jnp.where(rows == R_LB0, d_lb * s0 * (1.0 - s0), g)
        g = jnp.where(rows == R_LB1, -d_lb * s0 * s1, g)
        g_ref[...] = g
        d_ref[...], mo_ref[...], vo_ref[...] = _adamw(w_ref[...], g, m_ref[...], v_ref[...])
        loss_ref[...] = (0.5 / D) * jnp.sum(g[R_LOSS:R_LOSS + 1, :], axis=-1, keepdims=True)

    return pl.pallas_call(
        body, name="small_adam",
        out_shape=[jax.ShapeDtypeStruct((N_SMALL, D), F32)] * 4 + [jax.ShapeDtypeStruct((1, 1), F32)],
    )(gathered, lb_logits, w, m, v)


def _pad_rows(a, rows):
    return jnp.pad(a, ((0, rows - a.shape[0]), (0, 0)))


def _pack_rows(rows):
    rows = [r.reshape(-1, D) for r in rows]
    packed = jnp.concatenate(rows, axis=0)
    return _pad_rows(packed, N_SMALL)


def kernel(x, p, ln_g, w_in, conv_w, conv_b, cnorm_g, cnorm_b, w_pw2, b_pw2, lb_logits, onorm_g, w_out, pe_norm_g, w_pg, w_pp, final_g, loss_target, m_ln_g, m_w_in, m_conv_w, m_conv_b, m_cnorm_g, m_cnorm_b, m_w_pw2, m_b_pw2, m_lb_logits, m_onorm_g, m_w_out, m_pe_norm_g, m_w_pg, m_w_pp, m_final_g, v_ln_g, v_w_in, v_conv_w, v_conv_b, v_cnorm_g, v_cnorm_b, v_w_pw2, v_b_pw2, v_lb_logits, v_onorm_g, v_w_out, v_pe_norm_g, v_w_pg, v_w_pp, v_final_g):
    t = x.shape[1]
    x2 = x.reshape(t, D)
    p2 = p.reshape(t, PLE)
    tg2 = loss_target.reshape(t, D)
    fin_g = final_g.reshape(1, D)

    z, u, w_in_all = _inproj_fwd(x2, ln_g, w_in[0].astype(BF16))
    (o_raw, y_hgrn, states, conv_w_all, w_pw2_all, w_out_all, w_pg_all, w_pp_all) = _hgrn_fwd(
        z, lb_logits, onorm_g,
        [_pad_rows(conv_w[0], CONV_PAD), w_pw2[0].astype(BF16), w_out[0].astype(BF16),
         w_pg[0].astype(BF16), w_pp[0].astype(BF16)])
    w_pw2_full = w_pw2_all.reshape(D, D)
    w_out_full = w_out_all.reshape(2 * D, D)
    w_pg_full = w_pg_all.reshape(D, D)
    yc, y2, y_conv = _conv_fwd(z, conv_w_all, conv_b, cnorm_g, cnorm_b, w_pw2_full, b_pw2)

    (d_h1, dy_conv, dy_hgrn, d_w_out, d_w_pg, d_w_pp, d_pen_p, d_fin_p, loss_p) = _tail(
        x2, y_conv, y_hgrn, p2, tg2, w_out_full, w_pg_full, w_pp_all, pe_norm_g, fin_g)

    dz, d_on_p, d_lb_p, r_w_out, r_w_pg, r_w_pp = _hgrn_bwd(
        dy_hgrn, z, o_raw, states, lb_logits, onorm_g,
        [d_w_out.reshape(N_DEV, 2 * D // N_DEV, D), d_w_pg.reshape(N_DEV, D // N_DEV, D), d_w_pp])
    (d_w_in_hi,) = _inproj_bwd_dw("inproj_bwd_dw_hi", u, dz, N_DEV // 2, N_DEV // 2)
    dz, d_w_pw2, d_conv_w, d_b2_p, d_cng_p, d_cnb_p, d_cb_p, r_w_in_hi = _conv_bwd(
        dy_conv, z, yc, y2, conv_w_all, cnorm_g, cnorm_b, w_pw2_full, dz, [d_w_in_hi], [1])
    d_w_in_lo, r_w_pw2, r_conv_w = _inproj_bwd_dw(
        "inproj_bwd_dw_lo", u, dz, 0, N_DEV // 2,
        [d_w_pw2.reshape(N_DEV, D // N_DEV, D), d_conv_w], ["scatter", "scatter"])
    chip_lo = _pair_reduce("pair_reduce_lo", d_w_in_lo)
    grad_x, d_ln_p, r_w_in_lo = _inproj_bwd_dx(
        dz, x2, d_h1, ln_g, w_in_all, [chip_lo], [("chip", 0)])

    small = _pack_small({R_LN: d_ln_p, R_CONVB: d_cb_p, R_CNG: d_cng_p, R_CNB: d_cnb_p,
                         R_BPW2: d_b2_p, R_LB0: d_lb_p, R_ON: d_on_p, R_PEN: d_pen_p,
                         R_FIN: d_fin_p, R_LOSS: loss_p})
    (small_all,) = _exchange_call("gather_small", [small], ["gather"])

    big = {}
    big["w_in"] = _sum_adam("adam_w_in", [r_w_in_lo, r_w_in_hi], w_in[0], m_w_in[0], v_w_in[0], 128)
    cw = _sum_adam("adam_conv_w", [r_conv_w], _pad_rows(conv_w[0], CONV_PAD),
                   _pad_rows(m_conv_w[0], CONV_PAD), _pad_rows(v_conv_w[0], CONV_PAD), CONV_PAD)
    big["conv_w"] = [a[:CONV_K] for a in cw]
    big["w_pw2"] = _sum_adam("adam_w_pw2", [r_w_pw2], w_pw2[0], m_w_pw2[0], v_w_pw2[0], 128)
    big["w_out"] = _sum_adam("adam_w_out", [r_w_out], w_out[0], m_w_out[0], v_w_out[0], 128)
    big["w_pg"] = _sum_adam("adam_w_pg", [r_w_pg], w_pg[0], m_w_pg[0], v_w_pg[0], 128)
    big["w_pp"] = _sum_adam("adam_w_pp", [r_w_pp], w_pp[0], m_w_pp[0], v_w_pp[0], PLE)

    small_w = [ln_g, conv_b, cnorm_g, cnorm_b, b_pw2, lb_logits, onorm_g, pe_norm_g, final_g]
    small_m = [m_ln_g, m_conv_b, m_cnorm_g, m_cnorm_b, m_b_pw2, m_lb_logits, m_onorm_g,
               m_pe_norm_g, m_final_g]
    small_v = [v_ln_g, v_conv_b, v_cnorm_g, v_cnorm_b, v_b_pw2, v_lb_logits, v_onorm_g,
               v_pe_norm_g, v_final_g]
    sg, sd, sm, sv, loss = _small_adam(small_all, lb_logits, _pack_rows(small_w),
                                       _pack_rows(small_m), _pack_rows(small_v))

    small_rows = {"ln_g": (R_LN, 1), "conv_b": (R_CONVB, 1), "cnorm_g": (R_CNG, 1),
                  "cnorm_b": (R_CNB, 1), "b_pw2": (R_BPW2, 1), "lb_logits": (R_LB0, 2),
                  "onorm_g": (R_ON, 1), "pe_norm_g": (R_PEN, 1), "final_g": (R_FIN, 1)}
    order = ["ln_g", "w_in", "conv_w", "conv_b", "cnorm_g", "cnorm_b", "w_pw2", "b_pw2",
             "lb_logits", "onorm_g", "w_out", "pe_norm_g", "w_pg", "w_pp", "final_g"]

    def leaf(kind, name):
        if name in big:
            return big[name][kind][None]
        r0, n = small_rows[name]
        a = (sg, sd, sm, sv)[kind][r0:r0 + n]
        return a.reshape(D) if name == "final_g" else a

    outs = [loss.reshape(()), grad_x.reshape(1, t, D)]
    for kind in range(4):
        outs += [leaf(kind, name) for name in order]
    return tuple(outs)
```

```python
import functools

import jax
import jax.numpy as jnp
from jax import lax
from jax.experimental import pallas as pl
from jax.experimental.pallas import tpu as pltpu

F32 = jnp.float32
BF16 = jnp.bfloat16
MESH = pl.DeviceIdType.MESH

N_DEV = 8
D = 1024
N_COLS = 7 * D
COLS_PER_DEV = N_COLS // N_DEV
PLE = 256
HEAD = 128
N_HEADS = D // HEAD
CONV_K = 31
CONV_PAD = 32
CHUNK = 64
EPS = 1e-6
SUBLANES = 8

ADAM_LR = 0.001
ADAM_B1 = 0.9
ADAM_B2 = 0.999
ADAM_EPS = 1e-08
ADAM_WD = 0.01
ADAM_STEP = 10

MIB = 1024 * 1024
N_SMALL = 16
R_LN, R_CONVB, R_CNG, R_CNB, R_BPW2, R_LB0, R_LB1, R_ON, R_PEN, R_FIN, R_LOSS = range(11)


def _params(vmem_mib, **kw):
    return pltpu.CompilerParams(vmem_limit_bytes=vmem_mib * MIB, **kw)


def _dot(a, b):
    return jnp.dot(a.astype(BF16), b.astype(BF16), preferred_element_type=F32)


def _dot_nt(a, b):
    return lax.dot_general(a.astype(BF16), b.astype(BF16), (((1,), (1,)), ((), ())),
                           preferred_element_type=F32)


def _dot_tn(a, b):
    return lax.dot_general(a.astype(BF16), b.astype(BF16), (((0,), (0,)), ((), ())),
                           preferred_element_type=F32)


def _split(a):
    hi = a.astype(BF16)
    return hi, (a - hi.astype(F32)).astype(BF16)


def _sigmoid(x):
    return 1.0 / (1.0 + jnp.exp(-x))


def _rowsum8(a):
    r, c = a.shape
    return jnp.sum(a.reshape(r // SUBLANES, SUBLANES, c), axis=0)


def _tri_dot(tri, a):
    hi = a.astype(BF16)
    r1 = a - hi.astype(F32)
    mid = r1.astype(BF16)
    lo = (r1 - mid.astype(F32)).astype(BF16)
    return (jnp.dot(tri, hi, preferred_element_type=F32)
            + jnp.dot(tri, mid, preferred_element_type=F32)
            + jnp.dot(tri, lo, preferred_element_type=F32))


def _lower_bound(lbl):
    l0, l1 = lbl[0:1, :], lbl[1:2, :]
    m = jnp.maximum(l0, l1)
    e0, e1 = jnp.exp(l0 - m), jnp.exp(l1 - m)
    s = e0 + e1
    return e0 / s, e1 / s


ANY = pl.BlockSpec(memory_space=pl.ANY)


def _full(shape):
    return pl.BlockSpec(shape, lambda i: (0,) * len(shape))


def _peer(x, y, c, k):
    px = 1 - x if k & 4 else x
    py = 1 - y if k & 2 else y
    pc = 1 - c if k & 1 else c
    return (px, py, pc), 4 * px + 2 * py + pc


class _Exchange:
    def __init__(self, srcs, outs, modes, send_sems, recv_sems, local_sems):
        x, y, c = lax.axis_index("x"), lax.axis_index("y"), lax.axis_index("c")
        me = 4 * x + 2 * y + c
        self.starts, self.send_waits, self.recv_waits = [], [], []

        def remote(a, k, src, slot, peer, when):
            sem = a * N_DEV + k
            cp = pltpu.make_async_remote_copy(
                src_ref=src, dst_ref=outs[a].at[slot], send_sem=send_sems.at[sem],
                recv_sem=recv_sems.at[sem], device_id=peer, device_id_type=MESH)
            self.starts.append((when, cp.start))
            self.send_waits.append((when, cp.wait_send))

        def arrival(a, k, slot, when):
            sem = a * N_DEV + k
            cp = pltpu.make_async_remote_copy(
                src_ref=outs[a].at[slot], dst_ref=outs[a].at[slot], send_sem=send_sems.at[sem],
                recv_sem=recv_sems.at[sem], device_id=(x, y, c), device_id_type=MESH)
            self.recv_waits.append((when, cp.wait_recv))

        def local(a, src, slot, when):
            cp = pltpu.make_async_copy(src, outs[a].at[slot], local_sems.at[a])
            self.starts.append((when, cp.start))
            self.send_waits.append((when, cp.wait))

        for a, (src, mode) in enumerate(zip(srcs, modes)):
            if mode in ("gather", "scatter"):
                local(a, src if mode == "gather" else src.at[me], me, None)
                for k in range(1, N_DEV):
                    peer, peer_idx = _peer(x, y, c, k)
                    remote(a, k, src if mode == "gather" else src.at[peer_idx], me, peer, None)
                    arrival(a, k, peer_idx, None)
                continue
            if isinstance(mode, tuple):
                here, away = x == mode[1], x != mode[1]
                chip = 2 * x + y
                local(a, src.at[y], chip, here)
                remote(a, 1, src.at[1 - y], chip, (x, 1 - y, c), here)
                remote(a, 2, src.at[y], chip, (1 - x, y, c), away)
                remote(a, 3, src.at[1 - y], chip, (1 - x, 1 - y, c), away)
                arrival(a, 1, 2 * x + 1 - y, here)
                arrival(a, 2, 2 * (1 - x) + y, here)
                arrival(a, 3, 2 * (1 - x) + 1 - y, here)
                continue
            here, away = x == mode, x != mode
            for kk in range(4):
                py = 1 - y if kk & 2 else y
                pc = 1 - c if kk & 1 else c
                block = src.at[2 * py + pc]
                if kk == 0:
                    local(a, block, me, here)
                else:
                    remote(a, kk, block, me, (x, py, pc), here)
                remote(a, 4 + kk, block, me, (1 - x, py, pc), away)
            for k in range(1, N_DEV):
                arrival(a, k, _peer(x, y, c, k)[1], here)

    @staticmethod
    def _run(actions):
        for when, fn in actions:
            if when is None:
                fn()
            else:
                pl.when(when)(fn)

    def start(self):
        self._run(self.starts)

    def wait(self):
        self._run(self.recv_waits)
        self._run(self.send_waits)


def _exchange_scratch(n):
    return [pltpu.SemaphoreType.DMA((n * N_DEV,)), pltpu.SemaphoreType.DMA((n * N_DEV,)),
            pltpu.SemaphoreType.DMA((n,))]


def _recv_shapes(srcs, modes):
    def shape(s, m):
        if m == "gather":
            return (N_DEV,) + s.shape
        return (N_DEV // 2 if isinstance(m, tuple) else N_DEV,) + s.shape[1:]

    return [jax.ShapeDtypeStruct(shape(s, m), s.dtype) for s, m in zip(srcs, modes)]


def _pair_reduce(name, blocks):
    shape = (2,) + blocks.shape[1:]

    def body(src, out_ref, stage, mine, send_sems, recv_sems, local_sems):
        x, y, c = lax.axis_index("x"), lax.axis_index("y"), lax.axis_index("c")
        sends, waits = [], []
        for py in range(2):
            sends.append(pltpu.make_async_remote_copy(
                src_ref=src.at[2 * py + 1 - c], dst_ref=stage.at[py], send_sem=send_sems.at[py],
                recv_sem=recv_sems.at[py], device_id=(x, y, 1 - c), device_id_type=MESH))
            waits.append(pltpu.make_async_copy(src.at[2 * py + c], mine.at[py], local_sems.at[py]))
        for cp in sends + waits:
            cp.start()
        for cp in waits:
            cp.wait()
        for cp in sends:
            cp.wait_recv()
        out_ref[...] = mine[...] + stage[...]
        for cp in sends:
            cp.wait_send()

    return pl.pallas_call(
        body, name=name, out_shape=jax.ShapeDtypeStruct(shape, F32), in_specs=[ANY],
        scratch_shapes=[pltpu.VMEM(shape, F32), pltpu.VMEM(shape, F32),
                        pltpu.SemaphoreType.DMA((2,)), pltpu.SemaphoreType.DMA((2,)),
                        pltpu.SemaphoreType.DMA((2,))],
        compiler_params=_params(40),
    )(blocks)


def _exchange_call(name, srcs, modes):
    n = len(srcs)

    def body(*refs):
        xch = _Exchange(refs[:n], refs[n:2 * n], modes, *refs[2 * n:])
        xch.start()
        xch.wait()

    return pl.pallas_call(
        body, name=name, out_shape=_recv_shapes(srcs, modes),
        in_specs=[ANY] * n, out_specs=[ANY] * n, scratch_shapes=_exchange_scratch(n),
    )(*srcs)


def _hosted(body, n_in, n_out, grid, modes):
    n = len(modes)

    def hosted(*refs):
        ins, srcs = refs[:n_in], refs[n_in:n_in + n]
        outs = refs[n_in + n:n_in + n + n_out]
        bufs = refs[n_in + n + n_out:n_in + 2 * n + n_out]
        scratch = refs[n_in + 2 * n + n_out:-3]
        xch = _Exchange(srcs, bufs, modes, *refs[-3:])
        first, last = True, True
        for axis, size in enumerate(grid):
            first = jnp.logical_and(first, pl.program_id(axis) == 0)
            last = jnp.logical_and(last, pl.program_id(axis) == size - 1)
        pl.when(first)(xch.start)
        body(*ins, *outs, *scratch)
        pl.when(last)(xch.wait)

    return hosted


N_CHIPS = N_DEV // 2
PAIR_COLS = 2 * COLS_PER_DEV
PUSHED = (1, 2, 4, 6)
FORWARDED = (2, 4, 6)
NORM_ROWS = 32


def _inproj_fwd(x, ln_g, w_shard):
    t = x.shape[0]
    tt = min(512, t)
    n_t = t // tt
    chip = 2 * lax.axis_index("x") + lax.axis_index("y")
    order = jnp.bitwise_xor(chip, jnp.arange(N_CHIPS, dtype=jnp.int32)).astype(jnp.int32)

    def body(order_ref, x_ref, g_ref, shard_hbm, z_ref, u_ref, w_all,
             u_all, w_blk, w_send, w_recv, w_local):
        p, i = pl.program_id(0), pl.program_id(1)
        x, y, c = lax.axis_index("x"), lax.axis_index("y"), lax.axis_index("c")
        mine = 4 * x + 2 * y + c

        def push(k):
            peer, _ = _peer(x, y, c, k)
            return pltpu.make_async_remote_copy(
                src_ref=shard_hbm, dst_ref=w_all.at[mine], send_sem=w_send.at[k],
                recv_sem=w_recv.at[k], device_id=peer, device_id_type=MESH)

        def forward(k):
            _, owner = _peer(x, y, c, k)
            return pltpu.make_async_remote_copy(
                src_ref=w_all.at[owner], dst_ref=w_all.at[owner], send_sem=w_send.at[k + 1],
                recv_sem=w_recv.at[k + 1], device_id=(x, y, 1 - c), device_id_type=MESH)

        def landed(k):
            _, owner = _peer(x, y, c, k)
            return pltpu.make_async_remote_copy(
                src_ref=w_all.at[owner], dst_ref=w_all.at[owner], send_sem=w_send.at[k],
                recv_sem=w_recv.at[k], device_id=(x, y, c), device_id_type=MESH)

        keep = pltpu.make_async_copy(shard_hbm, w_all.at[mine], w_local.at[0])

        def load_pair(step):
            same = shard_hbm if step == 0 else w_all.at[_peer(x, y, c, 2 * step)[1]]
            other = w_all.at[_peer(x, y, c, 2 * step + 1)[1]]
            for side in range(2):
                @pl.when(c == side)
                def _(side=side):
                    pltpu.sync_copy(same, w_blk.at[:, pl.ds(COLS_PER_DEV * side, COLS_PER_DEV)])
                    pltpu.sync_copy(
                        other, w_blk.at[:, pl.ds(COLS_PER_DEV * (1 - side), COLS_PER_DEV)])

        @pl.when(jnp.logical_and(p == 0, i == 0))
        def _():
            for k in PUSHED[:-1]:
                push(k).start()
            keep.start()

        @pl.when(jnp.logical_and(p == 1, i == 0))
        def _():
            for k in PUSHED[1:-1]:
                push(k).wait_send()
            push(PUSHED[-1]).start()

        for step in range(N_CHIPS):
            @pl.when(jnp.logical_and(p == step, i == 0))
            def _(step=step):
                landed(2 * step + 1).wait_recv()
                load_pair(step)

        rows = pl.ds(pl.multiple_of(i * tt, tt), tt)

        @pl.when(p == 0)
        def _():
            def norm_rows(r, carry):
                sub = pl.ds(pl.multiple_of(r * NORM_ROWS, NORM_ROWS), NORM_ROWS)
                xv = x_ref[sub, :]
                rstd = lax.rsqrt(jnp.mean(xv * xv, axis=-1, keepdims=True) + EPS)
                ub = (xv * rstd * g_ref[...]).astype(BF16)
                u_ref[sub, :] = ub
                u_all[pl.ds(pl.multiple_of(i * tt + r * NORM_ROWS, NORM_ROWS), NORM_ROWS), :] = ub
                return carry

            lax.fori_loop(0, tt // NORM_ROWS, norm_rows, 0, unroll=2)

        z_ref[...] = jnp.dot(u_all[rows, :], w_blk[...], preferred_element_type=F32)

        for step in range(1, N_CHIPS):
            @pl.when(jnp.logical_and(p == step - 1, i == n_t - 1))
            def _(step=step):
                landed(2 * step).wait_recv()
                forward(2 * step).start()

        @pl.when(jnp.logical_and(p == N_CHIPS - 1, i == n_t - 1))
        def _():
            push(PUSHED[0]).wait_send()
            push(PUSHED[-1]).wait_send()
            for k in FORWARDED:
                forward(k).wait_send()
            keep.wait()

    first_pass = lambda p, i, order_ref: (jnp.where(p == 0, i, n_t - 1), 0)
    grid_spec = pltpu.PrefetchScalarGridSpec(
        num_scalar_prefetch=1, grid=(N_CHIPS, n_t),
        in_specs=[pl.BlockSpec((tt, D), first_pass),
                  pl.BlockSpec((1, D), lambda p, i, order_ref: (0, 0)), ANY],
        out_specs=[pl.BlockSpec((tt, PAIR_COLS), lambda p, i, order_ref: (i, order_ref[p])),
                   pl.BlockSpec((tt, D), first_pass), ANY],
        scratch_shapes=[pltpu.VMEM((t, D), BF16), pltpu.VMEM((D, PAIR_COLS), BF16),
                        pltpu.SemaphoreType.DMA((N_DEV,)), pltpu.SemaphoreType.DMA((N_DEV,)),
                        pltpu.SemaphoreType.DMA((1,))])
    return pl.pallas_call(
        body, name="inproj_fwd", grid_spec=grid_spec,
        out_shape=[jax.ShapeDtypeStruct((t, N_COLS), F32), jax.ShapeDtypeStruct((t, D), BF16),
                   jax.ShapeDtypeStruct((N_DEV,) + w_shard.shape, BF16)],
        compiler_params=_params(48, dimension_semantics=("arbitrary", "arbitrary")),
    )(order, x, ln_g, w_shard)


def _shifted_copies(buf, shifted, rows):
    for b in range(1, SUBLANES):
        shifted[b, 0:rows, :] = buf[b:b + rows, :]


def _tap_slabs(buf, shifted, offset_of_tap):
    groups = {}
    for k in range(CONV_K):
        a, b = divmod(offset_of_tap(k), SUBLANES)
        groups.setdefault(b, []).append((SUBLANES * a, k))
    out = []
    for b, taps in sorted(groups.items()):
        taps.sort()
        lo = taps[0][0]
        out.append((buf if b == 0 else shifted.at[b], lo, [(k, off - lo) for off, k in taps]))
    return out


def _group_norm_stats(blk):
    mu = jnp.mean(blk, axis=-1, keepdims=True)
    cen = blk - mu
    var = jnp.mean(cen * cen, axis=-1, keepdims=True)
    return cen * lax.rsqrt(var + EPS)


def _conv_fwd(z, conv_w_all, conv_b, cn_g, cn_b, w_pw2, b_pw2):
    t = z.shape[0]
    tt = min(256, t)
    rc = 128

    def body(val_ref, glu_ref, gate_ref, cw_ref, cb_ref, g_ref, b_ref, w_hbm, b2_ref,
             yc_ref, y2_ref, yo_ref, w_vmem, vbuf, vsh, y1buf):
        @pl.when(pl.program_id(0) == 0)
        def _():
            pltpu.sync_copy(w_hbm, w_vmem)
            vbuf[0:CONV_PAD, :] = jnp.zeros((CONV_PAD, D), F32)

        vbuf[CONV_PAD:CONV_PAD + tt, :] = val_ref[...] * _sigmoid(glu_ref[...])
        _shifted_copies(vbuf, vsh, tt + 24)

        for g in range(N_HEADS):
            cs = slice(HEAD * g, HEAD * (g + 1))

            def row_chunk(r, carry, g=g, cs=cs):
                r0 = pl.multiple_of(r * rc, rc)
                acc = jnp.broadcast_to(cb_ref[:, cs], (rc, HEAD))
                for ref, lo, taps in _tap_slabs(vbuf, vsh, lambda k: k + 2):
                    slab = ref[pl.ds(r0 + lo, rc + taps[-1][1]), cs]
                    for k, off in taps:
                        acc = acc + cw_ref[g, k:k + 1, :] * slab[off:off + rc]
                yc_ref[pl.ds(r0, rc), cs] = acc
                n = _group_norm_stats(acc) * g_ref[:, cs] + b_ref[:, cs]
                y1buf[pl.ds(r0, rc), cs] = (n * _sigmoid(n)).astype(BF16)
                return carry

            lax.fori_loop(0, tt // rc, row_chunk, 0, unroll=True)
        vbuf[0:CONV_PAD, :] = vbuf[tt:tt + CONV_PAD, :]
        y2 = jnp.dot(y1buf[...], w_vmem[...], preferred_element_type=F32) + b2_ref[...]
        y2_ref[...] = y2
        gate = gate_ref[...]
        yo_ref[...] = (y2 * gate * _sigmoid(gate)).astype(BF16)

    col = lambda j: pl.BlockSpec((tt, D), lambda i: (i, j))
    row = pl.BlockSpec((tt, D), lambda i: (i, 0))
    return pl.pallas_call(
        body, name="conv_fwd", grid=(t // tt,),
        out_shape=[jax.ShapeDtypeStruct((t, D), F32), jax.ShapeDtypeStruct((t, D), F32),
                   jax.ShapeDtypeStruct((t, D), BF16)],
        in_specs=[col(0), col(1), col(2), _full((N_DEV, CONV_PAD, HEAD)), _full((1, D)),
                  _full((1, D)), _full((1, D)), ANY, _full((1, D))],
        out_specs=[row, row, row],
        scratch_shapes=[pltpu.VMEM((D, D), BF16), pltpu.VMEM((tt + CONV_PAD, D), F32),
                        pltpu.VMEM((SUBLANES, tt + CONV_PAD, D), F32), pltpu.VMEM((tt, D), BF16)],
        compiler_params=_params(48, dimension_semantics=("arbitrary",)),
    )(z, z, z, conv_w_all, conv_b, cn_g, cn_b, w_pw2, b_pw2)


HEAD_GROUP = 8
_HEAD_LANES = [slice(HEAD * j, HEAD * (j + 1)) for j in range(HEAD_GROUP)]


def _head_mean(a):
    return jnp.concatenate(
        [jnp.broadcast_to(jnp.mean(a[:, hs], axis=-1, keepdims=True), (a.shape[0], HEAD))
         for hs in _HEAD_LANES], axis=1)


def _chunk_quantities(zq, zf, lbh, tri):
    sig = _sigmoid(zf)
    sig_neg = _sigmoid(-zf)
    f = lbh + (1.0 - lbh) * sig
    k = (1.0 - lbh) * sig_neg
    q = zq * _sigmoid(zq)
    b = _tri_dot(tri, jnp.log(f))
    b_mid = b[CHUNK // 2 - 1:CHUNK // 2, :]
    b_last = b[CHUNK - 1:CHUNK, :]
    e_q = jnp.exp(b)
    e_qm = jnp.exp(b - b_mid)
    e_km = jnp.exp(b_mid - b)
    e_kd = jnp.exp(b_last - b)
    return q, k, f, sig, sig_neg, e_q, e_qm, e_km, e_kd, jnp.exp(b_last)


def _hgrn_fwd(z, lb_logits, onorm_g, shards):
    t = z.shape[0]
    tt = min(256, t)
    nc = tt // CHUNK
    modes = ["gather"] * len(shards)
    n = len(modes)

    def body(q_ref, f_ref, i_ref, g_ref, lbl_ref, on_ref, o_ref, y_ref, s_ref, st):
        @pl.when(pl.program_id(0) == 0)
        def _():
            st[...] = jnp.zeros_like(st)

        lb, _ = _lower_bound(lbl_ref[...])
        rows = lax.broadcasted_iota(jnp.int32, (CHUNK, CHUNK), 0)
        cols = lax.broadcasted_iota(jnp.int32, (CHUNK, CHUNK), 1)
        causal = rows >= cols
        tri = causal.astype(BF16)

        def chunk(c, carry):
            r0 = pl.multiple_of(c * CHUNK, CHUNK)
            rs = pl.ds(r0, CHUNK)
            for h0 in range(0, N_HEADS, HEAD_GROUP):
                cs = slice(HEAD * h0, HEAD * (h0 + HEAD_GROUP))
                q, k, _, _, _, e_q, e_qm, e_km, e_kd, e_last = _chunk_quantities(
                    q_ref[rs, cs], f_ref[rs, cs], lb[:, cs], tri)
                v = i_ref[rs, cs].astype(BF16)
                qm, km = (q * e_qm).astype(BF16), (k * e_km).astype(BF16)
                qt, kd = (q * e_q).astype(BF16), (k * e_kd).astype(BF16)
                outs = []
                for j, hs in enumerate(_HEAD_LANES):
                    s_old = st[h0 + j]
                    s_ref[c, h0 + j] = s_old.astype(BF16)
                    a = jnp.where(causal, _dot_nt(qm[:, hs], km[:, hs]), 0.0)
                    outs.append(_dot_nt(qt[:, hs], s_old) + _dot(a, v[:, hs]))
                    st[h0 + j] = s_old * e_last[:, hs] + _dot_tn(v[:, hs], kd[:, hs])
                o = jnp.concatenate(outs, axis=1)
                o_ref[rs, cs] = o
                n = o * lax.rsqrt(_head_mean(o * o) + EPS)
                zg = g_ref[rs, cs]
                y_ref[rs, cs] = (n * on_ref[:, cs] * zg * _sigmoid(zg)).astype(BF16)
            return carry

        lax.fori_loop(0, nc, chunk, 0, unroll=True)

    col = lambda j: pl.BlockSpec((tt, D), lambda i: (i, j))
    row = pl.BlockSpec((tt, D), lambda i: (i, 0))
    return pl.pallas_call(
        _hosted(body, 6, 3, (t // tt,), modes), name="hgrn_fwd", grid=(t // tt,),
        out_shape=[jax.ShapeDtypeStruct((t, D), F32), jax.ShapeDtypeStruct((t, D), BF16),
                   jax.ShapeDtypeStruct((t // CHUNK, N_HEADS, HEAD, HEAD), BF16)]
        + _recv_shapes(shards, modes),
        in_specs=[col(3), col(4), col(5), col(6), _full((2, D)), _full((1, D))] + [ANY] * n,
        out_specs=[row, row, pl.BlockSpec((nc, N_HEADS, HEAD, HEAD), lambda i: (i, 0, 0, 0))]
        + [ANY] * n,
        scratch_shapes=[pltpu.VMEM((N_HEADS, HEAD, HEAD), F32)] + _exchange_scratch(n),
        compiler_params=_params(40, dimension_semantics=("arbitrary",)),
    )(z, z, z, z, lb_logits, onorm_g, *shards)


def _rms_bwd(dn, xhat, rstd):
    return rstd * (dn - xhat * jnp.mean(dn * xhat, axis=-1, keepdims=True))


def _tail(x, y_conv, y_hgrn, p, target, w_out, w_pg, w_pp_all, pe_g, fin_g):
    t = x.shape[0]
    tt = min(256, t)
    n_steps = t // tt

    def body(x_ref, yc_ref, yh_ref, p_ref, tg_ref, wo_hbm, wg_hbm, wp_hbm, pg_ref, fg_ref,
             dh1_ref, dyc_ref, dyh_ref, dwo_hbm, dwg_hbm, dwp_hbm, dpg_ref, dfg_ref, loss_ref,
             wo, wg, wp, dwo, dwg, dwp):
        i = pl.program_id(0)

        @pl.when(i == 0)
        def _():
            pltpu.sync_copy(wo_hbm, wo)
            pltpu.sync_copy(wg_hbm, wg)
            for d in range(N_DEV):
                pltpu.sync_copy(wp_hbm.at[d], wp.at[:, pl.ds(HEAD * d, HEAD)])
            dwo[...] = jnp.zeros_like(dwo)
            dwg[...] = jnp.zeros_like(dwg)
            dwp[...] = jnp.zeros_like(dwp)
            dpg_ref[...] = jnp.zeros_like(dpg_ref)
            dfg_ref[...] = jnp.zeros_like(dfg_ref)
            loss_ref[...] = jnp.zeros_like(loss_ref)

        ycv, yhv = yc_ref[...], yh_ref[...]
        h1 = (x_ref[...] + jnp.dot(ycv, wo[0:D, :], preferred_element_type=F32)
              + jnp.dot(yhv, wo[D:2 * D, :], preferred_element_type=F32))
        pb = p_ref[...].astype(BF16)
        pe = jnp.dot(pb, wp[...], preferred_element_type=F32)
        rstd1 = lax.rsqrt(jnp.mean(h1 * h1, axis=-1, keepdims=True) + EPS)
        n1 = h1 * rstd1
        rb = (n1 * pg_ref[...]).astype(BF16)
        gate = _sigmoid(jnp.dot(rb, wg[...], preferred_element_type=F32))
        h2 = h1 + gate * pe
        rstd2 = lax.rsqrt(jnp.mean(h2 * h2, axis=-1, keepdims=True) + EPS)
        n2 = h2 * rstd2
        err = n2 * fg_ref[...] - tg_ref[...]
        loss_ref[...] += _rowsum8(err * err)

        d_out = err * (1.0 / D)
        dfg_ref[...] += _rowsum8(d_out * n2)
        d_h2 = _rms_bwd(d_out * fg_ref[...], n2, rstd2)
        d_pe = (d_h2 * gate).astype(BF16)
        d_gpre = (d_h2 * pe * gate * (1.0 - gate)).astype(BF16)
        dwg[...] += _dot_tn(rb, d_gpre)
        dwp[...] += _dot_tn(pb, d_pe)
        dr = _dot_nt(d_gpre, wg[...])
        dpg_ref[...] += _rowsum8(dr * n1)
        d_h1 = d_h2 + _rms_bwd(dr * pg_ref[...], n1, rstd1)
        dh1_ref[...] = d_h1
        d_h1b = d_h1.astype(BF16)
        dwo[0:D, :] += _dot_tn(ycv, d_h1b)
        dwo[D:2 * D, :] += _dot_tn(yhv, d_h1b)
        dyc_ref[...] = _dot_nt(d_h1b, wo[0:D, :])
        dyh_ref[...] = _dot_nt(d_h1b, wo[D:2 * D, :])

        @pl.when(i == n_steps - 1)
        def _():
            pltpu.sync_copy(dwo, dwo_hbm)
            pltpu.sync_copy(dwg, dwg_hbm)
            for d in range(N_DEV):
                pltpu.sync_copy(dwp.at[:, pl.ds(HEAD * d, HEAD)], dwp_hbm.at[d])

    row = pl.BlockSpec((tt, D), lambda i: (i, 0))
    acc = _full((SUBLANES, D))
    return pl.pallas_call(
        body, name="tail_fwd_bwd", grid=(n_steps,),
        out_shape=[jax.ShapeDtypeStruct((t, D), F32)] * 3
        + [jax.ShapeDtypeStruct((2 * D, D), F32), jax.ShapeDtypeStruct((D, D), F32),
           jax.ShapeDtypeStruct((N_DEV, PLE, HEAD), F32)]
        + [jax.ShapeDtypeStruct((SUBLANES, D), F32)] * 3,
        in_specs=[row, row, row, pl.BlockSpec((tt, PLE), lambda i: (i, 0)), row,
                  ANY, ANY, ANY, _full((1, D)), _full((1, D))],
        out_specs=[row, row, row, ANY, ANY, ANY, acc, acc, acc],
        scratch_shapes=[pltpu.VMEM((2 * D, D), BF16), pltpu.VMEM((D, D), BF16),
                        pltpu.VMEM((PLE, D), BF16), pltpu.VMEM((2 * D, D), F32),
                        pltpu.VMEM((D, D), F32), pltpu.VMEM((PLE, D), F32)],
        compiler_params=_params(52, dimension_semantics=("arbitrary",)),
    )(x, y_conv, y_hgrn, p, target, w_out, w_pg, w_pp_all, pe_g, fin_g)


def _hgrn_bwd(dy, z, o_raw, states, lb_logits, onorm_g, u, grads):
    t = z.shape[0]
    tt = min(256, t)
    nc = tt // CHUNK
    n_steps = t // tt
    modes = ["scatter"] * len(grads)
    half = N_DEV // 2
    first_col = half * COLS_PER_DEV

    def body(dy_ref, q_ref, f_ref, i_ref, g_ref, o_ref, s_ref, lbl_ref, on_ref, u_ref,
             dz_ref, don_ref, dlb_ref, dwin_hbm, dst, dwin, prev_dz, prev_u):
        @pl.when(pl.program_id(0) == 0)
        def _():
            dst[...] = jnp.zeros_like(dst)
            don_ref[...] = jnp.zeros_like(don_ref)
            dlb_ref[...] = jnp.zeros_like(dlb_ref)
            dwin[...] = jnp.zeros_like(dwin)
            prev_dz[...] = jnp.zeros_like(prev_dz)
            prev_u[...] = jnp.zeros_like(prev_u)

        def add_weight_grad():
            for pair in range(half // 2):
                both = lax.dot_general(
                    prev_u[...], prev_dz[:, PAIR_COLS * pair:PAIR_COLS * (pair + 1)],
                    (((0,), (0,)), ((), ())), preferred_element_type=F32)
                dwin[2 * pair] += both[:, :COLS_PER_DEV]
                dwin[2 * pair + 1] += both[:, COLS_PER_DEV:]

        add_weight_grad()

        lb, _ = _lower_bound(lbl_ref[...])
        rows = lax.broadcasted_iota(jnp.int32, (CHUNK, CHUNK), 0)
        cols = lax.broadcasted_iota(jnp.int32, (CHUNK, CHUNK), 1)
        causal = rows >= cols
        tri = causal.astype(BF16)
        tri_rev = (rows <= cols).astype(BF16)
        width = HEAD * HEAD_GROUP
        is_last = lax.broadcasted_iota(jnp.int32, (CHUNK, width), 0) == CHUNK - 1
        nn = (((1,), (0,)), ((), ()))
        tn = (((0,), (0,)), ((), ()))
        dg = functools.partial(lax.dot_general, preferred_element_type=F32)

        def chunk(cc, carry):
            c = nc - 1 - cc
            r0 = pl.multiple_of(c * CHUNK, CHUNK)
            rs = pl.ds(r0, CHUNK)
            for h0 in range(0, N_HEADS, HEAD_GROUP):
                cs = slice(HEAD * h0, HEAD * h0 + width)
                zq, zf, zg = q_ref[rs, cs], f_ref[rs, cs], g_ref[rs, cs]
                lbh = lb[:, cs]
                q, k, f, sig, sig_neg, e_q, e_qm, e_km, e_kd, e_last = _chunk_quantities(
                    zq, zf, lbh, tri)
                vb = i_ref[rs, cs].astype(BF16)
                qt, qm, km, kd = q * e_q, q * e_qm, k * e_km, k * e_kd
                qt_b, kd_b = qt.astype(BF16), kd.astype(BF16)
                qm_h, qm_l = _split(qm)
                km_h, km_l = _split(km)

                o = o_ref[rs, cs]
                rstd = lax.rsqrt(_head_mean(o * o) + EPS)
                n = o * rstd
                sg = _sigmoid(zg)
                dyv = dy_ref[rs, cs]
                on = on_ref[:, cs]
                d_zg = dyv * n * on * sg * (1.0 + zg * (1.0 - sg))
                d_on = dyv * zg * sg
                don_ref[:, cs] += _rowsum8(d_on * n)
                dn = d_on * on
                do_b = (rstd * (dn - n * _head_mean(dn * n))).astype(BF16)

                dv, dkd, dqt, dqm, dkm, s_dots = [], [], [], [], [], []
                for j, hs in enumerate(_HEAD_LANES):
                    s_old, ds_new = s_ref[c, h0 + j], dst[h0 + j]
                    ds_b = ds_new.astype(BF16)
                    a = jnp.where(causal, _dot_nt(qm_h[:, hs], km_h[:, hs]), 0.0)
                    da = jnp.where(causal, _dot_nt(do_b[:, hs], vb[:, hs]), 0.0)
                    dv.append(_dot_tn(a, do_b[:, hs]) + _dot_nt(kd_b[:, hs], ds_b))
                    dkd.append(_dot(vb[:, hs], ds_b))
                    dqt.append(_dot(do_b[:, hs], s_old))
                    da_h, da_l = _split(da)
                    dqm.append(dg(da_h, km_h[:, hs], nn)
                               + (dg(da_h, km_l[:, hs], nn) + dg(da_l, km_h[:, hs], nn)))
                    dkm.append(dg(da_h, qm_h[:, hs], tn)
                               + (dg(da_h, qm_l[:, hs], tn) + dg(da_l, qm_h[:, hs], tn)))
                    dst[h0 + j] = ds_new * e_last[:, hs] + _dot_tn(do_b[:, hs], qt_b[:, hs])
                    s_dots.append(jnp.sum(s_old.astype(F32) * ds_new, axis=0, keepdims=True))
                dv, dkd, dqt, dqm, dkm, s_dots = [
                    jnp.concatenate(parts, axis=1) for parts in (dv, dkd, dqt, dqm, dkm, s_dots)]
                dq = dqt * e_q + dqm * e_qm
                dk = dkm * e_km + dkd * e_kd
                last = jnp.sum(dkd * kd, axis=0, keepdims=True) + e_last * s_dots
                db = q * dq - k * dk + jnp.where(is_last, last, 0.0)
                dlogf = _tri_dot(tri_rev, db)
                common = sig_neg * (dlogf / f - dk)
                dlb_ref[:, cs] += _rowsum8(common)
                c0 = 3 * D + HEAD * h0
                sq = _sigmoid(zq)
                dz_ref[rs, c0:c0 + width] = (dq * sq * (1.0 + zq * (1.0 - sq))).astype(BF16)
                dz_ref[rs, D + c0:D + c0 + width] = ((1.0 - lbh) * sig * common).astype(BF16)
                dz_ref[rs, 2 * D + c0:2 * D + c0 + width] = dv.astype(BF16)
                dz_ref[rs, 3 * D + c0:3 * D + c0 + width] = d_zg.astype(BF16)
            return carry

        lax.fori_loop(0, nc, chunk, 0, unroll=True)
        prev_dz[...] = dz_ref[:, first_col:N_COLS]
        prev_u[...] = u_ref[...]

        @pl.when(pl.program_id(0) == n_steps - 1)
        def _():
            add_weight_grad()
            pltpu.sync_copy(dwin, dwin_hbm)

    rev = lambda i: n_steps - 1 - i
    col = lambda j: pl.BlockSpec((tt, D), lambda i: (rev(i), j))
    row = pl.BlockSpec((tt, D), lambda i: (rev(i), 0))
    acc = _full((SUBLANES, D))
    n = len(modes)
    return pl.pallas_call(
        _hosted(body, 10, 4, (n_steps,), modes), name="hgrn_bwd", grid=(n_steps,),
        out_shape=[jax.ShapeDtypeStruct((t, N_COLS), BF16),
                   jax.ShapeDtypeStruct((SUBLANES, D), F32),
                   jax.ShapeDtypeStruct((SUBLANES, D), F32),
                   jax.ShapeDtypeStruct((half, D, COLS_PER_DEV), F32)] + _recv_shapes(grads, modes),
        in_specs=[row, col(3), col(4), col(5), col(6), row,
                  pl.BlockSpec((nc, N_HEADS, HEAD, HEAD), lambda i: (rev(i), 0, 0, 0)),
                  _full((2, D)), _full((1, D)), row] + [ANY] * n,
        out_specs=[pl.BlockSpec((tt, N_COLS), lambda i: (rev(i), 0)), acc, acc, ANY] + [ANY] * n,
        scratch_shapes=[pltpu.VMEM((N_HEADS, HEAD, HEAD), F32),
                        pltpu.VMEM((half, D, COLS_PER_DEV), F32),
                        pltpu.VMEM((tt, half * COLS_PER_DEV), BF16),
                        pltpu.VMEM((tt, D), BF16)] + _exchange_scratch(n),
        compiler_params=_params(56, dimension_semantics=("arbitrary",)),
    )(dy, z, z, z, z, o_raw, states, lb_logits, onorm_g, u, *grads)


def _conv_bwd(dy, z, yc, y2, conv_w_all, cn_g, cn_b, w_pw2, dz, grads, modes):
    t = z.shape[0]
    tt = min(256, t)
    rc = 32
    n_steps = t // tt

    def body(dy_ref, val_ref, glu_ref, gate_ref, yc_ref, y2_ref, cw_ref, g_ref, b_ref, w_hbm,
             dz_in, dz_ref, dw_hbm, dcw_out, db2_ref, dg_ref, dbeta_ref, dcb_ref,
             w_vmem, dw, dbuf, dsh, y1buf, dnbuf, dcw_ref):
        i = pl.program_id(0)

        @pl.when(i == 0)
        def _():
            pltpu.sync_copy(w_hbm, w_vmem)
            dw[...] = jnp.zeros_like(dw)
            dbuf[tt:tt + CONV_PAD, :] = jnp.zeros((CONV_PAD, D), F32)
            dcw_ref[...] = jnp.zeros_like(dcw_ref)
            dcw_out[...] = jnp.zeros_like(dcw_out)
            db2_ref[...] = jnp.zeros_like(db2_ref)
            dg_ref[...] = jnp.zeros_like(dg_ref)
            dbeta_ref[...] = jnp.zeros_like(dbeta_ref)
            dcb_ref[...] = jnp.zeros_like(dcb_ref)

        gate = gate_ref[...]
        sg = _sigmoid(gate)
        dyv = dy_ref[...]
        dy2 = dyv * gate * sg
        dz_ref[:, 2 * D:3 * D] = (dyv * y2_ref[...] * sg * (1.0 + gate * (1.0 - sg))).astype(BF16)
        db2_ref[...] += _rowsum8(dy2)
        dy2b = dy2.astype(BF16)
        dnbuf[...] = _dot_nt(dy2b, w_vmem[...])

        def norm_chunk(r, carry):
            r0 = pl.multiple_of(r * rc, rc)
            rs = pl.ds(r0, rc)
            for g in range(N_HEADS):
                cs = slice(HEAD * g, HEAD * (g + 1))
                blk = yc_ref[rs, cs]
                mu = jnp.mean(blk, axis=-1, keepdims=True)
                cen = blk - mu
                rstd = lax.rsqrt(jnp.mean(cen * cen, axis=-1, keepdims=True) + EPS)
                xhat = cen * rstd
                n = xhat * g_ref[:, cs] + b_ref[:, cs]
                sn = _sigmoid(n)
                y1buf[rs, cs] = (n * sn).astype(BF16)
                dn = dnbuf[rs, cs] * sn * (1.0 + n * (1.0 - sn))
                dg_ref[:, cs] += _rowsum8(dn * xhat)
                dbeta_ref[:, cs] += _rowsum8(dn)
                dxh = dn * g_ref[:, cs]
                dyc = rstd * (dxh - jnp.mean(dxh, axis=-1, keepdims=True)
                              - xhat * jnp.mean(dxh * xhat, axis=-1, keepdims=True))
                dcb_ref[:, cs] += _rowsum8(dyc)
                dbuf[rs, cs] = dyc
            return carry

        lax.fori_loop(0, tt // rc, norm_chunk, 0, unroll=True)
        dw[...] += _dot_tn(y1buf[...], dy2b)
        _shifted_copies(dbuf, dsh, tt + 24)

        def conv_chunk(r, carry):
            r0 = pl.multiple_of(r * rc, rc)
            rs = pl.ds(r0, rc)
            for g in range(N_HEADS):
                cs = slice(HEAD * g, HEAD * (g + 1))
                sglu = _sigmoid(glu_ref[rs, cs])
                val = val_ref[rs, cs]
                v = val * sglu
                dv = jnp.zeros((rc, HEAD), F32)
                for ref, lo, taps in _tap_slabs(dbuf, dsh, lambda k: CONV_K - 1 - k):
                    slab = ref[pl.ds(r0 + lo, rc + taps[-1][1]), cs]
                    for k, off in taps:
                        d_later = slab[off:off + rc]
                        dv = dv + cw_ref[g, k:k + 1, :] * d_later
                        dcw_ref[g, k] += _rowsum8(v * d_later)
                dz_ref[rs, cs] = (dv * sglu).astype(BF16)
                dz_ref[rs, D + HEAD * g:D + HEAD * (g + 1)] = (
                    dv * val * sglu * (1.0 - sglu)).astype(BF16)
            return carry

        lax.fori_loop(0, tt // rc, conv_chunk, 0, unroll=2)
        dbuf[tt:tt + CONV_PAD, :] = dbuf[0:CONV_PAD, :]

        @pl.when(i == n_steps - 1)
        def _():
            pltpu.sync_copy(dw, dw_hbm)
            for g in range(N_HEADS):
                for k in range(CONV_K):
                    dcw_out[g, k:k + 1, :] = jnp.sum(dcw_ref[g, k], axis=0, keepdims=True)

    rev = lambda i: n_steps - 1 - i
    col = lambda j: pl.BlockSpec((tt, D), lambda i: (rev(i), j))
    row = pl.BlockSpec((tt, D), lambda i: (rev(i), 0))
    acc = _full((SUBLANES, D))
    n = len(modes)
    return pl.pallas_call(
        _hosted(body, 11, 7, (n_steps,), modes), name="conv_bwd", grid=(n_steps,),
        out_shape=[jax.ShapeDtypeStruct((t, N_COLS), BF16), jax.ShapeDtypeStruct((D, D), F32),
                   jax.ShapeDtypeStruct((N_DEV, CONV_PAD, HEAD), F32)]
        + [jax.ShapeDtypeStruct((SUBLANES, D), F32)] * 4 + _recv_shapes(grads, modes),
        in_specs=[row, col(0), col(1), col(2), row, row, _full((N_DEV, CONV_PAD, HEAD)),
                  _full((1, D)), _full((1, D)), ANY, ANY] + [ANY] * n,
        out_specs=[pl.BlockSpec((tt, 3 * D), lambda i: (rev(i), 0)), ANY,
                   _full((N_DEV, CONV_PAD, HEAD)), acc, acc, acc, acc] + [ANY] * n,
        input_output_aliases={10: 0},
        scratch_shapes=[pltpu.VMEM((D, D), BF16), pltpu.VMEM((D, D), F32),
                        pltpu.VMEM((tt + CONV_PAD, D), F32),
                        pltpu.VMEM((SUBLANES, tt + CONV_PAD, D), F32),
                        pltpu.VMEM((tt, D), BF16), pltpu.VMEM((tt, D), F32),
                        pltpu.VMEM((N_DEV, CONV_PAD, SUBLANES, HEAD), F32)] + _exchange_scratch(n),
        compiler_params=_params(52, dimension_semantics=("arbitrary",)),
    )(dy, z, z, z, yc, y2, conv_w_all, cn_g, cn_b, w_pw2, dz, *grads)


def _inproj_bwd_dx(dz, x, d_h1, ln_g, w_in_all, grads, modes):
    t = x.shape[0]
    tt = min(256, t)

    def body(dz_ref, x_ref, dh1_ref, g_ref, w_hbm, dx_ref, dg_ref, w_vmem):
        @pl.when(pl.program_id(0) == 0)
        def _():
            for d in range(N_DEV):
                pltpu.sync_copy(w_hbm.at[d], w_vmem.at[
                    d // 2, :, pl.ds(COLS_PER_DEV * (d % 2), COLS_PER_DEV)])
            dg_ref[...] = jnp.zeros_like(dg_ref)

        du = jnp.zeros((tt, D), F32)
        for q in range(N_CHIPS):
            du = du + lax.dot_general(
                dz_ref[:, PAIR_COLS * q:PAIR_COLS * (q + 1)], w_vmem[q],
                (((1,), (1,)), ((), ())), preferred_element_type=F32)
        xv = x_ref[...]
        rstd = lax.rsqrt(jnp.mean(xv * xv, axis=-1, keepdims=True) + EPS)
        xhat = xv * rstd
        dg_ref[...] += _rowsum8(du * xhat)
        dx_ref[...] = dh1_ref[...] + _rms_bwd(du * g_ref[...], xhat, rstd)

    row = pl.BlockSpec((tt, D), lambda i: (i, 0))
    n = len(modes)
    return pl.pallas_call(
        _hosted(body, 5, 2, (t // tt,), modes), name="inproj_bwd_dx", grid=(t // tt,),
        out_shape=[jax.ShapeDtypeStruct((t, D), F32), jax.ShapeDtypeStruct((SUBLANES, D), F32)]
        + _recv_shapes(grads, modes),
        in_specs=[pl.BlockSpec((tt, N_COLS), lambda i: (i, 0)), row, row, _full((1, D)), ANY]
        + [ANY] * n,
        out_specs=[row, _full((SUBLANES, D))] + [ANY] * n,
        scratch_shapes=[pltpu.VMEM((N_CHIPS, D, PAIR_COLS), BF16)] + _exchange_scratch(n),
        compiler_params=_params(48, dimension_semantics=("arbitrary",)),
    )(dz, x, d_h1, ln_g, w_in_all, *grads)


def _inproj_bwd_dw(name, u, dz, first, count, grads=(), modes=()):
    t = u.shape[0]
    tt = min(512, t)
    grid = (count // 2, t // tt)
    n = len(modes)

    def body(u_ref, dz_ref, dw_ref):
        @pl.when(pl.program_id(1) == 0)
        def _():
            dw_ref[...] = jnp.zeros_like(dw_ref)

        both = lax.dot_general(u_ref[...], dz_ref[...], (((0,), (0,)), ((), ())),
                               preferred_element_type=F32)
        dw_ref[0] += both[:, :COLS_PER_DEV]
        dw_ref[1] += both[:, COLS_PER_DEV:]

    return pl.pallas_call(
        _hosted(body, 2, 1, grid, modes) if n else body, name=name, grid=grid,
        out_shape=[jax.ShapeDtypeStruct((count, D, COLS_PER_DEV), F32)]
        + _recv_shapes(grads, modes),
        in_specs=[pl.BlockSpec((tt, D), lambda j, i: (i, 0)),
                  pl.BlockSpec((tt, PAIR_COLS), lambda j, i: (i, first // 2 + j))] + [ANY] * n,
        out_specs=[pl.BlockSpec((2, D, COLS_PER_DEV), lambda j, i: (j, 0, 0))] + [ANY] * n,
        scratch_shapes=_exchange_scratch(n) if n else [],
        compiler_params=_params(40, dimension_semantics=("arbitrary", "arbitrary")),
    )(u, dz, *grads)


def _adamw(w, g, m, v):
    m = ADAM_B1 * m + (1.0 - ADAM_B1) * g
    v = ADAM_B2 * v + (1.0 - ADAM_B2) * (g * g)
    m_hat = m / (1.0 - ADAM_B1 ** ADAM_STEP)
    v_hat = v / (1.0 - ADAM_B2 ** ADAM_STEP)
    delta = -ADAM_LR * (m_hat / (jnp.sqrt(v_hat) + ADAM_EPS) + ADAM_WD * w)
    return delta, m, v


def _pack_small(partials):
    rows = sorted(partials)

    def body(*refs):
        ins, out_ref = refs[:-1], refs[-1]
        out_ref[...] = jnp.zeros_like(out_ref)
        for j, row in enumerate(rows):
            out_ref[row:row + 1, :] = jnp.sum(ins[j][...], axis=0, keepdims=True)

    return pl.pallas_call(
        body, name="pack_small", out_shape=jax.ShapeDtypeStruct((N_SMALL, D), F32),
    )(*[partials[row] for row in rows])


def _sum_adam(name, recvs, w, m, v, rows):
    r, c = w.shape
    n = len(recvs)
    side = lax.axis_index("x").astype(jnp.int32).reshape(1)

    def body(side_ref, *refs):
        w_ref, m_ref, v_ref, g_ref, d_ref, mo_ref, vo_ref = refs[n:]

        def finish(recv_ref):
            g = recv_ref[0]
            for s in range(1, recv_ref.shape[0]):
                g = g + recv_ref[s]
            g_ref[...] = g
            d_ref[...], mo_ref[...], vo_ref[...] = _adamw(w_ref[...], g, m_ref[...], v_ref[...])

        if n == 1:
            finish(refs[0])
        else:
            for s in range(n):
                pl.when(side_ref[0] == s)(functools.partial(finish, refs[s]))

    def recv_spec(s, rv):
        if n == 1:
            return pl.BlockSpec((rv.shape[0], rows, c), lambda i, side_ref: (0, i, 0))
        return pl.BlockSpec((rv.shape[0], rows, c),
                            lambda i, side_ref: (0, jnp.where(side_ref[0] == s, i, 0), 0))

    blk = pl.BlockSpec((rows, c), lambda i, side_ref: (i, 0))
    grid_spec = pltpu.PrefetchScalarGridSpec(
        num_scalar_prefetch=1, grid=(r // rows,),
        in_specs=[recv_spec(s, rv) for s, rv in enumerate(recvs)] + [blk, blk, blk],
        out_specs=[blk] * 4)
    return pl.pallas_call(
        body, name=name, grid_spec=grid_spec,
        out_shape=[jax.ShapeDtypeStruct((r, c), F32)] * 4,
        compiler_params=_params(48, dimension_semantics=("arbitrary",)),
    )(side, *recvs, w, m, v)


def _small_adam(gathered, lb_logits, w, m, v):
    def body(ga_ref, lbl_ref, w_ref, m_ref, v_ref, g_ref, d_ref, mo_ref, vo_ref, loss_ref):
        g = ga_ref[0]
        for s in range(1, N_DEV):
            g = g + ga_ref[s]
        s0, s1 = _lower_bound(lbl_ref[...])
        d_lb = g[R_LB0:R_LB0 + 1, :]
        rows = lax.broadcasted_iota(jnp.int32, (N_SMALL, D), 0)
        g = jnp.where(rows == R_LB0, d_lb * s0 * (1.0 - s0), g)
        g = jnp.where(rows == R_LB1, -d_lb * s0 * s1, g)
        g_ref[...] = g
        d_ref[...], mo_ref[...], vo_ref[...] = _adamw(w_ref[...], g, m_ref[...], v_ref[...])
        loss_ref[...] = (0.5 / D) * jnp.sum(g[R_LOSS:R_LOSS + 1, :], axis=-1, keepdims=True)

    return pl.pallas_call(
        body, name="small_adam",
        out_shape=[jax.ShapeDtypeStruct((N_SMALL, D), F32)] * 4 + [jax.ShapeDtypeStruct((1, 1), F32)],
    )(gathered, lb_logits, w, m, v)


def _pad_rows(a, rows):
    return jnp.pad(a, ((0, rows - a.shape[0]), (0, 0)))


def _pack_rows(rows):
    rows = [r.reshape(-1, D) for r in rows]
    packed = jnp.concatenate(rows, axis=0)
    return _pad_rows(packed, N_SMALL)


def kernel(x, p, ln_g, w_in, conv_w, conv_b, cnorm_g, cnorm_b, w_pw2, b_pw2, lb_logits, onorm_g, w_out, pe_norm_g, w_pg, w_pp, final_g, loss_target, m_ln_g, m_w_in, m_conv_w, m_conv_b, m_cnorm_g, m_cnorm_b, m_w_pw2, m_b_pw2, m_lb_logits, m_onorm_g, m_w_out, m_pe_norm_g, m_w_pg, m_w_pp, m_final_g, v_ln_g, v_w_in, v_conv_w, v_conv_b, v_cnorm_g, v_cnorm_b, v_w_pw2, v_b_pw2, v_lb_logits, v_onorm_g, v_w_out, v_pe_norm_g, v_w_pg, v_w_pp, v_final_g):
    t = x.shape[1]
    x2 = x.reshape(t, D)
    p2 = p.reshape(t, PLE)
    tg2 = loss_target.reshape(t, D)
    fin_g = final_g.reshape(1, D)

    z, u, w_in_all = _inproj_fwd(x2, ln_g, w_in[0].astype(BF16))
    (o_raw, y_hgrn, states, conv_w_all, w_pw2_all, w_out_all, w_pg_all, w_pp_all) = _hgrn_fwd(
        z, lb_logits, onorm_g,
        [_pad_rows(conv_w[0], CONV_PAD), w_pw2[0].astype(BF16), w_out[0].astype(BF16),
         w_pg[0].astype(BF16), w_pp[0].astype(BF16)])
    w_pw2_full = w_pw2_all.reshape(D, D)
    w_out_full = w_out_all.reshape(2 * D, D)
    w_pg_full = w_pg_all.reshape(D, D)
    yc, y2, y_conv = _conv_fwd(z, conv_w_all, conv_b, cnorm_g, cnorm_b, w_pw2_full, b_pw2)

    (d_h1, dy_conv, dy_hgrn, d_w_out, d_w_pg, d_w_pp, d_pen_p, d_fin_p, loss_p) = _tail(
        x2, y_conv, y_hgrn, p2, tg2, w_out_full, w_pg_full, w_pp_all, pe_norm_g, fin_g)

    dz, d_on_p, d_lb_p, d_w_in_hi, r_w_out, r_w_pg, r_w_pp = _hgrn_bwd(
        dy_hgrn, z, o_raw, states, lb_logits, onorm_g, u,
        [d_w_out.reshape(N_DEV, 2 * D // N_DEV, D), d_w_pg.reshape(N_DEV, D // N_DEV, D), d_w_pp])
    dz, d_w_pw2, d_conv_w, d_b2_p, d_cng_p, d_cnb_p, d_cb_p, r_w_in_hi = _conv_bwd(
        dy_conv, z, yc, y2, conv_w_all, cnorm_g, cnorm_b, w_pw2_full, dz, [d_w_in_hi], [1])
    d_w_in_lo, r_w_pw2, r_conv_w = _inproj_bwd_dw(
        "inproj_bwd_dw_lo", u, dz, 0, N_DEV // 2,
        [d_w_pw2.reshape(N_DEV, D // N_DEV, D), d_conv_w], ["scatter", "scatter"])
    chip_lo = _pair_reduce("pair_reduce_lo", d_w_in_lo)
    grad_x, d_ln_p, r_w_in_lo = _inproj_bwd_dx(
        dz, x2, d_h1, ln_g, w_in_all, [chip_lo], [("chip", 0)])

    small = _pack_small({R_LN: d_ln_p, R_CONVB: d_cb_p, R_CNG: d_cng_p, R_CNB: d_cnb_p,
                         R_BPW2: d_b2_p, R_LB0: d_lb_p, R_ON: d_on_p, R_PEN: d_pen_p,
                         R_FIN: d_fin_p, R_LOSS: loss_p})
    (small_all,) = _exchange_call("gather_small", [small], ["gather"])

    big = {}
    big["w_in"] = _sum_adam("adam_w_in", [r_w_in_lo, r_w_in_hi], w_in[0], m_w_in[0], v_w_in[0], 128)
    cw = _sum_adam("adam_conv_w", [r_conv_w], _pad_rows(conv_w[0], CONV_PAD),
                   _pad_rows(m_conv_w[0], CONV_PAD), _pad_rows(v_conv_w[0], CONV_PAD), CONV_PAD)
    big["conv_w"] = [a[:CONV_K] for a in cw]
    big["w_pw2"] = _sum_adam("adam_w_pw2", [r_w_pw2], w_pw2[0], m_w_pw2[0], v_w_pw2[0], 128)
    big["w_out"] = _sum_adam("adam_w_out", [r_w_out], w_out[0], m_w_out[0], v_w_out[0], 128)
    big["w_pg"] = _sum_adam("adam_w_pg", [r_w_pg], w_pg[0], m_w_pg[0], v_w_pg[0], 128)
    big["w_pp"] = _sum_adam("adam_w_pp", [r_w_pp], w_pp[0], m_w_pp[0], v_w_pp[0], PLE)

    small_w = [ln_g, conv_b, cnorm_g, cnorm_b, b_pw2, lb_logits, onorm_g, pe_norm_g, final_g]
    small_m = [m_ln_g, m_conv_b, m_cnorm_g, m_cnorm_b, m_b_pw2, m_lb_logits, m_onorm_g,
               m_pe_norm_g, m_final_g]
    small_v = [v_ln_g, v_conv_b, v_cnorm_g, v_cnorm_b, v_b_pw2, v_lb_logits, v_onorm_g,
               v_pe_norm_g, v_final_g]
    sg, sd, sm, sv, loss = _small_adam(small_all, lb_logits, _pack_rows(small_w),
                                       _pack_rows(small_m), _pack_rows(small_v))

    small_rows = {"ln_g": (R_LN, 1), "conv_b": (R_CONVB, 1), "cnorm_g": (R_CNG, 1),
                  "cnorm_b": (R_CNB, 1), "b_pw2": (R_BPW2, 1), "lb_logits": (R_LB0, 2),
                  "onorm_g": (R_ON, 1), "pe_norm_g": (R_PEN, 1), "final_g": (R_FIN, 1)}
    order = ["ln_g", "w_in", "conv_w", "conv_b", "cnorm_g", "cnorm_b", "w_pw2", "b_pw2",
             "lb_logits", "onorm_g", "w_out", "pe_norm_g", "w_pg", "w_pp", "final_g"]

    def leaf(kind, name):
        if name in big:
            return big[name][kind][None]
        r0, n = small_rows[name]
        a = (sg, sd, sm, sv)[kind][r0:r0 + n]
        return a.reshape(D) if name == "final_g" else a

    outs = [loss.reshape(()), grad_x.reshape(1, t, D)]
    for kind in range(4):
        outs += [leaf(kind, name) for name in order]
    return tuple(outs)
```

```python
import functools

import jax
import jax.numpy as jnp
from jax import lax
from jax.experimental import pallas as pl
from jax.experimental.pallas import tpu as pltpu

F32 = jnp.float32
BF16 = jnp.bfloat16
MESH = pl.DeviceIdType.MESH

N_DEV = 8
D = 1024
N_COLS = 7 * D
COLS_PER_DEV = N_COLS // N_DEV
PLE = 256
HEAD = 128
N_HEADS = D // HEAD
CONV_K = 31
CONV_PAD = 32
CHUNK = 64
EPS = 1e-6
SUBLANES = 8

ADAM_LR = 0.001
ADAM_B1 = 0.9
ADAM_B2 = 0.999
ADAM_EPS = 1e-08
ADAM_WD = 0.01
ADAM_STEP = 10

MIB = 1024 * 1024
N_SMALL = 16
R_LN, R_CONVB, R_CNG, R_CNB, R_BPW2, R_LB0, R_LB1, R_ON, R_PEN, R_FIN, R_LOSS = range(11)


def _params(vmem_mib, **kw):
    return pltpu.CompilerParams(vmem_limit_bytes=vmem_mib * MIB, **kw)


def _dot(a, b):
    return jnp.dot(a.astype(BF16), b.astype(BF16), preferred_element_type=F32)


def _dot_nt(a, b):
    return lax.dot_general(a.astype(BF16), b.astype(BF16), (((1,), (1,)), ((), ())),
                           preferred_element_type=F32)


def _dot_tn(a, b):
    return lax.dot_general(a.astype(BF16), b.astype(BF16), (((0,), (0,)), ((), ())),
                           preferred_element_type=F32)


def _split(a):
    hi = a.astype(BF16)
    return hi, (a - hi.astype(F32)).astype(BF16)


def _sigmoid(x):
    return 1.0 / (1.0 + jnp.exp(-x))


def _rowsum8(a):
    r, c = a.shape
    return jnp.sum(a.reshape(r // SUBLANES, SUBLANES, c), axis=0)


def _tri_dot(tri, a):
    hi = a.astype(BF16)
    r1 = a - hi.astype(F32)
    mid = r1.astype(BF16)
    lo = (r1 - mid.astype(F32)).astype(BF16)
    return (jnp.dot(tri, hi, preferred_element_type=F32)
            + jnp.dot(tri, mid, preferred_element_type=F32)
            + jnp.dot(tri, lo, preferred_element_type=F32))


def _lower_bound(lbl):
    l0, l1 = lbl[0:1, :], lbl[1:2, :]
    m = jnp.maximum(l0, l1)
    e0, e1 = jnp.exp(l0 - m), jnp.exp(l1 - m)
    s = e0 + e1
    return e0 / s, e1 / s


ANY = pl.BlockSpec(memory_space=pl.ANY)


def _full(shape):
    return pl.BlockSpec(shape, lambda i: (0,) * len(shape))


def _peer(x, y, c, k):
    px = 1 - x if k & 4 else x
    py = 1 - y if k & 2 else y
    pc = 1 - c if k & 1 else c
    return (px, py, pc), 4 * px + 2 * py + pc


class _Exchange:
    def __init__(self, srcs, outs, modes, send_sems, recv_sems, local_sems):
        x, y, c = lax.axis_index("x"), lax.axis_index("y"), lax.axis_index("c")
        me = 4 * x + 2 * y + c
        self.starts, self.send_waits, self.recv_waits = [], [], []

        def remote(a, k, src, slot, peer, when):
            sem = a * N_DEV + k
            cp = pltpu.make_async_remote_copy(
                src_ref=src, dst_ref=outs[a].at[slot], send_sem=send_sems.at[sem],
                recv_sem=recv_sems.at[sem], device_id=peer, device_id_type=MESH)
            self.starts.append((when, cp.start))
            self.send_waits.append((when, cp.wait_send))

        def arrival(a, k, slot, when):
            sem = a * N_DEV + k
            cp = pltpu.make_async_remote_copy(
                src_ref=outs[a].at[slot], dst_ref=outs[a].at[slot], send_sem=send_sems.at[sem],
                recv_sem=recv_sems.at[sem], device_id=(x, y, c), device_id_type=MESH)
            self.recv_waits.append((when, cp.wait_recv))

        def local(a, src, slot, when):
            cp = pltpu.make_async_copy(src, outs[a].at[slot], local_sems.at[a])
            self.starts.append((when, cp.start))
            self.send_waits.append((when, cp.wait))

        for a, (src, mode) in enumerate(zip(srcs, modes)):
            if mode in ("gather", "scatter"):
                local(a, src if mode == "gather" else src.at[me], me, None)
                for k in range(1, N_DEV):
                    peer, peer_idx = _peer(x, y, c, k)
                    remote(a, k, src if mode == "gather" else src.at[peer_idx], me, peer, None)
                    arrival(a, k, peer_idx, None)
                continue
            if isinstance(mode, tuple):
                here, away = x == mode[1], x != mode[1]
                chip = 2 * x + y
                local(a, src.at[y], chip, here)
                remote(a, 1, src.at[1 - y], chip, (x, 1 - y, c), here)
                remote(a, 2, src.at[y], chip, (1 - x, y, c), away)
                remote(a, 3, src.at[1 - y], chip, (1 - x, 1 - y, c), away)
                arrival(a, 1, 2 * x + 1 - y, here)
                arrival(a, 2, 2 * (1 - x) + y, here)
                arrival(a, 3, 2 * (1 - x) + 1 - y, here)
                continue
            here, away = x == mode, x != mode
            for kk in range(4):
                py = 1 - y if kk & 2 else y
                pc = 1 - c if kk & 1 else c
                block = src.at[2 * py + pc]
                if kk == 0:
                    local(a, block, me, here)
                else:
                    remote(a, kk, block, me, (x, py, pc), here)
                remote(a, 4 + kk, block, me, (1 - x, py, pc), away)
            for k in range(1, N_DEV):
                arrival(a, k, _peer(x, y, c, k)[1], here)

    @staticmethod
    def _run(actions):
        for when, fn in actions:
            if when is None:
                fn()
            else:
                pl.when(when)(fn)

    def start(self):
        self._run(self.starts)

    def wait(self):
        self._run(self.recv_waits)
        self._run(self.send_waits)


def _exchange_scratch(n):
    return [pltpu.SemaphoreType.DMA((n * N_DEV,)), pltpu.SemaphoreType.DMA((n * N_DEV,)),
            pltpu.SemaphoreType.DMA((n,))]


def _recv_shapes(srcs, modes):
    def shape(s, m):
        if m == "gather":
            return (N_DEV,) + s.shape
        return (N_DEV // 2 if isinstance(m, tuple) else N_DEV,) + s.shape[1:]

    return [jax.ShapeDtypeStruct(shape(s, m), s.dtype) for s, m in zip(srcs, modes)]


def _pair_reduce(name, blocks):
    shape = (2,) + blocks.shape[1:]

    def body(src, out_ref, stage, mine, send_sems, recv_sems, local_sems):
        x, y, c = lax.axis_index("x"), lax.axis_index("y"), lax.axis_index("c")
        sends, waits = [], []
        for py in range(2):
            sends.append(pltpu.make_async_remote_copy(
                src_ref=src.at[2 * py + 1 - c], dst_ref=stage.at[py], send_sem=send_sems.at[py],
                recv_sem=recv_sems.at[py], device_id=(x, y, 1 - c), device_id_type=MESH))
            waits.append(pltpu.make_async_copy(src.at[2 * py + c], mine.at[py], local_sems.at[py]))
        for cp in sends + waits:
            cp.start()
        for cp in waits:
            cp.wait()
        for cp in sends:
            cp.wait_recv()
        out_ref[...] = mine[...] + stage[...]
        for cp in sends:
            cp.wait_send()

    return pl.pallas_call(
        body, name=name, out_shape=jax.ShapeDtypeStruct(shape, F32), in_specs=[ANY],
        scratch_shapes=[pltpu.VMEM(shape, F32), pltpu.VMEM(shape, F32),
                        pltpu.SemaphoreType.DMA((2,)), pltpu.SemaphoreType.DMA((2,)),
                        pltpu.SemaphoreType.DMA((2,))],
        compiler_params=_params(40),
    )(blocks)


def _exchange_call(name, srcs, modes):
    n = len(srcs)

    def body(*refs):
        xch = _Exchange(refs[:n], refs[n:2 * n], modes, *refs[2 * n:])
        xch.start()
        xch.wait()

    return pl.pallas_call(
        body, name=name, out_shape=_recv_shapes(srcs, modes),
        in_specs=[ANY] * n, out_specs=[ANY] * n, scratch_shapes=_exchange_scratch(n),
    )(*srcs)


def _hosted(body, n_in, n_out, grid, modes):
    n = len(modes)

    def hosted(*refs):
        ins, srcs = refs[:n_in], refs[n_in:n_in + n]
        outs = refs[n_in + n:n_in + n + n_out]
        bufs = refs[n_in + n + n_out:n_in + 2 * n + n_out]
        scratch = refs[n_in + 2 * n + n_out:-3]
        xch = _Exchange(srcs, bufs, modes, *refs[-3:])
        first, last = True, True
        for axis, size in enumerate(grid):
            first = jnp.logical_and(first, pl.program_id(axis) == 0)
            last = jnp.logical_and(last, pl.program_id(axis) == size - 1)
        pl.when(first)(xch.start)
        body(*ins, *outs, *scratch)
        pl.when(last)(xch.wait)

    return hosted


N_CHIPS = N_DEV // 2
PAIR_COLS = 2 * COLS_PER_DEV
PUSHED = (1, 2, 4, 6)
FORWARDED = (2, 4, 6)
NORM_ROWS = 32


def _inproj_fwd(x, ln_g, w_shard):
    t = x.shape[0]
    tt = min(512, t)
    n_t = t // tt
    chip = 2 * lax.axis_index("x") + lax.axis_index("y")
    order = jnp.bitwise_xor(chip, jnp.arange(N_CHIPS, dtype=jnp.int32)).astype(jnp.int32)

    def body(order_ref, x_ref, g_ref, shard_hbm, z_ref, u_ref, w_all,
             u_all, w_blk, w_send, w_recv, w_local):
        p, i = pl.program_id(0), pl.program_id(1)
        x, y, c = lax.axis_index("x"), lax.axis_index("y"), lax.axis_index("c")
        mine = 4 * x + 2 * y + c

        def push(k):
            peer, _ = _peer(x, y, c, k)
            return pltpu.make_async_remote_copy(
                src_ref=shard_hbm, dst_ref=w_all.at[mine], send_sem=w_send.at[k],
                recv_sem=w_recv.at[k], device_id=peer, device_id_type=MESH)

        def forward(k):
            _, owner = _peer(x, y, c, k)
            return pltpu.make_async_remote_copy(
                src_ref=w_all.at[owner], dst_ref=w_all.at[owner], send_sem=w_send.at[k + 1],
                recv_sem=w_recv.at[k + 1], device_id=(x, y, 1 - c), device_id_type=MESH)

        def landed(k):
            _, owner = _peer(x, y, c, k)
            return pltpu.make_async_remote_copy(
                src_ref=w_all.at[owner], dst_ref=w_all.at[owner], send_sem=w_send.at[k],
                recv_sem=w_recv.at[k], device_id=(x, y, c), device_id_type=MESH)

        keep = pltpu.make_async_copy(shard_hbm, w_all.at[mine], w_local.at[0])

        def load_pair(step):
            same = shard_hbm if step == 0 else w_all.at[_peer(x, y, c, 2 * step)[1]]
            other = w_all.at[_peer(x, y, c, 2 * step + 1)[1]]
            for side in range(2):
                @pl.when(c == side)
                def _(side=side):
                    pltpu.sync_copy(same, w_blk.at[:, pl.ds(COLS_PER_DEV * side, COLS_PER_DEV)])
                    pltpu.sync_copy(
                        other, w_blk.at[:, pl.ds(COLS_PER_DEV * (1 - side), COLS_PER_DEV)])

        @pl.when(jnp.logical_and(p == 0, i == 0))
        def _():
            for k in PUSHED[:-1]:
                push(k).start()
            keep.start()

        @pl.when(jnp.logical_and(p == 1, i == 0))
        def _():
            for k in PUSHED[1:-1]:
                push(k).wait_send()
            push(PUSHED[-1]).start()

        for step in range(N_CHIPS):
            @pl.when(jnp.logical_and(p == step, i == 0))
            def _(step=step):
                landed(2 * step + 1).wait_recv()
                load_pair(step)

        rows = pl.ds(pl.multiple_of(i * tt, tt), tt)

        @pl.when(p == 0)
        def _():
            def norm_rows(r, carry):
                sub = pl.ds(pl.multiple_of(r * NORM_ROWS, NORM_ROWS), NORM_ROWS)
                xv = x_ref[sub, :]
                rstd = lax.rsqrt(jnp.mean(xv * xv, axis=-1, keepdims=True) + EPS)
                ub = (xv * rstd * g_ref[...]).astype(BF16)
                u_ref[sub, :] = ub
                u_all[pl.ds(pl.multiple_of(i * tt + r * NORM_ROWS, NORM_ROWS), NORM_ROWS), :] = ub
                return carry

            lax.fori_loop(0, tt // NORM_ROWS, norm_rows, 0, unroll=2)

        z_ref[...] = jnp.dot(u_all[rows, :], w_blk[...], preferred_element_type=F32)

        for step in range(1, N_CHIPS):
            @pl.when(jnp.logical_and(p == step - 1, i == n_t - 1))
            def _(step=step):
                landed(2 * step).wait_recv()
                forward(2 * step).start()

        @pl.when(jnp.logical_and(p == N_CHIPS - 1, i == n_t - 1))
        def _():
            push(PUSHED[0]).wait_send()
            push(PUSHED[-1]).wait_send()
            for k in FORWARDED:
                forward(k).wait_send()
            keep.wait()

    first_pass = lambda p, i, order_ref: (jnp.where(p == 0, i, n_t - 1), 0)
    grid_spec = pltpu.PrefetchScalarGridSpec(
        num_scalar_prefetch=1, grid=(N_CHIPS, n_t),
        in_specs=[pl.BlockSpec((tt, D), first_pass),
                  pl.BlockSpec((1, D), lambda p, i, order_ref: (0, 0)), ANY],
        out_specs=[pl.BlockSpec((tt, PAIR_COLS), lambda p, i, order_ref: (i, order_ref[p])),
                   pl.BlockSpec((tt, D), first_pass), ANY],
        scratch_shapes=[pltpu.VMEM((t, D), BF16), pltpu.VMEM((D, PAIR_COLS), BF16),
                        pltpu.SemaphoreType.DMA((N_DEV,)), pltpu.SemaphoreType.DMA((N_DEV,)),
                        pltpu.SemaphoreType.DMA((1,))])
    return pl.pallas_call(
        body, name="inproj_fwd", grid_spec=grid_spec,
        out_shape=[jax.ShapeDtypeStruct((t, N_COLS), F32), jax.ShapeDtypeStruct((t, D), BF16),
                   jax.ShapeDtypeStruct((N_DEV,) + w_shard.shape, BF16)],
        compiler_params=_params(48, dimension_semantics=("arbitrary", "arbitrary")),
    )(order, x, ln_g, w_shard)


def _shifted_copies(buf, shifted, rows):
    for b in range(1, SUBLANES):
        shifted[b, 0:rows, :] = buf[b:b + rows, :]


def _tap_slabs(buf, shifted, offset_of_tap):
    groups = {}
    for k in range(CONV_K):
        a, b = divmod(offset_of_tap(k), SUBLANES)
        groups.setdefault(b, []).append((SUBLANES * a, k))
    out = []
    for b, taps in sorted(groups.items()):
        taps.sort()
        lo = taps[0][0]
        out.append((buf if b == 0 else shifted.at[b], lo, [(k, off - lo) for off, k in taps]))
    return out


def _group_norm_stats(blk):
    mu = jnp.mean(blk, axis=-1, keepdims=True)
    cen = blk - mu
    var = jnp.mean(cen * cen, axis=-1, keepdims=True)
    return cen * lax.rsqrt(var + EPS)


def _conv_fwd(z, conv_w_all, conv_b, cn_g, cn_b, w_pw2, b_pw2):
    t = z.shape[0]
    tt = min(256, t)
    rc = 128

    def body(val_ref, glu_ref, gate_ref, cw_ref, cb_ref, g_ref, b_ref, w_hbm, b2_ref,
             yc_ref, y2_ref, yo_ref, w_vmem, vbuf, vsh, y1buf):
        @pl.when(pl.program_id(0) == 0)
        def _():
            pltpu.sync_copy(w_hbm, w_vmem)
            vbuf[0:CONV_PAD, :] = jnp.zeros((CONV_PAD, D), F32)

        vbuf[CONV_PAD:CONV_PAD + tt, :] = val_ref[...] * _sigmoid(glu_ref[...])
        _shifted_copies(vbuf, vsh, tt + 24)

        for g in range(N_HEADS):
            cs = slice(HEAD * g, HEAD * (g + 1))

            def row_chunk(r, carry, g=g, cs=cs):
                r0 = pl.multiple_of(r * rc, rc)
                acc = jnp.broadcast_to(cb_ref[:, cs], (rc, HEAD))
                for ref, lo, taps in _tap_slabs(vbuf, vsh, lambda k: k + 2):
                    slab = ref[pl.ds(r0 + lo, rc + taps[-1][1]), cs]
                    for k, off in taps:
                        acc = acc + cw_ref[g, k:k + 1, :] * slab[off:off + rc]
                yc_ref[pl.ds(r0, rc), cs] = acc
                n = _group_norm_stats(acc) * g_ref[:, cs] + b_ref[:, cs]
                y1buf[pl.ds(r0, rc), cs] = (n * _sigmoid(n)).astype(BF16)
                return carry

            lax.fori_loop(0, tt // rc, row_chunk, 0, unroll=True)
        vbuf[0:CONV_PAD, :] = vbuf[tt:tt + CONV_PAD, :]
        y2 = jnp.dot(y1buf[...], w_vmem[...], preferred_element_type=F32) + b2_ref[...]
        y2_ref[...] = y2
        gate = gate_ref[...]
        yo_ref[...] = (y2 * gate * _sigmoid(gate)).astype(BF16)

    col = lambda j: pl.BlockSpec((tt, D), lambda i: (i, j))
    row = pl.BlockSpec((tt, D), lambda i: (i, 0))
    return pl.pallas_call(
        body, name="conv_fwd", grid=(t // tt,),
        out_shape=[jax.ShapeDtypeStruct((t, D), F32), jax.ShapeDtypeStruct((t, D), F32),
                   jax.ShapeDtypeStruct((t, D), BF16)],
        in_specs=[col(0), col(1), col(2), _full((N_DEV, CONV_PAD, HEAD)), _full((1, D)),
                  _full((1, D)), _full((1, D)), ANY, _full((1, D))],
        out_specs=[row, row, row],
        scratch_shapes=[pltpu.VMEM((D, D), BF16), pltpu.VMEM((tt + CONV_PAD, D), F32),
                        pltpu.VMEM((SUBLANES, tt + CONV_PAD, D), F32), pltpu.VMEM((tt, D), BF16)],
        compiler_params=_params(48, dimension_semantics=("arbitrary",)),
    )(z, z, z, conv_w_all, conv_b, cn_g, cn_b, w_pw2, b_pw2)


HEAD_GROUP = 8
_HEAD_LANES = [slice(HEAD * j, HEAD * (j + 1)) for j in range(HEAD_GROUP)]


def _head_mean(a):
    return jnp.concatenate(
        [jnp.broadcast_to(jnp.mean(a[:, hs], axis=-1, keepdims=True), (a.shape[0], HEAD))
         for hs in _HEAD_LANES], axis=1)


def _chunk_quantities(zq, zf, lbh, tri):
    sig = _sigmoid(zf)
    sig_neg = _sigmoid(-zf)
    f = lbh + (1.0 - lbh) * sig
    k = (1.0 - lbh) * sig_neg
    q = zq * _sigmoid(zq)
    b = _tri_dot(tri, jnp.log(f))
    b_mid = b[CHUNK // 2 - 1:CHUNK // 2, :]
    b_last = b[CHUNK - 1:CHUNK, :]
    e_q = jnp.exp(b)
    e_qm = jnp.exp(b - b_mid)
    e_km = jnp.exp(b_mid - b)
    e_kd = jnp.exp(b_last - b)
    return q, k, f, sig, sig_neg, e_q, e_qm, e_km, e_kd, jnp.exp(b_last)


def _hgrn_fwd(z, lb_logits, onorm_g, shards):
    t = z.shape[0]
    tt = min(256, t)
    nc = tt // CHUNK
    modes = ["gather"] * len(shards)
    n = len(modes)

    def body(q_ref, f_ref, i_ref, g_ref, lbl_ref, on_ref, o_ref, y_ref, s_ref, st):
        @pl.when(pl.program_id(0) == 0)
        def _():
            st[...] = jnp.zeros_like(st)

        lb, _ = _lower_bound(lbl_ref[...])
        rows = lax.broadcasted_iota(jnp.int32, (CHUNK, CHUNK), 0)
        cols = lax.broadcasted_iota(jnp.int32, (CHUNK, CHUNK), 1)
        causal = rows >= cols
        tri = causal.astype(BF16)

        def chunk(c, carry):
            r0 = pl.multiple_of(c * CHUNK, CHUNK)
            rs = pl.ds(r0, CHUNK)
            for h0 in range(0, N_HEADS, HEAD_GROUP):
                cs = slice(HEAD * h0, HEAD * (h0 + HEAD_GROUP))
                q, k, _, _, _, e_q, e_qm, e_km, e_kd, e_last = _chunk_quantities(
                    q_ref[rs, cs], f_ref[rs, cs], lb[:, cs], tri)
                v = i_ref[rs, cs].astype(BF16)
                qm, km = (q * e_qm).astype(BF16), (k * e_km).astype(BF16)
                qt, kd = (q * e_q).astype(BF16), (k * e_kd).astype(BF16)
                outs = []
                for j, hs in enumerate(_HEAD_LANES):
                    s_old = st[h0 + j]
                    s_ref[c, h0 + j] = s_old.astype(BF16)
                    a = jnp.where(causal, _dot_nt(qm[:, hs], km[:, hs]), 0.0)
                    outs.append(_dot_nt(qt[:, hs], s_old) + _dot(a, v[:, hs]))
                    st[h0 + j] = s_old * e_last[:, hs] + _dot_tn(v[:, hs], kd[:, hs])
                o = jnp.concatenate(outs, axis=1)
                o_ref[rs, cs] = o
                n = o * lax.rsqrt(_head_mean(o * o) + EPS)
                zg = g_ref[rs, cs]
                y_ref[rs, cs] = (n * on_ref[:, cs] * zg * _sigmoid(zg)).astype(BF16)
            return carry

        lax.fori_loop(0, nc, chunk, 0, unroll=True)

    col = lambda j: pl.BlockSpec((tt, D), lambda i: (i, j))
    row = pl.BlockSpec((tt, D), lambda i: (i, 0))
    return pl.pallas_call(
        _hosted(body, 6, 3, (t // tt,), modes), name="hgrn_fwd", grid=(t // tt,),
        out_shape=[jax.ShapeDtypeStruct((t, D), F32), jax.ShapeDtypeStruct((t, D), BF16),
                   jax.ShapeDtypeStruct((t // CHUNK, N_HEADS, HEAD, HEAD), BF16)]
        + _recv_shapes(shards, modes),
        in_specs=[col(3), col(4), col(5), col(6), _full((2, D)), _full((1, D))] + [ANY] * n,
        out_specs=[row, row, pl.BlockSpec((nc, N_HEADS, HEAD, HEAD), lambda i: (i, 0, 0, 0))]
        + [ANY] * n,
        scratch_shapes=[pltpu.VMEM((N_HEADS, HEAD, HEAD), F32)] + _exchange_scratch(n),
        compiler_params=_params(40, dimension_semantics=("arbitrary",)),
    )(z, z, z, z, lb_logits, onorm_g, *shards)


def _rms_bwd(dn, xhat, rstd):
    return rstd * (dn - xhat * jnp.mean(dn * xhat, axis=-1, keepdims=True))


def _tail(x, y_conv, y_hgrn, p, target, w_out, w_pg, w_pp_all, pe_g, fin_g):
    t = x.shape[0]
    tt = min(256, t)
    n_steps = t // tt

    def body(x_ref, yc_ref, yh_ref, p_ref, tg_ref, wo_hbm, wg_hbm, wp_hbm, pg_ref, fg_ref,
             dh1_ref, dyc_ref, dyh_ref, dwo_hbm, dwg_hbm, dwp_hbm, dpg_ref, dfg_ref, loss_ref,
             wo, wg, wp, dwo, dwg, dwp):
        i = pl.program_id(0)

        @pl.when(i == 0)
        def _():
            pltpu.sync_copy(wo_hbm, wo)
            pltpu.sync_copy(wg_hbm, wg)
            for d in range(N_DEV):
                pltpu.sync_copy(wp_hbm.at[d], wp.at[:, pl.ds(HEAD * d, HEAD)])
            dwo[...] = jnp.zeros_like(dwo)
            dwg[...] = jnp.zeros_like(dwg)
            dwp[...] = jnp.zeros_like(dwp)
            dpg_ref[...] = jnp.zeros_like(dpg_ref)
            dfg_ref[...] = jnp.zeros_like(dfg_ref)
            loss_ref[...] = jnp.zeros_like(loss_ref)

        ycv, yhv = yc_ref[...], yh_ref[...]
        h1 = (x_ref[...] + jnp.dot(ycv, wo[0:D, :], preferred_element_type=F32)
              + jnp.dot(yhv, wo[D:2 * D, :], preferred_element_type=F32))
        pb = p_ref[...].astype(BF16)
        pe = jnp.dot(pb, wp[...], preferred_element_type=F32)
        rstd1 = lax.rsqrt(jnp.mean(h1 * h1, axis=-1, keepdims=True) + EPS)
        n1 = h1 * rstd1
        rb = (n1 * pg_ref[...]).astype(BF16)
        gate = _sigmoid(jnp.dot(rb, wg[...], preferred_element_type=F32))
        h2 = h1 + gate * pe
        rstd2 = lax.rsqrt(jnp.mean(h2 * h2, axis=-1, keepdims=True) + EPS)
        n2 = h2 * rstd2
        err = n2 * fg_ref[...] - tg_ref[...]
        loss_ref[...] += _rowsum8(err * err)

        d_out = err * (1.0 / D)
        dfg_ref[...] += _rowsum8(d_out * n2)
        d_h2 = _rms_bwd(d_out * fg_ref[...], n2, rstd2)
        d_pe = (d_h2 * gate).astype(BF16)
        d_gpre = (d_h2 * pe * gate * (1.0 - gate)).astype(BF16)
        dwg[...] += _dot_tn(rb, d_gpre)
        dwp[...] += _dot_tn(pb, d_pe)
        dr = _dot_nt(d_gpre, wg[...])
        dpg_ref[...] += _rowsum8(dr * n1)
        d_h1 = d_h2 + _rms_bwd(dr * pg_ref[...], n1, rstd1)
        dh1_ref[...] = d_h1
        d_h1b = d_h1.astype(BF16)
        dwo[0:D, :] += _dot_tn(ycv, d_h1b)
        dwo[D:2 * D, :] += _dot_tn(yhv, d_h1b)
        dyc_ref[...] = _dot_nt(d_h1b, wo[0:D, :])
        dyh_ref[...] = _dot_nt(d_h1b, wo[D:2 * D, :])

        @pl.when(i == n_steps - 1)
        def _():
            pltpu.sync_copy(dwo, dwo_hbm)
            pltpu.sync_copy(dwg, dwg_hbm)
            for d in range(N_DEV):
                pltpu.sync_copy(dwp.at[:, pl.ds(HEAD * d, HEAD)], dwp_hbm.at[d])

    row = pl.BlockSpec((tt, D), lambda i: (i, 0))
    acc = _full((SUBLANES, D))
    return pl.pallas_call(
        body, name="tail_fwd_bwd", grid=(n_steps,),
        out_shape=[jax.ShapeDtypeStruct((t, D), F32)] * 3
        + [jax.ShapeDtypeStruct((2 * D, D), F32), jax.ShapeDtypeStruct((D, D), F32),
           jax.ShapeDtypeStruct((N_DEV, PLE, HEAD), F32)]
        + [jax.ShapeDtypeStruct((SUBLANES, D), F32)] * 3,
        in_specs=[row, row, row, pl.BlockSpec((tt, PLE), lambda i: (i, 0)), row,
                  ANY, ANY, ANY, _full((1, D)), _full((1, D))],
        out_specs=[row, row, row, ANY, ANY, ANY, acc, acc, acc],
        scratch_shapes=[pltpu.VMEM((2 * D, D), BF16), pltpu.VMEM((D, D), BF16),
                        pltpu.VMEM((PLE, D), BF16), pltpu.VMEM((2 * D, D), F32),
                        pltpu.VMEM((D, D), F32), pltpu.VMEM((PLE, D), F32)],
        compiler_params=_params(52, dimension_semantics=("arbitrary",)),
    )(x, y_conv, y_hgrn, p, target, w_out, w_pg, w_pp_all, pe_g, fin_g)


def _hgrn_bwd(dy, z, o_raw, states, lb_logits, onorm_g, grads):
    t = z.shape[0]
    tt = min(256, t)
    nc = tt // CHUNK
    n_steps = t // tt
    modes = ["scatter"] * len(grads)

    def body(dy_ref, q_ref, f_ref, i_ref, g_ref, o_ref, s_ref, lbl_ref, on_ref,
             dz_ref, don_ref, dlb_ref, dst):
        @pl.when(pl.program_id(0) == 0)
        def _():
            dst[...] = jnp.zeros_like(dst)
            don_ref[...] = jnp.zeros_like(don_ref)
            dlb_ref[...] = jnp.zeros_like(dlb_ref)

        lb, _ = _lower_bound(lbl_ref[...])
        rows = lax.broadcasted_iota(jnp.int32, (CHUNK, CHUNK), 0)
        cols = lax.broadcasted_iota(jnp.int32, (CHUNK, CHUNK), 1)
        causal = rows >= cols
        tri = causal.astype(BF16)
        tri_rev = (rows <= cols).astype(BF16)
        width = HEAD * HEAD_GROUP
        is_last = lax.broadcasted_iota(jnp.int32, (CHUNK, width), 0) == CHUNK - 1
        nn = (((1,), (0,)), ((), ()))
        tn = (((0,), (0,)), ((), ()))
        dg = functools.partial(lax.dot_general, preferred_element_type=F32)

        def chunk(cc, carry):
            c = nc - 1 - cc
            r0 = pl.multiple_of(c * CHUNK, CHUNK)
            rs = pl.ds(r0, CHUNK)
            for h0 in range(0, N_HEADS, HEAD_GROUP):
                cs = slice(HEAD * h0, HEAD * h0 + width)
                zq, zf, zg = q_ref[rs, cs], f_ref[rs, cs], g_ref[rs, cs]
                lbh = lb[:, cs]
                q, k, f, sig, sig_neg, e_q, e_qm, e_km, e_kd, e_last = _chunk_quantities(
                    zq, zf, lbh, tri)
                vb = i_ref[rs, cs].astype(BF16)
                qt, qm, km, kd = q * e_q, q * e_qm, k * e_km, k * e_kd
                qt_b, kd_b = qt.astype(BF16), kd.astype(BF16)
                qm_h, qm_l = _split(qm)
                km_h, km_l = _split(km)

                o = o_ref[rs, cs]
                rstd = lax.rsqrt(_head_mean(o * o) + EPS)
                n = o * rstd
                sg = _sigmoid(zg)
                dyv = dy_ref[rs, cs]
                on = on_ref[:, cs]
                d_zg = dyv * n * on * sg * (1.0 + zg * (1.0 - sg))
                d_on = dyv * zg * sg
                don_ref[:, cs] += _rowsum8(d_on * n)
                dn = d_on * on
                do_b = (rstd * (dn - n * _head_mean(dn * n))).astype(BF16)

                dv, dkd, dqt, dqm, dkm, s_dots = [], [], [], [], [], []
                for j, hs in enumerate(_HEAD_LANES):
                    s_old, ds_new = s_ref[c, h0 + j], dst[h0 + j]
                    ds_b = ds_new.astype(BF16)
                    a = jnp.where(causal, _dot_nt(qm_h[:, hs], km_h[:, hs]), 0.0)
                    da = jnp.where(causal, _dot_nt(do_b[:, hs], vb[:, hs]), 0.0)
                    dv.append(_dot_tn(a, do_b[:, hs]) + _dot_nt(kd_b[:, hs], ds_b))
                    dkd.append(_dot(vb[:, hs], ds_b))
                    dqt.append(_dot(do_b[:, hs], s_old))
                    da_h, da_l = _split(da)
                    dqm.append(dg(da_h, km_h[:, hs], nn)
                               + (dg(da_h, km_l[:, hs], nn) + dg(da_l, km_h[:, hs], nn)))
                    dkm.append(dg(da_h, qm_h[:, hs], tn)
                               + (dg(da_h, qm_l[:, hs], tn) + dg(da_l, qm_h[:, hs], tn)))
                    dst[h0 + j] = ds_new * e_last[:, hs] + _dot_tn(do_b[:, hs], qt_b[:, hs])
                    s_dots.append(jnp.sum(s_old.astype(F32) * ds_new, axis=0, keepdims=True))
                dv, dkd, dqt, dqm, dkm, s_dots = [
                    jnp.concatenate(parts, axis=1) for parts in (dv, dkd, dqt, dqm, dkm, s_dots)]
                dq = dqt * e_q + dqm * e_qm
                dk = dkm * e_km + dkd * e_kd
                last = jnp.sum(dkd * kd, axis=0, keepdims=True) + e_last * s_dots
                db = q * dq - k * dk + jnp.where(is_last, last, 0.0)
                dlogf = _tri_dot(tri_rev, db)
                common = sig_neg * (dlogf / f - dk)
                dlb_ref[:, cs] += _rowsum8(common)
                c0 = 3 * D + HEAD * h0
                sq = _sigmoid(zq)
                dz_ref[rs, c0:c0 + width] = (dq * sq * (1.0 + zq * (1.0 - sq))).astype(BF16)
                dz_ref[rs, D + c0:D + c0 + width] = ((1.0 - lbh) * sig * common).astype(BF16)
                dz_ref[rs, 2 * D + c0:2 * D + c0 + width] = dv.astype(BF16)
                dz_ref[rs, 3 * D + c0:3 * D + c0 + width] = d_zg.astype(BF16)
            return carry

        lax.fori_loop(0, nc, chunk, 0, unroll=True)

    rev = lambda i: n_steps - 1 - i
    col = lambda j: pl.BlockSpec((tt, D), lambda i: (rev(i), j))
    row = pl.BlockSpec((tt, D), lambda i: (rev(i), 0))
    acc = _full((SUBLANES, D))
    n = len(modes)
    return pl.pallas_call(
        _hosted(body, 9, 3, (n_steps,), modes), name="hgrn_bwd", grid=(n_steps,),
        out_shape=[jax.ShapeDtypeStruct((t, N_COLS), BF16),
                   jax.ShapeDtypeStruct((SUBLANES, D), F32),
                   jax.ShapeDtypeStruct((SUBLANES, D), F32)] + _recv_shapes(grads, modes),
        in_specs=[row, col(3), col(4), col(5), col(6), row,
                  pl.BlockSpec((nc, N_HEADS, HEAD, HEAD), lambda i: (rev(i), 0, 0, 0)),
                  _full((2, D)), _full((1, D))] + [ANY] * n,
        out_specs=[pl.BlockSpec((tt, N_COLS), lambda i: (rev(i), 0)), acc, acc] + [ANY] * n,
        scratch_shapes=[pltpu.VMEM((N_HEADS, HEAD, HEAD), F32)] + _exchange_scratch(n),
        compiler_params=_params(48, dimension_semantics=("arbitrary",)),
    )(dy, z, z, z, z, o_raw, states, lb_logits, onorm_g, *grads)


def _conv_bwd(dy, z, yc, y2, conv_w_all, cn_g, cn_b, w_pw2, dz, grads, modes):
    t = z.shape[0]
    tt = min(256, t)
    rc = 32
    n_steps = t // tt

    def body(dy_ref, val_ref, glu_ref, gate_ref, yc_ref, y2_ref, cw_ref, g_ref, b_ref, w_hbm,
             dz_in, dz_ref, dw_hbm, dcw_out, db2_ref, dg_ref, dbeta_ref, dcb_ref,
             w_vmem, dw, dbuf, dsh, y1buf, dnbuf, dcw_ref):
        i = pl.program_id(0)

        @pl.when(i == 0)
        def _():
            pltpu.sync_copy(w_hbm, w_vmem)
            dw[...] = jnp.zeros_like(dw)
            dbuf[tt:tt + CONV_PAD, :] = jnp.zeros((CONV_PAD, D), F32)
            dcw_ref[...] = jnp.zeros_like(dcw_ref)
            dcw_out[...] = jnp.zeros_like(dcw_out)
            db2_ref[...] = jnp.zeros_like(db2_ref)
            dg_ref[...] = jnp.zeros_like(dg_ref)
            dbeta_ref[...] = jnp.zeros_like(dbeta_ref)
            dcb_ref[...] = jnp.zeros_like(dcb_ref)

        gate = gate_ref[...]
        sg = _sigmoid(gate)
        dyv = dy_ref[...]
        dy2 = dyv * gate * sg
        dz_ref[:, 2 * D:3 * D] = (dyv * y2_ref[...] * sg * (1.0 + gate * (1.0 - sg))).astype(BF16)
        db2_ref[...] += _rowsum8(dy2)
        dy2b = dy2.astype(BF16)
        dnbuf[...] = _dot_nt(dy2b, w_vmem[...])

        def norm_chunk(r, carry):
            r0 = pl.multiple_of(r * rc, rc)
            rs = pl.ds(r0, rc)
            for g in range(N_HEADS):
                cs = slice(HEAD * g, HEAD * (g + 1))
                blk = yc_ref[rs, cs]
                mu = jnp.mean(blk, axis=-1, keepdims=True)
                cen = blk - mu
                rstd = lax.rsqrt(jnp.mean(cen * cen, axis=-1, keepdims=True) + EPS)
                xhat = cen * rstd
                n = xhat * g_ref[:, cs] + b_ref[:, cs]
                sn = _sigmoid(n)
                y1buf[rs, cs] = (n * sn).astype(BF16)
                dn = dnbuf[rs, cs] * sn * (1.0 + n * (1.0 - sn))
                dg_ref[:, cs] += _rowsum8(dn * xhat)
                dbeta_ref[:, cs] += _rowsum8(dn)
                dxh = dn * g_ref[:, cs]
                dyc = rstd * (dxh - jnp.mean(dxh, axis=-1, keepdims=True)
                              - xhat * jnp.mean(dxh * xhat, axis=-1, keepdims=True))
                dcb_ref[:, cs] += _rowsum8(dyc)
                dbuf[rs, cs] = dyc
            return carry

        lax.fori_loop(0, tt // rc, norm_chunk, 0, unroll=True)
        dw[...] += _dot_tn(y1buf[...], dy2b)
        _shifted_copies(dbuf, dsh, tt + 24)

        def conv_chunk(r, carry):
            r0 = pl.multiple_of(r * rc, rc)
            rs = pl.ds(r0, rc)
            for g in range(N_HEADS):
                cs = slice(HEAD * g, HEAD * (g + 1))
                sglu = _sigmoid(glu_ref[rs, cs])
                val = val_ref[rs, cs]
                v = val * sglu
                dv = jnp.zeros((rc, HEAD), F32)
                for ref, lo, taps in _tap_slabs(dbuf, dsh, lambda k: CONV_K - 1 - k):
                    slab = ref[pl.ds(r0 + lo, rc + taps[-1][1]), cs]
                    for k, off in taps:
                        d_later = slab[off:off + rc]
                        dv = dv + cw_ref[g, k:k + 1, :] * d_later
                        dcw_ref[g, k] += _rowsum8(v * d_later)
                dz_ref[rs, cs] = (dv * sglu).astype(BF16)
                dz_ref[rs, D + HEAD * g:D + HEAD * (g + 1)] = (
                    dv * val * sglu * (1.0 - sglu)).astype(BF16)
            return carry

        lax.fori_loop(0, tt // rc, conv_chunk, 0, unroll=2)
        dbuf[tt:tt + CONV_PAD, :] = dbuf[0:CONV_PAD, :]

        @pl.when(i == n_steps - 1)
        def _():
            pltpu.sync_copy(dw, dw_hbm)
            for g in range(N_HEADS):
                for k in range(CONV_K):
                    dcw_out[g, k:k + 1, :] = jnp.sum(dcw_ref[g, k], axis=0, keepdims=True)

    rev = lambda i: n_steps - 1 - i
    col = lambda j: pl.BlockSpec((tt, D), lambda i: (rev(i), j))
    row = pl.BlockSpec((tt, D), lambda i: (rev(i), 0))
    acc = _full((SUBLANES, D))
    n = len(modes)
    return pl.pallas_call(
        _hosted(body, 11, 7, (n_steps,), modes), name="conv_bwd", grid=(n_steps,),
        out_shape=[jax.ShapeDtypeStruct((t, N_COLS), BF16), jax.ShapeDtypeStruct((D, D), F32),
                   jax.ShapeDtypeStruct((N_DEV, CONV_PAD, HEAD), F32)]
        + [jax.ShapeDtypeStruct((SUBLANES, D), F32)] * 4 + _recv_shapes(grads, modes),
        in_specs=[row, col(0), col(1), col(2), row, row, _full((N_DEV, CONV_PAD, HEAD)),
                  _full((1, D)), _full((1, D)), ANY, ANY] + [ANY] * n,
        out_specs=[pl.BlockSpec((tt, 3 * D), lambda i: (rev(i), 0)), ANY,
                   _full((N_DEV, CONV_PAD, HEAD)), acc, acc, acc, acc] + [ANY] * n,
        input_output_aliases={10: 0},
        scratch_shapes=[pltpu.VMEM((D, D), BF16), pltpu.VMEM((D, D), F32),
                        pltpu.VMEM((tt + CONV_PAD, D), F32),
                        pltpu.VMEM((SUBLANES, tt + CONV_PAD, D), F32),
                        pltpu.VMEM((tt, D), BF16), pltpu.VMEM((tt, D), F32),
                        pltpu.VMEM((N_DEV, CONV_PAD, SUBLANES, HEAD), F32)] + _exchange_scratch(n),
        compiler_params=_params(52, dimension_semantics=("arbitrary",)),
    )(dy, z, z, z, yc, y2, conv_w_all, cn_g, cn_b, w_pw2, dz, *grads)


def _inproj_bwd_dx(dz, x, d_h1, ln_g, w_in_all, grads, modes):
    t = x.shape[0]
    tt = min(256, t)

    def body(dz_ref, x_ref, dh1_ref, g_ref, w_hbm, dx_ref, dg_ref, w_vmem):
        @pl.when(pl.program_id(0) == 0)
        def _():
            for d in range(N_DEV):
                pltpu.sync_copy(w_hbm.at[d], w_vmem.at[
                    d // 2, :, pl.ds(COLS_PER_DEV * (d % 2), COLS_PER_DEV)])
            dg_ref[...] = jnp.zeros_like(dg_ref)

        du = jnp.zeros((tt, D), F32)
        for q in range(N_CHIPS):
            du = du + lax.dot_general(
                dz_ref[:, PAIR_COLS * q:PAIR_COLS * (q + 1)], w_vmem[q],
                (((1,), (1,)), ((), ())), preferred_element_type=F32)
        xv = x_ref[...]
        rstd = lax.rsqrt(jnp.mean(xv * xv, axis=-1, keepdims=True) + EPS)
        xhat = xv * rstd
        dg_ref[...] += _rowsum8(du * xhat)
        dx_ref[...] = dh1_ref[...] + _rms_bwd(du * g_ref[...], xhat, rstd)

    row = pl.BlockSpec((tt, D), lambda i: (i, 0))
    n = len(modes)
    return pl.pallas_call(
        _hosted(body, 5, 2, (t // tt,), modes), name="inproj_bwd_dx", grid=(t // tt,),
        out_shape=[jax.ShapeDtypeStruct((t, D), F32), jax.ShapeDtypeStruct((SUBLANES, D), F32)]
        + _recv_shapes(grads, modes),
        in_specs=[pl.BlockSpec((tt, N_COLS), lambda i: (i, 0)), row, row, _full((1, D)), ANY]
        + [ANY] * n,
        out_specs=[row, _full((SUBLANES, D))] + [ANY] * n,
        scratch_shapes=[pltpu.VMEM((N_CHIPS, D, PAIR_COLS), BF16)] + _exchange_scratch(n),
        compiler_params=_params(48, dimension_semantics=("arbitrary",)),
    )(dz, x, d_h1, ln_g, w_in_all, *grads)


def _inproj_bwd_dw(name, u, dz, first, count, grads=(), modes=()):
    t = u.shape[0]
    tt = min(1024, t)
    grid = (count // 2, t // tt)
    n = len(modes)

    def body(u_ref, dz_ref, dw_ref):
        @pl.when(pl.program_id(1) == 0)
        def _():
            dw_ref[...] = jnp.zeros_like(dw_ref)

        both = lax.dot_general(u_ref[...], dz_ref[...], (((0,), (0,)), ((), ())),
                               preferred_element_type=F32)
        dw_ref[0] += both[:, :COLS_PER_DEV]
        dw_ref[1] += both[:, COLS_PER_DEV:]

    return pl.pallas_call(
        _hosted(body, 2, 1, grid, modes) if n else body, name=name, grid=grid,
        out_shape=[jax.ShapeDtypeStruct((count, D, COLS_PER_DEV), F32)]
        + _recv_shapes(grads, modes),
        in_specs=[pl.BlockSpec((tt, D), lambda j, i: (i, 0)),
                  pl.BlockSpec((tt, PAIR_COLS), lambda j, i: (i, first // 2 + j))] + [ANY] * n,
        out_specs=[pl.BlockSpec((2, D, COLS_PER_DEV), lambda j, i: (j, 0, 0))] + [ANY] * n,
        scratch_shapes=_exchange_scratch(n) if n else [],
        compiler_params=_params(48, dimension_semantics=("arbitrary", "arbitrary")),
    )(u, dz, *grads)


def _adamw(w, g, m, v):
    m = ADAM_B1 * m + (1.0 - ADAM_B1) * g
    v = ADAM_B2 * v + (1.0 - ADAM_B2) * (g * g)
    m_hat = m / (1.0 - ADAM_B1 ** ADAM_STEP)
    v_hat = v / (1.0 - ADAM_B2 ** ADAM_STEP)
    delta = -ADAM_LR * (m_hat / (jnp.sqrt(v_hat) + ADAM_EPS) + ADAM_WD * w)
    return delta, m, v


def _pack_small(partials):
    rows = sorted(partials)

    def body(*refs):
        ins, out_ref = refs[:-1], refs[-1]
        out_ref[...] = jnp.zeros_like(out_ref)
        for j, row in enumerate(rows):
            out_ref[row:row + 1, :] = jnp.sum(ins[j][...], axis=0, keepdims=True)

    return pl.pallas_call(
        body, name="pack_small", out_shape=jax.ShapeDtypeStruct((N_SMALL, D), F32),
    )(*[partials[row] for row in rows])


def _sum_adam(name, recvs, w, m, v, rows):
    r, c = w.shape
    n = len(recvs)
    side = lax.axis_index("x").astype(jnp.int32).reshape(1)

    def body(side_ref, *refs):
        w_ref, m_ref, v_ref, g_ref, d_ref, mo_ref, vo_ref = refs[n:]

        def finish(recv_ref):
            g = recv_ref[0]
            for s in range(1, recv_ref.shape[0]):
                g = g + recv_ref[s]
            g_ref[...] = g
            d_ref[...], mo_ref[...], vo_ref[...] = _adamw(w_ref[...], g, m_ref[...], v_ref[...])

        if n == 1:
            finish(refs[0])
        else:
            for s in range(n):
                pl.when(side_ref[0] == s)(functools.partial(finish, refs[s]))

    def recv_spec(s, rv):
        if n == 1:
            return pl.BlockSpec((rv.shape[0], rows, c), lambda i, side_ref: (0, i, 0))
        return pl.BlockSpec((rv.shape[0], rows, c),
                            lambda i, side_ref: (0, jnp.where(side_ref[0] == s, i, 0), 0))

    blk = pl.BlockSpec((rows, c), lambda i, side_ref: (i, 0))
    grid_spec = pltpu.PrefetchScalarGridSpec(
        num_scalar_prefetch=1, grid=(r // rows,),
        in_specs=[recv_spec(s, rv) for s, rv in enumerate(recvs)] + [blk, blk, blk],
        out_specs=[blk] * 4)
    return pl.pallas_call(
        body, name=name, grid_spec=grid_spec,
        out_shape=[jax.ShapeDtypeStruct((r, c), F32)] * 4,
        compiler_params=_params(48, dimension_semantics=("arbitrary",)),
    )(side, *recvs, w, m, v)


def _small_adam(gathered, lb_logits, w, m, v):
    def body(ga_ref, lbl_ref, w_ref, m_ref, v_ref, g_ref, d_ref, mo_ref, vo_ref, loss_ref):
        g = ga_ref[0]
        for s in range(1, N_DEV):
            g = g + ga_ref[s]
        s0, s1 = _lower_bound(lbl_ref[...])
        d_lb = g[R_LB0:R_LB0 + 1, :]
        rows = lax.broadcasted_iota(jnp.int32, (N_SMALL, D), 0)
        g = jnp.where(rows == R_LB0, d_lb * s0 * (1.0 - s0), g)
        g = jnp.where(rows == R_LB1, -d_lb * s0 * s1, g)
        g_ref[...] = g
        d_ref[...], mo_ref[...], vo_ref[...] = _adamw(w_ref[...], g, m_ref[...], v_ref[...])
        loss_ref[...] = (0.5 / D) * jnp.sum(g[R_LOSS:R_LOSS + 1, :], axis=-1, keepdims=True)

    return pl.pallas_call(
        body, name="small_adam",
        out_shape=[jax.ShapeDtypeStruct((N_SMALL, D), F32)] * 4 + [jax.ShapeDtypeStruct((1, 1), F32)],
    )(gathered, lb_logits, w, m, v)


def _pad_rows(a, rows):
    return jnp.pad(a, ((0, rows - a.shape[0]), (0, 0)))


def _pack_rows(rows):
    rows = [r.reshape(-1, D) for r in rows]
    packed = jnp.concatenate(rows, axis=0)
    return _pad_rows(packed, N_SMALL)


def kernel(x, p, ln_g, w_in, conv_w, conv_b, cnorm_g, cnorm_b, w_pw2, b_pw2, lb_logits, onorm_g, w_out, pe_norm_g, w_pg, w_pp, final_g, loss_target, m_ln_g, m_w_in, m_conv_w, m_conv_b, m_cnorm_g, m_cnorm_b, m_w_pw2, m_b_pw2, m_lb_logits, m_onorm_g, m_w_out, m_pe_norm_g, m_w_pg, m_w_pp, m_final_g, v_ln_g, v_w_in, v_conv_w, v_conv_b, v_cnorm_g, v_cnorm_b, v_w_pw2, v_b_pw2, v_lb_logits, v_onorm_g, v_w_out, v_pe_norm_g, v_w_pg, v_w_pp, v_final_g):
    t = x.shape[1]
    x2 = x.reshape(t, D)
    p2 = p.reshape(t, PLE)
    tg2 = loss_target.reshape(t, D)
    fin_g = final_g.reshape(1, D)

    z, u, w_in_all = _inproj_fwd(x2, ln_g, w_in[0].astype(BF16))
    (o_raw, y_hgrn, states, conv_w_all, w_pw2_all, w_out_all, w_pg_all, w_pp_all) = _hgrn_fwd(
        z, lb_logits, onorm_g,
        [_pad_rows(conv_w[0], CONV_PAD), w_pw2[0].astype(BF16), w_out[0].astype(BF16),
         w_pg[0].astype(BF16), w_pp[0].astype(BF16)])
    w_pw2_full = w_pw2_all.reshape(D, D)
    w_out_full = w_out_all.reshape(2 * D, D)
    w_pg_full = w_pg_all.reshape(D, D)
    yc, y2, y_conv = _conv_fwd(z, conv_w_all, conv_b, cnorm_g, cnorm_b, w_pw2_full, b_pw2)

    (d_h1, dy_conv, dy_hgrn, d_w_out, d_w_pg, d_w_pp, d_pen_p, d_fin_p, loss_p) = _tail(
        x2, y_conv, y_hgrn, p2, tg2, w_out_full, w_pg_full, w_pp_all, pe_norm_g, fin_g)

    dz, d_on_p, d_lb_p, r_w_out, r_w_pg, r_w_pp = _hgrn_bwd(
        dy_hgrn, z, o_raw, states, lb_logits, onorm_g,
        [d_w_out.reshape(N_DEV, 2 * D // N_DEV, D), d_w_pg.reshape(N_DEV, D // N_DEV, D), d_w_pp])
    (d_w_in_hi,) = _inproj_bwd_dw("inproj_bwd_dw_hi", u, dz, N_DEV // 2, N_DEV // 2)
    dz, d_w_pw2, d_conv_w, d_b2_p, d_cng_p, d_cnb_p, d_cb_p, r_w_in_hi = _conv_bwd(
        dy_conv, z, yc, y2, conv_w_all, cnorm_g, cnorm_b, w_pw2_full, dz, [d_w_in_hi], [1])
    d_w_in_lo, r_w_pw2, r_conv_w = _inproj_bwd_dw(
        "inproj_bwd_dw_lo", u, dz, 0, N_DEV // 2,
        [d_w_pw2.reshape(N_DEV, D // N_DEV, D), d_conv_w], ["scatter", "scatter"])
    chip_lo = _pair_reduce("pair_reduce_lo", d_w_in_lo)
    grad_x, d_ln_p, r_w_in_lo = _inproj_bwd_dx(
        dz, x2, d_h1, ln_g, w_in_all, [chip_lo], [("chip", 0)])

    small = _pack_small({R_LN: d_ln_p, R_CONVB: d_cb_p, R_CNG: d_cng_p, R_CNB: d_cnb_p,
                         R_BPW2: d_b2_p, R_LB0: d_lb_p, R_ON: d_on_p, R_PEN: d_pen_p,
                         R_FIN: d_fin_p, R_LOSS: loss_p})
    (small_all,) = _exchange_call("gather_small", [small], ["gather"])

    big = {}
    big["w_in"] = _sum_adam("adam_w_in", [r_w_in_lo, r_w_in_hi], w_in[0], m_w_in[0], v_w_in[0], 128)
    cw = _sum_adam("adam_conv_w", [r_conv_w], _pad_rows(conv_w[0], CONV_PAD),
                   _pad_rows(m_conv_w[0], CONV_PAD), _pad_rows(v_conv_w[0], CONV_PAD), CONV_PAD)
    big["conv_w"] = [a[:CONV_K] for a in cw]
    big["w_pw2"] = _sum_adam("adam_w_pw2", [r_w_pw2], w_pw2[0], m_w_pw2[0], v_w_pw2[0], 128)
    big["w_out"] = _sum_adam("adam_w_out", [r_w_out], w_out[0], m_w_out[0], v_w_out[0], 128)
    big["w_pg"] = _sum_adam("adam_w_pg", [r_w_pg], w_pg[0], m_w_pg[0], v_w_pg[0], 128)
    big["w_pp"] = _sum_adam("adam_w_pp", [r_w_pp], w_pp[0], m_w_pp[0], v_w_pp[0], PLE)

    small_w = [ln_g, conv_b, cnorm_g, cnorm_b, b_pw2, lb_logits, onorm_g, pe_norm_g, final_g]
    small_m = [m_ln_g, m_conv_b, m_cnorm_g, m_cnorm_b, m_b_pw2, m_lb_logits, m_onorm_g,
               m_pe_norm_g, m_final_g]
    small_v = [v_ln_g, v_conv_b, v_cnorm_g, v_cnorm_b, v_b_pw2, v_lb_logits, v_onorm_g,
               v_pe_norm_g, v_final_g]
    sg, sd, sm, sv, loss = _small_adam(small_all, lb_logits, _pack_rows(small_w),
                                       _pack_rows(small_m), _pack_rows(small_v))

    small_rows = {"ln_g": (R_LN, 1), "conv_b": (R_CONVB, 1), "cnorm_g": (R_CNG, 1),
                  "cnorm_b": (R_CNB, 1), "b_pw2": (R_BPW2, 1), "lb_logits": (R_LB0, 2),
                  "onorm_g": (R_ON, 1), "pe_norm_g": (R_PEN, 1), "final_g": (R_FIN, 1)}
    order = ["ln_g", "w_in", "conv_w", "conv_b", "cnorm_g", "cnorm_b", "w_pw2", "b_pw2",
             "lb_logits", "onorm_g", "w_out", "pe_norm_g", "w_pg", "w_pp", "final_g"]

    def leaf(kind, name):
        if name in big:
            return big[name][kind][None]
        r0, n = small_rows[name]
        a = (sg, sd, sm, sv)[kind][r0:r0 + n]
        return a.reshape(D) if name == "final_g" else a

    outs = [loss.reshape(()), grad_x.reshape(1, t, D)]
    for kind in range(4):
        outs += [leaf(kind, name) for name in order]
    return tuple(outs)
```

```python
import functools

import jax
import jax.numpy as jnp
from jax import lax
from jax.experimental import pallas as pl
from jax.experimental.pallas import tpu as pltpu

F32 = jnp.float32
BF16 = jnp.bfloat16
MESH = pl.DeviceIdType.MESH

N_DEV = 8
D = 1024
N_COLS = 7 * D
COLS_PER_DEV = N_COLS // N_DEV
PLE = 256
HEAD = 128
N_HEADS = D // HEAD
CONV_K = 31
CONV_PAD = 32
CHUNK = 64
EPS = 1e-6
SUBLANES = 8

ADAM_LR = 0.001
ADAM_B1 = 0.9
ADAM_B2 = 0.999
ADAM_EPS = 1e-08
ADAM_WD = 0.01
ADAM_STEP = 10

MIB = 1024 * 1024
N_SMALL = 16
R_LN, R_CONVB, R_CNG, R_CNB, R_BPW2, R_LB0, R_LB1, R_ON, R_PEN, R_FIN, R_LOSS = range(11)


def _params(vmem_mib, **kw):
    return pltpu.CompilerParams(vmem_limit_bytes=vmem_mib * MIB, **kw)


def _dot(a, b):
    return jnp.dot(a.astype(BF16), b.astype(BF16), preferred_element_type=F32)


def _dot_nt(a, b):
    return lax.dot_general(a.astype(BF16), b.astype(BF16), (((1,), (1,)), ((), ())),
                           preferred_element_type=F32)


def _dot_tn(a, b):
    return lax.dot_general(a.astype(BF16), b.astype(BF16), (((0,), (0,)), ((), ())),
                           preferred_element_type=F32)


def _split(a):
    hi = a.astype(BF16)
    return hi, (a - hi.astype(F32)).astype(BF16)


def _sigmoid(x):
    return 1.0 / (1.0 + jnp.exp(-x))


def _rowsum8(a):
    r, c = a.shape
    return jnp.sum(a.reshape(r // SUBLANES, SUBLANES, c), axis=0)


def _tri_dot(tri, a):
    hi = a.astype(BF16)
    r1 = a - hi.astype(F32)
    mid = r1.astype(BF16)
    lo = (r1 - mid.astype(F32)).astype(BF16)
    return (jnp.dot(tri, hi, preferred_element_type=F32)
            + jnp.dot(tri, mid, preferred_element_type=F32)
            + jnp.dot(tri, lo, preferred_element_type=F32))


def _lower_bound(lbl):
    l0, l1 = lbl[0:1, :], lbl[1:2, :]
    m = jnp.maximum(l0, l1)
    e0, e1 = jnp.exp(l0 - m), jnp.exp(l1 - m)
    s = e0 + e1
    return e0 / s, e1 / s


ANY = pl.BlockSpec(memory_space=pl.ANY)


def _full(shape):
    return pl.BlockSpec(shape, lambda i: (0,) * len(shape))


def _peer(x, y, c, k):
    px = 1 - x if k & 4 else x
    py = 1 - y if k & 2 else y
    pc = 1 - c if k & 1 else c
    return (px, py, pc), 4 * px + 2 * py + pc


class _Exchange:
    def __init__(self, srcs, outs, modes, send_sems, recv_sems, local_sems):
        x, y, c = lax.axis_index("x"), lax.axis_index("y"), lax.axis_index("c")
        me = 4 * x + 2 * y + c
        self.starts, self.send_waits, self.recv_waits = [], [], []

        def remote(a, k, src, slot, peer, when):
            sem = a * N_DEV + k
            cp = pltpu.make_async_remote_copy(
                src_ref=src, dst_ref=outs[a].at[slot], send_sem=send_sems.at[sem],
                recv_sem=recv_sems.at[sem], device_id=peer, device_id_type=MESH)
            self.starts.append((when, cp.start))
            self.send_waits.append((when, cp.wait_send))

        def arrival(a, k, slot, when):
            sem = a * N_DEV + k
            cp = pltpu.make_async_remote_copy(
                src_ref=outs[a].at[slot], dst_ref=outs[a].at[slot], send_sem=send_sems.at[sem],
                recv_sem=recv_sems.at[sem], device_id=(x, y, c), device_id_type=MESH)
            self.recv_waits.append((when, cp.wait_recv))

        def local(a, src, slot, when):
            cp = pltpu.make_async_copy(src, outs[a].at[slot], local_sems.at[a])
            self.starts.append((when, cp.start))
            self.send_waits.append((when, cp.wait))

        for a, (src, mode) in enumerate(zip(srcs, modes)):
            if mode in ("gather", "scatter"):
                local(a, src if mode == "gather" else src.at[me], me, None)
                for k in range(1, N_DEV):
                    peer, peer_idx = _peer(x, y, c, k)
                    remote(a, k, src if mode == "gather" else src.at[peer_idx], me, peer, None)
                    arrival(a, k, peer_idx, None)
                continue
            if isinstance(mode, tuple):
                here, away = x == mode[1], x != mode[1]
                chip = 2 * x + y
                local(a, src.at[y], chip, here)
                remote(a, 1, src.at[1 - y], chip, (x, 1 - y, c), here)
                remote(a, 2, src.at[y], chip, (1 - x, y, c), away)
                remote(a, 3, src.at[1 - y], chip, (1 - x, 1 - y, c), away)
                arrival(a, 1, 2 * x + 1 - y, here)
                arrival(a, 2, 2 * (1 - x) + y, here)
                arrival(a, 3, 2 * (1 - x) + 1 - y, here)
                continue
            here, away = x == mode, x != mode
            for kk in range(4):
                py = 1 - y if kk & 2 else y
                pc = 1 - c if kk & 1 else c
                block = src.at[2 * py + pc]
                if kk == 0:
                    local(a, block, me, here)
                else:
                    remote(a, kk, block, me, (x, py, pc), here)
                remote(a, 4 + kk, block, me, (1 - x, py, pc), away)
            for k in range(1, N_DEV):
                arrival(a, k, _peer(x, y, c, k)[1], here)

    @staticmethod
    def _run(actions):
        for when, fn in actions:
            if when is None:
                fn()
            else:
                pl.when(when)(fn)

    def start(self):
        self._run(self.starts)

    def wait(self):
        self._run(self.recv_waits)
        self._run(self.send_waits)


def _exchange_scratch(n):
    return [pltpu.SemaphoreType.DMA((n * N_DEV,)), pltpu.SemaphoreType.DMA((n * N_DEV,)),
            pltpu.SemaphoreType.DMA((n,))]


def _recv_shapes(srcs, modes):
    def shape(s, m):
        if m == "gather":
            return (N_DEV,) + s.shape
        return (N_DEV // 2 if isinstance(m, tuple) else N_DEV,) + s.shape[1:]

    return [jax.ShapeDtypeStruct(shape(s, m), s.dtype) for s, m in zip(srcs, modes)]


def _pair_reduce(name, blocks):
    shape = (2,) + blocks.shape[1:]

    def body(src, out_ref, stage, mine, send_sems, recv_sems, local_sems):
        x, y, c = lax.axis_index("x"), lax.axis_index("y"), lax.axis_index("c")
        sends, waits = [], []
        for py in range(2):
            sends.append(pltpu.make_async_remote_copy(
                src_ref=src.at[2 * py + 1 - c], dst_ref=stage.at[py], send_sem=send_sems.at[py],
                recv_sem=recv_sems.at[py], device_id=(x, y, 1 - c), device_id_type=MESH))
            waits.append(pltpu.make_async_copy(src.at[2 * py + c], mine.at[py], local_sems.at[py]))
        for cp in sends + waits:
            cp.start()
        for cp in waits:
            cp.wait()
        for cp in sends:
            cp.wait_recv()
        out_ref[...] = mine[...] + stage[...]
        for cp in sends:
            cp.wait_send()

    return pl.pallas_call(
        body, name=name, out_shape=jax.ShapeDtypeStruct(shape, F32), in_specs=[ANY],
        scratch_shapes=[pltpu.VMEM(shape, F32), pltpu.VMEM(shape, F32),
                        pltpu.SemaphoreType.DMA((2,)), pltpu.SemaphoreType.DMA((2,)),
                        pltpu.SemaphoreType.DMA((2,))],
        compiler_params=_params(40),
    )(blocks)


def _exchange_call(name, srcs, modes):
    n = len(srcs)

    def body(*refs):
        xch = _Exchange(refs[:n], refs[n:2 * n], modes, *refs[2 * n:])
        xch.start()
        xch.wait()

    return pl.pallas_call(
        body, name=name, out_shape=_recv_shapes(srcs, modes),
        in_specs=[ANY] * n, out_specs=[ANY] * n, scratch_shapes=_exchange_scratch(n),
    )(*srcs)


def _hosted(body, n_in, n_out, grid, modes):
    n = len(modes)

    def hosted(*refs):
        ins, srcs = refs[:n_in], refs[n_in:n_in + n]
        outs = refs[n_in + n:n_in + n + n_out]
        bufs = refs[n_in + n + n_out:n_in + 2 * n + n_out]
        scratch = refs[n_in + 2 * n + n_out:-3]
        xch = _Exchange(srcs, bufs, modes, *refs[-3:])
        first, last = True, True
        for axis, size in enumerate(grid):
            first = jnp.logical_and(first, pl.program_id(axis) == 0)
            last = jnp.logical_and(last, pl.program_id(axis) == size - 1)
        pl.when(first)(xch.start)
        body(*ins, *outs, *scratch)
        pl.when(last)(xch.wait)

    return hosted


N_CHIPS = N_DEV // 2
PAIR_COLS = 2 * COLS_PER_DEV
PUSHED = (1, 2, 4, 6)
FORWARDED = (2, 4, 6)
NORM_ROWS = 32


def _inproj_fwd(x, ln_g, w_shard):
    t = x.shape[0]
    tt = min(512, t)
    n_t = t // tt
    chip = 2 * lax.axis_index("x") + lax.axis_index("y")
    order = jnp.bitwise_xor(chip, jnp.arange(N_CHIPS, dtype=jnp.int32)).astype(jnp.int32)

    def body(order_ref, x_ref, g_ref, shard_hbm, z_ref, u_ref, w_all,
             u_all, w_blk, w_send, w_recv, w_local):
        p, i = pl.program_id(0), pl.program_id(1)
        x, y, c = lax.axis_index("x"), lax.axis_index("y"), lax.axis_index("c")
        mine = 4 * x + 2 * y + c

        def push(k):
            peer, _ = _peer(x, y, c, k)
            return pltpu.make_async_remote_copy(
                src_ref=shard_hbm, dst_ref=w_all.at[mine], send_sem=w_send.at[k],
                recv_sem=w_recv.at[k], device_id=peer, device_id_type=MESH)

        def forward(k):
            _, owner = _peer(x, y, c, k)
            return pltpu.make_async_remote_copy(
                src_ref=w_all.at[owner], dst_ref=w_all.at[owner], send_sem=w_send.at[k + 1],
                recv_sem=w_recv.at[k + 1], device_id=(x, y, 1 - c), device_id_type=MESH)

        def landed(k):
            _, owner = _peer(x, y, c, k)
            return pltpu.make_async_remote_copy(
                src_ref=w_all.at[owner], dst_ref=w_all.at[owner], send_sem=w_send.at[k],
                recv_sem=w_recv.at[k], device_id=(x, y, c), device_id_type=MESH)

        keep = pltpu.make_async_copy(shard_hbm, w_all.at[mine], w_local.at[0])

        def load_pair(step):
            same = shard_hbm if step == 0 else w_all.at[_peer(x, y, c, 2 * step)[1]]
            other = w_all.at[_peer(x, y, c, 2 * step + 1)[1]]
            for side in range(2):
                @pl.when(c == side)
                def _(side=side):
                    pltpu.sync_copy(same, w_blk.at[:, pl.ds(COLS_PER_DEV * side, COLS_PER_DEV)])
                    pltpu.sync_copy(
                        other, w_blk.at[:, pl.ds(COLS_PER_DEV * (1 - side), COLS_PER_DEV)])

        @pl.when(jnp.logical_and(p == 0, i == 0))
        def _():
            for k in PUSHED[:-1]:
                push(k).start()
            keep.start()

        @pl.when(jnp.logical_and(p == 1, i == 0))
        def _():
            for k in PUSHED[1:-1]:
                push(k).wait_send()
            push(PUSHED[-1]).start()

        for step in range(N_CHIPS):
            @pl.when(jnp.logical_and(p == step, i == 0))
            def _(step=step):
                landed(2 * step + 1).wait_recv()
                load_pair(step)

        rows = pl.ds(pl.multiple_of(i * tt, tt), tt)

        @pl.when(p == 0)
        def _():
            def norm_rows(r, carry):
                sub = pl.ds(pl.multiple_of(r * NORM_ROWS, NORM_ROWS), NORM_ROWS)
                xv = x_ref[sub, :]
                rstd = lax.rsqrt(jnp.mean(xv * xv, axis=-1, keepdims=True) + EPS)
                ub = (xv * rstd * g_ref[...]).astype(BF16)
                u_ref[sub, :] = ub
                u_all[pl.ds(pl.multiple_of(i * tt + r * NORM_ROWS, NORM_ROWS), NORM_ROWS), :] = ub
                return carry

            lax.fori_loop(0, tt // NORM_ROWS, norm_rows, 0, unroll=2)

        z_ref[...] = jnp.dot(u_all[rows, :], w_blk[...], preferred_element_type=F32)

        for step in range(1, N_CHIPS):
            @pl.when(jnp.logical_and(p == step - 1, i == n_t - 1))
            def _(step=step):
                landed(2 * step).wait_recv()
                forward(2 * step).start()

        @pl.when(jnp.logical_and(p == N_CHIPS - 1, i == n_t - 1))
        def _():
            push(PUSHED[0]).wait_send()
            push(PUSHED[-1]).wait_send()
            for k in FORWARDED:
                forward(k).wait_send()
            keep.wait()

    first_pass = lambda p, i, order_ref: (jnp.where(p == 0, i, n_t - 1), 0)
    grid_spec = pltpu.PrefetchScalarGridSpec(
        num_scalar_prefetch=1, grid=(N_CHIPS, n_t),
        in_specs=[pl.BlockSpec((tt, D), first_pass),
                  pl.BlockSpec((1, D), lambda p, i, order_ref: (0, 0)), ANY],
        out_specs=[pl.BlockSpec((tt, PAIR_COLS), lambda p, i, order_ref: (i, order_ref[p])),
                   pl.BlockSpec((tt, D), first_pass), ANY],
        scratch_shapes=[pltpu.VMEM((t, D), BF16), pltpu.VMEM((D, PAIR_COLS), BF16),
                        pltpu.SemaphoreType.DMA((N_DEV,)), pltpu.SemaphoreType.DMA((N_DEV,)),
                        pltpu.SemaphoreType.DMA((1,))])
    return pl.pallas_call(
        body, name="inproj_fwd", grid_spec=grid_spec,
        out_shape=[jax.ShapeDtypeStruct((t, N_COLS), F32), jax.ShapeDtypeStruct((t, D), BF16),
                   jax.ShapeDtypeStruct((N_DEV,) + w_shard.shape, BF16)],
        compiler_params=_params(48, dimension_semantics=("arbitrary", "arbitrary")),
    )(order, x, ln_g, w_shard)


def _shifted_copies(buf, shifted, rows):
    for b in range(1, SUBLANES):
        shifted[b, 0:rows, :] = buf[b:b + rows, :]


def _tap_slabs(buf, shifted, offset_of_tap):
    groups = {}
    for k in range(CONV_K):
        a, b = divmod(offset_of_tap(k), SUBLANES)
        groups.setdefault(b, []).append((SUBLANES * a, k))
    out = []
    for b, taps in sorted(groups.items()):
        taps.sort()
        lo = taps[0][0]
        out.append((buf if b == 0 else shifted.at[b], lo, [(k, off - lo) for off, k in taps]))
    return out


def _group_norm_stats(blk):
    mu = jnp.mean(blk, axis=-1, keepdims=True)
    cen = blk - mu
    var = jnp.mean(cen * cen, axis=-1, keepdims=True)
    return cen * lax.rsqrt(var + EPS)


def _conv_fwd(z, conv_w_all, conv_b, cn_g, cn_b, w_pw2, b_pw2):
    t = z.shape[0]
    tt = min(256, t)
    rc = 128

    def body(val_ref, glu_ref, gate_ref, cw_ref, cb_ref, g_ref, b_ref, w_hbm, b2_ref,
             yc_ref, y2_ref, yo_ref, w_vmem, vbuf, vsh, y1buf):
        @pl.when(pl.program_id(0) == 0)
        def _():
            pltpu.sync_copy(w_hbm, w_vmem)
            vbuf[0:CONV_PAD, :] = jnp.zeros((CONV_PAD, D), F32)

        vbuf[CONV_PAD:CONV_PAD + tt, :] = val_ref[...] * _sigmoid(glu_ref[...])
        _shifted_copies(vbuf, vsh, tt + 24)

        for g in range(N_HEADS):
            cs = slice(HEAD * g, HEAD * (g + 1))

            def row_chunk(r, carry, g=g, cs=cs):
                r0 = pl.multiple_of(r * rc, rc)
                acc = jnp.broadcast_to(cb_ref[:, cs], (rc, HEAD))
                for ref, lo, taps in _tap_slabs(vbuf, vsh, lambda k: k + 2):
                    slab = ref[pl.ds(r0 + lo, rc + taps[-1][1]), cs]
                    for k, off in taps:
                        acc = acc + cw_ref[g, k:k + 1, :] * slab[off:off + rc]
                yc_ref[pl.ds(r0, rc), cs] = acc
                n = _group_norm_stats(acc) * g_ref[:, cs] + b_ref[:, cs]
                y1buf[pl.ds(r0, rc), cs] = (n * _sigmoid(n)).astype(BF16)
                return carry

            lax.fori_loop(0, tt // rc, row_chunk, 0, unroll=True)
        vbuf[0:CONV_PAD, :] = vbuf[tt:tt + CONV_PAD, :]
        y2 = jnp.dot(y1buf[...], w_vmem[...], preferred_element_type=F32) + b2_ref[...]
        y2_ref[...] = y2
        gate = gate_ref[...]
        yo_ref[...] = (y2 * gate * _sigmoid(gate)).astype(BF16)

    col = lambda j: pl.BlockSpec((tt, D), lambda i: (i, j))
    row = pl.BlockSpec((tt, D), lambda i: (i, 0))
    return pl.pallas_call(
        body, name="conv_fwd", grid=(t // tt,),
        out_shape=[jax.ShapeDtypeStruct((t, D), F32), jax.ShapeDtypeStruct((t, D), F32),
                   jax.ShapeDtypeStruct((t, D), BF16)],
        in_specs=[col(0), col(1), col(2), _full((N_DEV, CONV_PAD, HEAD)), _full((1, D)),
                  _full((1, D)), _full((1, D)), ANY, _full((1, D))],
        out_specs=[row, row, row],
        scratch_shapes=[pltpu.VMEM((D, D), BF16), pltpu.VMEM((tt + CONV_PAD, D), F32),
                        pltpu.VMEM((SUBLANES, tt + CONV_PAD, D), F32), pltpu.VMEM((tt, D), BF16)],
        compiler_params=_params(48, dimension_semantics=("arbitrary",)),
    )(z, z, z, conv_w_all, conv_b, cn_g, cn_b, w_pw2, b_pw2)


HEAD_GROUP = 8
_HEAD_LANES = [slice(HEAD * j, HEAD * (j + 1)) for j in range(HEAD_GROUP)]


def _head_mean(a):
    return jnp.concatenate(
        [jnp.broadcast_to(jnp.mean(a[:, hs], axis=-1, keepdims=True), (a.shape[0], HEAD))
         for hs in _HEAD_LANES], axis=1)


def _chunk_quantities(zq, zf, lbh, tri):
    sig = _sigmoid(zf)
    sig_neg = _sigmoid(-zf)
    f = lbh + (1.0 - lbh) * sig
    k = (1.0 - lbh) * sig_neg
    q = zq * _sigmoid(zq)
    b = _tri_dot(tri, jnp.log(f))
    b_mid = b[CHUNK // 2 - 1:CHUNK // 2, :]
    b_last = b[CHUNK - 1:CHUNK, :]
    e_q = jnp.exp(b)
    e_qm = jnp.exp(b - b_mid)
    e_km = jnp.exp(b_mid - b)
    e_kd = jnp.exp(b_last - b)
    return q, k, f, sig, sig_neg, e_q, e_qm, e_km, e_kd, jnp.exp(b_last)


def _hgrn_fwd(z, lb_logits, onorm_g, shards):
    t = z.shape[0]
    tt = min(256, t)
    nc = tt // CHUNK
    modes = ["gather"] * len(shards)
    n = len(modes)

    def body(q_ref, f_ref, i_ref, g_ref, lbl_ref, on_ref, o_ref, y_ref, s_ref, st):
        @pl.when(pl.program_id(0) == 0)
        def _():
            st[...] = jnp.zeros_like(st)

        lb, _ = _lower_bound(lbl_ref[...])
        rows = lax.broadcasted_iota(jnp.int32, (CHUNK, CHUNK), 0)
        cols = lax.broadcasted_iota(jnp.int32, (CHUNK, CHUNK), 1)
        causal = rows >= cols
        tri = causal.astype(BF16)

        def chunk(c, carry):
            r0 = pl.multiple_of(c * CHUNK, CHUNK)
            rs = pl.ds(r0, CHUNK)
            for h0 in range(0, N_HEADS, HEAD_GROUP):
                cs = slice(HEAD * h0, HEAD * (h0 + HEAD_GROUP))
                q, k, _, _, _, e_q, e_qm, e_km, e_kd, e_last = _chunk_quantities(
                    q_ref[rs, cs], f_ref[rs, cs], lb[:, cs], tri)
                v = i_ref[rs, cs].astype(BF16)
                qm, km = (q * e_qm).astype(BF16), (k * e_km).astype(BF16)
                qt, kd = (q * e_q).astype(BF16), (k * e_kd).astype(BF16)
                outs = []
                for j, hs in enumerate(_HEAD_LANES):
                    s_old = st[h0 + j]
                    s_ref[c, h0 + j] = s_old.astype(BF16)
                    a = jnp.where(causal, _dot_nt(qm[:, hs], km[:, hs]), 0.0)
                    outs.append(_dot_nt(qt[:, hs], s_old) + _dot(a, v[:, hs]))
                    st[h0 + j] = s_old * e_last[:, hs] + _dot_tn(v[:, hs], kd[:, hs])
                o = jnp.concatenate(outs, axis=1)
                o_ref[rs, cs] = o
                n = o * lax.rsqrt(_head_mean(o * o) + EPS)
                zg = g_ref[rs, cs]
                y_ref[rs, cs] = (n * on_ref[:, cs] * zg * _sigmoid(zg)).astype(BF16)
            return carry

        lax.fori_loop(0, nc, chunk, 0, unroll=True)

    col = lambda j: pl.BlockSpec((tt, D), lambda i: (i, j))
    row = pl.BlockSpec((tt, D), lambda i: (i, 0))
    return pl.pallas_call(
        _hosted(body, 6, 3, (t // tt,), modes), name="hgrn_fwd", grid=(t // tt,),
        out_shape=[jax.ShapeDtypeStruct((t, D), F32), jax.ShapeDtypeStruct((t, D), BF16),
                   jax.ShapeDtypeStruct((t // CHUNK, N_HEADS, HEAD, HEAD), BF16)]
        + _recv_shapes(shards, modes),
        in_specs=[col(3), col(4), col(5), col(6), _full((2, D)), _full((1, D))] + [ANY] * n,
        out_specs=[row, row, pl.BlockSpec((nc, N_HEADS, HEAD, HEAD), lambda i: (i, 0, 0, 0))]
        + [ANY] * n,
        scratch_shapes=[pltpu.VMEM((N_HEADS, HEAD, HEAD), F32)] + _exchange_scratch(n),
        compiler_params=_params(40, dimension_semantics=("arbitrary",)),
    )(z, z, z, z, lb_logits, onorm_g, *shards)


def _rms_bwd(dn, xhat, rstd):
    return rstd * (dn - xhat * jnp.mean(dn * xhat, axis=-1, keepdims=True))


def _tail(x, y_conv, y_hgrn, p, target, w_out, w_pg, w_pp_all, pe_g, fin_g):
    t = x.shape[0]
    tt = min(256, t)
    n_steps = t // tt

    def body(x_ref, yc_ref, yh_ref, p_ref, tg_ref, wo_hbm, wg_hbm, wp_hbm, pg_ref, fg_ref,
             dh1_ref, dyc_ref, dyh_ref, dwo_hbm, dwg_hbm, dwp_hbm, dpg_ref, dfg_ref, loss_ref,
             wo, wg, wp, dwo, dwg, dwp):
        i = pl.program_id(0)

        @pl.when(i == 0)
        def _():
            pltpu.sync_copy(wo_hbm, wo)
            pltpu.sync_copy(wg_hbm, wg)
            for d in range(N_DEV):
                pltpu.sync_copy(wp_hbm.at[d], wp.at[:, pl.ds(HEAD * d, HEAD)])
            dwo[...] = jnp.zeros_like(dwo)
            dwg[...] = jnp.zeros_like(dwg)
            dwp[...] = jnp.zeros_like(dwp)
            dpg_ref[...] = jnp.zeros_like(dpg_ref)
            dfg_ref[...] = jnp.zeros_like(dfg_ref)
            loss_ref[...] = jnp.zeros_like(loss_ref)

        ycv, yhv = yc_ref[...], yh_ref[...]
        h1 = (x_ref[...] + jnp.dot(ycv, wo[0:D, :], preferred_element_type=F32)
              + jnp.dot(yhv, wo[D:2 * D, :], preferred_element_type=F32))
        pb = p_ref[...].astype(BF16)
        pe = jnp.dot(pb, wp[...], preferred_element_type=F32)
        rstd1 = lax.rsqrt(jnp.mean(h1 * h1, axis=-1, keepdims=True) + EPS)
        n1 = h1 * rstd1
        rb = (n1 * pg_ref[...]).astype(BF16)
        gate = _sigmoid(jnp.dot(rb, wg[...], preferred_element_type=F32))
        h2 = h1 + gate * pe
        rstd2 = lax.rsqrt(jnp.mean(h2 * h2, axis=-1, keepdims=True) + EPS)
        n2 = h2 * rstd2
        err = n2 * fg_ref[...] - tg_ref[...]
        loss_ref[...] += _rowsum8(err * err)

        d_out = err * (1.0 / D)
        dfg_ref[...] += _rowsum8(d_out * n2)
        d_h2 = _rms_bwd(d_out * fg_ref[...], n2, rstd2)
        d_pe = (d_h2 * gate).astype(BF16)
        d_gpre = (d_h2 * pe * gate * (1.0 - gate)).astype(BF16)
        dwg[...] += _dot_tn(rb, d_gpre)
        dwp[...] += _dot_tn(pb, d_pe)
        dr = _dot_nt(d_gpre, wg[...])
        dpg_ref[...] += _rowsum8(dr * n1)
        d_h1 = d_h2 + _rms_bwd(dr * pg_ref[...], n1, rstd1)
        dh1_ref[...] = d_h1
        d_h1b = d_h1.astype(BF16)
        dwo[0:D, :] += _dot_tn(ycv, d_h1b)
        dwo[D:2 * D, :] += _dot_tn(yhv, d_h1b)
        dyc_ref[...] = _dot_nt(d_h1b, wo[0:D, :])
        dyh_ref[...] = _dot_nt(d_h1b, wo[D:2 * D, :])

        @pl.when(i == n_steps - 1)
        def _():
            pltpu.sync_copy(dwo, dwo_hbm)
            pltpu.sync_copy(dwg, dwg_hbm)
            for d in range(N_DEV):
                pltpu.sync_copy(dwp.at[:, pl.ds(HEAD * d, HEAD)], dwp_hbm.at[d])

    row = pl.BlockSpec((tt, D), lambda i: (i, 0))
    acc = _full((SUBLANES, D))
    return pl.pallas_call(
        body, name="tail_fwd_bwd", grid=(n_steps,),
        out_shape=[jax.ShapeDtypeStruct((t, D), F32)] * 3
        + [jax.ShapeDtypeStruct((2 * D, D), F32), jax.ShapeDtypeStruct((D, D), F32),
           jax.ShapeDtypeStruct((N_DEV, PLE, HEAD), F32)]
        + [jax.ShapeDtypeStruct((SUBLANES, D), F32)] * 3,
        in_specs=[row, row, row, pl.BlockSpec((tt, PLE), lambda i: (i, 0)), row,
                  ANY, ANY, ANY, _full((1, D)), _full((1, D))],
        out_specs=[row, row, row, ANY, ANY, ANY, acc, acc, acc],
        scratch_shapes=[pltpu.VMEM((2 * D, D), BF16), pltpu.VMEM((D, D), BF16),
                        pltpu.VMEM((PLE, D), BF16), pltpu.VMEM((2 * D, D), F32),
                        pltpu.VMEM((D, D), F32), pltpu.VMEM((PLE, D), F32)],
        compiler_params=_params(52, dimension_semantics=("arbitrary",)),
    )(x, y_conv, y_hgrn, p, target, w_out, w_pg, w_pp_all, pe_g, fin_g)


def _hgrn_bwd(dy, z, o_raw, states, lb_logits, onorm_g, grads):
    t = z.shape[0]
    tt = min(256, t)
    nc = tt // CHUNK
    n_steps = t // tt
    modes = ["scatter"] * len(grads)

    def body(dy_ref, q_ref, f_ref, i_ref, g_ref, o_ref, s_ref, lbl_ref, on_ref,
             dz_ref, don_ref, dlb_ref, dst):
        @pl.when(pl.program_id(0) == 0)
        def _():
            dst[...] = jnp.zeros_like(dst)
            don_ref[...] = jnp.zeros_like(don_ref)
            dlb_ref[...] = jnp.zeros_like(dlb_ref)

        lb, _ = _lower_bound(lbl_ref[...])
        rows = lax.broadcasted_iota(jnp.int32, (CHUNK, CHUNK), 0)
        cols = lax.broadcasted_iota(jnp.int32, (CHUNK, CHUNK), 1)
        causal = rows >= cols
        tri = causal.astype(BF16)
        tri_rev = (rows <= cols).astype(BF16)
        width = HEAD * HEAD_GROUP
        is_last = lax.broadcasted_iota(jnp.int32, (CHUNK, width), 0) == CHUNK - 1
        nn = (((1,), (0,)), ((), ()))
        tn = (((0,), (0,)), ((), ()))
        dg = functools.partial(lax.dot_general, preferred_element_type=F32)

        def chunk(cc, carry):
            c = nc - 1 - cc
            r0 = pl.multiple_of(c * CHUNK, CHUNK)
            rs = pl.ds(r0, CHUNK)
            for h0 in range(0, N_HEADS, HEAD_GROUP):
                cs = slice(HEAD * h0, HEAD * h0 + width)
                zq, zf, zg = q_ref[rs, cs], f_ref[rs, cs], g_ref[rs, cs]
                lbh = lb[:, cs]
                q, k, f, sig, sig_neg, e_q, e_qm, e_km, e_kd, e_last = _chunk_quantities(
                    zq, zf, lbh, tri)
                vb = i_ref[rs, cs].astype(BF16)
                qt, qm, km, kd = q * e_q, q * e_qm, k * e_km, k * e_kd
                qt_b, kd_b = qt.astype(BF16), kd.astype(BF16)
                qm_h, qm_l = _split(qm)
                km_h, km_l = _split(km)

                o = o_ref[rs, cs]
                rstd = lax.rsqrt(_head_mean(o * o) + EPS)
                n = o * rstd
                sg = _sigmoid(zg)
                dyv = dy_ref[rs, cs]
                on = on_ref[:, cs]
                d_zg = dyv * n * on * sg * (1.0 + zg * (1.0 - sg))
                d_on = dyv * zg * sg
                don_ref[:, cs] += _rowsum8(d_on * n)
                dn = d_on * on
                do_b = (rstd * (dn - n * _head_mean(dn * n))).astype(BF16)

                dv, dkd, dqt, dqm, dkm, s_dots = [], [], [], [], [], []
                for j, hs in enumerate(_HEAD_LANES):
                    s_old, ds_new = s_ref[c, h0 + j], dst[h0 + j]
                    ds_b = ds_new.astype(BF16)
                    a = jnp.where(causal, _dot_nt(qm_h[:, hs], km_h[:, hs]), 0.0)
                    da = jnp.where(causal, _dot_nt(do_b[:, hs], vb[:, hs]), 0.0)
                    dv.append(_dot_tn(a, do_b[:, hs]) + _dot_nt(kd_b[:, hs], ds_b))
                    dkd.append(_dot(vb[:, hs], ds_b))
                    dqt.append(_dot(do_b[:, hs], s_old))
                    da_h, da_l = _split(da)
                    dqm.append(dg(da_h, km_h[:, hs], nn)
                               + (dg(da_h, km_l[:, hs], nn) + dg(da_l, km_h[:, hs], nn)))
                    dkm.append(dg(da_h, qm_h[:, hs], tn)
                               + (dg(da_h, qm_l[:, hs], tn) + dg(da_l, qm_h[:, hs], tn)))
                    dst[h0 + j] = ds_new * e_last[:, hs] + _dot_tn(do_b[:, hs], qt_b[:, hs])
                    s_dots.append(jnp.sum(s_old.astype(F32) * ds_new, axis=0, keepdims=True))
                dv, dkd, dqt, dqm, dkm, s_dots = [
                    jnp.concatenate(parts, axis=1) for parts in (dv, dkd, dqt, dqm, dkm, s_dots)]
                dq = dqt * e_q + dqm * e_qm
                dk = dkm * e_km + dkd * e_kd
                last = jnp.sum(dkd * kd, axis=0, keepdims=True) + e_last * s_dots
                db = q * dq - k * dk + jnp.where(is_last, last, 0.0)
                dlogf = _tri_dot(tri_rev, db)
                common = sig_neg * (dlogf / f - dk)
                dlb_ref[:, cs] += _rowsum8(common)
                c0 = 3 * D + HEAD * h0
                sq = _sigmoid(zq)
                dz_ref[rs, c0:c0 + width] = (dq * sq * (1.0 + zq * (1.0 - sq))).astype(BF16)
                dz_ref[rs, D + c0:D + c0 + width] = ((1.0 - lbh) * sig * common).astype(BF16)
                dz_ref[rs, 2 * D + c0:2 * D + c0 + width] = dv.astype(BF16)
                dz_ref[rs, 3 * D + c0:3 * D + c0 + width] = d_zg.astype(BF16)
            return carry

        lax.fori_loop(0, nc, chunk, 0, unroll=True)

    rev = lambda i: n_steps - 1 - i
    col = lambda j: pl.BlockSpec((tt, D), lambda i: (rev(i), j))
    row = pl.BlockSpec((tt, D), lambda i: (rev(i), 0))
    acc = _full((SUBLANES, D))
    n = len(modes)
    return pl.pallas_call(
        _hosted(body, 9, 3, (n_steps,), modes), name="hgrn_bwd", grid=(n_steps,),
        out_shape=[jax.ShapeDtypeStruct((t, N_COLS), BF16),
                   jax.ShapeDtypeStruct((SUBLANES, D), F32),
                   jax.ShapeDtypeStruct((SUBLANES, D), F32)] + _recv_shapes(grads, modes),
        in_specs=[row, col(3), col(4), col(5), col(6), row,
                  pl.BlockSpec((nc, N_HEADS, HEAD, HEAD), lambda i: (rev(i), 0, 0, 0)),
                  _full((2, D)), _full((1, D))] + [ANY] * n,
        out_specs=[pl.BlockSpec((tt, N_COLS), lambda i: (rev(i), 0)), acc, acc] + [ANY] * n,
        scratch_shapes=[pltpu.VMEM((N_HEADS, HEAD, HEAD), F32)] + _exchange_scratch(n),
        compiler_params=_params(48, dimension_semantics=("arbitrary",)),
    )(dy, z, z, z, z, o_raw, states, lb_logits, onorm_g, *grads)


def _conv_bwd(dy, z, yc, y2, conv_w_all, cn_g, cn_b, w_pw2, dz, grads, modes):
    t = z.shape[0]
    tt = min(256, t)
    rc = 32
    n_steps = t // tt

    def body(dy_ref, val_ref, glu_ref, gate_ref, yc_ref, y2_ref, cw_ref, g_ref, b_ref, w_hbm,
             dz_in, dz_ref, dw_hbm, dcw_out, db2_ref, dg_ref, dbeta_ref, dcb_ref,
             w_vmem, dw, dbuf, dsh, y1buf, dnbuf, dcw_ref):
        i = pl.program_id(0)

        @pl.when(i == 0)
        def _():
            pltpu.sync_copy(w_hbm, w_vmem)
            dw[...] = jnp.zeros_like(dw)
            dbuf[tt:tt + CONV_PAD, :] = jnp.zeros((CONV_PAD, D), F32)
            dcw_ref[...] = jnp.zeros_like(dcw_ref)
            dcw_out[...] = jnp.zeros_like(dcw_out)
            db2_ref[...] = jnp.zeros_like(db2_ref)
            dg_ref[...] = jnp.zeros_like(dg_ref)
            dbeta_ref[...] = jnp.zeros_like(dbeta_ref)
            dcb_ref[...] = jnp.zeros_like(dcb_ref)

        gate = gate_ref[...]
        sg = _sigmoid(gate)
        dyv = dy_ref[...]
        dy2 = dyv * gate * sg
        dz_ref[:, 2 * D:3 * D] = (dyv * y2_ref[...] * sg * (1.0 + gate * (1.0 - sg))).astype(BF16)
        db2_ref[...] += _rowsum8(dy2)
        dy2b = dy2.astype(BF16)
        dnbuf[...] = _dot_nt(dy2b, w_vmem[...])

        def norm_chunk(r, carry):
            r0 = pl.multiple_of(r * rc, rc)
            rs = pl.ds(r0, rc)
            for g in range(N_HEADS):
                cs = slice(HEAD * g, HEAD * (g + 1))
                blk = yc_ref[rs, cs]
                mu = jnp.mean(blk, axis=-1, keepdims=True)
                cen = blk - mu
                rstd = lax.rsqrt(jnp.mean(cen * cen, axis=-1, keepdims=True) + EPS)
                xhat = cen * rstd
                n = xhat * g_ref[:, cs] + b_ref[:, cs]
                sn = _sigmoid(n)
                y1buf[rs, cs] = (n * sn).astype(BF16)
                dn = dnbuf[rs, cs] * sn * (1.0 + n * (1.0 - sn))
                dg_ref[:, cs] += _rowsum8(dn * xhat)
                dbeta_ref[:, cs] += _rowsum8(dn)
                dxh = dn * g_ref[:, cs]
                dyc = rstd * (dxh - jnp.mean(dxh, axis=-1, keepdims=True)
                              - xhat * jnp.mean(dxh * xhat, axis=-1, keepdims=True))
                dcb_ref[:, cs] += _rowsum8(dyc)
                dbuf[rs, cs] = dyc
            return carry

        lax.fori_loop(0, tt // rc, norm_chunk, 0, unroll=True)
        dw[...] += _dot_tn(y1buf[...], dy2b)
        _shifted_copies(dbuf, dsh, tt + 24)

        def conv_chunk(r, carry):
            r0 = pl.multiple_of(r * rc, rc)
            rs = pl.ds(r0, rc)
            for g in range(N_HEADS):
                cs = slice(HEAD * g, HEAD * (g + 1))
                sglu = _sigmoid(glu_ref[rs, cs])
                val = val_ref[rs, cs]
                v = val * sglu
                dv = jnp.zeros((rc, HEAD), F32)
                for ref, lo, taps in _tap_slabs(dbuf, dsh, lambda k: CONV_K - 1 - k):
                    slab = ref[pl.ds(r0 + lo, rc + taps[-1][1]), cs]
                    for k, off in taps:
                        d_later = slab[off:off + rc]
                        dv = dv + cw_ref[g, k:k + 1, :] * d_later
                        dcw_ref[g, k] += _rowsum8(v * d_later)
                dz_ref[rs, cs] = (dv * sglu).astype(BF16)
                dz_ref[rs, D + HEAD * g:D + HEAD * (g + 1)] = (
                    dv * val * sglu * (1.0 - sglu)).astype(BF16)
            return carry

        lax.fori_loop(0, tt // rc, conv_chunk, 0, unroll=2)
        dbuf[tt:tt + CONV_PAD, :] = dbuf[0:CONV_PAD, :]

        @pl.when(i == n_steps - 1)
        def _():
            pltpu.sync_copy(dw, dw_hbm)
            for g in range(N_HEADS):
                for k in range(CONV_K):
                    dcw_out[g, k:k + 1, :] = jnp.sum(dcw_ref[g, k], axis=0, keepdims=True)

    rev = lambda i: n_steps - 1 - i
    col = lambda j: pl.BlockSpec((tt, D), lambda i: (rev(i), j))
    row = pl.BlockSpec((tt, D), lambda i: (rev(i), 0))
    acc = _full((SUBLANES, D))
    n = len(modes)
    return pl.pallas_call(
        _hosted(body, 11, 7, (n_steps,), modes), name="conv_bwd", grid=(n_steps,),
        out_shape=[jax.ShapeDtypeStruct((t, N_COLS), BF16), jax.ShapeDtypeStruct((D, D), F32),
                   jax.ShapeDtypeStruct((N_DEV, CONV_PAD, HEAD), F32)]
        + [jax.ShapeDtypeStruct((SUBLANES, D), F32)] * 4 + _recv_shapes(grads, modes),
        in_specs=[row, col(0), col(1), col(2), row, row, _full((N_DEV, CONV_PAD, HEAD)),
                  _full((1, D)), _full((1, D)), ANY, ANY] + [ANY] * n,
        out_specs=[pl.BlockSpec((tt, 3 * D), lambda i: (rev(i), 0)), ANY,
                   _full((N_DEV, CONV_PAD, HEAD)), acc, acc, acc, acc] + [ANY] * n,
        input_output_aliases={10: 0},
        scratch_shapes=[pltpu.VMEM((D, D), BF16), pltpu.VMEM((D, D), F32),
                        pltpu.VMEM((tt + CONV_PAD, D), F32),
                        pltpu.VMEM((SUBLANES, tt + CONV_PAD, D), F32),
                        pltpu.VMEM((tt, D), BF16), pltpu.VMEM((tt, D), F32),
                        pltpu.VMEM((N_DEV, CONV_PAD, SUBLANES, HEAD), F32)] + _exchange_scratch(n),
        compiler_params=_params(52, dimension_semantics=("arbitrary",)),
    )(dy, z, z, z, yc, y2, conv_w_all, cn_g, cn_b, w_pw2, dz, *grads)


def _inproj_bwd_dx(dz, x, d_h1, ln_g, w_in_all, grads, modes):
    t = x.shape[0]
    tt = min(256, t)

    def body(dz_ref, x_ref, dh1_ref, g_ref, w_hbm, dx_ref, dg_ref, w_vmem):
        @pl.when(pl.program_id(0) == 0)
        def _():
            for d in range(N_DEV):
                pltpu.sync_copy(w_hbm.at[d], w_vmem.at[
                    d // 2, :, pl.ds(COLS_PER_DEV * (d % 2), COLS_PER_DEV)])
            dg_ref[...] = jnp.zeros_like(dg_ref)

        du = jnp.zeros((tt, D), F32)
        for q in range(N_CHIPS):
            du = du + lax.dot_general(
                dz_ref[:, PAIR_COLS * q:PAIR_COLS * (q + 1)], w_vmem[q],
                (((1,), (1,)), ((), ())), preferred_element_type=F32)
        xv = x_ref[...]
        rstd = lax.rsqrt(jnp.mean(xv * xv, axis=-1, keepdims=True) + EPS)
        xhat = xv * rstd
        dg_ref[...] += _rowsum8(du * xhat)
        dx_ref[...] = dh1_ref[...] + _rms_bwd(du * g_ref[...], xhat, rstd)

    row = pl.BlockSpec((tt, D), lambda i: (i, 0))
    n = len(modes)
    return pl.pallas_call(
        _hosted(body, 5, 2, (t // tt,), modes), name="inproj_bwd_dx", grid=(t // tt,),
        out_shape=[jax.ShapeDtypeStruct((t, D), F32), jax.ShapeDtypeStruct((SUBLANES, D), F32)]
        + _recv_shapes(grads, modes),
        in_specs=[pl.BlockSpec((tt, N_COLS), lambda i: (i, 0)), row, row, _full((1, D)), ANY]
        + [ANY] * n,
        out_specs=[row, _full((SUBLANES, D))] + [ANY] * n,
        scratch_shapes=[pltpu.VMEM((N_CHIPS, D, PAIR_COLS), BF16)] + _exchange_scratch(n),
        compiler_params=_params(48, dimension_semantics=("arbitrary",)),
    )(dz, x, d_h1, ln_g, w_in_all, *grads)


def _inproj_bwd_dw(name, u, dz, first, count, grads=(), modes=()):
    t = u.shape[0]
    tt = min(2048, t)
    grid = (count // 2, t // tt)
    n = len(modes)

    def body(u_ref, dz_ref, dw_ref):
        @pl.when(pl.program_id(1) == 0)
        def _():
            dw_ref[...] = jnp.zeros_like(dw_ref)

        both = lax.dot_general(u_ref[...], dz_ref[...], (((0,), (0,)), ((), ())),
                               preferred_element_type=F32)
        dw_ref[0] += both[:, :COLS_PER_DEV]
        dw_ref[1] += both[:, COLS_PER_DEV:]

    return pl.pallas_call(
        _hosted(body, 2, 1, grid, modes) if n else body, name=name, grid=grid,
        out_shape=[jax.ShapeDtypeStruct((count, D, COLS_PER_DEV), F32)]
        + _recv_shapes(grads, modes),
        in_specs=[pl.BlockSpec((tt, D), lambda j, i: (i, 0)),
                  pl.BlockSpec((tt, PAIR_COLS), lambda j, i: (i, first // 2 + j))] + [ANY] * n,
        out_specs=[pl.BlockSpec((2, D, COLS_PER_DEV), lambda j, i: (j, 0, 0))] + [ANY] * n,
        scratch_shapes=_exchange_scratch(n) if n else [],
        compiler_params=_params(56, dimension_semantics=("arbitrary", "arbitrary")),
    )(u, dz, *grads)


def _adamw(w, g, m, v):
    m = ADAM_B1 * m + (1.0 - ADAM_B1) * g
    v = ADAM_B2 * v + (1.0 - ADAM_B2) * (g * g)
    m_hat = m / (1.0 - ADAM_B1 ** ADAM_STEP)
    v_hat = v / (1.0 - ADAM_B2 ** ADAM_STEP)
    delta = -ADAM_LR * (m_hat / (jnp.sqrt(v_hat) + ADAM_EPS) + ADAM_WD * w)
    return delta, m, v


def _pack_small(partials):
    rows = sorted(partials)

    def body(*refs):
        ins, out_ref = refs[:-1], refs[-1]
        out_ref[...] = jnp.zeros_like(out_ref)
        for j, row in enumerate(rows):
            out_ref[row:row + 1, :] = jnp.sum(ins[j][...], axis=0, keepdims=True)

    return pl.pallas_call(
        body, name="pack_small", out_shape=jax.ShapeDtypeStruct((N_SMALL, D), F32),
    )(*[partials[row] for row in rows])


def _sum_adam(name, recvs, w, m, v, rows):
    r, c = w.shape
    n = len(recvs)
    side = lax.axis_index("x").astype(jnp.int32).reshape(1)

    def body(side_ref, *refs):
        w_ref, m_ref, v_ref, g_ref, d_ref, mo_ref, vo_ref = refs[n:]

        def finish(recv_ref):
            g = recv_ref[0]
            for s in range(1, recv_ref.shape[0]):
                g = g + recv_ref[s]
            g_ref[...] = g
            d_ref[...], mo_ref[...], vo_ref[...] = _adamw(w_ref[...], g, m_ref[...], v_ref[...])

        if n == 1:
            finish(refs[0])
        else:
            for s in range(n):
                pl.when(side_ref[0] == s)(functools.partial(finish, refs[s]))

    def recv_spec(s, rv):
        if n == 1:
            return pl.BlockSpec((rv.shape[0], rows, c), lambda i, side_ref: (0, i, 0))
        return pl.BlockSpec((rv.shape[0], rows, c),
                            lambda i, side_ref: (0, jnp.where(side_ref[0] == s, i, 0), 0))

    blk = pl.BlockSpec((rows, c), lambda i, side_ref: (i, 0))
    grid_spec = pltpu.PrefetchScalarGridSpec(
        num_scalar_prefetch=1, grid=(r // rows,),
        in_specs=[recv_spec(s, rv) for s, rv in enumerate(recvs)] + [blk, blk, blk],
        out_specs=[blk] * 4)
    return pl.pallas_call(
        body, name=name, grid_spec=grid_spec,
        out_shape=[jax.ShapeDtypeStruct((r, c), F32)] * 4,
        compiler_params=_params(48, dimension_semantics=("arbitrary",)),
    )(side, *recvs, w, m, v)


def _small_adam(gathered, lb_logits, w, m, v):
    def body(ga_ref, lbl_ref, w_ref, m_ref, v_ref, g_ref, d_ref, mo_ref, vo_ref, loss_ref):
        g = ga_ref[0]
        for s in range(1, N_DEV):
            g = g + ga_ref[s]
        s0, s1 = _lower_bound(lbl_ref[...])
        d_lb = g[R_LB0:R_LB0 + 1, :]
        rows = lax.broadcasted_iota(jnp.int32, (N_SMALL, D), 0)
        g = jnp.where(rows == R_LB0, d_lb * s0 * (1.0 - s0), g)
        g = jnp.where(rows == R_LB1, -d_lb * s0 * s1, g)
        g_ref[...] = g
        d_ref[...], mo_ref[...], vo_ref[...] = _adamw(w_ref[...], g, m_ref[...], v_ref[...])
        loss_ref[...] = (0.5 / D) * jnp.sum(g[R_LOSS:R_LOSS + 1, :], axis=-1, keepdims=True)

    return pl.pallas_call(
        body, name="small_adam",
        out_shape=[jax.ShapeDtypeStruct((N_SMALL, D), F32)] * 4 + [jax.ShapeDtypeStruct((1, 1), F32)],
    )(gathered, lb_logits, w, m, v)


def _pad_rows(a, rows):
    return jnp.pad(a, ((0, rows - a.shape[0]), (0, 0)))


def _pack_rows(rows):
    rows = [r.reshape(-1, D) for r in rows]
    packed = jnp.concatenate(rows, axis=0)
    return _pad_rows(packed, N_SMALL)


def kernel(x, p, ln_g, w_in, conv_w, conv_b, cnorm_g, cnorm_b, w_pw2, b_pw2, lb_logits, onorm_g, w_out, pe_norm_g, w_pg, w_pp, final_g, loss_target, m_ln_g, m_w_in, m_conv_w, m_conv_b, m_cnorm_g, m_cnorm_b, m_w_pw2, m_b_pw2, m_lb_logits, m_onorm_g, m_w_out, m_pe_norm_g, m_w_pg, m_w_pp, m_final_g, v_ln_g, v_w_in, v_conv_w, v_conv_b, v_cnorm_g, v_cnorm_b, v_w_pw2, v_b_pw2, v_lb_logits, v_onorm_g, v_w_out, v_pe_norm_g, v_w_pg, v_w_pp, v_final_g):
    t = x.shape[1]
    x2 = x.reshape(t, D)
    p2 = p.reshape(t, PLE)
    tg2 = loss_target.reshape(t, D)
    fin_g = final_g.reshape(1, D)

    z, u, w_in_all = _inproj_fwd(x2, ln_g, w_in[0].astype(BF16))
    (o_raw, y_hgrn, states, conv_w_all, w_pw2_all, w_out_all, w_pg_all, w_pp_all) = _hgrn_fwd(
        z, lb_logits, onorm_g,
        [_pad_rows(conv_w[0], CONV_PAD), w_pw2[0].astype(BF16), w_out[0].astype(BF16),
         w_pg[0].astype(BF16), w_pp[0].astype(BF16)])
    w_pw2_full = w_pw2_all.reshape(D, D)
    w_out_full = w_out_all.reshape(2 * D, D)
    w_pg_full = w_pg_all.reshape(D, D)
    yc, y2, y_conv = _conv_fwd(z, conv_w_all, conv_b, cnorm_g, cnorm_b, w_pw2_full, b_pw2)

    (d_h1, dy_conv, dy_hgrn, d_w_out, d_w_pg, d_w_pp, d_pen_p, d_fin_p, loss_p) = _tail(
        x2, y_conv, y_hgrn, p2, tg2, w_out_full, w_pg_full, w_pp_all, pe_norm_g, fin_g)

    dz, d_on_p, d_lb_p, r_w_out, r_w_pg, r_w_pp = _hgrn_bwd(
        dy_hgrn, z, o_raw, states, lb_logits, onorm_g,
        [d_w_out.reshape(N_DEV, 2 * D // N_DEV, D), d_w_pg.reshape(N_DEV, D // N_DEV, D), d_w_pp])
    (d_w_in_hi,) = _inproj_bwd_dw("inproj_bwd_dw_hi", u, dz, N_DEV // 2, N_DEV // 2)
    dz, d_w_pw2, d_conv_w, d_b2_p, d_cng_p, d_cnb_p, d_cb_p, r_w_in_hi = _conv_bwd(
        dy_conv, z, yc, y2, conv_w_all, cnorm_g, cnorm_b, w_pw2_full, dz, [d_w_in_hi], [1])
    d_w_in_lo, r_w_pw2, r_conv_w = _inproj_bwd_dw(
        "inproj_bwd_dw_lo", u, dz, 0, N_DEV // 2,
        [d_w_pw2.reshape(N_DEV, D // N_DEV, D), d_conv_w], ["scatter", "scatter"])
    chip_lo = _pair_reduce("pair_reduce_lo", d_w_in_lo)
    grad_x, d_ln_p, r_w_in_lo = _inproj_bwd_dx(
        dz, x2, d_h1, ln_g, w_in_all, [chip_lo], [("chip", 0)])

    small = _pack_small({R_LN: d_ln_p, R_CONVB: d_cb_p, R_CNG: d_cng_p, R_CNB: d_cnb_p,
                         R_BPW2: d_b2_p, R_LB0: d_lb_p, R_ON: d_on_p, R_PEN: d_pen_p,
                         R_FIN: d_fin_p, R_LOSS: loss_p})
    (small_all,) = _exchange_call("gather_small", [small], ["gather"])

    big = {}
    big["w_in"] = _sum_adam("adam_w_in", [r_w_in_lo, r_w_in_hi], w_in[0], m_w_in[0], v_w_in[0], 128)
    cw = _sum_adam("adam_conv_w", [r_conv_w], _pad_rows(conv_w[0], CONV_PAD),
                   _pad_rows(m_conv_w[0], CONV_PAD), _pad_rows(v_conv_w[0], CONV_PAD), CONV_PAD)
    big["conv_w"] = [a[:CONV_K] for a in cw]
    big["w_pw2"] = _sum_adam("adam_w_pw2", [r_w_pw2], w_pw2[0], m_w_pw2[0], v_w_pw2[0], 128)
    big["w_out"] = _sum_adam("adam_w_out", [r_w_out], w_out[0], m_w_out[0], v_w_out[0], 128)
    big["w_pg"] = _sum_adam("adam_w_pg", [r_w_pg], w_pg[0], m_w_pg[0], v_w_pg[0], 128)
    big["w_pp"] = _sum_adam("adam_w_pp", [r_w_pp], w_pp[0], m_w_pp[0], v_w_pp[0], PLE)

    small_w = [ln_g, conv_b, cnorm_g, cnorm_b, b_pw2, lb_logits, onorm_g, pe_norm_g, final_g]
    small_m = [m_ln_g, m_conv_b, m_cnorm_g, m_cnorm_b, m_b_pw2, m_lb_logits, m_onorm_g,
               m_pe_norm_g, m_final_g]
    small_v = [v_ln_g, v_conv_b, v_cnorm_g, v_cnorm_b, v_b_pw2, v_lb_logits, v_onorm_g,
               v_pe_norm_g, v_final_g]
    sg, sd, sm, sv, loss = _small_adam(small_all, lb_logits, _pack_rows(small_w),
                                       _pack_rows(small_m), _pack_rows(small_v))

    small_rows = {"ln_g": (R_LN, 1), "conv_b": (R_CONVB, 1), "cnorm_g": (R_CNG, 1),
                  "cnorm_b": (R_CNB, 1), "b_pw2": (R_BPW2, 1), "lb_logits": (R_LB0, 2),
                  "onorm_g": (R_ON, 1), "pe_norm_g": (R_PEN, 1), "final_g": (R_FIN, 1)}
    order = ["ln_g", "w_in", "conv_w", "conv_b", "cnorm_g", "cnorm_b", "w_pw2", "b_pw2",
             "lb_logits", "onorm_g", "w_out", "pe_norm_g", "w_pg", "w_pp", "final_g"]

    def leaf(kind, name):
        if name in big:
            return big[name][kind][None]
        r0, n = small_rows[name]
        a = (sg, sd, sm, sv)[kind][r0:r0 + n]
        return a.reshape(D) if name == "final_g" else a

    outs = [loss.reshape(()), grad_x.reshape(1, t, D)]
    for kind in range(4):
        outs += [leaf(kind, name) for name in order]
    return tuple(outs)
```

```python
import functools

import jax
import jax.numpy as jnp
from jax import lax
from jax.experimental import pallas as pl
from jax.experimental.pallas import tpu as pltpu

F32 = jnp.float32
BF16 = jnp.bfloat16
MESH = pl.DeviceIdType.MESH

N_DEV = 8
D = 1024
N_COLS = 7 * D
COLS_PER_DEV = N_COLS // N_DEV
PLE = 256
HEAD = 128
N_HEADS = D // HEAD
CONV_K = 31
CONV_PAD = 32
CHUNK = 64
EPS = 1e-6
SUBLANES = 8

ADAM_LR = 0.001
ADAM_B1 = 0.9
ADAM_B2 = 0.999
ADAM_EPS = 1e-08
ADAM_WD = 0.01
ADAM_STEP = 10

MIB = 1024 * 1024
N_SMALL = 16
R_LN, R_CONVB, R_CNG, R_CNB, R_BPW2, R_LB0, R_LB1, R_ON, R_PEN, R_FIN, R_LOSS = range(11)


def _params(vmem_mib, **kw):
    return pltpu.CompilerParams(vmem_limit_bytes=vmem_mib * MIB, **kw)


def _dot(a, b):
    return jnp.dot(a.astype(BF16), b.astype(BF16), preferred_element_type=F32)


def _dot_nt(a, b):
    return lax.dot_general(a.astype(BF16), b.astype(BF16), (((1,), (1,)), ((), ())),
                           preferred_element_type=F32)


def _dot_tn(a, b):
    return lax.dot_general(a.astype(BF16), b.astype(BF16), (((0,), (0,)), ((), ())),
                           preferred_element_type=F32)


def _split(a):
    hi = a.astype(BF16)
    return hi, (a - hi.astype(F32)).astype(BF16)


def _sigmoid(x):
    return 1.0 / (1.0 + jnp.exp(-x))


def _rowsum8(a):
    r, c = a.shape
    return jnp.sum(a.reshape(r // SUBLANES, SUBLANES, c), axis=0)


def _tri_dot(tri, a):
    hi = a.astype(BF16)
    r1 = a - hi.astype(F32)
    mid = r1.astype(BF16)
    lo = (r1 - mid.astype(F32)).astype(BF16)
    return (jnp.dot(tri, hi, preferred_element_type=F32)
            + jnp.dot(tri, mid, preferred_element_type=F32)
            + jnp.dot(tri, lo, preferred_element_type=F32))


def _lower_bound(lbl):
    l0, l1 = lbl[0:1, :], lbl[1:2, :]
    m = jnp.maximum(l0, l1)
    e0, e1 = jnp.exp(l0 - m), jnp.exp(l1 - m)
    s = e0 + e1
    return e0 / s, e1 / s


ANY = pl.BlockSpec(memory_space=pl.ANY)


def _full(shape):
    return pl.BlockSpec(shape, lambda i: (0,) * len(shape))


def _peer(x, y, c, k):
    px = 1 - x if k & 4 else x
    py = 1 - y if k & 2 else y
    pc = 1 - c if k & 1 else c
    return (px, py, pc), 4 * px + 2 * py + pc


class _Exchange:
    def __init__(self, srcs, outs, modes, send_sems, recv_sems, local_sems):
        x, y, c = lax.axis_index("x"), lax.axis_index("y"), lax.axis_index("c")
        me = 4 * x + 2 * y + c
        self.starts, self.send_waits, self.recv_waits = [], [], []

        def remote(a, k, src, slot, peer, when):
            sem = a * N_DEV + k
            cp = pltpu.make_async_remote_copy(
                src_ref=src, dst_ref=outs[a].at[slot], send_sem=send_sems.at[sem],
                recv_sem=recv_sems.at[sem], device_id=peer, device_id_type=MESH)
            self.starts.append((when, cp.start))
            self.send_waits.append((when, cp.wait_send))

        def arrival(a, k, slot, when):
            sem = a * N_DEV + k
            cp = pltpu.make_async_remote_copy(
                src_ref=outs[a].at[slot], dst_ref=outs[a].at[slot], send_sem=send_sems.at[sem],
                recv_sem=recv_sems.at[sem], device_id=(x, y, c), device_id_type=MESH)
            self.recv_waits.append((when, cp.wait_recv))

        def local(a, src, slot, when):
            cp = pltpu.make_async_copy(src, outs[a].at[slot], local_sems.at[a])
            self.starts.append((when, cp.start))
            self.send_waits.append((when, cp.wait))

        for a, (src, mode) in enumerate(zip(srcs, modes)):
            if mode in ("gather", "scatter"):
                local(a, src if mode == "gather" else src.at[me], me, None)
                for k in range(1, N_DEV):
                    peer, peer_idx = _peer(x, y, c, k)
                    remote(a, k, src if mode == "gather" else src.at[peer_idx], me, peer, None)
                    arrival(a, k, peer_idx, None)
                continue
            if isinstance(mode, tuple):
                here, away = x == mode[1], x != mode[1]
                chip = 2 * x + y
                local(a, src.at[y], chip, here)
                remote(a, 1, src.at[1 - y], chip, (x, 1 - y, c), here)
                remote(a, 2, src.at[y], chip, (1 - x, y, c), away)
                remote(a, 3, src.at[1 - y], chip, (1 - x, 1 - y, c), away)
                arrival(a, 1, 2 * x + 1 - y, here)
                arrival(a, 2, 2 * (1 - x) + y, here)
                arrival(a, 3, 2 * (1 - x) + 1 - y, here)
                continue
            here, away = x == mode, x != mode
            for kk in range(4):
                py = 1 - y if kk & 2 else y
                pc = 1 - c if kk & 1 else c
                block = src.at[2 * py + pc]
                if kk == 0:
                    local(a, block, me, here)
                else:
                    remote(a, kk, block, me, (x, py, pc), here)
                remote(a, 4 + kk, block, me, (1 - x, py, pc), away)
            for k in range(1, N_DEV):
                arrival(a, k, _peer(x, y, c, k)[1], here)

    @staticmethod
    def _run(actions):
        for when, fn in actions:
            if when is None:
                fn()
            else:
                pl.when(when)(fn)

    def start(self):
        self._run(self.starts)

    def wait(self):
        self._run(self.recv_waits)
        self._run(self.send_waits)


def _exchange_scratch(n):
    return [pltpu.SemaphoreType.DMA((n * N_DEV,)), pltpu.SemaphoreType.DMA((n * N_DEV,)),
            pltpu.SemaphoreType.DMA((n,))]


def _recv_shapes(srcs, modes):
    def shape(s, m):
        if m == "gather":
            return (N_DEV,) + s.shape
        return (N_DEV // 2 if isinstance(m, tuple) else N_DEV,) + s.shape[1:]

    return [jax.ShapeDtypeStruct(shape(s, m), s.dtype) for s, m in zip(srcs, modes)]


def _pair_reduce(name, blocks):
    shape = (2,) + blocks.shape[1:]

    def body(src, out_ref, stage, mine, send_sems, recv_sems, local_sems):
        x, y, c = lax.axis_index("x"), lax.axis_index("y"), lax.axis_index("c")
        sends, waits = [], []
        for py in range(2):
            sends.append(pltpu.make_async_remote_copy(
                src_ref=src.at[2 * py + 1 - c], dst_ref=stage.at[py], send_sem=send_sems.at[py],
                recv_sem=recv_sems.at[py], device_id=(x, y, 1 - c), device_id_type=MESH))
            waits.append(pltpu.make_async_copy(src.at[2 * py + c], mine.at[py], local_sems.at[py]))
        for cp in sends + waits:
            cp.start()
        for cp in waits:
            cp.wait()
        for cp in sends:
            cp.wait_recv()
        out_ref[...] = mine[...] + stage[...]
        for cp in sends:
            cp.wait_send()

    return pl.pallas_call(
        body, name=name, out_shape=jax.ShapeDtypeStruct(shape, F32), in_specs=[ANY],
        scratch_shapes=[pltpu.VMEM(shape, F32), pltpu.VMEM(shape, F32),
                        pltpu.SemaphoreType.DMA((2,)), pltpu.SemaphoreType.DMA((2,)),
                        pltpu.SemaphoreType.DMA((2,))],
        compiler_params=_params(40),
    )(blocks)


def _exchange_call(name, srcs, modes):
    n = len(srcs)

    def body(*refs):
        xch = _Exchange(refs[:n], refs[n:2 * n], modes, *refs[2 * n:])
        xch.start()
        xch.wait()

    return pl.pallas_call(
        body, name=name, out_shape=_recv_shapes(srcs, modes),
        in_specs=[ANY] * n, out_specs=[ANY] * n, scratch_shapes=_exchange_scratch(n),
    )(*srcs)


def _hosted(body, n_in, n_out, grid, modes):
    n = len(modes)

    def hosted(*refs):
        ins, srcs = refs[:n_in], refs[n_in:n_in + n]
        outs = refs[n_in + n:n_in + n + n_out]
        bufs = refs[n_in + n + n_out:n_in + 2 * n + n_out]
        scratch = refs[n_in + 2 * n + n_out:-3]
        xch = _Exchange(srcs, bufs, modes, *refs[-3:])
        first, last = True, True
        for axis, size in enumerate(grid):
            first = jnp.logical_and(first, pl.program_id(axis) == 0)
            last = jnp.logical_and(last, pl.program_id(axis) == size - 1)
        pl.when(first)(xch.start)
        body(*ins, *outs, *scratch)
        pl.when(last)(xch.wait)

    return hosted


N_CHIPS = N_DEV // 2
PAIR_COLS = 2 * COLS_PER_DEV
PUSHED = (1, 2, 4, 6)
FORWARDED = (2, 4, 6)
NORM_ROWS = 32
Z_SLOTS = 3


def _inproj_fwd(x, ln_g, w_shard):
    t = x.shape[0]
    tt = min(512, t)
    n_t = t // tt
    chip = 2 * lax.axis_index("x") + lax.axis_index("y")
    order = jnp.bitwise_xor(chip, jnp.arange(N_CHIPS, dtype=jnp.int32)).astype(jnp.int32)

    def body(order_ref, x_ref, g_ref, shard_hbm, z_hbm, u_ref, w_all,
             u_all, w_blk, w_send, w_recv, w_local, z_buf, z_sems):
        p, i = pl.program_id(0), pl.program_id(1)
        x, y, c = lax.axis_index("x"), lax.axis_index("y"), lax.axis_index("c")
        mine = 4 * x + 2 * y + c

        def push(k):
            peer, _ = _peer(x, y, c, k)
            return pltpu.make_async_remote_copy(
                src_ref=shard_hbm, dst_ref=w_all.at[mine], send_sem=w_send.at[k],
                recv_sem=w_recv.at[k], device_id=peer, device_id_type=MESH)

        def forward(k):
            _, owner = _peer(x, y, c, k)
            return pltpu.make_async_remote_copy(
                src_ref=w_all.at[owner], dst_ref=w_all.at[owner], send_sem=w_send.at[k + 1],
                recv_sem=w_recv.at[k + 1], device_id=(x, y, 1 - c), device_id_type=MESH)

        def landed(k):
            _, owner = _peer(x, y, c, k)
            return pltpu.make_async_remote_copy(
                src_ref=w_all.at[owner], dst_ref=w_all.at[owner], send_sem=w_send.at[k],
                recv_sem=w_recv.at[k], device_id=(x, y, c), device_id_type=MESH)

        keep = pltpu.make_async_copy(shard_hbm, w_all.at[mine], w_local.at[0])

        def load_pair(step):
            same = shard_hbm if step == 0 else w_all.at[_peer(x, y, c, 2 * step)[1]]
            other = w_all.at[_peer(x, y, c, 2 * step + 1)[1]]
            for side in range(2):
                @pl.when(c == side)
                def _(side=side):
                    pltpu.sync_copy(same, w_blk.at[:, pl.ds(COLS_PER_DEV * side, COLS_PER_DEV)])
                    pltpu.sync_copy(
                        other, w_blk.at[:, pl.ds(COLS_PER_DEV * (1 - side), COLS_PER_DEV)])

        @pl.when(jnp.logical_and(p == 0, i == 0))
        def _():
            for k in PUSHED[:-1]:
                push(k).start()
            keep.start()

        @pl.when(jnp.logical_and(p == 1, i == 0))
        def _():
            for k in PUSHED[1:-1]:
                push(k).wait_send()
            push(PUSHED[-1]).start()

        for step in range(N_CHIPS):
            @pl.when(jnp.logical_and(p == step, i == 0))
            def _(step=step):
                landed(2 * step + 1).wait_recv()
                load_pair(step)

        rows = pl.ds(pl.multiple_of(i * tt, tt), tt)

        @pl.when(p == 0)
        def _():
            def norm_rows(r, carry):
                sub = pl.ds(pl.multiple_of(r * NORM_ROWS, NORM_ROWS), NORM_ROWS)
                xv = x_ref[sub, :]
                rstd = lax.rsqrt(jnp.mean(xv * xv, axis=-1, keepdims=True) + EPS)
                ub = (xv * rstd * g_ref[...]).astype(BF16)
                u_ref[sub, :] = ub
                u_all[pl.ds(pl.multiple_of(i * tt + r * NORM_ROWS, NORM_ROWS), NORM_ROWS), :] = ub
                return carry

            lax.fori_loop(0, tt // NORM_ROWS, norm_rows, 0, unroll=2)

        step = p * n_t + i
        slot = lax.rem(step, Z_SLOTS)

        def write_back(buf, chip_col):
            return pltpu.make_async_copy(
                z_buf.at[buf], z_hbm.at[rows, pl.ds(PAIR_COLS * chip_col, PAIR_COLS)],
                z_sems.at[buf])

        @pl.when(step >= Z_SLOTS)
        def _():
            write_back(slot, 0).wait()

        z_buf[slot] = jnp.dot(u_all[rows, :], w_blk[...], preferred_element_type=F32)
        for q in range(N_CHIPS):
            pl.when(order_ref[p] == q)(write_back(slot, q).start)

        for step in range(1, N_CHIPS):
            @pl.when(jnp.logical_and(p == step - 1, i == n_t - 1))
            def _(step=step):
                landed(2 * step).wait_recv()
                forward(2 * step).start()

        @pl.when(jnp.logical_and(p == N_CHIPS - 1, i == n_t - 1))
        def _():
            push(PUSHED[0]).wait_send()
            push(PUSHED[-1]).wait_send()
            for k in FORWARDED:
                forward(k).wait_send()
            keep.wait()
            for buf in range(Z_SLOTS):
                write_back(buf, 0).wait()

    first_pass = lambda p, i, order_ref: (jnp.where(p == 0, i, n_t - 1), 0)
    grid_spec = pltpu.PrefetchScalarGridSpec(
        num_scalar_prefetch=1, grid=(N_CHIPS, n_t),
        in_specs=[pl.BlockSpec((tt, D), first_pass),
                  pl.BlockSpec((1, D), lambda p, i, order_ref: (0, 0)), ANY],
        out_specs=[ANY,
                   pl.BlockSpec((tt, D), first_pass), ANY],
        scratch_shapes=[pltpu.VMEM((t, D), BF16), pltpu.VMEM((D, PAIR_COLS), BF16),
                        pltpu.SemaphoreType.DMA((N_DEV,)), pltpu.SemaphoreType.DMA((N_DEV,)),
                        pltpu.SemaphoreType.DMA((1,)),
                        pltpu.VMEM((Z_SLOTS, tt, PAIR_COLS), F32),
                        pltpu.SemaphoreType.DMA((Z_SLOTS,))])
    return pl.pallas_call(
        body, name="inproj_fwd", grid_spec=grid_spec,
        out_shape=[jax.ShapeDtypeStruct((t, N_COLS), F32), jax.ShapeDtypeStruct((t, D), BF16),
                   jax.ShapeDtypeStruct((N_DEV,) + w_shard.shape, BF16)],
        compiler_params=_params(48, dimension_semantics=("arbitrary", "arbitrary")),
    )(order, x, ln_g, w_shard)


def _shifted_copies(buf, shifted, rows):
    for b in range(1, SUBLANES):
        shifted[b, 0:rows, :] = buf[b:b + rows, :]


def _tap_slabs(buf, shifted, offset_of_tap):
    groups = {}
    for k in range(CONV_K):
        a, b = divmod(offset_of_tap(k), SUBLANES)
        groups.setdefault(b, []).append((SUBLANES * a, k))
    out = []
    for b, taps in sorted(groups.items()):
        taps.sort()
        lo = taps[0][0]
        out.append((buf if b == 0 else shifted.at[b], lo, [(k, off - lo) for off, k in taps]))
    return out


def _group_norm_stats(blk):
    mu = jnp.mean(blk, axis=-1, keepdims=True)
    cen = blk - mu
    var = jnp.mean(cen * cen, axis=-1, keepdims=True)
    return cen * lax.rsqrt(var + EPS)


def _conv_fwd(z, conv_w_all, conv_b, cn_g, cn_b, w_pw2, b_pw2):
    t = z.shape[0]
    tt = min(256, t)
    rc = 128

    def body(val_ref, glu_ref, gate_ref, cw_ref, cb_ref, g_ref, b_ref, w_hbm, b2_ref,
             yc_ref, y2_ref, yo_ref, w_vmem, vbuf, vsh, y1buf):
        @pl.when(pl.program_id(0) == 0)
        def _():
            pltpu.sync_copy(w_hbm, w_vmem)
            vbuf[0:CONV_PAD, :] = jnp.zeros((CONV_PAD, D), F32)

        vbuf[CONV_PAD:CONV_PAD + tt, :] = val_ref[...] * _sigmoid(glu_ref[...])
        _shifted_copies(vbuf, vsh, tt + 24)

        for g in range(N_HEADS):
            cs = slice(HEAD * g, HEAD * (g + 1))

            def row_chunk(r, carry, g=g, cs=cs):
                r0 = pl.multiple_of(r * rc, rc)
                acc = jnp.broadcast_to(cb_ref[:, cs], (rc, HEAD))
                for ref, lo, taps in _tap_slabs(vbuf, vsh, lambda k: k + 2):
                    slab = ref[pl.ds(r0 + lo, rc + taps[-1][1]), cs]
                    for k, off in taps:
                        acc = acc + cw_ref[g, k:k + 1, :] * slab[off:off + rc]
                yc_ref[pl.ds(r0, rc), cs] = acc
                n = _group_norm_stats(acc) * g_ref[:, cs] + b_ref[:, cs]
                y1buf[pl.ds(r0, rc), cs] = (n * _sigmoid(n)).astype(BF16)
                return carry

            lax.fori_loop(0, tt // rc, row_chunk, 0, unroll=True)
        vbuf[0:CONV_PAD, :] = vbuf[tt:tt + CONV_PAD, :]
        y2 = jnp.dot(y1buf[...], w_vmem[...], preferred_element_type=F32) + b2_ref[...]
        y2_ref[...] = y2
        gate = gate_ref[...]
        yo_ref[...] = (y2 * gate * _sigmoid(gate)).astype(BF16)

    col = lambda j: pl.BlockSpec((tt, D), lambda i: (i, j))
    row = pl.BlockSpec((tt, D), lambda i: (i, 0))
    return pl.pallas_call(
        body, name="conv_fwd", grid=(t // tt,),
        out_shape=[jax.ShapeDtypeStruct((t, D), F32), jax.ShapeDtypeStruct((t, D), F32),
                   jax.ShapeDtypeStruct((t, D), BF16)],
        in_specs=[col(0), col(1), col(2), _full((N_DEV, CONV_PAD, HEAD)), _full((1, D)),
                  _full((1, D)), _full((1, D)), ANY, _full((1, D))],
        out_specs=[row, row, row],
        scratch_shapes=[pltpu.VMEM((D, D), BF16), pltpu.VMEM((tt + CONV_PAD, D), F32),
                        pltpu.VMEM((SUBLANES, tt + CONV_PAD, D), F32), pltpu.VMEM((tt, D), BF16)],
        compiler_params=_params(48, dimension_semantics=("arbitrary",)),
    )(z, z, z, conv_w_all, conv_b, cn_g, cn_b, w_pw2, b_pw2)


HEAD_GROUP = 8
_HEAD_LANES = [slice(HEAD * j, HEAD * (j + 1)) for j in range(HEAD_GROUP)]


def _head_mean(a):
    return jnp.concatenate(
        [jnp.broadcast_to(jnp.mean(a[:, hs], axis=-1, keepdims=True), (a.shape[0], HEAD))
         for hs in _HEAD_LANES], axis=1)


def _chunk_quantities(zq, zf, lbh, tri):
    sig = _sigmoid(zf)
    sig_neg = _sigmoid(-zf)
    f = lbh + (1.0 - lbh) * sig
    k = (1.0 - lbh) * sig_neg
    q = zq * _sigmoid(zq)
    b = _tri_dot(tri, jnp.log(f))
    b_mid = b[CHUNK // 2 - 1:CHUNK // 2, :]
    b_last = b[CHUNK - 1:CHUNK, :]
    e_q = jnp.exp(b)
    e_qm = jnp.exp(b - b_mid)
    e_km = jnp.exp(b_mid - b)
    e_kd = jnp.exp(b_last - b)
    return q, k, f, sig, sig_neg, e_q, e_qm, e_km, e_kd, jnp.exp(b_last)


def _hgrn_fwd(z, lb_logits, onorm_g, shards):
    t = z.shape[0]
    tt = min(256, t)
    nc = tt // CHUNK
    modes = ["gather"] * len(shards)
    n = len(modes)

    def body(q_ref, f_ref, i_ref, g_ref, lbl_ref, on_ref, o_ref, y_ref, s_ref, st):
        @pl.when(pl.program_id(0) == 0)
        def _():
            st[...] = jnp.zeros_like(st)

        lb, _ = _lower_bound(lbl_ref[...])
        rows = lax.broadcasted_iota(jnp.int32, (CHUNK, CHUNK), 0)
        cols = lax.broadcasted_iota(jnp.int32, (CHUNK, CHUNK), 1)
        causal = rows >= cols
        tri = causal.astype(BF16)

        def chunk(c, carry):
            r0 = pl.multiple_of(c * CHUNK, CHUNK)
            rs = pl.ds(r0, CHUNK)
            for h0 in range(0, N_HEADS, HEAD_GROUP):
                cs = slice(HEAD * h0, HEAD * (h0 + HEAD_GROUP))
                q, k, _, _, _, e_q, e_qm, e_km, e_kd, e_last = _chunk_quantities(
                    q_ref[rs, cs], f_ref[rs, cs], lb[:, cs], tri)
                v = i_ref[rs, cs].astype(BF16)
                qm, km = (q * e_qm).astype(BF16), (k * e_km).astype(BF16)
                qt, kd = (q * e_q).astype(BF16), (k * e_kd).astype(BF16)
                outs = []
                for j, hs in enumerate(_HEAD_LANES):
                    s_old = st[h0 + j]
                    s_ref[c, h0 + j] = s_old.astype(BF16)
                    a = jnp.where(causal, _dot_nt(qm[:, hs], km[:, hs]), 0.0)
                    outs.append(_dot_nt(qt[:, hs], s_old) + _dot(a, v[:, hs]))
                    st[h0 + j] = s_old * e_last[:, hs] + _dot_tn(v[:, hs], kd[:, hs])
                o = jnp.concatenate(outs, axis=1)
                o_ref[rs, cs] = o
                n = o * lax.rsqrt(_head_mean(o * o) + EPS)
                zg = g_ref[rs, cs]
                y_ref[rs, cs] = (n * on_ref[:, cs] * zg * _sigmoid(zg)).astype(BF16)
            return carry

        lax.fori_loop(0, nc, chunk, 0, unroll=True)

    col = lambda j: pl.BlockSpec((tt, D), lambda i: (i, j))
    row = pl.BlockSpec((tt, D), lambda i: (i, 0))
    return pl.pallas_call(
        _hosted(body, 6, 3, (t // tt,), modes), name="hgrn_fwd", grid=(t // tt,),
        out_shape=[jax.ShapeDtypeStruct((t, D), F32), jax.ShapeDtypeStruct((t, D), BF16),
                   jax.ShapeDtypeStruct((t // CHUNK, N_HEADS, HEAD, HEAD), BF16)]
        + _recv_shapes(shards, modes),
        in_specs=[col(3), col(4), col(5), col(6), _full((2, D)), _full((1, D))] + [ANY] * n,
        out_specs=[row, row, pl.BlockSpec((nc, N_HEADS, HEAD, HEAD), lambda i: (i, 0, 0, 0))]
        + [ANY] * n,
        scratch_shapes=[pltpu.VMEM((N_HEADS, HEAD, HEAD), F32)] + _exchange_scratch(n),
        compiler_params=_params(40, dimension_semantics=("arbitrary",)),
    )(z, z, z, z, lb_logits, onorm_g, *shards)


def _rms_bwd(dn, xhat, rstd):
    return rstd * (dn - xhat * jnp.mean(dn * xhat, axis=-1, keepdims=True))


def _tail(x, y_conv, y_hgrn, p, target, w_out, w_pg, w_pp_all, pe_g, fin_g):
    t = x.shape[0]
    tt = min(256, t)
    n_steps = t // tt

    def body(x_ref, yc_ref, yh_ref, p_ref, tg_ref, wo_hbm, wg_hbm, wp_hbm, pg_ref, fg_ref,
             dh1_ref, dyc_ref, dyh_ref, dwo_hbm, dwg_hbm, dwp_hbm, dpg_ref, dfg_ref, loss_ref,
             wo, wg, wp, dwo, dwg, dwp):
        i = pl.program_id(0)

        @pl.when(i == 0)
        def _():
            pltpu.sync_copy(wo_hbm, wo)
            pltpu.sync_copy(wg_hbm, wg)
            for d in range(N_DEV):
                pltpu.sync_copy(wp_hbm.at[d], wp.at[:, pl.ds(HEAD * d, HEAD)])
            dwo[...] = jnp.zeros_like(dwo)
            dwg[...] = jnp.zeros_like(dwg)
            dwp[...] = jnp.zeros_like(dwp)
            dpg_ref[...] = jnp.zeros_like(dpg_ref)
            dfg_ref[...] = jnp.zeros_like(dfg_ref)
            loss_ref[...] = jnp.zeros_like(loss_ref)

        ycv, yhv = yc_ref[...], yh_ref[...]
        h1 = (x_ref[...] + jnp.dot(ycv, wo[0:D, :], preferred_element_type=F32)
              + jnp.dot(yhv, wo[D:2 * D, :], preferred_element_type=F32))
        pb = p_ref[...].astype(BF16)
        pe = jnp.dot(pb, wp[...], preferred_element_type=F32)
        rstd1 = lax.rsqrt(jnp.mean(h1 * h1, axis=-1, keepdims=True) + EPS)
        n1 = h1 * rstd1
        rb = (n1 * pg_ref[...]).astype(BF16)
        gate = _sigmoid(jnp.dot(rb, wg[...], preferred_element_type=F32))
        h2 = h1 + gate * pe
        rstd2 = lax.rsqrt(jnp.mean(h2 * h2, axis=-1, keepdims=True) + EPS)
        n2 = h2 * rstd2
        err = n2 * fg_ref[...] - tg_ref[...]
        loss_ref[...] += _rowsum8(err * err)

        d_out = err * (1.0 / D)
        dfg_ref[...] += _rowsum8(d_out * n2)
        d_h2 = _rms_bwd(d_out * fg_ref[...], n2, rstd2)
        d_pe = (d_h2 * gate).astype(BF16)
        d_gpre = (d_h2 * pe * gate * (1.0 - gate)).astype(BF16)
        dwg[...] += _dot_tn(rb, d_gpre)
        dwp[...] += _dot_tn(pb, d_pe)
        dr = _dot_nt(d_gpre, wg[...])
        dpg_ref[...] += _rowsum8(dr * n1)
        d_h1 = d_h2 + _rms_bwd(dr * pg_ref[...], n1, rstd1)
        dh1_ref[...] = d_h1
        d_h1b = d_h1.astype(BF16)
        dwo[0:D, :] += _dot_tn(ycv, d_h1b)
        dwo[D:2 * D, :] += _dot_tn(yhv, d_h1b)
        dyc_ref[...] = _dot_nt(d_h1b, wo[0:D, :])
        dyh_ref[...] = _dot_nt(d_h1b, wo[D:2 * D, :])

        @pl.when(i == n_steps - 1)
        def _():
            pltpu.sync_copy(dwo, dwo_hbm)
            pltpu.sync_copy(dwg, dwg_hbm)
            for d in range(N_DEV):
                pltpu.sync_copy(dwp.at[:, pl.ds(HEAD * d, HEAD)], dwp_hbm.at[d])

    row = pl.BlockSpec((tt, D), lambda i: (i, 0))
    acc = _full((SUBLANES, D))
    return pl.pallas_call(
        body, name="tail_fwd_bwd", grid=(n_steps,),
        out_shape=[jax.ShapeDtypeStruct((t, D), F32)] * 3
        + [jax.ShapeDtypeStruct((2 * D, D), F32), jax.ShapeDtypeStruct((D, D), F32),
           jax.ShapeDtypeStruct((N_DEV, PLE, HEAD), F32)]
        + [jax.ShapeDtypeStruct((SUBLANES, D), F32)] * 3,
        in_specs=[row, row, row, pl.BlockSpec((tt, PLE), lambda i: (i, 0)), row,
                  ANY, ANY, ANY, _full((1, D)), _full((1, D))],
        out_specs=[row, row, row, ANY, ANY, ANY, acc, acc, acc],
        scratch_shapes=[pltpu.VMEM((2 * D, D), BF16), pltpu.VMEM((D, D), BF16),
                        pltpu.VMEM((PLE, D), BF16), pltpu.VMEM((2 * D, D), F32),
                        pltpu.VMEM((D, D), F32), pltpu.VMEM((PLE, D), F32)],
        compiler_params=_params(52, dimension_semantics=("arbitrary",)),
    )(x, y_conv, y_hgrn, p, target, w_out, w_pg, w_pp_all, pe_g, fin_g)


def _hgrn_bwd(dy, z, o_raw, states, lb_logits, onorm_g, grads):
    t = z.shape[0]
    tt = min(256, t)
    nc = tt // CHUNK
    n_steps = t // tt
    modes = ["scatter"] * len(grads)

    def body(dy_ref, q_ref, f_ref, i_ref, g_ref, o_ref, s_ref, lbl_ref, on_ref,
             dz_ref, don_ref, dlb_ref, dst):
        @pl.when(pl.program_id(0) == 0)
        def _():
            dst[...] = jnp.zeros_like(dst)
            don_ref[...] = jnp.zeros_like(don_ref)
            dlb_ref[...] = jnp.zeros_like(dlb_ref)

        lb, _ = _lower_bound(lbl_ref[...])
        rows = lax.broadcasted_iota(jnp.int32, (CHUNK, CHUNK), 0)
        cols = lax.broadcasted_iota(jnp.int32, (CHUNK, CHUNK), 1)
        causal = rows >= cols
        tri = causal.astype(BF16)
        tri_rev = (rows <= cols).astype(BF16)
        width = HEAD * HEAD_GROUP
        is_last = lax.broadcasted_iota(jnp.int32, (CHUNK, width), 0) == CHUNK - 1
        nn = (((1,), (0,)), ((), ()))
        tn = (((0,), (0,)), ((), ()))
        dg = functools.partial(lax.dot_general, preferred_element_type=F32)

        def chunk(cc, carry):
            c = nc - 1 - cc
            r0 = pl.multiple_of(c * CHUNK, CHUNK)
            rs = pl.ds(r0, CHUNK)
            for h0 in range(0, N_HEADS, HEAD_GROUP):
                cs = slice(HEAD * h0, HEAD * h0 + width)
                zq, zf, zg = q_ref[rs, cs], f_ref[rs, cs], g_ref[rs, cs]
                lbh = lb[:, cs]
                q, k, f, sig, sig_neg, e_q, e_qm, e_km, e_kd, e_last = _chunk_quantities(
                    zq, zf, lbh, tri)
                vb = i_ref[rs, cs].astype(BF16)
                qt, qm, km, kd = q * e_q, q * e_qm, k * e_km, k * e_kd
                qt_b, kd_b = qt.astype(BF16), kd.astype(BF16)
                qm_h, qm_l = _split(qm)
                km_h, km_l = _split(km)

                o = o_ref[rs, cs]
                rstd = lax.rsqrt(_head_mean(o * o) + EPS)
                n = o * rstd
                sg = _sigmoid(zg)
                dyv = dy_ref[rs, cs]
                on = on_ref[:, cs]
                d_zg = dyv * n * on * sg * (1.0 + zg * (1.0 - sg))
                d_on = dyv * zg * sg
                don_ref[:, cs] += _rowsum8(d_on * n)
                dn = d_on * on
                do_b = (rstd * (dn - n * _head_mean(dn * n))).astype(BF16)

                dv, dkd, dqt, dqm, dkm, s_dots = [], [], [], [], [], []
                for j, hs in enumerate(_HEAD_LANES):
                    s_old, ds_new = s_ref[c, h0 + j], dst[h0 + j]
                    ds_b = ds_new.astype(BF16)
                    a = jnp.where(causal, _dot_nt(qm_h[:, hs], km_h[:, hs]), 0.0)
                    da = jnp.where(causal, _dot_nt(do_b[:, hs], vb[:, hs]), 0.0)
                    dv.append(_dot_tn(a, do_b[:, hs]) + _dot_nt(kd_b[:, hs], ds_b))
                    dkd.append(_dot(vb[:, hs], ds_b))
                    dqt.append(_dot(do_b[:, hs], s_old))
                    da_h, da_l = _split(da)
                    dqm.append(dg(da_h, km_h[:, hs], nn)
                               + (dg(da_h, km_l[:, hs], nn) + dg(da_l, km_h[:, hs], nn)))
                    dkm.append(dg(da_h, qm_h[:, hs], tn)
                               + (dg(da_h, qm_l[:, hs], tn) + dg(da_l, qm_h[:, hs], tn)))
                    dst[h0 + j] = ds_new * e_last[:, hs] + _dot_tn(do_b[:, hs], qt_b[:, hs])
                    s_dots.append(jnp.sum(s_old.astype(F32) * ds_new, axis=0, keepdims=True))
                dv, dkd, dqt, dqm, dkm, s_dots = [
                    jnp.concatenate(parts, axis=1) for parts in (dv, dkd, dqt, dqm, dkm, s_dots)]
                dq = dqt * e_q + dqm * e_qm
                dk = dkm * e_km + dkd * e_kd
                last = jnp.sum(dkd * kd, axis=0, keepdims=True) + e_last * s_dots
                db = q * dq - k * dk + jnp.where(is_last, last, 0.0)
                dlogf = _tri_dot(tri_rev, db)
                common = sig_neg * (dlogf / f - dk)
                dlb_ref[:, cs] += _rowsum8(common)
                c0 = 3 * D + HEAD * h0
                sq = _sigmoid(zq)
                dz_ref[rs, c0:c0 + width] = (dq * sq * (1.0 + zq * (1.0 - sq))).astype(BF16)
                dz_ref[rs, D + c0:D + c0 + width] = ((1.0 - lbh) * sig * common).astype(BF16)
                dz_ref[rs, 2 * D + c0:2 * D + c0 + width] = dv.astype(BF16)
                dz_ref[rs, 3 * D + c0:3 * D + c0 + width] = d_zg.astype(BF16)
            return carry

        lax.fori_loop(0, nc, chunk, 0, unroll=True)

    rev = lambda i: n_steps - 1 - i
    col = lambda j: pl.BlockSpec((tt, D), lambda i: (rev(i), j))
    row = pl.BlockSpec((tt, D), lambda i: (rev(i), 0))
    acc = _full((SUBLANES, D))
    n = len(modes)
    return pl.pallas_call(
        _hosted(body, 9, 3, (n_steps,), modes), name="hgrn_bwd", grid=(n_steps,),
        out_shape=[jax.ShapeDtypeStruct((t, N_COLS), BF16),
                   jax.ShapeDtypeStruct((SUBLANES, D), F32),
                   jax.ShapeDtypeStruct((SUBLANES, D), F32)] + _recv_shapes(grads, modes),
        in_specs=[row, col(3), col(4), col(5), col(6), row,
                  pl.BlockSpec((nc, N_HEADS, HEAD, HEAD), lambda i: (rev(i), 0, 0, 0)),
                  _full((2, D)), _full((1, D))] + [ANY] * n,
        out_specs=[pl.BlockSpec((tt, N_COLS), lambda i: (rev(i), 0)), acc, acc] + [ANY] * n,
        scratch_shapes=[pltpu.VMEM((N_HEADS, HEAD, HEAD), F32)] + _exchange_scratch(n),
        compiler_params=_params(48, dimension_semantics=("arbitrary",)),
    )(dy, z, z, z, z, o_raw, states, lb_logits, onorm_g, *grads)


def _conv_bwd(dy, z, yc, y2, conv_w_all, cn_g, cn_b, w_pw2, dz, grads, modes):
    t = z.shape[0]
    tt = min(256, t)
    rc = 32
    n_steps = t // tt

    def body(dy_ref, val_ref, glu_ref, gate_ref, yc_ref, y2_ref, cw_ref, g_ref, b_ref, w_hbm,
             dz_in, dz_ref, dw_hbm, dcw_out, db2_ref, dg_ref, dbeta_ref, dcb_ref,
             w_vmem, dw, dbuf, dsh, y1buf, dnbuf, dcw_ref):
        i = pl.program_id(0)

        @pl.when(i == 0)
        def _():
            pltpu.sync_copy(w_hbm, w_vmem)
            dw[...] = jnp.zeros_like(dw)
            dbuf[tt:tt + CONV_PAD, :] = jnp.zeros((CONV_PAD, D), F32)
            dcw_ref[...] = jnp.zeros_like(dcw_ref)
            dcw_out[...] = jnp.zeros_like(dcw_out)
            db2_ref[...] = jnp.zeros_like(db2_ref)
            dg_ref[...] = jnp.zeros_like(dg_ref)
            dbeta_ref[...] = jnp.zeros_like(dbeta_ref)
            dcb_ref[...] = jnp.zeros_like(dcb_ref)

        gate = gate_ref[...]
        sg = _sigmoid(gate)
        dyv = dy_ref[...]
        dy2 = dyv * gate * sg
        dz_ref[:, 2 * D:3 * D] = (dyv * y2_ref[...] * sg * (1.0 + gate * (1.0 - sg))).astype(BF16)
        db2_ref[...] += _rowsum8(dy2)
        dy2b = dy2.astype(BF16)
        dnbuf[...] = _dot_nt(dy2b, w_vmem[...])

        def norm_chunk(r, carry):
            r0 = pl.multiple_of(r * rc, rc)
            rs = pl.ds(r0, rc)
            for g in range(N_HEADS):
                cs = slice(HEAD * g, HEAD * (g + 1))
                blk = yc_ref[rs, cs]
                mu = jnp.mean(blk, axis=-1, keepdims=True)
                cen = blk - mu
                rstd = lax.rsqrt(jnp.mean(cen * cen, axis=-1, keepdims=True) + EPS)
                xhat = cen * rstd
                n = xhat * g_ref[:, cs] + b_ref[:, cs]
                sn = _sigmoid(n)
                y1buf[rs, cs] = (n * sn).astype(BF16)
                dn = dnbuf[rs, cs] * sn * (1.0 + n * (1.0 - sn))
                dg_ref[:, cs] += _rowsum8(dn * xhat)
                dbeta_ref[:, cs] += _rowsum8(dn)
                dxh = dn * g_ref[:, cs]
                dyc = rstd * (dxh - jnp.mean(dxh, axis=-1, keepdims=True)
                              - xhat * jnp.mean(dxh * xhat, axis=-1, keepdims=True))
                dcb_ref[:, cs] += _rowsum8(dyc)
                dbuf[rs, cs] = dyc
            return carry

        lax.fori_loop(0, tt // rc, norm_chunk, 0, unroll=True)
        dw[...] += _dot_tn(y1buf[...], dy2b)
        _shifted_copies(dbuf, dsh, tt + 24)

        def conv_chunk(r, carry):
            r0 = pl.multiple_of(r * rc, rc)
            rs = pl.ds(r0, rc)
            for g in range(N_HEADS):
                cs = slice(HEAD * g, HEAD * (g + 1))
                sglu = _sigmoid(glu_ref[rs, cs])
                val = val_ref[rs, cs]
                v = val * sglu
                dv = jnp.zeros((rc, HEAD), F32)
                for ref, lo, taps in _tap_slabs(dbuf, dsh, lambda k: CONV_K - 1 - k):
                    slab = ref[pl.ds(r0 + lo, rc + taps[-1][1]), cs]
                    for k, off in taps:
                        d_later = slab[off:off + rc]
                        dv = dv + cw_ref[g, k:k + 1, :] * d_later
                        dcw_ref[g, k] += _rowsum8(v * d_later)
                dz_ref[rs, cs] = (dv * sglu).astype(BF16)
                dz_ref[rs, D + HEAD * g:D + HEAD * (g + 1)] = (
                    dv * val * sglu * (1.0 - sglu)).astype(BF16)
            return carry

        lax.fori_loop(0, tt // rc, conv_chunk, 0, unroll=2)
        dbuf[tt:tt + CONV_PAD, :] = dbuf[0:CONV_PAD, :]

        @pl.when(i == n_steps - 1)
        def _():
            pltpu.sync_copy(dw, dw_hbm)
            for g in range(N_HEADS):
                for k in range(CONV_K):
                    dcw_out[g, k:k + 1, :] = jnp.sum(dcw_ref[g, k], axis=0, keepdims=True)

    rev = lambda i: n_steps - 1 - i
    col = lambda j: pl.BlockSpec((tt, D), lambda i: (rev(i), j))
    row = pl.BlockSpec((tt, D), lambda i: (rev(i), 0))
    acc = _full((SUBLANES, D))
    n = len(modes)
    return pl.pallas_call(
        _hosted(body, 11, 7, (n_steps,), modes), name="conv_bwd", grid=(n_steps,),
        out_shape=[jax.ShapeDtypeStruct((t, N_COLS), BF16), jax.ShapeDtypeStruct((D, D), F32),
                   jax.ShapeDtypeStruct((N_DEV, CONV_PAD, HEAD), F32)]
        + [jax.ShapeDtypeStruct((SUBLANES, D), F32)] * 4 + _recv_shapes(grads, modes),
        in_specs=[row, col(0), col(1), col(2), row, row, _full((N_DEV, CONV_PAD, HEAD)),
                  _full((1, D)), _full((1, D)), ANY, ANY] + [ANY] * n,
        out_specs=[pl.BlockSpec((tt, 3 * D), lambda i: (rev(i), 0)), ANY,
                   _full((N_DEV, CONV_PAD, HEAD)), acc, acc, acc, acc] + [ANY] * n,
        input_output_aliases={10: 0},
        scratch_shapes=[pltpu.VMEM((D, D), BF16), pltpu.VMEM((D, D), F32),
                        pltpu.VMEM((tt + CONV_PAD, D), F32),
                        pltpu.VMEM((SUBLANES, tt + CONV_PAD, D), F32),
                        pltpu.VMEM((tt, D), BF16), pltpu.VMEM((tt, D), F32),
                        pltpu.VMEM((N_DEV, CONV_PAD, SUBLANES, HEAD), F32)] + _exchange_scratch(n),
        compiler_params=_params(52, dimension_semantics=("arbitrary",)),
    )(dy, z, z, z, yc, y2, conv_w_all, cn_g, cn_b, w_pw2, dz, *grads)


def _inproj_bwd_dx(dz, x, d_h1, ln_g, w_in_all, grads, modes):
    t = x.shape[0]
    tt = min(256, t)

    def body(dz_ref, x_ref, dh1_ref, g_ref, w_hbm, dx_ref, dg_ref, w_vmem):
        @pl.when(pl.program_id(0) == 0)
        def _():
            for d in range(N_DEV):
                pltpu.sync_copy(w_hbm.at[d], w_vmem.at[
                    d // 2, :, pl.ds(COLS_PER_DEV * (d % 2), COLS_PER_DEV)])
            dg_ref[...] = jnp.zeros_like(dg_ref)

        du = jnp.zeros((tt, D), F32)
        for q in range(N_CHIPS):
            du = du + lax.dot_general(
                dz_ref[:, PAIR_COLS * q:PAIR_COLS * (q + 1)], w_vmem[q],
                (((1,), (1,)), ((), ())), preferred_element_type=F32)
        xv = x_ref[...]
        rstd = lax.rsqrt(jnp.mean(xv * xv, axis=-1, keepdims=True) + EPS)
        xhat = xv * rstd
        dg_ref[...] += _rowsum8(du * xhat)
        dx_ref[...] = dh1_ref[...] + _rms_bwd(du * g_ref[...], xhat, rstd)

    row = pl.BlockSpec((tt, D), lambda i: (i, 0))
    n = len(modes)
    return pl.pallas_call(
        _hosted(body, 5, 2, (t // tt,), modes), name="inproj_bwd_dx", grid=(t // tt,),
        out_shape=[jax.ShapeDtypeStruct((t, D), F32), jax.ShapeDtypeStruct((SUBLANES, D), F32)]
        + _recv_shapes(grads, modes),
        in_specs=[pl.BlockSpec((tt, N_COLS), lambda i: (i, 0)), row, row, _full((1, D)), ANY]
        + [ANY] * n,
        out_specs=[row, _full((SUBLANES, D))] + [ANY] * n,
        scratch_shapes=[pltpu.VMEM((N_CHIPS, D, PAIR_COLS), BF16)] + _exchange_scratch(n),
        compiler_params=_params(48, dimension_semantics=("arbitrary",)),
    )(dz, x, d_h1, ln_g, w_in_all, *grads)


def _inproj_bwd_dw(name, u, dz, first, count, grads=(), modes=()):
    t = u.shape[0]
    tt = min(2048, t)
    grid = (count // 2, t // tt)
    n = len(modes)

    def body(u_ref, dz_ref, dw_ref):
        @pl.when(pl.program_id(1) == 0)
        def _():
            dw_ref[...] = jnp.zeros_like(dw_ref)

        both = lax.dot_general(u_ref[...], dz_ref[...], (((0,), (0,)), ((), ())),
                               preferred_element_type=F32)
        dw_ref[0] += both[:, :COLS_PER_DEV]
        dw_ref[1] += both[:, COLS_PER_DEV:]

    return pl.pallas_call(
        _hosted(body, 2, 1, grid, modes) if n else body, name=name, grid=grid,
        out_shape=[jax.ShapeDtypeStruct((count, D, COLS_PER_DEV), F32)]
        + _recv_shapes(grads, modes),
        in_specs=[pl.BlockSpec((tt, D), lambda j, i: (i, 0)),
                  pl.BlockSpec((tt, PAIR_COLS), lambda j, i: (i, first // 2 + j))] + [ANY] * n,
        out_specs=[pl.BlockSpec((2, D, COLS_PER_DEV), lambda j, i: (j, 0, 0))] + [ANY] * n,
        scratch_shapes=_exchange_scratch(n) if n else [],
        compiler_params=_params(56, dimension_semantics=("arbitrary", "arbitrary")),
    )(u, dz, *grads)


def _adamw(w, g, m, v):
    m = ADAM_B1 * m + (1.0 - ADAM_B1) * g
    v = ADAM_B2 * v + (1.0 - ADAM_B2) * (g * g)
    m_hat = m / (1.0 - ADAM_B1 ** ADAM_STEP)
    v_hat = v / (1.0 - ADAM_B2 ** ADAM_STEP)
    delta = -ADAM_LR * (m_hat / (jnp.sqrt(v_hat) + ADAM_EPS) + ADAM_WD * w)
    return delta, m, v


def _pack_small(partials):
    rows = sorted(partials)

    def body(*refs):
        ins, out_ref = refs[:-1], refs[-1]
        out_ref[...] = jnp.zeros_like(out_ref)
        for j, row in enumerate(rows):
            out_ref[row:row + 1, :] = jnp.sum(ins[j][...], axis=0, keepdims=True)

    return pl.pallas_call(
        body, name="pack_small", out_shape=jax.ShapeDtypeStruct((N_SMALL, D), F32),
    )(*[partials[row] for row in rows])


def _sum_adam(name, recvs, w, m, v, rows):
    r, c = w.shape
    n = len(recvs)
    side = lax.axis_index("x").astype(jnp.int32).reshape(1)

    def body(side_ref, *refs):
        w_ref, m_ref, v_ref, g_ref, d_ref, mo_ref, vo_ref = refs[n:]

        def finish(recv_ref):
            g = recv_ref[0]
            for s in range(1, recv_ref.shape[0]):
                g = g + recv_ref[s]
            g_ref[...] = g
            d_ref[...], mo_ref[...], vo_ref[...] = _adamw(w_ref[...], g, m_ref[...], v_ref[...])

        if n == 1:
            finish(refs[0])
        else:
            for s in range(n):
                pl.when(side_ref[0] == s)(functools.partial(finish, refs[s]))

    def recv_spec(s, rv):
        if n == 1:
            return pl.BlockSpec((rv.shape[0], rows, c), lambda i, side_ref: (0, i, 0))
        return pl.BlockSpec((rv.shape[0], rows, c),
                            lambda i, side_ref: (0, jnp.where(side_ref[0] == s, i, 0), 0))

    blk = pl.BlockSpec((rows, c), lambda i, side_ref: (i, 0))
    grid_spec = pltpu.PrefetchScalarGridSpec(
        num_scalar_prefetch=1, grid=(r // rows,),
        in_specs=[recv_spec(s, rv) for s, rv in enumerate(recvs)] + [blk, blk, blk],
        out_specs=[blk] * 4)
    return pl.pallas_call(
        body, name=name, grid_spec=grid_spec,
        out_shape=[jax.ShapeDtypeStruct((r, c), F32)] * 4,
        compiler_params=_params(48, dimension_semantics=("arbitrary",)),
    )(side, *recvs, w, m, v)


def _small_adam(gathered, lb_logits, w, m, v):
    def body(ga_ref, lbl_ref, w_ref, m_ref, v_ref, g_ref, d_ref, mo_ref, vo_ref, loss_ref):
        g = ga_ref[0]
        for s in range(1, N_DEV):
            g = g + ga_ref[s]
        s0, s1 = _lower_bound(lbl_ref[...])
        d_lb = g[R_LB0:R_LB0 + 1, :]
        rows = lax.broadcasted_iota(jnp.int32, (N_SMALL, D), 0)
        g = jnp.where(rows == R_LB0, d_lb * s0 * (1.0 - s0), g)
        g = jnp.where(rows == R_LB1, -d_lb * s0 * s1, g)
        g_ref[...] = g
        d_ref[...], mo_ref[...], vo_ref[...] = _adamw(w_ref[...], g, m_ref[...], v_ref[...])
        loss_ref[...] = (0.5 / D) * jnp.sum(g[R_LOSS:R_LOSS + 1, :], axis=-1, keepdims=True)

    return pl.pallas_call(
        body, name="small_adam",
        out_shape=[jax.ShapeDtypeStruct((N_SMALL, D), F32)] * 4 + [jax.ShapeDtypeStruct((1, 1), F32)],
    )(gathered, lb_logits, w, m, v)


def _pad_rows(a, rows):
    return jnp.pad(a, ((0, rows - a.shape[0]), (0, 0)))


def _pack_rows(rows):
    rows = [r.reshape(-1, D) for r in rows]
    packed = jnp.concatenate(rows, axis=0)
    return _pad_rows(packed, N_SMALL)


def kernel(x, p, ln_g, w_in, conv_w, conv_b, cnorm_g, cnorm_b, w_pw2, b_pw2, lb_logits, onorm_g, w_out, pe_norm_g, w_pg, w_pp, final_g, loss_target, m_ln_g, m_w_in, m_conv_w, m_conv_b, m_cnorm_g, m_cnorm_b, m_w_pw2, m_b_pw2, m_lb_logits, m_onorm_g, m_w_out, m_pe_norm_g, m_w_pg, m_w_pp, m_final_g, v_ln_g, v_w_in, v_conv_w, v_conv_b, v_cnorm_g, v_cnorm_b, v_w_pw2, v_b_pw2, v_lb_logits, v_onorm_g, v_w_out, v_pe_norm_g, v_w_pg, v_w_pp, v_final_g):
    t = x.shape[1]
    x2 = x.reshape(t, D)
    p2 = p.reshape(t, PLE)
    tg2 = loss_target.reshape(t, D)
    fin_g = final_g.reshape(1, D)

    z, u, w_in_all = _inproj_fwd(x2, ln_g, w_in[0].astype(BF16))
    (o_raw, y_hgrn, states, conv_w_all, w_pw2_all, w_out_all, w_pg_all, w_pp_all) = _hgrn_fwd(
        z, lb_logits, onorm_g,
        [_pad_rows(conv_w[0], CONV_PAD), w_pw2[0].astype(BF16), w_out[0].astype(BF16),
         w_pg[0].astype(BF16), w_pp[0].astype(BF16)])
    w_pw2_full = w_pw2_all.reshape(D, D)
    w_out_full = w_out_all.reshape(2 * D, D)
    w_pg_full = w_pg_all.reshape(D, D)
    yc, y2, y_conv = _conv_fwd(z, conv_w_all, conv_b, cnorm_g, cnorm_b, w_pw2_full, b_pw2)

    (d_h1, dy_conv, dy_hgrn, d_w_out, d_w_pg, d_w_pp, d_pen_p, d_fin_p, loss_p) = _tail(
        x2, y_conv, y_hgrn, p2, tg2, w_out_full, w_pg_full, w_pp_all, pe_norm_g, fin_g)

    dz, d_on_p, d_lb_p, r_w_out, r_w_pg, r_w_pp = _hgrn_bwd(
        dy_hgrn, z, o_raw, states, lb_logits, onorm_g,
        [d_w_out.reshape(N_DEV, 2 * D // N_DEV, D), d_w_pg.reshape(N_DEV, D // N_DEV, D), d_w_pp])
    (d_w_in_hi,) = _inproj_bwd_dw("inproj_bwd_dw_hi", u, dz, N_DEV // 2, N_DEV // 2)
    dz, d_w_pw2, d_conv_w, d_b2_p, d_cng_p, d_cnb_p, d_cb_p, r_w_in_hi = _conv_bwd(
        dy_conv, z, yc, y2, conv_w_all, cnorm_g, cnorm_b, w_pw2_full, dz, [d_w_in_hi], [1])
    d_w_in_lo, r_w_pw2, r_conv_w = _inproj_bwd_dw(
        "inproj_bwd_dw_lo", u, dz, 0, N_DEV // 2,
        [d_w_pw2.reshape(N_DEV, D // N_DEV, D), d_conv_w], ["scatter", "scatter"])
    chip_lo = _pair_reduce("pair_reduce_lo", d_w_in_lo)
    grad_x, d_ln_p, r_w_in_lo = _inproj_bwd_dx(
        dz, x2, d_h1, ln_g, w_in_all, [chip_lo], [("chip", 0)])

    small = _pack_small({R_LN: d_ln_p, R_CONVB: d_cb_p, R_CNG: d_cng_p, R_CNB: d_cnb_p,
                         R_BPW2: d_b2_p, R_LB0: d_lb_p, R_ON: d_on_p, R_PEN: d_pen_p,
                         R_FIN: d_fin_p, R_LOSS: loss_p})
    (small_all,) = _exchange_call("gather_small", [small], ["gather"])

    big = {}
    big["w_in"] = _sum_adam("adam_w_in", [r_w_in_lo, r_w_in_hi], w_in[0], m_w_in[0], v_w_in[0], 128)
    cw = _sum_adam("adam_conv_w", [r_conv_w], _pad_rows(conv_w[0], CONV_PAD),
                   _pad_rows(m_conv_w[0], CONV_PAD), _pad_rows(v_conv_w[0], CONV_PAD), CONV_PAD)
    big["conv_w"] = [a[:CONV_K] for a in cw]
    big["w_pw2"] = _sum_adam("adam_w_pw2", [r_w_pw2], w_pw2[0], m_w_pw2[0], v_w_pw2[0], 128)
    big["w_out"] = _sum_adam("adam_w_out", [r_w_out], w_out[0], m_w_out[0], v_w_out[0], 128)
    big["w_pg"] = _sum_adam("adam_w_pg", [r_w_pg], w_pg[0], m_w_pg[0], v_w_pg[0], 128)
    big["w_pp"] = _sum_adam("adam_w_pp", [r_w_pp], w_pp[0], m_w_pp[0], v_w_pp[0], PLE)

    small_w = [ln_g, conv_b, cnorm_g, cnorm_b, b_pw2, lb_logits, onorm_g, pe_norm_g, final_g]
    small_m = [m_ln_g, m_conv_b, m_cnorm_g, m_cnorm_b, m_b_pw2, m_lb_logits, m_onorm_g,
               m_pe_norm_g, m_final_g]
    small_v = [v_ln_g, v_conv_b, v_cnorm_g, v_cnorm_b, v_b_pw2, v_lb_logits, v_onorm_g,
               v_pe_norm_g, v_final_g]
    sg, sd, sm, sv, loss = _small_adam(small_all, lb_logits, _pack_rows(small_w),
                                       _pack_rows(small_m), _pack_rows(small_v))

    small_rows = {"ln_g": (R_LN, 1), "conv_b": (R_CONVB, 1), "cnorm_g": (R_CNG, 1),
                  "cnorm_b": (R_CNB, 1), "b_pw2": (R_BPW2, 1), "lb_logits": (R_LB0, 2),
                  "onorm_g": (R_ON, 1), "pe_norm_g": (R_PEN, 1), "final_g": (R_FIN, 1)}
    order = ["ln_g", "w_in", "conv_w", "conv_b", "cnorm_g", "cnorm_b", "w_pw2", "b_pw2",
             "lb_logits", "onorm_g", "w_out", "pe_norm_g", "w_pg", "w_pp", "final_g"]

    def leaf(kind, name):
        if name in big:
            return big[name][kind][None]
        r0, n = small_rows[name]
        a = (sg, sd, sm, sv)[kind][r0:r0 + n]
        return a.reshape(D) if name == "final_g" else a

    outs = [loss.reshape(()), grad_x.reshape(1, t, D)]
    for kind in range(4):
        outs += [leaf(kind, name) for name in order]
    return tuple(outs)
```

```python
import functools

import jax
import jax.numpy as jnp
from jax import lax
from jax.experimental import pallas as pl
from jax.experimental.pallas import tpu as pltpu

F32 = jnp.float32
BF16 = jnp.bfloat16
MESH = pl.DeviceIdType.MESH

N_DEV = 8
D = 1024
N_COLS = 7 * D
COLS_PER_DEV = N_COLS // N_DEV
PLE = 256
HEAD = 128
N_HEADS = D // HEAD
CONV_K = 31
CONV_PAD = 32
CHUNK = 64
EPS = 1e-6
SUBLANES = 8

ADAM_LR = 0.001
ADAM_B1 = 0.9
ADAM_B2 = 0.999
ADAM_EPS = 1e-08
ADAM_WD = 0.01
ADAM_STEP = 10

MIB = 1024 * 1024
N_SMALL = 16
R_LN, R_CONVB, R_CNG, R_CNB, R_BPW2, R_LB0, R_LB1, R_ON, R_PEN, R_FIN, R_LOSS = range(11)


def _params(vmem_mib, **kw):
    return pltpu.CompilerParams(vmem_limit_bytes=vmem_mib * MIB, **kw)


def _dot(a, b):
    return jnp.dot(a.astype(BF16), b.astype(BF16), preferred_element_type=F32)


def _dot_nt(a, b):
    return lax.dot_general(a.astype(BF16), b.astype(BF16), (((1,), (1,)), ((), ())),
                           preferred_element_type=F32)


def _dot_tn(a, b):
    return lax.dot_general(a.astype(BF16), b.astype(BF16), (((0,), (0,)), ((), ())),
                           preferred_element_type=F32)


def _split(a):
    hi = a.astype(BF16)
    return hi, (a - hi.astype(F32)).astype(BF16)


def _sigmoid(x):
    return 1.0 / (1.0 + jnp.exp(-x))


def _rowsum8(a):
    r, c = a.shape
    return jnp.sum(a.reshape(r // SUBLANES, SUBLANES, c), axis=0)


def _tri_dot(tri, a):
    hi = a.astype(BF16)
    r1 = a - hi.astype(F32)
    mid = r1.astype(BF16)
    lo = (r1 - mid.astype(F32)).astype(BF16)
    return (jnp.dot(tri, hi, preferred_element_type=F32)
            + jnp.dot(tri, mid, preferred_element_type=F32)
            + jnp.dot(tri, lo, preferred_element_type=F32))


def _lower_bound(lbl):
    l0, l1 = lbl[0:1, :], lbl[1:2, :]
    m = jnp.maximum(l0, l1)
    e0, e1 = jnp.exp(l0 - m), jnp.exp(l1 - m)
    s = e0 + e1
    return e0 / s, e1 / s


ANY = pl.BlockSpec(memory_space=pl.ANY)


def _full(shape):
    return pl.BlockSpec(shape, lambda i: (0,) * len(shape))


def _peer(x, y, c, k):
    px = 1 - x if k & 4 else x
    py = 1 - y if k & 2 else y
    pc = 1 - c if k & 1 else c
    return (px, py, pc), 4 * px + 2 * py + pc


class _Exchange:
    def __init__(self, srcs, outs, modes, send_sems, recv_sems, local_sems):
        x, y, c = lax.axis_index("x"), lax.axis_index("y"), lax.axis_index("c")
        me = 4 * x + 2 * y + c
        self.starts, self.send_waits, self.recv_waits = [], [], []

        def remote(a, k, src, slot, peer, when):
            sem = a * N_DEV + k
            cp = pltpu.make_async_remote_copy(
                src_ref=src, dst_ref=outs[a].at[slot], send_sem=send_sems.at[sem],
                recv_sem=recv_sems.at[sem], device_id=peer, device_id_type=MESH)
            self.starts.append((when, cp.start))
            self.send_waits.append((when, cp.wait_send))

        def arrival(a, k, slot, when):
            sem = a * N_DEV + k
            cp = pltpu.make_async_remote_copy(
                src_ref=outs[a].at[slot], dst_ref=outs[a].at[slot], send_sem=send_sems.at[sem],
                recv_sem=recv_sems.at[sem], device_id=(x, y, c), device_id_type=MESH)
            self.recv_waits.append((when, cp.wait_recv))

        def local(a, src, slot, when):
            cp = pltpu.make_async_copy(src, outs[a].at[slot], local_sems.at[a])
            self.starts.append((when, cp.start))
            self.send_waits.append((when, cp.wait))

        for a, (src, mode) in enumerate(zip(srcs, modes)):
            if mode in ("gather", "scatter"):
                local(a, src if mode == "gather" else src.at[me], me, None)
                for k in range(1, N_DEV):
                    peer, peer_idx = _peer(x, y, c, k)
                    remote(a, k, src if mode == "gather" else src.at[peer_idx], me, peer, None)
                    arrival(a, k, peer_idx, None)
                continue
            if isinstance(mode, tuple):
                here, away = x == mode[1], x != mode[1]
                chip = 2 * x + y
                local(a, src.at[y], chip, here)
                remote(a, 1, src.at[1 - y], chip, (x, 1 - y, c), here)
                remote(a, 2, src.at[y], chip, (1 - x, y, c), away)
                remote(a, 3, src.at[1 - y], chip, (1 - x, 1 - y, c), away)
                arrival(a, 1, 2 * x + 1 - y, here)
                arrival(a, 2, 2 * (1 - x) + y, here)
                arrival(a, 3, 2 * (1 - x) + 1 - y, here)
                continue
            here, away = x == mode, x != mode
            for kk in range(4):
                py = 1 - y if kk & 2 else y
                pc = 1 - c if kk & 1 else c
                block = src.at[2 * py + pc]
                if kk == 0:
                    local(a, block, me, here)
                else:
                    remote(a, kk, block, me, (x, py, pc), here)
                remote(a, 4 + kk, block, me, (1 - x, py, pc), away)
            for k in range(1, N_DEV):
                arrival(a, k, _peer(x, y, c, k)[1], here)

    @staticmethod
    def _run(actions):
        for when, fn in actions:
            if when is None:
                fn()
            else:
                pl.when(when)(fn)

    def start(self):
        self._run(self.starts)

    def wait(self):
        self._run(self.recv_waits)
        self._run(self.send_waits)


def _exchange_scratch(n):
    return [pltpu.SemaphoreType.DMA((n * N_DEV,)), pltpu.SemaphoreType.DMA((n * N_DEV,)),
            pltpu.SemaphoreType.DMA((n,))]


def _recv_shapes(srcs, modes):
    def shape(s, m):
        if m == "gather":
            return (N_DEV,) + s.shape
        return (N_DEV // 2 if isinstance(m, tuple) else N_DEV,) + s.shape[1:]

    return [jax.ShapeDtypeStruct(shape(s, m), s.dtype) for s, m in zip(srcs, modes)]


def _pair_reduce(name, blocks):
    shape = (2,) + blocks.shape[1:]

    def body(src, out_ref, stage, mine, send_sems, recv_sems, local_sems):
        x, y, c = lax.axis_index("x"), lax.axis_index("y"), lax.axis_index("c")
        sends, waits = [], []
        for py in range(2):
            sends.append(pltpu.make_async_remote_copy(
                src_ref=src.at[2 * py + 1 - c], dst_ref=stage.at[py], send_sem=send_sems.at[py],
                recv_sem=recv_sems.at[py], device_id=(x, y, 1 - c), device_id_type=MESH))
            waits.append(pltpu.make_async_copy(src.at[2 * py + c], mine.at[py], local_sems.at[py]))
        for cp in sends + waits:
            cp.start()
        for cp in waits:
            cp.wait()
        for cp in sends:
            cp.wait_recv()
        out_ref[...] = mine[...] + stage[...]
        for cp in sends:
            cp.wait_send()

    return pl.pallas_call(
        body, name=name, out_shape=jax.ShapeDtypeStruct(shape, F32), in_specs=[ANY],
        scratch_shapes=[pltpu.VMEM(shape, F32), pltpu.VMEM(shape, F32),
                        pltpu.SemaphoreType.DMA((2,)), pltpu.SemaphoreType.DMA((2,)),
                        pltpu.SemaphoreType.DMA((2,))],
        compiler_params=_params(40),
    )(blocks)


def _exchange_call(name, srcs, modes):
    n = len(srcs)

    def body(*refs):
        xch = _Exchange(refs[:n], refs[n:2 * n], modes, *refs[2 * n:])
        xch.start()
        xch.wait()

    return pl.pallas_call(
        body, name=name, out_shape=_recv_shapes(srcs, modes),
        in_specs=[ANY] * n, out_specs=[ANY] * n, scratch_shapes=_exchange_scratch(n),
    )(*srcs)


def _hosted(body, n_in, n_out, grid, modes):
    n = len(modes)

    def hosted(*refs):
        ins, srcs = refs[:n_in], refs[n_in:n_in + n]
        outs = refs[n_in + n:n_in + n + n_out]
        bufs = refs[n_in + n + n_out:n_in + 2 * n + n_out]
        scratch = refs[n_in + 2 * n + n_out:-3]
        xch = _Exchange(srcs, bufs, modes, *refs[-3:])
        first, last = True, True
        for axis, size in enumerate(grid):
            first = jnp.logical_and(first, pl.program_id(axis) == 0)
            last = jnp.logical_and(last, pl.program_id(axis) == size - 1)
        pl.when(first)(xch.start)
        body(*ins, *outs, *scratch)
        pl.when(last)(xch.wait)

    return hosted


N_CHIPS = N_DEV // 2
PAIR_COLS = 2 * COLS_PER_DEV
PUSHED = (1, 2, 4, 6)
FORWARDED = (2, 4, 6)
NORM_ROWS = 32
Z_SLOTS = 3


def _inproj_fwd(x, ln_g, w_shard):
    t = x.shape[0]
    tt = min(512, t)
    n_t = t // tt
    chip = 2 * lax.axis_index("x") + lax.axis_index("y")
    order = jnp.bitwise_xor(chip, jnp.arange(N_CHIPS, dtype=jnp.int32)).astype(jnp.int32)

    def body(order_ref, x_ref, g_ref, shard_hbm, z_hbm, u_ref, w_all,
             u_all, w_blk, w_send, w_recv, w_local, z_buf, z_sems):
        p, i = pl.program_id(0), pl.program_id(1)
        x, y, c = lax.axis_index("x"), lax.axis_index("y"), lax.axis_index("c")
        mine = 4 * x + 2 * y + c

        def push(k):
            peer, _ = _peer(x, y, c, k)
            return pltpu.make_async_remote_copy(
                src_ref=shard_hbm, dst_ref=w_all.at[mine], send_sem=w_send.at[k],
                recv_sem=w_recv.at[k], device_id=peer, device_id_type=MESH)

        def forward(k):
            _, owner = _peer(x, y, c, k)
            return pltpu.make_async_remote_copy(
                src_ref=w_all.at[owner], dst_ref=w_all.at[owner], send_sem=w_send.at[k + 1],
                recv_sem=w_recv.at[k + 1], device_id=(x, y, 1 - c), device_id_type=MESH)

        def landed(k):
            _, owner = _peer(x, y, c, k)
            return pltpu.make_async_remote_copy(
                src_ref=w_all.at[owner], dst_ref=w_all.at[owner], send_sem=w_send.at[k],
                recv_sem=w_recv.at[k], device_id=(x, y, c), device_id_type=MESH)

        keep = pltpu.make_async_copy(shard_hbm, w_all.at[mine], w_local.at[0])

        def load_pair(step):
            same = shard_hbm if step == 0 else w_all.at[_peer(x, y, c, 2 * step)[1]]
            other = w_all.at[_peer(x, y, c, 2 * step + 1)[1]]
            for side in range(2):
                @pl.when(c == side)
                def _(side=side):
                    pltpu.sync_copy(same, w_blk.at[:, pl.ds(COLS_PER_DEV * side, COLS_PER_DEV)])
                    pltpu.sync_copy(
                        other, w_blk.at[:, pl.ds(COLS_PER_DEV * (1 - side), COLS_PER_DEV)])

        @pl.when(jnp.logical_and(p == 0, i == 0))
        def _():
            for k in PUSHED[:-1]:
                push(k).start()
            keep.start()

        @pl.when(jnp.logical_and(p == 1, i == 0))
        def _():
            for k in PUSHED[1:-1]:
                push(k).wait_send()
            push(PUSHED[-1]).start()

        for step in range(N_CHIPS):
            @pl.when(jnp.logical_and(p == step, i == 0))
            def _(step=step):
                landed(2 * step + 1).wait_recv()
                load_pair(step)

        rows = pl.ds(pl.multiple_of(i * tt, tt), tt)

        @pl.when(p == 0)
        def _():
            def norm_rows(r, carry):
                sub = pl.ds(pl.multiple_of(r * NORM_ROWS, NORM_ROWS), NORM_ROWS)
                xv = x_ref[sub, :]
                rstd = lax.rsqrt(jnp.mean(xv * xv, axis=-1, keepdims=True) + EPS)
                ub = (xv * rstd * g_ref[...]).astype(BF16)
                u_ref[sub, :] = ub
                u_all[pl.ds(pl.multiple_of(i * tt + r * NORM_ROWS, NORM_ROWS), NORM_ROWS), :] = ub
                return carry

            lax.fori_loop(0, tt // NORM_ROWS, norm_rows, 0, unroll=2)

        step = p * n_t + i
        slot = lax.rem(step, Z_SLOTS)

        class write_back:
            def __init__(self, buf, chip_col):
                half = tt // 2
                self.halves = [pltpu.make_async_copy(
                    z_buf.at[buf, pl.ds(half * h, half)],
                    z_hbm.at[pl.ds(pl.multiple_of(i * tt + half * h, half), half),
                             pl.ds(PAIR_COLS * chip_col, PAIR_COLS)],
                    z_sems.at[2 * buf + h]) for h in range(2)]

            def start(self):
                for h, cp in enumerate(self.halves):
                    cp.start(priority=h)

            def wait(self):
                for cp in self.halves:
                    cp.wait()

        @pl.when(step >= Z_SLOTS)
        def _():
            write_back(slot, 0).wait()

        z_buf[slot] = jnp.dot(u_all[rows, :], w_blk[...], preferred_element_type=F32)
        for q in range(N_CHIPS):
            pl.when(order_ref[p] == q)(write_back(slot, q).start)

        for step in range(1, N_CHIPS):
            @pl.when(jnp.logical_and(p == step - 1, i == n_t - 1))
            def _(step=step):
                landed(2 * step).wait_recv()
                forward(2 * step).start()

        @pl.when(jnp.logical_and(p == N_CHIPS - 1, i == n_t - 1))
        def _():
            push(PUSHED[0]).wait_send()
            push(PUSHED[-1]).wait_send()
            for k in FORWARDED:
                forward(k).wait_send()
            keep.wait()
            for buf in range(Z_SLOTS):
                write_back(buf, 0).wait()

    first_pass = lambda p, i, order_ref: (jnp.where(p == 0, i, n_t - 1), 0)
    grid_spec = pltpu.PrefetchScalarGridSpec(
        num_scalar_prefetch=1, grid=(N_CHIPS, n_t),
        in_specs=[pl.BlockSpec((tt, D), first_pass),
                  pl.BlockSpec((1, D), lambda p, i, order_ref: (0, 0)), ANY],
        out_specs=[ANY,
                   pl.BlockSpec((tt, D), first_pass), ANY],
        scratch_shapes=[pltpu.VMEM((t, D), BF16), pltpu.VMEM((D, PAIR_COLS), BF16),
                        pltpu.SemaphoreType.DMA((N_DEV,)), pltpu.SemaphoreType.DMA((N_DEV,)),
                        pltpu.SemaphoreType.DMA((1,)),
                        pltpu.VMEM((Z_SLOTS, tt, PAIR_COLS), F32),
                        pltpu.SemaphoreType.DMA((2 * Z_SLOTS,))])
    return pl.pallas_call(
        body, name="inproj_fwd", grid_spec=grid_spec,
        out_shape=[jax.ShapeDtypeStruct((t, N_COLS), F32), jax.ShapeDtypeStruct((t, D), BF16),
                   jax.ShapeDtypeStruct((N_DEV,) + w_shard.shape, BF16)],
        compiler_params=_params(48, dimension_semantics=("arbitrary", "arbitrary")),
    )(order, x, ln_g, w_shard)


def _shifted_copies(buf, shifted, rows):
    for b in range(1, SUBLANES):
        shifted[b, 0:rows, :] = buf[b:b + rows, :]


def _tap_slabs(buf, shifted, offset_of_tap):
    groups = {}
    for k in range(CONV_K):
        a, b = divmod(offset_of_tap(k), SUBLANES)
        groups.setdefault(b, []).append((SUBLANES * a, k))
    out = []
    for b, taps in sorted(groups.items()):
        taps.sort()
        lo = taps[0][0]
        out.append((buf if b == 0 else shifted.at[b], lo, [(k, off - lo) for off, k in taps]))
    return out


def _group_norm_stats(blk):
    mu = jnp.mean(blk, axis=-1, keepdims=True)
    cen = blk - mu
    var = jnp.mean(cen * cen, axis=-1, keepdims=True)
    return cen * lax.rsqrt(var + EPS)


def _conv_fwd(z, conv_w_all, conv_b, cn_g, cn_b, w_pw2, b_pw2):
    t = z.shape[0]
    tt = min(256, t)
    rc = 128

    def body(val_ref, glu_ref, gate_ref, cw_ref, cb_ref, g_ref, b_ref, w_hbm, b2_ref,
             yc_ref, y2_ref, yo_ref, w_vmem, vbuf, vsh, y1buf):
        @pl.when(pl.program_id(0) == 0)
        def _():
            pltpu.sync_copy(w_hbm, w_vmem)
            vbuf[0:CONV_PAD, :] = jnp.zeros((CONV_PAD, D), F32)

        vbuf[CONV_PAD:CONV_PAD + tt, :] = val_ref[...] * _sigmoid(glu_ref[...])
        _shifted_copies(vbuf, vsh, tt + 24)

        for g in range(N_HEADS):
            cs = slice(HEAD * g, HEAD * (g + 1))

            def row_chunk(r, carry, g=g, cs=cs):
                r0 = pl.multiple_of(r * rc, rc)
                acc = jnp.broadcast_to(cb_ref[:, cs], (rc, HEAD))
                for ref, lo, taps in _tap_slabs(vbuf, vsh, lambda k: k + 2):
                    slab = ref[pl.ds(r0 + lo, rc + taps[-1][1]), cs]
                    for k, off in taps:
                        acc = acc + cw_ref[g, k:k + 1, :] * slab[off:off + rc]
                yc_ref[pl.ds(r0, rc), cs] = acc
                n = _group_norm_stats(acc) * g_ref[:, cs] + b_ref[:, cs]
                y1buf[pl.ds(r0, rc), cs] = (n * _sigmoid(n)).astype(BF16)
                return carry

            lax.fori_loop(0, tt // rc, row_chunk, 0, unroll=True)
        vbuf[0:CONV_PAD, :] = vbuf[tt:tt + CONV_PAD, :]
        y2 = jnp.dot(y1buf[...], w_vmem[...], preferred_element_type=F32) + b2_ref[...]
        y2_ref[...] = y2
        gate = gate_ref[...]
        yo_ref[...] = (y2 * gate * _sigmoid(gate)).astype(BF16)

    col = lambda j: pl.BlockSpec((tt, D), lambda i: (i, j))
    row = pl.BlockSpec((tt, D), lambda i: (i, 0))
    return pl.pallas_call(
        body, name="conv_fwd", grid=(t // tt,),
        out_shape=[jax.ShapeDtypeStruct((t, D), F32), jax.ShapeDtypeStruct((t, D), F32),
                   jax.ShapeDtypeStruct((t, D), BF16)],
        in_specs=[col(0), col(1), col(2), _full((N_DEV, CONV_PAD, HEAD)), _full((1, D)),
                  _full((1, D)), _full((1, D)), ANY, _full((1, D))],
        out_specs=[row, row, row],
        scratch_shapes=[pltpu.VMEM((D, D), BF16), pltpu.VMEM((tt + CONV_PAD, D), F32),
                        pltpu.VMEM((SUBLANES, tt + CONV_PAD, D), F32), pltpu.VMEM((tt, D), BF16)],
        compiler_params=_params(48, dimension_semantics=("arbitrary",)),
    )(z, z, z, conv_w_all, conv_b, cn_g, cn_b, w_pw2, b_pw2)


HEAD_GROUP = 8
_HEAD_LANES = [slice(HEAD * j, HEAD * (j + 1)) for j in range(HEAD_GROUP)]


def _head_mean(a):
    return jnp.concatenate(
        [jnp.broadcast_to(jnp.mean(a[:, hs], axis=-1, keepdims=True), (a.shape[0], HEAD))
         for hs in _HEAD_LANES], axis=1)


def _chunk_quantities(zq, zf, lbh, tri):
    sig = _sigmoid(zf)
    sig_neg = _sigmoid(-zf)
    f = lbh + (1.0 - lbh) * sig
    k = (1.0 - lbh) * sig_neg
    q = zq * _sigmoid(zq)
    b = _tri_dot(tri, jnp.log(f))
    b_mid = b[CHUNK // 2 - 1:CHUNK // 2, :]
    b_last = b[CHUNK - 1:CHUNK, :]
    e_q = jnp.exp(b)
    e_qm = jnp.exp(b - b_mid)
    e_km = jnp.exp(b_mid - b)
    e_kd = jnp.exp(b_last - b)
    return q, k, f, sig, sig_neg, e_q, e_qm, e_km, e_kd, jnp.exp(b_last)


def _hgrn_fwd(z, lb_logits, onorm_g, shards):
    t = z.shape[0]
    tt = min(256, t)
    nc = tt // CHUNK
    modes = ["gather"] * len(shards)
    n = len(modes)

    def body(q_ref, f_ref, i_ref, g_ref, lbl_ref, on_ref, o_ref, y_ref, s_ref, st):
        @pl.when(pl.program_id(0) == 0)
        def _():
            st[...] = jnp.zeros_like(st)

        lb, _ = _lower_bound(lbl_ref[...])
        rows = lax.broadcasted_iota(jnp.int32, (CHUNK, CHUNK), 0)
        cols = lax.broadcasted_iota(jnp.int32, (CHUNK, CHUNK), 1)
        causal = rows >= cols
        tri = causal.astype(BF16)

        def chunk(c, carry):
            r0 = pl.multiple_of(c * CHUNK, CHUNK)
            rs = pl.ds(r0, CHUNK)
            for h0 in range(0, N_HEADS, HEAD_GROUP):
                cs = slice(HEAD * h0, HEAD * (h0 + HEAD_GROUP))
                q, k, _, _, _, e_q, e_qm, e_km, e_kd, e_last = _chunk_quantities(
                    q_ref[rs, cs], f_ref[rs, cs], lb[:, cs], tri)
                v = i_ref[rs, cs].astype(BF16)
                qm, km = (q * e_qm).astype(BF16), (k * e_km).astype(BF16)
                qt, kd = (q * e_q).astype(BF16), (k * e_kd).astype(BF16)
                outs = []
                for j, hs in enumerate(_HEAD_LANES):
                    s_old = st[h0 + j]
                    s_ref[c, h0 + j] = s_old.astype(BF16)
                    a = jnp.where(causal, _dot_nt(qm[:, hs], km[:, hs]), 0.0)
                    outs.append(_dot_nt(qt[:, hs], s_old) + _dot(a, v[:, hs]))
                    st[h0 + j] = s_old * e_last[:, hs] + _dot_tn(v[:, hs], kd[:, hs])
                o = jnp.concatenate(outs, axis=1)
                o_ref[rs, cs] = o
                n = o * lax.rsqrt(_head_mean(o * o) + EPS)
                zg = g_ref[rs, cs]
                y_ref[rs, cs] = (n * on_ref[:, cs] * zg * _sigmoid(zg)).astype(BF16)
            return carry

        lax.fori_loop(0, nc, chunk, 0, unroll=True)

    col = lambda j: pl.BlockSpec((tt, D), lambda i: (i, j))
    row = pl.BlockSpec((tt, D), lambda i: (i, 0))
    return pl.pallas_call(
        _hosted(body, 6, 3, (t // tt,), modes), name="hgrn_fwd", grid=(t // tt,),
        out_shape=[jax.ShapeDtypeStruct((t, D), F32), jax.ShapeDtypeStruct((t, D), BF16),
                   jax.ShapeDtypeStruct((t // CHUNK, N_HEADS, HEAD, HEAD), BF16)]
        + _recv_shapes(shards, modes),
        in_specs=[col(3), col(4), col(5), col(6), _full((2, D)), _full((1, D))] + [ANY] * n,
        out_specs=[row, row, pl.BlockSpec((nc, N_HEADS, HEAD, HEAD), lambda i: (i, 0, 0, 0))]
        + [ANY] * n,
        scratch_shapes=[pltpu.VMEM((N_HEADS, HEAD, HEAD), F32)] + _exchange_scratch(n),
        compiler_params=_params(40, dimension_semantics=("arbitrary",)),
    )(z, z, z, z, lb_logits, onorm_g, *shards)


def _rms_bwd(dn, xhat, rstd):
    return rstd * (dn - xhat * jnp.mean(dn * xhat, axis=-1, keepdims=True))


def _tail(x, y_conv, y_hgrn, p, target, w_out, w_pg, w_pp_all, pe_g, fin_g):
    t = x.shape[0]
    tt = min(256, t)
    n_steps = t // tt

    def body(x_ref, yc_ref, yh_ref, p_ref, tg_ref, wo_hbm, wg_hbm, wp_hbm, pg_ref, fg_ref,
             dh1_ref, dyc_ref, dyh_ref, dwo_hbm, dwg_hbm, dwp_hbm, dpg_ref, dfg_ref, loss_ref,
             wo, wg, wp, dwo, dwg, dwp):
        i = pl.program_id(0)

        @pl.when(i == 0)
        def _():
            pltpu.sync_copy(wo_hbm, wo)
            pltpu.sync_copy(wg_hbm, wg)
            for d in range(N_DEV):
                pltpu.sync_copy(wp_hbm.at[d], wp.at[:, pl.ds(HEAD * d, HEAD)])
            dwo[...] = jnp.zeros_like(dwo)
            dwg[...] = jnp.zeros_like(dwg)
            dwp[...] = jnp.zeros_like(dwp)
            dpg_ref[...] = jnp.zeros_like(dpg_ref)
            dfg_ref[...] = jnp.zeros_like(dfg_ref)
            loss_ref[...] = jnp.zeros_like(loss_ref)

        ycv, yhv = yc_ref[...], yh_ref[...]
        h1 = (x_ref[...] + jnp.dot(ycv, wo[0:D, :], preferred_element_type=F32)
              + jnp.dot(yhv, wo[D:2 * D, :], preferred_element_type=F32))
        pb = p_ref[...].astype(BF16)
        pe = jnp.dot(pb, wp[...], preferred_element_type=F32)
        rstd1 = lax.rsqrt(jnp.mean(h1 * h1, axis=-1, keepdims=True) + EPS)
        n1 = h1 * rstd1
        rb = (n1 * pg_ref[...]).astype(BF16)
        gate = _sigmoid(jnp.dot(rb, wg[...], preferred_element_type=F32))
        h2 = h1 + gate * pe
        rstd2 = lax.rsqrt(jnp.mean(h2 * h2, axis=-1, keepdims=True) + EPS)
        n2 = h2 * rstd2
        err = n2 * fg_ref[...] - tg_ref[...]
        loss_ref[...] += _rowsum8(err * err)

        d_out = err * (1.0 / D)
        dfg_ref[...] += _rowsum8(d_out * n2)
        d_h2 = _rms_bwd(d_out * fg_ref[...], n2, rstd2)
        d_pe = (d_h2 * gate).astype(BF16)
        d_gpre = (d_h2 * pe * gate * (1.0 - gate)).astype(BF16)
        dwg[...] += _dot_tn(rb, d_gpre)
        dwp[...] += _dot_tn(pb, d_pe)
        dr = _dot_nt(d_gpre, wg[...])
        dpg_ref[...] += _rowsum8(dr * n1)
        d_h1 = d_h2 + _rms_bwd(dr * pg_ref[...], n1, rstd1)
        dh1_ref[...] = d_h1
        d_h1b = d_h1.astype(BF16)
        dwo[0:D, :] += _dot_tn(ycv, d_h1b)
        dwo[D:2 * D, :] += _dot_tn(yhv, d_h1b)
        dyc_ref[...] = _dot_nt(d_h1b, wo[0:D, :])
        dyh_ref[...] = _dot_nt(d_h1b, wo[D:2 * D, :])

        @pl.when(i == n_steps - 1)
        def _():
            pltpu.sync_copy(dwo, dwo_hbm)
            pltpu.sync_copy(dwg, dwg_hbm)
            for d in range(N_DEV):
                pltpu.sync_copy(dwp.at[:, pl.ds(HEAD * d, HEAD)], dwp_hbm.at[d])

    row = pl.BlockSpec((tt, D), lambda i: (i, 0))
    acc = _full((SUBLANES, D))
    return pl.pallas_call(
        body, name="tail_fwd_bwd", grid=(n_steps,),
        out_shape=[jax.ShapeDtypeStruct((t, D), F32)] * 3
        + [jax.ShapeDtypeStruct((2 * D, D), F32), jax.ShapeDtypeStruct((D, D), F32),
           jax.ShapeDtypeStruct((N_DEV, PLE, HEAD), F32)]
        + [jax.ShapeDtypeStruct((SUBLANES, D), F32)] * 3,
        in_specs=[row, row, row, pl.BlockSpec((tt, PLE), lambda i: (i, 0)), row,
                  ANY, ANY, ANY, _full((1, D)), _full((1, D))],
        out_specs=[row, row, row, ANY, ANY, ANY, acc, acc, acc],
        scratch_shapes=[pltpu.VMEM((2 * D, D), BF16), pltpu.VMEM((D, D), BF16),
                        pltpu.VMEM((PLE, D), BF16), pltpu.VMEM((2 * D, D), F32),
                        pltpu.VMEM((D, D), F32), pltpu.VMEM((PLE, D), F32)],
        compiler_params=_params(52, dimension_semantics=("arbitrary",)),
    )(x, y_conv, y_hgrn, p, target, w_out, w_pg, w_pp_all, pe_g, fin_g)


def _hgrn_bwd(dy, z, o_raw, states, lb_logits, onorm_g, grads):
    t = z.shape[0]
    tt = min(256, t)
    nc = tt // CHUNK
    n_steps = t // tt
    modes = ["scatter"] * len(grads)

    def body(dy_ref, q_ref, f_ref, i_ref, g_ref, o_ref, s_ref, lbl_ref, on_ref,
             dz_ref, don_ref, dlb_ref, dst):
        @pl.when(pl.program_id(0) == 0)
        def _():
            dst[...] = jnp.zeros_like(dst)
            don_ref[...] = jnp.zeros_like(don_ref)
            dlb_ref[...] = jnp.zeros_like(dlb_ref)

        lb, _ = _lower_bound(lbl_ref[...])
        rows = lax.broadcasted_iota(jnp.int32, (CHUNK, CHUNK), 0)
        cols = lax.broadcasted_iota(jnp.int32, (CHUNK, CHUNK), 1)
        causal = rows >= cols
        tri = causal.astype(BF16)
        tri_rev = (rows <= cols).astype(BF16)
        width = HEAD * HEAD_GROUP
        is_last = lax.broadcasted_iota(jnp.int32, (CHUNK, width), 0) == CHUNK - 1
        nn = (((1,), (0,)), ((), ()))
        tn = (((0,), (0,)), ((), ()))
        dg = functools.partial(lax.dot_general, preferred_element_type=F32)

        def chunk(cc, carry):
            c = nc - 1 - cc
            r0 = pl.multiple_of(c * CHUNK, CHUNK)
            rs = pl.ds(r0, CHUNK)
            for h0 in range(0, N_HEADS, HEAD_GROUP):
                cs = slice(HEAD * h0, HEAD * h0 + width)
                zq, zf, zg = q_ref[rs, cs], f_ref[rs, cs], g_ref[rs, cs]
                lbh = lb[:, cs]
                q, k, f, sig, sig_neg, e_q, e_qm, e_km, e_kd, e_last = _chunk_quantities(
                    zq, zf, lbh, tri)
                vb = i_ref[rs, cs].astype(BF16)
                qt, qm, km, kd = q * e_q, q * e_qm, k * e_km, k * e_kd
                qt_b, kd_b = qt.astype(BF16), kd.astype(BF16)
                qm_h, qm_l = _split(qm)
                km_h, km_l = _split(km)

                o = o_ref[rs, cs]
                rstd = lax.rsqrt(_head_mean(o * o) + EPS)
                n = o * rstd
                sg = _sigmoid(zg)
                dyv = dy_ref[rs, cs]
                on = on_ref[:, cs]
                d_zg = dyv * n * on * sg * (1.0 + zg * (1.0 - sg))
                d_on = dyv * zg * sg
                don_ref[:, cs] += _rowsum8(d_on * n)
                dn = d_on * on
                do_b = (rstd * (dn - n * _head_mean(dn * n))).astype(BF16)

                dv, dkd, dqt, dqm, dkm, s_dots = [], [], [], [], [], []
                for j, hs in enumerate(_HEAD_LANES):
                    s_old, ds_new = s_ref[c, h0 + j], dst[h0 + j]
                    ds_b = ds_new.astype(BF16)
                    a = jnp.where(causal, _dot_nt(qm_h[:, hs], km_h[:, hs]), 0.0)
                    da = jnp.where(causal, _dot_nt(do_b[:, hs], vb[:, hs]), 0.0)
                    dv.append(_dot_tn(a, do_b[:, hs]) + _dot_nt(kd_b[:, hs], ds_b))
                    dkd.append(_dot(vb[:, hs], ds_b))
                    dqt.append(_dot(do_b[:, hs], s_old))
                    da_h, da_l = _split(da)
                    dqm.append(dg(da_h, km_h[:, hs], nn)
                               + (dg(da_h, km_l[:, hs], nn) + dg(da_l, km_h[:, hs], nn)))
                    dkm.append(dg(da_h, qm_h[:, hs], tn)
                               + (dg(da_h, qm_l[:, hs], tn) + dg(da_l, qm_h[:, hs], tn)))
                    dst[h0 + j] = ds_new * e_last[:, hs] + _dot_tn(do_b[:, hs], qt_b[:, hs])
                    s_dots.append(jnp.sum(s_old.astype(F32) * ds_new, axis=0, keepdims=True))
                dv, dkd, dqt, dqm, dkm, s_dots = [
                    jnp.concatenate(parts, axis=1) for parts in (dv, dkd, dqt, dqm, dkm, s_dots)]
                dq = dqt * e_q + dqm * e_qm
                dk = dkm * e_km + dkd * e_kd
                last = jnp.sum(dkd * kd, axis=0, keepdims=True) + e_last * s_dots
                db = q * dq - k * dk + jnp.where(is_last, last, 0.0)
                dlogf = _tri_dot(tri_rev, db)
                common = sig_neg * (dlogf / f - dk)
                dlb_ref[:, cs] += _rowsum8(common)
                c0 = 3 * D + HEAD * h0
                sq = _sigmoid(zq)
                dz_ref[rs, c0:c0 + width] = (dq * sq * (1.0 + zq * (1.0 - sq))).astype(BF16)
                dz_ref[rs, D + c0:D + c0 + width] = ((1.0 - lbh) * sig * common).astype(BF16)
                dz_ref[rs, 2 * D + c0:2 * D + c0 + width] = dv.astype(BF16)
                dz_ref[rs, 3 * D + c0:3 * D + c0 + width] = d_zg.astype(BF16)
            return carry

        lax.fori_loop(0, nc, chunk, 0, unroll=True)

    rev = lambda i: n_steps - 1 - i
    col = lambda j: pl.BlockSpec((tt, D), lambda i: (rev(i), j))
    row = pl.BlockSpec((tt, D), lambda i: (rev(i), 0))
    acc = _full((SUBLANES, D))
    n = len(modes)
    return pl.pallas_call(
        _hosted(body, 9, 3, (n_steps,), modes), name="hgrn_bwd", grid=(n_steps,),
        out_shape=[jax.ShapeDtypeStruct((t, N_COLS), BF16),
                   jax.ShapeDtypeStruct((SUBLANES, D), F32),
                   jax.ShapeDtypeStruct((SUBLANES, D), F32)] + _recv_shapes(grads, modes),
        in_specs=[row, col(3), col(4), col(5), col(6), row,
                  pl.BlockSpec((nc, N_HEADS, HEAD, HEAD), lambda i: (rev(i), 0, 0, 0)),
                  _full((2, D)), _full((1, D))] + [ANY] * n,
        out_specs=[pl.BlockSpec((tt, N_COLS), lambda i: (rev(i), 0)), acc, acc] + [ANY] * n,
        scratch_shapes=[pltpu.VMEM((N_HEADS, HEAD, HEAD), F32)] + _exchange_scratch(n),
        compiler_params=_params(48, dimension_semantics=("arbitrary",)),
    )(dy, z, z, z, z, o_raw, states, lb_logits, onorm_g, *grads)


def _conv_bwd(dy, z, yc, y2, conv_w_all, cn_g, cn_b, w_pw2, dz, grads, modes):
    t = z.shape[0]
    tt = min(256, t)
    rc = 32
    n_steps = t // tt

    def body(dy_ref, val_ref, glu_ref, gate_ref, yc_ref, y2_ref, cw_ref, g_ref, b_ref, w_hbm,
             dz_in, dz_ref, dw_hbm, dcw_out, db2_ref, dg_ref, dbeta_ref, dcb_ref,
             w_vmem, dw, dbuf, dsh, y1buf, dnbuf, dcw_ref):
        i = pl.program_id(0)

        @pl.when(i == 0)
        def _():
            pltpu.sync_copy(w_hbm, w_vmem)
            dw[...] = jnp.zeros_like(dw)
            dbuf[tt:tt + CONV_PAD, :] = jnp.zeros((CONV_PAD, D), F32)
            dcw_ref[...] = jnp.zeros_like(dcw_ref)
            dcw_out[...] = jnp.zeros_like(dcw_out)
            db2_ref[...] = jnp.zeros_like(db2_ref)
            dg_ref[...] = jnp.zeros_like(dg_ref)
            dbeta_ref[...] = jnp.zeros_like(dbeta_ref)
            dcb_ref[...] = jnp.zeros_like(dcb_ref)

        gate = gate_ref[...]
        sg = _sigmoid(gate)
        dyv = dy_ref[...]
        dy2 = dyv * gate * sg
        dz_ref[:, 2 * D:3 * D] = (dyv * y2_ref[...] * sg * (1.0 + gate * (1.0 - sg))).astype(BF16)
        db2_ref[...] += _rowsum8(dy2)
        dy2b = dy2.astype(BF16)
        dnbuf[...] = _dot_nt(dy2b, w_vmem[...])

        def norm_chunk(r, carry):
            r0 = pl.multiple_of(r * rc, rc)
            rs = pl.ds(r0, rc)
            for g in range(N_HEADS):
                cs = slice(HEAD * g, HEAD * (g + 1))
                blk = yc_ref[rs, cs]
                mu = jnp.mean(blk, axis=-1, keepdims=True)
                cen = blk - mu
                rstd = lax.rsqrt(jnp.mean(cen * cen, axis=-1, keepdims=True) + EPS)
                xhat = cen * rstd
                n = xhat * g_ref[:, cs] + b_ref[:, cs]
                sn = _sigmoid(n)
                y1buf[rs, cs] = (n * sn).astype(BF16)
                dn = dnbuf[rs, cs] * sn * (1.0 + n * (1.0 - sn))
                dg_ref[:, cs] += _rowsum8(dn * xhat)
                dbeta_ref[:, cs] += _rowsum8(dn)
                dxh = dn * g_ref[:, cs]
                dyc = rstd * (dxh - jnp.mean(dxh, axis=-1, keepdims=True)
                              - xhat * jnp.mean(dxh * xhat, axis=-1, keepdims=True))
                dcb_ref[:, cs] += _rowsum8(dyc)
                dbuf[rs, cs] = dyc
            return carry

        lax.fori_loop(0, tt // rc, norm_chunk, 0, unroll=True)
        dw[...] += _dot_tn(y1buf[...], dy2b)
        _shifted_copies(dbuf, dsh, tt + 24)

        def conv_chunk(r, carry):
            r0 = pl.multiple_of(r * rc, rc)
            rs = pl.ds(r0, rc)
            for g in range(N_HEADS):
                cs = slice(HEAD * g, HEAD * (g + 1))
                sglu = _sigmoid(glu_ref[rs, cs])
                val = val_ref[rs, cs]
                v = val * sglu
                dv = jnp.zeros((rc, HEAD), F32)
                for ref, lo, taps in _tap_slabs(dbuf, dsh, lambda k: CONV_K - 1 - k):
                    slab = ref[pl.ds(r0 + lo, rc + taps[-1][1]), cs]
                    for k, off in taps:
                        d_later = slab[off:off + rc]
                        dv = dv + cw_ref[g, k:k + 1, :] * d_later
                        dcw_ref[g, k] += _rowsum8(v * d_later)
                dz_ref[rs, cs] = (dv * sglu).astype(BF16)
                dz_ref[rs, D + HEAD * g:D + HEAD * (g + 1)] = (
                    dv * val * sglu * (1.0 - sglu)).astype(BF16)
            return carry

        lax.fori_loop(0, tt // rc, conv_chunk, 0, unroll=2)
        dbuf[tt:tt + CONV_PAD, :] = dbuf[0:CONV_PAD, :]

        @pl.when(i == n_steps - 1)
        def _():
            pltpu.sync_copy(dw, dw_hbm)
            for g in range(N_HEADS):
                for k in range(CONV_K):
                    dcw_out[g, k:k + 1, :] = jnp.sum(dcw_ref[g, k], axis=0, keepdims=True)

    rev = lambda i: n_steps - 1 - i
    col = lambda j: pl.BlockSpec((tt, D), lambda i: (rev(i), j))
    row = pl.BlockSpec((tt, D), lambda i: (rev(i), 0))
    acc = _full((SUBLANES, D))
    n = len(modes)
    return pl.pallas_call(
        _hosted(body, 11, 7, (n_steps,), modes), name="conv_bwd", grid=(n_steps,),
        out_shape=[jax.ShapeDtypeStruct((t, N_COLS), BF16), jax.ShapeDtypeStruct((D, D), F32),
                   jax.ShapeDtypeStruct((N_DEV, CONV_PAD, HEAD), F32)]
        + [jax.ShapeDtypeStruct((SUBLANES, D), F32)] * 4 + _recv_shapes(grads, modes),
        in_specs=[row, col(0), col(1), col(2), row, row, _full((N_DEV, CONV_PAD, HEAD)),
                  _full((1, D)), _full((1, D)), ANY, ANY] + [ANY] * n,
        out_specs=[pl.BlockSpec((tt, 3 * D), lambda i: (rev(i), 0)), ANY,
                   _full((N_DEV, CONV_PAD, HEAD)), acc, acc, acc, acc] + [ANY] * n,
        input_output_aliases={10: 0},
        scratch_shapes=[pltpu.VMEM((D, D), BF16), pltpu.VMEM((D, D), F32),
                        pltpu.VMEM((tt + CONV_PAD, D), F32),
                        pltpu.VMEM((SUBLANES, tt + CONV_PAD, D), F32),
                        pltpu.VMEM((tt, D), BF16), pltpu.VMEM((tt, D), F32),
                        pltpu.VMEM((N_DEV, CONV_PAD, SUBLANES, HEAD), F32)] + _exchange_scratch(n),
        compiler_params=_params(52, dimension_semantics=("arbitrary",)),
    )(dy, z, z, z, yc, y2, conv_w_all, cn_g, cn_b, w_pw2, dz, *grads)


def _inproj_bwd_dx(dz, x, d_h1, ln_g, w_in_all, grads, modes):
    t = x.shape[0]
    tt = min(256, t)

    def body(dz_ref, x_ref, dh1_ref, g_ref, w_hbm, dx_ref, dg_ref, w_vmem):
        @pl.when(pl.program_id(0) == 0)
        def _():
            for d in range(N_DEV):
                pltpu.sync_copy(w_hbm.at[d], w_vmem.at[
                    d // 2, :, pl.ds(COLS_PER_DEV * (d % 2), COLS_PER_DEV)])
            dg_ref[...] = jnp.zeros_like(dg_ref)

        du = jnp.zeros((tt, D), F32)
        for q in range(N_CHIPS):
            du = du + lax.dot_general(
                dz_ref[:, PAIR_COLS * q:PAIR_COLS * (q + 1)], w_vmem[q],
                (((1,), (1,)), ((), ())), preferred_element_type=F32)
        xv = x_ref[...]
        rstd = lax.rsqrt(jnp.mean(xv * xv, axis=-1, keepdims=True) + EPS)
        xhat = xv * rstd
        dg_ref[...] += _rowsum8(du * xhat)
        dx_ref[...] = dh1_ref[...] + _rms_bwd(du * g_ref[...], xhat, rstd)

    row = pl.BlockSpec((tt, D), lambda i: (i, 0))
    n = len(modes)
    return pl.pallas_call(
        _hosted(body, 5, 2, (t // tt,), modes), name="inproj_bwd_dx", grid=(t // tt,),
        out_shape=[jax.ShapeDtypeStruct((t, D), F32), jax.ShapeDtypeStruct((SUBLANES, D), F32)]
        + _recv_shapes(grads, modes),
        in_specs=[pl.BlockSpec((tt, N_COLS), lambda i: (i, 0)), row, row, _full((1, D)), ANY]
        + [ANY] * n,
        out_specs=[row, _full((SUBLANES, D))] + [ANY] * n,
        scratch_shapes=[pltpu.VMEM((N_CHIPS, D, PAIR_COLS), BF16)] + _exchange_scratch(n),
        compiler_params=_params(48, dimension_semantics=("arbitrary",)),
    )(dz, x, d_h1, ln_g, w_in_all, *grads)


def _inproj_bwd_dw(name, u, dz, first, count, grads=(), modes=()):
    t = u.shape[0]
    tt = min(2048, t)
    grid = (count // 2, t // tt)
    n = len(modes)

    def body(u_ref, dz_ref, dw_ref):
        @pl.when(pl.program_id(1) == 0)
        def _():
            dw_ref[...] = jnp.zeros_like(dw_ref)

        both = lax.dot_general(u_ref[...], dz_ref[...], (((0,), (0,)), ((), ())),
                               preferred_element_type=F32)
        dw_ref[0] += both[:, :COLS_PER_DEV]
        dw_ref[1] += both[:, COLS_PER_DEV:]

    return pl.pallas_call(
        _hosted(body, 2, 1, grid, modes) if n else body, name=name, grid=grid,
        out_shape=[jax.ShapeDtypeStruct((count, D, COLS_PER_DEV), F32)]
        + _recv_shapes(grads, modes),
        in_specs=[pl.BlockSpec((tt, D), lambda j, i: (i, 0)),
                  pl.BlockSpec((tt, PAIR_COLS), lambda j, i: (i, first // 2 + j))] + [ANY] * n,
        out_specs=[pl.BlockSpec((2, D, COLS_PER_DEV), lambda j, i: (j, 0, 0))] + [ANY] * n,
        scratch_shapes=_exchange_scratch(n) if n else [],
        compiler_params=_params(56, dimension_semantics=("arbitrary", "arbitrary")),
    )(u, dz, *grads)


def _adamw(w, g, m, v):
    m = ADAM_B1 * m + (1.0 - ADAM_B1) * g
    v = ADAM_B2 * v + (1.0 - ADAM_B2) * (g * g)
    m_hat = m / (1.0 - ADAM_B1 ** ADAM_STEP)
    v_hat = v / (1.0 - ADAM_B2 ** ADAM_STEP)
    delta = -ADAM_LR * (m_hat / (jnp.sqrt(v_hat) + ADAM_EPS) + ADAM_WD * w)
    return delta, m, v


def _pack_small(partials):
    rows = sorted(partials)

    def body(*refs):
        ins, out_ref = refs[:-1], refs[-1]
        out_ref[...] = jnp.zeros_like(out_ref)
        for j, row in enumerate(rows):
            out_ref[row:row + 1, :] = jnp.sum(ins[j][...], axis=0, keepdims=True)

    return pl.pallas_call(
        body, name="pack_small", out_shape=jax.ShapeDtypeStruct((N_SMALL, D), F32),
    )(*[partials[row] for row in rows])


def _sum_adam(name, recvs, w, m, v, rows):
    r, c = w.shape
    n = len(recvs)
    side = lax.axis_index("x").astype(jnp.int32).reshape(1)

    def body(side_ref, *refs):
        w_ref, m_ref, v_ref, g_ref, d_ref, mo_ref, vo_ref = refs[n:]

        def finish(recv_ref):
            g = recv_ref[0]
            for s in range(1, recv_ref.shape[0]):
                g = g + recv_ref[s]
            g_ref[...] = g
            d_ref[...], mo_ref[...], vo_ref[...] = _adamw(w_ref[...], g, m_ref[...], v_ref[...])

        if n == 1:
            finish(refs[0])
        else:
            for s in range(n):
                pl.when(side_ref[0] == s)(functools.partial(finish, refs[s]))

    def recv_spec(s, rv):
        if n == 1:
            return pl.BlockSpec((rv.shape[0], rows, c), lambda i, side_ref: (0, i, 0))
        return pl.BlockSpec((rv.shape[0], rows, c),
                            lambda i, side_ref: (0, jnp.where(side_ref[0] == s, i, 0), 0))

    blk = pl.BlockSpec((rows, c), lambda i, side_ref: (i, 0))
    grid_spec = pltpu.PrefetchScalarGridSpec(
        num_scalar_prefetch=1, grid=(r // rows,),
        in_specs=[recv_spec(s, rv) for s, rv in enumerate(recvs)] + [blk, blk, blk],
        out_specs=[blk] * 4)
    return pl.pallas_call(
        body, name=name, grid_spec=grid_spec,
        out_shape=[jax.ShapeDtypeStruct((r, c), F32)] * 4,
        compiler_params=_params(48, dimension_semantics=("arbitrary",)),
    )(side, *recvs, w, m, v)


def _small_adam(gathered, lb_logits, w, m, v):
    def body(ga_ref, lbl_ref, w_ref, m_ref, v_ref, g_ref, d_ref, mo_ref, vo_ref, loss_ref):
        g = ga_ref[0]
        for s in range(1, N_DEV):
            g = g + ga_ref[s]
        s0, s1 = _lower_bound(lbl_ref[...])
        d_lb = g[R_LB0:R_LB0 + 1, :]
        rows = lax.broadcasted_iota(jnp.int32, (N_SMALL, D), 0)
        g = jnp.where(rows == R_LB0, d_lb * s0 * (1.0 - s0), g)
        g = jnp.where(rows == R_LB1, -d_lb * s0 * s1, g)
        g_ref[...] = g
        d_ref[...], mo_ref[...], vo_ref[...] = _adamw(w_ref[...], g, m_ref[...], v_ref[...])
        loss_ref[...] = (0.5 / D) * jnp.sum(g[R_LOSS:R_LOSS + 1, :], axis=-1, keepdims=True)

    return pl.pallas_call(
        body, name="small_adam",
        out_shape=[jax.ShapeDtypeStruct((N_SMALL, D), F32)] * 4 + [jax.ShapeDtypeStruct((1, 1), F32)],
    )(gathered, lb_logits, w, m, v)


def _pad_rows(a, rows):
    return jnp.pad(a, ((0, rows - a.shape[0]), (0, 0)))


def _pack_rows(rows):
    rows = [r.reshape(-1, D) for r in rows]
    packed = jnp.concatenate(rows, axis=0)
    return _pad_rows(packed, N_SMALL)


def kernel(x, p, ln_g, w_in, conv_w, conv_b, cnorm_g, cnorm_b, w_pw2, b_pw2, lb_logits, onorm_g, w_out, pe_norm_g, w_pg, w_pp, final_g, loss_target, m_ln_g, m_w_in, m_conv_w, m_conv_b, m_cnorm_g, m_cnorm_b, m_w_pw2, m_b_pw2, m_lb_logits, m_onorm_g, m_w_out, m_pe_norm_g, m_w_pg, m_w_pp, m_final_g, v_ln_g, v_w_in, v_conv_w, v_conv_b, v_cnorm_g, v_cnorm_b, v_w_pw2, v_b_pw2, v_lb_logits, v_onorm_g, v_w_out, v_pe_norm_g, v_w_pg, v_w_pp, v_final_g):
    t = x.shape[1]
    x2 = x.reshape(t, D)
    p2 = p.reshape(t, PLE)
    tg2 = loss_target.reshape(t, D)
    fin_g = final_g.reshape(1, D)

    z, u, w_in_all = _inproj_fwd(x2, ln_g, w_in[0].astype(BF16))
    (o_raw, y_hgrn, states, conv_w_all, w_pw2_all, w_out_all, w_pg_all, w_pp_all) = _hgrn_fwd(
        z, lb_logits, onorm_g,
        [_pad_rows(conv_w[0], CONV_PAD), w_pw2[0].astype(BF16), w_out[0].astype(BF16),
         w_pg[0].astype(BF16), w_pp[0].astype(BF16)])
    w_pw2_full = w_pw2_all.reshape(D, D)
    w_out_full = w_out_all.reshape(2 * D, D)
    w_pg_full = w_pg_all.reshape(D, D)
    yc, y2, y_conv = _conv_fwd(z, conv_w_all, conv_b, cnorm_g, cnorm_b, w_pw2_full, b_pw2)

    (d_h1, dy_conv, dy_hgrn, d_w_out, d_w_pg, d_w_pp, d_pen_p, d_fin_p, loss_p) = _tail(
        x2, y_conv, y_hgrn, p2, tg2, w_out_full, w_pg_full, w_pp_all, pe_norm_g, fin_g)

    dz, d_on_p, d_lb_p, r_w_out, r_w_pg, r_w_pp = _hgrn_bwd(
        dy_hgrn, z, o_raw, states, lb_logits, onorm_g,
        [d_w_out.reshape(N_DEV, 2 * D // N_DEV, D), d_w_pg.reshape(N_DEV, D // N_DEV, D), d_w_pp])
    (d_w_in_hi,) = _inproj_bwd_dw("inproj_bwd_dw_hi", u, dz, N_DEV // 2, N_DEV // 2)
    dz, d_w_pw2, d_conv_w, d_b2_p, d_cng_p, d_cnb_p, d_cb_p, r_w_in_hi = _conv_bwd(
        dy_conv, z, yc, y2, conv_w_all, cnorm_g, cnorm_b, w_pw2_full, dz, [d_w_in_hi], [1])
    d_w_in_lo, r_w_pw2, r_conv_w = _inproj_bwd_dw(
        "inproj_bwd_dw_lo", u, dz, 0, N_DEV // 2,
        [d_w_pw2.reshape(N_DEV, D // N_DEV, D), d_conv_w], ["scatter", "scatter"])
    chip_lo = _pair_reduce("pair_reduce_lo", d_w_in_lo)
    grad_x, d_ln_p, r_w_in_lo = _inproj_bwd_dx(
        dz, x2, d_h1, ln_g, w_in_all, [chip_lo], [("chip", 0)])

    small = _pack_small({R_LN: d_ln_p, R_CONVB: d_cb_p, R_CNG: d_cng_p, R_CNB: d_cnb_p,
                         R_BPW2: d_b2_p, R_LB0: d_lb_p, R_ON: d_on_p, R_PEN: d_pen_p,
                         R_FIN: d_fin_p, R_LOSS: loss_p})
    (small_all,) = _exchange_call("gather_small", [small], ["gather"])

    big = {}
    big["w_in"] = _sum_adam("adam_w_in", [r_w_in_lo, r_w_in_hi], w_in[0], m_w_in[0], v_w_in[0], 128)
    cw = _sum_adam("adam_conv_w", [r_conv_w], _pad_rows(conv_w[0], CONV_PAD),
                   _pad_rows(m_conv_w[0], CONV_PAD), _pad_rows(v_conv_w[0], CONV_PAD), CONV_PAD)
    big["conv_w"] = [a[:CONV_K] for a in cw]
    big["w_pw2"] = _sum_adam("adam_w_pw2", [r_w_pw2], w_pw2[0], m_w_pw2[0], v_w_pw2[0], 128)
    big["w_out"] = _sum_adam("adam_w_out", [r_w_out], w_out[0], m_w_out[0], v_w_out[0], 128)
    big["w_pg"] = _sum_adam("adam_w_pg", [r_w_pg], w_pg[0], m_w_pg[0], v_w_pg[0], 128)
    big["w_pp"] = _sum_adam("adam_w_pp", [r_w_pp], w_pp[0], m_w_pp[0], v_w_pp[0], PLE)

    small_w = [ln_g, conv_b, cnorm_g, cnorm_b, b_pw2, lb_logits, onorm_g, pe_norm_g, final_g]
    small_m = [m_ln_g, m_conv_b, m_cnorm_g, m_cnorm_b, m_b_pw2, m_lb_logits, m_onorm_g,
               m_pe_norm_g, m_final_g]
    small_v = [v_ln_g, v_conv_b, v_cnorm_g, v_cnorm_b, v_b_pw2, v_lb_logits, v_onorm_g,
               v_pe_norm_g, v_final_g]
    sg, sd, sm, sv, loss = _small_adam(small_all, lb_logits, _pack_rows(small_w),
                                       _pack_rows(small_m), _pack_rows(small_v))

    small_rows = {"ln_g": (R_LN, 1), "conv_b": (R_CONVB, 1), "cnorm_g": (R_CNG, 1),
                  "cnorm_b": (R_CNB, 1), "b_pw2": (R_BPW2, 1), "lb_logits": (R_LB0, 2),
                  "onorm_g": (R_ON, 1), "pe_norm_g": (R_PEN, 1), "final_g": (R_FIN, 1)}
    order = ["ln_g", "w_in", "conv_w", "conv_b", "cnorm_g", "cnorm_b", "w_pw2", "b_pw2",
             "lb_logits", "onorm_g", "w_out", "pe_norm_g", "w_pg", "w_pp", "final_g"]

    def leaf(kind, name):
        if name in big:
            return big[name][kind][None]
        r0, n = small_rows[name]
        a = (sg, sd, sm, sv)[kind][r0:r0 + n]
        return a.reshape(D) if name == "final_g" else a

    outs = [loss.reshape(()), grad_x.reshape(1, t, D)]
    for kind in range(4):
        outs += [leaf(kind, name) for name in order]
    return tuple(outs)
```
